```python
import jax, jax.numpy as jnp
from jax import lax
import numpy as np

D_MODEL = 1024
BATCH = 32
SEQ = 256
DEPTH = 2
DEC_BATCH = 2
DEC_SEQ = 1024
PAST_LEN = 512

GRID_W = 64
MIX_WIDTH = D_MODEL
NA_HEADS = 8
NA_HEAD_DIM = 64
NA_WIDTH = NA_HEADS * NA_HEAD_DIM
NA_KH = 8
NA_KW = 16
NA_QBW = 16
NA_KSPAN = 2 * NA_QBW
HG_HEADS = 4
HG_DK = 64
HG_DV = 64
HG_WIDTH = HG_HEADS * HG_DV
HG_CHUNK = 32
F_FLOOR = 1e-30
GM_GROUPS = 4
GM_GDIM = 64
GM_WIDTH = GM_GROUPS * GM_GDIM
GM_CHUNK = 128
IN_COLS = 3 * NA_WIDTH + 5 * HG_WIDTH + 2 * GM_WIDTH
N_EXPERTS = 32
TOP_K = 4
D_EXPERT = D_MODEL
MOE_BLOCK = 128
SWIGLU_LIMIT = 7.0
SWIGLU_ALPHA = 1.702
RMS_EPS = 1e-6
NEG_INF = -1e30

kernel_name = 'hybrid_natten_hgrn2_gmlp_moe_dit_step'


def rmsnorm(x, g):
    xf = x.astype(jnp.float32)
    y = xf * lax.rsqrt(jnp.mean(xf * xf, axis=-1, keepdims=True) + RMS_EPS)
    return (y * g.astype(jnp.float32)).astype(x.dtype)


def ada_modulation(cond, w, b):
    m = (cond @ w + b)[:, None, :]
    return jnp.split(m, 6, axis=-1)


def modulate(h, shift, scale):
    return h * (1.0 + scale) + shift


def dense_attention(q, k, v):
    B, N, H, Dh = q.shape
    s = jnp.einsum('bqhd,bkhd->bhqk', q, k).astype(jnp.float32) * (Dh ** -0.5)
    p = jax.nn.softmax(s, axis=-1).astype(v.dtype)
    return jnp.einsum('bhqk,bkhd->bqhd', p, v).reshape(B, N, H * Dh)


def neighbourhood_attention(q, k, v, ck, cv, rel_bias):
    B, N, H, Dh = q.shape
    rows = N // GRID_W
    kh = min(NA_KH, rows)
    nb = GRID_W // NA_QBW
    r = np.arange(rows)
    row_idx = np.clip(r - kh // 2, 0, rows - kh)[:, None] + np.arange(kh)[None, :]
    dr_idx = row_idx - r[:, None] + NA_KH - 1
    qcol = (np.arange(nb) * NA_QBW)[:, None] + np.arange(NA_QBW)[None, :]
    col_idx = (np.clip(np.arange(nb) * NA_QBW - NA_KW // 2, 0, GRID_W - NA_KSPAN)[:, None]
               + np.arange(NA_KSPAN)[None, :])
    q_start = np.clip(qcol - NA_KW // 2, 0, GRID_W - NA_KW)
    kc = col_idx[:, None, :]
    in_win = (kc >= q_start[:, :, None]) & (kc < q_start[:, :, None] + NA_KW)
    dc_idx = np.clip(kc - qcol[:, :, None] + NA_KW - 1, 0, 2 * NA_KW - 2)
    bias = rel_bias[:, dr_idx[:, None, None, :, None], dc_idx[None, :, :, None, :]].astype(jnp.float32)
    ri = row_idx[:, None, :, None]
    ci = col_idx[None, :, None, :]
    kg = k.reshape(B, rows, GRID_W, H, Dh)[:, ri, ci]
    vg = v.reshape(B, rows, GRID_W, H, Dh)[:, ri, ci]
    qb = q.reshape(B, rows, nb, NA_QBW, H, Dh)
    scale = Dh ** -0.5
    s_loc = jnp.einsum('brnqhd,brnkjhd->bhrnqkj', qb, kg).astype(jnp.float32) * scale + bias[None]
    s_loc = jnp.where(in_win[:, :, None, :][None], s_loc, NEG_INF)
    nloc = kh * NA_KSPAN
    s_loc = s_loc.reshape(B, H, rows, nb, NA_QBW, nloc)
    s_ctx = jnp.einsum('brnqhd,blhd->bhrnql', qb, ck).astype(jnp.float32) * scale
    p = jax.nn.softmax(jnp.concatenate([s_loc, s_ctx], axis=-1), axis=-1)
    p_loc = p[..., :nloc].reshape(B, H, rows, nb, NA_QBW, kh, NA_KSPAN).astype(v.dtype)
    p_ctx = p[..., nloc:].astype(v.dtype)
    o = (jnp.einsum('bhrnqkj,brnkjhd->brnqhd', p_loc, vg)
         + jnp.einsum('bhrnql,blhd->brnqhd', p_ctx, cv))
    return o.reshape(B, N, H * Dh)


def chunk_scan(q, k, v, logf, s0):
    B, N, H, DK = q.shape
    DV = v.shape[-1]
    C = HG_CHUNK
    n = N // C
    q, k, v, logf = (t.reshape(B, n, C, H, t.shape[-1]) for t in (q, k, v, logf))
    b = jnp.cumsum(logf, axis=2)
    causal = np.tril(np.ones((C, C), dtype=bool))[None, None, :, :, None, None]
    diff = b[:, :, :, None] - b[:, :, None, :]
    decay = jnp.where(causal, jnp.exp(jnp.where(causal, diff, 0.0)), 0.0)
    a = jnp.einsum('bnthk,bnshk,bntshk->bnhts', q, k, decay)
    o_intra = jnp.einsum('bnhts,bnshv->bnthv', a, v)
    b_last = b[:, :, -1]
    q_in = q * jnp.exp(b)
    k_st = k * jnp.exp(b_last[:, :, None] - b)
    chunk_decay = jnp.exp(b_last)

    def step(S, xs):
        qc, kc, vc, dc = xs
        o = jnp.einsum('bthk,bhkv->bthv', qc, S)
        S = S * dc[..., None] + jnp.einsum('bthk,bthv->bhkv', kc, vc)
        return S, o

    xs = tuple(jnp.moveaxis(t, 1, 0) for t in (q_in, k_st, v, chunk_decay))
    S, o_inter = lax.scan(step, s0, xs)
    o = o_intra + jnp.moveaxis(o_inter, 0, 1)
    return o.reshape(B, N, H, DV), S


def hgrn2_mixer(q, zf, zb, i, g, lower_l, onorm_g, s0f, s0b):
    B, N, _ = q.shape
    f32 = jnp.float32

    def heads(t):
        return t.astype(f32).reshape(B, N, HG_HEADS, HG_DK)

    def gates(z, lb):
        z = heads(z)
        lb = lb.reshape(HG_HEADS, HG_DK)
        f = lb + (1.0 - lb) * jax.nn.sigmoid(z)
        logf = jnp.log(jnp.maximum(f, F_FLOOR))
        key = (1.0 - lb) * jax.nn.sigmoid(-z)
        return logf, key

    qh = heads(q)
    vh = i.astype(f32).reshape(B, N, HG_HEADS, HG_DV)
    logf_f, k_f = gates(zf, lower_l[0])
    logf_b, k_b = gates(zb, lower_l[1])
    o_f, sf = chunk_scan(qh, k_f, vh, logf_f, s0f.astype(f32))
    o_b, sb = chunk_scan(qh[:, ::-1], k_b[:, ::-1], vh[:, ::-1], logf_b[:, ::-1], s0b.astype(f32))
    o = o_f + o_b[:, ::-1]
    o = o * lax.rsqrt(jnp.mean(o * o, axis=-1, keepdims=True) + RMS_EPS) * onorm_g.astype(f32)
    o = o.reshape(B, N, HG_WIDTH) * jax.nn.silu(g.astype(f32))
    return o.astype(q.dtype), sf, sb


def chunk_mlp(u, v, vnorm_g, ws, bs):
    B, N, _ = u.shape
    nc = N // GM_CHUNK
    vg = rmsnorm(v, vnorm_g).reshape(B, nc, GM_CHUNK, GM_GROUPS, GM_GDIM)
    z = jnp.einsum('gts,bnsgc->bntgc', ws, vg) + bs.T[:, :, None]
    return u * z.reshape(B, N, GM_WIDTH)


def token_mixer(h, w_in_l, w_out_l, rel_bias_l, lower_l, onorm_g_l, vnorm_g_l, ws_l, bs_l, ctx):
    B, N, _ = h.shape
    sizes = [NA_WIDTH] * 3 + [HG_WIDTH] * 5 + [GM_WIDTH] * 2
    splits = np.cumsum(sizes)[:-1].tolist()
    aq, ak, av, hq, hzf, hzb, hi, hgate, gu, gv = jnp.split(h @ w_in_l, splits, axis=-1)
    aq, ak, av = (t.reshape(B, N, NA_HEADS, NA_HEAD_DIM) for t in (aq, ak, av))
    if ctx is None:
        att = dense_attention(aq, ak, av)
        s0f = jnp.zeros((B, HG_HEADS, HG_DK, HG_DV), jnp.float32)
        s0b = s0f
    else:
        ck, cv, s0f, s0b = ctx
        att = neighbourhood_attention(aq, ak, av, ck, cv, rel_bias_l)
    rec, sf, sb = hgrn2_mixer(hq, hzf, hzb, hi, hgate, lower_l, onorm_g_l, s0f, s0b)
    mlp = chunk_mlp(gu, gv, vnorm_g_l, ws_l, bs_l)
    out = jnp.concatenate([att, rec, mlp], axis=-1) @ w_out_l
    return out, (ak, av, sf, sb)


def swiglu_expert(xb, wg, bg, wu, bu, wd, bd):
    gate = jnp.minimum(xb @ wg + bg, SWIGLU_LIMIT)
    up = jnp.clip(xb @ wu + bu, -SWIGLU_LIMIT, SWIGLU_LIMIT)
    glu = gate * jax.nn.sigmoid(SWIGLU_ALPHA * gate)
    return ((up + 1.0) * glu) @ wd + bd


def routed_moe(h, wr, br, wg, bg, wu, bu, wd, bd):
    B, N, D = h.shape
    T = B * N
    A = T * TOP_K
    hf = h.reshape(T, D)
    logits = (hf @ wr + br).astype(jnp.float32)
    top_v, top_i = lax.top_k(logits, TOP_K)
    gates = jax.nn.softmax(top_v, axis=-1)
    flat_e = top_i.reshape(A)
    flat_t = jnp.repeat(jnp.arange(T, dtype=jnp.int32), TOP_K)
    flat_w = gates.reshape(A)
    order = jnp.argsort(flat_e)
    se, st, sw = flat_e[order], flat_t[order], flat_w[order]
    counts = jnp.bincount(flat_e, length=N_EXPERTS)
    padded = (counts + MOE_BLOCK - 1) // MOE_BLOCK * MOE_BLOCK
    pad_end = jnp.cumsum(padded)
    pad_start = pad_end - padded
    start = jnp.cumsum(counts) - counts
    dest = pad_start[se] + jnp.arange(A, dtype=jnp.int32) - start[se]
    n_blocks = (A + N_EXPERTS * (MOE_BLOCK - 1) + MOE_BLOCK - 1) // MOE_BLOCK
    n_slots = n_blocks * MOE_BLOCK
    slot_tok = jnp.zeros((n_slots,), jnp.int32).at[dest].set(st)
    slot_w = jnp.zeros((n_slots,), jnp.float32).at[dest].set(sw)
    blk_e = jnp.minimum(jnp.searchsorted(pad_end, jnp.arange(n_blocks) * MOE_BLOCK, side='right'),
                        N_EXPERTS - 1)

    def run_block(args):
        tok, w, e = args
        y = swiglu_expert(hf[tok], wg[e], bg[e], wu[e], bu[e], wd[e], bd[e])
        return y.astype(jnp.float32) * w[:, None]

    ys = lax.map(run_block, (slot_tok.reshape(n_blocks, MOE_BLOCK),
                             slot_w.reshape(n_blocks, MOE_BLOCK), blk_e))
    out = jnp.zeros((T, D), jnp.float32).at[slot_tok].add(ys.reshape(n_slots, D))
    return out.astype(h.dtype).reshape(B, N, D)


def setup_inputs(seed: int = 0) -> dict:
    key = jax.random.key(seed)
    k = jax.random.split(key, 30)
    D = D_MODEL

    def nrm(kk, shape, scale):
        return jax.random.normal(kk, shape, jnp.float32) * scale

    return {
        'x_prompt': nrm(k[0], (BATCH, SEQ, D), 1.0),
        'x_sample': nrm(k[1], (DEC_BATCH, DEC_SEQ, D), 1.0),
        'cache_k': nrm(k[2], (DEC_BATCH, DEPTH, PAST_LEN, NA_HEADS, NA_HEAD_DIM), 1.0),
        'cache_v': nrm(k[3], (DEC_BATCH, DEPTH, PAST_LEN, NA_HEADS, NA_HEAD_DIM), 1.0),
        'state_hgrn_fwd': nrm(k[4], (DEC_BATCH, DEPTH, HG_HEADS, HG_DK, HG_DV), 0.5),
        'state_hgrn_bwd': nrm(k[5], (DEC_BATCH, DEPTH, HG_HEADS, HG_DK, HG_DV), 0.5),
        'c': nrm(k[6], (DEC_BATCH, D), 1.0),
        'c_ctx': nrm(k[7], (D,), 1.0),
        'w_mod': nrm(k[8], (DEPTH, D, 6 * D), D ** -0.5),
        'b_mod': nrm(k[9], (DEPTH, 6 * D), 0.02),
        'norm1_g': 1.0 + nrm(k[10], (DEPTH, D), 0.02),
        'norm2_g': 1.0 + nrm(k[11], (DEPTH, D), 0.02),
        'w_in': nrm(k[12], (DEPTH, D, IN_COLS), D ** -0.5),
        'na_rel_bias': nrm(k[13], (DEPTH, NA_HEADS, 2 * NA_KH - 1, 2 * NA_KW - 1), 0.1),
        'hgrn_lb': nrm(k[14], (2, DEPTH, HG_WIDTH), 0.5),
        'hgrn_onorm_g': 1.0 + nrm(k[15], (DEPTH, HG_DV), 0.02),
        'gmlp_vnorm_g': 1.0 + nrm(k[16], (DEPTH, GM_WIDTH), 0.02),
        'gmlp_ws': nrm(k[17], (DEPTH, GM_GROUPS, GM_CHUNK, GM_CHUNK), GM_CHUNK ** -0.5),
        'gmlp_b': 1.0 + nrm(k[18], (DEPTH, GM_GROUPS, GM_CHUNK), 0.1),
        'w_out': nrm(k[19], (DEPTH, MIX_WIDTH, D), MIX_WIDTH ** -0.5),
        'router_w': nrm(k[20], (DEPTH, D, N_EXPERTS), D ** -0.5),
        'router_b': nrm(k[21], (DEPTH, N_EXPERTS), 0.01),
        'w_gate': nrm(k[22], (DEPTH, N_EXPERTS, D, D_EXPERT), D ** -0.5),
        'b_gate': nrm(k[23], (DEPTH, N_EXPERTS, D_EXPERT), 0.01),
        'w_up': nrm(k[24], (DEPTH, N_EXPERTS, D, D_EXPERT), D ** -0.5),
        'b_up': nrm(k[25], (DEPTH, N_EXPERTS, D_EXPERT), 0.01),
        'w_down': nrm(k[26], (DEPTH, N_EXPERTS, D_EXPERT, D), D_EXPERT ** -0.5),
        'b_down': nrm(k[27], (DEPTH, N_EXPERTS, D), 0.01),
        'final_g': 1.0 + nrm(k[28], (D,), 0.02),
    }


def reference(x_prompt, x_sample, cache_k, cache_v, state_hgrn_fwd, state_hgrn_bwd, c, c_ctx,
              w_mod, b_mod, norm1_g, norm2_g, w_in, na_rel_bias, hgrn_lb, hgrn_onorm_g,
              gmlp_vnorm_g, gmlp_ws, gmlp_b, w_out, router_w, router_b, w_gate, b_gate,
              w_up, b_up, w_down, b_down, final_g):
    cond_ctx = jax.nn.silu(c_ctx)[None, :]
    cond_lat = jax.nn.silu(c)
    lb_soft = jax.nn.softmax(hgrn_lb.astype(jnp.float32), axis=1)
    lower = jnp.cumsum(lb_soft, axis=1) - lb_soft[:, :1]
    xp, xs = x_prompt, x_sample
    k_list, v_list, sf_list, sb_list = [], [], [], []
    for l in range(DEPTH):
        mixer_w = (w_in[l], w_out[l], na_rel_bias[l], lower[:, l], hgrn_onorm_g[l],
                   gmlp_vnorm_g[l], gmlp_ws[l], gmlp_b[l])
        moe_w = (router_w[l], router_b[l], w_gate[l], b_gate[l], w_up[l], b_up[l],
                 w_down[l], b_down[l])
        sh1, sc1, g1, sh2, sc2, g2 = ada_modulation(cond_ctx, w_mod[l], b_mod[l])
        out, (kc, vc, sf, sb) = token_mixer(modulate(rmsnorm(xp, norm1_g[l]), sh1, sc1), *mixer_w, None)
        xp = xp + g1 * out
        xp = xp + g2 * routed_moe(modulate(rmsnorm(xp, norm2_g[l]), sh2, sc2), *moe_w)
        k_list.append(kc)
        v_list.append(vc)
        sf_list.append(sf.astype(xp.dtype))
        sb_list.append(sb.astype(xp.dtype))
        sh1, sc1, g1, sh2, sc2, g2 = ada_modulation(cond_lat, w_mod[l], b_mod[l])
        ctx = (cache_k[:, l], cache_v[:, l], state_hgrn_fwd[:, l], state_hgrn_bwd[:, l])
        out, _ = token_mixer(modulate(rmsnorm(xs, norm1_g[l]), sh1, sc1), *mixer_w, ctx)
        xs = xs + g1 * out
        xs = xs + g2 * routed_moe(modulate(rmsnorm(xs, norm2_g[l]), sh2, sc2), *moe_w)
    y_prompt = rmsnorm(xp, final_g)
    y_sample = rmsnorm(xs, final_g)
    new_cache_k = jnp.stack(k_list, axis=1)
    new_cache_v = jnp.stack(v_list, axis=1)
    new_state_hgrn_fwd = jnp.stack(sf_list, axis=1)
    new_state_hgrn_bwd = jnp.stack(sb_list, axis=1)
    return (y_prompt, y_sample, new_cache_k, new_cache_v, new_state_hgrn_fwd, new_state_hgrn_bwd)
```

```python
import functools

import numpy as np
import jax
import jax.numpy as jnp
from jax import lax
from jax.experimental import pallas as pl
from jax.experimental.pallas import tpu as pltpu

F32 = jnp.float32
BF16 = jnp.bfloat16

D_MODEL = 1024
BATCH = 32
SEQ = 256
DEPTH = 2
DEC_BATCH = 2
DEC_SEQ = 1024
PAST_LEN = 512
GRID_W = 64
NA_HEADS = 8
NA_HEAD_DIM = 64
NA_WIDTH = NA_HEADS * NA_HEAD_DIM
NA_KH = 8
NA_KW = 16
HG_HEADS = 4
HG_DK = 64
HG_DV = 64
HG_WIDTH = HG_HEADS * HG_DV
HG_CHUNK = 32
F_FLOOR = 1e-30
GM_GROUPS = 4
GM_GDIM = 64
GM_WIDTH = GM_GROUPS * GM_GDIM
GM_CHUNK = 128
IN_COLS = 3 * NA_WIDTH + 5 * HG_WIDTH + 2 * GM_WIDTH
N_EXPERTS = 32
TOP_K = 4
SWIGLU_LIMIT = 7.0
SWIGLU_ALPHA = 1.702
RMS_EPS = 1e-6
NEG_INF = -1e30

T_PROMPT = BATCH * SEQ
T_SAMPLE = DEC_BATCH * DEC_SEQ
T_ALL = T_PROMPT + T_SAMPLE
TM = 256
N_TILES = T_ALL // TM
P_TILES = T_PROMPT // TM
MOE_BM = 256
MOE_SLOTS = -(-(T_ALL * TOP_K + N_EXPERTS * (MOE_BM - 1)) // MOE_BM) * MOE_BM
MOE_BLOCKS = MOE_SLOTS // MOE_BM
MOD_ROWS = 8
V7X_VMEM_LIMIT = 48 * 1024 * 1024

_CB_HQ, _CB_ZF, _CB_ZB, _CB_HI, _CB_HG, _CB_GU, _CB_GV = 6, 7, 8, 9, 10, 11, 12


def _dot(a, b):
    return jnp.dot(a, b, preferred_element_type=F32)


def _dot_nt(a, b):
    return lax.dot_general(a, b, (((1,), (1,)), ((), ())), preferred_element_type=F32)


def _dot_tn(a, b):
    return lax.dot_general(a, b, (((0,), (0,)), ((), ())), preferred_element_type=F32)


def _split3(x):
    hi = x.astype(BF16)
    r1 = x - hi.astype(F32)
    mid = r1.astype(BF16)
    lo = (r1 - mid.astype(F32)).astype(BF16)
    return hi, mid, lo


def _params(n_axes=1):
    return pltpu.CompilerParams(dimension_semantics=("arbitrary",) * n_axes,
                                vmem_limit_bytes=V7X_VMEM_LIMIT)


def _mod_kernel(cond_ref, w_ref, b_ref, o_ref):
    c = cond_ref[...]
    c = c * jax.nn.sigmoid(c)
    w = w_ref[0]
    c_hi = c.astype(BF16)
    c_lo = (c - c_hi.astype(F32)).astype(BF16)
    w_hi = w.astype(BF16)
    w_lo = (w - w_hi.astype(F32)).astype(BF16)
    o_ref[0] = _dot(c_hi, w_hi) + _dot(c_lo, w_hi) + _dot(c_hi, w_lo) + b_ref[0]


def _modulation(cond, w_mod, b_mod):
    tn = 1536
    return pl.pallas_call(
        _mod_kernel,
        out_shape=jax.ShapeDtypeStruct((DEPTH, MOD_ROWS, 6 * D_MODEL), F32),
        grid=(DEPTH, 6 * D_MODEL // tn),
        in_specs=[pl.BlockSpec((MOD_ROWS, D_MODEL), lambda l, j: (0, 0)),
                  pl.BlockSpec((1, D_MODEL, tn), lambda l, j: (l, 0, j)),
                  pl.BlockSpec((1, 1, tn), lambda l, j: (l, 0, j))],
        out_specs=pl.BlockSpec((1, MOD_ROWS, tn), lambda l, j: (l, 0, j)),
        compiler_params=_params(2),
        name="modulation",
    )(cond, w_mod, b_mod.reshape(DEPTH, 1, 6 * D_MODEL))


def _rms_mod(x, g, shift, scale):
    ms = jnp.mean(x * x, axis=-1, keepdims=True)
    y = x * lax.rsqrt(ms + RMS_EPS) * g
    return y * (1.0 + scale) + shift


def _inproj_first_kernel(x_ref, mod_ref, g_ref, w_ref, h_ref):
    hm = _rms_mod(x_ref[...], g_ref[...], mod_ref[0, 0:1, :], mod_ref[0, 1:2, :])
    h_ref[...] = _dot(hm.astype(BF16), w_ref[...])


def _inproj_next_kernel(x_ref, moe_ref, pmod_ref, mod_ref, g_ref, w_ref, h_ref, xo_ref):
    x = x_ref[...] + pmod_ref[0, 5:6, :] * moe_ref[...]
    xo_ref[...] = x
    hm = _rms_mod(x, g_ref[...], mod_ref[0, 0:1, :], mod_ref[0, 1:2, :])
    h_ref[...] = _dot(hm.astype(BF16), w_ref[...])


_TILE_SPEC = pl.BlockSpec((TM, D_MODEL), lambda i: (i, 0))
_MOD_SPEC = pl.BlockSpec((1, MOD_ROWS, D_MODEL), lambda i: (i, 0, 0))
_ROW_SPEC = pl.BlockSpec((1, D_MODEL), lambda i: (0, 0))


def _inproj(x, moe, prev_mod, mod, g, w_bf16):
    w_spec = pl.BlockSpec((D_MODEL, IN_COLS), lambda i: (0, 0))
    h_spec = pl.BlockSpec((TM, IN_COLS), lambda i: (i, 0))
    h_shape = jax.ShapeDtypeStruct((T_ALL, IN_COLS), F32)
    if moe is None:
        h = pl.pallas_call(
            _inproj_first_kernel, out_shape=h_shape, grid=(N_TILES,),
            in_specs=[_TILE_SPEC, _MOD_SPEC, _ROW_SPEC, w_spec], out_specs=h_spec,
            compiler_params=_params(), name="inproj_first",
        )(x, mod, g, w_bf16)
        return h, x
    return pl.pallas_call(
        _inproj_next_kernel,
        out_shape=(h_shape, jax.ShapeDtypeStruct((T_ALL, D_MODEL), F32)),
        grid=(N_TILES,),
        in_specs=[_TILE_SPEC, _TILE_SPEC, _MOD_SPEC, _MOD_SPEC, _ROW_SPEC, w_spec],
        out_specs=(h_spec, _TILE_SPEC),
        compiler_params=_params(), name="inproj_next",
    )(x, moe, prev_mod, mod, g, w_bf16)


def _pair_mask(hh):
    lane = lax.broadcasted_iota(jnp.int32, (1, 2 * NA_HEAD_DIM), 1)
    return (lane >= hh * NA_HEAD_DIM) & (lane < (hh + 1) * NA_HEAD_DIM)


def _attn_prompt_kernel(q_ref, k_ref, v_ref, o_ref):
    scale = NA_HEAD_DIM ** -0.5
    for p in range(NA_HEADS // 2):
        cols = slice(p * 128, (p + 1) * 128)
        qp = q_ref[:, cols] * scale
        kp = k_ref[:, cols].astype(BF16)
        vp = v_ref[:, cols].astype(BF16)
        outs = []
        for hh in range(2):
            qh = jnp.where(_pair_mask(hh), qp, 0.0).astype(BF16)
            s = _dot_nt(qh, kp)
            e = jnp.exp(s - jnp.max(s, axis=-1, keepdims=True))
            den = jnp.sum(e, axis=-1, keepdims=True)
            outs.append(_dot(e.astype(BF16), vp) / den)
        o_ref[:, cols] = jnp.where(_pair_mask(0), outs[0], outs[1]).astype(o_ref.dtype)


def _attn_prompt(h):
    return pl.pallas_call(
        _attn_prompt_kernel,
        out_shape=jax.ShapeDtypeStruct((T_PROMPT, NA_WIDTH), BF16),
        grid=(BATCH,),
        in_specs=[pl.BlockSpec((SEQ, NA_WIDTH), lambda b: (b, 0)),
                  pl.BlockSpec((SEQ, NA_WIDTH), lambda b: (b, 1)),
                  pl.BlockSpec((SEQ, NA_WIDTH), lambda b: (b, 2))],
        out_specs=pl.BlockSpec((SEQ, NA_WIDTH), lambda b: (b, 0)),
        compiler_params=_params(), name="attn_prompt",
    )(h, h, h)


_NA_ROWS = DEC_SEQ // GRID_W
_NA_LOC = NA_KH * GRID_W


def _attn_sample_kernel(q_ref, k_ref, v_ref, ck_ref, cv_ref, bias_ref, o_ref):
    r = pl.program_id(1)
    start = jnp.clip(r - NA_KH // 2, 0, _NA_ROWS - NA_KH)
    s0 = pl.multiple_of(start * GRID_W, GRID_W)
    scale = NA_HEAD_DIM ** -0.5
    for p in range(NA_HEADS // 2):
        cols = slice(p * 128, (p + 1) * 128)
        qp = q_ref[:, cols] * scale
        kl = k_ref[pl.ds(s0, _NA_LOC), cols].astype(BF16)
        vl = v_ref[pl.ds(s0, _NA_LOC), cols].astype(BF16)
        kc = ck_ref[0, :, cols].astype(BF16)
        vc = cv_ref[0, :, cols].astype(BF16)
        outs = []
        for hh in range(2):
            qh = jnp.where(_pair_mask(hh), qp, 0.0).astype(BF16)
            sl = _dot_nt(qh, kl) + bias_ref[2 * p + hh, 0]
            sc = _dot_nt(qh, kc)
            mx = jnp.maximum(jnp.max(sl, axis=-1, keepdims=True),
                             jnp.max(sc, axis=-1, keepdims=True))
            el = jnp.exp(sl - mx)
            ec = jnp.exp(sc - mx)
            den = jnp.sum(el, axis=-1, keepdims=True) + jnp.sum(ec, axis=-1, keepdims=True)
            outs.append((_dot(el.astype(BF16), vl) + _dot(ec.astype(BF16), vc)) / den)
        o_ref[:, cols] = jnp.where(_pair_mask(0), outs[0], outs[1]).astype(o_ref.dtype)


def _attn_sample(h, ck, cv, bias):
    q_row0 = T_PROMPT // GRID_W
    kv_row0 = T_PROMPT // DEC_SEQ
    return pl.pallas_call(
        _attn_sample_kernel,
        out_shape=jax.ShapeDtypeStruct((T_SAMPLE, NA_WIDTH), BF16),
        grid=(DEC_BATCH, _NA_ROWS),
        in_specs=[pl.BlockSpec((GRID_W, NA_WIDTH), lambda b, r: (q_row0 + b * _NA_ROWS + r, 0)),
                  pl.BlockSpec((DEC_SEQ, NA_WIDTH), lambda b, r: (kv_row0 + b, 1)),
                  pl.BlockSpec((DEC_SEQ, NA_WIDTH), lambda b, r: (kv_row0 + b, 2)),
                  pl.BlockSpec((1, PAST_LEN, NA_WIDTH), lambda b, r: (b, 0, 0)),
                  pl.BlockSpec((1, PAST_LEN, NA_WIDTH), lambda b, r: (b, 0, 0)),
                  pl.BlockSpec((NA_HEADS, 1, GRID_W, _NA_LOC), lambda b, r: (0, r, 0, 0))],
        out_specs=pl.BlockSpec((GRID_W, NA_WIDTH), lambda b, r: (b * _NA_ROWS + r, 0)),
        compiler_params=_params(2), name="attn_sample",
    )(h, h, h, ck, cv, bias)


def _na_bias_table(rel_bias):
    r = np.arange(_NA_ROWS)
    row_idx = np.clip(r - NA_KH // 2, 0, _NA_ROWS - NA_KH)[:, None] + np.arange(NA_KH)[None, :]
    dr = row_idx - r[:, None] + NA_KH - 1
    qc = np.arange(GRID_W)
    kc = np.arange(GRID_W)
    q_start = np.clip(qc - NA_KW // 2, 0, GRID_W - NA_KW)
    in_win = (kc[None, :] >= q_start[:, None]) & (kc[None, :] < q_start[:, None] + NA_KW)
    dc = np.clip(kc[None, :] - qc[:, None] + NA_KW - 1, 0, 2 * NA_KW - 2)
    b = rel_bias[:, dr[:, None, :, None], dc[None, :, None, :]].astype(F32)
    b = jnp.where(in_win[None, None, :, None, :], b, NEG_INF)
    return b.reshape(NA_HEADS, _NA_ROWS, GRID_W, _NA_LOC)


def _hgrn_kernel(*refs, n_tok, has_state):
    if has_state:
        (q_ref, zf_ref, zb_ref, v_ref, g_ref, lbf_ref, lbb_ref, og_ref, s0f_ref, s0b_ref,
         rec_ref, sf_ref, sb_ref, kf_s, bf_s, kb_s, bb_s, o_s, z_s, st_s) = refs
    else:
        (q_ref, zf_ref, zb_ref, v_ref, g_ref, lbf_ref, lbb_ref, og_ref,
         rec_ref, sf_ref, sb_ref, kf_s, bf_s, kb_s, bb_s, o_s, z_s, st_s) = refs
        s0f_ref = s0b_ref = None
    C = HG_CHUNK
    W = HG_WIDTH
    n_chunks = n_tok // C
    rr = lax.broadcasted_iota(jnp.int32, (W, W), 0)
    cc = lax.broadcasted_iota(jnp.int32, (W, W), 1)
    same_chunk = jnp.right_shift(rr, 5) == jnp.right_shift(cc, 5)
    tri_prefix = jnp.where(same_chunk & (cc <= rr), 1.0, 0.0).astype(BF16)
    tri_suffix = jnp.where(same_chunk & (cc >= rr), 1.0, 0.0).astype(BF16)
    same_head = jnp.right_shift(rr, 6) == jnp.right_shift(cc, 6)
    head_ones = jnp.where(same_head, 1.0, 0.0).astype(BF16)

    for ti in range(n_tok // W):
        rows = slice(ti * W, (ti + 1) * W)
        for z_ref, lb_ref, k_s, b_s, tri in ((zf_ref, lbf_ref, kf_s, bf_s, tri_prefix),
                                             (zb_ref, lbb_ref, kb_s, bb_s, tri_suffix)):
            z = z_ref[rows, :]
            lb = lb_ref[...]
            f = lb + (1.0 - lb) * jax.nn.sigmoid(z)
            logf = jnp.log(jnp.maximum(f, F_FLOOR))
            k_s[rows, :] = (1.0 - lb) * jax.nn.sigmoid(-z)
            hi, mid, lo = _split3(logf)
            b_s[rows, :] = _dot(tri, hi) + _dot(tri, mid) + _dot(tri, lo)

    srow = lax.broadcasted_iota(jnp.int32, (C, W), 0)

    def scan_direction(k_s, b_s, fwd):
        def chunk(ci, carry):
            c = ci if fwd else n_chunks - 1 - ci
            base = pl.multiple_of(c * C, C)
            q = q_ref[pl.ds(base, C), :]
            k = k_s[pl.ds(base, C), :]
            b = b_s[pl.ds(base, C), :]
            v = v_ref[pl.ds(base, C), :]
            for t in range(C):
                qt = q_ref[pl.ds(base + t, 1), :]
                bt = b_s[pl.ds(base + t, 1), :]
                keep = (srow <= t) if fwd else (srow >= t)
                zt = jnp.where(keep, (qt * k) * jnp.exp(bt - b), 0.0)
                z_s[t * C:(t + 1) * C, :] = zt.astype(BF16)
            a_rep = _dot(z_s[...], head_ones)
            o_intra = jnp.sum(a_rep.reshape(C, C, W) * v[None, :, :], axis=1)
            b_end = b_s[pl.ds(base + (C - 1 if fwd else 0), 1), :]
            q_in = q * jnp.exp(b)
            k_st = k * jnp.exp(b_end - b)
            st = st_s[...]
            o_inter = _dot_nt(q_in.astype(BF16), st.astype(BF16))
            upd = _dot_tn(v.astype(BF16), k_st.astype(BF16))
            st_s[...] = st * jnp.exp(b_end) + jnp.where(same_head, upd, 0.0)
            o = o_intra + o_inter
            if fwd:
                o_s[pl.ds(base, C), :] = o
            else:
                o_s[pl.ds(base, C), :] = o_s[pl.ds(base, C), :] + o
            return carry
        lax.fori_loop(0, n_chunks, chunk, 0)

    st_s[...] = s0f_ref[0] if has_state else jnp.zeros((W, W), F32)
    scan_direction(kf_s, bf_s, True)
    sf_ref[0] = st_s[...]
    st_s[...] = s0b_ref[0] if has_state else jnp.zeros((W, W), F32)
    scan_direction(kb_s, bb_s, False)
    sb_ref[0] = st_s[...]

    for ti in range(n_tok // W):
        rows = slice(ti * W, (ti + 1) * W)
        o = o_s[rows, :]
        sq = o * o
        sq_hi = sq.astype(BF16)
        sq_lo = (sq - sq_hi.astype(F32)).astype(BF16)
        ms = (_dot(sq_hi, head_ones) + _dot(sq_lo, head_ones)) * (1.0 / HG_DV)
        g = g_ref[rows, :]
        y = o * lax.rsqrt(ms + RMS_EPS) * og_ref[...] * (g * jax.nn.sigmoid(g))
        rec_ref[rows, :] = y.astype(rec_ref.dtype)


def _hgrn(h, lbf, lbb, og, s0f_t, s0b_t, n_tok, n_seq, row0):
    W = HG_WIDTH
    has_state = s0f_t is not None

    def col(cb):
        return pl.BlockSpec((n_tok, W), lambda i, cb=cb: (row0 + i, cb))

    vec = pl.BlockSpec((1, W), lambda i: (0, 0))
    st_spec = pl.BlockSpec((1, W, W), lambda i: (i, 0, 0))
    in_specs = [col(_CB_HQ), col(_CB_ZF), col(_CB_ZB), col(_CB_HI), col(_CB_HG), vec, vec, vec]
    args = [h, h, h, h, h, lbf, lbb, og]
    if has_state:
        in_specs += [st_spec, st_spec]
        args += [s0f_t, s0b_t]
    seq_f32 = pltpu.VMEM((n_tok, W), F32)
    return pl.pallas_call(
        functools.partial(_hgrn_kernel, n_tok=n_tok, has_state=has_state),
        out_shape=(jax.ShapeDtypeStruct((n_seq * n_tok, W), BF16),
                   jax.ShapeDtypeStruct((n_seq, W, W), F32),
                   jax.ShapeDtypeStruct((n_seq, W, W), F32)),
        grid=(n_seq,),
        in_specs=in_specs,
        out_specs=(pl.BlockSpec((n_tok, W), lambda i: (i, 0)), st_spec, st_spec),
        scratch_shapes=[seq_f32, seq_f32, seq_f32, seq_f32, seq_f32,
                        pltpu.VMEM((HG_CHUNK * HG_CHUNK, W), BF16),
                        pltpu.VMEM((W, W), F32)],
        compiler_params=_params(), name="hgrn_state" if has_state else "hgrn_zero",
    )(*args)


def _state_to_blockdiag_t(s):
    eye = jnp.eye(HG_HEADS, dtype=s.dtype)
    return jnp.einsum('bhkv,hg->bhvgk', s, eye).reshape(s.shape[0], HG_WIDTH, HG_HEADS * HG_DK)


def _blockdiag_t_to_state(st):
    s5 = st.reshape(st.shape[0], HG_HEADS, HG_DV, HG_HEADS, HG_DK)
    return jnp.stack([s5[:, hh, :, hh, :] for hh in range(HG_HEADS)], axis=1).transpose(0, 1, 3, 2)


def _gmlp_kernel(u_ref, v_ref, g_ref, ws_ref, b_ref, o_ref):
    v = v_ref[...]
    ms = jnp.mean(v * v, axis=-1, keepdims=True)
    vn = (v * lax.rsqrt(ms + RMS_EPS) * g_ref[...]).astype(BF16)
    lane = lax.broadcasted_iota(jnp.int32, (1, GM_WIDTH), 1)
    z = b_ref[...]
    for gi in range(GM_GROUPS):
        zg = _dot(ws_ref[gi], vn)
        in_group = (lane >= gi * GM_GDIM) & (lane < (gi + 1) * GM_GDIM)
        z = z + jnp.where(in_group, zg, 0.0)
    o_ref[...] = (u_ref[...] * z).astype(o_ref.dtype)


def _gmlp(h, vnorm_g, ws_bf16, bias_full):
    W = GM_WIDTH
    return pl.pallas_call(
        _gmlp_kernel,
        out_shape=jax.ShapeDtypeStruct((T_ALL, W), BF16),
        grid=(T_ALL // GM_CHUNK,),
        in_specs=[pl.BlockSpec((GM_CHUNK, W), lambda i: (i, _CB_GU)),
                  pl.BlockSpec((GM_CHUNK, W), lambda i: (i, _CB_GV)),
                  pl.BlockSpec((1, W), lambda i: (0, 0)),
                  pl.BlockSpec((GM_GROUPS, GM_CHUNK, GM_CHUNK), lambda i: (0, 0, 0)),
                  pl.BlockSpec((GM_CHUNK, W), lambda i: (0, 0))],
        out_specs=pl.BlockSpec((GM_CHUNK, W), lambda i: (i, 0)),
        compiler_params=_params(), name="gmlp",
    )(h, h, vnorm_g, ws_bf16, bias_full)


_RT_LANES = 128


def _outproj_kernel(att_ref, rec_ref, mlp_ref, x_ref, mod_ref, g_ref, w_ref, wr_ref, br_ref,
                    x1_ref, h2_ref, idx_ref, gate_ref):
    out = (_dot(att_ref[...], w_ref[0:NA_WIDTH, :])
           + _dot(rec_ref[...], w_ref[NA_WIDTH:NA_WIDTH + HG_WIDTH, :])
           + _dot(mlp_ref[...], w_ref[NA_WIDTH + HG_WIDTH:, :]))
    x1 = x_ref[...] + mod_ref[0, 2:3, :] * out
    x1_ref[...] = x1
    h2 = _rms_mod(x1, g_ref[...], mod_ref[0, 3:4, :], mod_ref[0, 4:5, :])
    h2_ref[...] = h2.astype(h2_ref.dtype)
    h_hi = h2.astype(BF16)
    h_lo = (h2 - h_hi.astype(F32)).astype(BF16)
    wr = wr_ref[...]
    w_hi = wr.astype(BF16)
    w_lo = (wr - w_hi.astype(F32)).astype(BF16)
    logits = _dot(h_hi, w_hi) + _dot(h_lo, w_hi) + _dot(h_hi, w_lo) + br_ref[...]
    lane_e = lax.broadcasted_iota(jnp.int32, (TM, N_EXPERTS), 1).astype(F32)
    lane_o = lax.broadcasted_iota(jnp.int32, (TM, _RT_LANES), 1)
    idx_acc = jnp.zeros((TM, _RT_LANES), F32)
    val_acc = jnp.zeros((TM, _RT_LANES), F32)
    top0 = None
    den = jnp.zeros((TM, 1), F32)
    work = logits
    for kk in range(TOP_K):
        m = jnp.max(work, axis=-1, keepdims=True)
        first = jnp.min(jnp.where(work == m, lane_e, float(N_EXPERTS)), axis=-1, keepdims=True)
        if kk == 0:
            top0 = m
        e = jnp.exp(m - top0)
        den = den + e
        idx_acc = jnp.where(lane_o == kk, first, idx_acc)
        val_acc = jnp.where(lane_o == kk, e, val_acc)
        work = jnp.where(lane_e == first, -jnp.inf, work)
    idx_ref[...] = idx_acc.astype(jnp.int32)
    gate_ref[...] = val_acc / den


def _outproj(att, rec, mlp, x, mod, g, w_bf16, wr, br):
    def tile(width):
        return pl.BlockSpec((TM, width), lambda i: (i, 0))

    return pl.pallas_call(
        _outproj_kernel,
        out_shape=(jax.ShapeDtypeStruct((T_ALL, D_MODEL), F32),
                   jax.ShapeDtypeStruct((T_ALL, D_MODEL), BF16),
                   jax.ShapeDtypeStruct((T_ALL, _RT_LANES), jnp.int32),
                   jax.ShapeDtypeStruct((T_ALL, _RT_LANES), F32)),
        grid=(N_TILES,),
        in_specs=[tile(NA_WIDTH), tile(HG_WIDTH), tile(GM_WIDTH), _TILE_SPEC, _MOD_SPEC, _ROW_SPEC,
                  pl.BlockSpec((D_MODEL, D_MODEL), lambda i: (0, 0)),
                  pl.BlockSpec((D_MODEL, N_EXPERTS), lambda i: (0, 0)),
                  pl.BlockSpec((1, N_EXPERTS), lambda i: (0, 0))],
        out_specs=(_TILE_SPEC, _TILE_SPEC, tile(_RT_LANES), tile(_RT_LANES)),
        compiler_params=_params(), name="outproj_router",
    )(att, rec, mlp, x, mod, g, w_bf16, wr, br)


def _moe_kernel(blk_e_ref, blk_on_ref, x_ref, wg_ref, bg_ref, wu_ref, bu_ref, wd_ref, bd_ref, y_ref):
    j = pl.program_id(0)

    @pl.when(blk_on_ref[j] != 0)
    def _():
        x = x_ref[...]
        gate = jnp.minimum(_dot(x, wg_ref[0]) + bg_ref[0], SWIGLU_LIMIT)
        up = jnp.clip(_dot(x, wu_ref[0]) + bu_ref[0], -SWIGLU_LIMIT, SWIGLU_LIMIT)
        glu = gate * jax.nn.sigmoid(SWIGLU_ALPHA * gate)
        act = ((up + 1.0) * glu).astype(BF16)
        y_ref[...] = (_dot(act, wd_ref[0]) + bd_ref[0]).astype(y_ref.dtype)

    @pl.when(blk_on_ref[j] == 0)
    def _():
        y_ref[...] = jnp.zeros_like(y_ref)


def _moe(blk_e, blk_on, x_sorted, wg, bg, wu, bu, wd, bd):
    w_spec = pl.BlockSpec((1, D_MODEL, D_MODEL), lambda j, be, on: (be[j], 0, 0))
    b_spec = pl.BlockSpec((1, 1, D_MODEL), lambda j, be, on: (be[j], 0, 0))
    x_spec = pl.BlockSpec((MOE_BM, D_MODEL), lambda j, be, on: (j, 0))
    return pl.pallas_call(
        _moe_kernel,
        out_shape=jax.ShapeDtypeStruct((MOE_SLOTS, D_MODEL), BF16),
        grid_spec=pltpu.PrefetchScalarGridSpec(
            num_scalar_prefetch=2, grid=(MOE_BLOCKS,),
            in_specs=[x_spec, w_spec, b_spec, w_spec, b_spec, w_spec, b_spec],
            out_specs=x_spec),
        compiler_params=_params(), name="moe_experts",
    )(blk_e, blk_on, x_sorted, wg, bg, wu, bu, wd, bd)


def _route(top_i):
    sel = jnp.sum(jax.nn.one_hot(top_i, N_EXPERTS, dtype=jnp.int32), axis=1)
    csum = jnp.cumsum(sel, axis=0)
    counts = csum[-1]
    rank = jnp.take_along_axis(csum - sel, top_i, axis=1)
    padded = (counts + MOE_BM - 1) // MOE_BM * MOE_BM
    pad_end = jnp.cumsum(padded)
    pad_start = pad_end - padded
    dest = pad_start[top_i] + rank
    blk_first = jnp.arange(MOE_BLOCKS, dtype=jnp.int32) * MOE_BM
    blk_e = jnp.minimum(jnp.searchsorted(pad_end, blk_first, side='right'), N_EXPERTS - 1)
    blk_on = (blk_first < pad_end[-1]).astype(jnp.int32)
    last_e = jnp.max(jnp.where(counts > 0, jnp.arange(N_EXPERTS), 0))
    blk_e = jnp.where(blk_on != 0, blk_e, last_e).astype(jnp.int32)
    return dest.astype(jnp.int32), blk_e, blk_on


def _final_kernel(x_ref, moe_ref, mod_ref, g_ref, o_ref):
    x = x_ref[...] + mod_ref[0, 5:6, :] * moe_ref[...]
    ms = jnp.mean(x * x, axis=-1, keepdims=True)
    o_ref[...] = x * lax.rsqrt(ms + RMS_EPS) * g_ref[...]


def _final(x, moe, mod, g):
    return pl.pallas_call(
        _final_kernel, out_shape=jax.ShapeDtypeStruct((T_ALL, D_MODEL), F32), grid=(N_TILES,),
        in_specs=[_TILE_SPEC, _TILE_SPEC, _MOD_SPEC, _ROW_SPEC], out_specs=_TILE_SPEC,
        compiler_params=_params(), name="final_norm",
    )(x, moe, mod, g)


def kernel(x_prompt, x_sample, cache_k, cache_v, state_hgrn_fwd, state_hgrn_bwd, c, c_ctx, w_mod, b_mod, norm1_g, norm2_g, w_in, na_rel_bias, hgrn_lb, hgrn_onorm_g, gmlp_vnorm_g, gmlp_ws, gmlp_b, w_out, router_w, router_b, w_gate, b_gate, w_up, b_up, w_down, b_down, final_g):
    x = jnp.concatenate([x_prompt.reshape(T_PROMPT, D_MODEL), x_sample.reshape(T_SAMPLE, D_MODEL)], axis=0)

    cond = jnp.zeros((MOD_ROWS, D_MODEL), F32).at[0].set(c_ctx).at[1:1 + DEC_BATCH].set(c)
    mod = _modulation(cond, w_mod, b_mod)
    tile_row = np.concatenate([np.zeros(P_TILES, np.int32),
                               1 + np.arange(N_TILES - P_TILES, dtype=np.int32) // (DEC_SEQ // TM)])
    mod_tiles = mod[:, tile_row].reshape(DEPTH, N_TILES, 6, D_MODEL)
    mod_tiles = jnp.pad(mod_tiles, ((0, 0), (0, 0), (0, MOD_ROWS - 6), (0, 0)))

    lb_soft = jax.nn.softmax(hgrn_lb.astype(F32), axis=1)
    lower = jnp.cumsum(lb_soft, axis=1) - lb_soft[:, :1]

    k_list, v_list, sf_list, sb_list = [], [], [], []
    moe_out = None
    for l in range(DEPTH):
        h, x = _inproj(x, moe_out, mod_tiles[l - 1] if l else None, mod_tiles[l],
                       norm1_g[l][None, :], w_in[l].astype(BF16))
        k_list.append(h[:T_PROMPT, NA_WIDTH:2 * NA_WIDTH].reshape(BATCH, SEQ, NA_HEADS, NA_HEAD_DIM))
        v_list.append(h[:T_PROMPT, 2 * NA_WIDTH:3 * NA_WIDTH].reshape(BATCH, SEQ, NA_HEADS, NA_HEAD_DIM))

        att_p = _attn_prompt(h)
        att_s = _attn_sample(h, cache_k[:, l].reshape(DEC_BATCH, PAST_LEN, NA_WIDTH),
                             cache_v[:, l].reshape(DEC_BATCH, PAST_LEN, NA_WIDTH),
                             _na_bias_table(na_rel_bias[l]))
        lbf = lower[0, l][None, :]
        lbb = lower[1, l][None, :]
        og = jnp.tile(hgrn_onorm_g[l], HG_HEADS)[None, :]
        rec_p, sf_t, sb_t = _hgrn(h, lbf, lbb, og, None, None, SEQ, BATCH, 0)
        rec_s, _, _ = _hgrn(h, lbf, lbb, og, _state_to_blockdiag_t(state_hgrn_fwd[:, l].astype(F32)),
                            _state_to_blockdiag_t(state_hgrn_bwd[:, l].astype(F32)),
                            DEC_SEQ, DEC_BATCH, T_PROMPT // DEC_SEQ)
        sf_list.append(_blockdiag_t_to_state(sf_t))
        sb_list.append(_blockdiag_t_to_state(sb_t))
        gm_bias = jnp.repeat(gmlp_b[l].T, GM_GDIM, axis=1)
        mlp = _gmlp(h, gmlp_vnorm_g[l][None, :], gmlp_ws[l].astype(BF16), gm_bias)

        att = jnp.concatenate([att_p, att_s], axis=0)
        rec = jnp.concatenate([rec_p, rec_s], axis=0)
        x, h2, idx_pad, gate_pad = _outproj(att, rec, mlp, x, mod_tiles[l], norm2_g[l][None, :],
                                            w_out[l].astype(BF16), router_w[l], router_b[l][None, :])
        top_i = idx_pad[:, :TOP_K]
        gates = gate_pad[:, :TOP_K]
        dest, blk_e, blk_on = _route(top_i)
        slot_tok = jnp.zeros((MOE_SLOTS,), jnp.int32).at[dest.reshape(-1)].set(
            jnp.repeat(jnp.arange(T_ALL, dtype=jnp.int32), TOP_K))
        y_sorted = _moe(blk_e, blk_on, h2[slot_tok],
                        w_gate[l].astype(BF16), b_gate[l][:, None, :],
                        w_up[l].astype(BF16), b_up[l][:, None, :],
                        w_down[l].astype(BF16), b_down[l][:, None, :])
        moe_out = jnp.einsum('tk,tkd->td', gates, y_sorted[dest].astype(F32))

    y = _final(x, moe_out, mod_tiles[DEPTH - 1], final_g[None, :])
    y_prompt = y[:T_PROMPT].reshape(BATCH, SEQ, D_MODEL)
    y_sample = y[T_PROMPT:].reshape(DEC_BATCH, DEC_SEQ, D_MODEL)
    return (y_prompt, y_sample, jnp.stack(k_list, axis=1), jnp.stack(v_list, axis=1),
            jnp.stack(sf_list, axis=1), jnp.stack(sb_list, axis=1))
```

```python
import functools

import numpy as np
import jax
import jax.numpy as jnp
from jax import lax
from jax.experimental import pallas as pl
from jax.experimental.pallas import tpu as pltpu
from jax.experimental.pallas import tpu_sc as plsc

F32 = jnp.float32
BF16 = jnp.bfloat16

D_MODEL = 1024
BATCH = 32
SEQ = 256
DEPTH = 2
DEC_BATCH = 2
DEC_SEQ = 1024
PAST_LEN = 512
GRID_W = 64
NA_HEADS = 8
NA_HEAD_DIM = 64
NA_WIDTH = NA_HEADS * NA_HEAD_DIM
NA_KH = 8
NA_KW = 16
HG_HEADS = 4
HG_DK = 64
HG_DV = 64
HG_WIDTH = HG_HEADS * HG_DV
HG_CHUNK = 32
F_FLOOR = 1e-30
GM_GROUPS = 4
GM_GDIM = 64
GM_WIDTH = GM_GROUPS * GM_GDIM
GM_CHUNK = 128
IN_COLS = 3 * NA_WIDTH + 5 * HG_WIDTH + 2 * GM_WIDTH
N_EXPERTS = 32
TOP_K = 4
SWIGLU_LIMIT = 7.0
SWIGLU_ALPHA = 1.702
RMS_EPS = 1e-6
NEG_INF = -1e30

T_PROMPT = BATCH * SEQ
T_SAMPLE = DEC_BATCH * DEC_SEQ
T_ALL = T_PROMPT + T_SAMPLE
TM = 256
N_TILES = T_ALL // TM
P_TILES = T_PROMPT // TM
MOE_BM = 256
MOE_SLOTS = -(-(T_ALL * TOP_K + N_EXPERTS * (MOE_BM - 1)) // MOE_BM) * MOE_BM
MOE_BLOCKS = MOE_SLOTS // MOE_BM
MOD_ROWS = 8
V7X_VMEM_LIMIT = 48 * 1024 * 1024

_CB_HQ, _CB_ZF, _CB_ZB, _CB_HI, _CB_HG, _CB_GU, _CB_GV = 6, 7, 8, 9, 10, 11, 12


def _dot(a, b):
    return jnp.dot(a, b, preferred_element_type=F32)


def _dot_nt(a, b):
    return lax.dot_general(a, b, (((1,), (1,)), ((), ())), preferred_element_type=F32)


def _dot_tn(a, b):
    return lax.dot_general(a, b, (((0,), (0,)), ((), ())), preferred_element_type=F32)


def _split3(x):
    hi = x.astype(BF16)
    r1 = x - hi.astype(F32)
    mid = r1.astype(BF16)
    lo = (r1 - mid.astype(F32)).astype(BF16)
    return hi, mid, lo


D_PACK = D_MODEL // 2
N_SPLIT = 2
D_SLAB = D_PACK // N_SPLIT


def _pack_halves(x):
    half = x.shape[1] // 2
    lo = pltpu.bitcast(x[:, :half].astype(BF16).astype(F32), jnp.uint32)
    hi = pltpu.bitcast(x[:, half:].astype(BF16).astype(F32), jnp.uint32)
    return pltpu.bitcast(jnp.right_shift(lo, jnp.uint32(16)) | hi, jnp.int32)


def _unpack_halves(w):
    u = pltpu.bitcast(w, jnp.uint32)
    lo = pltpu.bitcast(jnp.left_shift(u, jnp.uint32(16)), F32)
    hi = pltpu.bitcast(u & jnp.uint32(0xFFFF0000), F32)
    return lo, hi


def _load_slabs(refs, *lead):
    return jnp.concatenate([r[lead] if lead else r[...] for r in refs], axis=1)


def _store_slabs(refs, packed):
    for si, r in enumerate(refs):
        r[...] = packed[:, si * D_SLAB:(si + 1) * D_SLAB]


def _params(n_axes=1):
    return pltpu.CompilerParams(dimension_semantics=("arbitrary",) * n_axes,
                                vmem_limit_bytes=V7X_VMEM_LIMIT)


def _mod_kernel(cond_ref, w_ref, b_ref, o_ref):
    c = cond_ref[...]
    c = c * jax.nn.sigmoid(c)
    w = w_ref[0]
    c_hi = c.astype(BF16)
    c_lo = (c - c_hi.astype(F32)).astype(BF16)
    w_hi = w.astype(BF16)
    w_lo = (w - w_hi.astype(F32)).astype(BF16)
    o_ref[0] = _dot(c_hi, w_hi) + _dot(c_lo, w_hi) + _dot(c_hi, w_lo) + b_ref[0]


def _modulation(cond, w_mod, b_mod):
    tn = 1536
    return pl.pallas_call(
        _mod_kernel,
        out_shape=jax.ShapeDtypeStruct((DEPTH, MOD_ROWS, 6 * D_MODEL), F32),
        grid=(DEPTH, 6 * D_MODEL // tn),
        in_specs=[pl.BlockSpec((MOD_ROWS, D_MODEL), lambda l, j: (0, 0)),
                  pl.BlockSpec((1, D_MODEL, tn), lambda l, j: (l, 0, j)),
                  pl.BlockSpec((1, 1, tn), lambda l, j: (l, 0, j))],
        out_specs=pl.BlockSpec((1, MOD_ROWS, tn), lambda l, j: (l, 0, j)),
        compiler_params=_params(2),
        name="modulation",
    )(cond, w_mod, b_mod.reshape(DEPTH, 1, 6 * D_MODEL))


def _rms_mod(x, g, shift, scale):
    ms = jnp.mean(x * x, axis=-1, keepdims=True)
    y = x * lax.rsqrt(ms + RMS_EPS) * g
    return y * (1.0 + scale) + shift


def _inproj_first_kernel(x_ref, mod_ref, g_ref, w_ref, h_ref):
    hm = _rms_mod(x_ref[...], g_ref[...], mod_ref[0, 0:1, :], mod_ref[0, 1:2, :])
    h_ref[...] = _dot(hm.astype(BF16), w_ref[...])


def _combine_experts(yg_refs, gate_ref):
    gates = gate_ref[...]
    lo_acc = hi_acc = None
    for kk in range(TOP_K):
        lo, hi = _unpack_halves(_load_slabs(yg_refs, kk))
        gk = gates[:, kk:kk + 1]
        lo_acc = gk * lo if lo_acc is None else lo_acc + gk * lo
        hi_acc = gk * hi if hi_acc is None else hi_acc + gk * hi
    return jnp.concatenate([lo_acc, hi_acc], axis=1)


def _inproj_next_kernel(x_ref, yga_ref, ygb_ref, gate_ref, pmod_ref, mod_ref, g_ref, w_ref, h_ref, xo_ref):
    x = x_ref[...] + pmod_ref[0, 5:6, :] * _combine_experts((yga_ref, ygb_ref), gate_ref)
    xo_ref[...] = x
    hm = _rms_mod(x, g_ref[...], mod_ref[0, 0:1, :], mod_ref[0, 1:2, :])
    h_ref[...] = _dot(hm.astype(BF16), w_ref[...])


_TILE_SPEC = pl.BlockSpec((TM, D_MODEL), lambda i: (i, 0))
_MOD_SPEC = pl.BlockSpec((1, MOD_ROWS, D_MODEL), lambda i: (i, 0, 0))
_ROW_SPEC = pl.BlockSpec((1, D_MODEL), lambda i: (0, 0))
_RT_LANES = 128
_YG_SPEC = pl.BlockSpec((TOP_K, TM, D_SLAB), lambda i: (0, i, 0))
_GATE_SPEC = pl.BlockSpec((TM, _RT_LANES), lambda i: (i, 0))


def _inproj(x, moe, prev_mod, mod, g, w_bf16):
    w_spec = pl.BlockSpec((D_MODEL, IN_COLS), lambda i: (0, 0))
    h_spec = pl.BlockSpec((TM, IN_COLS), lambda i: (i, 0))
    h_shape = jax.ShapeDtypeStruct((T_ALL, IN_COLS), F32)
    if moe is None:
        h = pl.pallas_call(
            _inproj_first_kernel, out_shape=h_shape, grid=(N_TILES,),
            in_specs=[_TILE_SPEC, _MOD_SPEC, _ROW_SPEC, w_spec], out_specs=h_spec,
            compiler_params=_params(), name="inproj_first",
        )(x, mod, g, w_bf16)
        return h, x
    return pl.pallas_call(
        _inproj_next_kernel,
        out_shape=(h_shape, jax.ShapeDtypeStruct((T_ALL, D_MODEL), F32)),
        grid=(N_TILES,),
        in_specs=[_TILE_SPEC, _YG_SPEC, _YG_SPEC, _GATE_SPEC, _MOD_SPEC, _MOD_SPEC, _ROW_SPEC, w_spec],
        out_specs=(h_spec, _TILE_SPEC),
        compiler_params=_params(), name="inproj_next",
    )(x, *moe[0], moe[1], prev_mod, mod, g, w_bf16)


def _pair_mask(hh):
    lane = lax.broadcasted_iota(jnp.int32, (1, 2 * NA_HEAD_DIM), 1)
    return (lane >= hh * NA_HEAD_DIM) & (lane < (hh + 1) * NA_HEAD_DIM)


def _attn_prompt_kernel(q_ref, k_ref, v_ref, o_ref):
    scale = NA_HEAD_DIM ** -0.5
    for p in range(NA_HEADS // 2):
        cols = slice(p * 128, (p + 1) * 128)
        qp = q_ref[:, cols] * scale
        kp = k_ref[:, cols].astype(BF16)
        vp = v_ref[:, cols].astype(BF16)
        outs = []
        for hh in range(2):
            qh = jnp.where(_pair_mask(hh), qp, 0.0).astype(BF16)
            s = _dot_nt(qh, kp)
            e = jnp.exp(s - jnp.max(s, axis=-1, keepdims=True))
            den = jnp.sum(e, axis=-1, keepdims=True)
            outs.append(_dot(e.astype(BF16), vp) / den)
        o_ref[:, cols] = jnp.where(_pair_mask(0), outs[0], outs[1]).astype(o_ref.dtype)


def _attn_prompt(h):
    return pl.pallas_call(
        _attn_prompt_kernel,
        out_shape=jax.ShapeDtypeStruct((T_PROMPT, NA_WIDTH), BF16),
        grid=(BATCH,),
        in_specs=[pl.BlockSpec((SEQ, NA_WIDTH), lambda b: (b, 0)),
                  pl.BlockSpec((SEQ, NA_WIDTH), lambda b: (b, 1)),
                  pl.BlockSpec((SEQ, NA_WIDTH), lambda b: (b, 2))],
        out_specs=pl.BlockSpec((SEQ, NA_WIDTH), lambda b: (b, 0)),
        compiler_params=_params(), name="attn_prompt",
    )(h, h, h)


_NA_ROWS = DEC_SEQ // GRID_W
_NA_LOC = NA_KH * GRID_W


def _attn_sample_kernel(q_ref, k_ref, v_ref, ck_ref, cv_ref, bias_ref, o_ref):
    r = pl.program_id(1)
    start = jnp.clip(r - NA_KH // 2, 0, _NA_ROWS - NA_KH)
    s0 = pl.multiple_of(start * GRID_W, GRID_W)
    scale = NA_HEAD_DIM ** -0.5
    for p in range(NA_HEADS // 2):
        cols = slice(p * 128, (p + 1) * 128)
        qp = q_ref[:, cols] * scale
        kl = k_ref[pl.ds(s0, _NA_LOC), cols].astype(BF16)
        vl = v_ref[pl.ds(s0, _NA_LOC), cols].astype(BF16)
        kc = ck_ref[0, :, cols].astype(BF16)
        vc = cv_ref[0, :, cols].astype(BF16)
        outs = []
        for hh in range(2):
            qh = jnp.where(_pair_mask(hh), qp, 0.0).astype(BF16)
            sl = _dot_nt(qh, kl) + bias_ref[2 * p + hh, 0]
            sc = _dot_nt(qh, kc)
            mx = jnp.maximum(jnp.max(sl, axis=-1, keepdims=True),
                             jnp.max(sc, axis=-1, keepdims=True))
            el = jnp.exp(sl - mx)
            ec = jnp.exp(sc - mx)
            den = jnp.sum(el, axis=-1, keepdims=True) + jnp.sum(ec, axis=-1, keepdims=True)
            outs.append((_dot(el.astype(BF16), vl) + _dot(ec.astype(BF16), vc)) / den)
        o_ref[:, cols] = jnp.where(_pair_mask(0), outs[0], outs[1]).astype(o_ref.dtype)


def _attn_sample(h, ck, cv, bias):
    q_row0 = T_PROMPT // GRID_W
    kv_row0 = T_PROMPT // DEC_SEQ
    return pl.pallas_call(
        _attn_sample_kernel,
        out_shape=jax.ShapeDtypeStruct((T_SAMPLE, NA_WIDTH), BF16),
        grid=(DEC_BATCH, _NA_ROWS),
        in_specs=[pl.BlockSpec((GRID_W, NA_WIDTH), lambda b, r: (q_row0 + b * _NA_ROWS + r, 0)),
                  pl.BlockSpec((DEC_SEQ, NA_WIDTH), lambda b, r: (kv_row0 + b, 1)),
                  pl.BlockSpec((DEC_SEQ, NA_WIDTH), lambda b, r: (kv_row0 + b, 2)),
                  pl.BlockSpec((1, PAST_LEN, NA_WIDTH), lambda b, r: (b, 0, 0)),
                  pl.BlockSpec((1, PAST_LEN, NA_WIDTH), lambda b, r: (b, 0, 0)),
                  pl.BlockSpec((NA_HEADS, 1, GRID_W, _NA_LOC), lambda b, r: (0, r, 0, 0))],
        out_specs=pl.BlockSpec((GRID_W, NA_WIDTH), lambda b, r: (b * _NA_ROWS + r, 0)),
        compiler_params=_params(2), name="attn_sample",
    )(h, h, h, ck, cv, bias)


def _na_bias_table(rel_bias):
    r = np.arange(_NA_ROWS)
    row_idx = np.clip(r - NA_KH // 2, 0, _NA_ROWS - NA_KH)[:, None] + np.arange(NA_KH)[None, :]
    dr = row_idx - r[:, None] + NA_KH - 1
    qc = np.arange(GRID_W)
    kc = np.arange(GRID_W)
    q_start = np.clip(qc - NA_KW // 2, 0, GRID_W - NA_KW)
    in_win = (kc[None, :] >= q_start[:, None]) & (kc[None, :] < q_start[:, None] + NA_KW)
    dc = np.clip(kc[None, :] - qc[:, None] + NA_KW - 1, 0, 2 * NA_KW - 2)
    b = rel_bias[:, dr[:, None, :, None], dc[None, :, None, :]].astype(F32)
    b = jnp.where(in_win[None, None, :, None, :], b, NEG_INF)
    return b.reshape(NA_HEADS, _NA_ROWS, GRID_W, _NA_LOC)


def _hgrn_kernel(*refs, n_tok, has_state):
    if has_state:
        (q_ref, zf_ref, zb_ref, v_ref, g_ref, lbf_ref, lbb_ref, og_ref, s0f_ref, s0b_ref,
         rec_ref, sf_ref, sb_ref, kf_s, bf_s, kb_s, bb_s, o_s, z_s, st_s) = refs
    else:
        (q_ref, zf_ref, zb_ref, v_ref, g_ref, lbf_ref, lbb_ref, og_ref,
         rec_ref, sf_ref, sb_ref, kf_s, bf_s, kb_s, bb_s, o_s, z_s, st_s) = refs
        s0f_ref = s0b_ref = None
    C = HG_CHUNK
    W = HG_WIDTH
    n_chunks = n_tok // C
    rr = lax.broadcasted_iota(jnp.int32, (W, W), 0)
    cc = lax.broadcasted_iota(jnp.int32, (W, W), 1)
    same_chunk = jnp.right_shift(rr, 5) == jnp.right_shift(cc, 5)
    tri_prefix = jnp.where(same_chunk & (cc <= rr), 1.0, 0.0).astype(BF16)
    tri_suffix = jnp.where(same_chunk & (cc >= rr), 1.0, 0.0).astype(BF16)
    same_head = jnp.right_shift(rr, 6) == jnp.right_shift(cc, 6)
    head_ones = jnp.where(same_head, 1.0, 0.0).astype(BF16)

    for ti in range(n_tok // W):
        rows = slice(ti * W, (ti + 1) * W)
        for z_ref, lb_ref, k_s, b_s, tri in ((zf_ref, lbf_ref, kf_s, bf_s, tri_prefix),
                                             (zb_ref, lbb_ref, kb_s, bb_s, tri_suffix)):
            z = z_ref[rows, :]
            lb = lb_ref[...]
            f = lb + (1.0 - lb) * jax.nn.sigmoid(z)
            logf = jnp.log(jnp.maximum(f, F_FLOOR))
            k_s[rows, :] = (1.0 - lb) * jax.nn.sigmoid(-z)
            hi, mid, lo = _split3(logf)
            b_s[rows, :] = _dot(tri, hi) + _dot(tri, mid) + _dot(tri, lo)

    srow = lax.broadcasted_iota(jnp.int32, (C, W), 0)

    def scan_direction(k_s, b_s, fwd):
        def chunk(ci, carry):
            c = ci if fwd else n_chunks - 1 - ci
            base = pl.multiple_of(c * C, C)
            q = q_ref[pl.ds(base, C), :]
            k = k_s[pl.ds(base, C), :]
            b = b_s[pl.ds(base, C), :]
            v = v_ref[pl.ds(base, C), :]
            for t in range(C):
                qt = q_ref[pl.ds(base + t, 1), :]
                bt = b_s[pl.ds(base + t, 1), :]
                keep = (srow <= t) if fwd else (srow >= t)
                zt = jnp.where(keep, (qt * k) * jnp.exp(bt - b), 0.0)
                z_s[t * C:(t + 1) * C, :] = zt.astype(BF16)
            a_rep = _dot(z_s[...], head_ones)
            o_intra = jnp.sum(a_rep.reshape(C, C, W) * v[None, :, :], axis=1)
            b_end = b_s[pl.ds(base + (C - 1 if fwd else 0), 1), :]
            q_in = q * jnp.exp(b)
            k_st = k * jnp.exp(b_end - b)
            st = st_s[...]
            o_inter = _dot_nt(q_in.astype(BF16), st.astype(BF16))
            upd = _dot_tn(v.astype(BF16), k_st.astype(BF16))
            st_s[...] = st * jnp.exp(b_end) + jnp.where(same_head, upd, 0.0)
            o = o_intra + o_inter
            if fwd:
                o_s[pl.ds(base, C), :] = o
            else:
                o_s[pl.ds(base, C), :] = o_s[pl.ds(base, C), :] + o
            return carry
        lax.fori_loop(0, n_chunks, chunk, 0)

    st_s[...] = s0f_ref[0] if has_state else jnp.zeros((W, W), F32)
    scan_direction(kf_s, bf_s, True)
    sf_ref[0] = st_s[...]
    st_s[...] = s0b_ref[0] if has_state else jnp.zeros((W, W), F32)
    scan_direction(kb_s, bb_s, False)
    sb_ref[0] = st_s[...]

    for ti in range(n_tok // W):
        rows = slice(ti * W, (ti + 1) * W)
        o = o_s[rows, :]
        sq = o * o
        sq_hi = sq.astype(BF16)
        sq_lo = (sq - sq_hi.astype(F32)).astype(BF16)
        ms = (_dot(sq_hi, head_ones) + _dot(sq_lo, head_ones)) * (1.0 / HG_DV)
        g = g_ref[rows, :]
        y = o * lax.rsqrt(ms + RMS_EPS) * og_ref[...] * (g * jax.nn.sigmoid(g))
        rec_ref[rows, :] = y.astype(rec_ref.dtype)


def _hgrn(h, lbf, lbb, og, s0f_t, s0b_t, n_tok, n_seq, row0):
    W = HG_WIDTH
    has_state = s0f_t is not None

    def col(cb):
        return pl.BlockSpec((n_tok, W), lambda i, cb=cb: (row0 + i, cb))

    vec = pl.BlockSpec((1, W), lambda i: (0, 0))
    st_spec = pl.BlockSpec((1, W, W), lambda i: (i, 0, 0))
    in_specs = [col(_CB_HQ), col(_CB_ZF), col(_CB_ZB), col(_CB_HI), col(_CB_HG), vec, vec, vec]
    args = [h, h, h, h, h, lbf, lbb, og]
    if has_state:
        in_specs += [st_spec, st_spec]
        args += [s0f_t, s0b_t]
    seq_f32 = pltpu.VMEM((n_tok, W), F32)
    return pl.pallas_call(
        functools.partial(_hgrn_kernel, n_tok=n_tok, has_state=has_state),
        out_shape=(jax.ShapeDtypeStruct((n_seq * n_tok, W), BF16),
                   jax.ShapeDtypeStruct((n_seq, W, W), F32),
                   jax.ShapeDtypeStruct((n_seq, W, W), F32)),
        grid=(n_seq,),
        in_specs=in_specs,
        out_specs=(pl.BlockSpec((n_tok, W), lambda i: (i, 0)), st_spec, st_spec),
        scratch_shapes=[seq_f32, seq_f32, seq_f32, seq_f32, seq_f32,
                        pltpu.VMEM((HG_CHUNK * HG_CHUNK, W), BF16),
                        pltpu.VMEM((W, W), F32)],
        compiler_params=_params(), name="hgrn_state" if has_state else "hgrn_zero",
    )(*args)


def _state_to_blockdiag_t(s):
    eye = jnp.eye(HG_HEADS, dtype=s.dtype)
    return jnp.einsum('bhkv,hg->bhvgk', s, eye).reshape(s.shape[0], HG_WIDTH, HG_HEADS * HG_DK)


def _blockdiag_t_to_state(st):
    s5 = st.reshape(st.shape[0], HG_HEADS, HG_DV, HG_HEADS, HG_DK)
    return jnp.stack([s5[:, hh, :, hh, :] for hh in range(HG_HEADS)], axis=1).transpose(0, 1, 3, 2)


def _gmlp_kernel(u_ref, v_ref, g_ref, ws_ref, b_ref, o_ref):
    v = v_ref[...]
    ms = jnp.mean(v * v, axis=-1, keepdims=True)
    vn = (v * lax.rsqrt(ms + RMS_EPS) * g_ref[...]).astype(BF16)
    lane = lax.broadcasted_iota(jnp.int32, (1, GM_WIDTH), 1)
    z = b_ref[...]
    for gi in range(GM_GROUPS):
        zg = _dot(ws_ref[gi], vn)
        in_group = (lane >= gi * GM_GDIM) & (lane < (gi + 1) * GM_GDIM)
        z = z + jnp.where(in_group, zg, 0.0)
    o_ref[...] = (u_ref[...] * z).astype(o_ref.dtype)


def _gmlp(h, vnorm_g, ws_bf16, bias_full):
    W = GM_WIDTH
    return pl.pallas_call(
        _gmlp_kernel,
        out_shape=jax.ShapeDtypeStruct((T_ALL, W), BF16),
        grid=(T_ALL // GM_CHUNK,),
        in_specs=[pl.BlockSpec((GM_CHUNK, W), lambda i: (i, _CB_GU)),
                  pl.BlockSpec((GM_CHUNK, W), lambda i: (i, _CB_GV)),
                  pl.BlockSpec((1, W), lambda i: (0, 0)),
                  pl.BlockSpec((GM_GROUPS, GM_CHUNK, GM_CHUNK), lambda i: (0, 0, 0)),
                  pl.BlockSpec((GM_CHUNK, W), lambda i: (0, 0))],
        out_specs=pl.BlockSpec((GM_CHUNK, W), lambda i: (i, 0)),
        compiler_params=_params(), name="gmlp",
    )(h, h, vnorm_g, ws_bf16, bias_full)


def _outproj_kernel(att_ref, rec_ref, mlp_ref, x_ref, mod_ref, g_ref, w_ref, wr_ref, br_ref,
                    x1_ref, h2a_ref, h2b_ref, idx_ref, gate_ref, cnt_ref):
    @pl.when(pl.program_id(0) == 0)
    def _():
        cnt_ref[...] = jnp.zeros_like(cnt_ref)

    out = (_dot(att_ref[...], w_ref[0:NA_WIDTH, :])
           + _dot(rec_ref[...], w_ref[NA_WIDTH:NA_WIDTH + HG_WIDTH, :])
           + _dot(mlp_ref[...], w_ref[NA_WIDTH + HG_WIDTH:, :]))
    x1 = x_ref[...] + mod_ref[0, 2:3, :] * out
    x1_ref[...] = x1
    h2 = _rms_mod(x1, g_ref[...], mod_ref[0, 3:4, :], mod_ref[0, 4:5, :])
    _store_slabs((h2a_ref, h2b_ref), _pack_halves(h2))
    h_hi = h2.astype(BF16)
    h_lo = (h2 - h_hi.astype(F32)).astype(BF16)
    wr = wr_ref[...]
    w_hi = wr.astype(BF16)
    w_lo = (wr - w_hi.astype(F32)).astype(BF16)
    logits = _dot(h_hi, w_hi) + _dot(h_lo, w_hi) + _dot(h_hi, w_lo) + br_ref[...]
    lane_e = lax.broadcasted_iota(jnp.int32, (TM, N_EXPERTS), 1).astype(F32)
    lane_o = lax.broadcasted_iota(jnp.int32, (TM, _RT_LANES), 1)
    idx_acc = jnp.zeros((TM, _RT_LANES), F32)
    val_acc = jnp.zeros((TM, _RT_LANES), F32)
    top0 = None
    den = jnp.zeros((TM, 1), F32)
    work = logits
    picks = []
    for kk in range(TOP_K):
        m = jnp.max(work, axis=-1, keepdims=True)
        first = jnp.min(jnp.where(work == m, lane_e, float(N_EXPERTS)), axis=-1, keepdims=True)
        if kk == 0:
            top0 = m
        e = jnp.exp(m - top0)
        den = den + e
        idx_acc = jnp.where(lane_o == kk, first, idx_acc)
        val_acc = jnp.where(lane_o == kk, e, val_acc)
        picks.append(lane_e == first)
        work = jnp.where(picks[-1], -jnp.inf, work)
    gate_ref[...] = val_acc / den
    sel = jnp.zeros((TM, N_EXPERTS), F32)
    for pk in picks:
        sel = sel + jnp.where(pk, 1.0, 0.0)
    rr = lax.broadcasted_iota(jnp.int32, (TM, TM), 0)
    cc = lax.broadcasted_iota(jnp.int32, (TM, TM), 1)
    earlier = jnp.where(cc < rr, 1.0, 0.0).astype(BF16)
    seen = cnt_ref[0:1, 0:N_EXPERTS]
    before = _dot(earlier, sel.astype(BF16)) + seen
    for kk, pk in enumerate(picks):
        rank = jnp.sum(jnp.where(pk, before, 0.0), axis=-1, keepdims=True)
        idx_acc = jnp.where(lane_o == TOP_K + kk, rank, idx_acc)
    idx_ref[...] = idx_acc.astype(jnp.int32)
    cnt_ref[0:1, 0:N_EXPERTS] = seen + jnp.sum(sel, axis=0, keepdims=True)


def _outproj(att, rec, mlp, x, mod, g, w_bf16, wr, br):
    def tile(width):
        return pl.BlockSpec((TM, width), lambda i: (i, 0))

    return pl.pallas_call(
        _outproj_kernel,
        out_shape=(jax.ShapeDtypeStruct((T_ALL, D_MODEL), F32),
                   jax.ShapeDtypeStruct((T_ALL, D_SLAB), jnp.int32),
                   jax.ShapeDtypeStruct((T_ALL, D_SLAB), jnp.int32),
                   jax.ShapeDtypeStruct((T_ALL, _RT_LANES), jnp.int32),
                   jax.ShapeDtypeStruct((T_ALL, _RT_LANES), F32),
                   jax.ShapeDtypeStruct((8, _RT_LANES), F32)),
        grid=(N_TILES,),
        in_specs=[tile(NA_WIDTH), tile(HG_WIDTH), tile(GM_WIDTH), _TILE_SPEC, _MOD_SPEC, _ROW_SPEC,
                  pl.BlockSpec((D_MODEL, D_MODEL), lambda i: (0, 0)),
                  pl.BlockSpec((D_MODEL, N_EXPERTS), lambda i: (0, 0)),
                  pl.BlockSpec((1, N_EXPERTS), lambda i: (0, 0))],
        out_specs=(_TILE_SPEC, tile(D_SLAB), tile(D_SLAB), tile(_RT_LANES), tile(_RT_LANES),
                   pl.BlockSpec((8, _RT_LANES), lambda i: (0, 0))),
        compiler_params=_params(), name="outproj_router",
    )(att, rec, mlp, x, mod, g, w_bf16, wr, br)


_W_CAST_ROWS = 128


def _moe_kernel(blk_e_ref, blk_on_ref, blk_new_ref, xa_ref, xb_ref, wg_ref, bg_ref, wu_ref, bu_ref, wd_ref, bd_ref,
                ya_ref, yb_ref, wg_s, wu_s, wd_s):
    j = pl.program_id(0)

    @pl.when(blk_new_ref[j] != 0)
    def _():
        def cast_rows(ci, carry):
            rows = pl.ds(pl.multiple_of(ci * _W_CAST_ROWS, _W_CAST_ROWS), _W_CAST_ROWS)
            wg_s[rows, :] = wg_ref[0, rows, :].astype(BF16)
            wu_s[rows, :] = wu_ref[0, rows, :].astype(BF16)
            wd_s[rows, :] = wd_ref[0, rows, :].astype(BF16)
            return carry
        lax.fori_loop(0, D_MODEL // _W_CAST_ROWS, cast_rows, 0)

    @pl.when(blk_on_ref[j] != 0)
    def _():
        lo, hi = _unpack_halves(_load_slabs((xa_ref, xb_ref)))
        x = jnp.concatenate([lo.astype(BF16), hi.astype(BF16)], axis=1)
        gate = jnp.minimum(_dot(x, wg_s[...]) + bg_ref[0], SWIGLU_LIMIT)
        up = jnp.clip(_dot(x, wu_s[...]) + bu_ref[0], -SWIGLU_LIMIT, SWIGLU_LIMIT)
        glu = gate * jax.nn.sigmoid(SWIGLU_ALPHA * gate)
        act = ((up + 1.0) * glu).astype(BF16)
        _store_slabs((ya_ref, yb_ref), _pack_halves(_dot(act, wd_s[...]) + bd_ref[0]))

    @pl.when(blk_on_ref[j] == 0)
    def _():
        ya_ref[...] = jnp.zeros_like(ya_ref)
        yb_ref[...] = jnp.zeros_like(yb_ref)


def _moe(blk_e, blk_on, blk_new, x_sorted, wg, bg, wu, bu, wd, bd):
    w_spec = pl.BlockSpec((1, D_MODEL, D_MODEL), lambda j, be, on, nw: (be[j], 0, 0))
    b_spec = pl.BlockSpec((1, 1, D_MODEL), lambda j, be, on, nw: (be[j], 0, 0))
    x_spec = pl.BlockSpec((MOE_BM, D_SLAB), lambda j, be, on, nw: (j, 0))
    w_bf16 = pltpu.VMEM((D_MODEL, D_MODEL), BF16)
    return pl.pallas_call(
        _moe_kernel,
        out_shape=(jax.ShapeDtypeStruct((MOE_SLOTS, D_SLAB), jnp.int32),) * N_SPLIT,
        grid_spec=pltpu.PrefetchScalarGridSpec(
            num_scalar_prefetch=3, grid=(MOE_BLOCKS,),
            in_specs=[x_spec, x_spec, w_spec, b_spec, w_spec, b_spec, w_spec, b_spec],
            out_specs=(x_spec, x_spec),
            scratch_shapes=[w_bf16, w_bf16, w_bf16]),
        compiler_params=_params(), name="moe_experts",
    )(blk_e, blk_on, blk_new, *x_sorted, wg, bg, wu, bu, wd, bd)


def _route(top_i, rank, counts):
    experts = jnp.arange(N_EXPERTS, dtype=jnp.int32)
    padded = (counts + MOE_BM - 1) // MOE_BM * MOE_BM
    pad_end = jnp.cumsum(padded)
    pad_start = pad_end - padded
    start_of = jnp.sum(jnp.where(top_i[..., None] == experts, pad_start, 0), axis=-1)
    dest = start_of + rank
    blk_first = jnp.arange(MOE_BLOCKS, dtype=jnp.int32) * MOE_BM
    blk_e = jnp.minimum(jnp.sum((pad_end[None, :] <= blk_first[:, None]).astype(jnp.int32), axis=1),
                        N_EXPERTS - 1)
    blk_on = (blk_first < pad_end[-1]).astype(jnp.int32)
    last_e = jnp.max(jnp.where(counts > 0, experts, 0))
    blk_e = jnp.where(blk_on != 0, blk_e, last_e).astype(jnp.int32)
    blk_new = jnp.concatenate([jnp.ones((1,), jnp.int32), (blk_e[1:] != blk_e[:-1]).astype(jnp.int32)])
    return dest.astype(jnp.int32), blk_e, blk_on, blk_new


_SC_WINDOW = 128


def _sc_mesh():
    return plsc.VectorSubcoreMesh(core_axis_name="core", subcore_axis_name="subcore")


def _sc_scatter_rows(srcs, idx, n_out):
    n_src, width = srcs[0].shape
    n_idx = idx.shape[1]
    src_windows = n_src // _SC_WINDOW

    def body(*refs):
        x_hbm = refs[:len(srcs)]
        i_hbm = refs[len(srcs)]
        o_hbm = refs[len(srcs) + 1:]
        for xs, os_ in zip(x_hbm, o_hbm):
            def step(x_vmem, i_vmem, os_=os_):
                pltpu.sync_copy(x_vmem, os_.at[i_vmem.at[0]])

            pltpu.emit_pipeline(
                step, grid=(n_idx // _SC_WINDOW,),
                in_specs=[pl.BlockSpec((_SC_WINDOW, width), lambda i: (i % src_windows, 0)),
                          pl.BlockSpec((1, _SC_WINDOW), lambda i: (0, i))],
                out_specs=[],
                core_axis_name=("core", "subcore"),
                dimension_semantics=(pltpu.PARALLEL,),
            )(xs, i_hbm)

    out_type = tuple(jax.ShapeDtypeStruct((n_out, width), s.dtype) for s in srcs)
    return pl.kernel(body, out_type=out_type, mesh=_sc_mesh(), scratch_types=[],
                     name="sc_scatter_rows")(*srcs, idx)


def _sc_gather_rows(tables, idx):
    n_idx = idx.shape[1]
    width = tables[0].shape[1]

    def body(*refs):
        t_hbm = refs[:len(tables)]
        i_hbm = refs[len(tables)]
        o_hbm = refs[len(tables) + 1:]
        for ts, os_ in zip(t_hbm, o_hbm):
            def step(i_vmem, o_vmem, ts=ts):
                pltpu.sync_copy(ts.at[i_vmem.at[0]], o_vmem)

            pltpu.emit_pipeline(
                step, grid=(n_idx // _SC_WINDOW,),
                in_specs=[pl.BlockSpec((1, _SC_WINDOW), lambda i: (0, i))],
                out_specs=[pl.BlockSpec((_SC_WINDOW, width), lambda i: (i, 0))],
                core_axis_name=("core", "subcore"),
                dimension_semantics=(pltpu.PARALLEL,),
            )(i_hbm, os_)

    out_type = tuple(jax.ShapeDtypeStruct((n_idx, width), t.dtype) for t in tables)
    return pl.kernel(body, out_type=out_type, mesh=_sc_mesh(), scratch_types=[],
                     name="sc_gather_rows")(*tables, idx)


def _final_kernel(x_ref, yga_ref, ygb_ref, gate_ref, mod_ref, g_ref, o_ref):
    x = x_ref[...] + mod_ref[0, 5:6, :] * _combine_experts((yga_ref, ygb_ref), gate_ref)
    ms = jnp.mean(x * x, axis=-1, keepdims=True)
    o_ref[...] = x * lax.rsqrt(ms + RMS_EPS) * g_ref[...]


def _final(x, moe, mod, g):
    return pl.pallas_call(
        _final_kernel, out_shape=jax.ShapeDtypeStruct((T_ALL, D_MODEL), F32), grid=(N_TILES,),
        in_specs=[_TILE_SPEC, _YG_SPEC, _YG_SPEC, _GATE_SPEC, _MOD_SPEC, _ROW_SPEC], out_specs=_TILE_SPEC,
        compiler_params=_params(), name="final_norm",
    )(x, *moe[0], moe[1], mod, g)


def kernel(x_prompt, x_sample, cache_k, cache_v, state_hgrn_fwd, state_hgrn_bwd, c, c_ctx, w_mod, b_mod, norm1_g, norm2_g, w_in, na_rel_bias, hgrn_lb, hgrn_onorm_g, gmlp_vnorm_g, gmlp_ws, gmlp_b, w_out, router_w, router_b, w_gate, b_gate, w_up, b_up, w_down, b_down, final_g):
    x = jnp.concatenate([x_prompt.reshape(T_PROMPT, D_MODEL), x_sample.reshape(T_SAMPLE, D_MODEL)], axis=0)

    cond = jnp.zeros((MOD_ROWS, D_MODEL), F32).at[0].set(c_ctx).at[1:1 + DEC_BATCH].set(c)
    mod = _modulation(cond, w_mod, b_mod)
    tile_row = np.concatenate([np.zeros(P_TILES, np.int32),
                               1 + np.arange(N_TILES - P_TILES, dtype=np.int32) // (DEC_SEQ // TM)])
    mod_tiles = mod[:, tile_row].reshape(DEPTH, N_TILES, 6, D_MODEL)
    mod_tiles = jnp.pad(mod_tiles, ((0, 0), (0, 0), (0, MOD_ROWS - 6), (0, 0)))

    lb_soft = jax.nn.softmax(hgrn_lb.astype(F32), axis=1)
    lower = jnp.cumsum(lb_soft, axis=1) - lb_soft[:, :1]

    k_list, v_list, sf_list, sb_list = [], [], [], []
    moe_out = None
    for l in range(DEPTH):
        h, x = _inproj(x, moe_out, mod_tiles[l - 1] if l else None, mod_tiles[l],
                       norm1_g[l][None, :], w_in[l].astype(BF16))
        k_list.append(h[:T_PROMPT, NA_WIDTH:2 * NA_WIDTH].reshape(BATCH, SEQ, NA_HEADS, NA_HEAD_DIM))
        v_list.append(h[:T_PROMPT, 2 * NA_WIDTH:3 * NA_WIDTH].reshape(BATCH, SEQ, NA_HEADS, NA_HEAD_DIM))

        att_p = _attn_prompt(h)
        att_s = _attn_sample(h, cache_k[:, l].reshape(DEC_BATCH, PAST_LEN, NA_WIDTH),
                             cache_v[:, l].reshape(DEC_BATCH, PAST_LEN, NA_WIDTH),
                             _na_bias_table(na_rel_bias[l]))
        lbf = lower[0, l][None, :]
        lbb = lower[1, l][None, :]
        og = jnp.tile(hgrn_onorm_g[l], HG_HEADS)[None, :]
        rec_p, sf_t, sb_t = _hgrn(h, lbf, lbb, og, None, None, SEQ, BATCH, 0)
        rec_s, _, _ = _hgrn(h, lbf, lbb, og, _state_to_blockdiag_t(state_hgrn_fwd[:, l].astype(F32)),
                            _state_to_blockdiag_t(state_hgrn_bwd[:, l].astype(F32)),
                            DEC_SEQ, DEC_BATCH, T_PROMPT // DEC_SEQ)
        sf_list.append(_blockdiag_t_to_state(sf_t))
        sb_list.append(_blockdiag_t_to_state(sb_t))
        gm_bias = jnp.repeat(gmlp_b[l].T, GM_GDIM, axis=1)
        mlp = _gmlp(h, gmlp_vnorm_g[l][None, :], gmlp_ws[l].astype(BF16), gm_bias)

        att = jnp.concatenate([att_p, att_s], axis=0)
        rec = jnp.concatenate([rec_p, rec_s], axis=0)
        x, h2a, h2b, idx_pad, gate_pad, cnt = _outproj(att, rec, mlp, x, mod_tiles[l], norm2_g[l][None, :],
                                                 w_out[l].astype(BF16), router_w[l], router_b[l][None, :])
        dest, blk_e, blk_on, blk_new = _route(idx_pad[:, :TOP_K], idx_pad[:, TOP_K:2 * TOP_K],
                                              cnt[0, :N_EXPERTS].astype(jnp.int32))
        dest_flat = dest.T.reshape(1, TOP_K * T_ALL)
        x_sorted = _sc_scatter_rows((h2a, h2b), dest_flat, MOE_SLOTS)
        y_sorted = _moe(blk_e, blk_on, blk_new, x_sorted,
                        w_gate[l], b_gate[l][:, None, :], w_up[l], b_up[l][:, None, :],
                        w_down[l], b_down[l][:, None, :])
        y_tok = _sc_gather_rows(y_sorted, dest_flat)
        moe_out = ([yt.reshape(TOP_K, T_ALL, D_SLAB) for yt in y_tok], gate_pad)

    y = _final(x, moe_out, mod_tiles[DEPTH - 1], final_g[None, :])
    y_prompt = y[:T_PROMPT].reshape(BATCH, SEQ, D_MODEL)
    y_sample = y[T_PROMPT:].reshape(DEC_BATCH, DEC_SEQ, D_MODEL)
    return (y_prompt, y_sample, jnp.stack(k_list, axis=1), jnp.stack(v_list, axis=1),
            jnp.stack(sf_list, axis=1), jnp.stack(sb_list, axis=1))
```

```python
import functools

import numpy as np
import jax
import jax.numpy as jnp
from jax import lax
from jax.experimental import pallas as pl
from jax.experimental.pallas import tpu as pltpu
from jax.experimental.pallas import tpu_sc as plsc

F32 = jnp.float32
BF16 = jnp.bfloat16

D_MODEL = 1024
BATCH = 32
SEQ = 256
DEPTH = 2
DEC_BATCH = 2
DEC_SEQ = 1024
PAST_LEN = 512
GRID_W = 64
NA_HEADS = 8
NA_HEAD_DIM = 64
NA_WIDTH = NA_HEADS * NA_HEAD_DIM
NA_KH = 8
NA_KW = 16
HG_HEADS = 4
HG_DK = 64
HG_DV = 64
HG_WIDTH = HG_HEADS * HG_DV
HG_CHUNK = 32
F_FLOOR = 1e-30
GM_GROUPS = 4
GM_GDIM = 64
GM_WIDTH = GM_GROUPS * GM_GDIM
GM_CHUNK = 128
IN_COLS = 3 * NA_WIDTH + 5 * HG_WIDTH + 2 * GM_WIDTH
N_EXPERTS = 32
TOP_K = 4
SWIGLU_LIMIT = 7.0
SWIGLU_ALPHA = 1.702
RMS_EPS = 1e-6
NEG_INF = -1e30

T_PROMPT = BATCH * SEQ
T_SAMPLE = DEC_BATCH * DEC_SEQ
T_ALL = T_PROMPT + T_SAMPLE
TM = 256
N_TILES = T_ALL // TM
P_TILES = T_PROMPT // TM
MOE_BM = 256
MOE_SLOTS = -(-(T_ALL * TOP_K + N_EXPERTS * (MOE_BM - 1)) // MOE_BM) * MOE_BM
MOE_BLOCKS = MOE_SLOTS // MOE_BM
MOD_ROWS = 8
V7X_VMEM_LIMIT = 48 * 1024 * 1024

_CB_HQ, _CB_ZF, _CB_ZB, _CB_HI, _CB_HG, _CB_GU, _CB_GV = 6, 7, 8, 9, 10, 11, 12


def _dot(a, b):
    return jnp.dot(a, b, preferred_element_type=F32)


def _dot_nt(a, b):
    return lax.dot_general(a, b, (((1,), (1,)), ((), ())), preferred_element_type=F32)


def _dot_tn(a, b):
    return lax.dot_general(a, b, (((0,), (0,)), ((), ())), preferred_element_type=F32)


def _split3(x):
    hi = x.astype(BF16)
    r1 = x - hi.astype(F32)
    mid = r1.astype(BF16)
    lo = (r1 - mid.astype(F32)).astype(BF16)
    return hi, mid, lo


D_PACK = D_MODEL // 2
N_SPLIT = 2
D_SLAB = D_PACK // N_SPLIT


def _pack_halves(x):
    half = x.shape[1] // 2
    lo = pltpu.bitcast(x[:, :half].astype(BF16).astype(F32), jnp.uint32)
    hi = pltpu.bitcast(x[:, half:].astype(BF16).astype(F32), jnp.uint32)
    return pltpu.bitcast(jnp.right_shift(lo, jnp.uint32(16)) | hi, jnp.int32)


def _unpack_halves(w):
    u = pltpu.bitcast(w, jnp.uint32)
    lo = pltpu.bitcast(jnp.left_shift(u, jnp.uint32(16)), F32)
    hi = pltpu.bitcast(u & jnp.uint32(0xFFFF0000), F32)
    return lo, hi


def _load_slabs(refs, *lead):
    return jnp.concatenate([r[lead] if lead else r[...] for r in refs], axis=1)


def _store_slabs(refs, packed):
    for si, r in enumerate(refs):
        r[...] = packed[:, si * D_SLAB:(si + 1) * D_SLAB]


def _params(n_axes=1):
    return pltpu.CompilerParams(dimension_semantics=("arbitrary",) * n_axes,
                                vmem_limit_bytes=V7X_VMEM_LIMIT)


def _mod_kernel(cond_ref, w_ref, b_ref, o_ref):
    c = cond_ref[...]
    c = c * jax.nn.sigmoid(c)
    w = w_ref[0]
    c_hi = c.astype(BF16)
    c_lo = (c - c_hi.astype(F32)).astype(BF16)
    w_hi = w.astype(BF16)
    w_lo = (w - w_hi.astype(F32)).astype(BF16)
    o_ref[0] = _dot(c_hi, w_hi) + _dot(c_lo, w_hi) + _dot(c_hi, w_lo) + b_ref[0]


def _modulation(cond, w_mod, b_mod):
    tn = 1536
    return pl.pallas_call(
        _mod_kernel,
        out_shape=jax.ShapeDtypeStruct((DEPTH, MOD_ROWS, 6 * D_MODEL), F32),
        grid=(DEPTH, 6 * D_MODEL // tn),
        in_specs=[pl.BlockSpec((MOD_ROWS, D_MODEL), lambda l, j: (0, 0)),
                  pl.BlockSpec((1, D_MODEL, tn), lambda l, j: (l, 0, j)),
                  pl.BlockSpec((1, 1, tn), lambda l, j: (l, 0, j))],
        out_specs=pl.BlockSpec((1, MOD_ROWS, tn), lambda l, j: (l, 0, j)),
        compiler_params=_params(2),
        name="modulation",
    )(cond, w_mod, b_mod.reshape(DEPTH, 1, 6 * D_MODEL))


def _rms_mod(x, g, shift, scale):
    ms = jnp.mean(x * x, axis=-1, keepdims=True)
    y = x * lax.rsqrt(ms + RMS_EPS) * g
    return y * (1.0 + scale) + shift


def _project_in(hm, w_ref, h_ref, kc_ref, vc_ref):
    h = _dot(hm.astype(BF16), w_ref[...])
    h_ref[...] = h

    @pl.when(pl.program_id(0) < P_TILES)
    def _():
        kc_ref[0] = h[:, NA_WIDTH:2 * NA_WIDTH]
        vc_ref[0] = h[:, 2 * NA_WIDTH:3 * NA_WIDTH]


def _inproj_first_kernel(x_ref, mod_ref, g_ref, w_ref, h_ref, kc_ref, vc_ref):
    hm = _rms_mod(x_ref[...], g_ref[...], mod_ref[0, 0:1, :], mod_ref[0, 1:2, :])
    _project_in(hm, w_ref, h_ref, kc_ref, vc_ref)


def _combine_experts(yg_refs, gate_ref):
    gates = gate_ref[...]
    lo_acc = hi_acc = None
    for kk in range(TOP_K):
        lo, hi = _unpack_halves(_load_slabs(yg_refs, kk))
        gk = gates[:, kk:kk + 1]
        lo_acc = gk * lo if lo_acc is None else lo_acc + gk * lo
        hi_acc = gk * hi if hi_acc is None else hi_acc + gk * hi
    return jnp.concatenate([lo_acc, hi_acc], axis=1)


def _inproj_next_kernel(x_ref, yga_ref, ygb_ref, gate_ref, pmod_ref, mod_ref, g_ref, w_ref,
                        h_ref, xo_ref, kc_ref, vc_ref):
    x = x_ref[...] + pmod_ref[0, 5:6, :] * _combine_experts((yga_ref, ygb_ref), gate_ref)
    xo_ref[...] = x
    hm = _rms_mod(x, g_ref[...], mod_ref[0, 0:1, :], mod_ref[0, 1:2, :])
    _project_in(hm, w_ref, h_ref, kc_ref, vc_ref)


_TILE_SPEC = pl.BlockSpec((TM, D_MODEL), lambda i: (i, 0))
_MOD_SPEC = pl.BlockSpec((1, MOD_ROWS, D_MODEL), lambda i: (i, 0, 0))
_ROW_SPEC = pl.BlockSpec((1, D_MODEL), lambda i: (0, 0))
_RT_LANES = 128
_YG_SPEC = pl.BlockSpec((TOP_K, TM, D_SLAB), lambda i: (0, i, 0))
_GATE_SPEC = pl.BlockSpec((TM, _RT_LANES), lambda i: (i, 0))


def _inproj(x, moe, prev_mod, mod, g, w_bf16):
    w_spec = pl.BlockSpec((D_MODEL, IN_COLS), lambda i: (0, 0))
    h_spec = pl.BlockSpec((TM, IN_COLS), lambda i: (i, 0))
    h_shape = jax.ShapeDtypeStruct((T_ALL, IN_COLS), F32)
    c_spec = pl.BlockSpec((1, SEQ, NA_WIDTH), lambda i: (jnp.minimum(i, P_TILES - 1), 0, 0))
    c_shape = jax.ShapeDtypeStruct((BATCH, SEQ, NA_WIDTH), F32)
    if moe is None:
        h, kc, vc = pl.pallas_call(
            _inproj_first_kernel, out_shape=(h_shape, c_shape, c_shape), grid=(N_TILES,),
            in_specs=[_TILE_SPEC, _MOD_SPEC, _ROW_SPEC, w_spec], out_specs=(h_spec, c_spec, c_spec),
            compiler_params=_params(), name="inproj_first",
        )(x, mod, g, w_bf16)
        return h, x, (kc, vc)
    h, x, kc, vc = pl.pallas_call(
        _inproj_next_kernel,
        out_shape=(h_shape, jax.ShapeDtypeStruct((T_ALL, D_MODEL), F32), c_shape, c_shape),
        grid=(N_TILES,),
        in_specs=[_TILE_SPEC, _YG_SPEC, _YG_SPEC, _GATE_SPEC, _MOD_SPEC, _MOD_SPEC, _ROW_SPEC, w_spec],
        out_specs=(h_spec, _TILE_SPEC, c_spec, c_spec),
        compiler_params=_params(), name="inproj_next",
    )(x, *moe[0], moe[1], prev_mod, mod, g, w_bf16)
    return h, x, (kc, vc)


def _pair_mask(hh):
    lane = lax.broadcasted_iota(jnp.int32, (1, 2 * NA_HEAD_DIM), 1)
    return (lane >= hh * NA_HEAD_DIM) & (lane < (hh + 1) * NA_HEAD_DIM)


def _attn_prompt_kernel(q_ref, k_ref, v_ref, o_ref):
    scale = NA_HEAD_DIM ** -0.5
    for p in range(NA_HEADS // 2):
        cols = slice(p * 128, (p + 1) * 128)
        qp = q_ref[:, cols] * scale
        kp = k_ref[:, cols].astype(BF16)
        vp = v_ref[:, cols].astype(BF16)
        outs = []
        for hh in range(2):
            qh = jnp.where(_pair_mask(hh), qp, 0.0).astype(BF16)
            s = _dot_nt(qh, kp)
            e = jnp.exp(s - jnp.max(s, axis=-1, keepdims=True))
            den = jnp.sum(e, axis=-1, keepdims=True)
            outs.append(_dot(e.astype(BF16), vp) / den)
        o_ref[:, cols] = jnp.where(_pair_mask(0), outs[0], outs[1]).astype(o_ref.dtype)


def _attn_prompt(h):
    return pl.pallas_call(
        _attn_prompt_kernel,
        out_shape=jax.ShapeDtypeStruct((T_PROMPT, NA_WIDTH), BF16),
        grid=(BATCH,),
        in_specs=[pl.BlockSpec((SEQ, NA_WIDTH), lambda b: (b, 0)),
                  pl.BlockSpec((SEQ, NA_WIDTH), lambda b: (b, 1)),
                  pl.BlockSpec((SEQ, NA_WIDTH), lambda b: (b, 2))],
        out_specs=pl.BlockSpec((SEQ, NA_WIDTH), lambda b: (b, 0)),
        compiler_params=_params(), name="attn_prompt",
    )(h, h, h)


_NA_ROWS = DEC_SEQ // GRID_W
_NA_LOC = NA_KH * GRID_W


def _na_window_start(r):
    return jnp.clip(r - NA_KH // 2, 0, _NA_ROWS - NA_KH)


def _attn_sample_kernel(q_ref, k_ref, v_ref, ck_ref, cv_ref, bias_ref, o_ref):
    s0 = pl.multiple_of(_na_window_start(pl.program_id(1)) * GRID_W, GRID_W)
    scale = NA_HEAD_DIM ** -0.5
    for p in range(NA_HEADS // 2):
        cols = slice(p * 128, (p + 1) * 128)
        qp = q_ref[:, cols] * scale
        kl = k_ref[pl.ds(s0, _NA_LOC), cols].astype(BF16)
        vl = v_ref[pl.ds(s0, _NA_LOC), cols].astype(BF16)
        kc = ck_ref[0, :, cols].astype(BF16)
        vc = cv_ref[0, :, cols].astype(BF16)
        outs = []
        for hh in range(2):
            qh = jnp.where(_pair_mask(hh), qp, 0.0).astype(BF16)
            sl = _dot_nt(qh, kl) + bias_ref[0, 2 * p + hh]
            sc = _dot_nt(qh, kc)
            mx = jnp.maximum(jnp.max(sl, axis=-1, keepdims=True),
                             jnp.max(sc, axis=-1, keepdims=True))
            el = jnp.exp(sl - mx)
            ec = jnp.exp(sc - mx)
            den = jnp.sum(el, axis=-1, keepdims=True) + jnp.sum(ec, axis=-1, keepdims=True)
            outs.append((_dot(el.astype(BF16), vl) + _dot(ec.astype(BF16), vc)) / den)
        o_ref[:, cols] = jnp.where(_pair_mask(0), outs[0], outs[1]).astype(o_ref.dtype)


def _attn_sample(h, ck, cv, bias):
    q_row0 = T_PROMPT // GRID_W
    kv_row0 = T_PROMPT // DEC_SEQ
    return pl.pallas_call(
        _attn_sample_kernel,
        out_shape=jax.ShapeDtypeStruct((T_SAMPLE, NA_WIDTH), BF16),
        grid=(DEC_BATCH, _NA_ROWS),
        in_specs=[pl.BlockSpec((GRID_W, NA_WIDTH), lambda b, r: (q_row0 + b * _NA_ROWS + r, 0)),
                  pl.BlockSpec((DEC_SEQ, NA_WIDTH), lambda b, r: (kv_row0 + b, 1)),
                  pl.BlockSpec((DEC_SEQ, NA_WIDTH), lambda b, r: (kv_row0 + b, 2)),
                  pl.BlockSpec((1, PAST_LEN, NA_WIDTH), lambda b, r: (b, 0, 0)),
                  pl.BlockSpec((1, PAST_LEN, NA_WIDTH), lambda b, r: (b, 0, 0)),
                  pl.BlockSpec((1, NA_HEADS, GRID_W, _NA_LOC), lambda b, r: (_na_window_start(r) - r + NA_KH - 1, 0, 0, 0))],
        out_specs=pl.BlockSpec((GRID_W, NA_WIDTH), lambda b, r: (b * _NA_ROWS + r, 0)),
        compiler_params=_params(2), name="attn_sample",
    )(h, h, h, ck, cv, bias)


_NA_DR = 2 * NA_KH - 1
_NA_DC = 2 * NA_KW - 1
_NA_DC_PAD = 32


def _bias_expand_kernel(rb_ref, onehot_ref, inwin_ref, o_ref):
    hi, mid, lo = _split3(rb_ref[...])
    oh = onehot_ref[...]
    e = _dot(hi, oh) + _dot(mid, oh) + _dot(lo, oh)
    o_ref[...] = jnp.where(inwin_ref[...] != 0.0, e, NEG_INF)


def _na_bias_tables(rel_bias):
    qc = np.arange(GRID_W)
    kc = np.arange(GRID_W)
    q_start = np.clip(qc - NA_KW // 2, 0, GRID_W - NA_KW)
    in_win = (kc[None, :] >= q_start[:, None]) & (kc[None, :] < q_start[:, None] + NA_KW)
    dc = np.clip(kc[None, :] - qc[:, None] + NA_KW - 1, 0, _NA_DC - 1)
    onehot = (np.arange(_NA_DC_PAD)[:, None] == dc.reshape(1, -1)).astype(np.float32)
    n_rows = DEPTH * NA_HEADS * _NA_DR
    rb = jnp.pad(rel_bias.astype(F32).reshape(n_rows, _NA_DC), ((0, 0), (0, _NA_DC_PAD - _NA_DC)))
    full = lambda shape: pl.BlockSpec(shape, lambda i: (0, 0))
    e = pl.pallas_call(
        _bias_expand_kernel,
        out_shape=jax.ShapeDtypeStruct((n_rows, GRID_W * GRID_W), F32), grid=(1,),
        in_specs=[full((n_rows, _NA_DC_PAD)), full((_NA_DC_PAD, GRID_W * GRID_W)), full((1, GRID_W * GRID_W))],
        out_specs=full((n_rows, GRID_W * GRID_W)),
        compiler_params=_params(), name="na_bias_expand",
    )(rb, jnp.asarray(onehot, BF16), jnp.asarray(in_win.reshape(1, -1), F32))
    e = e.reshape(DEPTH, NA_HEADS, _NA_DR, GRID_W, GRID_W)
    w = jnp.stack([e[:, :, b:b + NA_KH] for b in range(NA_KH)], axis=1)
    return w.transpose(0, 1, 2, 4, 3, 5).reshape(DEPTH, NA_KH, NA_HEADS, GRID_W, _NA_LOC)


def _hgrn_kernel(*refs, n_tok, has_state):
    if has_state:
        (q_ref, zf_ref, zb_ref, v_ref, g_ref, lbf_ref, lbb_ref, og_ref, s0f_ref, s0b_ref,
         rec_ref, sf_ref, sb_ref, kf_s, bf_s, kb_s, bb_s, o_s, z_s, st_s) = refs
    else:
        (q_ref, zf_ref, zb_ref, v_ref, g_ref, lbf_ref, lbb_ref, og_ref,
         rec_ref, sf_ref, sb_ref, kf_s, bf_s, kb_s, bb_s, o_s, z_s, st_s) = refs
        s0f_ref = s0b_ref = None
    C = HG_CHUNK
    W = HG_WIDTH
    n_chunks = n_tok // C
    rr = lax.broadcasted_iota(jnp.int32, (W, W), 0)
    cc = lax.broadcasted_iota(jnp.int32, (W, W), 1)
    same_chunk = jnp.right_shift(rr, 5) == jnp.right_shift(cc, 5)
    tri_prefix = jnp.where(same_chunk & (cc <= rr), 1.0, 0.0).astype(BF16)
    tri_suffix = jnp.where(same_chunk & (cc >= rr), 1.0, 0.0).astype(BF16)
    same_head = jnp.right_shift(rr, 6) == jnp.right_shift(cc, 6)
    head_ones = jnp.where(same_head, 1.0, 0.0).astype(BF16)

    for ti in range(n_tok // W):
        rows = slice(ti * W, (ti + 1) * W)
        for z_ref, lb_ref, k_s, b_s, tri in ((zf_ref, lbf_ref, kf_s, bf_s, tri_prefix),
                                             (zb_ref, lbb_ref, kb_s, bb_s, tri_suffix)):
            z = z_ref[rows, :]
            lb = lb_ref[...]
            f = lb + (1.0 - lb) * jax.nn.sigmoid(z)
            logf = jnp.log(jnp.maximum(f, F_FLOOR))
            k_s[rows, :] = (1.0 - lb) * jax.nn.sigmoid(-z)
            hi, mid, lo = _split3(logf)
            b_s[rows, :] = _dot(tri, hi) + _dot(tri, mid) + _dot(tri, lo)

    srow = lax.broadcasted_iota(jnp.int32, (C, W), 0)

    def scan_direction(k_s, b_s, fwd):
        def chunk(ci, carry):
            c = ci if fwd else n_chunks - 1 - ci
            base = pl.multiple_of(c * C, C)
            q = q_ref[pl.ds(base, C), :]
            k = k_s[pl.ds(base, C), :]
            b = b_s[pl.ds(base, C), :]
            v = v_ref[pl.ds(base, C), :]
            for t in range(C):
                qt = q_ref[pl.ds(base + t, 1), :]
                bt = b_s[pl.ds(base + t, 1), :]
                keep = (srow <= t) if fwd else (srow >= t)
                zt = jnp.where(keep, (qt * k) * jnp.exp(bt - b), 0.0)
                z_s[t * C:(t + 1) * C, :] = zt.astype(BF16)
            a_rep = _dot(z_s[...], head_ones)
            o_intra = jnp.sum(a_rep.reshape(C, C, W) * v[None, :, :], axis=1)
            b_end = b_s[pl.ds(base + (C - 1 if fwd else 0), 1), :]
            q_in = q * jnp.exp(b)
            k_st = k * jnp.exp(b_end - b)
            st = st_s[...]
            o_inter = _dot_nt(q_in.astype(BF16), st.astype(BF16))
            upd = _dot_tn(v.astype(BF16), k_st.astype(BF16))
            st_s[...] = st * jnp.exp(b_end) + jnp.where(same_head, upd, 0.0)
            o = o_intra + o_inter
            if fwd:
                o_s[pl.ds(base, C), :] = o
            else:
                o_s[pl.ds(base, C), :] = o_s[pl.ds(base, C), :] + o
            return carry
        lax.fori_loop(0, n_chunks, chunk, 0)

    st_s[...] = s0f_ref[0] if has_state else jnp.zeros((W, W), F32)
    scan_direction(kf_s, bf_s, True)
    sf_ref[0] = st_s[...]
    st_s[...] = s0b_ref[0] if has_state else jnp.zeros((W, W), F32)
    scan_direction(kb_s, bb_s, False)
    sb_ref[0] = st_s[...]

    for ti in range(n_tok // W):
        rows = slice(ti * W, (ti + 1) * W)
        o = o_s[rows, :]
        sq = o * o
        sq_hi = sq.astype(BF16)
        sq_lo = (sq - sq_hi.astype(F32)).astype(BF16)
        ms = (_dot(sq_hi, head_ones) + _dot(sq_lo, head_ones)) * (1.0 / HG_DV)
        g = g_ref[rows, :]
        y = o * lax.rsqrt(ms + RMS_EPS) * og_ref[...] * (g * jax.nn.sigmoid(g))
        rec_ref[rows, :] = y.astype(rec_ref.dtype)


def _hgrn(h, lbf, lbb, og, s0f_t, s0b_t, n_tok, n_seq, row0):
    W = HG_WIDTH
    has_state = s0f_t is not None

    def col(cb):
        return pl.BlockSpec((n_tok, W), lambda i, cb=cb: (row0 + i, cb))

    vec = pl.BlockSpec((1, W), lambda i: (0, 0))
    st_spec = pl.BlockSpec((1, W, W), lambda i: (i, 0, 0))
    in_specs = [col(_CB_HQ), col(_CB_ZF), col(_CB_ZB), col(_CB_HI), col(_CB_HG), vec, vec, vec]
    args = [h, h, h, h, h, lbf, lbb, og]
    if has_state:
        in_specs += [st_spec, st_spec]
        args += [s0f_t, s0b_t]
    seq_f32 = pltpu.VMEM((n_tok, W), F32)
    return pl.pallas_call(
        functools.partial(_hgrn_kernel, n_tok=n_tok, has_state=has_state),
        out_shape=(jax.ShapeDtypeStruct((n_seq * n_tok, W), BF16),
                   jax.ShapeDtypeStruct((n_seq, W, W), F32),
                   jax.ShapeDtypeStruct((n_seq, W, W), F32)),
        grid=(n_seq,),
        in_specs=in_specs,
        out_specs=(pl.BlockSpec((n_tok, W), lambda i: (i, 0)), st_spec, st_spec),
        scratch_shapes=[seq_f32, seq_f32, seq_f32, seq_f32, seq_f32,
                        pltpu.VMEM((HG_CHUNK * HG_CHUNK, W), BF16),
                        pltpu.VMEM((W, W), F32)],
        compiler_params=_params(), name="hgrn_state" if has_state else "hgrn_zero",
    )(*args)


def _state_to_blockdiag_t(s):
    eye = jnp.eye(HG_HEADS, dtype=s.dtype)
    return jnp.einsum('bhkv,hg->bhvgk', s, eye).reshape(s.shape[0], HG_WIDTH, HG_HEADS * HG_DK)


def _blockdiag_t_to_state(st):
    s5 = st.reshape(st.shape[0], HG_HEADS, HG_DV, HG_HEADS, HG_DK)
    return jnp.stack([s5[:, hh, :, hh, :] for hh in range(HG_HEADS)], axis=1).transpose(0, 1, 3, 2)


def _gmlp_kernel(u_ref, v_ref, g_ref, ws_ref, b_ref, o_ref):
    v = v_ref[...]
    ms = jnp.mean(v * v, axis=-1, keepdims=True)
    vn = (v * lax.rsqrt(ms + RMS_EPS) * g_ref[...]).astype(BF16)
    lane = lax.broadcasted_iota(jnp.int32, (1, GM_WIDTH), 1)
    z = b_ref[...]
    for gi in range(GM_GROUPS):
        zg = _dot(ws_ref[gi], vn)
        in_group = (lane >= gi * GM_GDIM) & (lane < (gi + 1) * GM_GDIM)
        z = z + jnp.where(in_group, zg, 0.0)
    o_ref[...] = (u_ref[...] * z).astype(o_ref.dtype)


def _gmlp(h, vnorm_g, ws_bf16, bias_full):
    W = GM_WIDTH
    return pl.pallas_call(
        _gmlp_kernel,
        out_shape=jax.ShapeDtypeStruct((T_ALL, W), BF16),
        grid=(T_ALL // GM_CHUNK,),
        in_specs=[pl.BlockSpec((GM_CHUNK, W), lambda i: (i, _CB_GU)),
                  pl.BlockSpec((GM_CHUNK, W), lambda i: (i, _CB_GV)),
                  pl.BlockSpec((1, W), lambda i: (0, 0)),
                  pl.BlockSpec((GM_GROUPS, GM_CHUNK, GM_CHUNK), lambda i: (0, 0, 0)),
                  pl.BlockSpec((GM_CHUNK, W), lambda i: (0, 0))],
        out_specs=pl.BlockSpec((GM_CHUNK, W), lambda i: (i, 0)),
        compiler_params=_params(), name="gmlp",
    )(h, h, vnorm_g, ws_bf16, bias_full)


def _outproj_kernel(att_ref, rec_ref, mlp_ref, x_ref, mod_ref, g_ref, w_ref, wr_ref, br_ref,
                    x1_ref, h2a_ref, h2b_ref, idx_ref, gate_ref, cnt_ref):
    @pl.when(pl.program_id(0) == 0)
    def _():
        cnt_ref[...] = jnp.zeros_like(cnt_ref)

    out = (_dot(att_ref[...], w_ref[0:NA_WIDTH, :])
           + _dot(rec_ref[...], w_ref[NA_WIDTH:NA_WIDTH + HG_WIDTH, :])
           + _dot(mlp_ref[...], w_ref[NA_WIDTH + HG_WIDTH:, :]))
    x1 = x_ref[...] + mod_ref[0, 2:3, :] * out
    x1_ref[...] = x1
    h2 = _rms_mod(x1, g_ref[...], mod_ref[0, 3:4, :], mod_ref[0, 4:5, :])
    _store_slabs((h2a_ref, h2b_ref), _pack_halves(h2))
    h_hi = h2.astype(BF16)
    h_lo = (h2 - h_hi.astype(F32)).astype(BF16)
    wr = wr_ref[...]
    w_hi = wr.astype(BF16)
    w_lo = (wr - w_hi.astype(F32)).astype(BF16)
    logits = _dot(h_hi, w_hi) + _dot(h_lo, w_hi) + _dot(h_hi, w_lo) + br_ref[...]
    lane_e = lax.broadcasted_iota(jnp.int32, (TM, N_EXPERTS), 1).astype(F32)
    lane_o = lax.broadcasted_iota(jnp.int32, (TM, _RT_LANES), 1)
    idx_acc = jnp.zeros((TM, _RT_LANES), F32)
    val_acc = jnp.zeros((TM, _RT_LANES), F32)
    top0 = None
    den = jnp.zeros((TM, 1), F32)
    work = logits
    picks = []
    for kk in range(TOP_K):
        m = jnp.max(work, axis=-1, keepdims=True)
        first = jnp.min(jnp.where(work == m, lane_e, float(N_EXPERTS)), axis=-1, keepdims=True)
        if kk == 0:
            top0 = m
        e = jnp.exp(m - top0)
        den = den + e
        idx_acc = jnp.where(lane_o == kk, first, idx_acc)
        val_acc = jnp.where(lane_o == kk, e, val_acc)
        picks.append(lane_e == first)
        work = jnp.where(picks[-1], -jnp.inf, work)
    gate_ref[...] = val_acc / den
    sel = jnp.zeros((TM, N_EXPERTS), F32)
    for pk in picks:
        sel = sel + jnp.where(pk, 1.0, 0.0)
    rr = lax.broadcasted_iota(jnp.int32, (TM, TM), 0)
    cc = lax.broadcasted_iota(jnp.int32, (TM, TM), 1)
    earlier = jnp.where(cc < rr, 1.0, 0.0).astype(BF16)
    seen = cnt_ref[0:1, 0:N_EXPERTS]
    before = _dot(earlier, sel.astype(BF16)) + seen
    for kk, pk in enumerate(picks):
        rank = jnp.sum(jnp.where(pk, before, 0.0), axis=-1, keepdims=True)
        idx_acc = jnp.where(lane_o == TOP_K + kk, rank, idx_acc)
    idx_ref[...] = idx_acc.astype(jnp.int32)
    cnt_ref[0:1, 0:N_EXPERTS] = seen + jnp.sum(sel, axis=0, keepdims=True)


def _outproj(att, rec, mlp, x, mod, g, w_bf16, wr, br):
    def tile(width):
        return pl.BlockSpec((TM, width), lambda i: (i, 0))

    return pl.pallas_call(
        _outproj_kernel,
        out_shape=(jax.ShapeDtypeStruct((T_ALL, D_MODEL), F32),
                   jax.ShapeDtypeStruct((T_ALL, D_SLAB), jnp.int32),
                   jax.ShapeDtypeStruct((T_ALL, D_SLAB), jnp.int32),
                   jax.ShapeDtypeStruct((T_ALL, _RT_LANES), jnp.int32),
                   jax.ShapeDtypeStruct((T_ALL, _RT_LANES), F32),
                   jax.ShapeDtypeStruct((8, _RT_LANES), F32)),
        grid=(N_TILES,),
        in_specs=[tile(NA_WIDTH), tile(HG_WIDTH), tile(GM_WIDTH), _TILE_SPEC, _MOD_SPEC, _ROW_SPEC,
                  pl.BlockSpec((D_MODEL, D_MODEL), lambda i: (0, 0)),
                  pl.BlockSpec((D_MODEL, N_EXPERTS), lambda i: (0, 0)),
                  pl.BlockSpec((1, N_EXPERTS), lambda i: (0, 0))],
        out_specs=(_TILE_SPEC, tile(D_SLAB), tile(D_SLAB), tile(_RT_LANES), tile(_RT_LANES),
                   pl.BlockSpec((8, _RT_LANES), lambda i: (0, 0))),
        compiler_params=_params(), name="outproj_router",
    )(att, rec, mlp, x, mod, g, w_bf16, wr, br)


_W_CAST_ROWS = 128


def _moe_kernel(blk_e_ref, blk_on_ref, blk_new_ref, xa_ref, xb_ref, wg_ref, bg_ref, wu_ref, bu_ref, wd_ref, bd_ref,
                ya_ref, yb_ref, wg_s, wu_s, wd_s):
    j = pl.program_id(0)

    @pl.when(blk_new_ref[j] != 0)
    def _():
        def cast_rows(ci, carry):
            rows = pl.ds(pl.multiple_of(ci * _W_CAST_ROWS, _W_CAST_ROWS), _W_CAST_ROWS)
            wg_s[rows, :] = wg_ref[0, rows, :].astype(BF16)
            wu_s[rows, :] = wu_ref[0, rows, :].astype(BF16)
            wd_s[rows, :] = wd_ref[0, rows, :].astype(BF16)
            return carry
        lax.fori_loop(0, D_MODEL // _W_CAST_ROWS, cast_rows, 0)

    @pl.when(blk_on_ref[j] != 0)
    def _():
        lo, hi = _unpack_halves(_load_slabs((xa_ref, xb_ref)))
        x = jnp.concatenate([lo.astype(BF16), hi.astype(BF16)], axis=1)
        gate = jnp.minimum(_dot(x, wg_s[...]) + bg_ref[0], SWIGLU_LIMIT)
        up = jnp.clip(_dot(x, wu_s[...]) + bu_ref[0], -SWIGLU_LIMIT, SWIGLU_LIMIT)
        glu = gate * jax.nn.sigmoid(SWIGLU_ALPHA * gate)
        act = ((up + 1.0) * glu).astype(BF16)
        _store_slabs((ya_ref, yb_ref), _pack_halves(_dot(act, wd_s[...]) + bd_ref[0]))

    @pl.when(blk_on_ref[j] == 0)
    def _():
        ya_ref[...] = jnp.zeros_like(ya_ref)
        yb_ref[...] = jnp.zeros_like(yb_ref)


def _moe(blk_e, blk_on, blk_new, x_sorted, wg, bg, wu, bu, wd, bd):
    w_spec = pl.BlockSpec((1, D_MODEL, D_MODEL), lambda j, be, on, nw: (be[j], 0, 0))
    b_spec = pl.BlockSpec((1, 1, D_MODEL), lambda j, be, on, nw: (be[j], 0, 0))
    x_spec = pl.BlockSpec((MOE_BM, D_SLAB), lambda j, be, on, nw: (j, 0))
    w_bf16 = pltpu.VMEM((D_MODEL, D_MODEL), BF16)
    return pl.pallas_call(
        _moe_kernel,
        out_shape=(jax.ShapeDtypeStruct((MOE_SLOTS, D_SLAB), jnp.int32),) * N_SPLIT,
        grid_spec=pltpu.PrefetchScalarGridSpec(
            num_scalar_prefetch=3, grid=(MOE_BLOCKS,),
            in_specs=[x_spec, x_spec, w_spec, b_spec, w_spec, b_spec, w_spec, b_spec],
            out_specs=(x_spec, x_spec),
            scratch_shapes=[w_bf16, w_bf16, w_bf16]),
        compiler_params=_params(), name="moe_experts",
    )(blk_e, blk_on, blk_new, *x_sorted, wg, bg, wu, bu, wd, bd)


def _route(top_i, rank, counts):
    experts = jnp.arange(N_EXPERTS, dtype=jnp.int32)
    padded = (counts + MOE_BM - 1) // MOE_BM * MOE_BM
    pad_end = jnp.cumsum(padded)
    pad_start = pad_end - padded
    start_of = jnp.sum(jnp.where(top_i[..., None] == experts, pad_start, 0), axis=-1)
    dest = start_of + rank
    blk_first = jnp.arange(MOE_BLOCKS, dtype=jnp.int32) * MOE_BM
    blk_e = jnp.minimum(jnp.sum((pad_end[None, :] <= blk_first[:, None]).astype(jnp.int32), axis=1),
                        N_EXPERTS - 1)
    blk_on = (blk_first < pad_end[-1]).astype(jnp.int32)
    last_e = jnp.max(jnp.where(counts > 0, experts, 0))
    blk_e = jnp.where(blk_on != 0, blk_e, last_e).astype(jnp.int32)
    blk_new = jnp.concatenate([jnp.ones((1,), jnp.int32), (blk_e[1:] != blk_e[:-1]).astype(jnp.int32)])
    return dest.astype(jnp.int32), blk_e, blk_on, blk_new


_SC_WINDOW = 128


def _sc_mesh():
    return plsc.VectorSubcoreMesh(core_axis_name="core", subcore_axis_name="subcore")


def _sc_scatter_rows(srcs, idx, n_out):
    n_src, width = srcs[0].shape
    n_idx = idx.shape[1]
    src_windows = n_src // _SC_WINDOW

    def body(*refs):
        x_hbm = refs[:len(srcs)]
        i_hbm = refs[len(srcs)]
        o_hbm = refs[len(srcs) + 1:]
        for xs, os_ in zip(x_hbm, o_hbm):
            def step(x_vmem, i_vmem, os_=os_):
                pltpu.sync_copy(x_vmem, os_.at[i_vmem.at[0]])

            pltpu.emit_pipeline(
                step, grid=(n_idx // _SC_WINDOW,),
                in_specs=[pl.BlockSpec((_SC_WINDOW, width), lambda i: (i % src_windows, 0)),
                          pl.BlockSpec((1, _SC_WINDOW), lambda i: (0, i))],
                out_specs=[],
                core_axis_name=("core", "subcore"),
                dimension_semantics=(pltpu.PARALLEL,),
            )(xs, i_hbm)

    out_type = tuple(jax.ShapeDtypeStruct((n_out, width), s.dtype) for s in srcs)
    return pl.kernel(body, out_type=out_type, mesh=_sc_mesh(), scratch_types=[],
                     name="sc_scatter_rows")(*srcs, idx)


def _sc_gather_rows(tables, idx):
    n_idx = idx.shape[1]
    width = tables[0].shape[1]

    def body(*refs):
        t_hbm = refs[:len(tables)]
        i_hbm = refs[len(tables)]
        o_hbm = refs[len(tables) + 1:]
        for ts, os_ in zip(t_hbm, o_hbm):
            def step(i_vmem, o_vmem, ts=ts):
                pltpu.sync_copy(ts.at[i_vmem.at[0]], o_vmem)

            pltpu.emit_pipeline(
                step, grid=(n_idx // _SC_WINDOW,),
                in_specs=[pl.BlockSpec((1, _SC_WINDOW), lambda i: (0, i))],
                out_specs=[pl.BlockSpec((_SC_WINDOW, width), lambda i: (i, 0))],
                core_axis_name=("core", "subcore"),
                dimension_semantics=(pltpu.PARALLEL,),
            )(i_hbm, os_)

    out_type = tuple(jax.ShapeDtypeStruct((n_idx, width), t.dtype) for t in tables)
    return pl.kernel(body, out_type=out_type, mesh=_sc_mesh(), scratch_types=[],
                     name="sc_gather_rows")(*tables, idx)


def _final_kernel(x_ref, yga_ref, ygb_ref, gate_ref, mod_ref, g_ref, o_ref):
    x = x_ref[...] + mod_ref[0, 5:6, :] * _combine_experts((yga_ref, ygb_ref), gate_ref)
    ms = jnp.mean(x * x, axis=-1, keepdims=True)
    o_ref[...] = x * lax.rsqrt(ms + RMS_EPS) * g_ref[...]


def _final(x, moe, mod, g):
    return pl.pallas_call(
        _final_kernel, out_shape=jax.ShapeDtypeStruct((T_ALL, D_MODEL), F32), grid=(N_TILES,),
        in_specs=[_TILE_SPEC, _YG_SPEC, _YG_SPEC, _GATE_SPEC, _MOD_SPEC, _ROW_SPEC], out_specs=_TILE_SPEC,
        compiler_params=_params(), name="final_norm",
    )(x, *moe[0], moe[1], mod, g)


def kernel(x_prompt, x_sample, cache_k, cache_v, state_hgrn_fwd, state_hgrn_bwd, c, c_ctx, w_mod, b_mod, norm1_g, norm2_g, w_in, na_rel_bias, hgrn_lb, hgrn_onorm_g, gmlp_vnorm_g, gmlp_ws, gmlp_b, w_out, router_w, router_b, w_gate, b_gate, w_up, b_up, w_down, b_down, final_g):
    x = jnp.concatenate([x_prompt.reshape(T_PROMPT, D_MODEL), x_sample.reshape(T_SAMPLE, D_MODEL)], axis=0)

    cond = jnp.zeros((MOD_ROWS, D_MODEL), F32).at[0].set(c_ctx).at[1:1 + DEC_BATCH].set(c)
    mod = _modulation(cond, w_mod, b_mod)
    tile_row = np.concatenate([np.zeros(P_TILES, np.int32),
                               1 + np.arange(N_TILES - P_TILES, dtype=np.int32) // (DEC_SEQ // TM)])
    mod_tiles = mod[:, tile_row].reshape(DEPTH, N_TILES, 6, D_MODEL)
    mod_tiles = jnp.pad(mod_tiles, ((0, 0), (0, 0), (0, MOD_ROWS - 6), (0, 0)))

    lb_soft = jax.nn.softmax(hgrn_lb.astype(F32), axis=1)
    lower = jnp.cumsum(lb_soft, axis=1) - lb_soft[:, :1]

    na_bias = _na_bias_tables(na_rel_bias)

    k_list, v_list, sf_list, sb_list = [], [], [], []
    moe_out = None
    for l in range(DEPTH):
        h, x, (kc, vc) = _inproj(x, moe_out, mod_tiles[l - 1] if l else None, mod_tiles[l],
                                 norm1_g[l][None, :], w_in[l].astype(BF16))
        k_list.append(kc.reshape(BATCH, SEQ, NA_HEADS, NA_HEAD_DIM))
        v_list.append(vc.reshape(BATCH, SEQ, NA_HEADS, NA_HEAD_DIM))

        att_p = _attn_prompt(h)
        att_s = _attn_sample(h, cache_k[:, l].reshape(DEC_BATCH, PAST_LEN, NA_WIDTH),
                             cache_v[:, l].reshape(DEC_BATCH, PAST_LEN, NA_WIDTH), na_bias[l])
        lbf = lower[0, l][None, :]
        lbb = lower[1, l][None, :]
        og = jnp.tile(hgrn_onorm_g[l], HG_HEADS)[None, :]
        rec_p, sf_t, sb_t = _hgrn(h, lbf, lbb, og, None, None, SEQ, BATCH, 0)
        rec_s, _, _ = _hgrn(h, lbf, lbb, og, _state_to_blockdiag_t(state_hgrn_fwd[:, l].astype(F32)),
                            _state_to_blockdiag_t(state_hgrn_bwd[:, l].astype(F32)),
                            DEC_SEQ, DEC_BATCH, T_PROMPT // DEC_SEQ)
        sf_list.append(_blockdiag_t_to_state(sf_t))
        sb_list.append(_blockdiag_t_to_state(sb_t))
        gm_bias = jnp.repeat(gmlp_b[l].T, GM_GDIM, axis=1)
        mlp = _gmlp(h, gmlp_vnorm_g[l][None, :], gmlp_ws[l].astype(BF16), gm_bias)

        att = jnp.concatenate([att_p, att_s], axis=0)
        rec = jnp.concatenate([rec_p, rec_s], axis=0)
        x, h2a, h2b, idx_pad, gate_pad, cnt = _outproj(att, rec, mlp, x, mod_tiles[l], norm2_g[l][None, :],
                                                 w_out[l].astype(BF16), router_w[l], router_b[l][None, :])
        dest, blk_e, blk_on, blk_new = _route(idx_pad[:, :TOP_K], idx_pad[:, TOP_K:2 * TOP_K],
                                              cnt[0, :N_EXPERTS].astype(jnp.int32))
        dest_flat = dest.T.reshape(1, TOP_K * T_ALL)
        x_sorted = _sc_scatter_rows((h2a, h2b), dest_flat, MOE_SLOTS)
        y_sorted = _moe(blk_e, blk_on, blk_new, x_sorted,
                        w_gate[l], b_gate[l][:, None, :], w_up[l], b_up[l][:, None, :],
                        w_down[l], b_down[l][:, None, :])
        y_tok = _sc_gather_rows(y_sorted, dest_flat)
        moe_out = ([yt.reshape(TOP_K, T_ALL, D_SLAB) for yt in y_tok], gate_pad)

    y = _final(x, moe_out, mod_tiles[DEPTH - 1], final_g[None, :])
    y_prompt = y[:T_PROMPT].reshape(BATCH, SEQ, D_MODEL)
    y_sample = y[T_PROMPT:].reshape(DEC_BATCH, DEC_SEQ, D_MODEL)
    return (y_prompt, y_sample, jnp.stack(k_list, axis=1), jnp.stack(v_list, axis=1),
            jnp.stack(sf_list, axis=1), jnp.stack(sb_list, axis=1))
```

```python
import functools

import numpy as np
import jax
import jax.numpy as jnp
from jax import lax
from jax.experimental import pallas as pl
from jax.experimental.pallas import tpu as pltpu
from jax.experimental.pallas import tpu_sc as plsc

F32 = jnp.float32
BF16 = jnp.bfloat16

D_MODEL = 1024
BATCH = 32
SEQ = 256
DEPTH = 2
DEC_BATCH = 2
DEC_SEQ = 1024
PAST_LEN = 512
GRID_W = 64
NA_HEADS = 8
NA_HEAD_DIM = 64
NA_WIDTH = NA_HEADS * NA_HEAD_DIM
NA_KH = 8
NA_KW = 16
HG_HEADS = 4
HG_DK = 64
HG_DV = 64
HG_WIDTH = HG_HEADS * HG_DV
HG_CHUNK = 32
F_FLOOR = 1e-30
GM_GROUPS = 4
GM_GDIM = 64
GM_WIDTH = GM_GROUPS * GM_GDIM
GM_CHUNK = 128
IN_COLS = 3 * NA_WIDTH + 5 * HG_WIDTH + 2 * GM_WIDTH
N_EXPERTS = 32
TOP_K = 4
SWIGLU_LIMIT = 7.0
SWIGLU_ALPHA = 1.702
RMS_EPS = 1e-6
NEG_INF = -1e30

T_PROMPT = BATCH * SEQ
T_SAMPLE = DEC_BATCH * DEC_SEQ
T_ALL = T_PROMPT + T_SAMPLE
TM = 256
N_TILES = T_ALL // TM
P_TILES = T_PROMPT // TM
MOE_BM = 256
MOE_SLOTS = -(-(T_ALL * TOP_K + N_EXPERTS * (MOE_BM - 1)) // MOE_BM) * MOE_BM
MOD_ROWS = 8
V7X_VMEM_LIMIT = 48 * 1024 * 1024

_CB_HQ, _CB_ZF, _CB_ZB, _CB_HI, _CB_HG, _CB_GU, _CB_GV = 6, 7, 8, 9, 10, 11, 12


def _dot(a, b):
    return jnp.dot(a, b, preferred_element_type=F32)


def _dot_nt(a, b):
    return lax.dot_general(a, b, (((1,), (1,)), ((), ())), preferred_element_type=F32)


def _dot_tn(a, b):
    return lax.dot_general(a, b, (((0,), (0,)), ((), ())), preferred_element_type=F32)


def _split3(x):
    hi = x.astype(BF16)
    r1 = x - hi.astype(F32)
    mid = r1.astype(BF16)
    lo = (r1 - mid.astype(F32)).astype(BF16)
    return hi, mid, lo


D_PACK = D_MODEL // 2
N_SPLIT = 2
D_SLAB = D_PACK // N_SPLIT


def _pack_halves(x):
    half = x.shape[1] // 2
    lo = pltpu.bitcast(x[:, :half].astype(BF16).astype(F32), jnp.uint32)
    hi = pltpu.bitcast(x[:, half:].astype(BF16).astype(F32), jnp.uint32)
    return pltpu.bitcast(jnp.right_shift(lo, jnp.uint32(16)) | hi, jnp.int32)


def _unpack_halves(w):
    u = pltpu.bitcast(w, jnp.uint32)
    lo = pltpu.bitcast(jnp.left_shift(u, jnp.uint32(16)), F32)
    hi = pltpu.bitcast(u & jnp.uint32(0xFFFF0000), F32)
    return lo, hi


def _load_slabs(refs, *lead):
    return jnp.concatenate([r[lead] if lead else r[...] for r in refs], axis=1)


def _store_slabs(refs, packed):
    for si, r in enumerate(refs):
        r[...] = packed[:, si * D_SLAB:(si + 1) * D_SLAB]


def _params(n_axes=1):
    return pltpu.CompilerParams(dimension_semantics=("arbitrary",) * n_axes,
                                vmem_limit_bytes=V7X_VMEM_LIMIT)


def _mod_kernel(cond_ref, w_ref, b_ref, o_ref):
    c = cond_ref[...]
    c = c * jax.nn.sigmoid(c)
    w = w_ref[0]
    c_hi = c.astype(BF16)
    c_lo = (c - c_hi.astype(F32)).astype(BF16)
    w_hi = w.astype(BF16)
    w_lo = (w - w_hi.astype(F32)).astype(BF16)
    o_ref[0] = _dot(c_hi, w_hi) + _dot(c_lo, w_hi) + _dot(c_hi, w_lo) + b_ref[0]


def _modulation(cond, w_mod, b_mod):
    tn = 1536
    return pl.pallas_call(
        _mod_kernel,
        out_shape=jax.ShapeDtypeStruct((DEPTH, MOD_ROWS, 6 * D_MODEL), F32),
        grid=(DEPTH, 6 * D_MODEL // tn),
        in_specs=[pl.BlockSpec((MOD_ROWS, D_MODEL), lambda l, j: (0, 0)),
                  pl.BlockSpec((1, D_MODEL, tn), lambda l, j: (l, 0, j)),
                  pl.BlockSpec((1, 1, tn), lambda l, j: (l, 0, j))],
        out_specs=pl.BlockSpec((1, MOD_ROWS, tn), lambda l, j: (l, 0, j)),
        compiler_params=_params(2),
        name="modulation",
    )(cond, w_mod, b_mod.reshape(DEPTH, 1, 6 * D_MODEL))


def _rms_mod(x, g, shift, scale):
    ms = jnp.mean(x * x, axis=-1, keepdims=True)
    y = x * lax.rsqrt(ms + RMS_EPS) * g
    return y * (1.0 + scale) + shift


def _project_in(hm, w_ref, h_ref, kc_ref, vc_ref):
    h = _dot(hm.astype(BF16), w_ref[...])
    h_ref[...] = h

    @pl.when(pl.program_id(0) < P_TILES)
    def _():
        kc_ref[0] = h[:, NA_WIDTH:2 * NA_WIDTH]
        vc_ref[0] = h[:, 2 * NA_WIDTH:3 * NA_WIDTH]


def _pick_group(p_ref, s_ref):
    return jnp.where(pl.program_id(0) < P_TILES, p_ref[...], s_ref[...])


def _p_tile(width):
    return pl.BlockSpec((TM, width), lambda i: (jnp.minimum(i, P_TILES - 1), 0))


def _s_tile(width):
    return pl.BlockSpec((TM, width), lambda i: (jnp.maximum(i - P_TILES, 0), 0))


def _inproj_first_kernel(xp_ref, xs_ref, mod_ref, g_ref, w_ref, h_ref, xo_ref, kc_ref, vc_ref):
    x = _pick_group(xp_ref, xs_ref)
    xo_ref[...] = x
    hm = _rms_mod(x, g_ref[...], mod_ref[0, 0:1, :], mod_ref[0, 1:2, :])
    _project_in(hm, w_ref, h_ref, kc_ref, vc_ref)


def _combine_experts(yg_refs, gate_ref):
    gates = gate_ref[...]
    lo_acc = hi_acc = None
    for kk in range(TOP_K):
        lo, hi = _unpack_halves(_load_slabs(yg_refs, kk))
        gk = gates[:, kk:kk + 1]
        lo_acc = gk * lo if lo_acc is None else lo_acc + gk * lo
        hi_acc = gk * hi if hi_acc is None else hi_acc + gk * hi
    return jnp.concatenate([lo_acc, hi_acc], axis=1)


def _inproj_next_kernel(x_ref, yga_ref, ygb_ref, gate_ref, pmod_ref, mod_ref, g_ref, w_ref,
                        h_ref, xo_ref, kc_ref, vc_ref):
    x = x_ref[...] + pmod_ref[0, 5:6, :] * _combine_experts((yga_ref, ygb_ref), gate_ref)
    xo_ref[...] = x
    hm = _rms_mod(x, g_ref[...], mod_ref[0, 0:1, :], mod_ref[0, 1:2, :])
    _project_in(hm, w_ref, h_ref, kc_ref, vc_ref)


_TILE_SPEC = pl.BlockSpec((TM, D_MODEL), lambda i: (i, 0))
_MOD_SPEC = pl.BlockSpec((1, MOD_ROWS, D_MODEL), lambda i: (i, 0, 0))
_ROW_SPEC = pl.BlockSpec((1, D_MODEL), lambda i: (0, 0))
_RT_LANES = 128
_YG_SPEC = pl.BlockSpec((TOP_K, TM, D_SLAB), lambda i: (0, i, 0))
_GATE_SPEC = pl.BlockSpec((TM, _RT_LANES), lambda i: (i, 0))


def _inproj(x, moe, prev_mod, mod, g, w_bf16):
    w_spec = pl.BlockSpec((D_MODEL, IN_COLS), lambda i: (0, 0))
    h_spec = pl.BlockSpec((TM, IN_COLS), lambda i: (i, 0))
    h_shape = jax.ShapeDtypeStruct((T_ALL, IN_COLS), F32)
    c_spec = pl.BlockSpec((1, SEQ, NA_WIDTH), lambda i: (jnp.minimum(i, P_TILES - 1), 0, 0))
    c_shape = jax.ShapeDtypeStruct((BATCH, SEQ, NA_WIDTH), F32)
    x_shape = jax.ShapeDtypeStruct((T_ALL, D_MODEL), F32)
    if moe is None:
        h, x, kc, vc = pl.pallas_call(
            _inproj_first_kernel, out_shape=(h_shape, x_shape, c_shape, c_shape), grid=(N_TILES,),
            in_specs=[_p_tile(D_MODEL), _s_tile(D_MODEL), _MOD_SPEC, _ROW_SPEC, w_spec],
            out_specs=(h_spec, _TILE_SPEC, c_spec, c_spec),
            compiler_params=_params(), name="inproj_first",
        )(*x, mod, g, w_bf16)
        return h, x, (kc, vc)
    h, x, kc, vc = pl.pallas_call(
        _inproj_next_kernel,
        out_shape=(h_shape, x_shape, c_shape, c_shape),
        grid=(N_TILES,),
        in_specs=[_TILE_SPEC, _YG_SPEC, _YG_SPEC, _GATE_SPEC, _MOD_SPEC, _MOD_SPEC, _ROW_SPEC, w_spec],
        out_specs=(h_spec, _TILE_SPEC, c_spec, c_spec),
        compiler_params=_params(), name="inproj_next",
    )(x, *moe[0], moe[1], prev_mod, mod, g, w_bf16)
    return h, x, (kc, vc)


def _pair_mask(hh):
    lane = lax.broadcasted_iota(jnp.int32, (1, 2 * NA_HEAD_DIM), 1)
    return (lane >= hh * NA_HEAD_DIM) & (lane < (hh + 1) * NA_HEAD_DIM)


def _attn_prompt_kernel(q_ref, k_ref, v_ref, o_ref):
    scale = NA_HEAD_DIM ** -0.5
    for p in range(NA_HEADS // 2):
        cols = slice(p * 128, (p + 1) * 128)
        qp = q_ref[:, cols] * scale
        kp = k_ref[:, cols].astype(BF16)
        vp = v_ref[:, cols].astype(BF16)
        outs = []
        for hh in range(2):
            qh = jnp.where(_pair_mask(hh), qp, 0.0).astype(BF16)
            s = _dot_nt(qh, kp)
            e = jnp.exp(s - jnp.max(s, axis=-1, keepdims=True))
            den = jnp.sum(e, axis=-1, keepdims=True)
            outs.append(_dot(e.astype(BF16), vp) / den)
        o_ref[:, cols] = jnp.where(_pair_mask(0), outs[0], outs[1]).astype(o_ref.dtype)


def _attn_prompt(h):
    return pl.pallas_call(
        _attn_prompt_kernel,
        out_shape=jax.ShapeDtypeStruct((T_PROMPT, NA_WIDTH), BF16),
        grid=(BATCH,),
        in_specs=[pl.BlockSpec((SEQ, NA_WIDTH), lambda b: (b, 0)),
                  pl.BlockSpec((SEQ, NA_WIDTH), lambda b: (b, 1)),
                  pl.BlockSpec((SEQ, NA_WIDTH), lambda b: (b, 2))],
        out_specs=pl.BlockSpec((SEQ, NA_WIDTH), lambda b: (b, 0)),
        compiler_params=_params(), name="attn_prompt",
    )(h, h, h)


_NA_ROWS = DEC_SEQ // GRID_W
_NA_LOC = NA_KH * GRID_W


def _na_window_start(r):
    return jnp.clip(r - NA_KH // 2, 0, _NA_ROWS - NA_KH)


def _attn_sample_kernel(q_ref, k_ref, v_ref, ck_ref, cv_ref, bias_ref, o_ref):
    s0 = pl.multiple_of(_na_window_start(pl.program_id(1)) * GRID_W, GRID_W)
    scale = NA_HEAD_DIM ** -0.5
    for p in range(NA_HEADS // 2):
        cols = slice(p * 128, (p + 1) * 128)
        qp = q_ref[:, cols] * scale
        kl = k_ref[pl.ds(s0, _NA_LOC), cols].astype(BF16)
        vl = v_ref[pl.ds(s0, _NA_LOC), cols].astype(BF16)
        kc = ck_ref[0, :, cols].astype(BF16)
        vc = cv_ref[0, :, cols].astype(BF16)
        outs = []
        for hh in range(2):
            qh = jnp.where(_pair_mask(hh), qp, 0.0).astype(BF16)
            sl = _dot_nt(qh, kl) + bias_ref[0, 2 * p + hh]
            sc = _dot_nt(qh, kc)
            mx = jnp.maximum(jnp.max(sl, axis=-1, keepdims=True),
                             jnp.max(sc, axis=-1, keepdims=True))
            el = jnp.exp(sl - mx)
            ec = jnp.exp(sc - mx)
            den = jnp.sum(el, axis=-1, keepdims=True) + jnp.sum(ec, axis=-1, keepdims=True)
            outs.append((_dot(el.astype(BF16), vl) + _dot(ec.astype(BF16), vc)) / den)
        o_ref[:, cols] = jnp.where(_pair_mask(0), outs[0], outs[1]).astype(o_ref.dtype)


def _attn_sample(h, ck, cv, bias):
    q_row0 = T_PROMPT // GRID_W
    kv_row0 = T_PROMPT // DEC_SEQ
    return pl.pallas_call(
        _attn_sample_kernel,
        out_shape=jax.ShapeDtypeStruct((T_SAMPLE, NA_WIDTH), BF16),
        grid=(DEC_BATCH, _NA_ROWS),
        in_specs=[pl.BlockSpec((GRID_W, NA_WIDTH), lambda b, r: (q_row0 + b * _NA_ROWS + r, 0)),
                  pl.BlockSpec((DEC_SEQ, NA_WIDTH), lambda b, r: (kv_row0 + b, 1)),
                  pl.BlockSpec((DEC_SEQ, NA_WIDTH), lambda b, r: (kv_row0 + b, 2)),
                  pl.BlockSpec((1, PAST_LEN, NA_WIDTH), lambda b, r: (b, 0, 0)),
                  pl.BlockSpec((1, PAST_LEN, NA_WIDTH), lambda b, r: (b, 0, 0)),
                  pl.BlockSpec((1, NA_HEADS, GRID_W, _NA_LOC), lambda b, r: (_na_window_start(r) - r + NA_KH - 1, 0, 0, 0))],
        out_specs=pl.BlockSpec((GRID_W, NA_WIDTH), lambda b, r: (b * _NA_ROWS + r, 0)),
        compiler_params=_params(2), name="attn_sample",
    )(h, h, h, ck, cv, bias)


_NA_DR = 2 * NA_KH - 1
_NA_DC = 2 * NA_KW - 1
_NA_DC_PAD = 32


def _bias_expand_kernel(rb_ref, onehot_ref, inwin_ref, o_ref):
    hi, mid, lo = _split3(rb_ref[...])
    oh = onehot_ref[...]
    e = _dot(hi, oh) + _dot(mid, oh) + _dot(lo, oh)
    o_ref[...] = jnp.where(inwin_ref[...] != 0.0, e, NEG_INF)


def _na_bias_tables(rel_bias):
    qc = np.arange(GRID_W)
    kc = np.arange(GRID_W)
    q_start = np.clip(qc - NA_KW // 2, 0, GRID_W - NA_KW)
    in_win = (kc[None, :] >= q_start[:, None]) & (kc[None, :] < q_start[:, None] + NA_KW)
    dc = np.clip(kc[None, :] - qc[:, None] + NA_KW - 1, 0, _NA_DC - 1)
    onehot = (np.arange(_NA_DC_PAD)[:, None] == dc.reshape(1, -1)).astype(np.float32)
    n_rows = DEPTH * NA_HEADS * _NA_DR
    rb = jnp.pad(rel_bias.astype(F32).reshape(n_rows, _NA_DC), ((0, 0), (0, _NA_DC_PAD - _NA_DC)))
    full = lambda shape: pl.BlockSpec(shape, lambda i: (0, 0))
    e = pl.pallas_call(
        _bias_expand_kernel,
        out_shape=jax.ShapeDtypeStruct((n_rows, GRID_W * GRID_W), F32), grid=(1,),
        in_specs=[full((n_rows, _NA_DC_PAD)), full((_NA_DC_PAD, GRID_W * GRID_W)), full((1, GRID_W * GRID_W))],
        out_specs=full((n_rows, GRID_W * GRID_W)),
        compiler_params=_params(), name="na_bias_expand",
    )(rb, jnp.asarray(onehot, BF16), jnp.asarray(in_win.reshape(1, -1), F32))
    e = e.reshape(DEPTH, NA_HEADS, _NA_DR, GRID_W, GRID_W)
    w = jnp.stack([e[:, :, b:b + NA_KH] for b in range(NA_KH)], axis=1)
    return w.transpose(0, 1, 2, 4, 3, 5).reshape(DEPTH, NA_KH, NA_HEADS, GRID_W, _NA_LOC)


def _hgrn_kernel(*refs, n_tok, has_state):
    if has_state:
        (q_ref, zf_ref, zb_ref, v_ref, g_ref, lbf_ref, lbb_ref, og_ref, s0f_ref, s0b_ref,
         rec_ref, sf_ref, sb_ref, kf_s, bf_s, kb_s, bb_s, o_s, z_s, st_s) = refs
    else:
        (q_ref, zf_ref, zb_ref, v_ref, g_ref, lbf_ref, lbb_ref, og_ref,
         rec_ref, sf_ref, sb_ref, kf_s, bf_s, kb_s, bb_s, o_s, z_s, st_s) = refs
        s0f_ref = s0b_ref = None
    C = HG_CHUNK
    W = HG_WIDTH
    n_chunks = n_tok // C
    rr = lax.broadcasted_iota(jnp.int32, (W, W), 0)
    cc = lax.broadcasted_iota(jnp.int32, (W, W), 1)
    same_chunk = jnp.right_shift(rr, 5) == jnp.right_shift(cc, 5)
    tri_prefix = jnp.where(same_chunk & (cc <= rr), 1.0, 0.0).astype(BF16)
    tri_suffix = jnp.where(same_chunk & (cc >= rr), 1.0, 0.0).astype(BF16)
    same_head = jnp.right_shift(rr, 6) == jnp.right_shift(cc, 6)
    head_ones = jnp.where(same_head, 1.0, 0.0).astype(BF16)

    for ti in range(n_tok // W):
        rows = slice(ti * W, (ti + 1) * W)
        for z_ref, lb_ref, k_s, b_s, tri in ((zf_ref, lbf_ref, kf_s, bf_s, tri_prefix),
                                             (zb_ref, lbb_ref, kb_s, bb_s, tri_suffix)):
            z = z_ref[rows, :]
            lb = lb_ref[...]
            f = lb + (1.0 - lb) * jax.nn.sigmoid(z)
            logf = jnp.log(jnp.maximum(f, F_FLOOR))
            k_s[rows, :] = (1.0 - lb) * jax.nn.sigmoid(-z)
            hi, mid, lo = _split3(logf)
            b_s[rows, :] = _dot(tri, hi) + _dot(tri, mid) + _dot(tri, lo)

    srow = lax.broadcasted_iota(jnp.int32, (C, W), 0)

    def scan_direction(k_s, b_s, fwd):
        def chunk(ci, carry):
            c = ci if fwd else n_chunks - 1 - ci
            base = pl.multiple_of(c * C, C)
            q = q_ref[pl.ds(base, C), :]
            k = k_s[pl.ds(base, C), :]
            b = b_s[pl.ds(base, C), :]
            v = v_ref[pl.ds(base, C), :]
            for t in range(C):
                qt = q_ref[pl.ds(base + t, 1), :]
                bt = b_s[pl.ds(base + t, 1), :]
                keep = (srow <= t) if fwd else (srow >= t)
                zt = jnp.where(keep, (qt * k) * jnp.exp(bt - b), 0.0)
                z_s[t * C:(t + 1) * C, :] = zt.astype(BF16)
            a_rep = _dot(z_s[...], head_ones)
            o_intra = jnp.sum(a_rep.reshape(C, C, W) * v[None, :, :], axis=1)
            b_end = b_s[pl.ds(base + (C - 1 if fwd else 0), 1), :]
            q_in = q * jnp.exp(b)
            k_st = k * jnp.exp(b_end - b)
            st = st_s[...]
            o_inter = _dot_nt(q_in.astype(BF16), st.astype(BF16))
            upd = _dot_tn(v.astype(BF16), k_st.astype(BF16))
            st_s[...] = st * jnp.exp(b_end) + jnp.where(same_head, upd, 0.0)
            o = o_intra + o_inter
            if fwd:
                o_s[pl.ds(base, C), :] = o
            else:
                o_s[pl.ds(base, C), :] = o_s[pl.ds(base, C), :] + o
            return carry
        lax.fori_loop(0, n_chunks, chunk, 0)

    st_s[...] = s0f_ref[0] if has_state else jnp.zeros((W, W), F32)
    scan_direction(kf_s, bf_s, True)
    sf_ref[0] = st_s[...]
    st_s[...] = s0b_ref[0] if has_state else jnp.zeros((W, W), F32)
    scan_direction(kb_s, bb_s, False)
    sb_ref[0] = st_s[...]

    for ti in range(n_tok // W):
        rows = slice(ti * W, (ti + 1) * W)
        o = o_s[rows, :]
        sq = o * o
        sq_hi = sq.astype(BF16)
        sq_lo = (sq - sq_hi.astype(F32)).astype(BF16)
        ms = (_dot(sq_hi, head_ones) + _dot(sq_lo, head_ones)) * (1.0 / HG_DV)
        g = g_ref[rows, :]
        y = o * lax.rsqrt(ms + RMS_EPS) * og_ref[...] * (g * jax.nn.sigmoid(g))
        rec_ref[rows, :] = y.astype(rec_ref.dtype)


def _hgrn(h, lbf, lbb, og, s0f_t, s0b_t, n_tok, n_seq, row0):
    W = HG_WIDTH
    has_state = s0f_t is not None

    def col(cb):
        return pl.BlockSpec((n_tok, W), lambda i, cb=cb: (row0 + i, cb))

    vec = pl.BlockSpec((1, W), lambda i: (0, 0))
    st_spec = pl.BlockSpec((1, W, W), lambda i: (i, 0, 0))
    in_specs = [col(_CB_HQ), col(_CB_ZF), col(_CB_ZB), col(_CB_HI), col(_CB_HG), vec, vec, vec]
    args = [h, h, h, h, h, lbf, lbb, og]
    if has_state:
        in_specs += [st_spec, st_spec]
        args += [s0f_t, s0b_t]
    seq_f32 = pltpu.VMEM((n_tok, W), F32)
    return pl.pallas_call(
        functools.partial(_hgrn_kernel, n_tok=n_tok, has_state=has_state),
        out_shape=(jax.ShapeDtypeStruct((n_seq * n_tok, W), BF16),
                   jax.ShapeDtypeStruct((n_seq, W, W), F32),
                   jax.ShapeDtypeStruct((n_seq, W, W), F32)),
        grid=(n_seq,),
        in_specs=in_specs,
        out_specs=(pl.BlockSpec((n_tok, W), lambda i: (i, 0)), st_spec, st_spec),
        scratch_shapes=[seq_f32, seq_f32, seq_f32, seq_f32, seq_f32,
                        pltpu.VMEM((HG_CHUNK * HG_CHUNK, W), BF16),
                        pltpu.VMEM((W, W), F32)],
        compiler_params=_params(), name="hgrn_state" if has_state else "hgrn_zero",
    )(*args)


def _state_to_blockdiag_t(s):
    eye = jnp.eye(HG_HEADS, dtype=s.dtype)
    return jnp.einsum('bhkv,hg->bhvgk', s, eye).reshape(s.shape[0], HG_WIDTH, HG_HEADS * HG_DK)


def _blockdiag_t_to_state(st):
    s5 = st.reshape(st.shape[0], HG_HEADS, HG_DV, HG_HEADS, HG_DK)
    return jnp.stack([s5[:, hh, :, hh, :] for hh in range(HG_HEADS)], axis=1).transpose(0, 1, 3, 2)


def _gmlp_kernel(u_ref, v_ref, g_ref, ws_ref, b_ref, o_ref):
    v = v_ref[...]
    ms = jnp.mean(v * v, axis=-1, keepdims=True)
    vn = (v * lax.rsqrt(ms + RMS_EPS) * g_ref[...]).astype(BF16)
    lane = lax.broadcasted_iota(jnp.int32, (1, GM_WIDTH), 1)
    z = b_ref[...]
    for gi in range(GM_GROUPS):
        zg = _dot(ws_ref[gi], vn)
        in_group = (lane >= gi * GM_GDIM) & (lane < (gi + 1) * GM_GDIM)
        z = z + jnp.where(in_group, zg, 0.0)
    o_ref[...] = (u_ref[...] * z).astype(o_ref.dtype)


def _gmlp(h, vnorm_g, ws_bf16, bias_full):
    W = GM_WIDTH
    return pl.pallas_call(
        _gmlp_kernel,
        out_shape=jax.ShapeDtypeStruct((T_ALL, W), BF16),
        grid=(T_ALL // GM_CHUNK,),
        in_specs=[pl.BlockSpec((GM_CHUNK, W), lambda i: (i, _CB_GU)),
                  pl.BlockSpec((GM_CHUNK, W), lambda i: (i, _CB_GV)),
                  pl.BlockSpec((1, W), lambda i: (0, 0)),
                  pl.BlockSpec((GM_GROUPS, GM_CHUNK, GM_CHUNK), lambda i: (0, 0, 0)),
                  pl.BlockSpec((GM_CHUNK, W), lambda i: (0, 0))],
        out_specs=pl.BlockSpec((GM_CHUNK, W), lambda i: (i, 0)),
        compiler_params=_params(), name="gmlp",
    )(h, h, vnorm_g, ws_bf16, bias_full)


def _outproj_kernel(attp_ref, atts_ref, recp_ref, recs_ref, mlp_ref, x_ref, mod_ref, g_ref, w_ref, wr_ref, br_ref,
                    x1_ref, h2a_ref, h2b_ref, idx_ref, gate_ref, cnt_ref):
    @pl.when(pl.program_id(0) == 0)
    def _():
        cnt_ref[...] = jnp.zeros_like(cnt_ref)

    out = (_dot(_pick_group(attp_ref, atts_ref), w_ref[0:NA_WIDTH, :])
           + _dot(_pick_group(recp_ref, recs_ref), w_ref[NA_WIDTH:NA_WIDTH + HG_WIDTH, :])
           + _dot(mlp_ref[...], w_ref[NA_WIDTH + HG_WIDTH:, :]))
    x1 = x_ref[...] + mod_ref[0, 2:3, :] * out
    x1_ref[...] = x1
    h2 = _rms_mod(x1, g_ref[...], mod_ref[0, 3:4, :], mod_ref[0, 4:5, :])
    _store_slabs((h2a_ref, h2b_ref), _pack_halves(h2))
    h_hi = h2.astype(BF16)
    h_lo = (h2 - h_hi.astype(F32)).astype(BF16)
    wr = wr_ref[...]
    w_hi = wr.astype(BF16)
    w_lo = (wr - w_hi.astype(F32)).astype(BF16)
    logits = _dot(h_hi, w_hi) + _dot(h_lo, w_hi) + _dot(h_hi, w_lo) + br_ref[...]
    lane_e = lax.broadcasted_iota(jnp.int32, (TM, N_EXPERTS), 1).astype(F32)
    lane_o = lax.broadcasted_iota(jnp.int32, (TM, _RT_LANES), 1)
    idx_acc = jnp.zeros((TM, _RT_LANES), F32)
    val_acc = jnp.zeros((TM, _RT_LANES), F32)
    top0 = None
    den = jnp.zeros((TM, 1), F32)
    work = logits
    picks = []
    for kk in range(TOP_K):
        m = jnp.max(work, axis=-1, keepdims=True)
        first = jnp.min(jnp.where(work == m, lane_e, float(N_EXPERTS)), axis=-1, keepdims=True)
        if kk == 0:
            top0 = m
        e = jnp.exp(m - top0)
        den = den + e
        idx_acc = jnp.where(lane_o == kk, first, idx_acc)
        val_acc = jnp.where(lane_o == kk, e, val_acc)
        picks.append(lane_e == first)
        work = jnp.where(picks[-1], -jnp.inf, work)
    gate_ref[...] = val_acc / den
    sel = jnp.zeros((TM, N_EXPERTS), F32)
    for pk in picks:
        sel = sel + jnp.where(pk, 1.0, 0.0)
    rr = lax.broadcasted_iota(jnp.int32, (TM, TM), 0)
    cc = lax.broadcasted_iota(jnp.int32, (TM, TM), 1)
    earlier = jnp.where(cc < rr, 1.0, 0.0).astype(BF16)
    seen = cnt_ref[0:1, 0:N_EXPERTS]
    before = _dot(earlier, sel.astype(BF16)) + seen
    for kk, pk in enumerate(picks):
        rank = jnp.sum(jnp.where(pk, before, 0.0), axis=-1, keepdims=True)
        idx_acc = jnp.where(lane_o == TOP_K + kk, rank, idx_acc)
    idx_ref[...] = idx_acc.astype(jnp.int32)
    cnt_ref[0:1, 0:N_EXPERTS] = seen + jnp.sum(sel, axis=0, keepdims=True)


def _outproj(att_p, att_s, rec_p, rec_s, mlp, x, mod, g, w_bf16, wr, br):
    def tile(width):
        return pl.BlockSpec((TM, width), lambda i: (i, 0))

    return pl.pallas_call(
        _outproj_kernel,
        out_shape=(jax.ShapeDtypeStruct((T_ALL, D_MODEL), F32),
                   jax.ShapeDtypeStruct((T_ALL, D_SLAB), jnp.int32),
                   jax.ShapeDtypeStruct((T_ALL, D_SLAB), jnp.int32),
                   jax.ShapeDtypeStruct((T_ALL, _RT_LANES), jnp.int32),
                   jax.ShapeDtypeStruct((T_ALL, _RT_LANES), F32),
                   jax.ShapeDtypeStruct((8, _RT_LANES), F32)),
        grid=(N_TILES,),
        in_specs=[_p_tile(NA_WIDTH), _s_tile(NA_WIDTH), _p_tile(HG_WIDTH), _s_tile(HG_WIDTH),
                  tile(GM_WIDTH), _TILE_SPEC, _MOD_SPEC, _ROW_SPEC,
                  pl.BlockSpec((D_MODEL, D_MODEL), lambda i: (0, 0)),
                  pl.BlockSpec((D_MODEL, N_EXPERTS), lambda i: (0, 0)),
                  pl.BlockSpec((1, N_EXPERTS), lambda i: (0, 0))],
        out_specs=(_TILE_SPEC, tile(D_SLAB), tile(D_SLAB), tile(_RT_LANES), tile(_RT_LANES),
                   pl.BlockSpec((8, _RT_LANES), lambda i: (0, 0))),
        compiler_params=_params(), name="outproj_router",
    )(att_p, att_s, rec_p, rec_s, mlp, x, mod, g, w_bf16, wr, br)


_W_CAST_ROWS = 128


def _moe_kernel(blk0_ref, nblk_ref, xa_hbm, xb_hbm, wg_ref, bg_ref, wu_ref, bu_ref, wd_ref, bd_ref,
                ya_hbm, yb_hbm, wg_s, wu_s, wd_s):
    e = pl.program_id(0)
    n_blocks = nblk_ref[e]

    @pl.when(n_blocks > 0)
    def _():
        def cast_rows(ci, carry):
            rows = pl.ds(pl.multiple_of(ci * _W_CAST_ROWS, _W_CAST_ROWS), _W_CAST_ROWS)
            wg_s[rows, :] = wg_ref[0, 0, rows, :].astype(BF16)
            wu_s[rows, :] = wu_ref[0, 0, rows, :].astype(BF16)
            wd_s[rows, :] = wd_ref[0, 0, rows, :].astype(BF16)
            return carry
        lax.fori_loop(0, D_MODEL // _W_CAST_ROWS, cast_rows, 0)

        def block(xa_ref, xb_ref, ya_ref, yb_ref):
            lo, hi = _unpack_halves(_load_slabs((xa_ref, xb_ref)))
            x = jnp.concatenate([lo.astype(BF16), hi.astype(BF16)], axis=1)
            gate = jnp.minimum(_dot(x, wg_s[...]) + bg_ref[0, 0], SWIGLU_LIMIT)
            up = jnp.clip(_dot(x, wu_s[...]) + bu_ref[0, 0], -SWIGLU_LIMIT, SWIGLU_LIMIT)
            glu = gate * jax.nn.sigmoid(SWIGLU_ALPHA * gate)
            act = ((up + 1.0) * glu).astype(BF16)
            _store_slabs((ya_ref, yb_ref), _pack_halves(_dot(act, wd_s[...]) + bd_ref[0, 0]))

        blk0 = blk0_ref[e]
        rows_spec = pl.BlockSpec((MOE_BM, D_SLAB), lambda i: (blk0 + i, 0))
        pltpu.emit_pipeline(block, grid=(n_blocks,), in_specs=[rows_spec, rows_spec],
                            out_specs=[rows_spec, rows_spec])(xa_hbm, xb_hbm, ya_hbm, yb_hbm)


def _moe(layer, blk0, nblk, x_sorted, wg, bg, wu, bu, wd, bd):
    w_spec = pl.BlockSpec((1, 1, D_MODEL, D_MODEL), lambda e, b0, nb: (layer, e, 0, 0))
    b_spec = pl.BlockSpec((1, 1, 1, D_MODEL), lambda e, b0, nb: (layer, e, 0, 0))
    hbm = pl.BlockSpec(memory_space=pl.ANY)
    w_bf16 = pltpu.VMEM((D_MODEL, D_MODEL), BF16)
    bias4 = lambda b: b.reshape(DEPTH, N_EXPERTS, 1, D_MODEL)
    return pl.pallas_call(
        _moe_kernel,
        out_shape=(jax.ShapeDtypeStruct((MOE_SLOTS, D_SLAB), jnp.int32),) * N_SPLIT,
        grid_spec=pltpu.PrefetchScalarGridSpec(
            num_scalar_prefetch=2, grid=(N_EXPERTS,),
            in_specs=[hbm, hbm, w_spec, b_spec, w_spec, b_spec, w_spec, b_spec],
            out_specs=(hbm, hbm),
            scratch_shapes=[w_bf16, w_bf16, w_bf16]),
        compiler_params=_params(), name="moe_experts",
    )(blk0, nblk, *x_sorted, wg, bias4(bg), wu, bias4(bu), wd, bias4(bd))


def _route(top_i, rank, counts):
    experts = jnp.arange(N_EXPERTS, dtype=jnp.int32)
    nblk = (counts + MOE_BM - 1) // MOE_BM
    blk0 = jnp.cumsum(nblk) - nblk
    start_of = jnp.sum(jnp.where(top_i[..., None] == experts, blk0 * MOE_BM, 0), axis=-1)
    dest = start_of + rank
    return dest.astype(jnp.int32), blk0.astype(jnp.int32), nblk.astype(jnp.int32)


_SC_WINDOW = 128


def _sc_mesh():
    return plsc.VectorSubcoreMesh(core_axis_name="core", subcore_axis_name="subcore")


def _sc_scatter_rows(srcs, idx, n_out):
    n_src, width = srcs[0].shape
    n_idx = idx.shape[1]
    src_windows = n_src // _SC_WINDOW

    def body(*refs):
        x_hbm = refs[:len(srcs)]
        i_hbm = refs[len(srcs)]
        o_hbm = refs[len(srcs) + 1:]
        for xs, os_ in zip(x_hbm, o_hbm):
            def step(x_vmem, i_vmem, os_=os_):
                pltpu.sync_copy(x_vmem, os_.at[i_vmem.at[0]])

            pltpu.emit_pipeline(
                step, grid=(n_idx // _SC_WINDOW,),
                in_specs=[pl.BlockSpec((_SC_WINDOW, width), lambda i: (i % src_windows, 0)),
                          pl.BlockSpec((1, _SC_WINDOW), lambda i: (0, i))],
                out_specs=[],
                core_axis_name=("core", "subcore"),
                dimension_semantics=(pltpu.PARALLEL,),
            )(xs, i_hbm)

    out_type = tuple(jax.ShapeDtypeStruct((n_out, width), s.dtype) for s in srcs)
    return pl.kernel(body, out_type=out_type, mesh=_sc_mesh(), scratch_types=[],
                     name="sc_scatter_rows")(*srcs, idx)


def _sc_gather_rows(tables, idx):
    n_idx = idx.shape[1]
    width = tables[0].shape[1]

    def body(*refs):
        t_hbm = refs[:len(tables)]
        i_hbm = refs[len(tables)]
        o_hbm = refs[len(tables) + 1:]
        for ts, os_ in zip(t_hbm, o_hbm):
            def step(i_vmem, o_vmem, ts=ts):
                pltpu.sync_copy(ts.at[i_vmem.at[0]], o_vmem)

            pltpu.emit_pipeline(
                step, grid=(n_idx // _SC_WINDOW,),
                in_specs=[pl.BlockSpec((1, _SC_WINDOW), lambda i: (0, i))],
                out_specs=[pl.BlockSpec((_SC_WINDOW, width), lambda i: (i, 0))],
                core_axis_name=("core", "subcore"),
                dimension_semantics=(pltpu.PARALLEL,),
            )(i_hbm, os_)

    out_type = tuple(jax.ShapeDtypeStruct((n_idx, width), t.dtype) for t in tables)
    return pl.kernel(body, out_type=out_type, mesh=_sc_mesh(), scratch_types=[],
                     name="sc_gather_rows")(*tables, idx)


def _final_kernel(x_ref, yga_ref, ygb_ref, gate_ref, mod_ref, g_ref, yp_ref, ys_ref):
    x = x_ref[...] + mod_ref[0, 5:6, :] * _combine_experts((yga_ref, ygb_ref), gate_ref)
    ms = jnp.mean(x * x, axis=-1, keepdims=True)
    y = x * lax.rsqrt(ms + RMS_EPS) * g_ref[...]

    @pl.when(pl.program_id(0) < P_TILES)
    def _():
        yp_ref[...] = y

    @pl.when(pl.program_id(0) >= P_TILES)
    def _():
        ys_ref[...] = y


def _final(x, moe, mod, g):
    return pl.pallas_call(
        _final_kernel,
        out_shape=(jax.ShapeDtypeStruct((T_PROMPT, D_MODEL), F32), jax.ShapeDtypeStruct((T_SAMPLE, D_MODEL), F32)),
        grid=(N_TILES,),
        in_specs=[_TILE_SPEC, _YG_SPEC, _YG_SPEC, _GATE_SPEC, _MOD_SPEC, _ROW_SPEC],
        out_specs=(_p_tile(D_MODEL), _s_tile(D_MODEL)),
        compiler_params=_params(), name="final_norm",
    )(x, *moe[0], moe[1], mod, g)


def kernel(x_prompt, x_sample, cache_k, cache_v, state_hgrn_fwd, state_hgrn_bwd, c, c_ctx, w_mod, b_mod, norm1_g, norm2_g, w_in, na_rel_bias, hgrn_lb, hgrn_onorm_g, gmlp_vnorm_g, gmlp_ws, gmlp_b, w_out, router_w, router_b, w_gate, b_gate, w_up, b_up, w_down, b_down, final_g):
    x = (x_prompt.reshape(T_PROMPT, D_MODEL), x_sample.reshape(T_SAMPLE, D_MODEL))

    cond = jnp.zeros((MOD_ROWS, D_MODEL), F32).at[0].set(c_ctx).at[1:1 + DEC_BATCH].set(c)
    mod = _modulation(cond, w_mod, b_mod)
    tile_row = np.concatenate([np.zeros(P_TILES, np.int32),
                               1 + np.arange(N_TILES - P_TILES, dtype=np.int32) // (DEC_SEQ // TM)])
    mod_tiles = mod[:, tile_row].reshape(DEPTH, N_TILES, 6, D_MODEL)
    mod_tiles = jnp.pad(mod_tiles, ((0, 0), (0, 0), (0, MOD_ROWS - 6), (0, 0)))

    lb_soft = jax.nn.softmax(hgrn_lb.astype(F32), axis=1)
    lower = jnp.cumsum(lb_soft, axis=1) - lb_soft[:, :1]

    na_bias = _na_bias_tables(na_rel_bias)

    k_list, v_list, sf_list, sb_list = [], [], [], []
    moe_out = None
    for l in range(DEPTH):
        h, x, (kc, vc) = _inproj(x, moe_out, mod_tiles[l - 1] if l else None, mod_tiles[l],
                                 norm1_g[l][None, :], w_in[l].astype(BF16))
        k_list.append(kc.reshape(BATCH, SEQ, NA_HEADS, NA_HEAD_DIM))
        v_list.append(vc.reshape(BATCH, SEQ, NA_HEADS, NA_HEAD_DIM))

        att_p = _attn_prompt(h)
        att_s = _attn_sample(h, cache_k[:, l].reshape(DEC_BATCH, PAST_LEN, NA_WIDTH),
                             cache_v[:, l].reshape(DEC_BATCH, PAST_LEN, NA_WIDTH), na_bias[l])
        lbf = lower[0, l][None, :]
        lbb = lower[1, l][None, :]
        og = jnp.tile(hgrn_onorm_g[l], HG_HEADS)[None, :]
        rec_p, sf_t, sb_t = _hgrn(h, lbf, lbb, og, None, None, SEQ, BATCH, 0)
        rec_s, _, _ = _hgrn(h, lbf, lbb, og, _state_to_blockdiag_t(state_hgrn_fwd[:, l].astype(F32)),
                            _state_to_blockdiag_t(state_hgrn_bwd[:, l].astype(F32)),
                            DEC_SEQ, DEC_BATCH, T_PROMPT // DEC_SEQ)
        sf_list.append(_blockdiag_t_to_state(sf_t))
        sb_list.append(_blockdiag_t_to_state(sb_t))
        gm_bias = jnp.repeat(gmlp_b[l].T, GM_GDIM, axis=1)
        mlp = _gmlp(h, gmlp_vnorm_g[l][None, :], gmlp_ws[l].astype(BF16), gm_bias)

        x, h2a, h2b, idx_pad, gate_pad, cnt = _outproj(att_p, att_s, rec_p, rec_s, mlp, x, mod_tiles[l],
                                                       norm2_g[l][None, :], w_out[l].astype(BF16),
                                                       router_w[l], router_b[l][None, :])
        dest, blk0, nblk = _route(idx_pad[:, :TOP_K], idx_pad[:, TOP_K:2 * TOP_K],
                                  cnt[0, :N_EXPERTS].astype(jnp.int32))
        dest_flat = dest.T.reshape(1, TOP_K * T_ALL)
        x_sorted = _sc_scatter_rows((h2a, h2b), dest_flat, MOE_SLOTS)
        y_sorted = _moe(l, blk0, nblk, x_sorted, w_gate, b_gate, w_up, b_up, w_down, b_down)
        y_tok = _sc_gather_rows(y_sorted, dest_flat)
        moe_out = ([yt.reshape(TOP_K, T_ALL, D_SLAB) for yt in y_tok], gate_pad)

    y_prompt, y_sample = _final(x, moe_out, mod_tiles[DEPTH - 1], final_g[None, :])
    y_prompt = y_prompt.reshape(BATCH, SEQ, D_MODEL)
    y_sample = y_sample.reshape(DEC_BATCH, DEC_SEQ, D_MODEL)
    return (y_prompt, y_sample, jnp.stack(k_list, axis=1), jnp.stack(v_list, axis=1),
            jnp.stack(sf_list, axis=1), jnp.stack(sb_list, axis=1))
```

```python
import functools

import numpy as np
import jax
import jax.numpy as jnp
from jax import lax
from jax.experimental import pallas as pl
from jax.experimental.pallas import tpu as pltpu
from jax.experimental.pallas import tpu_sc as plsc

F32 = jnp.float32
BF16 = jnp.bfloat16

D_MODEL = 1024
BATCH = 32
SEQ = 256
DEPTH = 2
DEC_BATCH = 2
DEC_SEQ = 1024
PAST_LEN = 512
GRID_W = 64
NA_HEADS = 8
NA_HEAD_DIM = 64
NA_WIDTH = NA_HEADS * NA_HEAD_DIM
NA_KH = 8
NA_KW = 16
HG_HEADS = 4
HG_DK = 64
HG_DV = 64
HG_WIDTH = HG_HEADS * HG_DV
HG_CHUNK = 32
F_FLOOR = 1e-30
GM_GROUPS = 4
GM_GDIM = 64
GM_WIDTH = GM_GROUPS * GM_GDIM
GM_CHUNK = 128
IN_COLS = 3 * NA_WIDTH + 5 * HG_WIDTH + 2 * GM_WIDTH
N_EXPERTS = 32
TOP_K = 4
SWIGLU_LIMIT = 7.0
SWIGLU_ALPHA = 1.702
RMS_EPS = 1e-6
NEG_INF = -1e30

T_PROMPT = BATCH * SEQ
T_SAMPLE = DEC_BATCH * DEC_SEQ
T_ALL = T_PROMPT + T_SAMPLE
TM = 256
N_TILES = T_ALL // TM
P_TILES = T_PROMPT // TM
MOE_BM = 256
MOE_SLOTS = -(-(T_ALL * TOP_K + N_EXPERTS * (MOE_BM - 1)) // MOE_BM) * MOE_BM
MOE_BLOCKS = MOE_SLOTS // MOE_BM
MOD_ROWS = 8
V7X_VMEM_LIMIT = 48 * 1024 * 1024

_CB_HQ, _CB_ZF, _CB_ZB, _CB_HI, _CB_HG, _CB_GU, _CB_GV = 6, 7, 8, 9, 10, 11, 12


def _dot(a, b):
    return jnp.dot(a, b, preferred_element_type=F32)


def _dot_nt(a, b):
    return lax.dot_general(a, b, (((1,), (1,)), ((), ())), preferred_element_type=F32)


def _dot_tn(a, b):
    return lax.dot_general(a, b, (((0,), (0,)), ((), ())), preferred_element_type=F32)


def _split3(x):
    hi = x.astype(BF16)
    r1 = x - hi.astype(F32)
    mid = r1.astype(BF16)
    lo = (r1 - mid.astype(F32)).astype(BF16)
    return hi, mid, lo


D_PACK = D_MODEL // 2
N_SPLIT = 2
D_SLAB = D_PACK // N_SPLIT


def _pack_halves(x):
    half = x.shape[1] // 2
    lo = pltpu.bitcast(x[:, :half].astype(BF16).astype(F32), jnp.uint32)
    hi = pltpu.bitcast(x[:, half:].astype(BF16).astype(F32), jnp.uint32)
    return pltpu.bitcast(jnp.right_shift(lo, jnp.uint32(16)) | hi, jnp.int32)


def _unpack_halves(w):
    u = pltpu.bitcast(w, jnp.uint32)
    lo = pltpu.bitcast(jnp.left_shift(u, jnp.uint32(16)), F32)
    hi = pltpu.bitcast(u & jnp.uint32(0xFFFF0000), F32)
    return lo, hi


def _load_slabs(refs, *lead):
    return jnp.concatenate([r[lead] if lead else r[...] for r in refs], axis=1)


def _store_slabs(refs, packed):
    for si, r in enumerate(refs):
        r[...] = packed[:, si * D_SLAB:(si + 1) * D_SLAB]


def _params(n_axes=1):
    return pltpu.CompilerParams(dimension_semantics=("arbitrary",) * n_axes,
                                vmem_limit_bytes=V7X_VMEM_LIMIT)


def _mod_kernel(cond_ref, w_ref, b_ref, o_ref):
    c = cond_ref[...]
    c = c * jax.nn.sigmoid(c)
    w = w_ref[0]
    c_hi = c.astype(BF16)
    c_lo = (c - c_hi.astype(F32)).astype(BF16)
    w_hi = w.astype(BF16)
    w_lo = (w - w_hi.astype(F32)).astype(BF16)
    o_ref[0] = _dot(c_hi, w_hi) + _dot(c_lo, w_hi) + _dot(c_hi, w_lo) + b_ref[0]


def _modulation(cond, w_mod, b_mod):
    tn = 1536
    return pl.pallas_call(
        _mod_kernel,
        out_shape=jax.ShapeDtypeStruct((DEPTH, MOD_ROWS, 6 * D_MODEL), F32),
        grid=(DEPTH, 6 * D_MODEL // tn),
        in_specs=[pl.BlockSpec((MOD_ROWS, D_MODEL), lambda l, j: (0, 0)),
                  pl.BlockSpec((1, D_MODEL, tn), lambda l, j: (l, 0, j)),
                  pl.BlockSpec((1, 1, tn), lambda l, j: (l, 0, j))],
        out_specs=pl.BlockSpec((1, MOD_ROWS, tn), lambda l, j: (l, 0, j)),
        compiler_params=_params(2),
        name="modulation",
    )(cond, w_mod, b_mod.reshape(DEPTH, 1, 6 * D_MODEL))


def _rms_mod(x, g, shift, scale):
    ms = jnp.mean(x * x, axis=-1, keepdims=True)
    y = x * lax.rsqrt(ms + RMS_EPS) * g
    return y * (1.0 + scale) + shift


def _project_in(hm, w_ref, h_ref, kc_ref, vc_ref):
    h = _dot(hm.astype(BF16), w_ref[...])
    h_ref[...] = h

    @pl.when(pl.program_id(0) < P_TILES)
    def _():
        kc_ref[0] = h[:, NA_WIDTH:2 * NA_WIDTH]
        vc_ref[0] = h[:, 2 * NA_WIDTH:3 * NA_WIDTH]


def _pick_group(p_ref, s_ref):
    return jnp.where(pl.program_id(0) < P_TILES, p_ref[...], s_ref[...])


def _p_tile(width):
    return pl.BlockSpec((TM, width), lambda i: (jnp.minimum(i, P_TILES - 1), 0))


def _s_tile(width):
    return pl.BlockSpec((TM, width), lambda i: (jnp.maximum(i - P_TILES, 0), 0))


def _inproj_first_kernel(xp_ref, xs_ref, mod_ref, g_ref, w_ref, h_ref, xo_ref, kc_ref, vc_ref):
    x = _pick_group(xp_ref, xs_ref)
    xo_ref[...] = x
    hm = _rms_mod(x, g_ref[...], mod_ref[0, 0:1, :], mod_ref[0, 1:2, :])
    _project_in(hm, w_ref, h_ref, kc_ref, vc_ref)


def _combine_experts(yg_refs, gate_ref):
    gates = gate_ref[...]
    lo_acc = hi_acc = None
    for kk in range(TOP_K):
        lo, hi = _unpack_halves(_load_slabs(yg_refs, kk))
        gk = gates[:, kk:kk + 1]
        lo_acc = gk * lo if lo_acc is None else lo_acc + gk * lo
        hi_acc = gk * hi if hi_acc is None else hi_acc + gk * hi
    return jnp.concatenate([lo_acc, hi_acc], axis=1)


def _inproj_next_kernel(x_ref, yga_ref, ygb_ref, gate_ref, pmod_ref, mod_ref, g_ref, w_ref,
                        h_ref, xo_ref, kc_ref, vc_ref):
    x = x_ref[...] + pmod_ref[0, 5:6, :] * _combine_experts((yga_ref, ygb_ref), gate_ref)
    xo_ref[...] = x
    hm = _rms_mod(x, g_ref[...], mod_ref[0, 0:1, :], mod_ref[0, 1:2, :])
    _project_in(hm, w_ref, h_ref, kc_ref, vc_ref)


_TILE_SPEC = pl.BlockSpec((TM, D_MODEL), lambda i: (i, 0))
_MOD_SPEC = pl.BlockSpec((1, MOD_ROWS, D_MODEL), lambda i: (i, 0, 0))
_ROW_SPEC = pl.BlockSpec((1, D_MODEL), lambda i: (0, 0))
_RT_LANES = 128
_YG_SPEC = pl.BlockSpec((TOP_K, TM, D_SLAB), lambda i: (0, i, 0))
_GATE_SPEC = pl.BlockSpec((TM, _RT_LANES), lambda i: (i, 0))


def _inproj(x, moe, prev_mod, mod, g, w_bf16):
    w_spec = pl.BlockSpec((D_MODEL, IN_COLS), lambda i: (0, 0))
    h_spec = pl.BlockSpec((TM, IN_COLS), lambda i: (i, 0))
    h_shape = jax.ShapeDtypeStruct((T_ALL, IN_COLS), F32)
    c_spec = pl.BlockSpec((1, SEQ, NA_WIDTH), lambda i: (jnp.minimum(i, P_TILES - 1), 0, 0))
    c_shape = jax.ShapeDtypeStruct((BATCH, SEQ, NA_WIDTH), F32)
    x_shape = jax.ShapeDtypeStruct((T_ALL, D_MODEL), F32)
    if moe is None:
        h, x, kc, vc = pl.pallas_call(
            _inproj_first_kernel, out_shape=(h_shape, x_shape, c_shape, c_shape), grid=(N_TILES,),
            in_specs=[_p_tile(D_MODEL), _s_tile(D_MODEL), _MOD_SPEC, _ROW_SPEC, w_spec],
            out_specs=(h_spec, _TILE_SPEC, c_spec, c_spec),
            compiler_params=_params(), name="inproj_first",
        )(*x, mod, g, w_bf16)
        return h, x, (kc, vc)
    h, x, kc, vc = pl.pallas_call(
        _inproj_next_kernel,
        out_shape=(h_shape, x_shape, c_shape, c_shape),
        grid=(N_TILES,),
        in_specs=[_TILE_SPEC, _YG_SPEC, _YG_SPEC, _GATE_SPEC, _MOD_SPEC, _MOD_SPEC, _ROW_SPEC, w_spec],
        out_specs=(h_spec, _TILE_SPEC, c_spec, c_spec),
        compiler_params=_params(), name="inproj_next",
    )(x, *moe[0], moe[1], prev_mod, mod, g, w_bf16)
    return h, x, (kc, vc)


def _pair_mask(hh):
    lane = lax.broadcasted_iota(jnp.int32, (1, 2 * NA_HEAD_DIM), 1)
    return (lane >= hh * NA_HEAD_DIM) & (lane < (hh + 1) * NA_HEAD_DIM)


def _attn_prompt_kernel(q_ref, k_ref, v_ref, o_ref):
    scale = NA_HEAD_DIM ** -0.5
    for p in range(NA_HEADS // 2):
        cols = slice(p * 128, (p + 1) * 128)
        qp = q_ref[:, cols] * scale
        kp = k_ref[:, cols].astype(BF16)
        vp = v_ref[:, cols].astype(BF16)
        outs = []
        for hh in range(2):
            qh = jnp.where(_pair_mask(hh), qp, 0.0).astype(BF16)
            s = _dot_nt(qh, kp)
            e = jnp.exp(s - jnp.max(s, axis=-1, keepdims=True))
            den = jnp.sum(e, axis=-1, keepdims=True)
            outs.append(_dot(e.astype(BF16), vp) / den)
        o_ref[:, cols] = jnp.where(_pair_mask(0), outs[0], outs[1]).astype(o_ref.dtype)


def _attn_prompt(h):
    return pl.pallas_call(
        _attn_prompt_kernel,
        out_shape=jax.ShapeDtypeStruct((T_PROMPT, NA_WIDTH), BF16),
        grid=(BATCH,),
        in_specs=[pl.BlockSpec((SEQ, NA_WIDTH), lambda b: (b, 0)),
                  pl.BlockSpec((SEQ, NA_WIDTH), lambda b: (b, 1)),
                  pl.BlockSpec((SEQ, NA_WIDTH), lambda b: (b, 2))],
        out_specs=pl.BlockSpec((SEQ, NA_WIDTH), lambda b: (b, 0)),
        compiler_params=_params(), name="attn_prompt",
    )(h, h, h)


_NA_ROWS = DEC_SEQ // GRID_W
_NA_LOC = NA_KH * GRID_W


def _na_window_start(r):
    return jnp.clip(r - NA_KH // 2, 0, _NA_ROWS - NA_KH)


def _attn_sample_kernel(q_ref, k_ref, v_ref, ck_ref, cv_ref, bias_ref, o_ref):
    s0 = pl.multiple_of(_na_window_start(pl.program_id(1)) * GRID_W, GRID_W)
    scale = NA_HEAD_DIM ** -0.5
    for p in range(NA_HEADS // 2):
        cols = slice(p * 128, (p + 1) * 128)
        qp = q_ref[:, cols] * scale
        kl = k_ref[pl.ds(s0, _NA_LOC), cols].astype(BF16)
        vl = v_ref[pl.ds(s0, _NA_LOC), cols].astype(BF16)
        kc = ck_ref[0, :, cols].astype(BF16)
        vc = cv_ref[0, :, cols].astype(BF16)
        outs = []
        for hh in range(2):
            qh = jnp.where(_pair_mask(hh), qp, 0.0).astype(BF16)
            sl = _dot_nt(qh, kl) + bias_ref[0, 2 * p + hh]
            sc = _dot_nt(qh, kc)
            mx = jnp.maximum(jnp.max(sl, axis=-1, keepdims=True),
                             jnp.max(sc, axis=-1, keepdims=True))
            el = jnp.exp(sl - mx)
            ec = jnp.exp(sc - mx)
            den = jnp.sum(el, axis=-1, keepdims=True) + jnp.sum(ec, axis=-1, keepdims=True)
            outs.append((_dot(el.astype(BF16), vl) + _dot(ec.astype(BF16), vc)) / den)
        o_ref[:, cols] = jnp.where(_pair_mask(0), outs[0], outs[1]).astype(o_ref.dtype)


def _attn_sample(h, ck, cv, bias):
    q_row0 = T_PROMPT // GRID_W
    kv_row0 = T_PROMPT // DEC_SEQ
    return pl.pallas_call(
        _attn_sample_kernel,
        out_shape=jax.ShapeDtypeStruct((T_SAMPLE, NA_WIDTH), BF16),
        grid=(DEC_BATCH, _NA_ROWS),
        in_specs=[pl.BlockSpec((GRID_W, NA_WIDTH), lambda b, r: (q_row0 + b * _NA_ROWS + r, 0)),
                  pl.BlockSpec((DEC_SEQ, NA_WIDTH), lambda b, r: (kv_row0 + b, 1)),
                  pl.BlockSpec((DEC_SEQ, NA_WIDTH), lambda b, r: (kv_row0 + b, 2)),
                  pl.BlockSpec((1, PAST_LEN, NA_WIDTH), lambda b, r: (b, 0, 0)),
                  pl.BlockSpec((1, PAST_LEN, NA_WIDTH), lambda b, r: (b, 0, 0)),
                  pl.BlockSpec((1, NA_HEADS, GRID_W, _NA_LOC), lambda b, r: (_na_window_start(r) - r + NA_KH - 1, 0, 0, 0))],
        out_specs=pl.BlockSpec((GRID_W, NA_WIDTH), lambda b, r: (b * _NA_ROWS + r, 0)),
        compiler_params=_params(2), name="attn_sample",
    )(h, h, h, ck, cv, bias)


_NA_DR = 2 * NA_KH - 1
_NA_DC = 2 * NA_KW - 1
_NA_DC_PAD = 32


def _bias_expand_kernel(rb_ref, onehot_ref, inwin_ref, o_ref):
    hi, mid, lo = _split3(rb_ref[...])
    oh = onehot_ref[...]
    e = _dot(hi, oh) + _dot(mid, oh) + _dot(lo, oh)
    o_ref[...] = jnp.where(inwin_ref[...] != 0.0, e, NEG_INF)


def _na_bias_tables(rel_bias):
    qc = np.arange(GRID_W)
    kc = np.arange(GRID_W)
    q_start = np.clip(qc - NA_KW // 2, 0, GRID_W - NA_KW)
    in_win = (kc[None, :] >= q_start[:, None]) & (kc[None, :] < q_start[:, None] + NA_KW)
    dc = np.clip(kc[None, :] - qc[:, None] + NA_KW - 1, 0, _NA_DC - 1)
    onehot = (np.arange(_NA_DC_PAD)[:, None] == dc.reshape(1, -1)).astype(np.float32)
    n_rows = DEPTH * NA_HEADS * _NA_DR
    rb = jnp.pad(rel_bias.astype(F32).reshape(n_rows, _NA_DC), ((0, 0), (0, _NA_DC_PAD - _NA_DC)))
    full = lambda shape: pl.BlockSpec(shape, lambda i: (0, 0))
    e = pl.pallas_call(
        _bias_expand_kernel,
        out_shape=jax.ShapeDtypeStruct((n_rows, GRID_W * GRID_W), F32), grid=(1,),
        in_specs=[full((n_rows, _NA_DC_PAD)), full((_NA_DC_PAD, GRID_W * GRID_W)), full((1, GRID_W * GRID_W))],
        out_specs=full((n_rows, GRID_W * GRID_W)),
        compiler_params=_params(), name="na_bias_expand",
    )(rb, jnp.asarray(onehot, BF16), jnp.asarray(in_win.reshape(1, -1), F32))
    e = e.reshape(DEPTH, NA_HEADS, _NA_DR, GRID_W, GRID_W)
    w = jnp.stack([e[:, :, b:b + NA_KH] for b in range(NA_KH)], axis=1)
    return w.transpose(0, 1, 2, 4, 3, 5).reshape(DEPTH, NA_KH, NA_HEADS, GRID_W, _NA_LOC)


def _hgrn_kernel(*refs, n_tok, has_state):
    if has_state:
        (q_ref, zf_ref, zb_ref, v_ref, g_ref, lbf_ref, lbb_ref, og_ref, s0f_ref, s0b_ref,
         rec_ref, sf_ref, sb_ref, kf_s, bf_s, kb_s, bb_s, o_s, z_s, st_s) = refs
    else:
        (q_ref, zf_ref, zb_ref, v_ref, g_ref, lbf_ref, lbb_ref, og_ref,
         rec_ref, sf_ref, sb_ref, kf_s, bf_s, kb_s, bb_s, o_s, z_s, st_s) = refs
        s0f_ref = s0b_ref = None
    C = HG_CHUNK
    W = HG_WIDTH
    n_chunks = n_tok // C
    rr = lax.broadcasted_iota(jnp.int32, (W, W), 0)
    cc = lax.broadcasted_iota(jnp.int32, (W, W), 1)
    same_chunk = jnp.right_shift(rr, 5) == jnp.right_shift(cc, 5)
    tri_prefix = jnp.where(same_chunk & (cc <= rr), 1.0, 0.0).astype(BF16)
    tri_suffix = jnp.where(same_chunk & (cc >= rr), 1.0, 0.0).astype(BF16)
    same_head = jnp.right_shift(rr, 6) == jnp.right_shift(cc, 6)
    head_ones = jnp.where(same_head, 1.0, 0.0).astype(BF16)

    for ti in range(n_tok // W):
        rows = slice(ti * W, (ti + 1) * W)
        for z_ref, lb_ref, k_s, b_s, tri in ((zf_ref, lbf_ref, kf_s, bf_s, tri_prefix),
                                             (zb_ref, lbb_ref, kb_s, bb_s, tri_suffix)):
            z = z_ref[rows, :]
            lb = lb_ref[...]
            f = lb + (1.0 - lb) * jax.nn.sigmoid(z)
            logf = jnp.log(jnp.maximum(f, F_FLOOR))
            k_s[rows, :] = (1.0 - lb) * jax.nn.sigmoid(-z)
            hi, mid, lo = _split3(logf)
            b_s[rows, :] = _dot(tri, hi) + _dot(tri, mid) + _dot(tri, lo)

    srow = lax.broadcasted_iota(jnp.int32, (C, W), 0)

    def scan_direction(k_s, b_s, fwd):
        def chunk(ci, carry):
            c = ci if fwd else n_chunks - 1 - ci
            base = pl.multiple_of(c * C, C)
            q = q_ref[pl.ds(base, C), :]
            k = k_s[pl.ds(base, C), :]
            b = b_s[pl.ds(base, C), :]
            v = v_ref[pl.ds(base, C), :]
            for t in range(C):
                qt = q_ref[pl.ds(base + t, 1), :]
                bt = b_s[pl.ds(base + t, 1), :]
                keep = (srow <= t) if fwd else (srow >= t)
                zt = jnp.where(keep, (qt * k) * jnp.exp(bt - b), 0.0)
                z_s[t * C:(t + 1) * C, :] = zt.astype(BF16)
            a_rep = _dot(z_s[...], head_ones)
            o_intra = jnp.sum(a_rep.reshape(C, C, W) * v[None, :, :], axis=1)
            b_end = b_s[pl.ds(base + (C - 1 if fwd else 0), 1), :]
            q_in = q * jnp.exp(b)
            k_st = k * jnp.exp(b_end - b)
            st = st_s[...]
            o_inter = _dot_nt(q_in.astype(BF16), st.astype(BF16))
            upd = _dot_tn(v.astype(BF16), k_st.astype(BF16))
            st_s[...] = st * jnp.exp(b_end) + jnp.where(same_head, upd, 0.0)
            o = o_intra + o_inter
            if fwd:
                o_s[pl.ds(base, C), :] = o
            else:
                o_s[pl.ds(base, C), :] = o_s[pl.ds(base, C), :] + o
            return carry
        lax.fori_loop(0, n_chunks, chunk, 0)

    st_s[...] = s0f_ref[0] if has_state else jnp.zeros((W, W), F32)
    scan_direction(kf_s, bf_s, True)
    sf_ref[0] = st_s[...]
    st_s[...] = s0b_ref[0] if has_state else jnp.zeros((W, W), F32)
    scan_direction(kb_s, bb_s, False)
    sb_ref[0] = st_s[...]

    for ti in range(n_tok // W):
        rows = slice(ti * W, (ti + 1) * W)
        o = o_s[rows, :]
        sq = o * o
        sq_hi = sq.astype(BF16)
        sq_lo = (sq - sq_hi.astype(F32)).astype(BF16)
        ms = (_dot(sq_hi, head_ones) + _dot(sq_lo, head_ones)) * (1.0 / HG_DV)
        g = g_ref[rows, :]
        y = o * lax.rsqrt(ms + RMS_EPS) * og_ref[...] * (g * jax.nn.sigmoid(g))
        rec_ref[rows, :] = y.astype(rec_ref.dtype)


def _hgrn(h, lbf, lbb, og, s0f_t, s0b_t, n_tok, n_seq, row0):
    W = HG_WIDTH
    has_state = s0f_t is not None

    def col(cb):
        return pl.BlockSpec((n_tok, W), lambda i, cb=cb: (row0 + i, cb))

    vec = pl.BlockSpec((1, W), lambda i: (0, 0))
    st_spec = pl.BlockSpec((1, W, W), lambda i: (i, 0, 0))
    in_specs = [col(_CB_HQ), col(_CB_ZF), col(_CB_ZB), col(_CB_HI), col(_CB_HG), vec, vec, vec]
    args = [h, h, h, h, h, lbf, lbb, og]
    if has_state:
        in_specs += [st_spec, st_spec]
        args += [s0f_t, s0b_t]
    seq_f32 = pltpu.VMEM((n_tok, W), F32)
    return pl.pallas_call(
        functools.partial(_hgrn_kernel, n_tok=n_tok, has_state=has_state),
        out_shape=(jax.ShapeDtypeStruct((n_seq * n_tok, W), BF16),
                   jax.ShapeDtypeStruct((n_seq, W, W), F32),
                   jax.ShapeDtypeStruct((n_seq, W, W), F32)),
        grid=(n_seq,),
        in_specs=in_specs,
        out_specs=(pl.BlockSpec((n_tok, W), lambda i: (i, 0)), st_spec, st_spec),
        scratch_shapes=[seq_f32, seq_f32, seq_f32, seq_f32, seq_f32,
                        pltpu.VMEM((HG_CHUNK * HG_CHUNK, W), BF16),
                        pltpu.VMEM((W, W), F32)],
        compiler_params=_params(), name="hgrn_state" if has_state else "hgrn_zero",
    )(*args)


def _state_to_blockdiag_t(s):
    eye = jnp.eye(HG_HEADS, dtype=s.dtype)
    return jnp.einsum('bhkv,hg->bhvgk', s, eye).reshape(s.shape[0], HG_WIDTH, HG_HEADS * HG_DK)


def _blockdiag_t_to_state(st):
    s5 = st.reshape(st.shape[0], HG_HEADS, HG_DV, HG_HEADS, HG_DK)
    return jnp.stack([s5[:, hh, :, hh, :] for hh in range(HG_HEADS)], axis=1).transpose(0, 1, 3, 2)


def _gmlp_kernel(u_ref, v_ref, g_ref, ws_ref, b_ref, o_ref):
    v = v_ref[...]
    ms = jnp.mean(v * v, axis=-1, keepdims=True)
    vn = (v * lax.rsqrt(ms + RMS_EPS) * g_ref[...]).astype(BF16)
    lane = lax.broadcasted_iota(jnp.int32, (1, GM_WIDTH), 1)
    z = b_ref[...]
    for gi in range(GM_GROUPS):
        zg = _dot(ws_ref[gi], vn)
        in_group = (lane >= gi * GM_GDIM) & (lane < (gi + 1) * GM_GDIM)
        z = z + jnp.where(in_group, zg, 0.0)
    o_ref[...] = (u_ref[...] * z).astype(o_ref.dtype)


def _gmlp(h, vnorm_g, ws_bf16, bias_full):
    W = GM_WIDTH
    return pl.pallas_call(
        _gmlp_kernel,
        out_shape=jax.ShapeDtypeStruct((T_ALL, W), BF16),
        grid=(T_ALL // GM_CHUNK,),
        in_specs=[pl.BlockSpec((GM_CHUNK, W), lambda i: (i, _CB_GU)),
                  pl.BlockSpec((GM_CHUNK, W), lambda i: (i, _CB_GV)),
                  pl.BlockSpec((1, W), lambda i: (0, 0)),
                  pl.BlockSpec((GM_GROUPS, GM_CHUNK, GM_CHUNK), lambda i: (0, 0, 0)),
                  pl.BlockSpec((GM_CHUNK, W), lambda i: (0, 0))],
        out_specs=pl.BlockSpec((GM_CHUNK, W), lambda i: (i, 0)),
        compiler_params=_params(), name="gmlp",
    )(h, h, vnorm_g, ws_bf16, bias_full)


def _outproj_kernel(attp_ref, atts_ref, recp_ref, recs_ref, mlp_ref, x_ref, mod_ref, g_ref, w_ref, wr_ref, br_ref,
                    x1_ref, h2a_ref, h2b_ref, idx_ref, gate_ref, cnt_ref):
    @pl.when(pl.program_id(0) == 0)
    def _():
        cnt_ref[...] = jnp.zeros_like(cnt_ref)

    out = (_dot(_pick_group(attp_ref, atts_ref), w_ref[0:NA_WIDTH, :])
           + _dot(_pick_group(recp_ref, recs_ref), w_ref[NA_WIDTH:NA_WIDTH + HG_WIDTH, :])
           + _dot(mlp_ref[...], w_ref[NA_WIDTH + HG_WIDTH:, :]))
    x1 = x_ref[...] + mod_ref[0, 2:3, :] * out
    x1_ref[...] = x1
    h2 = _rms_mod(x1, g_ref[...], mod_ref[0, 3:4, :], mod_ref[0, 4:5, :])
    _store_slabs((h2a_ref, h2b_ref), _pack_halves(h2))
    h_hi = h2.astype(BF16)
    h_lo = (h2 - h_hi.astype(F32)).astype(BF16)
    wr = wr_ref[...]
    w_hi = wr.astype(BF16)
    w_lo = (wr - w_hi.astype(F32)).astype(BF16)
    logits = _dot(h_hi, w_hi) + _dot(h_lo, w_hi) + _dot(h_hi, w_lo) + br_ref[...]
    lane_e = lax.broadcasted_iota(jnp.int32, (TM, N_EXPERTS), 1).astype(F32)
    lane_o = lax.broadcasted_iota(jnp.int32, (TM, _RT_LANES), 1)
    idx_acc = jnp.zeros((TM, _RT_LANES), F32)
    val_acc = jnp.zeros((TM, _RT_LANES), F32)
    top0 = None
    den = jnp.zeros((TM, 1), F32)
    work = logits
    picks = []
    for kk in range(TOP_K):
        m = jnp.max(work, axis=-1, keepdims=True)
        first = jnp.min(jnp.where(work == m, lane_e, float(N_EXPERTS)), axis=-1, keepdims=True)
        if kk == 0:
            top0 = m
        e = jnp.exp(m - top0)
        den = den + e
        idx_acc = jnp.where(lane_o == kk, first, idx_acc)
        val_acc = jnp.where(lane_o == kk, e, val_acc)
        picks.append(lane_e == first)
        work = jnp.where(picks[-1], -jnp.inf, work)
    gate_ref[...] = val_acc / den
    sel = jnp.zeros((TM, N_EXPERTS), F32)
    for pk in picks:
        sel = sel + jnp.where(pk, 1.0, 0.0)
    rr = lax.broadcasted_iota(jnp.int32, (TM, TM), 0)
    cc = lax.broadcasted_iota(jnp.int32, (TM, TM), 1)
    earlier = jnp.where(cc < rr, 1.0, 0.0).astype(BF16)
    seen = cnt_ref[0:1, 0:N_EXPERTS]
    before = _dot(earlier, sel.astype(BF16)) + seen
    for kk, pk in enumerate(picks):
        rank = jnp.sum(jnp.where(pk, before, 0.0), axis=-1, keepdims=True)
        idx_acc = jnp.where(lane_o == TOP_K + kk, rank, idx_acc)
    idx_ref[...] = idx_acc.astype(jnp.int32)
    cnt_ref[0:1, 0:N_EXPERTS] = seen + jnp.sum(sel, axis=0, keepdims=True)


def _outproj(att_p, att_s, rec_p, rec_s, mlp, x, mod, g, w_bf16, wr, br):
    def tile(width):
        return pl.BlockSpec((TM, width), lambda i: (i, 0))

    return pl.pallas_call(
        _outproj_kernel,
        out_shape=(jax.ShapeDtypeStruct((T_ALL, D_MODEL), F32),
                   jax.ShapeDtypeStruct((T_ALL, D_SLAB), jnp.int32),
                   jax.ShapeDtypeStruct((T_ALL, D_SLAB), jnp.int32),
                   jax.ShapeDtypeStruct((T_ALL, _RT_LANES), jnp.int32),
                   jax.ShapeDtypeStruct((T_ALL, _RT_LANES), F32),
                   jax.ShapeDtypeStruct((8, _RT_LANES), F32)),
        grid=(N_TILES,),
        in_specs=[_p_tile(NA_WIDTH), _s_tile(NA_WIDTH), _p_tile(HG_WIDTH), _s_tile(HG_WIDTH),
                  tile(GM_WIDTH), _TILE_SPEC, _MOD_SPEC, _ROW_SPEC,
                  pl.BlockSpec((D_MODEL, D_MODEL), lambda i: (0, 0)),
                  pl.BlockSpec((D_MODEL, N_EXPERTS), lambda i: (0, 0)),
                  pl.BlockSpec((1, N_EXPERTS), lambda i: (0, 0))],
        out_specs=(_TILE_SPEC, tile(D_SLAB), tile(D_SLAB), tile(_RT_LANES), tile(_RT_LANES),
                   pl.BlockSpec((8, _RT_LANES), lambda i: (0, 0))),
        compiler_params=_params(), name="outproj_router",
    )(att_p, att_s, rec_p, rec_s, mlp, x, mod, g, w_bf16, wr, br)


def _moe_kernel(blk_e_ref, blk_on_ref, blk_new_ref, blk_slot_ref, blk_next_ref,
                xa_ref, xb_ref, wg_hbm, bg_ref, wu_hbm, bu_ref, wd_hbm, bd_ref,
                ya_ref, yb_ref, w_buf, w_sem, *, layer):
    j = pl.program_id(0)

    def weight_copies(expert, slot):
        return [pltpu.make_async_copy(w_hbm.at[layer, expert], w_buf.at[slot, wi], w_sem.at[slot, wi])
                for wi, w_hbm in enumerate((wg_hbm, wu_hbm, wd_hbm))]

    @pl.when(j == 0)
    def _():
        for cp in weight_copies(blk_e_ref[0], 0):
            cp.start()

    @pl.when(blk_new_ref[j] != 0)
    def _():
        slot = blk_slot_ref[j]
        for cp in weight_copies(blk_e_ref[j], slot):
            cp.wait()

        @pl.when(blk_next_ref[j] >= 0)
        def _():
            for cp in weight_copies(blk_next_ref[j], 1 - slot):
                cp.start()

    @pl.when(blk_on_ref[j] != 0)
    def _():
        slot = blk_slot_ref[j]
        lo, hi = _unpack_halves(_load_slabs((xa_ref, xb_ref)))
        x = jnp.concatenate([lo.astype(BF16), hi.astype(BF16)], axis=1)
        gate = jnp.minimum(_dot(x, w_buf[slot, 0].astype(BF16)) + bg_ref[0, 0], SWIGLU_LIMIT)
        up = jnp.clip(_dot(x, w_buf[slot, 1].astype(BF16)) + bu_ref[0, 0], -SWIGLU_LIMIT, SWIGLU_LIMIT)
        glu = gate * jax.nn.sigmoid(SWIGLU_ALPHA * gate)
        act = ((up + 1.0) * glu).astype(BF16)
        _store_slabs((ya_ref, yb_ref), _pack_halves(_dot(act, w_buf[slot, 2].astype(BF16)) + bd_ref[0, 0]))

    @pl.when(blk_on_ref[j] == 0)
    def _():
        ya_ref[...] = jnp.zeros_like(ya_ref)
        yb_ref[...] = jnp.zeros_like(yb_ref)


def _moe(layer, plan, x_sorted, wg, bg, wu, bu, wd, bd):
    n_plan = len(plan)
    b_spec = pl.BlockSpec((1, 1, 1, D_MODEL), lambda j, be, *_: (layer, be[j], 0, 0))
    x_spec = pl.BlockSpec((MOE_BM, D_SLAB), lambda j, *_: (j, 0))
    hbm = pl.BlockSpec(memory_space=pl.ANY)
    bias4 = lambda b: b.reshape(DEPTH, N_EXPERTS, 1, D_MODEL)
    return pl.pallas_call(
        functools.partial(_moe_kernel, layer=layer),
        out_shape=(jax.ShapeDtypeStruct((MOE_SLOTS, D_SLAB), jnp.int32),) * N_SPLIT,
        grid_spec=pltpu.PrefetchScalarGridSpec(
            num_scalar_prefetch=n_plan, grid=(MOE_BLOCKS,),
            in_specs=[x_spec, x_spec, hbm, b_spec, hbm, b_spec, hbm, b_spec],
            out_specs=(x_spec, x_spec),
            scratch_shapes=[pltpu.VMEM((2, 3, D_MODEL, D_MODEL), F32), pltpu.SemaphoreType.DMA((2, 3))]),
        compiler_params=_params(), name="moe_experts",
    )(*plan, *x_sorted, wg, bias4(bg), wu, bias4(bu), wd, bias4(bd))


def _route(top_i, rank, counts):
    experts = jnp.arange(N_EXPERTS, dtype=jnp.int32)
    nblk = (counts + MOE_BM - 1) // MOE_BM
    blk_end = jnp.cumsum(nblk)
    blk0 = blk_end - nblk
    start_of = jnp.sum(jnp.where(top_i[..., None] == experts, blk0 * MOE_BM, 0), axis=-1)
    dest = start_of + rank
    live = counts > 0
    last_live = jnp.max(jnp.where(live, experts, 0))
    later_live = live[None, :] & (experts[None, :] > experts[:, None])
    next_live = jnp.where(jnp.any(later_live, axis=1), jnp.argmax(later_live, axis=1), -1)
    ordinal = jnp.cumsum(live.astype(jnp.int32)) - 1
    blk = jnp.arange(MOE_BLOCKS, dtype=jnp.int32)
    blk_on = blk < blk_end[-1]
    blk_e = jnp.where(blk_on, jnp.minimum(jnp.sum((blk_end[None, :] <= blk[:, None]).astype(jnp.int32), axis=1),
                                          N_EXPERTS - 1), last_live)
    blk_new = blk_on & jnp.concatenate([jnp.ones((1,), bool), blk_e[1:] != blk_e[:-1]])
    plan = (blk_e, blk_on, blk_new, ordinal[blk_e] % 2, next_live[blk_e])
    return dest.astype(jnp.int32), tuple(p.astype(jnp.int32) for p in plan)


_SC_WINDOW = 128


def _sc_mesh():
    return plsc.VectorSubcoreMesh(core_axis_name="core", subcore_axis_name="subcore")


def _sc_scatter_rows(srcs, idx, n_out):
    n_src, width = srcs[0].shape
    n_idx = idx.shape[1]
    src_windows = n_src // _SC_WINDOW

    def body(*refs):
        x_hbm = refs[:len(srcs)]
        i_hbm = refs[len(srcs)]
        o_hbm = refs[len(srcs) + 1:]
        for xs, os_ in zip(x_hbm, o_hbm):
            def step(x_vmem, i_vmem, os_=os_):
                pltpu.sync_copy(x_vmem, os_.at[i_vmem.at[0]])

            pltpu.emit_pipeline(
                step, grid=(n_idx // _SC_WINDOW,),
                in_specs=[pl.BlockSpec((_SC_WINDOW, width), lambda i: (i % src_windows, 0)),
                          pl.BlockSpec((1, _SC_WINDOW), lambda i: (0, i))],
                out_specs=[],
                core_axis_name=("core", "subcore"),
                dimension_semantics=(pltpu.PARALLEL,),
            )(xs, i_hbm)

    out_type = tuple(jax.ShapeDtypeStruct((n_out, width), s.dtype) for s in srcs)
    return pl.kernel(body, out_type=out_type, mesh=_sc_mesh(), scratch_types=[],
                     name="sc_scatter_rows")(*srcs, idx)


def _sc_gather_rows(tables, idx):
    n_idx = idx.shape[1]
    width = tables[0].shape[1]

    def body(*refs):
        t_hbm = refs[:len(tables)]
        i_hbm = refs[len(tables)]
        o_hbm = refs[len(tables) + 1:]
        for ts, os_ in zip(t_hbm, o_hbm):
            def step(i_vmem, o_vmem, ts=ts):
                pltpu.sync_copy(ts.at[i_vmem.at[0]], o_vmem)

            pltpu.emit_pipeline(
                step, grid=(n_idx // _SC_WINDOW,),
                in_specs=[pl.BlockSpec((1, _SC_WINDOW), lambda i: (0, i))],
                out_specs=[pl.BlockSpec((_SC_WINDOW, width), lambda i: (i, 0))],
                core_axis_name=("core", "subcore"),
                dimension_semantics=(pltpu.PARALLEL,),
            )(i_hbm, os_)

    out_type = tuple(jax.ShapeDtypeStruct((n_idx, width), t.dtype) for t in tables)
    return pl.kernel(body, out_type=out_type, mesh=_sc_mesh(), scratch_types=[],
                     name="sc_gather_rows")(*tables, idx)


def _final_kernel(x_ref, yga_ref, ygb_ref, gate_ref, mod_ref, g_ref, yp_ref, ys_ref):
    x = x_ref[...] + mod_ref[0, 5:6, :] * _combine_experts((yga_ref, ygb_ref), gate_ref)
    ms = jnp.mean(x * x, axis=-1, keepdims=True)
    y = x * lax.rsqrt(ms + RMS_EPS) * g_ref[...]

    @pl.when(pl.program_id(0) < P_TILES)
    def _():
        yp_ref[...] = y

    @pl.when(pl.program_id(0) >= P_TILES)
    def _():
        ys_ref[...] = y


def _final(x, moe, mod, g):
    return pl.pallas_call(
        _final_kernel,
        out_shape=(jax.ShapeDtypeStruct((T_PROMPT, D_MODEL), F32), jax.ShapeDtypeStruct((T_SAMPLE, D_MODEL), F32)),
        grid=(N_TILES,),
        in_specs=[_TILE_SPEC, _YG_SPEC, _YG_SPEC, _GATE_SPEC, _MOD_SPEC, _ROW_SPEC],
        out_specs=(_p_tile(D_MODEL), _s_tile(D_MODEL)),
        compiler_params=_params(), name="final_norm",
    )(x, *moe[0], moe[1], mod, g)


def kernel(x_prompt, x_sample, cache_k, cache_v, state_hgrn_fwd, state_hgrn_bwd, c, c_ctx, w_mod, b_mod, norm1_g, norm2_g, w_in, na_rel_bias, hgrn_lb, hgrn_onorm_g, gmlp_vnorm_g, gmlp_ws, gmlp_b, w_out, router_w, router_b, w_gate, b_gate, w_up, b_up, w_down, b_down, final_g):
    x = (x_prompt.reshape(T_PROMPT, D_MODEL), x_sample.reshape(T_SAMPLE, D_MODEL))

    cond = jnp.zeros((MOD_ROWS, D_MODEL), F32).at[0].set(c_ctx).at[1:1 + DEC_BATCH].set(c)
    mod = _modulation(cond, w_mod, b_mod)
    tile_row = np.concatenate([np.zeros(P_TILES, np.int32),
                               1 + np.arange(N_TILES - P_TILES, dtype=np.int32) // (DEC_SEQ // TM)])
    mod_tiles = mod[:, tile_row].reshape(DEPTH, N_TILES, 6, D_MODEL)
    mod_tiles = jnp.pad(mod_tiles, ((0, 0), (0, 0), (0, MOD_ROWS - 6), (0, 0)))

    lb_soft = jax.nn.softmax(hgrn_lb.astype(F32), axis=1)
    lower = jnp.cumsum(lb_soft, axis=1) - lb_soft[:, :1]

    na_bias = _na_bias_tables(na_rel_bias)

    k_list, v_list, sf_list, sb_list = [], [], [], []
    moe_out = None
    for l in range(DEPTH):
        h, x, (kc, vc) = _inproj(x, moe_out, mod_tiles[l - 1] if l else None, mod_tiles[l],
                                 norm1_g[l][None, :], w_in[l].astype(BF16))
        k_list.append(kc.reshape(BATCH, SEQ, NA_HEADS, NA_HEAD_DIM))
        v_list.append(vc.reshape(BATCH, SEQ, NA_HEADS, NA_HEAD_DIM))

        att_p = _attn_prompt(h)
        att_s = _attn_sample(h, cache_k[:, l].reshape(DEC_BATCH, PAST_LEN, NA_WIDTH),
                             cache_v[:, l].reshape(DEC_BATCH, PAST_LEN, NA_WIDTH), na_bias[l])
        lbf = lower[0, l][None, :]
        lbb = lower[1, l][None, :]
        og = jnp.tile(hgrn_onorm_g[l], HG_HEADS)[None, :]
        rec_p, sf_t, sb_t = _hgrn(h, lbf, lbb, og, None, None, SEQ, BATCH, 0)
        rec_s, _, _ = _hgrn(h, lbf, lbb, og, _state_to_blockdiag_t(state_hgrn_fwd[:, l].astype(F32)),
                            _state_to_blockdiag_t(state_hgrn_bwd[:, l].astype(F32)),
                            DEC_SEQ, DEC_BATCH, T_PROMPT // DEC_SEQ)
        sf_list.append(_blockdiag_t_to_state(sf_t))
        sb_list.append(_blockdiag_t_to_state(sb_t))
        gm_bias = jnp.repeat(gmlp_b[l].T, GM_GDIM, axis=1)
        mlp = _gmlp(h, gmlp_vnorm_g[l][None, :], gmlp_ws[l].astype(BF16), gm_bias)

        x, h2a, h2b, idx_pad, gate_pad, cnt = _outproj(att_p, att_s, rec_p, rec_s, mlp, x, mod_tiles[l],
                                                       norm2_g[l][None, :], w_out[l].astype(BF16),
                                                       router_w[l], router_b[l][None, :])
        dest, plan = _route(idx_pad[:, :TOP_K], idx_pad[:, TOP_K:2 * TOP_K],
                            cnt[0, :N_EXPERTS].astype(jnp.int32))
        dest_flat = dest.T.reshape(1, TOP_K * T_ALL)
        x_sorted = _sc_scatter_rows((h2a, h2b), dest_flat, MOE_SLOTS)
        y_sorted = _moe(l, plan, x_sorted, w_gate, b_gate, w_up, b_up, w_down, b_down)
        y_tok = _sc_gather_rows(y_sorted, dest_flat)
        moe_out = ([yt.reshape(TOP_K, T_ALL, D_SLAB) for yt in y_tok], gate_pad)

    y_prompt, y_sample = _final(x, moe_out, mod_tiles[DEPTH - 1], final_g[None, :])
    y_prompt = y_prompt.reshape(BATCH, SEQ, D_MODEL)
    y_sample = y_sample.reshape(DEC_BATCH, DEC_SEQ, D_MODEL)
    return (y_prompt, y_sample, jnp.stack(k_list, axis=1), jnp.stack(v_list, axis=1),
            jnp.stack(sf_list, axis=1), jnp.stack(sb_list, axis=1))
```

```python
import functools

import numpy as np
import jax
import jax.numpy as jnp
from jax import lax
from jax.experimental import pallas as pl
from jax.experimental.pallas import tpu as pltpu
from jax.experimental.pallas import tpu_sc as plsc

F32 = jnp.float32
BF16 = jnp.bfloat16

D_MODEL = 1024
BATCH = 32
SEQ = 256
DEPTH = 2
DEC_BATCH = 2
DEC_SEQ = 1024
PAST_LEN = 512
GRID_W = 64
NA_HEADS = 8
NA_HEAD_DIM = 64
NA_WIDTH = NA_HEADS * NA_HEAD_DIM
NA_KH = 8
NA_KW = 16
HG_HEADS = 4
HG_DK = 64
HG_DV = 64
HG_WIDTH = HG_HEADS * HG_DV
HG_CHUNK = 32
F_FLOOR = 1e-30
GM_GROUPS = 4
GM_GDIM = 64
GM_WIDTH = GM_GROUPS * GM_GDIM
GM_CHUNK = 128
IN_COLS = 3 * NA_WIDTH + 5 * HG_WIDTH + 2 * GM_WIDTH
N_EXPERTS = 32
TOP_K = 4
SWIGLU_LIMIT = 7.0
SWIGLU_ALPHA = 1.702
RMS_EPS = 1e-6
NEG_INF = -1e30

T_PROMPT = BATCH * SEQ
T_SAMPLE = DEC_BATCH * DEC_SEQ
T_ALL = T_PROMPT + T_SAMPLE
TM = 256
N_TILES = T_ALL // TM
P_TILES = T_PROMPT // TM
MOE_BM = 256
MOE_SLOTS = -(-(T_ALL * TOP_K + N_EXPERTS * (MOE_BM - 1)) // MOE_BM) * MOE_BM
MOE_BLOCKS = MOE_SLOTS // MOE_BM
MOD_ROWS = 8
V7X_VMEM_LIMIT = 48 * 1024 * 1024

_CB_HQ, _CB_ZF, _CB_ZB, _CB_HI, _CB_HG, _CB_GU, _CB_GV = 6, 7, 8, 9, 10, 11, 12


def _dot(a, b):
    return jnp.dot(a, b, preferred_element_type=F32)


def _dot_nt(a, b):
    return lax.dot_general(a, b, (((1,), (1,)), ((), ())), preferred_element_type=F32)


def _dot_tn(a, b):
    return lax.dot_general(a, b, (((0,), (0,)), ((), ())), preferred_element_type=F32)


def _split3(x):
    hi = x.astype(BF16)
    r1 = x - hi.astype(F32)
    mid = r1.astype(BF16)
    lo = (r1 - mid.astype(F32)).astype(BF16)
    return hi, mid, lo


D_PACK = D_MODEL // 2
N_SPLIT = 2
D_SLAB = D_PACK // N_SPLIT


def _pack_halves(x):
    half = x.shape[1] // 2
    lo = pltpu.bitcast(x[:, :half].astype(BF16).astype(F32), jnp.uint32)
    hi = pltpu.bitcast(x[:, half:].astype(BF16).astype(F32), jnp.uint32)
    return pltpu.bitcast(jnp.right_shift(lo, jnp.uint32(16)) | hi, jnp.int32)


def _unpack_halves(w):
    u = pltpu.bitcast(w, jnp.uint32)
    lo = pltpu.bitcast(jnp.left_shift(u, jnp.uint32(16)), F32)
    hi = pltpu.bitcast(u & jnp.uint32(0xFFFF0000), F32)
    return lo, hi


def _load_slabs(refs, *lead):
    return jnp.concatenate([r[lead] if lead else r[...] for r in refs], axis=1)


def _store_slabs(refs, packed):
    for si, r in enumerate(refs):
        r[...] = packed[:, si * D_SLAB:(si + 1) * D_SLAB]


def _params(n_axes=1):
    return pltpu.CompilerParams(dimension_semantics=("arbitrary",) * n_axes,
                                vmem_limit_bytes=V7X_VMEM_LIMIT)


def _mod_kernel(cond_ref, w_ref, b_ref, o_ref):
    c = cond_ref[...]
    c = c * jax.nn.sigmoid(c)
    w = w_ref[0]
    c_hi = c.astype(BF16)
    c_lo = (c - c_hi.astype(F32)).astype(BF16)
    w_hi = w.astype(BF16)
    w_lo = (w - w_hi.astype(F32)).astype(BF16)
    o_ref[0] = _dot(c_hi, w_hi) + _dot(c_lo, w_hi) + _dot(c_hi, w_lo) + b_ref[0]


def _modulation(cond, w_mod, b_mod):
    tn = 1536
    return pl.pallas_call(
        _mod_kernel,
        out_shape=jax.ShapeDtypeStruct((DEPTH, MOD_ROWS, 6 * D_MODEL), F32),
        grid=(DEPTH, 6 * D_MODEL // tn),
        in_specs=[pl.BlockSpec((MOD_ROWS, D_MODEL), lambda l, j: (0, 0)),
                  pl.BlockSpec((1, D_MODEL, tn), lambda l, j: (l, 0, j)),
                  pl.BlockSpec((1, 1, tn), lambda l, j: (l, 0, j))],
        out_specs=pl.BlockSpec((1, MOD_ROWS, tn), lambda l, j: (l, 0, j)),
        compiler_params=_params(2),
        name="modulation",
    )(cond, w_mod, b_mod.reshape(DEPTH, 1, 6 * D_MODEL))


def _rms_mod(x, g, shift, scale):
    ms = jnp.mean(x * x, axis=-1, keepdims=True)
    y = x * lax.rsqrt(ms + RMS_EPS) * g
    return y * (1.0 + scale) + shift


def _project_in(hm, w_ref, h_ref, kc_ref, vc_ref):
    h = _dot(hm.astype(BF16), w_ref[...])
    h_ref[...] = h

    @pl.when(pl.program_id(0) < P_TILES)
    def _():
        kc_ref[0] = h[:, NA_WIDTH:2 * NA_WIDTH]
        vc_ref[0] = h[:, 2 * NA_WIDTH:3 * NA_WIDTH]


def _pick_group(p_ref, s_ref):
    return jnp.where(pl.program_id(0) < P_TILES, p_ref[...], s_ref[...])


def _p_tile(width):
    return pl.BlockSpec((TM, width), lambda i: (jnp.minimum(i, P_TILES - 1), 0))


def _s_tile(width):
    return pl.BlockSpec((TM, width), lambda i: (jnp.maximum(i - P_TILES, 0), 0))


def _inproj_first_kernel(xp_ref, xs_ref, mod_ref, g_ref, w_ref, h_ref, xo_ref, kc_ref, vc_ref):
    x = _pick_group(xp_ref, xs_ref)
    xo_ref[...] = x
    hm = _rms_mod(x, g_ref[...], mod_ref[0, 0:1, :], mod_ref[0, 1:2, :])
    _project_in(hm, w_ref, h_ref, kc_ref, vc_ref)


def _combine_experts(yg_refs, gate_ref):
    gates = gate_ref[...]
    lo_acc = hi_acc = None
    for kk in range(TOP_K):
        lo, hi = _unpack_halves(_load_slabs(yg_refs, kk))
        gk = gates[:, kk:kk + 1]
        lo_acc = gk * lo if lo_acc is None else lo_acc + gk * lo
        hi_acc = gk * hi if hi_acc is None else hi_acc + gk * hi
    return jnp.concatenate([lo_acc, hi_acc], axis=1)


def _inproj_next_kernel(x_ref, yga_ref, ygb_ref, gate_ref, pmod_ref, mod_ref, g_ref, w_ref,
                        h_ref, xo_ref, kc_ref, vc_ref):
    x = x_ref[...] + pmod_ref[0, 5:6, :] * _combine_experts((yga_ref, ygb_ref), gate_ref)
    xo_ref[...] = x
    hm = _rms_mod(x, g_ref[...], mod_ref[0, 0:1, :], mod_ref[0, 1:2, :])
    _project_in(hm, w_ref, h_ref, kc_ref, vc_ref)


_TILE_SPEC = pl.BlockSpec((TM, D_MODEL), lambda i: (i, 0))
_MOD_SPEC = pl.BlockSpec((1, MOD_ROWS, D_MODEL), lambda i: (i, 0, 0))
_ROW_SPEC = pl.BlockSpec((1, D_MODEL), lambda i: (0, 0))
_RT_LANES = 128
_YG_SPEC = pl.BlockSpec((TOP_K, TM, D_SLAB), lambda i: (0, i, 0))
_GATE_SPEC = pl.BlockSpec((TM, _RT_LANES), lambda i: (i, 0))


def _inproj(x, moe, prev_mod, mod, g, w_bf16):
    w_spec = pl.BlockSpec((D_MODEL, IN_COLS), lambda i: (0, 0))
    h_spec = pl.BlockSpec((TM, IN_COLS), lambda i: (i, 0))
    h_shape = jax.ShapeDtypeStruct((T_ALL, IN_COLS), F32)
    c_spec = pl.BlockSpec((1, SEQ, NA_WIDTH), lambda i: (jnp.minimum(i, P_TILES - 1), 0, 0))
    c_shape = jax.ShapeDtypeStruct((BATCH, SEQ, NA_WIDTH), F32)
    x_shape = jax.ShapeDtypeStruct((T_ALL, D_MODEL), F32)
    if moe is None:
        h, x, kc, vc = pl.pallas_call(
            _inproj_first_kernel, out_shape=(h_shape, x_shape, c_shape, c_shape), grid=(N_TILES,),
            in_specs=[_p_tile(D_MODEL), _s_tile(D_MODEL), _MOD_SPEC, _ROW_SPEC, w_spec],
            out_specs=(h_spec, _TILE_SPEC, c_spec, c_spec),
            compiler_params=_params(), name="inproj_first",
        )(*x, mod, g, w_bf16)
        return h, x, (kc, vc)
    h, x, kc, vc = pl.pallas_call(
        _inproj_next_kernel,
        out_shape=(h_shape, x_shape, c_shape, c_shape),
        grid=(N_TILES,),
        in_specs=[_TILE_SPEC, _YG_SPEC, _YG_SPEC, _GATE_SPEC, _MOD_SPEC, _MOD_SPEC, _ROW_SPEC, w_spec],
        out_specs=(h_spec, _TILE_SPEC, c_spec, c_spec),
        compiler_params=_params(), name="inproj_next",
    )(x, *moe[0], moe[1], prev_mod, mod, g, w_bf16)
    return h, x, (kc, vc)


def _pair_mask(hh):
    lane = lax.broadcasted_iota(jnp.int32, (1, 2 * NA_HEAD_DIM), 1)
    return (lane >= hh * NA_HEAD_DIM) & (lane < (hh + 1) * NA_HEAD_DIM)


def _attn_prompt_kernel(q_ref, k_ref, v_ref, o_ref):
    scale = NA_HEAD_DIM ** -0.5
    for p in range(NA_HEADS // 2):
        cols = slice(p * 128, (p + 1) * 128)
        qp = q_ref[:, cols] * scale
        kp = k_ref[:, cols].astype(BF16)
        vp = v_ref[:, cols].astype(BF16)
        outs = []
        for hh in range(2):
            qh = jnp.where(_pair_mask(hh), qp, 0.0).astype(BF16)
            s = _dot_nt(qh, kp)
            e = jnp.exp(s - jnp.max(s, axis=-1, keepdims=True))
            den = jnp.sum(e, axis=-1, keepdims=True)
            outs.append(_dot(e.astype(BF16), vp) / den)
        o_ref[:, cols] = jnp.where(_pair_mask(0), outs[0], outs[1]).astype(o_ref.dtype)


def _attn_prompt(h):
    return pl.pallas_call(
        _attn_prompt_kernel,
        out_shape=jax.ShapeDtypeStruct((T_PROMPT, NA_WIDTH), BF16),
        grid=(BATCH,),
        in_specs=[pl.BlockSpec((SEQ, NA_WIDTH), lambda b: (b, 0)),
                  pl.BlockSpec((SEQ, NA_WIDTH), lambda b: (b, 1)),
                  pl.BlockSpec((SEQ, NA_WIDTH), lambda b: (b, 2))],
        out_specs=pl.BlockSpec((SEQ, NA_WIDTH), lambda b: (b, 0)),
        compiler_params=_params(), name="attn_prompt",
    )(h, h, h)


_NA_ROWS = DEC_SEQ // GRID_W
_NA_LOC = NA_KH * GRID_W


def _na_window_start(r):
    return jnp.clip(r - NA_KH // 2, 0, _NA_ROWS - NA_KH)


def _attn_sample_kernel(q_ref, k_ref, v_ref, ck_ref, cv_ref, bias_ref, o_ref):
    s0 = pl.multiple_of(_na_window_start(pl.program_id(1)) * GRID_W, GRID_W)
    scale = NA_HEAD_DIM ** -0.5
    for p in range(NA_HEADS // 2):
        cols = slice(p * 128, (p + 1) * 128)
        qp = q_ref[:, cols] * scale
        kl = k_ref[pl.ds(s0, _NA_LOC), cols].astype(BF16)
        vl = v_ref[pl.ds(s0, _NA_LOC), cols].astype(BF16)
        kc = ck_ref[0, :, cols].astype(BF16)
        vc = cv_ref[0, :, cols].astype(BF16)
        outs = []
        for hh in range(2):
            qh = jnp.where(_pair_mask(hh), qp, 0.0).astype(BF16)
            sl = _dot_nt(qh, kl) + bias_ref[0, 2 * p + hh]
            sc = _dot_nt(qh, kc)
            mx = jnp.maximum(jnp.max(sl, axis=-1, keepdims=True),
                             jnp.max(sc, axis=-1, keepdims=True))
            el = jnp.exp(sl - mx)
            ec = jnp.exp(sc - mx)
            den = jnp.sum(el, axis=-1, keepdims=True) + jnp.sum(ec, axis=-1, keepdims=True)
            outs.append((_dot(el.astype(BF16), vl) + _dot(ec.astype(BF16), vc)) / den)
        o_ref[:, cols] = jnp.where(_pair_mask(0), outs[0], outs[1]).astype(o_ref.dtype)


def _attn_sample(h, ck, cv, bias):
    q_row0 = T_PROMPT // GRID_W
    kv_row0 = T_PROMPT // DEC_SEQ
    return pl.pallas_call(
        _attn_sample_kernel,
        out_shape=jax.ShapeDtypeStruct((T_SAMPLE, NA_WIDTH), BF16),
        grid=(DEC_BATCH, _NA_ROWS),
        in_specs=[pl.BlockSpec((GRID_W, NA_WIDTH), lambda b, r: (q_row0 + b * _NA_ROWS + r, 0)),
                  pl.BlockSpec((DEC_SEQ, NA_WIDTH), lambda b, r: (kv_row0 + b, 1)),
                  pl.BlockSpec((DEC_SEQ, NA_WIDTH), lambda b, r: (kv_row0 + b, 2)),
                  pl.BlockSpec((1, PAST_LEN, NA_WIDTH), lambda b, r: (b, 0, 0)),
                  pl.BlockSpec((1, PAST_LEN, NA_WIDTH), lambda b, r: (b, 0, 0)),
                  pl.BlockSpec((1, NA_HEADS, GRID_W, _NA_LOC), lambda b, r: (_na_window_start(r) - r + NA_KH - 1, 0, 0, 0))],
        out_specs=pl.BlockSpec((GRID_W, NA_WIDTH), lambda b, r: (b * _NA_ROWS + r, 0)),
        compiler_params=_params(2), name="attn_sample",
    )(h, h, h, ck, cv, bias)


_NA_DR = 2 * NA_KH - 1
_NA_DC = 2 * NA_KW - 1
_NA_DC_PAD = 32


def _bias_expand_kernel(rb_ref, onehot_ref, inwin_ref, o_ref):
    hi, mid, lo = _split3(rb_ref[...])
    oh = onehot_ref[...]
    e = _dot(hi, oh) + _dot(mid, oh) + _dot(lo, oh)
    o_ref[...] = jnp.where(inwin_ref[...] != 0.0, e, NEG_INF)


def _na_bias_tables(rel_bias):
    qc = np.arange(GRID_W)
    kc = np.arange(GRID_W)
    q_start = np.clip(qc - NA_KW // 2, 0, GRID_W - NA_KW)
    in_win = (kc[None, :] >= q_start[:, None]) & (kc[None, :] < q_start[:, None] + NA_KW)
    dc = np.clip(kc[None, :] - qc[:, None] + NA_KW - 1, 0, _NA_DC - 1)
    onehot = (np.arange(_NA_DC_PAD)[:, None] == dc.reshape(1, -1)).astype(np.float32)
    n_rows = DEPTH * NA_HEADS * _NA_DR
    rb = jnp.pad(rel_bias.astype(F32).reshape(n_rows, _NA_DC), ((0, 0), (0, _NA_DC_PAD - _NA_DC)))
    full = lambda shape: pl.BlockSpec(shape, lambda i: (0, 0))
    e = pl.pallas_call(
        _bias_expand_kernel,
        out_shape=jax.ShapeDtypeStruct((n_rows, GRID_W * GRID_W), F32), grid=(1,),
        in_specs=[full((n_rows, _NA_DC_PAD)), full((_NA_DC_PAD, GRID_W * GRID_W)), full((1, GRID_W * GRID_W))],
        out_specs=full((n_rows, GRID_W * GRID_W)),
        compiler_params=_params(), name="na_bias_expand",
    )(rb, jnp.asarray(onehot, BF16), jnp.asarray(in_win.reshape(1, -1), F32))
    e = e.reshape(DEPTH, NA_HEADS, _NA_DR, GRID_W, GRID_W)
    w = jnp.stack([e[:, :, b:b + NA_KH] for b in range(NA_KH)], axis=1)
    return w.transpose(0, 1, 2, 4, 3, 5).reshape(DEPTH, NA_KH, NA_HEADS, GRID_W, _NA_LOC)


_HG_GROUP = 8


def _hgrn_kernel(*refs, n_tok, has_state):
    if has_state:
        (q_ref, zf_ref, zb_ref, v_ref, g_ref, lbf_ref, lbb_ref, og_ref, s0f_ref, s0b_ref,
         rec_ref, sf_ref, sb_ref, kf_s, bf_s, kb_s, bb_s, of_s, ob_s, zf_s, zb_s, qsf_s, qsb_s) = refs
    else:
        (q_ref, zf_ref, zb_ref, v_ref, g_ref, lbf_ref, lbb_ref, og_ref,
         rec_ref, sf_ref, sb_ref, kf_s, bf_s, kb_s, bb_s, of_s, ob_s, zf_s, zb_s, qsf_s, qsb_s) = refs
        s0f_ref = s0b_ref = None
    C = HG_CHUNK
    W = HG_WIDTH
    n_chunks = n_tok // C
    rr = lax.broadcasted_iota(jnp.int32, (W, W), 0)
    cc = lax.broadcasted_iota(jnp.int32, (W, W), 1)
    same_chunk = jnp.right_shift(rr, 5) == jnp.right_shift(cc, 5)
    tri_prefix = jnp.where(same_chunk & (cc <= rr), 1.0, 0.0).astype(BF16)
    tri_suffix = jnp.where(same_chunk & (cc >= rr), 1.0, 0.0).astype(BF16)
    same_head = jnp.right_shift(rr, 6) == jnp.right_shift(cc, 6)
    head_ones = jnp.where(same_head, 1.0, 0.0).astype(BF16)

    for ti in range(n_tok // W):
        rows = slice(ti * W, (ti + 1) * W)
        for z_ref, lb_ref, k_s, b_s, tri in ((zf_ref, lbf_ref, kf_s, bf_s, tri_prefix),
                                             (zb_ref, lbb_ref, kb_s, bb_s, tri_suffix)):
            z = z_ref[rows, :]
            lb = lb_ref[...]
            f = lb + (1.0 - lb) * jax.nn.sigmoid(z)
            logf = jnp.log(jnp.maximum(f, F_FLOOR))
            k_s[rows, :] = (1.0 - lb) * jax.nn.sigmoid(-z)
            hi, mid, lo = _split3(logf)
            b_s[rows, :] = _dot(tri, hi) + _dot(tri, mid) + _dot(tri, lo)

    G = _HG_GROUP
    n_groups = C // G
    srow = lax.broadcasted_iota(jnp.int32, (G, W), 0)
    zf_s[...] = jnp.zeros_like(zf_s)
    zb_s[...] = jnp.zeros_like(zb_s)

    def scan_chunk(ci, k_s, b_s, z_s, qs_s, st_ref, o_dir_s, fwd):
        c = ci if fwd else n_chunks - 1 - ci
        base = pl.multiple_of(c * C, C)
        q = q_ref[pl.ds(base, C), :]
        k = k_s[pl.ds(base, C), :]
        b = b_s[pl.ds(base, C), :]
        v = v_ref[pl.ds(base, C), :]
        k_far = {}
        for gt in range(n_groups):
            others = range(gt) if fwd else range(gt + 1, n_groups)
            if not others:
                continue
            rows_t = slice(gt * G, (gt + 1) * G)
            edge = gt * G - 1 if fwd else (gt + 1) * G
            b_edge = b[edge:edge + 1, :]
            qs_s[rows_t, :] = q[rows_t] * jnp.exp(b[rows_t] - b_edge)
            for gs in others:
                rows_s = slice(gs * G, (gs + 1) * G)
                k_far[gt, gs] = k[rows_s] * jnp.exp(b_edge - b[rows_s])
        for t in range(C):
            gt = t // G
            rows_t = slice(gt * G, (gt + 1) * G)
            qt = q_ref[pl.ds(base + t, 1), :]
            bt = b_s[pl.ds(base + t, 1), :]
            keep = (srow + gt * G <= t) if fwd else (srow + gt * G >= t)
            z_s[t * C + gt * G:t * C + (gt + 1) * G, :] = jnp.where(
                keep, (qt * k[rows_t]) * jnp.exp(bt - b[rows_t]), 0.0)
            others = range(gt) if fwd else range(gt + 1, n_groups)
            if others:
                qst = qs_s[t:t + 1, :]
                for gs in others:
                    z_s[t * C + gs * G:t * C + (gs + 1) * G, :] = qst * k_far[gt, gs]
        a_rep = _dot(z_s[...].astype(BF16), head_ones)
        o_intra = jnp.sum(a_rep.reshape(C, C, W) * v[None, :, :], axis=1)
        b_end = b_s[pl.ds(base + (C - 1 if fwd else 0), 1), :]
        q_in = q * jnp.exp(b)
        k_st = k * jnp.exp(b_end - b)
        st = st_ref[0]
        o_inter = _dot_nt(q_in.astype(BF16), st.astype(BF16))
        upd = _dot_tn(v.astype(BF16), k_st.astype(BF16))
        st_ref[0] = st * jnp.exp(b_end) + jnp.where(same_head, upd, 0.0)
        o_dir_s[pl.ds(base, C), :] = o_intra + o_inter

    sf_ref[0] = s0f_ref[0] if has_state else jnp.zeros((W, W), F32)
    sb_ref[0] = s0b_ref[0] if has_state else jnp.zeros((W, W), F32)

    def scan_both(ci, carry):
        scan_chunk(ci, kf_s, bf_s, zf_s, qsf_s, sf_ref, of_s, True)
        scan_chunk(ci, kb_s, bb_s, zb_s, qsb_s, sb_ref, ob_s, False)
        return carry
    lax.fori_loop(0, n_chunks, scan_both, 0)

    for ti in range(n_tok // W):
        rows = slice(ti * W, (ti + 1) * W)
        o = of_s[rows, :] + ob_s[rows, :]
        sq = o * o
        sq_hi = sq.astype(BF16)
        sq_lo = (sq - sq_hi.astype(F32)).astype(BF16)
        ms = (_dot(sq_hi, head_ones) + _dot(sq_lo, head_ones)) * (1.0 / HG_DV)
        g = g_ref[rows, :]
        y = o * lax.rsqrt(ms + RMS_EPS) * og_ref[...] * (g * jax.nn.sigmoid(g))
        rec_ref[rows, :] = y.astype(rec_ref.dtype)


def _hgrn(h, lbf, lbb, og, s0f_t, s0b_t, n_tok, n_seq, row0):
    W = HG_WIDTH
    has_state = s0f_t is not None

    def col(cb):
        return pl.BlockSpec((n_tok, W), lambda i, cb=cb: (row0 + i, cb))

    vec = pl.BlockSpec((1, W), lambda i: (0, 0))
    st_spec = pl.BlockSpec((1, W, W), lambda i: (i, 0, 0))
    in_specs = [col(_CB_HQ), col(_CB_ZF), col(_CB_ZB), col(_CB_HI), col(_CB_HG), vec, vec, vec]
    args = [h, h, h, h, h, lbf, lbb, og]
    if has_state:
        in_specs += [st_spec, st_spec]
        args += [s0f_t, s0b_t]
    seq_f32 = pltpu.VMEM((n_tok, W), F32)
    return pl.pallas_call(
        functools.partial(_hgrn_kernel, n_tok=n_tok, has_state=has_state),
        out_shape=(jax.ShapeDtypeStruct((n_seq * n_tok, W), BF16),
                   jax.ShapeDtypeStruct((n_seq, W, W), F32),
                   jax.ShapeDtypeStruct((n_seq, W, W), F32)),
        grid=(n_seq,),
        in_specs=in_specs,
        out_specs=(pl.BlockSpec((n_tok, W), lambda i: (i, 0)), st_spec, st_spec),
        scratch_shapes=[seq_f32, seq_f32, seq_f32, seq_f32, seq_f32, seq_f32,
                        pltpu.VMEM((HG_CHUNK * HG_CHUNK, W), F32),
                        pltpu.VMEM((HG_CHUNK * HG_CHUNK, W), F32),
                        pltpu.VMEM((HG_CHUNK, W), F32),
                        pltpu.VMEM((HG_CHUNK, W), F32)],
        compiler_params=_params(), name="hgrn_state" if has_state else "hgrn_zero",
    )(*args)


def _state_to_blockdiag_t(s):
    eye = jnp.eye(HG_HEADS, dtype=s.dtype)
    return jnp.einsum('bhkv,hg->bhvgk', s, eye).reshape(s.shape[0], HG_WIDTH, HG_HEADS * HG_DK)


def _blockdiag_t_to_state(st):
    s5 = st.reshape(st.shape[0], HG_HEADS, HG_DV, HG_HEADS, HG_DK)
    return jnp.stack([s5[:, hh, :, hh, :] for hh in range(HG_HEADS)], axis=1).transpose(0, 1, 3, 2)


def _gmlp_kernel(u_ref, v_ref, g_ref, ws_ref, b_ref, o_ref):
    v = v_ref[...]
    ms = jnp.mean(v * v, axis=-1, keepdims=True)
    vn = (v * lax.rsqrt(ms + RMS_EPS) * g_ref[...]).astype(BF16)
    lane = lax.broadcasted_iota(jnp.int32, (1, GM_WIDTH), 1)
    z = b_ref[...]
    for gi in range(GM_GROUPS):
        zg = _dot(ws_ref[gi], vn)
        in_group = (lane >= gi * GM_GDIM) & (lane < (gi + 1) * GM_GDIM)
        z = z + jnp.where(in_group, zg, 0.0)
    o_ref[...] = (u_ref[...] * z).astype(o_ref.dtype)


def _gmlp(h, vnorm_g, ws_bf16, bias_full):
    W = GM_WIDTH
    return pl.pallas_call(
        _gmlp_kernel,
        out_shape=jax.ShapeDtypeStruct((T_ALL, W), BF16),
        grid=(T_ALL // GM_CHUNK,),
        in_specs=[pl.BlockSpec((GM_CHUNK, W), lambda i: (i, _CB_GU)),
                  pl.BlockSpec((GM_CHUNK, W), lambda i: (i, _CB_GV)),
                  pl.BlockSpec((1, W), lambda i: (0, 0)),
                  pl.BlockSpec((GM_GROUPS, GM_CHUNK, GM_CHUNK), lambda i: (0, 0, 0)),
                  pl.BlockSpec((GM_CHUNK, W), lambda i: (0, 0))],
        out_specs=pl.BlockSpec((GM_CHUNK, W), lambda i: (i, 0)),
        compiler_params=_params(), name="gmlp",
    )(h, h, vnorm_g, ws_bf16, bias_full)


def _outproj_kernel(attp_ref, atts_ref, recp_ref, recs_ref, mlp_ref, x_ref, mod_ref, g_ref, w_ref, wr_ref, br_ref,
                    x1_ref, h2a_ref, h2b_ref, idx_ref, gate_ref, cnt_ref):
    @pl.when(pl.program_id(0) == 0)
    def _():
        cnt_ref[...] = jnp.zeros_like(cnt_ref)

    out = (_dot(_pick_group(attp_ref, atts_ref), w_ref[0:NA_WIDTH, :])
           + _dot(_pick_group(recp_ref, recs_ref), w_ref[NA_WIDTH:NA_WIDTH + HG_WIDTH, :])
           + _dot(mlp_ref[...], w_ref[NA_WIDTH + HG_WIDTH:, :]))
    x1 = x_ref[...] + mod_ref[0, 2:3, :] * out
    x1_ref[...] = x1
    h2 = _rms_mod(x1, g_ref[...], mod_ref[0, 3:4, :], mod_ref[0, 4:5, :])
    _store_slabs((h2a_ref, h2b_ref), _pack_halves(h2))
    h_hi = h2.astype(BF16)
    h_lo = (h2 - h_hi.astype(F32)).astype(BF16)
    wr = wr_ref[...]
    w_hi = wr.astype(BF16)
    w_lo = (wr - w_hi.astype(F32)).astype(BF16)
    logits = _dot(h_hi, w_hi) + _dot(h_lo, w_hi) + _dot(h_hi, w_lo) + br_ref[...]
    lane_e = lax.broadcasted_iota(jnp.int32, (TM, N_EXPERTS), 1).astype(F32)
    lane_o = lax.broadcasted_iota(jnp.int32, (TM, _RT_LANES), 1)
    idx_acc = jnp.zeros((TM, _RT_LANES), F32)
    val_acc = jnp.zeros((TM, _RT_LANES), F32)
    top0 = None
    den = jnp.zeros((TM, 1), F32)
    work = logits
    picks = []
    for kk in range(TOP_K):
        m = jnp.max(work, axis=-1, keepdims=True)
        first = jnp.min(jnp.where(work == m, lane_e, float(N_EXPERTS)), axis=-1, keepdims=True)
        if kk == 0:
            top0 = m
        e = jnp.exp(m - top0)
        den = den + e
        idx_acc = jnp.where(lane_o == kk, first, idx_acc)
        val_acc = jnp.where(lane_o == kk, e, val_acc)
        picks.append(lane_e == first)
        work = jnp.where(picks[-1], -jnp.inf, work)
    gate_ref[...] = val_acc / den
    sel = jnp.zeros((TM, N_EXPERTS), F32)
    for pk in picks:
        sel = sel + jnp.where(pk, 1.0, 0.0)
    rr = lax.broadcasted_iota(jnp.int32, (TM, TM), 0)
    cc = lax.broadcasted_iota(jnp.int32, (TM, TM), 1)
    earlier = jnp.where(cc < rr, 1.0, 0.0).astype(BF16)
    seen = cnt_ref[0:1, 0:N_EXPERTS]
    before = _dot(earlier, sel.astype(BF16)) + seen
    for kk, pk in enumerate(picks):
        rank = jnp.sum(jnp.where(pk, before, 0.0), axis=-1, keepdims=True)
        idx_acc = jnp.where(lane_o == TOP_K + kk, rank, idx_acc)
    idx_ref[...] = idx_acc.astype(jnp.int32)
    cnt_ref[0:1, 0:N_EXPERTS] = seen + jnp.sum(sel, axis=0, keepdims=True)


def _outproj(att_p, att_s, rec_p, rec_s, mlp, x, mod, g, w_bf16, wr, br):
    def tile(width):
        return pl.BlockSpec((TM, width), lambda i: (i, 0))

    return pl.pallas_call(
        _outproj_kernel,
        out_shape=(jax.ShapeDtypeStruct((T_ALL, D_MODEL), F32),
                   jax.ShapeDtypeStruct((T_ALL, D_SLAB), jnp.int32),
                   jax.ShapeDtypeStruct((T_ALL, D_SLAB), jnp.int32),
                   jax.ShapeDtypeStruct((T_ALL, _RT_LANES), jnp.int32),
                   jax.ShapeDtypeStruct((T_ALL, _RT_LANES), F32),
                   jax.ShapeDtypeStruct((8, _RT_LANES), F32)),
        grid=(N_TILES,),
        in_specs=[_p_tile(NA_WIDTH), _s_tile(NA_WIDTH), _p_tile(HG_WIDTH), _s_tile(HG_WIDTH),
                  tile(GM_WIDTH), _TILE_SPEC, _MOD_SPEC, _ROW_SPEC,
                  pl.BlockSpec((D_MODEL, D_MODEL), lambda i: (0, 0)),
                  pl.BlockSpec((D_MODEL, N_EXPERTS), lambda i: (0, 0)),
                  pl.BlockSpec((1, N_EXPERTS), lambda i: (0, 0))],
        out_specs=(_TILE_SPEC, tile(D_SLAB), tile(D_SLAB), tile(_RT_LANES), tile(_RT_LANES),
                   pl.BlockSpec((8, _RT_LANES), lambda i: (0, 0))),
        compiler_params=_params(), name="outproj_router",
    )(att_p, att_s, rec_p, rec_s, mlp, x, mod, g, w_bf16, wr, br)


_W_CHUNKS = 4


def _moe_kernel(blk_e_ref, blk_on_ref, blk_new_ref, blk_slot_ref, blk_next_ref,
                xa_ref, xb_ref, wg_hbm, bg_ref, wu_hbm, bu_ref, wd_hbm, bd_ref,
                ya_ref, yb_ref, w_buf, w_sem, *, layer):
    j = pl.program_id(0)

    def weight_copies(expert, slot):
        rows = D_MODEL // _W_CHUNKS
        return [pltpu.make_async_copy(w_hbm.at[layer, expert, pl.ds(ci * rows, rows)],
                                      w_buf.at[slot, wi, pl.ds(ci * rows, rows)], w_sem.at[slot, wi, ci])
                for wi, w_hbm in enumerate((wg_hbm, wu_hbm, wd_hbm)) for ci in range(_W_CHUNKS)]

    @pl.when(j == 0)
    def _():
        for cp in weight_copies(blk_e_ref[0], 0):
            cp.start()

    @pl.when(blk_new_ref[j] != 0)
    def _():
        slot = blk_slot_ref[j]
        for cp in weight_copies(blk_e_ref[j], slot):
            cp.wait()

        @pl.when(blk_next_ref[j] >= 0)
        def _():
            for cp in weight_copies(blk_next_ref[j], 1 - slot):
                cp.start()

    @pl.when(blk_on_ref[j] != 0)
    def _():
        slot = blk_slot_ref[j]
        lo, hi = _unpack_halves(_load_slabs((xa_ref, xb_ref)))
        x = jnp.concatenate([lo.astype(BF16), hi.astype(BF16)], axis=1)
        gate = jnp.minimum(_dot(x, w_buf[slot, 0].astype(BF16)) + bg_ref[0, 0], SWIGLU_LIMIT)
        up = jnp.clip(_dot(x, w_buf[slot, 1].astype(BF16)) + bu_ref[0, 0], -SWIGLU_LIMIT, SWIGLU_LIMIT)
        glu = gate * jax.nn.sigmoid(SWIGLU_ALPHA * gate)
        act = ((up + 1.0) * glu).astype(BF16)
        _store_slabs((ya_ref, yb_ref), _pack_halves(_dot(act, w_buf[slot, 2].astype(BF16)) + bd_ref[0, 0]))

    @pl.when(blk_on_ref[j] == 0)
    def _():
        ya_ref[...] = jnp.zeros_like(ya_ref)
        yb_ref[...] = jnp.zeros_like(yb_ref)


def _moe(layer, plan, x_sorted, wg, bg, wu, bu, wd, bd):
    n_plan = len(plan)
    b_spec = pl.BlockSpec((1, 1, 1, D_MODEL), lambda j, be, *_: (layer, be[j], 0, 0))
    x_spec = pl.BlockSpec((MOE_BM, D_SLAB), lambda j, *_: (j, 0))
    hbm = pl.BlockSpec(memory_space=pl.ANY)
    bias4 = lambda b: b.reshape(DEPTH, N_EXPERTS, 1, D_MODEL)
    return pl.pallas_call(
        functools.partial(_moe_kernel, layer=layer),
        out_shape=(jax.ShapeDtypeStruct((MOE_SLOTS, D_SLAB), jnp.int32),) * N_SPLIT,
        grid_spec=pltpu.PrefetchScalarGridSpec(
            num_scalar_prefetch=n_plan, grid=(MOE_BLOCKS,),
            in_specs=[x_spec, x_spec, hbm, b_spec, hbm, b_spec, hbm, b_spec],
            out_specs=(x_spec, x_spec),
            scratch_shapes=[pltpu.VMEM((2, 3, D_MODEL, D_MODEL), F32),
                            pltpu.SemaphoreType.DMA((2, 3, _W_CHUNKS))]),
        compiler_params=_params(), name="moe_experts",
    )(*plan, *x_sorted, wg, bias4(bg), wu, bias4(bu), wd, bias4(bd))


def _route(top_i, rank, counts):
    experts = jnp.arange(N_EXPERTS, dtype=jnp.int32)
    nblk = (counts + MOE_BM - 1) // MOE_BM
    blk_end = jnp.cumsum(nblk)
    blk0 = blk_end - nblk
    start_of = jnp.sum(jnp.where(top_i[..., None] == experts, blk0 * MOE_BM, 0), axis=-1)
    dest = start_of + rank
    live = counts > 0
    last_live = jnp.max(jnp.where(live, experts, 0))
    later_live = live[None, :] & (experts[None, :] > experts[:, None])
    next_live = jnp.where(jnp.any(later_live, axis=1), jnp.argmax(later_live, axis=1), -1)
    ordinal = jnp.cumsum(live.astype(jnp.int32)) - 1
    blk = jnp.arange(MOE_BLOCKS, dtype=jnp.int32)
    blk_on = blk < blk_end[-1]
    blk_e = jnp.where(blk_on, jnp.minimum(jnp.sum((blk_end[None, :] <= blk[:, None]).astype(jnp.int32), axis=1),
                                          N_EXPERTS - 1), last_live)
    blk_new = blk_on & jnp.concatenate([jnp.ones((1,), bool), blk_e[1:] != blk_e[:-1]])
    plan = (blk_e, blk_on, blk_new, ordinal[blk_e] % 2, next_live[blk_e])
    return dest.astype(jnp.int32), tuple(p.astype(jnp.int32) for p in plan)


_SC_WINDOW = 128


def _sc_mesh():
    return plsc.VectorSubcoreMesh(core_axis_name="core", subcore_axis_name="subcore")


def _sc_scatter_rows(srcs, idx, n_out):
    n_src, width = srcs[0].shape
    n_idx = idx.shape[1]
    src_windows = n_src // _SC_WINDOW

    def body(*refs):
        x_hbm = refs[:len(srcs)]
        i_hbm = refs[len(srcs)]
        o_hbm = refs[len(srcs) + 1:]
        for xs, os_ in zip(x_hbm, o_hbm):
            def step(x_vmem, i_vmem, os_=os_):
                pltpu.sync_copy(x_vmem, os_.at[i_vmem.at[0]])

            pltpu.emit_pipeline(
                step, grid=(n_idx // _SC_WINDOW,),
                in_specs=[pl.BlockSpec((_SC_WINDOW, width), lambda i: (i % src_windows, 0)),
                          pl.BlockSpec((1, _SC_WINDOW), lambda i: (0, i))],
                out_specs=[],
                core_axis_name=("core", "subcore"),
                dimension_semantics=(pltpu.PARALLEL,),
            )(xs, i_hbm)

    out_type = tuple(jax.ShapeDtypeStruct((n_out, width), s.dtype) for s in srcs)
    return pl.kernel(body, out_type=out_type, mesh=_sc_mesh(), scratch_types=[],
                     name="sc_scatter_rows")(*srcs, idx)


def _sc_gather_rows(tables, idx):
    n_idx = idx.shape[1]
    width = tables[0].shape[1]

    def body(*refs):
        t_hbm = refs[:len(tables)]
        i_hbm = refs[len(tables)]
        o_hbm = refs[len(tables) + 1:]
        for ts, os_ in zip(t_hbm, o_hbm):
            def step(i_vmem, o_vmem, ts=ts):
                pltpu.sync_copy(ts.at[i_vmem.at[0]], o_vmem)

            pltpu.emit_pipeline(
                step, grid=(n_idx // _SC_WINDOW,),
                in_specs=[pl.BlockSpec((1, _SC_WINDOW), lambda i: (0, i))],
                out_specs=[pl.BlockSpec((_SC_WINDOW, width), lambda i: (i, 0))],
                core_axis_name=("core", "subcore"),
                dimension_semantics=(pltpu.PARALLEL,),
            )(i_hbm, os_)

    out_type = tuple(jax.ShapeDtypeStruct((n_idx, width), t.dtype) for t in tables)
    return pl.kernel(body, out_type=out_type, mesh=_sc_mesh(), scratch_types=[],
                     name="sc_gather_rows")(*tables, idx)


def _final_kernel(x_ref, yga_ref, ygb_ref, gate_ref, mod_ref, g_ref, yp_ref, ys_ref):
    x = x_ref[...] + mod_ref[0, 5:6, :] * _combine_experts((yga_ref, ygb_ref), gate_ref)
    ms = jnp.mean(x * x, axis=-1, keepdims=True)
    y = x * lax.rsqrt(ms + RMS_EPS) * g_ref[...]

    @pl.when(pl.program_id(0) < P_TILES)
    def _():
        yp_ref[...] = y

    @pl.when(pl.program_id(0) >= P_TILES)
    def _():
        ys_ref[...] = y


def _final(x, moe, mod, g):
    return pl.pallas_call(
        _final_kernel,
        out_shape=(jax.ShapeDtypeStruct((T_PROMPT, D_MODEL), F32), jax.ShapeDtypeStruct((T_SAMPLE, D_MODEL), F32)),
        grid=(N_TILES,),
        in_specs=[_TILE_SPEC, _YG_SPEC, _YG_SPEC, _GATE_SPEC, _MOD_SPEC, _ROW_SPEC],
        out_specs=(_p_tile(D_MODEL), _s_tile(D_MODEL)),
        compiler_params=_params(), name="final_norm",
    )(x, *moe[0], moe[1], mod, g)


def kernel(x_prompt, x_sample, cache_k, cache_v, state_hgrn_fwd, state_hgrn_bwd, c, c_ctx, w_mod, b_mod, norm1_g, norm2_g, w_in, na_rel_bias, hgrn_lb, hgrn_onorm_g, gmlp_vnorm_g, gmlp_ws, gmlp_b, w_out, router_w, router_b, w_gate, b_gate, w_up, b_up, w_down, b_down, final_g):
    x = (x_prompt.reshape(T_PROMPT, D_MODEL), x_sample.reshape(T_SAMPLE, D_MODEL))

    cond = jnp.zeros((MOD_ROWS, D_MODEL), F32).at[0].set(c_ctx).at[1:1 + DEC_BATCH].set(c)
    mod = _modulation(cond, w_mod, b_mod)
    tile_row = np.concatenate([np.zeros(P_TILES, np.int32),
                               1 + np.arange(N_TILES - P_TILES, dtype=np.int32) // (DEC_SEQ // TM)])
    mod_tiles = mod[:, tile_row].reshape(DEPTH, N_TILES, 6, D_MODEL)
    mod_tiles = jnp.pad(mod_tiles, ((0, 0), (0, 0), (0, MOD_ROWS - 6), (0, 0)))

    lb_soft = jax.nn.softmax(hgrn_lb.astype(F32), axis=1)
    lower = jnp.cumsum(lb_soft, axis=1) - lb_soft[:, :1]

    na_bias = _na_bias_tables(na_rel_bias)

    k_list, v_list, sf_list, sb_list = [], [], [], []
    moe_out = None
    for l in range(DEPTH):
        h, x, (kc, vc) = _inproj(x, moe_out, mod_tiles[l - 1] if l else None, mod_tiles[l],
                                 norm1_g[l][None, :], w_in[l].astype(BF16))
        k_list.append(kc.reshape(BATCH, SEQ, NA_HEADS, NA_HEAD_DIM))
        v_list.append(vc.reshape(BATCH, SEQ, NA_HEADS, NA_HEAD_DIM))

        att_p = _attn_prompt(h)
        att_s = _attn_sample(h, cache_k[:, l].reshape(DEC_BATCH, PAST_LEN, NA_WIDTH),
                             cache_v[:, l].reshape(DEC_BATCH, PAST_LEN, NA_WIDTH), na_bias[l])
        lbf = lower[0, l][None, :]
        lbb = lower[1, l][None, :]
        og = jnp.tile(hgrn_onorm_g[l], HG_HEADS)[None, :]
        rec_p, sf_t, sb_t = _hgrn(h, lbf, lbb, og, None, None, SEQ, BATCH, 0)
        rec_s, _, _ = _hgrn(h, lbf, lbb, og, _state_to_blockdiag_t(state_hgrn_fwd[:, l].astype(F32)),
                            _state_to_blockdiag_t(state_hgrn_bwd[:, l].astype(F32)),
                            DEC_SEQ, DEC_BATCH, T_PROMPT // DEC_SEQ)
        sf_list.append(_blockdiag_t_to_state(sf_t))
        sb_list.append(_blockdiag_t_to_state(sb_t))
        gm_bias = jnp.repeat(gmlp_b[l].T, GM_GDIM, axis=1)
        mlp = _gmlp(h, gmlp_vnorm_g[l][None, :], gmlp_ws[l].astype(BF16), gm_bias)

        x, h2a, h2b, idx_pad, gate_pad, cnt = _outproj(att_p, att_s, rec_p, rec_s, mlp, x, mod_tiles[l],
                                                       norm2_g[l][None, :], w_out[l].astype(BF16),
                                                       router_w[l], router_b[l][None, :])
        dest, plan = _route(idx_pad[:, :TOP_K], idx_pad[:, TOP_K:2 * TOP_K],
                            cnt[0, :N_EXPERTS].astype(jnp.int32))
        dest_flat = dest.T.reshape(1, TOP_K * T_ALL)
        x_sorted = _sc_scatter_rows((h2a, h2b), dest_flat, MOE_SLOTS)
        y_sorted = _moe(l, plan, x_sorted, w_gate, b_gate, w_up, b_up, w_down, b_down)
        y_tok = _sc_gather_rows(y_sorted, dest_flat)
        moe_out = ([yt.reshape(TOP_K, T_ALL, D_SLAB) for yt in y_tok], gate_pad)

    y_prompt, y_sample = _final(x, moe_out, mod_tiles[DEPTH - 1], final_g[None, :])
    y_prompt = y_prompt.reshape(BATCH, SEQ, D_MODEL)
    y_sample = y_sample.reshape(DEC_BATCH, DEC_SEQ, D_MODEL)
    return (y_prompt, y_sample, jnp.stack(k_list, axis=1), jnp.stack(v_list, axis=1),
            jnp.stack(sf_list, axis=1), jnp.stack(sb_list, axis=1))
```

```python
import functools

import numpy as np
import jax
import jax.numpy as jnp
from jax import lax
from jax.experimental import pallas as pl
from jax.experimental.pallas import tpu as pltpu
from jax.experimental.pallas import tpu_sc as plsc

F32 = jnp.float32
BF16 = jnp.bfloat16

D_MODEL = 1024
BATCH = 32
SEQ = 256
DEPTH = 2
DEC_BATCH = 2
DEC_SEQ = 1024
PAST_LEN = 512
GRID_W = 64
NA_HEADS = 8
NA_HEAD_DIM = 64
NA_WIDTH = NA_HEADS * NA_HEAD_DIM
NA_KH = 8
NA_KW = 16
HG_HEADS = 4
HG_DK = 64
HG_DV = 64
HG_WIDTH = HG_HEADS * HG_DV
HG_CHUNK = 32
F_FLOOR = 1e-30
GM_GROUPS = 4
GM_GDIM = 64
GM_WIDTH = GM_GROUPS * GM_GDIM
GM_CHUNK = 128
IN_COLS = 3 * NA_WIDTH + 5 * HG_WIDTH + 2 * GM_WIDTH
N_EXPERTS = 32
TOP_K = 4
SWIGLU_LIMIT = 7.0
SWIGLU_ALPHA = 1.702
RMS_EPS = 1e-6
NEG_INF = -1e30

T_PROMPT = BATCH * SEQ
T_SAMPLE = DEC_BATCH * DEC_SEQ
T_ALL = T_PROMPT + T_SAMPLE
TM = 256
N_TILES = T_ALL // TM
P_TILES = T_PROMPT // TM
MOE_BM = 256
MOE_SLOTS = -(-(T_ALL * TOP_K + N_EXPERTS * (MOE_BM - 1)) // MOE_BM) * MOE_BM
MOE_BLOCKS = MOE_SLOTS // MOE_BM
MOD_ROWS = 8
V7X_VMEM_LIMIT = 48 * 1024 * 1024

_CB_HQ, _CB_ZF, _CB_ZB, _CB_HI, _CB_HG, _CB_GU, _CB_GV = 6, 7, 8, 9, 10, 11, 12


def _dot(a, b):
    return jnp.dot(a, b, preferred_element_type=F32)


def _dot_nt(a, b):
    return lax.dot_general(a, b, (((1,), (1,)), ((), ())), preferred_element_type=F32)


def _dot_tn(a, b):
    return lax.dot_general(a, b, (((0,), (0,)), ((), ())), preferred_element_type=F32)


def _split3(x):
    hi = x.astype(BF16)
    r1 = x - hi.astype(F32)
    mid = r1.astype(BF16)
    lo = (r1 - mid.astype(F32)).astype(BF16)
    return hi, mid, lo


D_PACK = D_MODEL // 2
N_SPLIT = 2
D_SLAB = D_PACK // N_SPLIT


def _pack_halves(x):
    half = x.shape[1] // 2
    lo = pltpu.bitcast(x[:, :half].astype(BF16).astype(F32), jnp.uint32)
    hi = pltpu.bitcast(x[:, half:].astype(BF16).astype(F32), jnp.uint32)
    return pltpu.bitcast(jnp.right_shift(lo, jnp.uint32(16)) | hi, jnp.int32)


def _unpack_halves(w):
    u = pltpu.bitcast(w, jnp.uint32)
    lo = pltpu.bitcast(jnp.left_shift(u, jnp.uint32(16)), F32)
    hi = pltpu.bitcast(u & jnp.uint32(0xFFFF0000), F32)
    return lo, hi


def _load_slabs(refs, *lead):
    return jnp.concatenate([r[lead] if lead else r[...] for r in refs], axis=1)


def _store_slabs(refs, packed):
    for si, r in enumerate(refs):
        r[...] = packed[:, si * D_SLAB:(si + 1) * D_SLAB]


def _params(n_axes=1):
    return pltpu.CompilerParams(dimension_semantics=("arbitrary",) * n_axes,
                                vmem_limit_bytes=V7X_VMEM_LIMIT)


def _mod_kernel(cond_ref, w_ref, b_ref, o_ref):
    c = cond_ref[...]
    c = c * jax.nn.sigmoid(c)
    w = w_ref[0]
    c_hi = c.astype(BF16)
    c_lo = (c - c_hi.astype(F32)).astype(BF16)
    w_hi = w.astype(BF16)
    w_lo = (w - w_hi.astype(F32)).astype(BF16)
    o_ref[0] = _dot(c_hi, w_hi) + _dot(c_lo, w_hi) + _dot(c_hi, w_lo) + b_ref[0]


def _modulation(cond, w_mod, b_mod):
    tn = 1536
    return pl.pallas_call(
        _mod_kernel,
        out_shape=jax.ShapeDtypeStruct((DEPTH, MOD_ROWS, 6 * D_MODEL), F32),
        grid=(DEPTH, 6 * D_MODEL // tn),
        in_specs=[pl.BlockSpec((MOD_ROWS, D_MODEL), lambda l, j: (0, 0)),
                  pl.BlockSpec((1, D_MODEL, tn), lambda l, j: (l, 0, j)),
                  pl.BlockSpec((1, 1, tn), lambda l, j: (l, 0, j))],
        out_specs=pl.BlockSpec((1, MOD_ROWS, tn), lambda l, j: (l, 0, j)),
        compiler_params=_params(2),
        name="modulation",
    )(cond, w_mod, b_mod.reshape(DEPTH, 1, 6 * D_MODEL))


def _rms_mod(x, g, shift, scale):
    ms = jnp.mean(x * x, axis=-1, keepdims=True)
    y = x * lax.rsqrt(ms + RMS_EPS) * g
    return y * (1.0 + scale) + shift


def _project_in(hm, w_ref, h_ref, kc_ref, vc_ref):
    h = _dot(hm.astype(BF16), w_ref[...])
    h_ref[...] = h

    @pl.when(pl.program_id(0) < P_TILES)
    def _():
        kc_ref[0, 0:SEQ] = h[:, NA_WIDTH:2 * NA_WIDTH]
        vc_ref[0, 0:SEQ] = h[:, 2 * NA_WIDTH:3 * NA_WIDTH]
        if kc_ref.shape[1] > SEQ:
            kc_ref[0, SEQ:] = jnp.zeros((kc_ref.shape[1] - SEQ, NA_WIDTH), F32)
            vc_ref[0, SEQ:] = jnp.zeros((vc_ref.shape[1] - SEQ, NA_WIDTH), F32)


def _pick_group(p_ref, s_ref):
    return jnp.where(pl.program_id(0) < P_TILES, p_ref[...], s_ref[...])


def _p_tile(width):
    return pl.BlockSpec((TM, width), lambda i: (jnp.minimum(i, P_TILES - 1), 0))


def _s_tile(width):
    return pl.BlockSpec((TM, width), lambda i: (jnp.maximum(i - P_TILES, 0), 0))


def _inproj_first_kernel(xp_ref, xs_ref, mod_ref, g_ref, w_ref, h_ref, xo_ref, kc_ref, vc_ref):
    x = _pick_group(xp_ref, xs_ref)
    xo_ref[...] = x
    hm = _rms_mod(x, g_ref[...], mod_ref[0, 0:1, :], mod_ref[0, 1:2, :])
    _project_in(hm, w_ref, h_ref, kc_ref, vc_ref)


def _combine_experts(yg_refs, gate_ref):
    gates = gate_ref[...]
    lo_acc = hi_acc = None
    for kk in range(TOP_K):
        lo, hi = _unpack_halves(_load_slabs(yg_refs, kk))
        gk = gates[:, kk:kk + 1]
        lo_acc = gk * lo if lo_acc is None else lo_acc + gk * lo
        hi_acc = gk * hi if hi_acc is None else hi_acc + gk * hi
    return jnp.concatenate([lo_acc, hi_acc], axis=1)


def _inproj_next_kernel(x_ref, yga_ref, ygb_ref, gate_ref, pmod_ref, mod_ref, g_ref, w_ref, kc_in, vc_in,
                        h_ref, xo_ref, kc_ref, vc_ref):
    del kc_in, vc_in
    x = x_ref[...] + pmod_ref[0, 5:6, :] * _combine_experts((yga_ref, ygb_ref), gate_ref)
    xo_ref[...] = x
    hm = _rms_mod(x, g_ref[...], mod_ref[0, 0:1, :], mod_ref[0, 1:2, :])
    _project_in(hm, w_ref, h_ref, kc_ref, vc_ref)


_TILE_SPEC = pl.BlockSpec((TM, D_MODEL), lambda i: (i, 0))
_MOD_SPEC = pl.BlockSpec((1, MOD_ROWS, D_MODEL), lambda i: (i, 0, 0))
_ROW_SPEC = pl.BlockSpec((1, D_MODEL), lambda i: (0, 0))
_RT_LANES = 128
_RT_ROWS = 2 * TOP_K
_YG_SPEC = pl.BlockSpec((TOP_K, TM, D_SLAB), lambda i: (0, i, 0))
_GATE_SPEC = pl.BlockSpec((TM, _RT_LANES), lambda i: (i, 0))


def _inproj(layer, x, moe, prev_mod, mod, g, w_bf16, caches):
    w_spec = pl.BlockSpec((D_MODEL, IN_COLS), lambda i: (0, 0))
    h_spec = pl.BlockSpec((TM, IN_COLS), lambda i: (i, 0))
    h_shape = jax.ShapeDtypeStruct((T_ALL, IN_COLS), F32)
    c_spec = pl.BlockSpec((1, SEQ, NA_WIDTH), lambda i: (jnp.minimum(i, P_TILES - 1), layer, 0))
    c_shape = jax.ShapeDtypeStruct((BATCH, DEPTH * SEQ, NA_WIDTH), F32)
    x_shape = jax.ShapeDtypeStruct((T_ALL, D_MODEL), F32)
    if moe is None:
        c_all = pl.BlockSpec((1, DEPTH * SEQ, NA_WIDTH), lambda i: (jnp.minimum(i, P_TILES - 1), 0, 0))
        h, x, kc, vc = pl.pallas_call(
            _inproj_first_kernel, out_shape=(h_shape, x_shape, c_shape, c_shape), grid=(N_TILES,),
            in_specs=[_p_tile(D_MODEL), _s_tile(D_MODEL), _MOD_SPEC, _ROW_SPEC, w_spec],
            out_specs=(h_spec, _TILE_SPEC, c_all, c_all),
            compiler_params=_params(), name="inproj_first",
        )(*x, mod, g, w_bf16)
        return h, x, (kc, vc)
    h, x, kc, vc = pl.pallas_call(
        _inproj_next_kernel,
        out_shape=(h_shape, x_shape, c_shape, c_shape),
        grid=(N_TILES,),
        in_specs=[_TILE_SPEC, _YG_SPEC, _YG_SPEC, _GATE_SPEC, _MOD_SPEC, _MOD_SPEC, _ROW_SPEC, w_spec,
                  pl.BlockSpec(memory_space=pl.ANY), pl.BlockSpec(memory_space=pl.ANY)],
        out_specs=(h_spec, _TILE_SPEC, c_spec, c_spec),
        input_output_aliases={8: 2, 9: 3},
        compiler_params=_params(), name="inproj_next",
    )(x, *moe[0], moe[1], prev_mod, mod, g, w_bf16, *caches)
    return h, x, (kc, vc)


def _pair_mask(hh):
    lane = lax.broadcasted_iota(jnp.int32, (1, 2 * NA_HEAD_DIM), 1)
    return (lane >= hh * NA_HEAD_DIM) & (lane < (hh + 1) * NA_HEAD_DIM)


def _attn_prompt_kernel(q_ref, k_ref, v_ref, o_ref):
    scale = NA_HEAD_DIM ** -0.5
    for p in range(NA_HEADS // 2):
        cols = slice(p * 128, (p + 1) * 128)
        qp = q_ref[:, cols] * scale
        kp = k_ref[:, cols].astype(BF16)
        vp = v_ref[:, cols].astype(BF16)
        outs = []
        for hh in range(2):
            qh = jnp.where(_pair_mask(hh), qp, 0.0).astype(BF16)
            s = _dot_nt(qh, kp)
            e = jnp.exp(s - jnp.max(s, axis=-1, keepdims=True))
            den = jnp.sum(e, axis=-1, keepdims=True)
            outs.append(_dot(e.astype(BF16), vp) / den)
        o_ref[:, cols] = jnp.where(_pair_mask(0), outs[0], outs[1]).astype(o_ref.dtype)


def _attn_prompt(h):
    return pl.pallas_call(
        _attn_prompt_kernel,
        out_shape=jax.ShapeDtypeStruct((T_PROMPT, NA_WIDTH), BF16),
        grid=(BATCH,),
        in_specs=[pl.BlockSpec((SEQ, NA_WIDTH), lambda b: (b, 0)),
                  pl.BlockSpec((SEQ, NA_WIDTH), lambda b: (b, 1)),
                  pl.BlockSpec((SEQ, NA_WIDTH), lambda b: (b, 2))],
        out_specs=pl.BlockSpec((SEQ, NA_WIDTH), lambda b: (b, 0)),
        compiler_params=_params(), name="attn_prompt",
    )(h, h, h)


_NA_ROWS = DEC_SEQ // GRID_W
_NA_LOC = NA_KH * GRID_W


def _na_window_start(r):
    return jnp.clip(r - NA_KH // 2, 0, _NA_ROWS - NA_KH)


def _attn_sample_kernel(q_ref, k_ref, v_ref, ck_ref, cv_ref, bias_ref, o_ref):
    s0 = pl.multiple_of(_na_window_start(pl.program_id(1)) * GRID_W, GRID_W)
    scale = NA_HEAD_DIM ** -0.5
    for p in range(NA_HEADS // 2):
        cols = slice(p * 128, (p + 1) * 128)
        qp = q_ref[:, cols] * scale
        kl = k_ref[pl.ds(s0, _NA_LOC), cols].astype(BF16)
        vl = v_ref[pl.ds(s0, _NA_LOC), cols].astype(BF16)
        kc = ck_ref[0, :, cols].astype(BF16)
        vc = cv_ref[0, :, cols].astype(BF16)
        outs = []
        for hh in range(2):
            qh = jnp.where(_pair_mask(hh), qp, 0.0).astype(BF16)
            sl = _dot_nt(qh, kl) + bias_ref[0, 2 * p + hh]
            sc = _dot_nt(qh, kc)
            mx = jnp.maximum(jnp.max(sl, axis=-1, keepdims=True),
                             jnp.max(sc, axis=-1, keepdims=True))
            el = jnp.exp(sl - mx)
            ec = jnp.exp(sc - mx)
            den = jnp.sum(el, axis=-1, keepdims=True) + jnp.sum(ec, axis=-1, keepdims=True)
            outs.append((_dot(el.astype(BF16), vl) + _dot(ec.astype(BF16), vc)) / den)
        o_ref[:, cols] = jnp.where(_pair_mask(0), outs[0], outs[1]).astype(o_ref.dtype)


def _attn_sample(h, ck, cv, bias):
    q_row0 = T_PROMPT // GRID_W
    kv_row0 = T_PROMPT // DEC_SEQ
    return pl.pallas_call(
        _attn_sample_kernel,
        out_shape=jax.ShapeDtypeStruct((T_SAMPLE, NA_WIDTH), BF16),
        grid=(DEC_BATCH, _NA_ROWS),
        in_specs=[pl.BlockSpec((GRID_W, NA_WIDTH), lambda b, r: (q_row0 + b * _NA_ROWS + r, 0)),
                  pl.BlockSpec((DEC_SEQ, NA_WIDTH), lambda b, r: (kv_row0 + b, 1)),
                  pl.BlockSpec((DEC_SEQ, NA_WIDTH), lambda b, r: (kv_row0 + b, 2)),
                  pl.BlockSpec((1, PAST_LEN, NA_WIDTH), lambda b, r: (b, 0, 0)),
                  pl.BlockSpec((1, PAST_LEN, NA_WIDTH), lambda b, r: (b, 0, 0)),
                  pl.BlockSpec((1, NA_HEADS, GRID_W, _NA_LOC), lambda b, r: (_na_window_start(r) - r + NA_KH - 1, 0, 0, 0))],
        out_specs=pl.BlockSpec((GRID_W, NA_WIDTH), lambda b, r: (b * _NA_ROWS + r, 0)),
        compiler_params=_params(2), name="attn_sample",
    )(h, h, h, ck, cv, bias)


_NA_DR = 2 * NA_KH - 1
_NA_DC = 2 * NA_KW - 1


def _na_bias_kernel(rb_ref, o_ref):
    i = pl.program_id(0)
    qc = lax.broadcasted_iota(jnp.int32, (GRID_W, GRID_W), 0)
    kc = lax.broadcasted_iota(jnp.int32, (GRID_W, GRID_W), 1)
    q_start = jnp.clip(qc - NA_KW // 2, 0, GRID_W - NA_KW)
    in_win = (kc >= q_start) & (kc < q_start + NA_KW)
    dc = jnp.clip(kc - qc + NA_KW - 1, 0, _NA_DC - 1)
    picks = [dc == d for d in range(_NA_DC)]
    tiles = []
    for dr in range(_NA_DR):
        acc = jnp.zeros((GRID_W, GRID_W), F32)
        for d in range(_NA_DC):
            acc = jnp.where(picks[d], rb_ref[i, dr * _NA_DC + d], acc)
        tiles.append(jnp.where(in_win, acc, NEG_INF))
    for base in range(NA_KH):
        o_ref[0, base, 0] = jnp.concatenate(tiles[base:base + NA_KH], axis=1)


def _na_bias_tables(rel_bias):
    rb = rel_bias.astype(F32).reshape(DEPTH * NA_HEADS, _NA_DR * _NA_DC)
    return pl.pallas_call(
        _na_bias_kernel,
        out_shape=jax.ShapeDtypeStruct((DEPTH, NA_KH, NA_HEADS, GRID_W, _NA_LOC), F32),
        grid=(DEPTH * NA_HEADS,),
        in_specs=[pl.BlockSpec(memory_space=pltpu.SMEM)],
        out_specs=pl.BlockSpec((1, NA_KH, 1, GRID_W, _NA_LOC),
                               lambda i: (i // NA_HEADS, 0, i % NA_HEADS, 0, 0)),
        compiler_params=_params(), name="na_bias_tables",
    )(rb)


_HG_GROUP = 8


def _hgrn_kernel(*refs, n_tok, has_state):
    if has_state:
        (q_ref, zf_ref, zb_ref, v_ref, g_ref, lbf_ref, lbb_ref, og_ref, s0f_ref, s0b_ref,
         rec_ref, sf_ref, sb_ref, kf_s, bf_s, kb_s, bb_s, of_s, ob_s, zf_s, zb_s, qsf_s, qsb_s, stf_s, stb_s) = refs
    else:
        (q_ref, zf_ref, zb_ref, v_ref, g_ref, lbf_ref, lbb_ref, og_ref,
         rec_ref, sf_ref, sb_ref, kf_s, bf_s, kb_s, bb_s, of_s, ob_s, zf_s, zb_s, qsf_s, qsb_s, stf_s, stb_s) = refs
        s0f_ref = s0b_ref = None
    C = HG_CHUNK
    W = HG_WIDTH
    n_chunks = n_tok // C
    rr = lax.broadcasted_iota(jnp.int32, (W, W), 0)
    cc = lax.broadcasted_iota(jnp.int32, (W, W), 1)
    same_chunk = jnp.right_shift(rr, 5) == jnp.right_shift(cc, 5)
    tri_prefix = jnp.where(same_chunk & (cc <= rr), 1.0, 0.0).astype(BF16)
    tri_suffix = jnp.where(same_chunk & (cc >= rr), 1.0, 0.0).astype(BF16)
    same_head = jnp.right_shift(rr, 6) == jnp.right_shift(cc, 6)
    head_ones = jnp.where(same_head, 1.0, 0.0).astype(BF16)

    for ti in range(n_tok // W):
        rows = slice(ti * W, (ti + 1) * W)
        for z_ref, lb_ref, k_s, b_s, tri in ((zf_ref, lbf_ref, kf_s, bf_s, tri_prefix),
                                             (zb_ref, lbb_ref, kb_s, bb_s, tri_suffix)):
            z = z_ref[rows, :]
            lb = lb_ref[...]
            f = lb + (1.0 - lb) * jax.nn.sigmoid(z)
            logf = jnp.log(jnp.maximum(f, F_FLOOR))
            k_s[rows, :] = (1.0 - lb) * jax.nn.sigmoid(-z)
            hi, mid, lo = _split3(logf)
            b_s[rows, :] = _dot(tri, hi) + _dot(tri, mid) + _dot(tri, lo)

    G = _HG_GROUP
    n_groups = C // G
    srow = lax.broadcasted_iota(jnp.int32, (G, W), 0)
    zf_s[...] = jnp.zeros_like(zf_s)
    zb_s[...] = jnp.zeros_like(zb_s)

    def scan_chunk(ci, k_s, b_s, z_s, qs_s, st_s, o_dir_s, fwd):
        c = ci if fwd else n_chunks - 1 - ci
        base = pl.multiple_of(c * C, C)
        q = q_ref[pl.ds(base, C), :]
        k = k_s[pl.ds(base, C), :]
        b = b_s[pl.ds(base, C), :]
        v = v_ref[pl.ds(base, C), :]
        k_far = {}
        for gt in range(n_groups):
            others = range(gt) if fwd else range(gt + 1, n_groups)
            if not others:
                continue
            rows_t = slice(gt * G, (gt + 1) * G)
            edge = gt * G - 1 if fwd else (gt + 1) * G
            b_edge = b[edge:edge + 1, :]
            qs_s[rows_t, :] = q[rows_t] * jnp.exp(b[rows_t] - b_edge)
            for gs in others:
                rows_s = slice(gs * G, (gs + 1) * G)
                k_far[gt, gs] = k[rows_s] * jnp.exp(b_edge - b[rows_s])
        for t in range(C):
            gt = t // G
            rows_t = slice(gt * G, (gt + 1) * G)
            qt = q_ref[pl.ds(base + t, 1), :]
            bt = b_s[pl.ds(base + t, 1), :]
            keep = (srow + gt * G <= t) if fwd else (srow + gt * G >= t)
            z_s[t * C + gt * G:t * C + (gt + 1) * G, :] = jnp.where(
                keep, (qt * k[rows_t]) * jnp.exp(bt - b[rows_t]), 0.0)
            others = range(gt) if fwd else range(gt + 1, n_groups)
            if others:
                qst = qs_s[t:t + 1, :]
                for gs in others:
                    z_s[t * C + gs * G:t * C + (gs + 1) * G, :] = qst * k_far[gt, gs]
        a_rep = _dot(z_s[...].astype(BF16), head_ones)
        o_intra = jnp.sum(a_rep.reshape(C, C, W) * v[None, :, :], axis=1)
        b_end = b_s[pl.ds(base + (C - 1 if fwd else 0), 1), :]
        q_in = q * jnp.exp(b)
        k_st = k * jnp.exp(b_end - b)
        st = st_s[...]
        o_inter = _dot_nt(q_in.astype(BF16), st.astype(BF16))
        upd = _dot_tn(v.astype(BF16), k_st.astype(BF16))
        st_s[...] = st * jnp.exp(b_end) + jnp.where(same_head, upd, 0.0)
        o_dir_s[pl.ds(base, C), :] = o_intra + o_inter

    def load_state(s0_ref, st_s):
        if s0_ref is None:
            st_s[...] = jnp.zeros((W, W), F32)
            return
        for hh in range(HG_HEADS):
            parts = [s0_ref[0, hh] if g == hh else jnp.zeros((HG_DK, HG_DV), F32) for g in range(HG_HEADS)]
            st_s[hh * HG_DK:(hh + 1) * HG_DK, :] = jnp.concatenate(parts, axis=1)
        st_s[...] = st_s[...].T

    def store_state(st_s, out_ref):
        by_head = st_s[...].T
        for hh in range(HG_HEADS):
            out_ref[0, hh] = by_head[hh * HG_DK:(hh + 1) * HG_DK, hh * HG_DV:(hh + 1) * HG_DV]

    load_state(s0f_ref, stf_s)
    load_state(s0b_ref, stb_s)

    def scan_both(ci, carry):
        scan_chunk(ci, kf_s, bf_s, zf_s, qsf_s, stf_s, of_s, True)
        scan_chunk(ci, kb_s, bb_s, zb_s, qsb_s, stb_s, ob_s, False)
        return carry
    lax.fori_loop(0, n_chunks, scan_both, 0)
    store_state(stf_s, sf_ref)
    store_state(stb_s, sb_ref)

    for ti in range(n_tok // W):
        rows = slice(ti * W, (ti + 1) * W)
        o = of_s[rows, :] + ob_s[rows, :]
        sq = o * o
        sq_hi = sq.astype(BF16)
        sq_lo = (sq - sq_hi.astype(F32)).astype(BF16)
        ms = (_dot(sq_hi, head_ones) + _dot(sq_lo, head_ones)) * (1.0 / HG_DV)
        g = g_ref[rows, :]
        y = o * lax.rsqrt(ms + RMS_EPS) * og_ref[...] * (g * jax.nn.sigmoid(g))
        rec_ref[rows, :] = y.astype(rec_ref.dtype)


def _hgrn(h, lbf, lbb, og, s0f, s0b, n_tok, n_seq, row0):
    W = HG_WIDTH
    has_state = s0f is not None

    def col(cb):
        return pl.BlockSpec((n_tok, W), lambda i, cb=cb: (row0 + i, cb))

    vec = pl.BlockSpec((1, W), lambda i: (0, 0))
    st_spec = pl.BlockSpec((1, HG_HEADS, HG_DK, HG_DV), lambda i: (i, 0, 0, 0))
    in_specs = [col(_CB_HQ), col(_CB_ZF), col(_CB_ZB), col(_CB_HI), col(_CB_HG), vec, vec, vec]
    args = [h, h, h, h, h, lbf, lbb, og]
    if has_state:
        in_specs += [st_spec, st_spec]
        args += [s0f, s0b]
    seq_f32 = pltpu.VMEM((n_tok, W), F32)
    return pl.pallas_call(
        functools.partial(_hgrn_kernel, n_tok=n_tok, has_state=has_state),
        out_shape=(jax.ShapeDtypeStruct((n_seq * n_tok, W), BF16),
                   jax.ShapeDtypeStruct((n_seq, HG_HEADS, HG_DK, HG_DV), F32),
                   jax.ShapeDtypeStruct((n_seq, HG_HEADS, HG_DK, HG_DV), F32)),
        grid=(n_seq,),
        in_specs=in_specs,
        out_specs=(pl.BlockSpec((n_tok, W), lambda i: (i, 0)), st_spec, st_spec),
        scratch_shapes=[seq_f32, seq_f32, seq_f32, seq_f32, seq_f32, seq_f32,
                        pltpu.VMEM((HG_CHUNK * HG_CHUNK, W), F32),
                        pltpu.VMEM((HG_CHUNK * HG_CHUNK, W), F32),
                        pltpu.VMEM((HG_CHUNK, W), F32),
                        pltpu.VMEM((HG_CHUNK, W), F32),
                        pltpu.VMEM((W, W), F32),
                        pltpu.VMEM((W, W), F32)],
        compiler_params=_params(), name="hgrn_state" if has_state else "hgrn_zero",
    )(*args)


def _gmlp_kernel(u_ref, v_ref, g_ref, ws_ref, b_ref, o_ref):
    v = v_ref[...]
    ms = jnp.mean(v * v, axis=-1, keepdims=True)
    vn = (v * lax.rsqrt(ms + RMS_EPS) * g_ref[...]).astype(BF16)
    lane = lax.broadcasted_iota(jnp.int32, (1, GM_WIDTH), 1)
    z = b_ref[...]
    for gi in range(GM_GROUPS):
        zg = _dot(ws_ref[gi], vn)
        in_group = (lane >= gi * GM_GDIM) & (lane < (gi + 1) * GM_GDIM)
        z = z + jnp.where(in_group, zg, 0.0)
    o_ref[...] = (u_ref[...] * z).astype(o_ref.dtype)


def _gmlp(h, vnorm_g, ws_bf16, bias_full):
    W = GM_WIDTH
    return pl.pallas_call(
        _gmlp_kernel,
        out_shape=jax.ShapeDtypeStruct((T_ALL, W), BF16),
        grid=(T_ALL // GM_CHUNK,),
        in_specs=[pl.BlockSpec((GM_CHUNK, W), lambda i: (i, _CB_GU)),
                  pl.BlockSpec((GM_CHUNK, W), lambda i: (i, _CB_GV)),
                  pl.BlockSpec((1, W), lambda i: (0, 0)),
                  pl.BlockSpec((GM_GROUPS, GM_CHUNK, GM_CHUNK), lambda i: (0, 0, 0)),
                  pl.BlockSpec((GM_CHUNK, W), lambda i: (0, 0))],
        out_specs=pl.BlockSpec((GM_CHUNK, W), lambda i: (i, 0)),
        compiler_params=_params(), name="gmlp",
    )(h, h, vnorm_g, ws_bf16, bias_full)


def _outproj_kernel(attp_ref, atts_ref, recp_ref, recs_ref, mlp_ref, x_ref, mod_ref, g_ref, w_ref, wr_ref, br_ref,
                    x1_ref, h2a_ref, h2b_ref, rt_ref, gate_ref, cnt_ref):
    @pl.when(pl.program_id(0) == 0)
    def _():
        cnt_ref[...] = jnp.zeros_like(cnt_ref)

    out = (_dot(_pick_group(attp_ref, atts_ref), w_ref[0:NA_WIDTH, :])
           + _dot(_pick_group(recp_ref, recs_ref), w_ref[NA_WIDTH:NA_WIDTH + HG_WIDTH, :])
           + _dot(mlp_ref[...], w_ref[NA_WIDTH + HG_WIDTH:, :]))
    x1 = x_ref[...] + mod_ref[0, 2:3, :] * out
    x1_ref[...] = x1
    h2 = _rms_mod(x1, g_ref[...], mod_ref[0, 3:4, :], mod_ref[0, 4:5, :])
    _store_slabs((h2a_ref, h2b_ref), _pack_halves(h2))
    h_hi = h2.astype(BF16)
    h_lo = (h2 - h_hi.astype(F32)).astype(BF16)
    wr = wr_ref[...]
    w_hi = wr.astype(BF16)
    w_lo = (wr - w_hi.astype(F32)).astype(BF16)
    logits = _dot(h_hi, w_hi) + _dot(h_lo, w_hi) + _dot(h_hi, w_lo) + br_ref[...]
    lane_e = lax.broadcasted_iota(jnp.int32, (TM, N_EXPERTS), 1).astype(F32)
    lane_o = lax.broadcasted_iota(jnp.int32, (TM, _RT_LANES), 1)
    idx_acc = jnp.zeros((TM, _RT_LANES), F32)
    val_acc = jnp.zeros((TM, _RT_LANES), F32)
    top0 = None
    den = jnp.zeros((TM, 1), F32)
    work = logits
    picks = []
    for kk in range(TOP_K):
        m = jnp.max(work, axis=-1, keepdims=True)
        first = jnp.min(jnp.where(work == m, lane_e, float(N_EXPERTS)), axis=-1, keepdims=True)
        if kk == 0:
            top0 = m
        e = jnp.exp(m - top0)
        den = den + e
        idx_acc = jnp.where(lane_o == kk, first, idx_acc)
        val_acc = jnp.where(lane_o == kk, e, val_acc)
        picks.append(lane_e == first)
        work = jnp.where(picks[-1], -jnp.inf, work)
    gate_ref[...] = val_acc / den
    sel = jnp.zeros((TM, N_EXPERTS), F32)
    for pk in picks:
        sel = sel + jnp.where(pk, 1.0, 0.0)
    rr = lax.broadcasted_iota(jnp.int32, (TM, TM), 0)
    cc = lax.broadcasted_iota(jnp.int32, (TM, TM), 1)
    earlier = jnp.where(cc < rr, 1.0, 0.0).astype(BF16)
    seen = cnt_ref[0:1, 0:N_EXPERTS]
    before = _dot(earlier, sel.astype(BF16)) + seen
    for kk, pk in enumerate(picks):
        rank = jnp.sum(jnp.where(pk, before, 0.0), axis=-1, keepdims=True)
        idx_acc = jnp.where(lane_o == TOP_K + kk, rank, idx_acc)
    rt_ref[...] = idx_acc.T[0:_RT_ROWS, :].astype(jnp.int32)
    cnt_ref[0:1, 0:N_EXPERTS] = seen + jnp.sum(sel, axis=0, keepdims=True)


def _outproj(att_p, att_s, rec_p, rec_s, mlp, x, mod, g, w_bf16, wr, br):
    def tile(width):
        return pl.BlockSpec((TM, width), lambda i: (i, 0))

    return pl.pallas_call(
        _outproj_kernel,
        out_shape=(jax.ShapeDtypeStruct((T_ALL, D_MODEL), F32),
                   jax.ShapeDtypeStruct((T_ALL, D_SLAB), jnp.int32),
                   jax.ShapeDtypeStruct((T_ALL, D_SLAB), jnp.int32),
                   jax.ShapeDtypeStruct((_RT_ROWS, T_ALL), jnp.int32),
                   jax.ShapeDtypeStruct((T_ALL, _RT_LANES), F32),
                   jax.ShapeDtypeStruct((8, _RT_LANES), F32)),
        grid=(N_TILES,),
        in_specs=[_p_tile(NA_WIDTH), _s_tile(NA_WIDTH), _p_tile(HG_WIDTH), _s_tile(HG_WIDTH),
                  tile(GM_WIDTH), _TILE_SPEC, _MOD_SPEC, _ROW_SPEC,
                  pl.BlockSpec((D_MODEL, D_MODEL), lambda i: (0, 0)),
                  pl.BlockSpec((D_MODEL, N_EXPERTS), lambda i: (0, 0)),
                  pl.BlockSpec((1, N_EXPERTS), lambda i: (0, 0))],
        out_specs=(_TILE_SPEC, tile(D_SLAB), tile(D_SLAB), pl.BlockSpec((_RT_ROWS, TM), lambda i: (0, i)),
                   tile(_RT_LANES),
                   pl.BlockSpec((8, _RT_LANES), lambda i: (0, 0))),
        compiler_params=_params(), name="outproj_router",
    )(att_p, att_s, rec_p, rec_s, mlp, x, mod, g, w_bf16, wr, br)


_W_CHUNKS = 4


def _moe_kernel(blk_e_ref, blk_on_ref, blk_new_ref, blk_slot_ref, blk_next_ref,
                xa_ref, xb_ref, wg_hbm, bg_ref, wu_hbm, bu_ref, wd_hbm, bd_ref,
                ya_ref, yb_ref, w_buf, w_sem, *, layer):
    j = pl.program_id(0)

    def weight_copies(expert, slot):
        rows = D_MODEL // _W_CHUNKS
        return [pltpu.make_async_copy(w_hbm.at[layer, expert, pl.ds(ci * rows, rows)],
                                      w_buf.at[slot, wi, pl.ds(ci * rows, rows)], w_sem.at[slot, wi, ci])
                for wi, w_hbm in enumerate((wg_hbm, wu_hbm, wd_hbm)) for ci in range(_W_CHUNKS)]

    @pl.when(j == 0)
    def _():
        for cp in weight_copies(blk_e_ref[0], 0):
            cp.start()

    @pl.when(blk_new_ref[j] != 0)
    def _():
        slot = blk_slot_ref[j]
        for cp in weight_copies(blk_e_ref[j], slot):
            cp.wait()

        @pl.when(blk_next_ref[j] >= 0)
        def _():
            for cp in weight_copies(blk_next_ref[j], 1 - slot):
                cp.start()

    @pl.when(blk_on_ref[j] != 0)
    def _():
        slot = blk_slot_ref[j]
        lo, hi = _unpack_halves(_load_slabs((xa_ref, xb_ref)))
        x = jnp.concatenate([lo.astype(BF16), hi.astype(BF16)], axis=1)
        gate = jnp.minimum(_dot(x, w_buf[slot, 0].astype(BF16)) + bg_ref[0, 0], SWIGLU_LIMIT)
        up = jnp.clip(_dot(x, w_buf[slot, 1].astype(BF16)) + bu_ref[0, 0], -SWIGLU_LIMIT, SWIGLU_LIMIT)
        glu = gate * jax.nn.sigmoid(SWIGLU_ALPHA * gate)
        act = ((up + 1.0) * glu).astype(BF16)
        _store_slabs((ya_ref, yb_ref), _pack_halves(_dot(act, w_buf[slot, 2].astype(BF16)) + bd_ref[0, 0]))

    @pl.when(blk_on_ref[j] == 0)
    def _():
        ya_ref[...] = jnp.zeros_like(ya_ref)
        yb_ref[...] = jnp.zeros_like(yb_ref)


def _moe(layer, plan, x_sorted, wg, bg, wu, bu, wd, bd):
    n_plan = len(plan)
    b_spec = pl.BlockSpec((1, 1, 1, D_MODEL), lambda j, be, *_: (layer, be[j], 0, 0))
    x_spec = pl.BlockSpec((MOE_BM, D_SLAB), lambda j, *_: (j, 0))
    hbm = pl.BlockSpec(memory_space=pl.ANY)
    bias4 = lambda b: b.reshape(DEPTH, N_EXPERTS, 1, D_MODEL)
    return pl.pallas_call(
        functools.partial(_moe_kernel, layer=layer),
        out_shape=(jax.ShapeDtypeStruct((MOE_SLOTS, D_SLAB), jnp.int32),) * N_SPLIT,
        grid_spec=pltpu.PrefetchScalarGridSpec(
            num_scalar_prefetch=n_plan, grid=(MOE_BLOCKS,),
            in_specs=[x_spec, x_spec, hbm, b_spec, hbm, b_spec, hbm, b_spec],
            out_specs=(x_spec, x_spec),
            scratch_shapes=[pltpu.VMEM((2, 3, D_MODEL, D_MODEL), F32),
                            pltpu.SemaphoreType.DMA((2, 3, _W_CHUNKS))]),
        compiler_params=_params(), name="moe_experts",
    )(*plan, *x_sorted, wg, bias4(bg), wu, bias4(bu), wd, bias4(bd))


def _route(rt, counts):
    experts = jnp.arange(N_EXPERTS, dtype=jnp.int32)
    nblk = (counts + MOE_BM - 1) // MOE_BM
    blk_end = jnp.cumsum(nblk)
    row0 = (blk_end - nblk) * MOE_BM
    top_i, rank = rt[:TOP_K], rt[TOP_K:]
    start_of = jnp.sum(jnp.where(top_i[None] == experts[:, None, None], row0[:, None, None], 0), axis=0)
    dest = (start_of + rank).reshape(1, TOP_K * T_ALL)
    live = counts > 0
    last_live = jnp.max(jnp.where(live, experts, 0))
    later_live = live[None, :] & (experts[None, :] > experts[:, None])
    next_live = jnp.min(jnp.where(later_live, experts[None, :], N_EXPERTS), axis=1)
    next_live = jnp.where(next_live == N_EXPERTS, -1, next_live)
    parity = (jnp.cumsum(live.astype(jnp.int32)) - 1) % 2
    blk = jnp.arange(MOE_BLOCKS, dtype=jnp.int32)
    blk_on = blk < blk_end[-1]
    blk_e = jnp.where(blk_on, jnp.minimum(jnp.sum((blk_end[None, :] <= blk[:, None]).astype(jnp.int32), axis=1),
                                          N_EXPERTS - 1), last_live)
    blk_new = blk_on & jnp.concatenate([jnp.ones((1,), bool), blk_e[1:] != blk_e[:-1]])
    is_e = blk_e[:, None] == experts[None, :]
    lookup = lambda table: jnp.sum(jnp.where(is_e, table[None, :], 0), axis=1)
    plan = (blk_e, blk_on, blk_new, lookup(parity), lookup(next_live))
    return dest.astype(jnp.int32), tuple(p.astype(jnp.int32) for p in plan)


_SC_WINDOW = 128


def _sc_mesh():
    return plsc.VectorSubcoreMesh(core_axis_name="core", subcore_axis_name="subcore")


def _sc_scatter_rows(srcs, idx, n_out):
    n_src, width = srcs[0].shape
    n_idx = idx.shape[1]
    src_windows = n_src // _SC_WINDOW

    def body(*refs):
        x_hbm = refs[:len(srcs)]
        i_hbm = refs[len(srcs)]
        o_hbm = refs[len(srcs) + 1:]
        for xs, os_ in zip(x_hbm, o_hbm):
            def step(x_vmem, i_vmem, os_=os_):
                pltpu.sync_copy(x_vmem, os_.at[i_vmem.at[0]])

            pltpu.emit_pipeline(
                step, grid=(n_idx // _SC_WINDOW,),
                in_specs=[pl.BlockSpec((_SC_WINDOW, width), lambda i: (i % src_windows, 0)),
                          pl.BlockSpec((1, _SC_WINDOW), lambda i: (0, i))],
                out_specs=[],
                core_axis_name=("core", "subcore"),
                dimension_semantics=(pltpu.PARALLEL,),
            )(xs, i_hbm)

    out_type = tuple(jax.ShapeDtypeStruct((n_out, width), s.dtype) for s in srcs)
    return pl.kernel(body, out_type=out_type, mesh=_sc_mesh(), scratch_types=[],
                     name="sc_scatter_rows")(*srcs, idx)


def _sc_gather_rows(tables, idx):
    n_idx = idx.shape[1]
    width = tables[0].shape[1]

    def body(*refs):
        t_hbm = refs[:len(tables)]
        i_hbm = refs[len(tables)]
        o_hbm = refs[len(tables) + 1:]
        for ts, os_ in zip(t_hbm, o_hbm):
            def step(i_vmem, o_vmem, ts=ts):
                pltpu.sync_copy(ts.at[i_vmem.at[0]], o_vmem)

            pltpu.emit_pipeline(
                step, grid=(n_idx // _SC_WINDOW,),
                in_specs=[pl.BlockSpec((1, _SC_WINDOW), lambda i: (0, i))],
                out_specs=[pl.BlockSpec((_SC_WINDOW, width), lambda i: (i, 0))],
                core_axis_name=("core", "subcore"),
                dimension_semantics=(pltpu.PARALLEL,),
            )(i_hbm, os_)

    out_type = tuple(jax.ShapeDtypeStruct((n_idx, width), t.dtype) for t in tables)
    return pl.kernel(body, out_type=out_type, mesh=_sc_mesh(), scratch_types=[],
                     name="sc_gather_rows")(*tables, idx)


def _final_kernel(x_ref, yga_ref, ygb_ref, gate_ref, mod_ref, g_ref, yp_ref, ys_ref):
    x = x_ref[...] + mod_ref[0, 5:6, :] * _combine_experts((yga_ref, ygb_ref), gate_ref)
    ms = jnp.mean(x * x, axis=-1, keepdims=True)
    y = x * lax.rsqrt(ms + RMS_EPS) * g_ref[...]

    @pl.when(pl.program_id(0) < P_TILES)
    def _():
        yp_ref[...] = y

    @pl.when(pl.program_id(0) >= P_TILES)
    def _():
        ys_ref[...] = y


def _final(x, moe, mod, g):
    return pl.pallas_call(
        _final_kernel,
        out_shape=(jax.ShapeDtypeStruct((T_PROMPT, D_MODEL), F32), jax.ShapeDtypeStruct((T_SAMPLE, D_MODEL), F32)),
        grid=(N_TILES,),
        in_specs=[_TILE_SPEC, _YG_SPEC, _YG_SPEC, _GATE_SPEC, _MOD_SPEC, _ROW_SPEC],
        out_specs=(_p_tile(D_MODEL), _s_tile(D_MODEL)),
        compiler_params=_params(), name="final_norm",
    )(x, *moe[0], moe[1], mod, g)


def kernel(x_prompt, x_sample, cache_k, cache_v, state_hgrn_fwd, state_hgrn_bwd, c, c_ctx, w_mod, b_mod, norm1_g, norm2_g, w_in, na_rel_bias, hgrn_lb, hgrn_onorm_g, gmlp_vnorm_g, gmlp_ws, gmlp_b, w_out, router_w, router_b, w_gate, b_gate, w_up, b_up, w_down, b_down, final_g):
    x = (x_prompt.reshape(T_PROMPT, D_MODEL), x_sample.reshape(T_SAMPLE, D_MODEL))

    cond = jnp.zeros((MOD_ROWS, D_MODEL), F32).at[0].set(c_ctx).at[1:1 + DEC_BATCH].set(c)
    mod = _modulation(cond, w_mod, b_mod)
    tile_row = np.concatenate([np.zeros(P_TILES, np.int32),
                               1 + np.arange(N_TILES - P_TILES, dtype=np.int32) // (DEC_SEQ // TM)])
    mod_tiles = mod[:, tile_row].reshape(DEPTH, N_TILES, 6, D_MODEL)
    mod_tiles = jnp.pad(mod_tiles, ((0, 0), (0, 0), (0, MOD_ROWS - 6), (0, 0)))

    lb_soft = jax.nn.softmax(hgrn_lb.astype(F32), axis=1)
    lower = jnp.cumsum(lb_soft, axis=1) - lb_soft[:, :1]

    na_bias = _na_bias_tables(na_rel_bias)

    sf_list, sb_list = [], []
    moe_out = caches = None
    for l in range(DEPTH):
        h, x, caches = _inproj(l, x, moe_out, mod_tiles[l - 1] if l else None, mod_tiles[l],
                               norm1_g[l][None, :], w_in[l].astype(BF16), caches)

        att_p = _attn_prompt(h)
        att_s = _attn_sample(h, cache_k[:, l].reshape(DEC_BATCH, PAST_LEN, NA_WIDTH),
                             cache_v[:, l].reshape(DEC_BATCH, PAST_LEN, NA_WIDTH), na_bias[l])
        lbf = lower[0, l][None, :]
        lbb = lower[1, l][None, :]
        og = jnp.tile(hgrn_onorm_g[l], HG_HEADS)[None, :]
        rec_p, sf, sb = _hgrn(h, lbf, lbb, og, None, None, SEQ, BATCH, 0)
        rec_s, _, _ = _hgrn(h, lbf, lbb, og, state_hgrn_fwd[:, l].astype(F32), state_hgrn_bwd[:, l].astype(F32),
                            DEC_SEQ, DEC_BATCH, T_PROMPT // DEC_SEQ)
        sf_list.append(sf)
        sb_list.append(sb)
        gm_bias = jnp.repeat(gmlp_b[l].T, GM_GDIM, axis=1)
        mlp = _gmlp(h, gmlp_vnorm_g[l][None, :], gmlp_ws[l].astype(BF16), gm_bias)

        x, h2a, h2b, rt, gate_pad, cnt = _outproj(att_p, att_s, rec_p, rec_s, mlp, x, mod_tiles[l],
                                                  norm2_g[l][None, :], w_out[l].astype(BF16),
                                                  router_w[l], router_b[l][None, :])
        dest_flat, plan = _route(rt, cnt[0, :N_EXPERTS].astype(jnp.int32))
        x_sorted = _sc_scatter_rows((h2a, h2b), dest_flat, MOE_SLOTS)
        y_sorted = _moe(l, plan, x_sorted, w_gate, b_gate, w_up, b_up, w_down, b_down)
        y_tok = _sc_gather_rows(y_sorted, dest_flat)
        moe_out = ([yt.reshape(TOP_K, T_ALL, D_SLAB) for yt in y_tok], gate_pad)

    y_prompt, y_sample = _final(x, moe_out, mod_tiles[DEPTH - 1], final_g[None, :])
    y_prompt = y_prompt.reshape(BATCH, SEQ, D_MODEL)
    y_sample = y_sample.reshape(DEC_BATCH, DEC_SEQ, D_MODEL)
    new_k, new_v = (cache.reshape(BATCH, DEPTH, SEQ, NA_HEADS, NA_HEAD_DIM) for cache in caches)
    return (y_prompt, y_sample, new_k, new_v, jnp.stack(sf_list, axis=1), jnp.stack(sb_list, axis=1))
```

```python
import functools

import numpy as np
import jax
import jax.numpy as jnp
from jax import lax
from jax.experimental import pallas as pl
from jax.experimental.pallas import tpu as pltpu
from jax.experimental.pallas import tpu_sc as plsc

F32 = jnp.float32
BF16 = jnp.bfloat16

D_MODEL = 1024
BATCH = 32
SEQ = 256
DEPTH = 2
DEC_BATCH = 2
DEC_SEQ = 1024
PAST_LEN = 512
GRID_W = 64
NA_HEADS = 8
NA_HEAD_DIM = 64
NA_WIDTH = NA_HEADS * NA_HEAD_DIM
NA_KH = 8
NA_KW = 16
HG_HEADS = 4
HG_DK = 64
HG_DV = 64
HG_WIDTH = HG_HEADS * HG_DV
HG_CHUNK = 16
F_FLOOR = 1e-30
GM_GROUPS = 4
GM_GDIM = 64
GM_WIDTH = GM_GROUPS * GM_GDIM
GM_CHUNK = 128
IN_COLS = 3 * NA_WIDTH + 5 * HG_WIDTH + 2 * GM_WIDTH
N_EXPERTS = 32
TOP_K = 4
SWIGLU_LIMIT = 7.0
SWIGLU_ALPHA = 1.702
RMS_EPS = 1e-6
NEG_INF = -1e30

T_PROMPT = BATCH * SEQ
T_SAMPLE = DEC_BATCH * DEC_SEQ
T_ALL = T_PROMPT + T_SAMPLE
TM = 256
N_TILES = T_ALL // TM
P_TILES = T_PROMPT // TM
MOE_BM = 256
MOE_SLOTS = -(-(T_ALL * TOP_K + N_EXPERTS * (MOE_BM - 1)) // MOE_BM) * MOE_BM
MOE_BLOCKS = MOE_SLOTS // MOE_BM
MOD_ROWS = 8
V7X_VMEM_LIMIT = 48 * 1024 * 1024

_CB_HQ, _CB_ZF, _CB_ZB, _CB_HI, _CB_HG, _CB_GU, _CB_GV = 6, 7, 8, 9, 10, 11, 12


def _dot(a, b):
    return jnp.dot(a, b, preferred_element_type=F32)


def _dot_nt(a, b):
    return lax.dot_general(a, b, (((1,), (1,)), ((), ())), preferred_element_type=F32)


def _dot_tn(a, b):
    return lax.dot_general(a, b, (((0,), (0,)), ((), ())), preferred_element_type=F32)


def _split3(x):
    hi = x.astype(BF16)
    r1 = x - hi.astype(F32)
    mid = r1.astype(BF16)
    lo = (r1 - mid.astype(F32)).astype(BF16)
    return hi, mid, lo


D_PACK = D_MODEL // 2
N_SPLIT = 2
D_SLAB = D_PACK // N_SPLIT


def _pack_halves(x):
    half = x.shape[1] // 2
    lo = pltpu.bitcast(x[:, :half].astype(BF16).astype(F32), jnp.uint32)
    hi = pltpu.bitcast(x[:, half:].astype(BF16).astype(F32), jnp.uint32)
    return pltpu.bitcast(jnp.right_shift(lo, jnp.uint32(16)) | hi, jnp.int32)


def _unpack_halves(w):
    u = pltpu.bitcast(w, jnp.uint32)
    lo = pltpu.bitcast(jnp.left_shift(u, jnp.uint32(16)), F32)
    hi = pltpu.bitcast(u & jnp.uint32(0xFFFF0000), F32)
    return lo, hi


def _load_slabs(refs, *lead):
    return jnp.concatenate([r[lead] if lead else r[...] for r in refs], axis=1)


def _store_slabs(refs, packed):
    for si, r in enumerate(refs):
        r[...] = packed[:, si * D_SLAB:(si + 1) * D_SLAB]


def _params(n_axes=1):
    return pltpu.CompilerParams(dimension_semantics=("arbitrary",) * n_axes,
                                vmem_limit_bytes=V7X_VMEM_LIMIT)


def _mod_kernel(cond_ref, w_ref, b_ref, o_ref):
    c = cond_ref[...]
    c = c * jax.nn.sigmoid(c)
    w = w_ref[0]
    c_hi = c.astype(BF16)
    c_lo = (c - c_hi.astype(F32)).astype(BF16)
    w_hi = w.astype(BF16)
    w_lo = (w - w_hi.astype(F32)).astype(BF16)
    o_ref[0] = _dot(c_hi, w_hi) + _dot(c_lo, w_hi) + _dot(c_hi, w_lo) + b_ref[0]


def _modulation(cond, w_mod, b_mod):
    tn = 1536
    return pl.pallas_call(
        _mod_kernel,
        out_shape=jax.ShapeDtypeStruct((DEPTH, MOD_ROWS, 6 * D_MODEL), F32),
        grid=(DEPTH, 6 * D_MODEL // tn),
        in_specs=[pl.BlockSpec((MOD_ROWS, D_MODEL), lambda l, j: (0, 0)),
                  pl.BlockSpec((1, D_MODEL, tn), lambda l, j: (l, 0, j)),
                  pl.BlockSpec((1, 1, tn), lambda l, j: (l, 0, j))],
        out_specs=pl.BlockSpec((1, MOD_ROWS, tn), lambda l, j: (l, 0, j)),
        compiler_params=_params(2),
        name="modulation",
    )(cond, w_mod, b_mod.reshape(DEPTH, 1, 6 * D_MODEL))


def _rms_mod(x, g, shift, scale):
    ms = jnp.mean(x * x, axis=-1, keepdims=True)
    y = x * lax.rsqrt(ms + RMS_EPS) * g
    return y * (1.0 + scale) + shift


def _project_in(hm, w_ref, h_ref, kc_ref, vc_ref):
    h = _dot(hm.astype(BF16), w_ref[...])
    h_ref[...] = h

    @pl.when(pl.program_id(0) < P_TILES)
    def _():
        kc_ref[0, 0:NA_WIDTH] = h[:, NA_WIDTH:2 * NA_WIDTH].T
        vc_ref[0, 0:NA_WIDTH] = h[:, 2 * NA_WIDTH:3 * NA_WIDTH].T
        if kc_ref.shape[1] > NA_WIDTH:
            kc_ref[0, NA_WIDTH:] = jnp.zeros((kc_ref.shape[1] - NA_WIDTH, SEQ), F32)
            vc_ref[0, NA_WIDTH:] = jnp.zeros((vc_ref.shape[1] - NA_WIDTH, SEQ), F32)


def _pick_group(p_ref, s_ref):
    return jnp.where(pl.program_id(0) < P_TILES, p_ref[...], s_ref[...])


def _p_tile(width):
    return pl.BlockSpec((TM, width), lambda i: (jnp.minimum(i, P_TILES - 1), 0))


def _s_tile(width):
    return pl.BlockSpec((TM, width), lambda i: (jnp.maximum(i - P_TILES, 0), 0))


def _inproj_first_kernel(xp_ref, xs_ref, mod_ref, g_ref, w_ref, h_ref, xo_ref, kc_ref, vc_ref):
    x = _pick_group(xp_ref, xs_ref)
    xo_ref[...] = x
    hm = _rms_mod(x, g_ref[...], mod_ref[0, 0:1, :], mod_ref[0, 1:2, :])
    _project_in(hm, w_ref, h_ref, kc_ref, vc_ref)


def _combine_experts(yg_refs, gate_ref):
    gates = gate_ref[...]
    lo_acc = hi_acc = None
    for kk in range(TOP_K):
        lo, hi = _unpack_halves(_load_slabs(yg_refs, kk))
        gk = gates[:, kk:kk + 1]
        lo_acc = gk * lo if lo_acc is None else lo_acc + gk * lo
        hi_acc = gk * hi if hi_acc is None else hi_acc + gk * hi
    return jnp.concatenate([lo_acc, hi_acc], axis=1)


def _inproj_next_kernel(x_ref, yga_ref, ygb_ref, gate_ref, pmod_ref, mod_ref, g_ref, w_ref, kc_in, vc_in,
                        h_ref, xo_ref, kc_ref, vc_ref):
    del kc_in, vc_in
    x = x_ref[...] + pmod_ref[0, 5:6, :] * _combine_experts((yga_ref, ygb_ref), gate_ref)
    xo_ref[...] = x
    hm = _rms_mod(x, g_ref[...], mod_ref[0, 0:1, :], mod_ref[0, 1:2, :])
    _project_in(hm, w_ref, h_ref, kc_ref, vc_ref)


_TILE_SPEC = pl.BlockSpec((TM, D_MODEL), lambda i: (i, 0))
_MOD_SPEC = pl.BlockSpec((1, MOD_ROWS, D_MODEL), lambda i: (i, 0, 0))
_ROW_SPEC = pl.BlockSpec((1, D_MODEL), lambda i: (0, 0))
_RT_LANES = 128
_RT_ROWS = 2 * TOP_K
_YG_SPEC = pl.BlockSpec((TOP_K, TM, D_SLAB), lambda i: (0, i, 0))
_GATE_SPEC = pl.BlockSpec((TM, _RT_LANES), lambda i: (i, 0))


def _inproj(layer, x, moe, prev_mod, mod, g, w_bf16, caches):
    w_spec = pl.BlockSpec((D_MODEL, IN_COLS), lambda i: (0, 0))
    h_spec = pl.BlockSpec((TM, IN_COLS), lambda i: (i, 0))
    h_shape = jax.ShapeDtypeStruct((T_ALL, IN_COLS), F32)
    c_spec = pl.BlockSpec((1, NA_WIDTH, SEQ), lambda i: (jnp.minimum(i, P_TILES - 1), layer, 0))
    c_shape = jax.ShapeDtypeStruct((BATCH, DEPTH * NA_WIDTH, SEQ), F32)
    x_shape = jax.ShapeDtypeStruct((T_ALL, D_MODEL), F32)
    if moe is None:
        c_all = pl.BlockSpec((1, DEPTH * NA_WIDTH, SEQ), lambda i: (jnp.minimum(i, P_TILES - 1), 0, 0))
        h, x, kc, vc = pl.pallas_call(
            _inproj_first_kernel, out_shape=(h_shape, x_shape, c_shape, c_shape), grid=(N_TILES,),
            in_specs=[_p_tile(D_MODEL), _s_tile(D_MODEL), _MOD_SPEC, _ROW_SPEC, w_spec],
            out_specs=(h_spec, _TILE_SPEC, c_all, c_all),
            compiler_params=_params(), name="inproj_first",
        )(*x, mod, g, w_bf16)
        return h, x, (kc, vc)
    h, x, kc, vc = pl.pallas_call(
        _inproj_next_kernel,
        out_shape=(h_shape, x_shape, c_shape, c_shape),
        grid=(N_TILES,),
        in_specs=[_TILE_SPEC, _YG_SPEC, _YG_SPEC, _GATE_SPEC, _MOD_SPEC, _MOD_SPEC, _ROW_SPEC, w_spec,
                  pl.BlockSpec(memory_space=pl.ANY), pl.BlockSpec(memory_space=pl.ANY)],
        out_specs=(h_spec, _TILE_SPEC, c_spec, c_spec),
        input_output_aliases={8: 2, 9: 3},
        compiler_params=_params(), name="inproj_next",
    )(x, *moe[0], moe[1], prev_mod, mod, g, w_bf16, *caches)
    return h, x, (kc, vc)


def _pair_mask(hh):
    lane = lax.broadcasted_iota(jnp.int32, (1, 2 * NA_HEAD_DIM), 1)
    return (lane >= hh * NA_HEAD_DIM) & (lane < (hh + 1) * NA_HEAD_DIM)


def _attn_prompt_kernel(q_ref, k_ref, v_ref, o_ref):
    scale = NA_HEAD_DIM ** -0.5
    for p in range(NA_HEADS // 2):
        cols = slice(p * 128, (p + 1) * 128)
        qp = q_ref[:, cols] * scale
        kp = k_ref[:, cols].astype(BF16)
        vp = v_ref[:, cols].astype(BF16)
        outs = []
        for hh in range(2):
            qh = jnp.where(_pair_mask(hh), qp, 0.0).astype(BF16)
            s = _dot_nt(qh, kp)
            e = jnp.exp(s - jnp.max(s, axis=-1, keepdims=True))
            den = jnp.sum(e, axis=-1, keepdims=True)
            outs.append(_dot(e.astype(BF16), vp) / den)
        o_ref[:, cols] = jnp.where(_pair_mask(0), outs[0], outs[1]).astype(o_ref.dtype)


def _attn_prompt(h):
    return pl.pallas_call(
        _attn_prompt_kernel,
        out_shape=jax.ShapeDtypeStruct((T_PROMPT, NA_WIDTH), BF16),
        grid=(BATCH,),
        in_specs=[pl.BlockSpec((SEQ, NA_WIDTH), lambda b: (b, 0)),
                  pl.BlockSpec((SEQ, NA_WIDTH), lambda b: (b, 1)),
                  pl.BlockSpec((SEQ, NA_WIDTH), lambda b: (b, 2))],
        out_specs=pl.BlockSpec((SEQ, NA_WIDTH), lambda b: (b, 0)),
        compiler_params=_params(), name="attn_prompt",
    )(h, h, h)


_NA_ROWS = DEC_SEQ // GRID_W
_NA_LOC = NA_KH * GRID_W


def _na_window_start(r):
    return jnp.clip(r - NA_KH // 2, 0, _NA_ROWS - NA_KH)


def _attn_sample_kernel(q_ref, k_ref, v_ref, ck_ref, cv_ref, bias_ref, o_ref):
    s0 = pl.multiple_of(_na_window_start(pl.program_id(1)) * GRID_W, GRID_W)
    scale = NA_HEAD_DIM ** -0.5
    for p in range(NA_HEADS // 2):
        cols = slice(p * 128, (p + 1) * 128)
        qp = q_ref[:, cols] * scale
        kl = k_ref[pl.ds(s0, _NA_LOC), cols].astype(BF16)
        vl = v_ref[pl.ds(s0, _NA_LOC), cols].astype(BF16)
        kc = ck_ref[0, :, cols].astype(BF16)
        vc = cv_ref[0, :, cols].astype(BF16)
        outs = []
        for hh in range(2):
            qh = jnp.where(_pair_mask(hh), qp, 0.0).astype(BF16)
            sl = _dot_nt(qh, kl) + bias_ref[0, 2 * p + hh]
            sc = _dot_nt(qh, kc)
            mx = jnp.maximum(jnp.max(sl, axis=-1, keepdims=True),
                             jnp.max(sc, axis=-1, keepdims=True))
            el = jnp.exp(sl - mx)
            ec = jnp.exp(sc - mx)
            den = jnp.sum(el, axis=-1, keepdims=True) + jnp.sum(ec, axis=-1, keepdims=True)
            outs.append((_dot(el.astype(BF16), vl) + _dot(ec.astype(BF16), vc)) / den)
        o_ref[:, cols] = jnp.where(_pair_mask(0), outs[0], outs[1]).astype(o_ref.dtype)


def _attn_sample(h, ck, cv, bias):
    q_row0 = T_PROMPT // GRID_W
    kv_row0 = T_PROMPT // DEC_SEQ
    return pl.pallas_call(
        _attn_sample_kernel,
        out_shape=jax.ShapeDtypeStruct((T_SAMPLE, NA_WIDTH), BF16),
        grid=(DEC_BATCH, _NA_ROWS),
        in_specs=[pl.BlockSpec((GRID_W, NA_WIDTH), lambda b, r: (q_row0 + b * _NA_ROWS + r, 0)),
                  pl.BlockSpec((DEC_SEQ, NA_WIDTH), lambda b, r: (kv_row0 + b, 1)),
                  pl.BlockSpec((DEC_SEQ, NA_WIDTH), lambda b, r: (kv_row0 + b, 2)),
                  pl.BlockSpec((1, PAST_LEN, NA_WIDTH), lambda b, r: (b, 0, 0)),
                  pl.BlockSpec((1, PAST_LEN, NA_WIDTH), lambda b, r: (b, 0, 0)),
                  pl.BlockSpec((1, NA_HEADS, GRID_W, _NA_LOC), lambda b, r: (_na_window_start(r) - r + NA_KH - 1, 0, 0, 0))],
        out_specs=pl.BlockSpec((GRID_W, NA_WIDTH), lambda b, r: (b * _NA_ROWS + r, 0)),
        compiler_params=_params(2), name="attn_sample",
    )(h, h, h, ck, cv, bias)


_NA_DR = 2 * NA_KH - 1
_NA_DC = 2 * NA_KW - 1


def _na_bias_kernel(rb_ref, o_ref):
    i = pl.program_id(0)
    qc = lax.broadcasted_iota(jnp.int32, (GRID_W, GRID_W), 0)
    kc = lax.broadcasted_iota(jnp.int32, (GRID_W, GRID_W), 1)
    q_start = jnp.clip(qc - NA_KW // 2, 0, GRID_W - NA_KW)
    in_win = (kc >= q_start) & (kc < q_start + NA_KW)
    dc = jnp.clip(kc - qc + NA_KW - 1, 0, _NA_DC - 1)
    picks = [dc == d for d in range(_NA_DC)]
    tiles = []
    for dr in range(_NA_DR):
        acc = jnp.zeros((GRID_W, GRID_W), F32)
        for d in range(_NA_DC):
            acc = jnp.where(picks[d], rb_ref[i, dr * _NA_DC + d], acc)
        tiles.append(jnp.where(in_win, acc, NEG_INF))
    for base in range(NA_KH):
        o_ref[0, base, 0] = jnp.concatenate(tiles[base:base + NA_KH], axis=1)


def _na_bias_tables(rel_bias):
    rb = rel_bias.astype(F32).reshape(DEPTH * NA_HEADS, _NA_DR * _NA_DC)
    return pl.pallas_call(
        _na_bias_kernel,
        out_shape=jax.ShapeDtypeStruct((DEPTH, NA_KH, NA_HEADS, GRID_W, _NA_LOC), F32),
        grid=(DEPTH * NA_HEADS,),
        in_specs=[pl.BlockSpec(memory_space=pltpu.SMEM)],
        out_specs=pl.BlockSpec((1, NA_KH, 1, GRID_W, _NA_LOC),
                               lambda i: (i // NA_HEADS, 0, i % NA_HEADS, 0, 0)),
        compiler_params=_params(), name="na_bias_tables",
    )(rb)


_HG_GROUP = 8


def _hgrn_kernel(*refs, n_tok, has_state):
    if has_state:
        (q_ref, zf_ref, zb_ref, v_ref, g_ref, lbf_ref, lbb_ref, og_ref, s0f_ref, s0b_ref,
         rec_ref, sf_ref, sb_ref, kf_s, bf_s, kb_s, bb_s, of_s, ob_s, zf_s, zb_s, qsf_s, qsb_s, stf_s, stb_s) = refs
    else:
        (q_ref, zf_ref, zb_ref, v_ref, g_ref, lbf_ref, lbb_ref, og_ref,
         rec_ref, sf_ref, sb_ref, kf_s, bf_s, kb_s, bb_s, of_s, ob_s, zf_s, zb_s, qsf_s, qsb_s, stf_s, stb_s) = refs
        s0f_ref = s0b_ref = None
    C = HG_CHUNK
    W = HG_WIDTH
    n_chunks = n_tok // C
    rr = lax.broadcasted_iota(jnp.int32, (W, W), 0)
    cc = lax.broadcasted_iota(jnp.int32, (W, W), 1)
    log2_c = C.bit_length() - 1
    same_chunk = jnp.right_shift(rr, log2_c) == jnp.right_shift(cc, log2_c)
    tri_prefix = jnp.where(same_chunk & (cc <= rr), 1.0, 0.0).astype(BF16)
    tri_suffix = jnp.where(same_chunk & (cc >= rr), 1.0, 0.0).astype(BF16)
    same_head = jnp.right_shift(rr, 6) == jnp.right_shift(cc, 6)
    head_ones = jnp.where(same_head, 1.0, 0.0).astype(BF16)

    for ti in range(n_tok // W):
        rows = slice(ti * W, (ti + 1) * W)
        for z_ref, lb_ref, k_s, b_s, tri in ((zf_ref, lbf_ref, kf_s, bf_s, tri_prefix),
                                             (zb_ref, lbb_ref, kb_s, bb_s, tri_suffix)):
            z = z_ref[rows, :]
            lb = lb_ref[...]
            f = lb + (1.0 - lb) * jax.nn.sigmoid(z)
            logf = jnp.log(jnp.maximum(f, F_FLOOR))
            k_s[rows, :] = (1.0 - lb) * jax.nn.sigmoid(-z)
            hi, mid, lo = _split3(logf)
            b_s[rows, :] = _dot(tri, hi) + _dot(tri, mid) + _dot(tri, lo)

    G = _HG_GROUP
    n_groups = C // G
    srow = lax.broadcasted_iota(jnp.int32, (G, W), 0)
    zf_s[...] = jnp.zeros_like(zf_s)
    zb_s[...] = jnp.zeros_like(zb_s)

    def scan_chunk(ci, k_s, b_s, z_s, qs_s, st_s, o_dir_s, fwd):
        c = ci if fwd else n_chunks - 1 - ci
        base = pl.multiple_of(c * C, C)
        q = q_ref[pl.ds(base, C), :]
        k = k_s[pl.ds(base, C), :]
        b = b_s[pl.ds(base, C), :]
        v = v_ref[pl.ds(base, C), :]
        k_far = {}
        for gt in range(n_groups):
            others = range(gt) if fwd else range(gt + 1, n_groups)
            if not others:
                continue
            rows_t = slice(gt * G, (gt + 1) * G)
            edge = gt * G - 1 if fwd else (gt + 1) * G
            b_edge = b[edge:edge + 1, :]
            qs_s[rows_t, :] = q[rows_t] * jnp.exp(b[rows_t] - b_edge)
            for gs in others:
                rows_s = slice(gs * G, (gs + 1) * G)
                k_far[gt, gs] = k[rows_s] * jnp.exp(b_edge - b[rows_s])
        for t in range(C):
            gt = t // G
            rows_t = slice(gt * G, (gt + 1) * G)
            qt = q_ref[pl.ds(base + t, 1), :]
            bt = b_s[pl.ds(base + t, 1), :]
            keep = (srow + gt * G <= t) if fwd else (srow + gt * G >= t)
            z_s[t * C + gt * G:t * C + (gt + 1) * G, :] = jnp.where(
                keep, (qt * k[rows_t]) * jnp.exp(bt - b[rows_t]), 0.0)
            others = range(gt) if fwd else range(gt + 1, n_groups)
            if others:
                qst = qs_s[t:t + 1, :]
                for gs in others:
                    z_s[t * C + gs * G:t * C + (gs + 1) * G, :] = qst * k_far[gt, gs]
        a_rep = _dot(z_s[...].astype(BF16), head_ones)
        o_intra = jnp.sum(a_rep.reshape(C, C, W) * v[None, :, :], axis=1)
        b_end = b_s[pl.ds(base + (C - 1 if fwd else 0), 1), :]
        q_in = q * jnp.exp(b)
        k_st = k * jnp.exp(b_end - b)
        st = st_s[...]
        o_inter = _dot_nt(q_in.astype(BF16), st.astype(BF16))
        upd = _dot_tn(v.astype(BF16), k_st.astype(BF16))
        st_s[...] = st * jnp.exp(b_end) + jnp.where(same_head, upd, 0.0)
        o_dir_s[pl.ds(base, C), :] = o_intra + o_inter

    def load_state(s0_ref, st_s):
        if s0_ref is None:
            st_s[...] = jnp.zeros((W, W), F32)
            return
        for hh in range(HG_HEADS):
            parts = [s0_ref[0, hh] if g == hh else jnp.zeros((HG_DK, HG_DV), F32) for g in range(HG_HEADS)]
            st_s[hh * HG_DK:(hh + 1) * HG_DK, :] = jnp.concatenate(parts, axis=1)
        st_s[...] = st_s[...].T

    def store_state(st_s, out_ref):
        by_head = st_s[...].T
        for hh in range(HG_HEADS):
            out_ref[0, hh] = by_head[hh * HG_DK:(hh + 1) * HG_DK, hh * HG_DV:(hh + 1) * HG_DV]

    load_state(s0f_ref, stf_s)
    load_state(s0b_ref, stb_s)

    def scan_both(ci, carry):
        scan_chunk(ci, kf_s, bf_s, zf_s, qsf_s, stf_s, of_s, True)
        scan_chunk(ci, kb_s, bb_s, zb_s, qsb_s, stb_s, ob_s, False)
        return carry
    lax.fori_loop(0, n_chunks, scan_both, 0)
    store_state(stf_s, sf_ref)
    store_state(stb_s, sb_ref)

    for ti in range(n_tok // W):
        rows = slice(ti * W, (ti + 1) * W)
        o = of_s[rows, :] + ob_s[rows, :]
        sq = o * o
        sq_hi = sq.astype(BF16)
        sq_lo = (sq - sq_hi.astype(F32)).astype(BF16)
        ms = (_dot(sq_hi, head_ones) + _dot(sq_lo, head_ones)) * (1.0 / HG_DV)
        g = g_ref[rows, :]
        y = o * lax.rsqrt(ms + RMS_EPS) * og_ref[...] * (g * jax.nn.sigmoid(g))
        rec_ref[rows, :] = y.astype(rec_ref.dtype)


def _hgrn(h, lbf, lbb, og, s0f, s0b, n_tok, n_seq, row0):
    W = HG_WIDTH
    has_state = s0f is not None

    def col(cb):
        return pl.BlockSpec((n_tok, W), lambda i, cb=cb: (row0 + i, cb))

    vec = pl.BlockSpec((1, W), lambda i: (0, 0))
    st_spec = pl.BlockSpec((1, HG_HEADS, HG_DK, HG_DV), lambda i: (i, 0, 0, 0))
    in_specs = [col(_CB_HQ), col(_CB_ZF), col(_CB_ZB), col(_CB_HI), col(_CB_HG), vec, vec, vec]
    args = [h, h, h, h, h, lbf, lbb, og]
    if has_state:
        in_specs += [st_spec, st_spec]
        args += [s0f, s0b]
    seq_f32 = pltpu.VMEM((n_tok, W), F32)
    return pl.pallas_call(
        functools.partial(_hgrn_kernel, n_tok=n_tok, has_state=has_state),
        out_shape=(jax.ShapeDtypeStruct((n_seq * n_tok, W), BF16),
                   jax.ShapeDtypeStruct((n_seq, HG_HEADS, HG_DK, HG_DV), F32),
                   jax.ShapeDtypeStruct((n_seq, HG_HEADS, HG_DK, HG_DV), F32)),
        grid=(n_seq,),
        in_specs=in_specs,
        out_specs=(pl.BlockSpec((n_tok, W), lambda i: (i, 0)), st_spec, st_spec),
        scratch_shapes=[seq_f32, seq_f32, seq_f32, seq_f32, seq_f32, seq_f32,
                        pltpu.VMEM((HG_CHUNK * HG_CHUNK, W), F32),
                        pltpu.VMEM((HG_CHUNK * HG_CHUNK, W), F32),
                        pltpu.VMEM((HG_CHUNK, W), F32),
                        pltpu.VMEM((HG_CHUNK, W), F32),
                        pltpu.VMEM((W, W), F32),
                        pltpu.VMEM((W, W), F32)],
        compiler_params=_params(), name="hgrn_state" if has_state else "hgrn_zero",
    )(*args)


def _gmlp_kernel(u_ref, v_ref, g_ref, ws_ref, b_ref, o_ref):
    lane = lax.broadcasted_iota(jnp.int32, (1, GM_WIDTH), 1)
    for ci in range(TM // GM_CHUNK):
        rows = slice(ci * GM_CHUNK, (ci + 1) * GM_CHUNK)
        v = v_ref[rows, :]
        ms = jnp.mean(v * v, axis=-1, keepdims=True)
        vn = (v * lax.rsqrt(ms + RMS_EPS) * g_ref[...]).astype(BF16)
        z = b_ref[...]
        for gi in range(GM_GROUPS):
            zg = _dot(ws_ref[gi], vn)
            in_group = (lane >= gi * GM_GDIM) & (lane < (gi + 1) * GM_GDIM)
            z = z + jnp.where(in_group, zg, 0.0)
        o_ref[rows, :] = (u_ref[rows, :] * z).astype(o_ref.dtype)


def _gmlp(h, vnorm_g, ws_bf16, bias_full):
    W = GM_WIDTH
    return pl.pallas_call(
        _gmlp_kernel,
        out_shape=jax.ShapeDtypeStruct((T_ALL, W), BF16),
        grid=(N_TILES,),
        in_specs=[pl.BlockSpec((TM, W), lambda i: (i, _CB_GU)),
                  pl.BlockSpec((TM, W), lambda i: (i, _CB_GV)),
                  pl.BlockSpec((1, W), lambda i: (0, 0)),
                  pl.BlockSpec((GM_GROUPS, GM_CHUNK, GM_CHUNK), lambda i: (0, 0, 0)),
                  pl.BlockSpec((GM_CHUNK, W), lambda i: (0, 0))],
        out_specs=pl.BlockSpec((TM, W), lambda i: (i, 0)),
        compiler_params=_params(), name="gmlp",
    )(h, h, vnorm_g, ws_bf16, bias_full)


def _outproj_kernel(attp_ref, atts_ref, recp_ref, recs_ref, mlp_ref, x_ref, mod_ref, g_ref, w_ref, wr_ref, br_ref,
                    x1_ref, h2a_ref, h2b_ref, rt_ref, gate_ref, cnt_ref):
    @pl.when(pl.program_id(0) == 0)
    def _():
        cnt_ref[...] = jnp.zeros_like(cnt_ref)

    out = (_dot(_pick_group(attp_ref, atts_ref), w_ref[0:NA_WIDTH, :])
           + _dot(_pick_group(recp_ref, recs_ref), w_ref[NA_WIDTH:NA_WIDTH + HG_WIDTH, :])
           + _dot(mlp_ref[...], w_ref[NA_WIDTH + HG_WIDTH:, :]))
    x1 = x_ref[...] + mod_ref[0, 2:3, :] * out
    x1_ref[...] = x1
    h2 = _rms_mod(x1, g_ref[...], mod_ref[0, 3:4, :], mod_ref[0, 4:5, :])
    _store_slabs((h2a_ref, h2b_ref), _pack_halves(h2))
    h_hi = h2.astype(BF16)
    h_lo = (h2 - h_hi.astype(F32)).astype(BF16)
    wr = wr_ref[...]
    w_hi = wr.astype(BF16)
    w_lo = (wr - w_hi.astype(F32)).astype(BF16)
    logits = _dot(h_hi, w_hi) + _dot(h_lo, w_hi) + _dot(h_hi, w_lo) + br_ref[...]
    lane_e = lax.broadcasted_iota(jnp.int32, (TM, N_EXPERTS), 1).astype(F32)
    lane_o = lax.broadcasted_iota(jnp.int32, (TM, _RT_LANES), 1)
    idx_acc = jnp.zeros((TM, _RT_LANES), F32)
    val_acc = jnp.zeros((TM, _RT_LANES), F32)
    top0 = None
    den = jnp.zeros((TM, 1), F32)
    work = logits
    picks = []
    for kk in range(TOP_K):
        m = jnp.max(work, axis=-1, keepdims=True)
        first = jnp.min(jnp.where(work == m, lane_e, float(N_EXPERTS)), axis=-1, keepdims=True)
        if kk == 0:
            top0 = m
        e = jnp.exp(m - top0)
        den = den + e
        idx_acc = jnp.where(lane_o == kk, first, idx_acc)
        val_acc = jnp.where(lane_o == kk, e, val_acc)
        picks.append(lane_e == first)
        work = jnp.where(picks[-1], -jnp.inf, work)
    gate_ref[...] = val_acc / den
    sel = jnp.zeros((TM, N_EXPERTS), F32)
    for pk in picks:
        sel = sel + jnp.where(pk, 1.0, 0.0)
    rr = lax.broadcasted_iota(jnp.int32, (TM, TM), 0)
    cc = lax.broadcasted_iota(jnp.int32, (TM, TM), 1)
    earlier = jnp.where(cc < rr, 1.0, 0.0).astype(BF16)
    seen = cnt_ref[0:1, 0:N_EXPERTS]
    before = _dot(earlier, sel.astype(BF16)) + seen
    for kk, pk in enumerate(picks):
        rank = jnp.sum(jnp.where(pk, before, 0.0), axis=-1, keepdims=True)
        idx_acc = jnp.where(lane_o == TOP_K + kk, rank, idx_acc)
    rt_ref[...] = idx_acc.T[0:_RT_ROWS, :].astype(jnp.int32)
    cnt_ref[0:1, 0:N_EXPERTS] = seen + jnp.sum(sel, axis=0, keepdims=True)


def _outproj(att_p, att_s, rec_p, rec_s, mlp, x, mod, g, w_bf16, wr, br):
    def tile(width):
        return pl.BlockSpec((TM, width), lambda i: (i, 0))

    return pl.pallas_call(
        _outproj_kernel,
        out_shape=(jax.ShapeDtypeStruct((T_ALL, D_MODEL), F32),
                   jax.ShapeDtypeStruct((T_ALL, D_SLAB), jnp.int32),
                   jax.ShapeDtypeStruct((T_ALL, D_SLAB), jnp.int32),
                   jax.ShapeDtypeStruct((_RT_ROWS, T_ALL), jnp.int32),
                   jax.ShapeDtypeStruct((T_ALL, _RT_LANES), F32),
                   jax.ShapeDtypeStruct((8, _RT_LANES), F32)),
        grid=(N_TILES,),
        in_specs=[_p_tile(NA_WIDTH), _s_tile(NA_WIDTH), _p_tile(HG_WIDTH), _s_tile(HG_WIDTH),
                  tile(GM_WIDTH), _TILE_SPEC, _MOD_SPEC, _ROW_SPEC,
                  pl.BlockSpec((D_MODEL, D_MODEL), lambda i: (0, 0)),
                  pl.BlockSpec((D_MODEL, N_EXPERTS), lambda i: (0, 0)),
                  pl.BlockSpec((1, N_EXPERTS), lambda i: (0, 0))],
        out_specs=(_TILE_SPEC, tile(D_SLAB), tile(D_SLAB), pl.BlockSpec((_RT_ROWS, TM), lambda i: (0, i)),
                   tile(_RT_LANES),
                   pl.BlockSpec((8, _RT_LANES), lambda i: (0, 0))),
        compiler_params=_params(), name="outproj_router",
    )(att_p, att_s, rec_p, rec_s, mlp, x, mod, g, w_bf16, wr, br)


_W_CHUNKS = 4


def _moe_kernel(blk_e_ref, blk_on_ref, blk_new_ref, blk_slot_ref, blk_next_ref,
                xa_ref, xb_ref, wg_hbm, bg_ref, wu_hbm, bu_ref, wd_hbm, bd_ref,
                ya_ref, yb_ref, w_buf, w_sem, *, layer):
    j = pl.program_id(0)

    def weight_copies(expert, slot):
        rows = D_MODEL // _W_CHUNKS
        return [pltpu.make_async_copy(w_hbm.at[layer, expert, pl.ds(ci * rows, rows)],
                                      w_buf.at[slot, wi, pl.ds(ci * rows, rows)], w_sem.at[slot, wi, ci])
                for wi, w_hbm in enumerate((wg_hbm, wu_hbm, wd_hbm)) for ci in range(_W_CHUNKS)]

    @pl.when(j == 0)
    def _():
        for cp in weight_copies(blk_e_ref[0], 0):
            cp.start()

    @pl.when(blk_new_ref[j] != 0)
    def _():
        slot = blk_slot_ref[j]
        for cp in weight_copies(blk_e_ref[j], slot):
            cp.wait()

        @pl.when(blk_next_ref[j] >= 0)
        def _():
            for cp in weight_copies(blk_next_ref[j], 1 - slot):
                cp.start()

    @pl.when(blk_on_ref[j] != 0)
    def _():
        slot = blk_slot_ref[j]
        lo, hi = _unpack_halves(_load_slabs((xa_ref, xb_ref)))
        x = jnp.concatenate([lo.astype(BF16), hi.astype(BF16)], axis=1)
        gate = jnp.minimum(_dot(x, w_buf[slot, 0].astype(BF16)) + bg_ref[0, 0], SWIGLU_LIMIT)
        up = jnp.clip(_dot(x, w_buf[slot, 1].astype(BF16)) + bu_ref[0, 0], -SWIGLU_LIMIT, SWIGLU_LIMIT)
        glu = gate * jax.nn.sigmoid(SWIGLU_ALPHA * gate)
        act = ((up + 1.0) * glu).astype(BF16)
        _store_slabs((ya_ref, yb_ref), _pack_halves(_dot(act, w_buf[slot, 2].astype(BF16)) + bd_ref[0, 0]))

    @pl.when(blk_on_ref[j] == 0)
    def _():
        ya_ref[...] = jnp.zeros_like(ya_ref)
        yb_ref[...] = jnp.zeros_like(yb_ref)


def _moe(layer, plan, x_sorted, wg, bg, wu, bu, wd, bd):
    n_plan = len(plan)
    b_spec = pl.BlockSpec((1, 1, 1, D_MODEL), lambda j, be, *_: (layer, be[j], 0, 0))
    x_spec = pl.BlockSpec((MOE_BM, D_SLAB), lambda j, *_: (j, 0))
    hbm = pl.BlockSpec(memory_space=pl.ANY)
    bias4 = lambda b: b.reshape(DEPTH, N_EXPERTS, 1, D_MODEL)
    return pl.pallas_call(
        functools.partial(_moe_kernel, layer=layer),
        out_shape=(jax.ShapeDtypeStruct((MOE_SLOTS, D_SLAB), jnp.int32),) * N_SPLIT,
        grid_spec=pltpu.PrefetchScalarGridSpec(
            num_scalar_prefetch=n_plan, grid=(MOE_BLOCKS,),
            in_specs=[x_spec, x_spec, hbm, b_spec, hbm, b_spec, hbm, b_spec],
            out_specs=(x_spec, x_spec),
            scratch_shapes=[pltpu.VMEM((2, 3, D_MODEL, D_MODEL), F32),
                            pltpu.SemaphoreType.DMA((2, 3, _W_CHUNKS))]),
        compiler_params=_params(), name="moe_experts",
    )(*plan, *x_sorted, wg, bias4(bg), wu, bias4(bu), wd, bias4(bd))


def _route(rt, counts):
    experts = jnp.arange(N_EXPERTS, dtype=jnp.int32)
    nblk = (counts + MOE_BM - 1) // MOE_BM
    blk_end = jnp.cumsum(nblk)
    row0 = (blk_end - nblk) * MOE_BM
    top_i, rank = rt[:TOP_K], rt[TOP_K:]
    start_of = jnp.sum(jnp.where(top_i[None] == experts[:, None, None], row0[:, None, None], 0), axis=0)
    dest = (start_of + rank).reshape(1, TOP_K * T_ALL)
    live = counts > 0
    last_live = jnp.max(jnp.where(live, experts, 0))
    later_live = live[None, :] & (experts[None, :] > experts[:, None])
    next_live = jnp.min(jnp.where(later_live, experts[None, :], N_EXPERTS), axis=1)
    next_live = jnp.where(next_live == N_EXPERTS, -1, next_live)
    parity = (jnp.cumsum(live.astype(jnp.int32)) - 1) % 2
    blk = jnp.arange(MOE_BLOCKS, dtype=jnp.int32)
    blk_on = blk < blk_end[-1]
    blk_e = jnp.where(blk_on, jnp.minimum(jnp.sum((blk_end[None, :] <= blk[:, None]).astype(jnp.int32), axis=1),
                                          N_EXPERTS - 1), last_live)
    blk_new = blk_on & jnp.concatenate([jnp.ones((1,), bool), blk_e[1:] != blk_e[:-1]])
    is_e = blk_e[:, None] == experts[None, :]
    lookup = lambda table: jnp.sum(jnp.where(is_e, table[None, :], 0), axis=1)
    plan = (blk_e, blk_on, blk_new, lookup(parity), lookup(next_live))
    return dest.astype(jnp.int32), tuple(p.astype(jnp.int32) for p in plan)


_SC_WINDOW = 128


def _sc_mesh():
    return plsc.VectorSubcoreMesh(core_axis_name="core", subcore_axis_name="subcore")


def _sc_scatter_rows(srcs, idx, n_out):
    n_src, width = srcs[0].shape
    n_idx = idx.shape[1]
    src_windows = n_src // _SC_WINDOW

    def body(*refs):
        x_hbm = refs[:len(srcs)]
        i_hbm = refs[len(srcs)]
        o_hbm = refs[len(srcs) + 1:]
        for xs, os_ in zip(x_hbm, o_hbm):
            def step(x_vmem, i_vmem, os_=os_):
                pltpu.sync_copy(x_vmem, os_.at[i_vmem.at[0]])

            pltpu.emit_pipeline(
                step, grid=(n_idx // _SC_WINDOW,),
                in_specs=[pl.BlockSpec((_SC_WINDOW, width), lambda i: (i % src_windows, 0)),
                          pl.BlockSpec((1, _SC_WINDOW), lambda i: (0, i))],
                out_specs=[],
                core_axis_name=("core", "subcore"),
                dimension_semantics=(pltpu.PARALLEL,),
            )(xs, i_hbm)

    out_type = tuple(jax.ShapeDtypeStruct((n_out, width), s.dtype) for s in srcs)
    return pl.kernel(body, out_type=out_type, mesh=_sc_mesh(), scratch_types=[],
                     name="sc_scatter_rows")(*srcs, idx)


def _sc_gather_rows(tables, idx):
    n_idx = idx.shape[1]
    width = tables[0].shape[1]

    def body(*refs):
        t_hbm = refs[:len(tables)]
        i_hbm = refs[len(tables)]
        o_hbm = refs[len(tables) + 1:]
        for ts, os_ in zip(t_hbm, o_hbm):
            def step(i_vmem, o_vmem, ts=ts):
                pltpu.sync_copy(ts.at[i_vmem.at[0]], o_vmem)

            pltpu.emit_pipeline(
                step, grid=(n_idx // _SC_WINDOW,),
                in_specs=[pl.BlockSpec((1, _SC_WINDOW), lambda i: (0, i))],
                out_specs=[pl.BlockSpec((_SC_WINDOW, width), lambda i: (i, 0))],
                core_axis_name=("core", "subcore"),
                dimension_semantics=(pltpu.PARALLEL,),
            )(i_hbm, os_)

    out_type = tuple(jax.ShapeDtypeStruct((n_idx, width), t.dtype) for t in tables)
    return pl.kernel(body, out_type=out_type, mesh=_sc_mesh(), scratch_types=[],
                     name="sc_gather_rows")(*tables, idx)


def _final_kernel(x_ref, yga_ref, ygb_ref, gate_ref, mod_ref, g_ref, yp_ref, ys_ref):
    x = x_ref[...] + mod_ref[0, 5:6, :] * _combine_experts((yga_ref, ygb_ref), gate_ref)
    ms = jnp.mean(x * x, axis=-1, keepdims=True)
    y = x * lax.rsqrt(ms + RMS_EPS) * g_ref[...]

    @pl.when(pl.program_id(0) < P_TILES)
    def _():
        yp_ref[...] = y

    @pl.when(pl.program_id(0) >= P_TILES)
    def _():
        ys_ref[...] = y


def _final(x, moe, mod, g):
    return pl.pallas_call(
        _final_kernel,
        out_shape=(jax.ShapeDtypeStruct((T_PROMPT, D_MODEL), F32), jax.ShapeDtypeStruct((T_SAMPLE, D_MODEL), F32)),
        grid=(N_TILES,),
        in_specs=[_TILE_SPEC, _YG_SPEC, _YG_SPEC, _GATE_SPEC, _MOD_SPEC, _ROW_SPEC],
        out_specs=(_p_tile(D_MODEL), _s_tile(D_MODEL)),
        compiler_params=_params(), name="final_norm",
    )(x, *moe[0], moe[1], mod, g)


def kernel(x_prompt, x_sample, cache_k, cache_v, state_hgrn_fwd, state_hgrn_bwd, c, c_ctx, w_mod, b_mod, norm1_g, norm2_g, w_in, na_rel_bias, hgrn_lb, hgrn_onorm_g, gmlp_vnorm_g, gmlp_ws, gmlp_b, w_out, router_w, router_b, w_gate, b_gate, w_up, b_up, w_down, b_down, final_g):
    x = (x_prompt.reshape(T_PROMPT, D_MODEL), x_sample.reshape(T_SAMPLE, D_MODEL))

    cond = jnp.zeros((MOD_ROWS, D_MODEL), F32).at[0].set(c_ctx).at[1:1 + DEC_BATCH].set(c)
    mod = _modulation(cond, w_mod, b_mod)
    tile_row = np.concatenate([np.zeros(P_TILES, np.int32),
                               1 + np.arange(N_TILES - P_TILES, dtype=np.int32) // (DEC_SEQ // TM)])
    mod_tiles = mod[:, tile_row].reshape(DEPTH, N_TILES, 6, D_MODEL)
    mod_tiles = jnp.pad(mod_tiles, ((0, 0), (0, 0), (0, MOD_ROWS - 6), (0, 0)))

    lb_soft = jax.nn.softmax(hgrn_lb.astype(F32), axis=1)
    lower = jnp.cumsum(lb_soft, axis=1) - lb_soft[:, :1]

    na_bias = _na_bias_tables(na_rel_bias)

    sf_list, sb_list = [], []
    moe_out = caches = None
    for l in range(DEPTH):
        h, x, caches = _inproj(l, x, moe_out, mod_tiles[l - 1] if l else None, mod_tiles[l],
                               norm1_g[l][None, :], w_in[l].astype(BF16), caches)

        att_p = _attn_prompt(h)
        att_s = _attn_sample(h, cache_k[:, l].reshape(DEC_BATCH, PAST_LEN, NA_WIDTH),
                             cache_v[:, l].reshape(DEC_BATCH, PAST_LEN, NA_WIDTH), na_bias[l])
        lbf = lower[0, l][None, :]
        lbb = lower[1, l][None, :]
        og = jnp.tile(hgrn_onorm_g[l], HG_HEADS)[None, :]
        rec_p, sf, sb = _hgrn(h, lbf, lbb, og, None, None, SEQ, BATCH, 0)
        rec_s, _, _ = _hgrn(h, lbf, lbb, og, state_hgrn_fwd[:, l].astype(F32), state_hgrn_bwd[:, l].astype(F32),
                            DEC_SEQ, DEC_BATCH, T_PROMPT // DEC_SEQ)
        sf_list.append(sf)
        sb_list.append(sb)
        gm_bias = jnp.repeat(gmlp_b[l].T, GM_GDIM, axis=1)
        mlp = _gmlp(h, gmlp_vnorm_g[l][None, :], gmlp_ws[l].astype(BF16), gm_bias)

        x, h2a, h2b, rt, gate_pad, cnt = _outproj(att_p, att_s, rec_p, rec_s, mlp, x, mod_tiles[l],
                                                  norm2_g[l][None, :], w_out[l].astype(BF16),
                                                  router_w[l], router_b[l][None, :])
        dest_flat, plan = _route(rt, cnt[0, :N_EXPERTS].astype(jnp.int32))
        x_sorted = _sc_scatter_rows((h2a, h2b), dest_flat, MOE_SLOTS)
        y_sorted = _moe(l, plan, x_sorted, w_gate, b_gate, w_up, b_up, w_down, b_down)
        y_tok = _sc_gather_rows(y_sorted, dest_flat)
        moe_out = ([yt.reshape(TOP_K, T_ALL, D_SLAB) for yt in y_tok], gate_pad)

    y_prompt, y_sample = _final(x, moe_out, mod_tiles[DEPTH - 1], final_g[None, :])
    y_prompt = y_prompt.reshape(BATCH, SEQ, D_MODEL)
    y_sample = y_sample.reshape(DEC_BATCH, DEC_SEQ, D_MODEL)
    new_k, new_v = (cache.reshape(BATCH, DEPTH, NA_HEADS, NA_HEAD_DIM, SEQ).transpose(0, 1, 4, 2, 3)
                    for cache in caches)
    return (y_prompt, y_sample, new_k, new_v, jnp.stack(sf_list, axis=1), jnp.stack(sb_list, axis=1))
```

```python
import functools

import numpy as np
import jax
import jax.numpy as jnp
from jax import lax
from jax.experimental import pallas as pl
from jax.experimental.pallas import tpu as pltpu
from jax.experimental.pallas import tpu_sc as plsc

F32 = jnp.float32
BF16 = jnp.bfloat16

D_MODEL = 1024
BATCH = 32
SEQ = 256
DEPTH = 2
DEC_BATCH = 2
DEC_SEQ = 1024
PAST_LEN = 512
GRID_W = 64
NA_HEADS = 8
NA_HEAD_DIM = 64
NA_WIDTH = NA_HEADS * NA_HEAD_DIM
NA_KH = 8
NA_KW = 16
HG_HEADS = 4
HG_DK = 64
HG_DV = 64
HG_WIDTH = HG_HEADS * HG_DV
HG_CHUNK = 16
F_FLOOR = 1e-30
GM_GROUPS = 4
GM_GDIM = 64
GM_WIDTH = GM_GROUPS * GM_GDIM
GM_CHUNK = 128
IN_COLS = 3 * NA_WIDTH + 5 * HG_WIDTH + 2 * GM_WIDTH
N_EXPERTS = 32
TOP_K = 4
SWIGLU_LIMIT = 7.0
SWIGLU_ALPHA = 1.702
RMS_EPS = 1e-6
NEG_INF = -1e30

T_PROMPT = BATCH * SEQ
T_SAMPLE = DEC_BATCH * DEC_SEQ
T_ALL = T_PROMPT + T_SAMPLE
TM = 256
N_TILES = T_ALL // TM
P_TILES = T_PROMPT // TM
MOE_BM = 256
MOE_SLOTS = -(-(T_ALL * TOP_K + N_EXPERTS * (MOE_BM - 1)) // MOE_BM) * MOE_BM
MOE_BLOCKS = MOE_SLOTS // MOE_BM
MOD_ROWS = 8
V7X_VMEM_LIMIT = 48 * 1024 * 1024

QKV_COLS = 3 * NA_WIDTH
REST_COLS = IN_COLS - QKV_COLS
_CB_HQ, _CB_ZF, _CB_ZB, _CB_HI, _CB_HG, _CB_GU, _CB_GV = range(7)


def _dot(a, b):
    return jnp.dot(a, b, preferred_element_type=F32)


def _dot_nt(a, b):
    return lax.dot_general(a, b, (((1,), (1,)), ((), ())), preferred_element_type=F32)


def _dot_tn(a, b):
    return lax.dot_general(a, b, (((0,), (0,)), ((), ())), preferred_element_type=F32)


def _split3(x):
    hi = x.astype(BF16)
    r1 = x - hi.astype(F32)
    mid = r1.astype(BF16)
    lo = (r1 - mid.astype(F32)).astype(BF16)
    return hi, mid, lo


D_PACK = D_MODEL // 2
N_SPLIT = 2
D_SLAB = D_PACK // N_SPLIT


def _pack_halves(x):
    half = x.shape[1] // 2
    lo = pltpu.bitcast(x[:, :half].astype(BF16).astype(F32), jnp.uint32)
    hi = pltpu.bitcast(x[:, half:].astype(BF16).astype(F32), jnp.uint32)
    return pltpu.bitcast(jnp.right_shift(lo, jnp.uint32(16)) | hi, jnp.int32)


def _unpack_halves(w):
    u = pltpu.bitcast(w, jnp.uint32)
    lo = pltpu.bitcast(jnp.left_shift(u, jnp.uint32(16)), F32)
    hi = pltpu.bitcast(u & jnp.uint32(0xFFFF0000), F32)
    return lo, hi


def _load_slabs(refs, *lead):
    return jnp.concatenate([r[lead] if lead else r[...] for r in refs], axis=1)


def _store_slabs(refs, packed):
    for si, r in enumerate(refs):
        r[...] = packed[:, si * D_SLAB:(si + 1) * D_SLAB]


def _params(n_axes=1):
    return pltpu.CompilerParams(dimension_semantics=("arbitrary",) * n_axes,
                                vmem_limit_bytes=V7X_VMEM_LIMIT)


def _mod_kernel(cond_ref, w_ref, b_ref, o_ref):
    c = cond_ref[...]
    c = c * jax.nn.sigmoid(c)
    w = w_ref[0]
    c_hi = c.astype(BF16)
    c_lo = (c - c_hi.astype(F32)).astype(BF16)
    w_hi = w.astype(BF16)
    w_lo = (w - w_hi.astype(F32)).astype(BF16)
    o_ref[0] = _dot(c_hi, w_hi) + _dot(c_lo, w_hi) + _dot(c_hi, w_lo) + b_ref[0]


def _modulation(cond, w_mod, b_mod):
    tn = 1536
    return pl.pallas_call(
        _mod_kernel,
        out_shape=jax.ShapeDtypeStruct((DEPTH, MOD_ROWS, 6 * D_MODEL), F32),
        grid=(DEPTH, 6 * D_MODEL // tn),
        in_specs=[pl.BlockSpec((MOD_ROWS, D_MODEL), lambda l, j: (0, 0)),
                  pl.BlockSpec((1, D_MODEL, tn), lambda l, j: (l, 0, j)),
                  pl.BlockSpec((1, 1, tn), lambda l, j: (l, 0, j))],
        out_specs=pl.BlockSpec((1, MOD_ROWS, tn), lambda l, j: (l, 0, j)),
        compiler_params=_params(2),
        name="modulation",
    )(cond, w_mod, b_mod.reshape(DEPTH, 1, 6 * D_MODEL))


def _rms_mod(x, g, shift, scale):
    ms = jnp.mean(x * x, axis=-1, keepdims=True)
    y = x * lax.rsqrt(ms + RMS_EPS) * g
    return y * (1.0 + scale) + shift


def _project_in(hm, w_ref, qkv_ref, h_ref, kc_ref, vc_ref):
    h = _dot(hm.astype(BF16), w_ref[...])
    qkv_ref[...] = h[:, :QKV_COLS].astype(BF16)
    h_ref[...] = h[:, QKV_COLS:]

    @pl.when(pl.program_id(0) < P_TILES)
    def _():
        kc_ref[0, 0:NA_WIDTH] = h[:, NA_WIDTH:2 * NA_WIDTH].T
        vc_ref[0, 0:NA_WIDTH] = h[:, 2 * NA_WIDTH:3 * NA_WIDTH].T
        if kc_ref.shape[1] > NA_WIDTH:
            kc_ref[0, NA_WIDTH:] = jnp.zeros((kc_ref.shape[1] - NA_WIDTH, SEQ), F32)
            vc_ref[0, NA_WIDTH:] = jnp.zeros((vc_ref.shape[1] - NA_WIDTH, SEQ), F32)


def _pick_group(p_ref, s_ref):
    return jnp.where(pl.program_id(0) < P_TILES, p_ref[...], s_ref[...])


def _p_tile(width):
    return pl.BlockSpec((TM, width), lambda i: (jnp.minimum(i, P_TILES - 1), 0))


def _s_tile(width):
    return pl.BlockSpec((TM, width), lambda i: (jnp.maximum(i - P_TILES, 0), 0))


def _inproj_first_kernel(xp_ref, xs_ref, mod_ref, g_ref, w_ref, qkv_ref, h_ref, xo_ref, kc_ref, vc_ref):
    x = _pick_group(xp_ref, xs_ref)
    xo_ref[...] = x
    hm = _rms_mod(x, g_ref[...], mod_ref[0, 0:1, :], mod_ref[0, 1:2, :])
    _project_in(hm, w_ref, qkv_ref, h_ref, kc_ref, vc_ref)


def _combine_experts(yg_refs, gate_ref):
    gates = gate_ref[...]
    lo_acc = hi_acc = None
    for kk in range(TOP_K):
        lo, hi = _unpack_halves(_load_slabs(yg_refs, kk))
        gk = gates[:, kk:kk + 1]
        lo_acc = gk * lo if lo_acc is None else lo_acc + gk * lo
        hi_acc = gk * hi if hi_acc is None else hi_acc + gk * hi
    return jnp.concatenate([lo_acc, hi_acc], axis=1)


def _inproj_next_kernel(x_ref, yga_ref, ygb_ref, gate_ref, pmod_ref, mod_ref, g_ref, w_ref, kc_in, vc_in,
                        qkv_ref, h_ref, xo_ref, kc_ref, vc_ref):
    del kc_in, vc_in
    x = x_ref[...] + pmod_ref[0, 5:6, :] * _combine_experts((yga_ref, ygb_ref), gate_ref)
    xo_ref[...] = x
    hm = _rms_mod(x, g_ref[...], mod_ref[0, 0:1, :], mod_ref[0, 1:2, :])
    _project_in(hm, w_ref, qkv_ref, h_ref, kc_ref, vc_ref)


_TILE_SPEC = pl.BlockSpec((TM, D_MODEL), lambda i: (i, 0))
_MOD_SPEC = pl.BlockSpec((1, MOD_ROWS, D_MODEL), lambda i: (i, 0, 0))
_ROW_SPEC = pl.BlockSpec((1, D_MODEL), lambda i: (0, 0))
_RT_LANES = 128
_RT_ROWS = 2 * TOP_K
_YG_SPEC = pl.BlockSpec((TOP_K, TM, D_SLAB), lambda i: (0, i, 0))
_GATE_SPEC = pl.BlockSpec((TM, _RT_LANES), lambda i: (i, 0))


def _inproj(layer, x, moe, prev_mod, mod, g, w_bf16, caches):
    w_spec = pl.BlockSpec((D_MODEL, IN_COLS), lambda i: (0, 0))
    h_spec = pl.BlockSpec((TM, REST_COLS), lambda i: (i, 0))
    h_shape = jax.ShapeDtypeStruct((T_ALL, REST_COLS), F32)
    q_spec = pl.BlockSpec((TM, QKV_COLS), lambda i: (i, 0))
    q_shape = jax.ShapeDtypeStruct((T_ALL, QKV_COLS), BF16)
    c_spec = pl.BlockSpec((1, NA_WIDTH, SEQ), lambda i: (jnp.minimum(i, P_TILES - 1), layer, 0))
    c_shape = jax.ShapeDtypeStruct((BATCH, DEPTH * NA_WIDTH, SEQ), F32)
    x_shape = jax.ShapeDtypeStruct((T_ALL, D_MODEL), F32)
    if moe is None:
        c_all = pl.BlockSpec((1, DEPTH * NA_WIDTH, SEQ), lambda i: (jnp.minimum(i, P_TILES - 1), 0, 0))
        qkv, h, x, kc, vc = pl.pallas_call(
            _inproj_first_kernel, out_shape=(q_shape, h_shape, x_shape, c_shape, c_shape), grid=(N_TILES,),
            in_specs=[_p_tile(D_MODEL), _s_tile(D_MODEL), _MOD_SPEC, _ROW_SPEC, w_spec],
            out_specs=(q_spec, h_spec, _TILE_SPEC, c_all, c_all),
            compiler_params=_params(), name="inproj_first",
        )(*x, mod, g, w_bf16)
        return qkv, h, x, (kc, vc)
    qkv, h, x, kc, vc = pl.pallas_call(
        _inproj_next_kernel,
        out_shape=(q_shape, h_shape, x_shape, c_shape, c_shape),
        grid=(N_TILES,),
        in_specs=[_TILE_SPEC, _YG_SPEC, _YG_SPEC, _GATE_SPEC, _MOD_SPEC, _MOD_SPEC, _ROW_SPEC, w_spec,
                  pl.BlockSpec(memory_space=pl.ANY), pl.BlockSpec(memory_space=pl.ANY)],
        out_specs=(q_spec, h_spec, _TILE_SPEC, c_spec, c_spec),
        input_output_aliases={8: 3, 9: 4},
        compiler_params=_params(), name="inproj_next",
    )(x, *moe[0], moe[1], prev_mod, mod, g, w_bf16, *caches)
    return qkv, h, x, (kc, vc)


def _pair_mask(hh):
    lane = lax.broadcasted_iota(jnp.int32, (1, 2 * NA_HEAD_DIM), 1)
    return (lane >= hh * NA_HEAD_DIM) & (lane < (hh + 1) * NA_HEAD_DIM)


def _attn_prompt_kernel(q_ref, k_ref, v_ref, o_ref):
    scale = NA_HEAD_DIM ** -0.5
    for p in range(NA_HEADS // 2):
        cols = slice(p * 128, (p + 1) * 128)
        qp = q_ref[:, cols] * scale
        kp = k_ref[:, cols]
        vp = v_ref[:, cols]
        outs = []
        for hh in range(2):
            qh = jnp.where(_pair_mask(hh), qp, jnp.zeros_like(qp))
            s = _dot_nt(qh, kp)
            e = jnp.exp(s - jnp.max(s, axis=-1, keepdims=True))
            den = jnp.sum(e, axis=-1, keepdims=True)
            outs.append(_dot(e.astype(BF16), vp) / den)
        o_ref[:, cols] = jnp.where(_pair_mask(0), outs[0], outs[1]).astype(o_ref.dtype)


def _attn_prompt(qkv):
    return pl.pallas_call(
        _attn_prompt_kernel,
        out_shape=jax.ShapeDtypeStruct((T_PROMPT, NA_WIDTH), BF16),
        grid=(BATCH,),
        in_specs=[pl.BlockSpec((SEQ, NA_WIDTH), lambda b: (b, 0)),
                  pl.BlockSpec((SEQ, NA_WIDTH), lambda b: (b, 1)),
                  pl.BlockSpec((SEQ, NA_WIDTH), lambda b: (b, 2))],
        out_specs=pl.BlockSpec((SEQ, NA_WIDTH), lambda b: (b, 0)),
        compiler_params=_params(), name="attn_prompt",
    )(qkv, qkv, qkv)


_NA_ROWS = DEC_SEQ // GRID_W
_NA_LOC = NA_KH * GRID_W


def _na_window_start(r):
    return jnp.clip(r - NA_KH // 2, 0, _NA_ROWS - NA_KH)


def _attn_sample_kernel(q_ref, k_ref, v_ref, ck_ref, cv_ref, bias_ref, o_ref):
    s0 = pl.multiple_of(_na_window_start(pl.program_id(1)) * GRID_W, GRID_W)
    scale = NA_HEAD_DIM ** -0.5
    for p in range(NA_HEADS // 2):
        cols = slice(p * 128, (p + 1) * 128)
        qp = q_ref[:, cols] * scale
        kl = k_ref[pl.ds(s0, _NA_LOC), cols]
        vl = v_ref[pl.ds(s0, _NA_LOC), cols]
        kc = ck_ref[0, :, cols].astype(BF16)
        vc = cv_ref[0, :, cols].astype(BF16)
        outs = []
        for hh in range(2):
            qh = jnp.where(_pair_mask(hh), qp, jnp.zeros_like(qp))
            sl = _dot_nt(qh, kl) + bias_ref[0, 2 * p + hh]
            sc = _dot_nt(qh, kc)
            mx = jnp.maximum(jnp.max(sl, axis=-1, keepdims=True),
                             jnp.max(sc, axis=-1, keepdims=True))
            el = jnp.exp(sl - mx)
            ec = jnp.exp(sc - mx)
            den = jnp.sum(el, axis=-1, keepdims=True) + jnp.sum(ec, axis=-1, keepdims=True)
            outs.append((_dot(el.astype(BF16), vl) + _dot(ec.astype(BF16), vc)) / den)
        o_ref[:, cols] = jnp.where(_pair_mask(0), outs[0], outs[1]).astype(o_ref.dtype)


def _attn_sample(qkv, ck, cv, bias):
    q_row0 = T_PROMPT // GRID_W
    kv_row0 = T_PROMPT // DEC_SEQ
    return pl.pallas_call(
        _attn_sample_kernel,
        out_shape=jax.ShapeDtypeStruct((T_SAMPLE, NA_WIDTH), BF16),
        grid=(DEC_BATCH, _NA_ROWS),
        in_specs=[pl.BlockSpec((GRID_W, NA_WIDTH), lambda b, r: (q_row0 + b * _NA_ROWS + r, 0)),
                  pl.BlockSpec((DEC_SEQ, NA_WIDTH), lambda b, r: (kv_row0 + b, 1)),
                  pl.BlockSpec((DEC_SEQ, NA_WIDTH), lambda b, r: (kv_row0 + b, 2)),
                  pl.BlockSpec((1, PAST_LEN, NA_WIDTH), lambda b, r: (b, 0, 0)),
                  pl.BlockSpec((1, PAST_LEN, NA_WIDTH), lambda b, r: (b, 0, 0)),
                  pl.BlockSpec((1, NA_HEADS, GRID_W, _NA_LOC), lambda b, r: (_na_window_start(r) - r + NA_KH - 1, 0, 0, 0))],
        out_specs=pl.BlockSpec((GRID_W, NA_WIDTH), lambda b, r: (b * _NA_ROWS + r, 0)),
        compiler_params=_params(2), name="attn_sample",
    )(qkv, qkv, qkv, ck, cv, bias)


_NA_DR = 2 * NA_KH - 1
_NA_DC = 2 * NA_KW - 1


def _na_bias_kernel(rb_ref, o_ref):
    i = pl.program_id(0)
    qc = lax.broadcasted_iota(jnp.int32, (GRID_W, GRID_W), 0)
    kc = lax.broadcasted_iota(jnp.int32, (GRID_W, GRID_W), 1)
    q_start = jnp.clip(qc - NA_KW // 2, 0, GRID_W - NA_KW)
    in_win = (kc >= q_start) & (kc < q_start + NA_KW)
    dc = jnp.clip(kc - qc + NA_KW - 1, 0, _NA_DC - 1)
    picks = [dc == d for d in range(_NA_DC)]
    tiles = []
    for dr in range(_NA_DR):
        acc = jnp.zeros((GRID_W, GRID_W), F32)
        for d in range(_NA_DC):
            acc = jnp.where(picks[d], rb_ref[i, dr * _NA_DC + d], acc)
        tiles.append(jnp.where(in_win, acc, NEG_INF))
    for base in range(NA_KH):
        o_ref[0, base, 0] = jnp.concatenate(tiles[base:base + NA_KH], axis=1)


def _na_bias_tables(rel_bias):
    rb = rel_bias.astype(F32).reshape(DEPTH * NA_HEADS, _NA_DR * _NA_DC)
    return pl.pallas_call(
        _na_bias_kernel,
        out_shape=jax.ShapeDtypeStruct((DEPTH, NA_KH, NA_HEADS, GRID_W, _NA_LOC), F32),
        grid=(DEPTH * NA_HEADS,),
        in_specs=[pl.BlockSpec(memory_space=pltpu.SMEM)],
        out_specs=pl.BlockSpec((1, NA_KH, 1, GRID_W, _NA_LOC),
                               lambda i: (i // NA_HEADS, 0, i % NA_HEADS, 0, 0)),
        compiler_params=_params(), name="na_bias_tables",
    )(rb)


_HG_GROUP = 8


def _hgrn_kernel(*refs, n_tok, has_state):
    if has_state:
        (q_ref, zf_ref, zb_ref, v_ref, g_ref, lbf_ref, lbb_ref, og_ref, s0f_ref, s0b_ref,
         rec_ref, sf_ref, sb_ref, kf_s, bf_s, kb_s, bb_s, of_s, ob_s, zf_s, zb_s, qsf_s, qsb_s, stf_s, stb_s) = refs
    else:
        (q_ref, zf_ref, zb_ref, v_ref, g_ref, lbf_ref, lbb_ref, og_ref,
         rec_ref, sf_ref, sb_ref, kf_s, bf_s, kb_s, bb_s, of_s, ob_s, zf_s, zb_s, qsf_s, qsb_s, stf_s, stb_s) = refs
        s0f_ref = s0b_ref = None
    C = HG_CHUNK
    W = HG_WIDTH
    n_chunks = n_tok // C
    rr = lax.broadcasted_iota(jnp.int32, (W, W), 0)
    cc = lax.broadcasted_iota(jnp.int32, (W, W), 1)
    log2_c = C.bit_length() - 1
    same_chunk = jnp.right_shift(rr, log2_c) == jnp.right_shift(cc, log2_c)
    tri_prefix = jnp.where(same_chunk & (cc <= rr), 1.0, 0.0).astype(BF16)
    tri_suffix = jnp.where(same_chunk & (cc >= rr), 1.0, 0.0).astype(BF16)
    same_head = jnp.right_shift(rr, 6) == jnp.right_shift(cc, 6)
    head_ones = jnp.where(same_head, 1.0, 0.0).astype(BF16)

    for ti in range(n_tok // W):
        rows = slice(ti * W, (ti + 1) * W)
        for z_ref, lb_ref, k_s, b_s, tri in ((zf_ref, lbf_ref, kf_s, bf_s, tri_prefix),
                                             (zb_ref, lbb_ref, kb_s, bb_s, tri_suffix)):
            z = z_ref[rows, :]
            lb = lb_ref[...]
            f = lb + (1.0 - lb) * jax.nn.sigmoid(z)
            logf = jnp.log(jnp.maximum(f, F_FLOOR))
            k_s[rows, :] = (1.0 - lb) * jax.nn.sigmoid(-z)
            hi, mid, lo = _split3(logf)
            b_s[rows, :] = _dot(tri, hi) + _dot(tri, mid) + _dot(tri, lo)

    G = _HG_GROUP
    n_groups = C // G
    srow = lax.broadcasted_iota(jnp.int32, (G, W), 0)
    zf_s[...] = jnp.zeros_like(zf_s)
    zb_s[...] = jnp.zeros_like(zb_s)

    def scan_chunk(ci, k_s, b_s, z_s, qs_s, st_s, o_dir_s, fwd):
        c = ci if fwd else n_chunks - 1 - ci
        base = pl.multiple_of(c * C, C)
        q = q_ref[pl.ds(base, C), :]
        k = k_s[pl.ds(base, C), :]
        b = b_s[pl.ds(base, C), :]
        v = v_ref[pl.ds(base, C), :]
        k_far = {}
        for gt in range(n_groups):
            others = range(gt) if fwd else range(gt + 1, n_groups)
            if not others:
                continue
            rows_t = slice(gt * G, (gt + 1) * G)
            edge = gt * G - 1 if fwd else (gt + 1) * G
            b_edge = b[edge:edge + 1, :]
            qs_s[rows_t, :] = q[rows_t] * jnp.exp(b[rows_t] - b_edge)
            for gs in others:
                rows_s = slice(gs * G, (gs + 1) * G)
                k_far[gt, gs] = k[rows_s] * jnp.exp(b_edge - b[rows_s])
        for t in range(C):
            gt = t // G
            rows_t = slice(gt * G, (gt + 1) * G)
            qt = q_ref[pl.ds(base + t, 1), :]
            bt = b_s[pl.ds(base + t, 1), :]
            keep = (srow + gt * G <= t) if fwd else (srow + gt * G >= t)
            z_s[t * C + gt * G:t * C + (gt + 1) * G, :] = jnp.where(
                keep, (qt * k[rows_t]) * jnp.exp(bt - b[rows_t]), 0.0)
            others = range(gt) if fwd else range(gt + 1, n_groups)
            if others:
                qst = qs_s[t:t + 1, :]
                for gs in others:
                    z_s[t * C + gs * G:t * C + (gs + 1) * G, :] = qst * k_far[gt, gs]
        a_rep = _dot(z_s[...].astype(BF16), head_ones)
        o_intra = jnp.sum(a_rep.reshape(C, C, W) * v[None, :, :], axis=1)
        b_end = b_s[pl.ds(base + (C - 1 if fwd else 0), 1), :]
        q_in = q * jnp.exp(b)
        k_st = k * jnp.exp(b_end - b)
        st = st_s[...]
        o_inter = _dot_nt(q_in.astype(BF16), st.astype(BF16))
        upd = _dot_tn(v.astype(BF16), k_st.astype(BF16))
        st_s[...] = st * jnp.exp(b_end) + jnp.where(same_head, upd, 0.0)
        o_dir_s[pl.ds(base, C), :] = o_intra + o_inter

    def load_state(s0_ref, st_s):
        if s0_ref is None:
            st_s[...] = jnp.zeros((W, W), F32)
            return
        for hh in range(HG_HEADS):
            parts = [s0_ref[0, hh] if g == hh else jnp.zeros((HG_DK, HG_DV), F32) for g in range(HG_HEADS)]
            st_s[hh * HG_DK:(hh + 1) * HG_DK, :] = jnp.concatenate(parts, axis=1)
        st_s[...] = st_s[...].T

    def store_state(st_s, out_ref):
        by_head = st_s[...].T
        for hh in range(HG_HEADS):
            out_ref[0, hh] = by_head[hh * HG_DK:(hh + 1) * HG_DK, hh * HG_DV:(hh + 1) * HG_DV]

    load_state(s0f_ref, stf_s)
    load_state(s0b_ref, stb_s)

    def scan_both(ci, carry):
        scan_chunk(ci, kf_s, bf_s, zf_s, qsf_s, stf_s, of_s, True)
        scan_chunk(ci, kb_s, bb_s, zb_s, qsb_s, stb_s, ob_s, False)
        return carry
    lax.fori_loop(0, n_chunks, scan_both, 0)
    store_state(stf_s, sf_ref)
    store_state(stb_s, sb_ref)

    for ti in range(n_tok // W):
        rows = slice(ti * W, (ti + 1) * W)
        o = of_s[rows, :] + ob_s[rows, :]
        sq = o * o
        sq_hi = sq.astype(BF16)
        sq_lo = (sq - sq_hi.astype(F32)).astype(BF16)
        ms = (_dot(sq_hi, head_ones) + _dot(sq_lo, head_ones)) * (1.0 / HG_DV)
        g = g_ref[rows, :]
        y = o * lax.rsqrt(ms + RMS_EPS) * og_ref[...] * (g * jax.nn.sigmoid(g))
        rec_ref[rows, :] = y.astype(rec_ref.dtype)


def _hgrn(h, lbf, lbb, og, s0f, s0b, n_tok, n_seq, row0):
    W = HG_WIDTH
    has_state = s0f is not None

    def col(cb):
        return pl.BlockSpec((n_tok, W), lambda i, cb=cb: (row0 + i, cb))

    vec = pl.BlockSpec((1, W), lambda i: (0, 0))
    st_spec = pl.BlockSpec((1, HG_HEADS, HG_DK, HG_DV), lambda i: (i, 0, 0, 0))
    in_specs = [col(_CB_HQ), col(_CB_ZF), col(_CB_ZB), col(_CB_HI), col(_CB_HG), vec, vec, vec]
    args = [h, h, h, h, h, lbf, lbb, og]
    if has_state:
        in_specs += [st_spec, st_spec]
        args += [s0f, s0b]
    seq_f32 = pltpu.VMEM((n_tok, W), F32)
    return pl.pallas_call(
        functools.partial(_hgrn_kernel, n_tok=n_tok, has_state=has_state),
        out_shape=(jax.ShapeDtypeStruct((n_seq * n_tok, W), BF16),
                   jax.ShapeDtypeStruct((n_seq, HG_HEADS, HG_DK, HG_DV), F32),
                   jax.ShapeDtypeStruct((n_seq, HG_HEADS, HG_DK, HG_DV), F32)),
        grid=(n_seq,),
        in_specs=in_specs,
        out_specs=(pl.BlockSpec((n_tok, W), lambda i: (i, 0)), st_spec, st_spec),
        scratch_shapes=[seq_f32, seq_f32, seq_f32, seq_f32, seq_f32, seq_f32,
                        pltpu.VMEM((HG_CHUNK * HG_CHUNK, W), F32),
                        pltpu.VMEM((HG_CHUNK * HG_CHUNK, W), F32),
                        pltpu.VMEM((HG_CHUNK, W), F32),
                        pltpu.VMEM((HG_CHUNK, W), F32),
                        pltpu.VMEM((W, W), F32),
                        pltpu.VMEM((W, W), F32)],
        compiler_params=_params(), name="hgrn_state" if has_state else "hgrn_zero",
    )(*args)


def _gmlp_kernel(u_ref, v_ref, g_ref, ws_ref, b_ref, o_ref):
    lane = lax.broadcasted_iota(jnp.int32, (1, GM_WIDTH), 1)
    for ci in range(TM // GM_CHUNK):
        rows = slice(ci * GM_CHUNK, (ci + 1) * GM_CHUNK)
        v = v_ref[rows, :]
        ms = jnp.mean(v * v, axis=-1, keepdims=True)
        vn = (v * lax.rsqrt(ms + RMS_EPS) * g_ref[...]).astype(BF16)
        z = b_ref[...]
        for gi in range(GM_GROUPS):
            zg = _dot(ws_ref[gi], vn)
            in_group = (lane >= gi * GM_GDIM) & (lane < (gi + 1) * GM_GDIM)
            z = z + jnp.where(in_group, zg, 0.0)
        o_ref[rows, :] = (u_ref[rows, :] * z).astype(o_ref.dtype)


def _gmlp(h, vnorm_g, ws_bf16, bias_full):
    W = GM_WIDTH
    return pl.pallas_call(
        _gmlp_kernel,
        out_shape=jax.ShapeDtypeStruct((T_ALL, W), BF16),
        grid=(N_TILES,),
        in_specs=[pl.BlockSpec((TM, W), lambda i: (i, _CB_GU)),
                  pl.BlockSpec((TM, W), lambda i: (i, _CB_GV)),
                  pl.BlockSpec((1, W), lambda i: (0, 0)),
                  pl.BlockSpec((GM_GROUPS, GM_CHUNK, GM_CHUNK), lambda i: (0, 0, 0)),
                  pl.BlockSpec((GM_CHUNK, W), lambda i: (0, 0))],
        out_specs=pl.BlockSpec((TM, W), lambda i: (i, 0)),
        compiler_params=_params(), name="gmlp",
    )(h, h, vnorm_g, ws_bf16, bias_full)


def _outproj_kernel(attp_ref, atts_ref, recp_ref, recs_ref, mlp_ref, x_ref, mod_ref, g_ref, w_ref, wr_ref, br_ref,
                    x1_ref, h2a_ref, h2b_ref, rt_ref, gate_ref, cnt_ref):
    @pl.when(pl.program_id(0) == 0)
    def _():
        cnt_ref[...] = jnp.zeros_like(cnt_ref)

    out = (_dot(_pick_group(attp_ref, atts_ref), w_ref[0:NA_WIDTH, :])
           + _dot(_pick_group(recp_ref, recs_ref), w_ref[NA_WIDTH:NA_WIDTH + HG_WIDTH, :])
           + _dot(mlp_ref[...], w_ref[NA_WIDTH + HG_WIDTH:, :]))
    x1 = x_ref[...] + mod_ref[0, 2:3, :] * out
    x1_ref[...] = x1
    h2 = _rms_mod(x1, g_ref[...], mod_ref[0, 3:4, :], mod_ref[0, 4:5, :])
    _store_slabs((h2a_ref, h2b_ref), _pack_halves(h2))
    h_hi = h2.astype(BF16)
    h_lo = (h2 - h_hi.astype(F32)).astype(BF16)
    wr = wr_ref[...]
    w_hi = wr.astype(BF16)
    w_lo = (wr - w_hi.astype(F32)).astype(BF16)
    logits = _dot(h_hi, w_hi) + _dot(h_lo, w_hi) + _dot(h_hi, w_lo) + br_ref[...]
    lane_e = lax.broadcasted_iota(jnp.int32, (TM, N_EXPERTS), 1).astype(F32)
    lane_o = lax.broadcasted_iota(jnp.int32, (TM, _RT_LANES), 1)
    idx_acc = jnp.zeros((TM, _RT_LANES), F32)
    val_acc = jnp.zeros((TM, _RT_LANES), F32)
    top0 = None
    den = jnp.zeros((TM, 1), F32)
    work = logits
    picks = []
    for kk in range(TOP_K):
        m = jnp.max(work, axis=-1, keepdims=True)
        first = jnp.min(jnp.where(work == m, lane_e, float(N_EXPERTS)), axis=-1, keepdims=True)
        if kk == 0:
            top0 = m
        e = jnp.exp(m - top0)
        den = den + e
        idx_acc = jnp.where(lane_o == kk, first, idx_acc)
        val_acc = jnp.where(lane_o == kk, e, val_acc)
        picks.append(lane_e == first)
        work = jnp.where(picks[-1], -jnp.inf, work)
    gate_ref[...] = val_acc / den
    sel = jnp.zeros((TM, N_EXPERTS), F32)
    for pk in picks:
        sel = sel + jnp.where(pk, 1.0, 0.0)
    rr = lax.broadcasted_iota(jnp.int32, (TM, TM), 0)
    cc = lax.broadcasted_iota(jnp.int32, (TM, TM), 1)
    earlier = jnp.where(cc < rr, 1.0, 0.0).astype(BF16)
    seen = cnt_ref[0:1, 0:N_EXPERTS]
    before = _dot(earlier, sel.astype(BF16)) + seen
    for kk, pk in enumerate(picks):
        rank = jnp.sum(jnp.where(pk, before, 0.0), axis=-1, keepdims=True)
        idx_acc = jnp.where(lane_o == TOP_K + kk, rank, idx_acc)
    rt_ref[...] = idx_acc.T[0:_RT_ROWS, :].astype(jnp.int32)
    cnt_ref[0:1, 0:N_EXPERTS] = seen + jnp.sum(sel, axis=0, keepdims=True)


def _outproj(att_p, att_s, rec_p, rec_s, mlp, x, mod, g, w_bf16, wr, br):
    def tile(width):
        return pl.BlockSpec((TM, width), lambda i: (i, 0))

    return pl.pallas_call(
        _outproj_kernel,
        out_shape=(jax.ShapeDtypeStruct((T_ALL, D_MODEL), F32),
                   jax.ShapeDtypeStruct((T_ALL, D_SLAB), jnp.int32),
                   jax.ShapeDtypeStruct((T_ALL, D_SLAB), jnp.int32),
                   jax.ShapeDtypeStruct((_RT_ROWS, T_ALL), jnp.int32),
                   jax.ShapeDtypeStruct((T_ALL, _RT_LANES), F32),
                   jax.ShapeDtypeStruct((8, _RT_LANES), F32)),
        grid=(N_TILES,),
        in_specs=[_p_tile(NA_WIDTH), _s_tile(NA_WIDTH), _p_tile(HG_WIDTH), _s_tile(HG_WIDTH),
                  tile(GM_WIDTH), _TILE_SPEC, _MOD_SPEC, _ROW_SPEC,
                  pl.BlockSpec((D_MODEL, D_MODEL), lambda i: (0, 0)),
                  pl.BlockSpec((D_MODEL, N_EXPERTS), lambda i: (0, 0)),
                  pl.BlockSpec((1, N_EXPERTS), lambda i: (0, 0))],
        out_specs=(_TILE_SPEC, tile(D_SLAB), tile(D_SLAB), pl.BlockSpec((_RT_ROWS, TM), lambda i: (0, i)),
                   tile(_RT_LANES),
                   pl.BlockSpec((8, _RT_LANES), lambda i: (0, 0))),
        compiler_params=_params(), name="outproj_router",
    )(att_p, att_s, rec_p, rec_s, mlp, x, mod, g, w_bf16, wr, br)


_W_CHUNKS = 4
_W_CAST_ROWS = 128


def _moe_kernel(blk_e_ref, blk_on_ref, blk_new_ref, blk_next_ref,
                xa_ref, xb_ref, wg_hbm, bg_ref, wu_hbm, bu_ref, wd_hbm, bd_ref,
                ya_ref, yb_ref, w_f32, w_bf16, w_sem, *, layer):
    j = pl.program_id(0)

    def weight_copies(expert):
        rows = D_MODEL // _W_CHUNKS
        return [pltpu.make_async_copy(w_hbm.at[layer, expert, pl.ds(ci * rows, rows)],
                                      w_f32.at[wi, pl.ds(ci * rows, rows)], w_sem.at[wi, ci])
                for wi, w_hbm in enumerate((wg_hbm, wu_hbm, wd_hbm)) for ci in range(_W_CHUNKS)]

    @pl.when(j == 0)
    def _():
        for cp in weight_copies(blk_e_ref[0]):
            cp.start()

    @pl.when(blk_new_ref[j] != 0)
    def _():
        for cp in weight_copies(blk_e_ref[j]):
            cp.wait()

        def cast_rows(ci, carry):
            rows = pl.ds(pl.multiple_of(ci * _W_CAST_ROWS, _W_CAST_ROWS), _W_CAST_ROWS)
            for wi in range(3):
                w_bf16[wi, rows, :] = w_f32[wi, rows, :].astype(BF16)
            return carry
        lax.fori_loop(0, D_MODEL // _W_CAST_ROWS, cast_rows, 0)

        @pl.when(blk_next_ref[j] >= 0)
        def _():
            for cp in weight_copies(blk_next_ref[j]):
                cp.start()

    @pl.when(blk_on_ref[j] != 0)
    def _():
        lo, hi = _unpack_halves(_load_slabs((xa_ref, xb_ref)))
        x = jnp.concatenate([lo.astype(BF16), hi.astype(BF16)], axis=1)
        gate = jnp.minimum(_dot(x, w_bf16[0]) + bg_ref[0, 0], SWIGLU_LIMIT)
        up = jnp.clip(_dot(x, w_bf16[1]) + bu_ref[0, 0], -SWIGLU_LIMIT, SWIGLU_LIMIT)
        glu = gate * jax.nn.sigmoid(SWIGLU_ALPHA * gate)
        act = ((up + 1.0) * glu).astype(BF16)
        _store_slabs((ya_ref, yb_ref), _pack_halves(_dot(act, w_bf16[2]) + bd_ref[0, 0]))

    @pl.when(blk_on_ref[j] == 0)
    def _():
        ya_ref[...] = jnp.zeros_like(ya_ref)
        yb_ref[...] = jnp.zeros_like(yb_ref)


def _moe(layer, plan, x_sorted, wg, bg, wu, bu, wd, bd):
    n_plan = len(plan)
    b_spec = pl.BlockSpec((1, 1, 1, D_MODEL), lambda j, be, *_: (layer, be[j], 0, 0))
    x_spec = pl.BlockSpec((MOE_BM, D_SLAB), lambda j, *_: (j, 0))
    hbm = pl.BlockSpec(memory_space=pl.ANY)
    bias4 = lambda b: b.reshape(DEPTH, N_EXPERTS, 1, D_MODEL)
    return pl.pallas_call(
        functools.partial(_moe_kernel, layer=layer),
        out_shape=(jax.ShapeDtypeStruct((MOE_SLOTS, D_SLAB), jnp.int32),) * N_SPLIT,
        grid_spec=pltpu.PrefetchScalarGridSpec(
            num_scalar_prefetch=n_plan, grid=(MOE_BLOCKS,),
            in_specs=[x_spec, x_spec, hbm, b_spec, hbm, b_spec, hbm, b_spec],
            out_specs=(x_spec, x_spec),
            scratch_shapes=[pltpu.VMEM((3, D_MODEL, D_MODEL), F32), pltpu.VMEM((3, D_MODEL, D_MODEL), BF16),
                            pltpu.SemaphoreType.DMA((3, _W_CHUNKS))]),
        compiler_params=_params(), name="moe_experts",
    )(*plan, *x_sorted, wg, bias4(bg), wu, bias4(bu), wd, bias4(bd))


def _route(rt, counts):
    experts = jnp.arange(N_EXPERTS, dtype=jnp.int32)
    nblk = (counts + MOE_BM - 1) // MOE_BM
    blk_end = jnp.cumsum(nblk)
    row0 = (blk_end - nblk) * MOE_BM
    top_i, rank = rt[:TOP_K], rt[TOP_K:]
    start_of = jnp.sum(jnp.where(top_i[None] == experts[:, None, None], row0[:, None, None], 0), axis=0)
    dest = (start_of + rank).reshape(1, TOP_K * T_ALL)
    live = counts > 0
    last_live = jnp.max(jnp.where(live, experts, 0))
    later_live = live[None, :] & (experts[None, :] > experts[:, None])
    next_live = jnp.min(jnp.where(later_live, experts[None, :], N_EXPERTS), axis=1)
    next_live = jnp.where(next_live == N_EXPERTS, -1, next_live)
    blk = jnp.arange(MOE_BLOCKS, dtype=jnp.int32)
    blk_on = blk < blk_end[-1]
    blk_e = jnp.where(blk_on, jnp.minimum(jnp.sum((blk_end[None, :] <= blk[:, None]).astype(jnp.int32), axis=1),
                                          N_EXPERTS - 1), last_live)
    blk_new = blk_on & jnp.concatenate([jnp.ones((1,), bool), blk_e[1:] != blk_e[:-1]])
    is_e = blk_e[:, None] == experts[None, :]
    lookup = lambda table: jnp.sum(jnp.where(is_e, table[None, :], 0), axis=1)
    plan = (blk_e, blk_on, blk_new, lookup(next_live))
    return dest.astype(jnp.int32), tuple(p.astype(jnp.int32) for p in plan)


_SC_WINDOW = 128


def _sc_mesh():
    return plsc.VectorSubcoreMesh(core_axis_name="core", subcore_axis_name="subcore")


def _sc_scatter_rows(srcs, idx, n_out):
    n_src, width = srcs[0].shape
    n_idx = idx.shape[1]
    src_windows = n_src // _SC_WINDOW

    def body(*refs):
        x_hbm = refs[:len(srcs)]
        i_hbm = refs[len(srcs)]
        o_hbm = refs[len(srcs) + 1:]
        for xs, os_ in zip(x_hbm, o_hbm):
            def step(x_vmem, i_vmem, os_=os_):
                pltpu.sync_copy(x_vmem, os_.at[i_vmem.at[0]])

            pltpu.emit_pipeline(
                step, grid=(n_idx // _SC_WINDOW,),
                in_specs=[pl.BlockSpec((_SC_WINDOW, width), lambda i: (i % src_windows, 0)),
                          pl.BlockSpec((1, _SC_WINDOW), lambda i: (0, i))],
                out_specs=[],
                core_axis_name=("core", "subcore"),
                dimension_semantics=(pltpu.PARALLEL,),
            )(xs, i_hbm)

    out_type = tuple(jax.ShapeDtypeStruct((n_out, width), s.dtype) for s in srcs)
    return pl.kernel(body, out_type=out_type, mesh=_sc_mesh(), scratch_types=[],
                     name="sc_scatter_rows")(*srcs, idx)


def _sc_gather_rows(tables, idx):
    n_idx = idx.shape[1]
    width = tables[0].shape[1]

    def body(*refs):
        t_hbm = refs[:len(tables)]
        i_hbm = refs[len(tables)]
        o_hbm = refs[len(tables) + 1:]
        for ts, os_ in zip(t_hbm, o_hbm):
            def step(i_vmem, o_vmem, ts=ts):
                pltpu.sync_copy(ts.at[i_vmem.at[0]], o_vmem)

            pltpu.emit_pipeline(
                step, grid=(n_idx // _SC_WINDOW,),
                in_specs=[pl.BlockSpec((1, _SC_WINDOW), lambda i: (0, i))],
                out_specs=[pl.BlockSpec((_SC_WINDOW, width), lambda i: (i, 0))],
                core_axis_name=("core", "subcore"),
                dimension_semantics=(pltpu.PARALLEL,),
            )(i_hbm, os_)

    out_type = tuple(jax.ShapeDtypeStruct((n_idx, width), t.dtype) for t in tables)
    return pl.kernel(body, out_type=out_type, mesh=_sc_mesh(), scratch_types=[],
                     name="sc_gather_rows")(*tables, idx)


def _final_kernel(x_ref, yga_ref, ygb_ref, gate_ref, mod_ref, g_ref, yp_ref, ys_ref):
    x = x_ref[...] + mod_ref[0, 5:6, :] * _combine_experts((yga_ref, ygb_ref), gate_ref)
    ms = jnp.mean(x * x, axis=-1, keepdims=True)
    y = x * lax.rsqrt(ms + RMS_EPS) * g_ref[...]

    @pl.when(pl.program_id(0) < P_TILES)
    def _():
        yp_ref[...] = y

    @pl.when(pl.program_id(0) >= P_TILES)
    def _():
        ys_ref[...] = y


def _final(x, moe, mod, g):
    return pl.pallas_call(
        _final_kernel,
        out_shape=(jax.ShapeDtypeStruct((T_PROMPT, D_MODEL), F32), jax.ShapeDtypeStruct((T_SAMPLE, D_MODEL), F32)),
        grid=(N_TILES,),
        in_specs=[_TILE_SPEC, _YG_SPEC, _YG_SPEC, _GATE_SPEC, _MOD_SPEC, _ROW_SPEC],
        out_specs=(_p_tile(D_MODEL), _s_tile(D_MODEL)),
        compiler_params=_params(), name="final_norm",
    )(x, *moe[0], moe[1], mod, g)


def kernel(x_prompt, x_sample, cache_k, cache_v, state_hgrn_fwd, state_hgrn_bwd, c, c_ctx, w_mod, b_mod, norm1_g, norm2_g, w_in, na_rel_bias, hgrn_lb, hgrn_onorm_g, gmlp_vnorm_g, gmlp_ws, gmlp_b, w_out, router_w, router_b, w_gate, b_gate, w_up, b_up, w_down, b_down, final_g):
    x = (x_prompt.reshape(T_PROMPT, D_MODEL), x_sample.reshape(T_SAMPLE, D_MODEL))

    cond = jnp.zeros((MOD_ROWS, D_MODEL), F32).at[0].set(c_ctx).at[1:1 + DEC_BATCH].set(c)
    mod = _modulation(cond, w_mod, b_mod)
    tile_row = np.concatenate([np.zeros(P_TILES, np.int32),
                               1 + np.arange(N_TILES - P_TILES, dtype=np.int32) // (DEC_SEQ // TM)])
    mod_tiles = mod[:, tile_row].reshape(DEPTH, N_TILES, 6, D_MODEL)
    mod_tiles = jnp.pad(mod_tiles, ((0, 0), (0, 0), (0, MOD_ROWS - 6), (0, 0)))

    lb_soft = jax.nn.softmax(hgrn_lb.astype(F32), axis=1)
    lower = jnp.cumsum(lb_soft, axis=1) - lb_soft[:, :1]

    na_bias = _na_bias_tables(na_rel_bias)

    sf_list, sb_list = [], []
    moe_out = caches = None
    for l in range(DEPTH):
        qkv, h, x, caches = _inproj(l, x, moe_out, mod_tiles[l - 1] if l else None, mod_tiles[l],
                                    norm1_g[l][None, :], w_in[l].astype(BF16), caches)

        att_p = _attn_prompt(qkv)
        att_s = _attn_sample(qkv, cache_k[:, l].reshape(DEC_BATCH, PAST_LEN, NA_WIDTH),
                             cache_v[:, l].reshape(DEC_BATCH, PAST_LEN, NA_WIDTH), na_bias[l])
        lbf = lower[0, l][None, :]
        lbb = lower[1, l][None, :]
        og = jnp.tile(hgrn_onorm_g[l], HG_HEADS)[None, :]
        rec_p, sf, sb = _hgrn(h, lbf, lbb, og, None, None, SEQ, BATCH, 0)
        rec_s, _, _ = _hgrn(h, lbf, lbb, og, state_hgrn_fwd[:, l].astype(F32), state_hgrn_bwd[:, l].astype(F32),
                            DEC_SEQ, DEC_BATCH, T_PROMPT // DEC_SEQ)
        sf_list.append(sf)
        sb_list.append(sb)
        gm_bias = jnp.repeat(gmlp_b[l].T, GM_GDIM, axis=1)
        mlp = _gmlp(h, gmlp_vnorm_g[l][None, :], gmlp_ws[l].astype(BF16), gm_bias)

        x, h2a, h2b, rt, gate_pad, cnt = _outproj(att_p, att_s, rec_p, rec_s, mlp, x, mod_tiles[l],
                                                  norm2_g[l][None, :], w_out[l].astype(BF16),
                                                  router_w[l], router_b[l][None, :])
        dest_flat, plan = _route(rt, cnt[0, :N_EXPERTS].astype(jnp.int32))
        x_sorted = _sc_scatter_rows((h2a, h2b), dest_flat, MOE_SLOTS)
        y_sorted = _moe(l, plan, x_sorted, w_gate, b_gate, w_up, b_up, w_down, b_down)
        y_tok = _sc_gather_rows(y_sorted, dest_flat)
        moe_out = ([yt.reshape(TOP_K, T_ALL, D_SLAB) for yt in y_tok], gate_pad)

    y_prompt, y_sample = _final(x, moe_out, mod_tiles[DEPTH - 1], final_g[None, :])
    y_prompt = y_prompt.reshape(BATCH, SEQ, D_MODEL)
    y_sample = y_sample.reshape(DEC_BATCH, DEC_SEQ, D_MODEL)
    new_k, new_v = (cache.reshape(BATCH, DEPTH, NA_HEADS, NA_HEAD_DIM, SEQ).transpose(0, 1, 4, 2, 3)
                    for cache in caches)
    return (y_prompt, y_sample, new_k, new_v, jnp.stack(sf_list, axis=1), jnp.stack(sb_list, axis=1))
```

```python
import functools

import numpy as np
import jax
import jax.numpy as jnp
from jax import lax
from jax.experimental import pallas as pl
from jax.experimental.pallas import tpu as pltpu
from jax.experimental.pallas import tpu_sc as plsc

F32 = jnp.float32
BF16 = jnp.bfloat16

D_MODEL = 1024
BATCH = 32
SEQ = 256
DEPTH = 2
DEC_BATCH = 2
DEC_SEQ = 1024
PAST_LEN = 512
GRID_W = 64
NA_HEADS = 8
NA_HEAD_DIM = 64
NA_WIDTH = NA_HEADS * NA_HEAD_DIM
NA_KH = 8
NA_KW = 16
HG_HEADS = 4
HG_DK = 64
HG_DV = 64
HG_WIDTH = HG_HEADS * HG_DV
HG_CHUNK = 16
F_FLOOR = 1e-30
GM_GROUPS = 4
GM_GDIM = 64
GM_WIDTH = GM_GROUPS * GM_GDIM
GM_CHUNK = 128
IN_COLS = 3 * NA_WIDTH + 5 * HG_WIDTH + 2 * GM_WIDTH
N_EXPERTS = 32
TOP_K = 4
SWIGLU_LIMIT = 7.0
SWIGLU_ALPHA = 1.702
RMS_EPS = 1e-6
NEG_INF = -1e30

T_PROMPT = BATCH * SEQ
T_SAMPLE = DEC_BATCH * DEC_SEQ
T_ALL = T_PROMPT + T_SAMPLE
TM = 512
SEQ_PER_TILE = TM // SEQ
N_TILES = T_ALL // TM
P_TILES = T_PROMPT // TM
MOE_BM = 256
MOE_SLOTS = -(-(T_ALL * TOP_K + N_EXPERTS * (MOE_BM - 1)) // MOE_BM) * MOE_BM
MOE_BLOCKS = MOE_SLOTS // MOE_BM
MOD_ROWS = 8
V7X_VMEM_LIMIT = 48 * 1024 * 1024

QKV_COLS = 3 * NA_WIDTH
REST_COLS = IN_COLS - QKV_COLS
_CB_HQ, _CB_ZF, _CB_ZB, _CB_HI, _CB_HG, _CB_GU, _CB_GV = range(7)


def _dot(a, b):
    return jnp.dot(a, b, preferred_element_type=F32)


def _dot_nt(a, b):
    return lax.dot_general(a, b, (((1,), (1,)), ((), ())), preferred_element_type=F32)


def _dot_tn(a, b):
    return lax.dot_general(a, b, (((0,), (0,)), ((), ())), preferred_element_type=F32)


def _split3(x):
    hi = x.astype(BF16)
    r1 = x - hi.astype(F32)
    mid = r1.astype(BF16)
    lo = (r1 - mid.astype(F32)).astype(BF16)
    return hi, mid, lo


D_PACK = D_MODEL // 2
N_SPLIT = 2
D_SLAB = D_PACK // N_SPLIT


def _pack_halves(x):
    half = x.shape[1] // 2
    lo = pltpu.bitcast(x[:, :half].astype(BF16).astype(F32), jnp.uint32)
    hi = pltpu.bitcast(x[:, half:].astype(BF16).astype(F32), jnp.uint32)
    return pltpu.bitcast(jnp.right_shift(lo, jnp.uint32(16)) | hi, jnp.int32)


def _unpack_halves(w):
    u = pltpu.bitcast(w, jnp.uint32)
    lo = pltpu.bitcast(jnp.left_shift(u, jnp.uint32(16)), F32)
    hi = pltpu.bitcast(u & jnp.uint32(0xFFFF0000), F32)
    return lo, hi


def _load_slabs(refs, *lead):
    return jnp.concatenate([r[lead] if lead else r[...] for r in refs], axis=1)


def _store_slabs(refs, packed):
    for si, r in enumerate(refs):
        r[...] = packed[:, si * D_SLAB:(si + 1) * D_SLAB]


def _params(n_axes=1):
    return pltpu.CompilerParams(dimension_semantics=("arbitrary",) * n_axes,
                                vmem_limit_bytes=V7X_VMEM_LIMIT)


def _mod_kernel(cond_ref, w_ref, b_ref, o_ref):
    c = cond_ref[...]
    c = c * jax.nn.sigmoid(c)
    w = w_ref[0]
    c_hi = c.astype(BF16)
    c_lo = (c - c_hi.astype(F32)).astype(BF16)
    w_hi = w.astype(BF16)
    w_lo = (w - w_hi.astype(F32)).astype(BF16)
    o_ref[0] = _dot(c_hi, w_hi) + _dot(c_lo, w_hi) + _dot(c_hi, w_lo) + b_ref[0]


def _modulation(cond, w_mod, b_mod):
    tn = 1536
    return pl.pallas_call(
        _mod_kernel,
        out_shape=jax.ShapeDtypeStruct((DEPTH, MOD_ROWS, 6 * D_MODEL), F32),
        grid=(DEPTH, 6 * D_MODEL // tn),
        in_specs=[pl.BlockSpec((MOD_ROWS, D_MODEL), lambda l, j: (0, 0)),
                  pl.BlockSpec((1, D_MODEL, tn), lambda l, j: (l, 0, j)),
                  pl.BlockSpec((1, 1, tn), lambda l, j: (l, 0, j))],
        out_specs=pl.BlockSpec((1, MOD_ROWS, tn), lambda l, j: (l, 0, j)),
        compiler_params=_params(2),
        name="modulation",
    )(cond, w_mod, b_mod.reshape(DEPTH, 1, 6 * D_MODEL))


def _rms_mod(x, g, shift, scale):
    ms = jnp.mean(x * x, axis=-1, keepdims=True)
    y = x * lax.rsqrt(ms + RMS_EPS) * g
    return y * (1.0 + scale) + shift


def _project_in(hm, w_ref, qkv_ref, h_ref, kc_ref, vc_ref):
    h = _dot(hm.astype(BF16), w_ref[...])
    qkv_ref[...] = h[:, :QKV_COLS].astype(BF16)
    h_ref[...] = h[:, QKV_COLS:]

    @pl.when(pl.program_id(0) < P_TILES)
    def _():
        for sq in range(SEQ_PER_TILE):
            rows = slice(sq * SEQ, (sq + 1) * SEQ)
            kc_ref[sq, 0:NA_WIDTH] = h[rows, NA_WIDTH:2 * NA_WIDTH].T
            vc_ref[sq, 0:NA_WIDTH] = h[rows, 2 * NA_WIDTH:3 * NA_WIDTH].T
            if kc_ref.shape[1] > NA_WIDTH:
                kc_ref[sq, NA_WIDTH:] = jnp.zeros((kc_ref.shape[1] - NA_WIDTH, SEQ), F32)
                vc_ref[sq, NA_WIDTH:] = jnp.zeros((vc_ref.shape[1] - NA_WIDTH, SEQ), F32)


def _pick_group(p_ref, s_ref):
    return jnp.where(pl.program_id(0) < P_TILES, p_ref[...], s_ref[...])


def _p_tile(width):
    return pl.BlockSpec((TM, width), lambda i: (jnp.minimum(i, P_TILES - 1), 0))


def _s_tile(width):
    return pl.BlockSpec((TM, width), lambda i: (jnp.maximum(i - P_TILES, 0), 0))


def _inproj_first_kernel(xp_ref, xs_ref, mod_ref, g_ref, w_ref, qkv_ref, h_ref, xo_ref, kc_ref, vc_ref):
    x = _pick_group(xp_ref, xs_ref)
    xo_ref[...] = x
    hm = _rms_mod(x, g_ref[...], mod_ref[0, 0:1, :], mod_ref[0, 1:2, :])
    _project_in(hm, w_ref, qkv_ref, h_ref, kc_ref, vc_ref)


def _combine_experts(yg_refs, gate_ref):
    gates = gate_ref[...]
    lo_acc = hi_acc = None
    for kk in range(TOP_K):
        lo, hi = _unpack_halves(_load_slabs(yg_refs, kk))
        gk = gates[:, kk:kk + 1]
        lo_acc = gk * lo if lo_acc is None else lo_acc + gk * lo
        hi_acc = gk * hi if hi_acc is None else hi_acc + gk * hi
    return jnp.concatenate([lo_acc, hi_acc], axis=1)


def _inproj_next_kernel(x_ref, yga_ref, ygb_ref, gate_ref, pmod_ref, mod_ref, g_ref, w_ref, kc_in, vc_in,
                        qkv_ref, h_ref, xo_ref, kc_ref, vc_ref):
    del kc_in, vc_in
    x = x_ref[...] + pmod_ref[0, 5:6, :] * _combine_experts((yga_ref, ygb_ref), gate_ref)
    xo_ref[...] = x
    hm = _rms_mod(x, g_ref[...], mod_ref[0, 0:1, :], mod_ref[0, 1:2, :])
    _project_in(hm, w_ref, qkv_ref, h_ref, kc_ref, vc_ref)


_TILE_SPEC = pl.BlockSpec((TM, D_MODEL), lambda i: (i, 0))
_MOD_SPEC = pl.BlockSpec((1, MOD_ROWS, D_MODEL), lambda i: (i, 0, 0))
_ROW_SPEC = pl.BlockSpec((1, D_MODEL), lambda i: (0, 0))
_RT_LANES = 128
_RT_ROWS = 2 * TOP_K
_YG_SPEC = pl.BlockSpec((TOP_K, TM, D_SLAB), lambda i: (0, i, 0))
_GATE_SPEC = pl.BlockSpec((TM, _RT_LANES), lambda i: (i, 0))


def _inproj(layer, x, moe, prev_mod, mod, g, w_bf16, caches):
    w_spec = pl.BlockSpec((D_MODEL, IN_COLS), lambda i: (0, 0))
    h_spec = pl.BlockSpec((TM, REST_COLS), lambda i: (i, 0))
    h_shape = jax.ShapeDtypeStruct((T_ALL, REST_COLS), F32)
    q_spec = pl.BlockSpec((TM, QKV_COLS), lambda i: (i, 0))
    q_shape = jax.ShapeDtypeStruct((T_ALL, QKV_COLS), BF16)
    c_spec = pl.BlockSpec((SEQ_PER_TILE, NA_WIDTH, SEQ), lambda i: (jnp.minimum(i, P_TILES - 1), layer, 0))
    c_shape = jax.ShapeDtypeStruct((BATCH, DEPTH * NA_WIDTH, SEQ), F32)
    x_shape = jax.ShapeDtypeStruct((T_ALL, D_MODEL), F32)
    if moe is None:
        c_all = pl.BlockSpec((SEQ_PER_TILE, DEPTH * NA_WIDTH, SEQ), lambda i: (jnp.minimum(i, P_TILES - 1), 0, 0))
        qkv, h, x, kc, vc = pl.pallas_call(
            _inproj_first_kernel, out_shape=(q_shape, h_shape, x_shape, c_shape, c_shape), grid=(N_TILES,),
            in_specs=[_p_tile(D_MODEL), _s_tile(D_MODEL), _MOD_SPEC, _ROW_SPEC, w_spec],
            out_specs=(q_spec, h_spec, _TILE_SPEC, c_all, c_all),
            compiler_params=_params(), name="inproj_first",
        )(*x, mod, g, w_bf16)
        return qkv, h, x, (kc, vc)
    qkv, h, x, kc, vc = pl.pallas_call(
        _inproj_next_kernel,
        out_shape=(q_shape, h_shape, x_shape, c_shape, c_shape),
        grid=(N_TILES,),
        in_specs=[_TILE_SPEC, _YG_SPEC, _YG_SPEC, _GATE_SPEC, _MOD_SPEC, _MOD_SPEC, _ROW_SPEC, w_spec,
                  pl.BlockSpec(memory_space=pl.ANY), pl.BlockSpec(memory_space=pl.ANY)],
        out_specs=(q_spec, h_spec, _TILE_SPEC, c_spec, c_spec),
        input_output_aliases={8: 3, 9: 4},
        compiler_params=_params(), name="inproj_next",
    )(x, *moe[0], moe[1], prev_mod, mod, g, w_bf16, *caches)
    return qkv, h, x, (kc, vc)


def _pair_mask(hh):
    lane = lax.broadcasted_iota(jnp.int32, (1, 2 * NA_HEAD_DIM), 1)
    return (lane >= hh * NA_HEAD_DIM) & (lane < (hh + 1) * NA_HEAD_DIM)


_ATT_SEQS = 2


def _attn_prompt_kernel(q_ref, k_ref, v_ref, o_ref):
    scale = NA_HEAD_DIM ** -0.5
    for sq in range(_ATT_SEQS):
        rows = slice(sq * SEQ, (sq + 1) * SEQ)
        for p in range(NA_HEADS // 2):
            cols = slice(p * 128, (p + 1) * 128)
            qp = q_ref[rows, cols] * scale
            kp = k_ref[rows, cols]
            vp = v_ref[rows, cols]
            outs = []
            for hh in range(2):
                qh = jnp.where(_pair_mask(hh), qp, jnp.zeros_like(qp))
                s = _dot_nt(qh, kp)
                e = jnp.exp(s - jnp.max(s, axis=-1, keepdims=True))
                den = jnp.sum(e, axis=-1, keepdims=True)
                outs.append(_dot(e.astype(BF16), vp) / den)
            o_ref[rows, cols] = jnp.where(_pair_mask(0), outs[0], outs[1]).astype(o_ref.dtype)


def _attn_prompt(qkv):
    rows = _ATT_SEQS * SEQ
    return pl.pallas_call(
        _attn_prompt_kernel,
        out_shape=jax.ShapeDtypeStruct((T_PROMPT, NA_WIDTH), BF16),
        grid=(BATCH // _ATT_SEQS,),
        in_specs=[pl.BlockSpec((rows, NA_WIDTH), lambda b: (b, 0)),
                  pl.BlockSpec((rows, NA_WIDTH), lambda b: (b, 1)),
                  pl.BlockSpec((rows, NA_WIDTH), lambda b: (b, 2))],
        out_specs=pl.BlockSpec((rows, NA_WIDTH), lambda b: (b, 0)),
        compiler_params=_params(), name="attn_prompt",
    )(qkv, qkv, qkv)


_NA_ROWS = DEC_SEQ // GRID_W
_NA_LOC = NA_KH * GRID_W
_NA_STEP_ROWS = 2


def _na_window_start(r):
    return jnp.clip(r - NA_KH // 2, 0, _NA_ROWS - NA_KH)


def _attn_sample_kernel(q_ref, k_ref, v_ref, ck_ref, cv_ref, *rest):
    bias_refs, o_ref = rest[:_NA_STEP_ROWS], rest[_NA_STEP_ROWS]
    scale = NA_HEAD_DIM ** -0.5
    for p in range(NA_HEADS // 2):
        cols = slice(p * 128, (p + 1) * 128)
        kc = ck_ref[0, :, cols].astype(BF16)
        vc = cv_ref[0, :, cols].astype(BF16)
        for u in range(_NA_STEP_ROWS):
            rows = slice(u * GRID_W, (u + 1) * GRID_W)
            s0 = pl.multiple_of(_na_window_start(pl.program_id(1) * _NA_STEP_ROWS + u) * GRID_W, GRID_W)
            qp = q_ref[rows, cols] * scale
            kl = k_ref[pl.ds(s0, _NA_LOC), cols]
            vl = v_ref[pl.ds(s0, _NA_LOC), cols]
            outs = []
            for hh in range(2):
                qh = jnp.where(_pair_mask(hh), qp, jnp.zeros_like(qp))
                sl = _dot_nt(qh, kl) + bias_refs[u][0, 2 * p + hh]
                sc = _dot_nt(qh, kc)
                mx = jnp.maximum(jnp.max(sl, axis=-1, keepdims=True),
                                 jnp.max(sc, axis=-1, keepdims=True))
                el = jnp.exp(sl - mx)
                ec = jnp.exp(sc - mx)
                den = jnp.sum(el, axis=-1, keepdims=True) + jnp.sum(ec, axis=-1, keepdims=True)
                outs.append((_dot(el.astype(BF16), vl) + _dot(ec.astype(BF16), vc)) / den)
            o_ref[rows, cols] = jnp.where(_pair_mask(0), outs[0], outs[1]).astype(o_ref.dtype)


def _attn_sample(qkv, ck, cv, bias):
    q_rows = _NA_STEP_ROWS * GRID_W
    steps = _NA_ROWS // _NA_STEP_ROWS
    q_blk0 = T_PROMPT // q_rows
    kv_row0 = T_PROMPT // DEC_SEQ

    def bias_spec(u):
        def index(b, r2):
            r = r2 * _NA_STEP_ROWS + u
            return (_na_window_start(r) - r + NA_KH - 1, 0, 0, 0)
        return pl.BlockSpec((1, NA_HEADS, GRID_W, _NA_LOC), index)

    return pl.pallas_call(
        _attn_sample_kernel,
        out_shape=jax.ShapeDtypeStruct((T_SAMPLE, NA_WIDTH), BF16),
        grid=(DEC_BATCH, steps),
        in_specs=[pl.BlockSpec((q_rows, NA_WIDTH), lambda b, r2: (q_blk0 + b * steps + r2, 0)),
                  pl.BlockSpec((DEC_SEQ, NA_WIDTH), lambda b, r2: (kv_row0 + b, 1)),
                  pl.BlockSpec((DEC_SEQ, NA_WIDTH), lambda b, r2: (kv_row0 + b, 2)),
                  pl.BlockSpec((1, PAST_LEN, NA_WIDTH), lambda b, r2: (b, 0, 0)),
                  pl.BlockSpec((1, PAST_LEN, NA_WIDTH), lambda b, r2: (b, 0, 0))]
                 + [bias_spec(u) for u in range(_NA_STEP_ROWS)],
        out_specs=pl.BlockSpec((q_rows, NA_WIDTH), lambda b, r2: (b * steps + r2, 0)),
        compiler_params=_params(2), name="attn_sample",
    )(qkv, qkv, qkv, ck, cv, *([bias] * _NA_STEP_ROWS))


_NA_DR = 2 * NA_KH - 1
_NA_DC = 2 * NA_KW - 1


def _na_bias_kernel(rb_ref, o_ref):
    i = pl.program_id(0)
    qc = lax.broadcasted_iota(jnp.int32, (GRID_W, GRID_W), 0)
    kc = lax.broadcasted_iota(jnp.int32, (GRID_W, GRID_W), 1)
    q_start = jnp.clip(qc - NA_KW // 2, 0, GRID_W - NA_KW)
    in_win = (kc >= q_start) & (kc < q_start + NA_KW)
    dc = jnp.clip(kc - qc + NA_KW - 1, 0, _NA_DC - 1)
    picks = [dc == d for d in range(_NA_DC)]
    tiles = []
    for dr in range(_NA_DR):
        acc = jnp.zeros((GRID_W, GRID_W), F32)
        for d in range(_NA_DC):
            acc = jnp.where(picks[d], rb_ref[i, dr * _NA_DC + d], acc)
        tiles.append(jnp.where(in_win, acc, NEG_INF))
    for base in range(NA_KH):
        o_ref[0, base, 0] = jnp.concatenate(tiles[base:base + NA_KH], axis=1)


def _na_bias_tables(rel_bias):
    rb = rel_bias.astype(F32).reshape(DEPTH * NA_HEADS, _NA_DR * _NA_DC)
    return pl.pallas_call(
        _na_bias_kernel,
        out_shape=jax.ShapeDtypeStruct((DEPTH, NA_KH, NA_HEADS, GRID_W, _NA_LOC), F32),
        grid=(DEPTH * NA_HEADS,),
        in_specs=[pl.BlockSpec(memory_space=pltpu.SMEM)],
        out_specs=pl.BlockSpec((1, NA_KH, 1, GRID_W, _NA_LOC),
                               lambda i: (i // NA_HEADS, 0, i % NA_HEADS, 0, 0)),
        compiler_params=_params(), name="na_bias_tables",
    )(rb)


_HG_GROUP = 8


def _hgrn_kernel(*refs, n_tok, has_state):
    if has_state:
        (q_ref, zf_ref, zb_ref, v_ref, g_ref, lbf_ref, lbb_ref, og_ref, s0f_ref, s0b_ref,
         rec_ref, sf_ref, sb_ref, kf_s, bf_s, kb_s, bb_s, of_s, ob_s, zf_s, zb_s, qsf_s, qsb_s, stf_s, stb_s) = refs
    else:
        (q_ref, zf_ref, zb_ref, v_ref, g_ref, lbf_ref, lbb_ref, og_ref,
         rec_ref, sf_ref, sb_ref, kf_s, bf_s, kb_s, bb_s, of_s, ob_s, zf_s, zb_s, qsf_s, qsb_s, stf_s, stb_s) = refs
        s0f_ref = s0b_ref = None
    C = HG_CHUNK
    W = HG_WIDTH
    n_chunks = n_tok // C
    rr = lax.broadcasted_iota(jnp.int32, (W, W), 0)
    cc = lax.broadcasted_iota(jnp.int32, (W, W), 1)
    log2_c = C.bit_length() - 1
    same_chunk = jnp.right_shift(rr, log2_c) == jnp.right_shift(cc, log2_c)
    tri_prefix = jnp.where(same_chunk & (cc <= rr), 1.0, 0.0).astype(BF16)
    tri_suffix = jnp.where(same_chunk & (cc >= rr), 1.0, 0.0).astype(BF16)
    same_head = jnp.right_shift(rr, 6) == jnp.right_shift(cc, 6)
    head_ones = jnp.where(same_head, 1.0, 0.0).astype(BF16)

    for ti in range(n_tok // W):
        rows = slice(ti * W, (ti + 1) * W)
        for z_ref, lb_ref, k_s, b_s, tri in ((zf_ref, lbf_ref, kf_s, bf_s, tri_prefix),
                                             (zb_ref, lbb_ref, kb_s, bb_s, tri_suffix)):
            z = z_ref[rows, :]
            lb = lb_ref[...]
            f = lb + (1.0 - lb) * jax.nn.sigmoid(z)
            logf = jnp.log(jnp.maximum(f, F_FLOOR))
            k_s[rows, :] = (1.0 - lb) * jax.nn.sigmoid(-z)
            hi, mid, lo = _split3(logf)
            b_s[rows, :] = _dot(tri, hi) + _dot(tri, mid) + _dot(tri, lo)

    G = _HG_GROUP
    n_groups = C // G
    srow = lax.broadcasted_iota(jnp.int32, (G, W), 0)
    zf_s[...] = jnp.zeros_like(zf_s)
    zb_s[...] = jnp.zeros_like(zb_s)

    def scan_chunk(ci, k_s, b_s, z_s, qs_s, st_s, o_dir_s, fwd):
        c = ci if fwd else n_chunks - 1 - ci
        base = pl.multiple_of(c * C, C)
        q = q_ref[pl.ds(base, C), :]
        k = k_s[pl.ds(base, C), :]
        b = b_s[pl.ds(base, C), :]
        v = v_ref[pl.ds(base, C), :]
        k_far = {}
        for gt in range(n_groups):
            others = range(gt) if fwd else range(gt + 1, n_groups)
            if not others:
                continue
            rows_t = slice(gt * G, (gt + 1) * G)
            edge = gt * G - 1 if fwd else (gt + 1) * G
            b_edge = b[edge:edge + 1, :]
            qs_s[rows_t, :] = q[rows_t] * jnp.exp(b[rows_t] - b_edge)
            for gs in others:
                rows_s = slice(gs * G, (gs + 1) * G)
                k_far[gt, gs] = k[rows_s] * jnp.exp(b_edge - b[rows_s])
        for t in range(C):
            gt = t // G
            rows_t = slice(gt * G, (gt + 1) * G)
            qt = q_ref[pl.ds(base + t, 1), :]
            bt = b_s[pl.ds(base + t, 1), :]
            keep = (srow + gt * G <= t) if fwd else (srow + gt * G >= t)
            z_s[t * C + gt * G:t * C + (gt + 1) * G, :] = jnp.where(
                keep, (qt * k[rows_t]) * jnp.exp(bt - b[rows_t]), 0.0)
            others = range(gt) if fwd else range(gt + 1, n_groups)
            if others:
                qst = qs_s[t:t + 1, :]
                for gs in others:
                    z_s[t * C + gs * G:t * C + (gs + 1) * G, :] = qst * k_far[gt, gs]
        a_rep = _dot(z_s[...].astype(BF16), head_ones)
        o_intra = jnp.sum(a_rep.reshape(C, C, W) * v[None, :, :], axis=1)
        b_end = b_s[pl.ds(base + (C - 1 if fwd else 0), 1), :]
        q_in = q * jnp.exp(b)
        k_st = k * jnp.exp(b_end - b)
        st = st_s[...]
        o_inter = _dot_nt(q_in.astype(BF16), st.astype(BF16))
        upd = _dot_tn(v.astype(BF16), k_st.astype(BF16))
        st_s[...] = st * jnp.exp(b_end) + jnp.where(same_head, upd, 0.0)
        o_dir_s[pl.ds(base, C), :] = o_intra + o_inter

    def load_state(s0_ref, st_s):
        if s0_ref is None:
            st_s[...] = jnp.zeros((W, W), F32)
            return
        for hh in range(HG_HEADS):
            parts = [s0_ref[0, hh] if g == hh else jnp.zeros((HG_DK, HG_DV), F32) for g in range(HG_HEADS)]
            st_s[hh * HG_DK:(hh + 1) * HG_DK, :] = jnp.concatenate(parts, axis=1)
        st_s[...] = st_s[...].T

    def store_state(st_s, out_ref):
        by_head = st_s[...].T
        for hh in range(HG_HEADS):
            out_ref[0, hh] = by_head[hh * HG_DK:(hh + 1) * HG_DK, hh * HG_DV:(hh + 1) * HG_DV]

    load_state(s0f_ref, stf_s)
    load_state(s0b_ref, stb_s)

    def scan_both(ci, carry):
        scan_chunk(ci, kf_s, bf_s, zf_s, qsf_s, stf_s, of_s, True)
        scan_chunk(ci, kb_s, bb_s, zb_s, qsb_s, stb_s, ob_s, False)
        return carry
    lax.fori_loop(0, n_chunks, scan_both, 0)
    store_state(stf_s, sf_ref)
    store_state(stb_s, sb_ref)

    for ti in range(n_tok // W):
        rows = slice(ti * W, (ti + 1) * W)
        o = of_s[rows, :] + ob_s[rows, :]
        sq = o * o
        sq_hi = sq.astype(BF16)
        sq_lo = (sq - sq_hi.astype(F32)).astype(BF16)
        ms = (_dot(sq_hi, head_ones) + _dot(sq_lo, head_ones)) * (1.0 / HG_DV)
        g = g_ref[rows, :]
        y = o * lax.rsqrt(ms + RMS_EPS) * og_ref[...] * (g * jax.nn.sigmoid(g))
        rec_ref[rows, :] = y.astype(rec_ref.dtype)


def _hgrn(h, lbf, lbb, og, s0f, s0b, n_tok, n_seq, row0):
    W = HG_WIDTH
    has_state = s0f is not None

    def col(cb):
        return pl.BlockSpec((n_tok, W), lambda i, cb=cb: (row0 + i, cb))

    vec = pl.BlockSpec((1, W), lambda i: (0, 0))
    st_spec = pl.BlockSpec((1, HG_HEADS, HG_DK, HG_DV), lambda i: (i, 0, 0, 0))
    in_specs = [col(_CB_HQ), col(_CB_ZF), col(_CB_ZB), col(_CB_HI), col(_CB_HG), vec, vec, vec]
    args = [h, h, h, h, h, lbf, lbb, og]
    if has_state:
        in_specs += [st_spec, st_spec]
        args += [s0f, s0b]
    seq_f32 = pltpu.VMEM((n_tok, W), F32)
    return pl.pallas_call(
        functools.partial(_hgrn_kernel, n_tok=n_tok, has_state=has_state),
        out_shape=(jax.ShapeDtypeStruct((n_seq * n_tok, W), BF16),
                   jax.ShapeDtypeStruct((n_seq, HG_HEADS, HG_DK, HG_DV), F32),
                   jax.ShapeDtypeStruct((n_seq, HG_HEADS, HG_DK, HG_DV), F32)),
        grid=(n_seq,),
        in_specs=in_specs,
        out_specs=(pl.BlockSpec((n_tok, W), lambda i: (i, 0)), st_spec, st_spec),
        scratch_shapes=[seq_f32, seq_f32, seq_f32, seq_f32, seq_f32, seq_f32,
                        pltpu.VMEM((HG_CHUNK * HG_CHUNK, W), F32),
                        pltpu.VMEM((HG_CHUNK * HG_CHUNK, W), F32),
                        pltpu.VMEM((HG_CHUNK, W), F32),
                        pltpu.VMEM((HG_CHUNK, W), F32),
                        pltpu.VMEM((W, W), F32),
                        pltpu.VMEM((W, W), F32)],
        compiler_params=_params(), name="hgrn_state" if has_state else "hgrn_zero",
    )(*args)


def _gmlp_kernel(u_ref, v_ref, g_ref, ws_ref, b_ref, o_ref):
    lane = lax.broadcasted_iota(jnp.int32, (1, GM_WIDTH), 1)
    for ci in range(TM // GM_CHUNK):
        rows = slice(ci * GM_CHUNK, (ci + 1) * GM_CHUNK)
        v = v_ref[rows, :]
        ms = jnp.mean(v * v, axis=-1, keepdims=True)
        vn = (v * lax.rsqrt(ms + RMS_EPS) * g_ref[...]).astype(BF16)
        z = b_ref[...]
        for gi in range(GM_GROUPS):
            zg = _dot(ws_ref[gi], vn)
            in_group = (lane >= gi * GM_GDIM) & (lane < (gi + 1) * GM_GDIM)
            z = z + jnp.where(in_group, zg, 0.0)
        o_ref[rows, :] = (u_ref[rows, :] * z).astype(o_ref.dtype)


def _gmlp(h, vnorm_g, ws_bf16, bias_full):
    W = GM_WIDTH
    return pl.pallas_call(
        _gmlp_kernel,
        out_shape=jax.ShapeDtypeStruct((T_ALL, W), BF16),
        grid=(N_TILES,),
        in_specs=[pl.BlockSpec((TM, W), lambda i: (i, _CB_GU)),
                  pl.BlockSpec((TM, W), lambda i: (i, _CB_GV)),
                  pl.BlockSpec((1, W), lambda i: (0, 0)),
                  pl.BlockSpec((GM_GROUPS, GM_CHUNK, GM_CHUNK), lambda i: (0, 0, 0)),
                  pl.BlockSpec((GM_CHUNK, W), lambda i: (0, 0))],
        out_specs=pl.BlockSpec((TM, W), lambda i: (i, 0)),
        compiler_params=_params(), name="gmlp",
    )(h, h, vnorm_g, ws_bf16, bias_full)


def _outproj_kernel(attp_ref, atts_ref, recp_ref, recs_ref, mlp_ref, x_ref, mod_ref, g_ref, w_ref, wr_ref, br_ref,
                    x1_ref, h2a_ref, h2b_ref, rt_ref, gate_ref, cnt_ref):
    @pl.when(pl.program_id(0) == 0)
    def _():
        cnt_ref[...] = jnp.zeros_like(cnt_ref)

    out = (_dot(_pick_group(attp_ref, atts_ref), w_ref[0:NA_WIDTH, :])
           + _dot(_pick_group(recp_ref, recs_ref), w_ref[NA_WIDTH:NA_WIDTH + HG_WIDTH, :])
           + _dot(mlp_ref[...], w_ref[NA_WIDTH + HG_WIDTH:, :]))
    x1 = x_ref[...] + mod_ref[0, 2:3, :] * out
    x1_ref[...] = x1
    h2 = _rms_mod(x1, g_ref[...], mod_ref[0, 3:4, :], mod_ref[0, 4:5, :])
    _store_slabs((h2a_ref, h2b_ref), _pack_halves(h2))
    h_hi = h2.astype(BF16)
    h_lo = (h2 - h_hi.astype(F32)).astype(BF16)
    wr = wr_ref[...]
    w_hi = wr.astype(BF16)
    w_lo = (wr - w_hi.astype(F32)).astype(BF16)
    logits = _dot(h_hi, w_hi) + _dot(h_lo, w_hi) + _dot(h_hi, w_lo) + br_ref[...]
    lane_e = lax.broadcasted_iota(jnp.int32, (TM, N_EXPERTS), 1).astype(F32)
    lane_o = lax.broadcasted_iota(jnp.int32, (TM, _RT_LANES), 1)
    idx_acc = jnp.zeros((TM, _RT_LANES), F32)
    val_acc = jnp.zeros((TM, _RT_LANES), F32)
    top0 = None
    den = jnp.zeros((TM, 1), F32)
    work = logits
    picks = []
    for kk in range(TOP_K):
        m = jnp.max(work, axis=-1, keepdims=True)
        first = jnp.min(jnp.where(work == m, lane_e, float(N_EXPERTS)), axis=-1, keepdims=True)
        if kk == 0:
            top0 = m
        e = jnp.exp(m - top0)
        den = den + e
        idx_acc = jnp.where(lane_o == kk, first, idx_acc)
        val_acc = jnp.where(lane_o == kk, e, val_acc)
        picks.append(lane_e == first)
        work = jnp.where(picks[-1], -jnp.inf, work)
    gate_ref[...] = val_acc / den
    sel = jnp.zeros((TM, N_EXPERTS), F32)
    for pk in picks:
        sel = sel + jnp.where(pk, 1.0, 0.0)
    rr = lax.broadcasted_iota(jnp.int32, (TM, TM), 0)
    cc = lax.broadcasted_iota(jnp.int32, (TM, TM), 1)
    earlier = jnp.where(cc < rr, 1.0, 0.0).astype(BF16)
    seen = cnt_ref[0:1, 0:N_EXPERTS]
    before = _dot(earlier, sel.astype(BF16)) + seen
    for kk, pk in enumerate(picks):
        rank = jnp.sum(jnp.where(pk, before, 0.0), axis=-1, keepdims=True)
        idx_acc = jnp.where(lane_o == TOP_K + kk, rank, idx_acc)
    rt_ref[...] = idx_acc.T[0:_RT_ROWS, :].astype(jnp.int32)
    cnt_ref[0:1, 0:N_EXPERTS] = seen + jnp.sum(sel, axis=0, keepdims=True)


def _outproj(att_p, att_s, rec_p, rec_s, mlp, x, mod, g, w_bf16, wr, br):
    def tile(width):
        return pl.BlockSpec((TM, width), lambda i: (i, 0))

    return pl.pallas_call(
        _outproj_kernel,
        out_shape=(jax.ShapeDtypeStruct((T_ALL, D_MODEL), F32),
                   jax.ShapeDtypeStruct((T_ALL, D_SLAB), jnp.int32),
                   jax.ShapeDtypeStruct((T_ALL, D_SLAB), jnp.int32),
                   jax.ShapeDtypeStruct((_RT_ROWS, T_ALL), jnp.int32),
                   jax.ShapeDtypeStruct((T_ALL, _RT_LANES), F32),
                   jax.ShapeDtypeStruct((8, _RT_LANES), F32)),
        grid=(N_TILES,),
        in_specs=[_p_tile(NA_WIDTH), _s_tile(NA_WIDTH), _p_tile(HG_WIDTH), _s_tile(HG_WIDTH),
                  tile(GM_WIDTH), _TILE_SPEC, _MOD_SPEC, _ROW_SPEC,
                  pl.BlockSpec((D_MODEL, D_MODEL), lambda i: (0, 0)),
                  pl.BlockSpec((D_MODEL, N_EXPERTS), lambda i: (0, 0)),
                  pl.BlockSpec((1, N_EXPERTS), lambda i: (0, 0))],
        out_specs=(_TILE_SPEC, tile(D_SLAB), tile(D_SLAB), pl.BlockSpec((_RT_ROWS, TM), lambda i: (0, i)),
                   tile(_RT_LANES),
                   pl.BlockSpec((8, _RT_LANES), lambda i: (0, 0))),
        compiler_params=_params(), name="outproj_router",
    )(att_p, att_s, rec_p, rec_s, mlp, x, mod, g, w_bf16, wr, br)


_W_CHUNKS = 4
_W_CAST_ROWS = 128


def _moe_kernel(blk_e_ref, blk_on_ref, blk_new_ref, blk_next_ref,
                xa_ref, xb_ref, wg_hbm, bg_ref, wu_hbm, bu_ref, wd_hbm, bd_ref,
                ya_ref, yb_ref, w_f32, w_bf16, w_sem, *, layer):
    j = pl.program_id(0)

    def weight_copies(expert):
        rows = D_MODEL // _W_CHUNKS
        return [pltpu.make_async_copy(w_hbm.at[layer, expert, pl.ds(ci * rows, rows)],
                                      w_f32.at[wi, pl.ds(ci * rows, rows)], w_sem.at[wi, ci])
                for wi, w_hbm in enumerate((wg_hbm, wu_hbm, wd_hbm)) for ci in range(_W_CHUNKS)]

    @pl.when(j == 0)
    def _():
        for cp in weight_copies(blk_e_ref[0]):
            cp.start()

    @pl.when(blk_new_ref[j] != 0)
    def _():
        for cp in weight_copies(blk_e_ref[j]):
            cp.wait()

        def cast_rows(ci, carry):
            rows = pl.ds(pl.multiple_of(ci * _W_CAST_ROWS, _W_CAST_ROWS), _W_CAST_ROWS)
            for wi in range(3):
                w_bf16[wi, rows, :] = w_f32[wi, rows, :].astype(BF16)
            return carry
        lax.fori_loop(0, D_MODEL // _W_CAST_ROWS, cast_rows, 0)

        @pl.when(blk_next_ref[j] >= 0)
        def _():
            for cp in weight_copies(blk_next_ref[j]):
                cp.start()

    @pl.when(blk_on_ref[j] != 0)
    def _():
        lo, hi = _unpack_halves(_load_slabs((xa_ref, xb_ref)))
        x = jnp.concatenate([lo.astype(BF16), hi.astype(BF16)], axis=1)
        gate = jnp.minimum(_dot(x, w_bf16[0]) + bg_ref[0, 0], SWIGLU_LIMIT)
        up = jnp.clip(_dot(x, w_bf16[1]) + bu_ref[0, 0], -SWIGLU_LIMIT, SWIGLU_LIMIT)
        glu = gate * jax.nn.sigmoid(SWIGLU_ALPHA * gate)
        act = ((up + 1.0) * glu).astype(BF16)
        _store_slabs((ya_ref, yb_ref), _pack_halves(_dot(act, w_bf16[2]) + bd_ref[0, 0]))

    @pl.when(blk_on_ref[j] == 0)
    def _():
        ya_ref[...] = jnp.zeros_like(ya_ref)
        yb_ref[...] = jnp.zeros_like(yb_ref)


def _moe(layer, plan, x_sorted, wg, bg, wu, bu, wd, bd):
    n_plan = len(plan)
    b_spec = pl.BlockSpec((1, 1, 1, D_MODEL), lambda j, be, *_: (layer, be[j], 0, 0))
    x_spec = pl.BlockSpec((MOE_BM, D_SLAB), lambda j, *_: (j, 0))
    hbm = pl.BlockSpec(memory_space=pl.ANY)
    bias4 = lambda b: b.reshape(DEPTH, N_EXPERTS, 1, D_MODEL)
    return pl.pallas_call(
        functools.partial(_moe_kernel, layer=layer),
        out_shape=(jax.ShapeDtypeStruct((MOE_SLOTS, D_SLAB), jnp.int32),) * N_SPLIT,
        grid_spec=pltpu.PrefetchScalarGridSpec(
            num_scalar_prefetch=n_plan, grid=(MOE_BLOCKS,),
            in_specs=[x_spec, x_spec, hbm, b_spec, hbm, b_spec, hbm, b_spec],
            out_specs=(x_spec, x_spec),
            scratch_shapes=[pltpu.VMEM((3, D_MODEL, D_MODEL), F32), pltpu.VMEM((3, D_MODEL, D_MODEL), BF16),
                            pltpu.SemaphoreType.DMA((3, _W_CHUNKS))]),
        compiler_params=_params(), name="moe_experts",
    )(*plan, *x_sorted, wg, bias4(bg), wu, bias4(bu), wd, bias4(bd))


def _route(rt, counts):
    experts = jnp.arange(N_EXPERTS, dtype=jnp.int32)
    nblk = (counts + MOE_BM - 1) // MOE_BM
    blk_end = jnp.cumsum(nblk)
    row0 = (blk_end - nblk) * MOE_BM
    top_i, rank = rt[:TOP_K], rt[TOP_K:]
    start_of = jnp.sum(jnp.where(top_i[None] == experts[:, None, None], row0[:, None, None], 0), axis=0)
    dest = (start_of + rank).reshape(1, TOP_K * T_ALL)
    live = counts > 0
    last_live = jnp.max(jnp.where(live, experts, 0))
    later_live = live[None, :] & (experts[None, :] > experts[:, None])
    next_live = jnp.min(jnp.where(later_live, experts[None, :], N_EXPERTS), axis=1)
    next_live = jnp.where(next_live == N_EXPERTS, -1, next_live)
    blk = jnp.arange(MOE_BLOCKS, dtype=jnp.int32)
    blk_on = blk < blk_end[-1]
    blk_e = jnp.where(blk_on, jnp.minimum(jnp.sum((blk_end[None, :] <= blk[:, None]).astype(jnp.int32), axis=1),
                                          N_EXPERTS - 1), last_live)
    blk_new = blk_on & jnp.concatenate([jnp.ones((1,), bool), blk_e[1:] != blk_e[:-1]])
    is_e = blk_e[:, None] == experts[None, :]
    lookup = lambda table: jnp.sum(jnp.where(is_e, table[None, :], 0), axis=1)
    plan = (blk_e, blk_on, blk_new, lookup(next_live))
    return dest.astype(jnp.int32), tuple(p.astype(jnp.int32) for p in plan)


_SC_WINDOW = 128


def _sc_mesh():
    return plsc.VectorSubcoreMesh(core_axis_name="core", subcore_axis_name="subcore")


def _sc_scatter_rows(srcs, idx, n_out):
    n_src, width = srcs[0].shape
    n_idx = idx.shape[1]
    src_windows = n_src // _SC_WINDOW

    def body(*refs):
        x_hbm = refs[:len(srcs)]
        i_hbm = refs[len(srcs)]
        o_hbm = refs[len(srcs) + 1:]
        for xs, os_ in zip(x_hbm, o_hbm):
            def step(x_vmem, i_vmem, os_=os_):
                pltpu.sync_copy(x_vmem, os_.at[i_vmem.at[0]])

            pltpu.emit_pipeline(
                step, grid=(n_idx // _SC_WINDOW,),
                in_specs=[pl.BlockSpec((_SC_WINDOW, width), lambda i: (i % src_windows, 0)),
                          pl.BlockSpec((1, _SC_WINDOW), lambda i: (0, i))],
                out_specs=[],
                core_axis_name=("core", "subcore"),
                dimension_semantics=(pltpu.PARALLEL,),
            )(xs, i_hbm)

    out_type = tuple(jax.ShapeDtypeStruct((n_out, width), s.dtype) for s in srcs)
    return pl.kernel(body, out_type=out_type, mesh=_sc_mesh(), scratch_types=[],
                     name="sc_scatter_rows")(*srcs, idx)


def _sc_gather_rows(tables, idx):
    n_idx = idx.shape[1]
    width = tables[0].shape[1]

    def body(*refs):
        t_hbm = refs[:len(tables)]
        i_hbm = refs[len(tables)]
        o_hbm = refs[len(tables) + 1:]
        for ts, os_ in zip(t_hbm, o_hbm):
            def step(i_vmem, o_vmem, ts=ts):
                pltpu.sync_copy(ts.at[i_vmem.at[0]], o_vmem)

            pltpu.emit_pipeline(
                step, grid=(n_idx // _SC_WINDOW,),
                in_specs=[pl.BlockSpec((1, _SC_WINDOW), lambda i: (0, i))],
                out_specs=[pl.BlockSpec((_SC_WINDOW, width), lambda i: (i, 0))],
                core_axis_name=("core", "subcore"),
                dimension_semantics=(pltpu.PARALLEL,),
            )(i_hbm, os_)

    out_type = tuple(jax.ShapeDtypeStruct((n_idx, width), t.dtype) for t in tables)
    return pl.kernel(body, out_type=out_type, mesh=_sc_mesh(), scratch_types=[],
                     name="sc_gather_rows")(*tables, idx)


def _final_kernel(x_ref, yga_ref, ygb_ref, gate_ref, mod_ref, g_ref, yp_ref, ys_ref):
    x = x_ref[...] + mod_ref[0, 5:6, :] * _combine_experts((yga_ref, ygb_ref), gate_ref)
    ms = jnp.mean(x * x, axis=-1, keepdims=True)
    y = x * lax.rsqrt(ms + RMS_EPS) * g_ref[...]

    @pl.when(pl.program_id(0) < P_TILES)
    def _():
        yp_ref[...] = y

    @pl.when(pl.program_id(0) >= P_TILES)
    def _():
        ys_ref[...] = y


def _final(x, moe, mod, g):
    return pl.pallas_call(
        _final_kernel,
        out_shape=(jax.ShapeDtypeStruct((T_PROMPT, D_MODEL), F32), jax.ShapeDtypeStruct((T_SAMPLE, D_MODEL), F32)),
        grid=(N_TILES,),
        in_specs=[_TILE_SPEC, _YG_SPEC, _YG_SPEC, _GATE_SPEC, _MOD_SPEC, _ROW_SPEC],
        out_specs=(_p_tile(D_MODEL), _s_tile(D_MODEL)),
        compiler_params=_params(), name="final_norm",
    )(x, *moe[0], moe[1], mod, g)


def kernel(x_prompt, x_sample, cache_k, cache_v, state_hgrn_fwd, state_hgrn_bwd, c, c_ctx, w_mod, b_mod, norm1_g, norm2_g, w_in, na_rel_bias, hgrn_lb, hgrn_onorm_g, gmlp_vnorm_g, gmlp_ws, gmlp_b, w_out, router_w, router_b, w_gate, b_gate, w_up, b_up, w_down, b_down, final_g):
    x = (x_prompt.reshape(T_PROMPT, D_MODEL), x_sample.reshape(T_SAMPLE, D_MODEL))

    cond = jnp.zeros((MOD_ROWS, D_MODEL), F32).at[0].set(c_ctx).at[1:1 + DEC_BATCH].set(c)
    mod = _modulation(cond, w_mod, b_mod)
    tile_row = np.concatenate([np.zeros(P_TILES, np.int32),
                               1 + np.arange(N_TILES - P_TILES, dtype=np.int32) // (DEC_SEQ // TM)])
    mod_tiles = mod[:, tile_row].reshape(DEPTH, N_TILES, 6, D_MODEL)
    mod_tiles = jnp.pad(mod_tiles, ((0, 0), (0, 0), (0, MOD_ROWS - 6), (0, 0)))

    lb_soft = jax.nn.softmax(hgrn_lb.astype(F32), axis=1)
    lower = jnp.cumsum(lb_soft, axis=1) - lb_soft[:, :1]

    na_bias = _na_bias_tables(na_rel_bias)

    sf_list, sb_list = [], []
    moe_out = caches = None
    for l in range(DEPTH):
        qkv, h, x, caches = _inproj(l, x, moe_out, mod_tiles[l - 1] if l else None, mod_tiles[l],
                                    norm1_g[l][None, :], w_in[l].astype(BF16), caches)

        att_p = _attn_prompt(qkv)
        att_s = _attn_sample(qkv, cache_k[:, l].reshape(DEC_BATCH, PAST_LEN, NA_WIDTH),
                             cache_v[:, l].reshape(DEC_BATCH, PAST_LEN, NA_WIDTH), na_bias[l])
        lbf = lower[0, l][None, :]
        lbb = lower[1, l][None, :]
        og = jnp.tile(hgrn_onorm_g[l], HG_HEADS)[None, :]
        rec_p, sf, sb = _hgrn(h, lbf, lbb, og, None, None, SEQ, BATCH, 0)
        rec_s, _, _ = _hgrn(h, lbf, lbb, og, state_hgrn_fwd[:, l].astype(F32), state_hgrn_bwd[:, l].astype(F32),
                            DEC_SEQ, DEC_BATCH, T_PROMPT // DEC_SEQ)
        sf_list.append(sf)
        sb_list.append(sb)
        gm_bias = jnp.repeat(gmlp_b[l].T, GM_GDIM, axis=1)
        mlp = _gmlp(h, gmlp_vnorm_g[l][None, :], gmlp_ws[l].astype(BF16), gm_bias)

        x, h2a, h2b, rt, gate_pad, cnt = _outproj(att_p, att_s, rec_p, rec_s, mlp, x, mod_tiles[l],
                                                  norm2_g[l][None, :], w_out[l].astype(BF16),
                                                  router_w[l], router_b[l][None, :])
        dest_flat, plan = _route(rt, cnt[0, :N_EXPERTS].astype(jnp.int32))
        x_sorted = _sc_scatter_rows((h2a, h2b), dest_flat, MOE_SLOTS)
        y_sorted = _moe(l, plan, x_sorted, w_gate, b_gate, w_up, b_up, w_down, b_down)
        y_tok = _sc_gather_rows(y_sorted, dest_flat)
        moe_out = ([yt.reshape(TOP_K, T_ALL, D_SLAB) for yt in y_tok], gate_pad)

    y_prompt, y_sample = _final(x, moe_out, mod_tiles[DEPTH - 1], final_g[None, :])
    y_prompt = y_prompt.reshape(BATCH, SEQ, D_MODEL)
    y_sample = y_sample.reshape(DEC_BATCH, DEC_SEQ, D_MODEL)
    new_k, new_v = (cache.reshape(BATCH, DEPTH, NA_HEADS, NA_HEAD_DIM, SEQ).transpose(0, 1, 4, 2, 3)
                    for cache in caches)
    return (y_prompt, y_sample, new_k, new_v, jnp.stack(sf_list, axis=1), jnp.stack(sb_list, axis=1))
```

```python
import functools

import numpy as np
import jax
import jax.numpy as jnp
from jax import lax
from jax.experimental import pallas as pl
from jax.experimental.pallas import tpu as pltpu
from jax.experimental.pallas import tpu_sc as plsc

F32 = jnp.float32
BF16 = jnp.bfloat16

D_MODEL = 1024
BATCH = 32
SEQ = 256
DEPTH = 2
DEC_BATCH = 2
DEC_SEQ = 1024
PAST_LEN = 512
GRID_W = 64
NA_HEADS = 8
NA_HEAD_DIM = 64
NA_WIDTH = NA_HEADS * NA_HEAD_DIM
NA_KH = 8
NA_KW = 16
HG_HEADS = 4
HG_DK = 64
HG_DV = 64
HG_WIDTH = HG_HEADS * HG_DV
HG_CHUNK = 16
F_FLOOR = 1e-30
GM_GROUPS = 4
GM_GDIM = 64
GM_WIDTH = GM_GROUPS * GM_GDIM
GM_CHUNK = 128
IN_COLS = 3 * NA_WIDTH + 5 * HG_WIDTH + 2 * GM_WIDTH
N_EXPERTS = 32
TOP_K = 4
SWIGLU_LIMIT = 7.0
SWIGLU_ALPHA = 1.702
RMS_EPS = 1e-6
NEG_INF = -1e30

T_PROMPT = BATCH * SEQ
T_SAMPLE = DEC_BATCH * DEC_SEQ
T_ALL = T_PROMPT + T_SAMPLE
TM = 512
SEQ_PER_TILE = TM // SEQ
N_TILES = T_ALL // TM
P_TILES = T_PROMPT // TM
MOE_BM = 256
MOE_SLOTS = -(-(T_ALL * TOP_K + N_EXPERTS * (MOE_BM - 1)) // MOE_BM) * MOE_BM
MOE_BLOCKS = MOE_SLOTS // MOE_BM
MOD_ROWS = 8
V7X_VMEM_LIMIT = 48 * 1024 * 1024

QKV_COLS = 3 * NA_WIDTH
REST_COLS = IN_COLS - QKV_COLS
_CB_HQ, _CB_ZF, _CB_ZB, _CB_HI, _CB_HG, _CB_GU, _CB_GV = range(7)


def _dot(a, b):
    return jnp.dot(a, b, preferred_element_type=F32)


def _dot_nt(a, b):
    return lax.dot_general(a, b, (((1,), (1,)), ((), ())), preferred_element_type=F32)


def _dot_tn(a, b):
    return lax.dot_general(a, b, (((0,), (0,)), ((), ())), preferred_element_type=F32)


def _split3(x):
    hi = x.astype(BF16)
    r1 = x - hi.astype(F32)
    mid = r1.astype(BF16)
    lo = (r1 - mid.astype(F32)).astype(BF16)
    return hi, mid, lo


D_PACK = D_MODEL // 2
N_SPLIT = 2
D_SLAB = D_PACK // N_SPLIT


def _pack_halves(x):
    half = x.shape[1] // 2
    lo = pltpu.bitcast(x[:, :half].astype(BF16).astype(F32), jnp.uint32)
    hi = pltpu.bitcast(x[:, half:].astype(BF16).astype(F32), jnp.uint32)
    return pltpu.bitcast(jnp.right_shift(lo, jnp.uint32(16)) | hi, jnp.int32)


def _unpack_halves(w):
    u = pltpu.bitcast(w, jnp.uint32)
    lo = pltpu.bitcast(jnp.left_shift(u, jnp.uint32(16)), F32)
    hi = pltpu.bitcast(u & jnp.uint32(0xFFFF0000), F32)
    return lo, hi


def _load_slabs(refs, *lead):
    return jnp.concatenate([r[lead] if lead else r[...] for r in refs], axis=1)


def _store_slabs(refs, packed):
    for si, r in enumerate(refs):
        r[...] = packed[:, si * D_SLAB:(si + 1) * D_SLAB]


def _params(n_axes=1):
    return pltpu.CompilerParams(dimension_semantics=("arbitrary",) * n_axes,
                                vmem_limit_bytes=V7X_VMEM_LIMIT)


def _mod_kernel(cond_ref, w_ref, b_ref, o_ref):
    c = cond_ref[...]
    c = c * jax.nn.sigmoid(c)
    w = w_ref[0]
    c_hi = c.astype(BF16)
    c_lo = (c - c_hi.astype(F32)).astype(BF16)
    w_hi = w.astype(BF16)
    w_lo = (w - w_hi.astype(F32)).astype(BF16)
    o_ref[0] = _dot(c_hi, w_hi) + _dot(c_lo, w_hi) + _dot(c_hi, w_lo) + b_ref[0]


def _modulation(cond, w_mod, b_mod):
    tn = 1536
    return pl.pallas_call(
        _mod_kernel,
        out_shape=jax.ShapeDtypeStruct((DEPTH, MOD_ROWS, 6 * D_MODEL), F32),
        grid=(DEPTH, 6 * D_MODEL // tn),
        in_specs=[pl.BlockSpec((MOD_ROWS, D_MODEL), lambda l, j: (0, 0)),
                  pl.BlockSpec((1, D_MODEL, tn), lambda l, j: (l, 0, j)),
                  pl.BlockSpec((1, 1, tn), lambda l, j: (l, 0, j))],
        out_specs=pl.BlockSpec((1, MOD_ROWS, tn), lambda l, j: (l, 0, j)),
        compiler_params=_params(2),
        name="modulation",
    )(cond, w_mod, b_mod.reshape(DEPTH, 1, 6 * D_MODEL))


def _rms_mod(x, g, shift, scale):
    ms = jnp.mean(x * x, axis=-1, keepdims=True)
    y = x * lax.rsqrt(ms + RMS_EPS) * g
    return y * (1.0 + scale) + shift


def _project_in(hm, w_ref, qkv_ref, h_ref, kc_ref, vc_ref):
    h = _dot(hm.astype(BF16), w_ref[0])
    qkv_ref[...] = h[:, :QKV_COLS].astype(BF16)
    h_ref[...] = h[:, QKV_COLS:]

    @pl.when(pl.program_id(0) < P_TILES)
    def _():
        for sq in range(SEQ_PER_TILE):
            rows = slice(sq * SEQ, (sq + 1) * SEQ)
            kc_ref[sq, 0:NA_WIDTH] = h[rows, NA_WIDTH:2 * NA_WIDTH].T
            vc_ref[sq, 0:NA_WIDTH] = h[rows, 2 * NA_WIDTH:3 * NA_WIDTH].T
            if kc_ref.shape[1] > NA_WIDTH:
                kc_ref[sq, NA_WIDTH:] = jnp.zeros((kc_ref.shape[1] - NA_WIDTH, SEQ), F32)
                vc_ref[sq, NA_WIDTH:] = jnp.zeros((vc_ref.shape[1] - NA_WIDTH, SEQ), F32)


def _pick_group(p_ref, s_ref):
    return jnp.where(pl.program_id(0) < P_TILES, p_ref[...], s_ref[...])


def _p_tile(width):
    return pl.BlockSpec((TM, width), lambda i: (jnp.minimum(i, P_TILES - 1), 0))


def _s_tile(width):
    return pl.BlockSpec((TM, width), lambda i: (jnp.maximum(i - P_TILES, 0), 0))


def _inproj_first_kernel(xp_ref, xs_ref, mod_ref, g_ref, w_ref, qkv_ref, h_ref, xo_ref, kc_ref, vc_ref):
    x = _pick_group(xp_ref, xs_ref)
    xo_ref[...] = x
    hm = _rms_mod(x, g_ref[...], mod_ref[0, 0:1, :], mod_ref[0, 1:2, :])
    _project_in(hm, w_ref, qkv_ref, h_ref, kc_ref, vc_ref)


def _combine_experts(yg_refs, gate_ref):
    gates = gate_ref[...]
    lo_acc = hi_acc = None
    for kk in range(TOP_K):
        lo, hi = _unpack_halves(_load_slabs(yg_refs, kk))
        gk = gates[:, kk:kk + 1]
        lo_acc = gk * lo if lo_acc is None else lo_acc + gk * lo
        hi_acc = gk * hi if hi_acc is None else hi_acc + gk * hi
    return jnp.concatenate([lo_acc, hi_acc], axis=1)


def _inproj_next_kernel(x_ref, yga_ref, ygb_ref, gate_ref, pmod_ref, mod_ref, g_ref, w_ref, kc_in, vc_in,
                        qkv_ref, h_ref, xo_ref, kc_ref, vc_ref):
    del kc_in, vc_in
    x = x_ref[...] + pmod_ref[0, 5:6, :] * _combine_experts((yga_ref, ygb_ref), gate_ref)
    xo_ref[...] = x
    hm = _rms_mod(x, g_ref[...], mod_ref[0, 0:1, :], mod_ref[0, 1:2, :])
    _project_in(hm, w_ref, qkv_ref, h_ref, kc_ref, vc_ref)


_TILE_SPEC = pl.BlockSpec((TM, D_MODEL), lambda i: (i, 0))
_MOD_SPEC = pl.BlockSpec((1, MOD_ROWS, D_MODEL), lambda i: (i, 0, 0))
_ROW_SPEC = pl.BlockSpec((1, D_MODEL), lambda i: (0, 0))
_RT_LANES = 128
_RT_ROWS = 2 * TOP_K
_YG_SPEC = pl.BlockSpec((TOP_K, TM, D_SLAB), lambda i: (0, i, 0))
_GATE_SPEC = pl.BlockSpec((TM, _RT_LANES), lambda i: (i, 0))


def _inproj(layer, x, moe, prev_mod, mod, g, w_bf16, caches):
    w_spec = pl.BlockSpec((1, D_MODEL, IN_COLS), lambda i: (layer, 0, 0))
    h_spec = pl.BlockSpec((TM, REST_COLS), lambda i: (i, 0))
    h_shape = jax.ShapeDtypeStruct((T_ALL, REST_COLS), F32)
    q_spec = pl.BlockSpec((TM, QKV_COLS), lambda i: (i, 0))
    q_shape = jax.ShapeDtypeStruct((T_ALL, QKV_COLS), BF16)
    c_spec = pl.BlockSpec((SEQ_PER_TILE, NA_WIDTH, SEQ), lambda i: (jnp.minimum(i, P_TILES - 1), layer, 0))
    c_shape = jax.ShapeDtypeStruct((BATCH, DEPTH * NA_WIDTH, SEQ), F32)
    x_shape = jax.ShapeDtypeStruct((T_ALL, D_MODEL), F32)
    if moe is None:
        c_all = pl.BlockSpec((SEQ_PER_TILE, DEPTH * NA_WIDTH, SEQ), lambda i: (jnp.minimum(i, P_TILES - 1), 0, 0))
        qkv, h, x, kc, vc = pl.pallas_call(
            _inproj_first_kernel, out_shape=(q_shape, h_shape, x_shape, c_shape, c_shape), grid=(N_TILES,),
            in_specs=[_p_tile(D_MODEL), _s_tile(D_MODEL), _MOD_SPEC, _ROW_SPEC, w_spec],
            out_specs=(q_spec, h_spec, _TILE_SPEC, c_all, c_all),
            compiler_params=_params(), name="inproj_first",
        )(*x, mod, g, w_bf16)
        return qkv, h, x, (kc, vc)
    qkv, h, x, kc, vc = pl.pallas_call(
        _inproj_next_kernel,
        out_shape=(q_shape, h_shape, x_shape, c_shape, c_shape),
        grid=(N_TILES,),
        in_specs=[_TILE_SPEC, _YG_SPEC, _YG_SPEC, _GATE_SPEC, _MOD_SPEC, _MOD_SPEC, _ROW_SPEC, w_spec,
                  pl.BlockSpec(memory_space=pl.ANY), pl.BlockSpec(memory_space=pl.ANY)],
        out_specs=(q_spec, h_spec, _TILE_SPEC, c_spec, c_spec),
        input_output_aliases={8: 3, 9: 4},
        compiler_params=_params(), name="inproj_next",
    )(x, *moe[0], moe[1], prev_mod, mod, g, w_bf16, *caches)
    return qkv, h, x, (kc, vc)


def _pair_mask(hh):
    lane = lax.broadcasted_iota(jnp.int32, (1, 2 * NA_HEAD_DIM), 1)
    return (lane >= hh * NA_HEAD_DIM) & (lane < (hh + 1) * NA_HEAD_DIM)


_ATT_SEQS = 2


def _stack_pair(qp):
    return jnp.concatenate([jnp.where(_pair_mask(hh), qp, jnp.zeros_like(qp)) for hh in range(2)], axis=0)


def _unstack_pair(o2):
    half = o2.shape[0] // 2
    return jnp.where(_pair_mask(0), o2[:half], o2[half:])


def _attn_prompt_kernel(q_ref, k_ref, v_ref, o_ref):
    scale = NA_HEAD_DIM ** -0.5
    for sq in range(_ATT_SEQS):
        rows = slice(sq * SEQ, (sq + 1) * SEQ)
        for p in range(NA_HEADS // 2):
            cols = slice(p * 128, (p + 1) * 128)
            q2 = _stack_pair(q_ref[rows, cols] * scale)
            s = _dot_nt(q2, k_ref[rows, cols])
            e = jnp.exp(s - jnp.max(s, axis=-1, keepdims=True))
            den = jnp.sum(e, axis=-1, keepdims=True)
            o_ref[rows, cols] = _unstack_pair(_dot(e.astype(BF16), v_ref[rows, cols]) / den).astype(o_ref.dtype)


def _attn_prompt(qkv):
    rows = _ATT_SEQS * SEQ
    return pl.pallas_call(
        _attn_prompt_kernel,
        out_shape=jax.ShapeDtypeStruct((T_PROMPT, NA_WIDTH), BF16),
        grid=(BATCH // _ATT_SEQS,),
        in_specs=[pl.BlockSpec((rows, NA_WIDTH), lambda b: (b, 0)),
                  pl.BlockSpec((rows, NA_WIDTH), lambda b: (b, 1)),
                  pl.BlockSpec((rows, NA_WIDTH), lambda b: (b, 2))],
        out_specs=pl.BlockSpec((rows, NA_WIDTH), lambda b: (b, 0)),
        compiler_params=_params(), name="attn_prompt",
    )(qkv, qkv, qkv)


_NA_ROWS = DEC_SEQ // GRID_W
_NA_LOC = NA_KH * GRID_W
_NA_STEP_ROWS = 2


def _na_window_start(r):
    return jnp.clip(r - NA_KH // 2, 0, _NA_ROWS - NA_KH)


def _attn_sample_kernel(q_ref, k_ref, v_ref, ck_ref, cv_ref, *rest):
    bias_refs, o_ref = rest[:_NA_STEP_ROWS], rest[_NA_STEP_ROWS]
    scale = NA_HEAD_DIM ** -0.5
    for p in range(NA_HEADS // 2):
        cols = slice(p * 128, (p + 1) * 128)
        kc = ck_ref[0, :, cols].astype(BF16)
        vc = cv_ref[0, :, cols].astype(BF16)
        for u in range(_NA_STEP_ROWS):
            rows = slice(u * GRID_W, (u + 1) * GRID_W)
            s0 = pl.multiple_of(_na_window_start(pl.program_id(1) * _NA_STEP_ROWS + u) * GRID_W, GRID_W)
            q2 = _stack_pair(q_ref[rows, cols] * scale)
            bias2 = jnp.concatenate([bias_refs[u][0, 2 * p], bias_refs[u][0, 2 * p + 1]], axis=0)
            sl = _dot_nt(q2, k_ref[pl.ds(s0, _NA_LOC), cols]) + bias2
            sc = _dot_nt(q2, kc)
            mx = jnp.maximum(jnp.max(sl, axis=-1, keepdims=True), jnp.max(sc, axis=-1, keepdims=True))
            el = jnp.exp(sl - mx)
            ec = jnp.exp(sc - mx)
            den = jnp.sum(el, axis=-1, keepdims=True) + jnp.sum(ec, axis=-1, keepdims=True)
            o2 = (_dot(el.astype(BF16), v_ref[pl.ds(s0, _NA_LOC), cols]) + _dot(ec.astype(BF16), vc)) / den
            o_ref[rows, cols] = _unstack_pair(o2).astype(o_ref.dtype)


def _attn_sample(qkv, ck, cv, bias):
    q_rows = _NA_STEP_ROWS * GRID_W
    steps = _NA_ROWS // _NA_STEP_ROWS
    q_blk0 = T_PROMPT // q_rows
    kv_row0 = T_PROMPT // DEC_SEQ

    def bias_spec(u):
        def index(b, r2):
            r = r2 * _NA_STEP_ROWS + u
            return (_na_window_start(r) - r + NA_KH - 1, 0, 0, 0)
        return pl.BlockSpec((1, NA_HEADS, GRID_W, _NA_LOC), index)

    return pl.pallas_call(
        _attn_sample_kernel,
        out_shape=jax.ShapeDtypeStruct((T_SAMPLE, NA_WIDTH), BF16),
        grid=(DEC_BATCH, steps),
        in_specs=[pl.BlockSpec((q_rows, NA_WIDTH), lambda b, r2: (q_blk0 + b * steps + r2, 0)),
                  pl.BlockSpec((DEC_SEQ, NA_WIDTH), lambda b, r2: (kv_row0 + b, 1)),
                  pl.BlockSpec((DEC_SEQ, NA_WIDTH), lambda b, r2: (kv_row0 + b, 2)),
                  pl.BlockSpec((1, PAST_LEN, NA_WIDTH), lambda b, r2: (b, 0, 0)),
                  pl.BlockSpec((1, PAST_LEN, NA_WIDTH), lambda b, r2: (b, 0, 0))]
                 + [bias_spec(u) for u in range(_NA_STEP_ROWS)],
        out_specs=pl.BlockSpec((q_rows, NA_WIDTH), lambda b, r2: (b * steps + r2, 0)),
        compiler_params=_params(2), name="attn_sample",
    )(qkv, qkv, qkv, ck, cv, *([bias] * _NA_STEP_ROWS))


_NA_DR = 2 * NA_KH - 1
_NA_DC = 2 * NA_KW - 1


def _na_bias_kernel(rb_ref, o_ref):
    i = pl.program_id(0)
    qc = lax.broadcasted_iota(jnp.int32, (GRID_W, GRID_W), 0)
    kc = lax.broadcasted_iota(jnp.int32, (GRID_W, GRID_W), 1)
    q_start = jnp.clip(qc - NA_KW // 2, 0, GRID_W - NA_KW)
    in_win = (kc >= q_start) & (kc < q_start + NA_KW)
    dc = jnp.clip(kc - qc + NA_KW - 1, 0, _NA_DC - 1)
    picks = [dc == d for d in range(_NA_DC)]
    tiles = []
    for dr in range(_NA_DR):
        acc = jnp.zeros((GRID_W, GRID_W), F32)
        for d in range(_NA_DC):
            acc = jnp.where(picks[d], rb_ref[i, dr * _NA_DC + d], acc)
        tiles.append(jnp.where(in_win, acc, NEG_INF))
    for base in range(NA_KH):
        o_ref[0, base, 0] = jnp.concatenate(tiles[base:base + NA_KH], axis=1)


def _na_bias_tables(rel_bias):
    rb = rel_bias.astype(F32).reshape(DEPTH * NA_HEADS, _NA_DR * _NA_DC)
    return pl.pallas_call(
        _na_bias_kernel,
        out_shape=jax.ShapeDtypeStruct((DEPTH, NA_KH, NA_HEADS, GRID_W, _NA_LOC), F32),
        grid=(DEPTH * NA_HEADS,),
        in_specs=[pl.BlockSpec(memory_space=pltpu.SMEM)],
        out_specs=pl.BlockSpec((1, NA_KH, 1, GRID_W, _NA_LOC),
                               lambda i: (i // NA_HEADS, 0, i % NA_HEADS, 0, 0)),
        compiler_params=_params(), name="na_bias_tables",
    )(rb)


_HG_GROUP = 8


def _hgrn_kernel(*refs, n_tok, has_state):
    if has_state:
        (q_ref, zf_ref, zb_ref, v_ref, g_ref, lbf_ref, lbb_ref, og_ref, s0f_ref, s0b_ref,
         rec_ref, sf_ref, sb_ref, kf_s, bf_s, kb_s, bb_s, of_s, ob_s, zf_s, zb_s, qsf_s, qsb_s, stf_s, stb_s) = refs
    else:
        (q_ref, zf_ref, zb_ref, v_ref, g_ref, lbf_ref, lbb_ref, og_ref,
         rec_ref, sf_ref, sb_ref, kf_s, bf_s, kb_s, bb_s, of_s, ob_s, zf_s, zb_s, qsf_s, qsb_s, stf_s, stb_s) = refs
        s0f_ref = s0b_ref = None
    C = HG_CHUNK
    W = HG_WIDTH
    n_chunks = n_tok // C
    rr = lax.broadcasted_iota(jnp.int32, (W, W), 0)
    cc = lax.broadcasted_iota(jnp.int32, (W, W), 1)
    log2_c = C.bit_length() - 1
    same_chunk = jnp.right_shift(rr, log2_c) == jnp.right_shift(cc, log2_c)
    tri_prefix = jnp.where(same_chunk & (cc <= rr), 1.0, 0.0).astype(BF16)
    tri_suffix = jnp.where(same_chunk & (cc >= rr), 1.0, 0.0).astype(BF16)
    same_head = jnp.right_shift(rr, 6) == jnp.right_shift(cc, 6)
    head_ones = jnp.where(same_head, 1.0, 0.0).astype(BF16)

    for ti in range(n_tok // W):
        rows = slice(ti * W, (ti + 1) * W)
        for z_ref, lb_ref, k_s, b_s, tri in ((zf_ref, lbf_ref, kf_s, bf_s, tri_prefix),
                                             (zb_ref, lbb_ref, kb_s, bb_s, tri_suffix)):
            z = z_ref[rows, :]
            lb = lb_ref[...]
            f = lb + (1.0 - lb) * jax.nn.sigmoid(z)
            logf = jnp.log(jnp.maximum(f, F_FLOOR))
            k_s[rows, :] = (1.0 - lb) * jax.nn.sigmoid(-z)
            hi, mid, lo = _split3(logf)
            b_s[rows, :] = _dot(tri, hi) + _dot(tri, mid) + _dot(tri, lo)

    G = _HG_GROUP
    n_groups = C // G
    srow = lax.broadcasted_iota(jnp.int32, (G, W), 0)
    zf_s[...] = jnp.zeros_like(zf_s)
    zb_s[...] = jnp.zeros_like(zb_s)

    def scan_chunk(ci, k_s, b_s, z_s, qs_s, st_s, o_dir_s, fwd):
        c = ci if fwd else n_chunks - 1 - ci
        base = pl.multiple_of(c * C, C)
        q = q_ref[pl.ds(base, C), :]
        k = k_s[pl.ds(base, C), :]
        b = b_s[pl.ds(base, C), :]
        v = v_ref[pl.ds(base, C), :]
        k_far = {}
        for gt in range(n_groups):
            others = range(gt) if fwd else range(gt + 1, n_groups)
            if not others:
                continue
            rows_t = slice(gt * G, (gt + 1) * G)
            edge = gt * G - 1 if fwd else (gt + 1) * G
            b_edge = b[edge:edge + 1, :]
            qs_s[rows_t, :] = q[rows_t] * jnp.exp(b[rows_t] - b_edge)
            for gs in others:
                rows_s = slice(gs * G, (gs + 1) * G)
                k_far[gt, gs] = k[rows_s] * jnp.exp(b_edge - b[rows_s])
        for t in range(C):
            gt = t // G
            rows_t = slice(gt * G, (gt + 1) * G)
            qt = q_ref[pl.ds(base + t, 1), :]
            bt = b_s[pl.ds(base + t, 1), :]
            keep = (srow + gt * G <= t) if fwd else (srow + gt * G >= t)
            z_s[t * C + gt * G:t * C + (gt + 1) * G, :] = jnp.where(
                keep, (qt * k[rows_t]) * jnp.exp(bt - b[rows_t]), 0.0)
            others = range(gt) if fwd else range(gt + 1, n_groups)
            if others:
                qst = qs_s[t:t + 1, :]
                for gs in others:
                    z_s[t * C + gs * G:t * C + (gs + 1) * G, :] = qst * k_far[gt, gs]
        a_rep = _dot(z_s[...].astype(BF16), head_ones)
        o_intra = jnp.sum(a_rep.reshape(C, C, W) * v[None, :, :], axis=1)
        b_end = b_s[pl.ds(base + (C - 1 if fwd else 0), 1), :]
        q_in = q * jnp.exp(b)
        k_st = k * jnp.exp(b_end - b)
        st = st_s[...]
        o_inter = _dot_nt(q_in.astype(BF16), st.astype(BF16))
        upd = _dot_tn(v.astype(BF16), k_st.astype(BF16))
        st_s[...] = st * jnp.exp(b_end) + jnp.where(same_head, upd, 0.0)
        o_dir_s[pl.ds(base, C), :] = o_intra + o_inter

    def load_state(s0_ref, st_s):
        if s0_ref is None:
            st_s[...] = jnp.zeros((W, W), F32)
            return
        for hh in range(HG_HEADS):
            parts = [s0_ref[0, hh] if g == hh else jnp.zeros((HG_DK, HG_DV), F32) for g in range(HG_HEADS)]
            st_s[hh * HG_DK:(hh + 1) * HG_DK, :] = jnp.concatenate(parts, axis=1)
        st_s[...] = st_s[...].T

    def store_state(st_s, out_ref):
        by_head = st_s[...].T
        for hh in range(HG_HEADS):
            out_ref[0, hh] = by_head[hh * HG_DK:(hh + 1) * HG_DK, hh * HG_DV:(hh + 1) * HG_DV]

    load_state(s0f_ref, stf_s)
    load_state(s0b_ref, stb_s)

    def scan_both(ci, carry):
        scan_chunk(ci, kf_s, bf_s, zf_s, qsf_s, stf_s, of_s, True)
        scan_chunk(ci, kb_s, bb_s, zb_s, qsb_s, stb_s, ob_s, False)
        return carry
    lax.fori_loop(0, n_chunks, scan_both, 0)
    store_state(stf_s, sf_ref)
    store_state(stb_s, sb_ref)

    for ti in range(n_tok // W):
        rows = slice(ti * W, (ti + 1) * W)
        o = of_s[rows, :] + ob_s[rows, :]
        sq = o * o
        sq_hi = sq.astype(BF16)
        sq_lo = (sq - sq_hi.astype(F32)).astype(BF16)
        ms = (_dot(sq_hi, head_ones) + _dot(sq_lo, head_ones)) * (1.0 / HG_DV)
        g = g_ref[rows, :]
        y = o * lax.rsqrt(ms + RMS_EPS) * og_ref[...] * (g * jax.nn.sigmoid(g))
        rec_ref[rows, :] = y.astype(rec_ref.dtype)


def _hgrn(h, lbf, lbb, og, s0f, s0b, n_tok, n_seq, row0):
    W = HG_WIDTH
    has_state = s0f is not None

    def col(cb):
        return pl.BlockSpec((n_tok, W), lambda i, cb=cb: (row0 + i, cb))

    vec = pl.BlockSpec((1, W), lambda i: (0, 0))
    st_spec = pl.BlockSpec((1, HG_HEADS, HG_DK, HG_DV), lambda i: (i, 0, 0, 0))
    in_specs = [col(_CB_HQ), col(_CB_ZF), col(_CB_ZB), col(_CB_HI), col(_CB_HG), vec, vec, vec]
    args = [h, h, h, h, h, lbf, lbb, og]
    if has_state:
        in_specs += [st_spec, st_spec]
        args += [s0f, s0b]
    seq_f32 = pltpu.VMEM((n_tok, W), F32)
    return pl.pallas_call(
        functools.partial(_hgrn_kernel, n_tok=n_tok, has_state=has_state),
        out_shape=(jax.ShapeDtypeStruct((n_seq * n_tok, W), BF16),
                   jax.ShapeDtypeStruct((n_seq, HG_HEADS, HG_DK, HG_DV), F32),
                   jax.ShapeDtypeStruct((n_seq, HG_HEADS, HG_DK, HG_DV), F32)),
        grid=(n_seq,),
        in_specs=in_specs,
        out_specs=(pl.BlockSpec((n_tok, W), lambda i: (i, 0)), st_spec, st_spec),
        scratch_shapes=[seq_f32, seq_f32, seq_f32, seq_f32, seq_f32, seq_f32,
                        pltpu.VMEM((HG_CHUNK * HG_CHUNK, W), F32),
                        pltpu.VMEM((HG_CHUNK * HG_CHUNK, W), F32),
                        pltpu.VMEM((HG_CHUNK, W), F32),
                        pltpu.VMEM((HG_CHUNK, W), F32),
                        pltpu.VMEM((W, W), F32),
                        pltpu.VMEM((W, W), F32)],
        compiler_params=_params(), name="hgrn_state" if has_state else "hgrn_zero",
    )(*args)


def _gmlp_kernel(u_ref, v_ref, g_ref, ws_ref, b_ref, o_ref):
    lane = lax.broadcasted_iota(jnp.int32, (1, GM_WIDTH), 1)
    for ci in range(TM // GM_CHUNK):
        rows = slice(ci * GM_CHUNK, (ci + 1) * GM_CHUNK)
        v = v_ref[rows, :]
        ms = jnp.mean(v * v, axis=-1, keepdims=True)
        vn = (v * lax.rsqrt(ms + RMS_EPS) * g_ref[...]).astype(BF16)
        z = b_ref[...]
        for gi in range(GM_GROUPS):
            zg = _dot(ws_ref[gi], vn)
            in_group = (lane >= gi * GM_GDIM) & (lane < (gi + 1) * GM_GDIM)
            z = z + jnp.where(in_group, zg, 0.0)
        o_ref[rows, :] = (u_ref[rows, :] * z).astype(o_ref.dtype)


def _gmlp(h, vnorm_g, ws_bf16, bias_full):
    W = GM_WIDTH
    return pl.pallas_call(
        _gmlp_kernel,
        out_shape=jax.ShapeDtypeStruct((T_ALL, W), BF16),
        grid=(N_TILES,),
        in_specs=[pl.BlockSpec((TM, W), lambda i: (i, _CB_GU)),
                  pl.BlockSpec((TM, W), lambda i: (i, _CB_GV)),
                  pl.BlockSpec((1, W), lambda i: (0, 0)),
                  pl.BlockSpec((GM_GROUPS, GM_CHUNK, GM_CHUNK), lambda i: (0, 0, 0)),
                  pl.BlockSpec((GM_CHUNK, W), lambda i: (0, 0))],
        out_specs=pl.BlockSpec((TM, W), lambda i: (i, 0)),
        compiler_params=_params(), name="gmlp",
    )(h, h, vnorm_g, ws_bf16, bias_full)


def _outproj_kernel(attp_ref, atts_ref, recp_ref, recs_ref, mlp_ref, x_ref, mod_ref, g_ref, w_ref, wr_ref, br_ref,
                    x1_ref, h2a_ref, h2b_ref, rt_ref, gate_ref, cnt_ref):
    @pl.when(pl.program_id(0) == 0)
    def _():
        cnt_ref[...] = jnp.zeros_like(cnt_ref)

    out = (_dot(_pick_group(attp_ref, atts_ref), w_ref[0, 0:NA_WIDTH, :])
           + _dot(_pick_group(recp_ref, recs_ref), w_ref[0, NA_WIDTH:NA_WIDTH + HG_WIDTH, :])
           + _dot(mlp_ref[...], w_ref[0, NA_WIDTH + HG_WIDTH:, :]))
    x1 = x_ref[...] + mod_ref[0, 2:3, :] * out
    x1_ref[...] = x1
    h2 = _rms_mod(x1, g_ref[...], mod_ref[0, 3:4, :], mod_ref[0, 4:5, :])
    _store_slabs((h2a_ref, h2b_ref), _pack_halves(h2))
    h_hi = h2.astype(BF16)
    h_lo = (h2 - h_hi.astype(F32)).astype(BF16)
    wr = wr_ref[...]
    w_hi = wr.astype(BF16)
    w_lo = (wr - w_hi.astype(F32)).astype(BF16)
    logits = _dot(h_hi, w_hi) + _dot(h_lo, w_hi) + _dot(h_hi, w_lo) + br_ref[...]
    lane_e = lax.broadcasted_iota(jnp.int32, (TM, N_EXPERTS), 1).astype(F32)
    lane_o = lax.broadcasted_iota(jnp.int32, (TM, _RT_LANES), 1)
    idx_acc = jnp.zeros((TM, _RT_LANES), F32)
    val_acc = jnp.zeros((TM, _RT_LANES), F32)
    top0 = None
    den = jnp.zeros((TM, 1), F32)
    work = logits
    picks = []
    for kk in range(TOP_K):
        m = jnp.max(work, axis=-1, keepdims=True)
        first = jnp.min(jnp.where(work == m, lane_e, float(N_EXPERTS)), axis=-1, keepdims=True)
        if kk == 0:
            top0 = m
        e = jnp.exp(m - top0)
        den = den + e
        idx_acc = jnp.where(lane_o == kk, first, idx_acc)
        val_acc = jnp.where(lane_o == kk, e, val_acc)
        picks.append(lane_e == first)
        work = jnp.where(picks[-1], -jnp.inf, work)
    gate_ref[...] = val_acc / den
    sel = jnp.zeros((TM, N_EXPERTS), F32)
    for pk in picks:
        sel = sel + jnp.where(pk, 1.0, 0.0)
    rr = lax.broadcasted_iota(jnp.int32, (TM, TM), 0)
    cc = lax.broadcasted_iota(jnp.int32, (TM, TM), 1)
    earlier = jnp.where(cc < rr, 1.0, 0.0).astype(BF16)
    seen = cnt_ref[0:1, 0:N_EXPERTS]
    before = _dot(earlier, sel.astype(BF16)) + seen
    for kk, pk in enumerate(picks):
        rank = jnp.sum(jnp.where(pk, before, 0.0), axis=-1, keepdims=True)
        idx_acc = jnp.where(lane_o == TOP_K + kk, rank, idx_acc)
    rt_ref[...] = idx_acc.T[0:_RT_ROWS, :].astype(jnp.int32)
    cnt_ref[0:1, 0:N_EXPERTS] = seen + jnp.sum(sel, axis=0, keepdims=True)


def _outproj(layer, att_p, att_s, rec_p, rec_s, mlp, x, mod, g, w_bf16, wr, br):
    def tile(width):
        return pl.BlockSpec((TM, width), lambda i: (i, 0))

    return pl.pallas_call(
        _outproj_kernel,
        out_shape=(jax.ShapeDtypeStruct((T_ALL, D_MODEL), F32),
                   jax.ShapeDtypeStruct((T_ALL, D_SLAB), jnp.int32),
                   jax.ShapeDtypeStruct((T_ALL, D_SLAB), jnp.int32),
                   jax.ShapeDtypeStruct((_RT_ROWS, T_ALL), jnp.int32),
                   jax.ShapeDtypeStruct((T_ALL, _RT_LANES), F32),
                   jax.ShapeDtypeStruct((8, _RT_LANES), F32)),
        grid=(N_TILES,),
        in_specs=[_p_tile(NA_WIDTH), _s_tile(NA_WIDTH), _p_tile(HG_WIDTH), _s_tile(HG_WIDTH),
                  tile(GM_WIDTH), _TILE_SPEC, _MOD_SPEC, _ROW_SPEC,
                  pl.BlockSpec((1, D_MODEL, D_MODEL), lambda i: (layer, 0, 0)),
                  pl.BlockSpec((D_MODEL, N_EXPERTS), lambda i: (0, 0)),
                  pl.BlockSpec((1, N_EXPERTS), lambda i: (0, 0))],
        out_specs=(_TILE_SPEC, tile(D_SLAB), tile(D_SLAB), pl.BlockSpec((_RT_ROWS, TM), lambda i: (0, i)),
                   tile(_RT_LANES),
                   pl.BlockSpec((8, _RT_LANES), lambda i: (0, 0))),
        compiler_params=_params(), name="outproj_router",
    )(att_p, att_s, rec_p, rec_s, mlp, x, mod, g, w_bf16, wr, br)


_W_CHUNKS = 4
_W_CAST_ROWS = 128


def _moe_kernel(blk_e_ref, blk_on_ref, blk_new_ref, blk_next_ref,
                xa_ref, xb_ref, wg_hbm, bg_ref, wu_hbm, bu_ref, wd_hbm, bd_ref,
                ya_ref, yb_ref, w_f32, w_bf16, w_sem, *, layer):
    j = pl.program_id(0)

    def weight_copies(expert):
        rows = D_MODEL // _W_CHUNKS
        return [pltpu.make_async_copy(w_hbm.at[layer, expert, pl.ds(ci * rows, rows)],
                                      w_f32.at[wi, pl.ds(ci * rows, rows)], w_sem.at[wi, ci])
                for wi, w_hbm in enumerate((wg_hbm, wu_hbm, wd_hbm)) for ci in range(_W_CHUNKS)]

    @pl.when(j == 0)
    def _():
        for cp in weight_copies(blk_e_ref[0]):
            cp.start()

    @pl.when(blk_new_ref[j] != 0)
    def _():
        for cp in weight_copies(blk_e_ref[j]):
            cp.wait()

        def cast_rows(ci, carry):
            rows = pl.ds(pl.multiple_of(ci * _W_CAST_ROWS, _W_CAST_ROWS), _W_CAST_ROWS)
            for wi in range(3):
                w_bf16[wi, rows, :] = w_f32[wi, rows, :].astype(BF16)
            return carry
        lax.fori_loop(0, D_MODEL // _W_CAST_ROWS, cast_rows, 0)

        @pl.when(blk_next_ref[j] >= 0)
        def _():
            for cp in weight_copies(blk_next_ref[j]):
                cp.start()

    @pl.when(blk_on_ref[j] != 0)
    def _():
        lo, hi = _unpack_halves(_load_slabs((xa_ref, xb_ref)))
        x = jnp.concatenate([lo.astype(BF16), hi.astype(BF16)], axis=1)
        gate = jnp.minimum(_dot(x, w_bf16[0]) + bg_ref[0, 0], SWIGLU_LIMIT)
        up = jnp.clip(_dot(x, w_bf16[1]) + bu_ref[0, 0], -SWIGLU_LIMIT, SWIGLU_LIMIT)
        glu = gate * jax.nn.sigmoid(SWIGLU_ALPHA * gate)
        act = ((up + 1.0) * glu).astype(BF16)
        _store_slabs((ya_ref, yb_ref), _pack_halves(_dot(act, w_bf16[2]) + bd_ref[0, 0]))

    @pl.when(blk_on_ref[j] == 0)
    def _():
        ya_ref[...] = jnp.zeros_like(ya_ref)
        yb_ref[...] = jnp.zeros_like(yb_ref)


def _moe(layer, plan, x_sorted, wg, bg, wu, bu, wd, bd):
    n_plan = len(plan)
    b_spec = pl.BlockSpec((1, 1, 1, D_MODEL), lambda j, be, *_: (layer, be[j], 0, 0))
    x_spec = pl.BlockSpec((MOE_BM, D_SLAB), lambda j, *_: (j, 0))
    hbm = pl.BlockSpec(memory_space=pl.ANY)
    bias4 = lambda b: b.reshape(DEPTH, N_EXPERTS, 1, D_MODEL)
    return pl.pallas_call(
        functools.partial(_moe_kernel, layer=layer),
        out_shape=(jax.ShapeDtypeStruct((MOE_SLOTS, D_SLAB), jnp.int32),) * N_SPLIT,
        grid_spec=pltpu.PrefetchScalarGridSpec(
            num_scalar_prefetch=n_plan, grid=(MOE_BLOCKS,),
            in_specs=[x_spec, x_spec, hbm, b_spec, hbm, b_spec, hbm, b_spec],
            out_specs=(x_spec, x_spec),
            scratch_shapes=[pltpu.VMEM((3, D_MODEL, D_MODEL), F32), pltpu.VMEM((3, D_MODEL, D_MODEL), BF16),
                            pltpu.SemaphoreType.DMA((3, _W_CHUNKS))]),
        compiler_params=_params(), name="moe_experts",
    )(*plan, *x_sorted, wg, bias4(bg), wu, bias4(bu), wd, bias4(bd))


def _route(rt, counts):
    experts = jnp.arange(N_EXPERTS, dtype=jnp.int32)
    nblk = (counts + MOE_BM - 1) // MOE_BM
    blk_end = jnp.cumsum(nblk)
    row0 = (blk_end - nblk) * MOE_BM
    top_i, rank = rt[:TOP_K], rt[TOP_K:]
    start_of = jnp.sum(jnp.where(top_i[None] == experts[:, None, None], row0[:, None, None], 0), axis=0)
    dest = (start_of + rank).reshape(1, TOP_K * T_ALL)
    live = counts > 0
    last_live = jnp.max(jnp.where(live, experts, 0))
    later_live = live[None, :] & (experts[None, :] > experts[:, None])
    next_live = jnp.min(jnp.where(later_live, experts[None, :], N_EXPERTS), axis=1)
    next_live = jnp.where(next_live == N_EXPERTS, -1, next_live)
    blk = jnp.arange(MOE_BLOCKS, dtype=jnp.int32)
    blk_on = blk < blk_end[-1]
    blk_e = jnp.where(blk_on, jnp.minimum(jnp.sum((blk_end[None, :] <= blk[:, None]).astype(jnp.int32), axis=1),
                                          N_EXPERTS - 1), last_live)
    blk_new = blk_on & jnp.concatenate([jnp.ones((1,), bool), blk_e[1:] != blk_e[:-1]])
    is_e = blk_e[:, None] == experts[None, :]
    lookup = lambda table: jnp.sum(jnp.where(is_e, table[None, :], 0), axis=1)
    plan = (blk_e, blk_on, blk_new, lookup(next_live))
    return dest.astype(jnp.int32), tuple(p.astype(jnp.int32) for p in plan)


_SC_WINDOW = 128


def _sc_mesh():
    return plsc.VectorSubcoreMesh(core_axis_name="core", subcore_axis_name="subcore")


def _sc_scatter_rows(srcs, idx, n_out):
    n_src, width = srcs[0].shape
    n_idx = idx.shape[1]
    src_windows = n_src // _SC_WINDOW

    def body(*refs):
        x_hbm = refs[:len(srcs)]
        i_hbm = refs[len(srcs)]
        o_hbm = refs[len(srcs) + 1:]
        for xs, os_ in zip(x_hbm, o_hbm):
            def step(x_vmem, i_vmem, os_=os_):
                pltpu.sync_copy(x_vmem, os_.at[i_vmem.at[0]])

            pltpu.emit_pipeline(
                step, grid=(n_idx // _SC_WINDOW,),
                in_specs=[pl.BlockSpec((_SC_WINDOW, width), lambda i: (i % src_windows, 0)),
                          pl.BlockSpec((1, _SC_WINDOW), lambda i: (0, i))],
                out_specs=[],
                core_axis_name=("core", "subcore"),
                dimension_semantics=(pltpu.PARALLEL,),
            )(xs, i_hbm)

    out_type = tuple(jax.ShapeDtypeStruct((n_out, width), s.dtype) for s in srcs)
    return pl.kernel(body, out_type=out_type, mesh=_sc_mesh(), scratch_types=[],
                     name="sc_scatter_rows")(*srcs, idx)


def _sc_gather_rows(tables, idx):
    n_idx = idx.shape[1]
    width = tables[0].shape[1]

    def body(*refs):
        t_hbm = refs[:len(tables)]
        i_hbm = refs[len(tables)]
        o_hbm = refs[len(tables) + 1:]
        for ts, os_ in zip(t_hbm, o_hbm):
            def step(i_vmem, o_vmem, ts=ts):
                pltpu.sync_copy(ts.at[i_vmem.at[0]], o_vmem)

            pltpu.emit_pipeline(
                step, grid=(n_idx // _SC_WINDOW,),
                in_specs=[pl.BlockSpec((1, _SC_WINDOW), lambda i: (0, i))],
                out_specs=[pl.BlockSpec((_SC_WINDOW, width), lambda i: (i, 0))],
                core_axis_name=("core", "subcore"),
                dimension_semantics=(pltpu.PARALLEL,),
            )(i_hbm, os_)

    out_type = tuple(jax.ShapeDtypeStruct((n_idx, width), t.dtype) for t in tables)
    return pl.kernel(body, out_type=out_type, mesh=_sc_mesh(), scratch_types=[],
                     name="sc_gather_rows")(*tables, idx)


def _final_kernel(x_ref, yga_ref, ygb_ref, gate_ref, mod_ref, g_ref, yp_ref, ys_ref):
    x = x_ref[...] + mod_ref[0, 5:6, :] * _combine_experts((yga_ref, ygb_ref), gate_ref)
    ms = jnp.mean(x * x, axis=-1, keepdims=True)
    y = x * lax.rsqrt(ms + RMS_EPS) * g_ref[...]

    @pl.when(pl.program_id(0) < P_TILES)
    def _():
        yp_ref[...] = y

    @pl.when(pl.program_id(0) >= P_TILES)
    def _():
        ys_ref[...] = y


def _final(x, moe, mod, g):
    return pl.pallas_call(
        _final_kernel,
        out_shape=(jax.ShapeDtypeStruct((T_PROMPT, D_MODEL), F32), jax.ShapeDtypeStruct((T_SAMPLE, D_MODEL), F32)),
        grid=(N_TILES,),
        in_specs=[_TILE_SPEC, _YG_SPEC, _YG_SPEC, _GATE_SPEC, _MOD_SPEC, _ROW_SPEC],
        out_specs=(_p_tile(D_MODEL), _s_tile(D_MODEL)),
        compiler_params=_params(), name="final_norm",
    )(x, *moe[0], moe[1], mod, g)


def kernel(x_prompt, x_sample, cache_k, cache_v, state_hgrn_fwd, state_hgrn_bwd, c, c_ctx, w_mod, b_mod, norm1_g, norm2_g, w_in, na_rel_bias, hgrn_lb, hgrn_onorm_g, gmlp_vnorm_g, gmlp_ws, gmlp_b, w_out, router_w, router_b, w_gate, b_gate, w_up, b_up, w_down, b_down, final_g):
    x = (x_prompt.reshape(T_PROMPT, D_MODEL), x_sample.reshape(T_SAMPLE, D_MODEL))

    cond = jnp.zeros((MOD_ROWS, D_MODEL), F32).at[0].set(c_ctx).at[1:1 + DEC_BATCH].set(c)
    mod = _modulation(cond, w_mod, b_mod)
    tile_row = np.concatenate([np.zeros(P_TILES, np.int32),
                               1 + np.arange(N_TILES - P_TILES, dtype=np.int32) // (DEC_SEQ // TM)])
    mod_tiles = mod[:, tile_row].reshape(DEPTH, N_TILES, 6, D_MODEL)
    mod_tiles = jnp.pad(mod_tiles, ((0, 0), (0, 0), (0, MOD_ROWS - 6), (0, 0)))

    lb_soft = jax.nn.softmax(hgrn_lb.astype(F32), axis=1)
    lower = jnp.cumsum(lb_soft, axis=1) - lb_soft[:, :1]

    na_bias = _na_bias_tables(na_rel_bias)
    w_in_bf16 = w_in.astype(BF16)
    w_out_bf16 = w_out.astype(BF16)

    sf_list, sb_list = [], []
    moe_out = caches = None
    for l in range(DEPTH):
        qkv, h, x, caches = _inproj(l, x, moe_out, mod_tiles[l - 1] if l else None, mod_tiles[l],
                                    norm1_g[l][None, :], w_in_bf16, caches)

        att_p = _attn_prompt(qkv)
        att_s = _attn_sample(qkv, cache_k[:, l].reshape(DEC_BATCH, PAST_LEN, NA_WIDTH),
                             cache_v[:, l].reshape(DEC_BATCH, PAST_LEN, NA_WIDTH), na_bias[l])
        lbf = lower[0, l][None, :]
        lbb = lower[1, l][None, :]
        og = jnp.tile(hgrn_onorm_g[l], HG_HEADS)[None, :]
        rec_p, sf, sb = _hgrn(h, lbf, lbb, og, None, None, SEQ, BATCH, 0)
        rec_s, _, _ = _hgrn(h, lbf, lbb, og, state_hgrn_fwd[:, l].astype(F32), state_hgrn_bwd[:, l].astype(F32),
                            DEC_SEQ, DEC_BATCH, T_PROMPT // DEC_SEQ)
        sf_list.append(sf)
        sb_list.append(sb)
        gm_bias = jnp.repeat(gmlp_b[l].T, GM_GDIM, axis=1)
        mlp = _gmlp(h, gmlp_vnorm_g[l][None, :], gmlp_ws[l].astype(BF16), gm_bias)

        x, h2a, h2b, rt, gate_pad, cnt = _outproj(l, att_p, att_s, rec_p, rec_s, mlp, x, mod_tiles[l],
                                                  norm2_g[l][None, :], w_out_bf16,
                                                  router_w[l], router_b[l][None, :])
        dest_flat, plan = _route(rt, cnt[0, :N_EXPERTS].astype(jnp.int32))
        x_sorted = _sc_scatter_rows((h2a, h2b), dest_flat, MOE_SLOTS)
        y_sorted = _moe(l, plan, x_sorted, w_gate, b_gate, w_up, b_up, w_down, b_down)
        y_tok = _sc_gather_rows(y_sorted, dest_flat)
        moe_out = ([yt.reshape(TOP_K, T_ALL, D_SLAB) for yt in y_tok], gate_pad)

    y_prompt, y_sample = _final(x, moe_out, mod_tiles[DEPTH - 1], final_g[None, :])
    y_prompt = y_prompt.reshape(BATCH, SEQ, D_MODEL)
    y_sample = y_sample.reshape(DEC_BATCH, DEC_SEQ, D_MODEL)
    new_k, new_v = (cache.reshape(BATCH, DEPTH, NA_HEADS, NA_HEAD_DIM, SEQ).transpose(0, 1, 4, 2, 3)
                    for cache in caches)
    return (y_prompt, y_sample, new_k, new_v, jnp.stack(sf_list, axis=1), jnp.stack(sb_list, axis=1))
```

```python
import functools

import numpy as np
import jax
import jax.numpy as jnp
from jax import lax
from jax.experimental import pallas as pl
from jax.experimental.pallas import tpu as pltpu
from jax.experimental.pallas import tpu_sc as plsc

F32 = jnp.float32
BF16 = jnp.bfloat16

D_MODEL = 1024
BATCH = 32
SEQ = 256
DEPTH = 2
DEC_BATCH = 2
DEC_SEQ = 1024
PAST_LEN = 512
GRID_W = 64
NA_HEADS = 8
NA_HEAD_DIM = 64
NA_WIDTH = NA_HEADS * NA_HEAD_DIM
NA_KH = 8
NA_KW = 16
HG_HEADS = 4
HG_DK = 64
HG_DV = 64
HG_WIDTH = HG_HEADS * HG_DV
HG_CHUNK = 16
F_FLOOR = 1e-30
GM_GROUPS = 4
GM_GDIM = 64
GM_WIDTH = GM_GROUPS * GM_GDIM
GM_CHUNK = 128
IN_COLS = 3 * NA_WIDTH + 5 * HG_WIDTH + 2 * GM_WIDTH
N_EXPERTS = 32
TOP_K = 4
SWIGLU_LIMIT = 7.0
SWIGLU_ALPHA = 1.702
RMS_EPS = 1e-6
NEG_INF = -1e30

T_PROMPT = BATCH * SEQ
T_SAMPLE = DEC_BATCH * DEC_SEQ
T_ALL = T_PROMPT + T_SAMPLE
TM = 512
SEQ_PER_TILE = TM // SEQ
N_TILES = T_ALL // TM
P_TILES = T_PROMPT // TM
MOE_BM = 256
MOE_SLOTS = -(-(T_ALL * TOP_K + N_EXPERTS * (MOE_BM - 1)) // MOE_BM) * MOE_BM
MOE_BLOCKS = MOE_SLOTS // MOE_BM
MOD_ROWS = 8
V7X_VMEM_LIMIT = 48 * 1024 * 1024

QKV_COLS = 3 * NA_WIDTH
REST_COLS = IN_COLS - QKV_COLS
_CB_HQ, _CB_ZF, _CB_ZB, _CB_HI, _CB_HG, _CB_GU, _CB_GV = range(7)


def _dot(a, b):
    return jnp.dot(a, b, preferred_element_type=F32)


def _dot_nt(a, b):
    return lax.dot_general(a, b, (((1,), (1,)), ((), ())), preferred_element_type=F32)


def _dot_tn(a, b):
    return lax.dot_general(a, b, (((0,), (0,)), ((), ())), preferred_element_type=F32)


def _split3(x):
    hi = x.astype(BF16)
    r1 = x - hi.astype(F32)
    mid = r1.astype(BF16)
    lo = (r1 - mid.astype(F32)).astype(BF16)
    return hi, mid, lo


D_PACK = D_MODEL // 2
N_SPLIT = 2
D_SLAB = D_PACK // N_SPLIT


def _pack_halves(x):
    half = x.shape[1] // 2
    lo = pltpu.bitcast(x[:, :half].astype(BF16).astype(F32), jnp.uint32)
    hi = pltpu.bitcast(x[:, half:].astype(BF16).astype(F32), jnp.uint32)
    return pltpu.bitcast(jnp.right_shift(lo, jnp.uint32(16)) | hi, jnp.int32)


def _unpack_halves(w):
    u = pltpu.bitcast(w, jnp.uint32)
    lo = pltpu.bitcast(jnp.left_shift(u, jnp.uint32(16)), F32)
    hi = pltpu.bitcast(u & jnp.uint32(0xFFFF0000), F32)
    return lo, hi


def _load_slabs(refs, *lead):
    return jnp.concatenate([r[lead] if lead else r[...] for r in refs], axis=1)


def _store_slabs(refs, packed):
    for si, r in enumerate(refs):
        r[...] = packed[:, si * D_SLAB:(si + 1) * D_SLAB]


def _params(n_axes=1):
    return pltpu.CompilerParams(dimension_semantics=("arbitrary",) * n_axes,
                                vmem_limit_bytes=V7X_VMEM_LIMIT)


def _mod_kernel(cond_ref, w_ref, b_ref, o_ref):
    c = cond_ref[...]
    c = c * jax.nn.sigmoid(c)
    w = w_ref[0]
    c_hi = c.astype(BF16)
    c_lo = (c - c_hi.astype(F32)).astype(BF16)
    w_hi = w.astype(BF16)
    w_lo = (w - w_hi.astype(F32)).astype(BF16)
    o_ref[0] = _dot(c_hi, w_hi) + _dot(c_lo, w_hi) + _dot(c_hi, w_lo) + b_ref[0]


def _modulation(cond, w_mod, b_mod):
    tn = 1536
    return pl.pallas_call(
        _mod_kernel,
        out_shape=jax.ShapeDtypeStruct((DEPTH, MOD_ROWS, 6 * D_MODEL), F32),
        grid=(DEPTH, 6 * D_MODEL // tn),
        in_specs=[pl.BlockSpec((MOD_ROWS, D_MODEL), lambda l, j: (0, 0)),
                  pl.BlockSpec((1, D_MODEL, tn), lambda l, j: (l, 0, j)),
                  pl.BlockSpec((1, 1, tn), lambda l, j: (l, 0, j))],
        out_specs=pl.BlockSpec((1, MOD_ROWS, tn), lambda l, j: (l, 0, j)),
        compiler_params=_params(2),
        name="modulation",
    )(cond, w_mod, b_mod.reshape(DEPTH, 1, 6 * D_MODEL))


def _rms_mod(x, g, shift, scale):
    ms = jnp.mean(x * x, axis=-1, keepdims=True)
    y = x * lax.rsqrt(ms + RMS_EPS) * g
    return y * (1.0 + scale) + shift


def _project_in(hm, w_ref, qkv_ref, h_ref, kc_ref, vc_ref):
    h = _dot(hm.astype(BF16), w_ref[0])
    qkv_ref[...] = h[:, :QKV_COLS].astype(BF16)
    h_ref[...] = h[:, QKV_COLS:]

    @pl.when(pl.program_id(0) < P_TILES)
    def _():
        for sq in range(SEQ_PER_TILE):
            rows = slice(sq * SEQ, (sq + 1) * SEQ)
            kc_ref[sq, 0:NA_WIDTH] = h[rows, NA_WIDTH:2 * NA_WIDTH].T
            vc_ref[sq, 0:NA_WIDTH] = h[rows, 2 * NA_WIDTH:3 * NA_WIDTH].T
            if kc_ref.shape[1] > NA_WIDTH:
                kc_ref[sq, NA_WIDTH:] = jnp.zeros((kc_ref.shape[1] - NA_WIDTH, SEQ), F32)
                vc_ref[sq, NA_WIDTH:] = jnp.zeros((vc_ref.shape[1] - NA_WIDTH, SEQ), F32)


def _pick_group(p_ref, s_ref):
    return jnp.where(pl.program_id(0) < P_TILES, p_ref[...], s_ref[...])


def _p_tile(width):
    return pl.BlockSpec((TM, width), lambda i: (jnp.minimum(i, P_TILES - 1), 0))


def _s_tile(width):
    return pl.BlockSpec((TM, width), lambda i: (jnp.maximum(i - P_TILES, 0), 0))


def _inproj_first_kernel(xp_ref, xs_ref, mod_ref, g_ref, w_ref, qkv_ref, h_ref, kc_ref, vc_ref):
    x = _pick_group(xp_ref, xs_ref)
    hm = _rms_mod(x, g_ref[...], mod_ref[0, 0:1, :], mod_ref[0, 1:2, :])
    _project_in(hm, w_ref, qkv_ref, h_ref, kc_ref, vc_ref)


def _combine_experts(yg_refs, gate_ref):
    gates = gate_ref[...]
    lo_acc = hi_acc = None
    for kk in range(TOP_K):
        lo, hi = _unpack_halves(_load_slabs(yg_refs, kk))
        gk = gates[:, kk:kk + 1]
        lo_acc = gk * lo if lo_acc is None else lo_acc + gk * lo
        hi_acc = gk * hi if hi_acc is None else hi_acc + gk * hi
    return jnp.concatenate([lo_acc, hi_acc], axis=1)


def _inproj_next_kernel(x_ref, yga_ref, ygb_ref, gate_ref, pmod_ref, mod_ref, g_ref, w_ref, kc_in, vc_in,
                        qkv_ref, h_ref, xo_ref, kc_ref, vc_ref):
    del kc_in, vc_in
    x = x_ref[...] + pmod_ref[0, 5:6, :] * _combine_experts((yga_ref, ygb_ref), gate_ref)
    xo_ref[...] = x
    hm = _rms_mod(x, g_ref[...], mod_ref[0, 0:1, :], mod_ref[0, 1:2, :])
    _project_in(hm, w_ref, qkv_ref, h_ref, kc_ref, vc_ref)


_TILE_SPEC = pl.BlockSpec((TM, D_MODEL), lambda i: (i, 0))
_MOD_SPEC = pl.BlockSpec((1, MOD_ROWS, D_MODEL), lambda i: (i, 0, 0))
_ROW_SPEC = pl.BlockSpec((1, D_MODEL), lambda i: (0, 0))
_RT_LANES = 128
_RT_ROWS = 2 * TOP_K
_YG_SPEC = pl.BlockSpec((TOP_K, TM, D_SLAB), lambda i: (0, i, 0))
_GATE_SPEC = pl.BlockSpec((TM, _RT_LANES), lambda i: (i, 0))


def _inproj(layer, x, moe, prev_mod, mod, g, w_bf16, caches):
    w_spec = pl.BlockSpec((1, D_MODEL, IN_COLS), lambda i: (layer, 0, 0))
    h_spec = pl.BlockSpec((TM, REST_COLS), lambda i: (i, 0))
    h_shape = jax.ShapeDtypeStruct((T_ALL, REST_COLS), F32)
    q_spec = pl.BlockSpec((TM, QKV_COLS), lambda i: (i, 0))
    q_shape = jax.ShapeDtypeStruct((T_ALL, QKV_COLS), BF16)
    c_spec = pl.BlockSpec((SEQ_PER_TILE, NA_WIDTH, SEQ), lambda i: (jnp.minimum(i, P_TILES - 1), layer, 0))
    c_shape = jax.ShapeDtypeStruct((BATCH, DEPTH * NA_WIDTH, SEQ), F32)
    x_shape = jax.ShapeDtypeStruct((T_ALL, D_MODEL), F32)
    if moe is None:
        c_all = pl.BlockSpec((SEQ_PER_TILE, DEPTH * NA_WIDTH, SEQ), lambda i: (jnp.minimum(i, P_TILES - 1), 0, 0))
        qkv, h, kc, vc = pl.pallas_call(
            _inproj_first_kernel, out_shape=(q_shape, h_shape, c_shape, c_shape), grid=(N_TILES,),
            in_specs=[_p_tile(D_MODEL), _s_tile(D_MODEL), _MOD_SPEC, _ROW_SPEC, w_spec],
            out_specs=(q_spec, h_spec, c_all, c_all),
            compiler_params=_params(), name="inproj_first",
        )(*x, mod, g, w_bf16)
        return qkv, h, x, (kc, vc)
    qkv, h, x, kc, vc = pl.pallas_call(
        _inproj_next_kernel,
        out_shape=(q_shape, h_shape, x_shape, c_shape, c_shape),
        grid=(N_TILES,),
        in_specs=[_TILE_SPEC, _YG_SPEC, _YG_SPEC, _GATE_SPEC, _MOD_SPEC, _MOD_SPEC, _ROW_SPEC, w_spec,
                  pl.BlockSpec(memory_space=pl.ANY), pl.BlockSpec(memory_space=pl.ANY)],
        out_specs=(q_spec, h_spec, _TILE_SPEC, c_spec, c_spec),
        input_output_aliases={8: 3, 9: 4},
        compiler_params=_params(), name="inproj_next",
    )(x, *moe[0], moe[1], prev_mod, mod, g, w_bf16, *caches)
    return qkv, h, x, (kc, vc)


def _pair_mask(hh):
    lane = lax.broadcasted_iota(jnp.int32, (1, 2 * NA_HEAD_DIM), 1)
    return (lane >= hh * NA_HEAD_DIM) & (lane < (hh + 1) * NA_HEAD_DIM)


_ATT_SEQS = 2


def _stack_pair(qp):
    return jnp.concatenate([jnp.where(_pair_mask(hh), qp, jnp.zeros_like(qp)) for hh in range(2)], axis=0)


def _unstack_pair(o2):
    half = o2.shape[0] // 2
    return jnp.where(_pair_mask(0), o2[:half], o2[half:])


def _attn_prompt_kernel(q_ref, k_ref, v_ref, o_ref):
    scale = NA_HEAD_DIM ** -0.5
    for sq in range(_ATT_SEQS):
        rows = slice(sq * SEQ, (sq + 1) * SEQ)
        for p in range(NA_HEADS // 2):
            cols = slice(p * 128, (p + 1) * 128)
            q2 = _stack_pair(q_ref[rows, cols] * scale)
            s = _dot_nt(q2, k_ref[rows, cols])
            e = jnp.exp(s - jnp.max(s, axis=-1, keepdims=True))
            den = jnp.sum(e, axis=-1, keepdims=True)
            o_ref[rows, cols] = _unstack_pair(_dot(e.astype(BF16), v_ref[rows, cols]) / den).astype(o_ref.dtype)


def _attn_prompt(qkv):
    rows = _ATT_SEQS * SEQ
    return pl.pallas_call(
        _attn_prompt_kernel,
        out_shape=jax.ShapeDtypeStruct((T_PROMPT, NA_WIDTH), BF16),
        grid=(BATCH // _ATT_SEQS,),
        in_specs=[pl.BlockSpec((rows, NA_WIDTH), lambda b: (b, 0)),
                  pl.BlockSpec((rows, NA_WIDTH), lambda b: (b, 1)),
                  pl.BlockSpec((rows, NA_WIDTH), lambda b: (b, 2))],
        out_specs=pl.BlockSpec((rows, NA_WIDTH), lambda b: (b, 0)),
        compiler_params=_params(), name="attn_prompt",
    )(qkv, qkv, qkv)


_NA_ROWS = DEC_SEQ // GRID_W
_NA_LOC = NA_KH * GRID_W
_NA_STEP_ROWS = 2


def _na_window_start(r):
    return jnp.clip(r - NA_KH // 2, 0, _NA_ROWS - NA_KH)


def _attn_sample_kernel(q_ref, k_ref, v_ref, ck_ref, cv_ref, *rest):
    bias_refs, o_ref = rest[:_NA_STEP_ROWS], rest[_NA_STEP_ROWS]
    scale = NA_HEAD_DIM ** -0.5
    for p in range(NA_HEADS // 2):
        cols = slice(p * 128, (p + 1) * 128)
        kc = ck_ref[0, :, cols].astype(BF16)
        vc = cv_ref[0, :, cols].astype(BF16)
        for u in range(_NA_STEP_ROWS):
            rows = slice(u * GRID_W, (u + 1) * GRID_W)
            s0 = pl.multiple_of(_na_window_start(pl.program_id(1) * _NA_STEP_ROWS + u) * GRID_W, GRID_W)
            q2 = _stack_pair(q_ref[rows, cols] * scale)
            bias2 = jnp.concatenate([bias_refs[u][0, 2 * p], bias_refs[u][0, 2 * p + 1]], axis=0)
            sl = _dot_nt(q2, k_ref[pl.ds(s0, _NA_LOC), cols]) + bias2
            sc = _dot_nt(q2, kc)
            mx = jnp.maximum(jnp.max(sl, axis=-1, keepdims=True), jnp.max(sc, axis=-1, keepdims=True))
            el = jnp.exp(sl - mx)
            ec = jnp.exp(sc - mx)
            den = jnp.sum(el, axis=-1, keepdims=True) + jnp.sum(ec, axis=-1, keepdims=True)
            o2 = (_dot(el.astype(BF16), v_ref[pl.ds(s0, _NA_LOC), cols]) + _dot(ec.astype(BF16), vc)) / den
            o_ref[rows, cols] = _unstack_pair(o2).astype(o_ref.dtype)


def _attn_sample(qkv, ck, cv, bias):
    q_rows = _NA_STEP_ROWS * GRID_W
    steps = _NA_ROWS // _NA_STEP_ROWS
    q_blk0 = T_PROMPT // q_rows
    kv_row0 = T_PROMPT // DEC_SEQ

    def bias_spec(u):
        def index(b, r2):
            r = r2 * _NA_STEP_ROWS + u
            return (_na_window_start(r) - r + NA_KH - 1, 0, 0, 0)
        return pl.BlockSpec((1, NA_HEADS, GRID_W, _NA_LOC), index)

    return pl.pallas_call(
        _attn_sample_kernel,
        out_shape=jax.ShapeDtypeStruct((T_SAMPLE, NA_WIDTH), BF16),
        grid=(DEC_BATCH, steps),
        in_specs=[pl.BlockSpec((q_rows, NA_WIDTH), lambda b, r2: (q_blk0 + b * steps + r2, 0)),
                  pl.BlockSpec((DEC_SEQ, NA_WIDTH), lambda b, r2: (kv_row0 + b, 1)),
                  pl.BlockSpec((DEC_SEQ, NA_WIDTH), lambda b, r2: (kv_row0 + b, 2)),
                  pl.BlockSpec((1, PAST_LEN, NA_WIDTH), lambda b, r2: (b, 0, 0)),
                  pl.BlockSpec((1, PAST_LEN, NA_WIDTH), lambda b, r2: (b, 0, 0))]
                 + [bias_spec(u) for u in range(_NA_STEP_ROWS)],
        out_specs=pl.BlockSpec((q_rows, NA_WIDTH), lambda b, r2: (b * steps + r2, 0)),
        compiler_params=_params(2), name="attn_sample",
    )(qkv, qkv, qkv, ck, cv, *([bias] * _NA_STEP_ROWS))


_NA_DR = 2 * NA_KH - 1
_NA_DC = 2 * NA_KW - 1


def _na_bias_kernel(rb_ref, o_ref):
    i = pl.program_id(0)
    qc = lax.broadcasted_iota(jnp.int32, (GRID_W, GRID_W), 0)
    kc = lax.broadcasted_iota(jnp.int32, (GRID_W, GRID_W), 1)
    q_start = jnp.clip(qc - NA_KW // 2, 0, GRID_W - NA_KW)
    in_win = (kc >= q_start) & (kc < q_start + NA_KW)
    dc = jnp.clip(kc - qc + NA_KW - 1, 0, _NA_DC - 1)
    picks = [dc == d for d in range(_NA_DC)]
    tiles = []
    for dr in range(_NA_DR):
        acc = jnp.zeros((GRID_W, GRID_W), F32)
        for d in range(_NA_DC):
            acc = jnp.where(picks[d], rb_ref[i, dr * _NA_DC + d], acc)
        tiles.append(jnp.where(in_win, acc, NEG_INF))
    for base in range(NA_KH):
        o_ref[0, base, 0] = jnp.concatenate(tiles[base:base + NA_KH], axis=1)


def _na_bias_tables(rel_bias):
    rb = rel_bias.astype(F32).reshape(DEPTH * NA_HEADS, _NA_DR * _NA_DC)
    return pl.pallas_call(
        _na_bias_kernel,
        out_shape=jax.ShapeDtypeStruct((DEPTH, NA_KH, NA_HEADS, GRID_W, _NA_LOC), F32),
        grid=(DEPTH * NA_HEADS,),
        in_specs=[pl.BlockSpec(memory_space=pltpu.SMEM)],
        out_specs=pl.BlockSpec((1, NA_KH, 1, GRID_W, _NA_LOC),
                               lambda i: (i // NA_HEADS, 0, i % NA_HEADS, 0, 0)),
        compiler_params=_params(), name="na_bias_tables",
    )(rb)


_HG_GROUP = 8


def _hgrn_kernel(*refs, n_tok, has_state):
    if has_state:
        (q_ref, zf_ref, zb_ref, v_ref, g_ref, lbf_ref, lbb_ref, og_ref, s0f_ref, s0b_ref,
         rec_ref, sf_ref, sb_ref, kf_s, bf_s, kb_s, bb_s, of_s, ob_s, zf_s, zb_s, qsf_s, qsb_s, stf_s, stb_s) = refs
    else:
        (q_ref, zf_ref, zb_ref, v_ref, g_ref, lbf_ref, lbb_ref, og_ref,
         rec_ref, sf_ref, sb_ref, kf_s, bf_s, kb_s, bb_s, of_s, ob_s, zf_s, zb_s, qsf_s, qsb_s, stf_s, stb_s) = refs
        s0f_ref = s0b_ref = None
    C = HG_CHUNK
    W = HG_WIDTH
    n_chunks = n_tok // C
    rr = lax.broadcasted_iota(jnp.int32, (W, W), 0)
    cc = lax.broadcasted_iota(jnp.int32, (W, W), 1)
    log2_c = C.bit_length() - 1
    same_chunk = jnp.right_shift(rr, log2_c) == jnp.right_shift(cc, log2_c)
    tri_prefix = jnp.where(same_chunk & (cc <= rr), 1.0, 0.0).astype(BF16)
    tri_suffix = jnp.where(same_chunk & (cc >= rr), 1.0, 0.0).astype(BF16)
    same_head = jnp.right_shift(rr, 6) == jnp.right_shift(cc, 6)
    head_ones = jnp.where(same_head, 1.0, 0.0).astype(BF16)

    for ti in range(n_tok // W):
        rows = slice(ti * W, (ti + 1) * W)
        for z_ref, lb_ref, k_s, b_s, tri in ((zf_ref, lbf_ref, kf_s, bf_s, tri_prefix),
                                             (zb_ref, lbb_ref, kb_s, bb_s, tri_suffix)):
            z = z_ref[rows, :]
            lb = lb_ref[...]
            e = jnp.exp(-jnp.abs(z))
            big = 1.0 / (1.0 + e)
            small = e * big
            f = lb + (1.0 - lb) * jnp.where(z >= 0.0, big, small)
            logf = jnp.log(jnp.maximum(f, F_FLOOR))
            k_s[rows, :] = (1.0 - lb) * jnp.where(z >= 0.0, small, big)
            hi, mid, lo = _split3(logf)
            b_s[rows, :] = _dot(tri, hi) + _dot(tri, mid) + _dot(tri, lo)

    G = _HG_GROUP
    n_groups = C // G
    srow = lax.broadcasted_iota(jnp.int32, (G, W), 0)
    zf_s[...] = jnp.zeros_like(zf_s)
    zb_s[...] = jnp.zeros_like(zb_s)

    def scan_chunk(ci, k_s, b_s, z_s, qs_s, st_s, o_dir_s, fwd):
        c = ci if fwd else n_chunks - 1 - ci
        base = pl.multiple_of(c * C, C)
        q = q_ref[pl.ds(base, C), :]
        k = k_s[pl.ds(base, C), :]
        b = b_s[pl.ds(base, C), :]
        v = v_ref[pl.ds(base, C), :]
        k_far = {}
        for gt in range(n_groups):
            others = range(gt) if fwd else range(gt + 1, n_groups)
            if not others:
                continue
            rows_t = slice(gt * G, (gt + 1) * G)
            edge = gt * G - 1 if fwd else (gt + 1) * G
            b_edge = b[edge:edge + 1, :]
            qs_s[rows_t, :] = q[rows_t] * jnp.exp(b[rows_t] - b_edge)
            for gs in others:
                rows_s = slice(gs * G, (gs + 1) * G)
                k_far[gt, gs] = k[rows_s] * jnp.exp(b_edge - b[rows_s])
        for t in range(C):
            gt = t // G
            rows_t = slice(gt * G, (gt + 1) * G)
            qt = q_ref[pl.ds(base + t, 1), :]
            bt = b_s[pl.ds(base + t, 1), :]
            keep = (srow + gt * G <= t) if fwd else (srow + gt * G >= t)
            z_s[t * C + gt * G:t * C + (gt + 1) * G, :] = jnp.where(
                keep, (qt * k[rows_t]) * jnp.exp(bt - b[rows_t]), 0.0)
            others = range(gt) if fwd else range(gt + 1, n_groups)
            if others:
                qst = qs_s[t:t + 1, :]
                for gs in others:
                    z_s[t * C + gs * G:t * C + (gs + 1) * G, :] = qst * k_far[gt, gs]
        a_rep = _dot(z_s[...].astype(BF16), head_ones)
        o_intra = jnp.sum(a_rep.reshape(C, C, W) * v[None, :, :], axis=1)
        b_end = b_s[pl.ds(base + (C - 1 if fwd else 0), 1), :]
        q_in = q * jnp.exp(b)
        k_st = k * jnp.exp(b_end - b)
        st = st_s[...]
        o_inter = _dot_nt(q_in.astype(BF16), st.astype(BF16))
        upd = _dot_tn(v.astype(BF16), k_st.astype(BF16))
        st_s[...] = st * jnp.exp(b_end) + jnp.where(same_head, upd, 0.0)
        o_dir_s[pl.ds(base, C), :] = o_intra + o_inter

    def load_state(s0_ref, st_s):
        if s0_ref is None:
            st_s[...] = jnp.zeros((W, W), F32)
            return
        for hh in range(HG_HEADS):
            parts = [s0_ref[0, hh] if g == hh else jnp.zeros((HG_DK, HG_DV), F32) for g in range(HG_HEADS)]
            st_s[hh * HG_DK:(hh + 1) * HG_DK, :] = jnp.concatenate(parts, axis=1)
        st_s[...] = st_s[...].T

    def store_state(st_s, out_ref):
        by_head = st_s[...].T
        for hh in range(HG_HEADS):
            out_ref[0, hh] = by_head[hh * HG_DK:(hh + 1) * HG_DK, hh * HG_DV:(hh + 1) * HG_DV]

    load_state(s0f_ref, stf_s)
    load_state(s0b_ref, stb_s)

    def scan_both(ci, carry):
        scan_chunk(ci, kf_s, bf_s, zf_s, qsf_s, stf_s, of_s, True)
        scan_chunk(ci, kb_s, bb_s, zb_s, qsb_s, stb_s, ob_s, False)
        return carry
    lax.fori_loop(0, n_chunks, scan_both, 0)
    store_state(stf_s, sf_ref)
    store_state(stb_s, sb_ref)

    for ti in range(n_tok // W):
        rows = slice(ti * W, (ti + 1) * W)
        o = of_s[rows, :] + ob_s[rows, :]
        sq = o * o
        sq_hi = sq.astype(BF16)
        sq_lo = (sq - sq_hi.astype(F32)).astype(BF16)
        ms = (_dot(sq_hi, head_ones) + _dot(sq_lo, head_ones)) * (1.0 / HG_DV)
        g = g_ref[rows, :]
        y = o * lax.rsqrt(ms + RMS_EPS) * og_ref[...] * (g * jax.nn.sigmoid(g))
        rec_ref[rows, :] = y.astype(rec_ref.dtype)


def _hgrn(h, lbf, lbb, og, s0f, s0b, n_tok, n_seq, row0):
    W = HG_WIDTH
    has_state = s0f is not None

    def col(cb):
        return pl.BlockSpec((n_tok, W), lambda i, cb=cb: (row0 + i, cb))

    vec = pl.BlockSpec((1, W), lambda i: (0, 0))
    st_spec = pl.BlockSpec((1, HG_HEADS, HG_DK, HG_DV), lambda i: (i, 0, 0, 0))
    in_specs = [col(_CB_HQ), col(_CB_ZF), col(_CB_ZB), col(_CB_HI), col(_CB_HG), vec, vec, vec]
    args = [h, h, h, h, h, lbf, lbb, og]
    if has_state:
        in_specs += [st_spec, st_spec]
        args += [s0f, s0b]
    seq_f32 = pltpu.VMEM((n_tok, W), F32)
    return pl.pallas_call(
        functools.partial(_hgrn_kernel, n_tok=n_tok, has_state=has_state),
        out_shape=(jax.ShapeDtypeStruct((n_seq * n_tok, W), BF16),
                   jax.ShapeDtypeStruct((n_seq, HG_HEADS, HG_DK, HG_DV), F32),
                   jax.ShapeDtypeStruct((n_seq, HG_HEADS, HG_DK, HG_DV), F32)),
        grid=(n_seq,),
        in_specs=in_specs,
        out_specs=(pl.BlockSpec((n_tok, W), lambda i: (i, 0)), st_spec, st_spec),
        scratch_shapes=[seq_f32, seq_f32, seq_f32, seq_f32, seq_f32, seq_f32,
                        pltpu.VMEM((HG_CHUNK * HG_CHUNK, W), F32),
                        pltpu.VMEM((HG_CHUNK * HG_CHUNK, W), F32),
                        pltpu.VMEM((HG_CHUNK, W), F32),
                        pltpu.VMEM((HG_CHUNK, W), F32),
                        pltpu.VMEM((W, W), F32),
                        pltpu.VMEM((W, W), F32)],
        compiler_params=_params(), name="hgrn_state" if has_state else "hgrn_zero",
    )(*args)


def _gmlp_kernel(u_ref, v_ref, g_ref, ws_ref, b_ref, o_ref):
    lane = lax.broadcasted_iota(jnp.int32, (1, GM_WIDTH), 1)
    for ci in range(TM // GM_CHUNK):
        rows = slice(ci * GM_CHUNK, (ci + 1) * GM_CHUNK)
        v = v_ref[rows, :]
        ms = jnp.mean(v * v, axis=-1, keepdims=True)
        vn = (v * lax.rsqrt(ms + RMS_EPS) * g_ref[...]).astype(BF16)
        z = b_ref[...]
        for gi in range(GM_GROUPS):
            zg = _dot(ws_ref[gi], vn)
            in_group = (lane >= gi * GM_GDIM) & (lane < (gi + 1) * GM_GDIM)
            z = z + jnp.where(in_group, zg, 0.0)
        o_ref[rows, :] = (u_ref[rows, :] * z).astype(o_ref.dtype)


def _gmlp(h, vnorm_g, ws_bf16, bias_full):
    W = GM_WIDTH
    return pl.pallas_call(
        _gmlp_kernel,
        out_shape=jax.ShapeDtypeStruct((T_ALL, W), BF16),
        grid=(N_TILES,),
        in_specs=[pl.BlockSpec((TM, W), lambda i: (i, _CB_GU)),
                  pl.BlockSpec((TM, W), lambda i: (i, _CB_GV)),
                  pl.BlockSpec((1, W), lambda i: (0, 0)),
                  pl.BlockSpec((GM_GROUPS, GM_CHUNK, GM_CHUNK), lambda i: (0, 0, 0)),
                  pl.BlockSpec((GM_CHUNK, W), lambda i: (0, 0))],
        out_specs=pl.BlockSpec((TM, W), lambda i: (i, 0)),
        compiler_params=_params(), name="gmlp",
    )(h, h, vnorm_g, ws_bf16, bias_full)


def _outproj_kernel(attp_ref, atts_ref, recp_ref, recs_ref, mlp_ref, xp_ref, xs_ref, mod_ref, g_ref, w_ref, wr_ref, br_ref,
                    x1_ref, h2a_ref, h2b_ref, rt_ref, gate_ref, cnt_ref):
    @pl.when(pl.program_id(0) == 0)
    def _():
        cnt_ref[...] = jnp.zeros_like(cnt_ref)

    out = (_dot(_pick_group(attp_ref, atts_ref), w_ref[0, 0:NA_WIDTH, :])
           + _dot(_pick_group(recp_ref, recs_ref), w_ref[0, NA_WIDTH:NA_WIDTH + HG_WIDTH, :])
           + _dot(mlp_ref[...], w_ref[0, NA_WIDTH + HG_WIDTH:, :]))
    x1 = _pick_group(xp_ref, xs_ref) + mod_ref[0, 2:3, :] * out
    x1_ref[...] = x1
    h2 = _rms_mod(x1, g_ref[...], mod_ref[0, 3:4, :], mod_ref[0, 4:5, :])
    _store_slabs((h2a_ref, h2b_ref), _pack_halves(h2))
    h_hi = h2.astype(BF16)
    h_lo = (h2 - h_hi.astype(F32)).astype(BF16)
    wr = wr_ref[...]
    w_hi = wr.astype(BF16)
    w_lo = (wr - w_hi.astype(F32)).astype(BF16)
    logits = _dot(h_hi, w_hi) + _dot(h_lo, w_hi) + _dot(h_hi, w_lo) + br_ref[...]
    lane_e = lax.broadcasted_iota(jnp.int32, (TM, N_EXPERTS), 1).astype(F32)
    lane_o = lax.broadcasted_iota(jnp.int32, (TM, _RT_LANES), 1)
    idx_acc = jnp.zeros((TM, _RT_LANES), F32)
    val_acc = jnp.zeros((TM, _RT_LANES), F32)
    top0 = None
    den = jnp.zeros((TM, 1), F32)
    work = logits
    picks = []
    for kk in range(TOP_K):
        m = jnp.max(work, axis=-1, keepdims=True)
        first = jnp.min(jnp.where(work == m, lane_e, float(N_EXPERTS)), axis=-1, keepdims=True)
        if kk == 0:
            top0 = m
        e = jnp.exp(m - top0)
        den = den + e
        idx_acc = jnp.where(lane_o == kk, first, idx_acc)
        val_acc = jnp.where(lane_o == kk, e, val_acc)
        picks.append(lane_e == first)
        work = jnp.where(picks[-1], -jnp.inf, work)
    gate_ref[...] = val_acc / den
    sel = jnp.zeros((TM, N_EXPERTS), F32)
    for pk in picks:
        sel = sel + jnp.where(pk, 1.0, 0.0)
    rr = lax.broadcasted_iota(jnp.int32, (TM, TM), 0)
    cc = lax.broadcasted_iota(jnp.int32, (TM, TM), 1)
    earlier = jnp.where(cc < rr, 1.0, 0.0).astype(BF16)
    seen = cnt_ref[0:1, 0:N_EXPERTS]
    before = _dot(earlier, sel.astype(BF16)) + seen
    for kk, pk in enumerate(picks):
        rank = jnp.sum(jnp.where(pk, before, 0.0), axis=-1, keepdims=True)
        idx_acc = jnp.where(lane_o == TOP_K + kk, rank, idx_acc)
    rt_ref[...] = idx_acc.T[0:_RT_ROWS, :].astype(jnp.int32)
    cnt_ref[0:1, 0:N_EXPERTS] = seen + jnp.sum(sel, axis=0, keepdims=True)


def _outproj(layer, att_p, att_s, rec_p, rec_s, mlp, x, mod, g, w_bf16, wr, br):
    def tile(width):
        return pl.BlockSpec((TM, width), lambda i: (i, 0))

    if isinstance(x, tuple):
        x_args, x_specs = x, [_p_tile(D_MODEL), _s_tile(D_MODEL)]
    else:
        x_args = (x, x)
        x_specs = [_p_tile(D_MODEL), pl.BlockSpec((TM, D_MODEL), lambda i: (jnp.maximum(i, P_TILES), 0))]
    return pl.pallas_call(
        _outproj_kernel,
        out_shape=(jax.ShapeDtypeStruct((T_ALL, D_MODEL), F32),
                   jax.ShapeDtypeStruct((T_ALL, D_SLAB), jnp.int32),
                   jax.ShapeDtypeStruct((T_ALL, D_SLAB), jnp.int32),
                   jax.ShapeDtypeStruct((_RT_ROWS, T_ALL), jnp.int32),
                   jax.ShapeDtypeStruct((T_ALL, _RT_LANES), F32),
                   jax.ShapeDtypeStruct((8, _RT_LANES), F32)),
        grid=(N_TILES,),
        in_specs=[_p_tile(NA_WIDTH), _s_tile(NA_WIDTH), _p_tile(HG_WIDTH), _s_tile(HG_WIDTH),
                  tile(GM_WIDTH), *x_specs, _MOD_SPEC, _ROW_SPEC,
                  pl.BlockSpec((1, D_MODEL, D_MODEL), lambda i: (layer, 0, 0)),
                  pl.BlockSpec((D_MODEL, N_EXPERTS), lambda i: (0, 0)),
                  pl.BlockSpec((1, N_EXPERTS), lambda i: (0, 0))],
        out_specs=(_TILE_SPEC, tile(D_SLAB), tile(D_SLAB), pl.BlockSpec((_RT_ROWS, TM), lambda i: (0, i)),
                   tile(_RT_LANES),
                   pl.BlockSpec((8, _RT_LANES), lambda i: (0, 0))),
        compiler_params=_params(), name="outproj_router",
    )(att_p, att_s, rec_p, rec_s, mlp, *x_args, mod, g, w_bf16, wr, br)


_W_CHUNKS = 4
_W_CAST_ROWS = 128


def _moe_kernel(blk_e_ref, blk_on_ref, blk_new_ref, blk_next_ref,
                xa_ref, xb_ref, wg_hbm, bg_ref, wu_hbm, bu_ref, wd_hbm, bd_ref,
                ya_ref, yb_ref, w_f32, w_bf16, w_sem, *, layer):
    j = pl.program_id(0)

    def weight_copies(expert):
        rows = D_MODEL // _W_CHUNKS
        return [pltpu.make_async_copy(w_hbm.at[layer, expert, pl.ds(ci * rows, rows)],
                                      w_f32.at[wi, pl.ds(ci * rows, rows)], w_sem.at[wi, ci])
                for wi, w_hbm in enumerate((wg_hbm, wu_hbm, wd_hbm)) for ci in range(_W_CHUNKS)]

    @pl.when(j == 0)
    def _():
        for cp in weight_copies(blk_e_ref[0]):
            cp.start()

    @pl.when(blk_new_ref[j] != 0)
    def _():
        for cp in weight_copies(blk_e_ref[j]):
            cp.wait()

        def cast_rows(ci, carry):
            rows = pl.ds(pl.multiple_of(ci * _W_CAST_ROWS, _W_CAST_ROWS), _W_CAST_ROWS)
            for wi in range(3):
                w_bf16[wi, rows, :] = w_f32[wi, rows, :].astype(BF16)
            return carry
        lax.fori_loop(0, D_MODEL // _W_CAST_ROWS, cast_rows, 0)

        @pl.when(blk_next_ref[j] >= 0)
        def _():
            for cp in weight_copies(blk_next_ref[j]):
                cp.start()

    @pl.when(blk_on_ref[j] != 0)
    def _():
        lo, hi = _unpack_halves(_load_slabs((xa_ref, xb_ref)))
        x = jnp.concatenate([lo.astype(BF16), hi.astype(BF16)], axis=1)
        gate = jnp.minimum(_dot(x, w_bf16[0]) + bg_ref[0, 0], SWIGLU_LIMIT)
        up = jnp.clip(_dot(x, w_bf16[1]) + bu_ref[0, 0], -SWIGLU_LIMIT, SWIGLU_LIMIT)
        glu = gate * jax.nn.sigmoid(SWIGLU_ALPHA * gate)
        act = ((up + 1.0) * glu).astype(BF16)
        _store_slabs((ya_ref, yb_ref), _pack_halves(_dot(act, w_bf16[2]) + bd_ref[0, 0]))

    @pl.when(blk_on_ref[j] == 0)
    def _():
        ya_ref[...] = jnp.zeros_like(ya_ref)
        yb_ref[...] = jnp.zeros_like(yb_ref)


def _moe(layer, plan, x_sorted, wg, bg, wu, bu, wd, bd):
    n_plan = len(plan)
    b_spec = pl.BlockSpec((1, 1, 1, D_MODEL), lambda j, be, *_: (layer, be[j], 0, 0))
    x_spec = pl.BlockSpec((MOE_BM, D_SLAB), lambda j, *_: (j, 0))
    hbm = pl.BlockSpec(memory_space=pl.ANY)
    bias4 = lambda b: b.reshape(DEPTH, N_EXPERTS, 1, D_MODEL)
    return pl.pallas_call(
        functools.partial(_moe_kernel, layer=layer),
        out_shape=(jax.ShapeDtypeStruct((MOE_SLOTS, D_SLAB), jnp.int32),) * N_SPLIT,
        grid_spec=pltpu.PrefetchScalarGridSpec(
            num_scalar_prefetch=n_plan, grid=(MOE_BLOCKS,),
            in_specs=[x_spec, x_spec, hbm, b_spec, hbm, b_spec, hbm, b_spec],
            out_specs=(x_spec, x_spec),
            scratch_shapes=[pltpu.VMEM((3, D_MODEL, D_MODEL), F32), pltpu.VMEM((3, D_MODEL, D_MODEL), BF16),
                            pltpu.SemaphoreType.DMA((3, _W_CHUNKS))]),
        compiler_params=_params(), name="moe_experts",
    )(*plan, *x_sorted, wg, bias4(bg), wu, bias4(bu), wd, bias4(bd))


def _route(rt, counts):
    experts = jnp.arange(N_EXPERTS, dtype=jnp.int32)
    nblk = (counts + MOE_BM - 1) // MOE_BM
    blk_end = jnp.cumsum(nblk)
    row0 = (blk_end - nblk) * MOE_BM
    top_i, rank = rt[:TOP_K], rt[TOP_K:]
    start_of = jnp.sum(jnp.where(top_i[None] == experts[:, None, None], row0[:, None, None], 0), axis=0)
    dest = (start_of + rank).reshape(1, TOP_K * T_ALL)
    live = counts > 0
    last_live = jnp.max(jnp.where(live, experts, 0))
    later_live = live[None, :] & (experts[None, :] > experts[:, None])
    next_live = jnp.min(jnp.where(later_live, experts[None, :], N_EXPERTS), axis=1)
    next_live = jnp.where(next_live == N_EXPERTS, -1, next_live)
    blk = jnp.arange(MOE_BLOCKS, dtype=jnp.int32)
    blk_on = blk < blk_end[-1]
    blk_e = jnp.where(blk_on, jnp.minimum(jnp.sum((blk_end[None, :] <= blk[:, None]).astype(jnp.int32), axis=1),
                                          N_EXPERTS - 1), last_live)
    blk_new = blk_on & jnp.concatenate([jnp.ones((1,), bool), blk_e[1:] != blk_e[:-1]])
    is_e = blk_e[:, None] == experts[None, :]
    lookup = lambda table: jnp.sum(jnp.where(is_e, table[None, :], 0), axis=1)
    plan = (blk_e, blk_on, blk_new, lookup(next_live))
    return dest.astype(jnp.int32), tuple(p.astype(jnp.int32) for p in plan)


_SC_WINDOW = 128


def _sc_mesh():
    return plsc.VectorSubcoreMesh(core_axis_name="core", subcore_axis_name="subcore")


def _sc_scatter_rows(srcs, idx, n_out):
    n_src, width = srcs[0].shape
    n_rep = idx.shape[1] // n_src
    src_windows = n_src // _SC_WINDOW
    assert len(srcs) == 2

    def body(*refs):
        x_hbm = refs[:len(srcs)]
        i_hbm = refs[len(srcs)]
        o_hbm = refs[len(srcs) + 1:]

        def run(xs, os_):
            def step(x_vmem, *i_vmem):
                for iv in i_vmem:
                    pltpu.sync_copy(x_vmem, os_.at[iv.at[0]])

            pltpu.emit_pipeline(
                step, grid=(src_windows,),
                in_specs=[pl.BlockSpec((_SC_WINDOW, width), lambda i: (i, 0))]
                         + [pl.BlockSpec((1, _SC_WINDOW), lambda i, kk=kk: (0, kk * src_windows + i))
                            for kk in range(n_rep)],
                out_specs=[],
                core_axis_name="subcore",
                dimension_semantics=(pltpu.PARALLEL,),
            )(xs, *([i_hbm] * n_rep))

        for ci, (xs, os_) in enumerate(zip(x_hbm, o_hbm)):
            pl.when(lax.axis_index("core") == ci)(functools.partial(run, xs, os_))

    out_type = tuple(jax.ShapeDtypeStruct((n_out, width), s.dtype) for s in srcs)
    return pl.kernel(body, out_type=out_type, mesh=_sc_mesh(), scratch_types=[],
                     name="sc_scatter_rows")(*srcs, idx)


def _sc_gather_rows(tables, idx):
    n_idx = idx.shape[1]
    width = tables[0].shape[1]

    def body(*refs):
        t_hbm = refs[:len(tables)]
        i_hbm = refs[len(tables)]
        o_hbm = refs[len(tables) + 1:]
        for ts, os_ in zip(t_hbm, o_hbm):
            def step(i_vmem, o_vmem, ts=ts):
                pltpu.sync_copy(ts.at[i_vmem.at[0]], o_vmem)

            pltpu.emit_pipeline(
                step, grid=(n_idx // _SC_WINDOW,),
                in_specs=[pl.BlockSpec((1, _SC_WINDOW), lambda i: (0, i))],
                out_specs=[pl.BlockSpec((_SC_WINDOW, width), lambda i: (i, 0))],
                core_axis_name=("core", "subcore"),
                dimension_semantics=(pltpu.PARALLEL,),
            )(i_hbm, os_)

    out_type = tuple(jax.ShapeDtypeStruct((n_idx, width), t.dtype) for t in tables)
    return pl.kernel(body, out_type=out_type, mesh=_sc_mesh(), scratch_types=[],
                     name="sc_gather_rows")(*tables, idx)


def _final_kernel(x_ref, yga_ref, ygb_ref, gate_ref, mod_ref, g_ref, yp_ref, ys_ref):
    x = x_ref[...] + mod_ref[0, 5:6, :] * _combine_experts((yga_ref, ygb_ref), gate_ref)
    ms = jnp.mean(x * x, axis=-1, keepdims=True)
    y = x * lax.rsqrt(ms + RMS_EPS) * g_ref[...]

    @pl.when(pl.program_id(0) < P_TILES)
    def _():
        yp_ref[...] = y

    @pl.when(pl.program_id(0) >= P_TILES)
    def _():
        ys_ref[...] = y


def _final(x, moe, mod, g):
    return pl.pallas_call(
        _final_kernel,
        out_shape=(jax.ShapeDtypeStruct((T_PROMPT, D_MODEL), F32), jax.ShapeDtypeStruct((T_SAMPLE, D_MODEL), F32)),
        grid=(N_TILES,),
        in_specs=[_TILE_SPEC, _YG_SPEC, _YG_SPEC, _GATE_SPEC, _MOD_SPEC, _ROW_SPEC],
        out_specs=(_p_tile(D_MODEL), _s_tile(D_MODEL)),
        compiler_params=_params(), name="final_norm",
    )(x, *moe[0], moe[1], mod, g)


def kernel(x_prompt, x_sample, cache_k, cache_v, state_hgrn_fwd, state_hgrn_bwd, c, c_ctx, w_mod, b_mod, norm1_g, norm2_g, w_in, na_rel_bias, hgrn_lb, hgrn_onorm_g, gmlp_vnorm_g, gmlp_ws, gmlp_b, w_out, router_w, router_b, w_gate, b_gate, w_up, b_up, w_down, b_down, final_g):
    x = (x_prompt.reshape(T_PROMPT, D_MODEL), x_sample.reshape(T_SAMPLE, D_MODEL))

    cond = jnp.zeros((MOD_ROWS, D_MODEL), F32).at[0].set(c_ctx).at[1:1 + DEC_BATCH].set(c)
    mod = _modulation(cond, w_mod, b_mod)
    tile_row = np.concatenate([np.zeros(P_TILES, np.int32),
                               1 + np.arange(N_TILES - P_TILES, dtype=np.int32) // (DEC_SEQ // TM)])
    mod_tiles = mod[:, tile_row].reshape(DEPTH, N_TILES, 6, D_MODEL)
    mod_tiles = jnp.pad(mod_tiles, ((0, 0), (0, 0), (0, MOD_ROWS - 6), (0, 0)))

    lb_soft = jax.nn.softmax(hgrn_lb.astype(F32), axis=1)
    lower = jnp.cumsum(lb_soft, axis=1) - lb_soft[:, :1]

    na_bias = _na_bias_tables(na_rel_bias)
    w_in_bf16 = w_in.astype(BF16)
    w_out_bf16 = w_out.astype(BF16)

    sf_list, sb_list = [], []
    moe_out = caches = None
    for l in range(DEPTH):
        qkv, h, x_next, caches = _inproj(l, x, moe_out, mod_tiles[l - 1] if l else None, mod_tiles[l],
                                    norm1_g[l][None, :], w_in_bf16, caches)

        att_p = _attn_prompt(qkv)
        att_s = _attn_sample(qkv, cache_k[:, l].reshape(DEC_BATCH, PAST_LEN, NA_WIDTH),
                             cache_v[:, l].reshape(DEC_BATCH, PAST_LEN, NA_WIDTH), na_bias[l])
        lbf = lower[0, l][None, :]
        lbb = lower[1, l][None, :]
        og = jnp.tile(hgrn_onorm_g[l], HG_HEADS)[None, :]
        rec_p, sf, sb = _hgrn(h, lbf, lbb, og, None, None, SEQ, BATCH, 0)
        rec_s, _, _ = _hgrn(h, lbf, lbb, og, state_hgrn_fwd[:, l].astype(F32), state_hgrn_bwd[:, l].astype(F32),
                            DEC_SEQ, DEC_BATCH, T_PROMPT // DEC_SEQ)
        sf_list.append(sf)
        sb_list.append(sb)
        gm_bias = jnp.repeat(gmlp_b[l].T, GM_GDIM, axis=1)
        mlp = _gmlp(h, gmlp_vnorm_g[l][None, :], gmlp_ws[l].astype(BF16), gm_bias)

        x, h2a, h2b, rt, gate_pad, cnt = _outproj(l, att_p, att_s, rec_p, rec_s, mlp, x_next, mod_tiles[l],
                                                  norm2_g[l][None, :], w_out_bf16,
                                                  router_w[l], router_b[l][None, :])
        dest_flat, plan = _route(rt, cnt[0, :N_EXPERTS].astype(jnp.int32))
        x_sorted = _sc_scatter_rows((h2a, h2b), dest_flat, MOE_SLOTS)
        y_sorted = _moe(l, plan, x_sorted, w_gate, b_gate, w_up, b_up, w_down, b_down)
        y_tok = _sc_gather_rows(y_sorted, dest_flat)
        moe_out = ([yt.reshape(TOP_K, T_ALL, D_SLAB) for yt in y_tok], gate_pad)

    y_prompt, y_sample = _final(x, moe_out, mod_tiles[DEPTH - 1], final_g[None, :])
    y_prompt = y_prompt.reshape(BATCH, SEQ, D_MODEL)
    y_sample = y_sample.reshape(DEC_BATCH, DEC_SEQ, D_MODEL)
    new_k, new_v = (cache.reshape(BATCH, DEPTH, NA_HEADS, NA_HEAD_DIM, SEQ).transpose(0, 1, 4, 2, 3)
                    for cache in caches)
    return (y_prompt, y_sample, new_k, new_v, jnp.stack(sf_list, axis=1), jnp.stack(sb_list, axis=1))
```

```python
import functools

import numpy as np
import jax
import jax.numpy as jnp
from jax import lax
from jax.experimental import pallas as pl
from jax.experimental.pallas import tpu as pltpu
from jax.experimental.pallas import tpu_sc as plsc

F32 = jnp.float32
BF16 = jnp.bfloat16

D_MODEL = 1024
BATCH = 32
SEQ = 256
DEPTH = 2
DEC_BATCH = 2
DEC_SEQ = 1024
PAST_LEN = 512
GRID_W = 64
NA_HEADS = 8
NA_HEAD_DIM = 64
NA_WIDTH = NA_HEADS * NA_HEAD_DIM
NA_KH = 8
NA_KW = 16
HG_HEADS = 4
HG_DK = 64
HG_DV = 64
HG_WIDTH = HG_HEADS * HG_DV
HG_CHUNK = 16
F_FLOOR = 1e-30
GM_GROUPS = 4
GM_GDIM = 64
GM_WIDTH = GM_GROUPS * GM_GDIM
GM_CHUNK = 128
IN_COLS = 3 * NA_WIDTH + 5 * HG_WIDTH + 2 * GM_WIDTH
N_EXPERTS = 32
TOP_K = 4
SWIGLU_LIMIT = 7.0
SWIGLU_ALPHA = 1.702
RMS_EPS = 1e-6
NEG_INF = -1e30

T_PROMPT = BATCH * SEQ
T_SAMPLE = DEC_BATCH * DEC_SEQ
T_ALL = T_PROMPT + T_SAMPLE
TM = 512
SEQ_PER_TILE = TM // SEQ
N_TILES = T_ALL // TM
P_TILES = T_PROMPT // TM
MOE_BM = 256
MOE_SLOTS = -(-(T_ALL * TOP_K + N_EXPERTS * (MOE_BM - 1)) // MOE_BM) * MOE_BM
MOE_BLOCKS = MOE_SLOTS // MOE_BM
MOD_ROWS = 8
V7X_VMEM_LIMIT = 48 * 1024 * 1024

QKV_COLS = 3 * NA_WIDTH
REST_COLS = IN_COLS - QKV_COLS
_CB_HQ, _CB_ZF, _CB_ZB, _CB_HI, _CB_HG, _CB_GU, _CB_GV = range(7)


def _dot(a, b):
    return jnp.dot(a, b, preferred_element_type=F32)


def _dot_nt(a, b):
    return lax.dot_general(a, b, (((1,), (1,)), ((), ())), preferred_element_type=F32)


def _dot_tn(a, b):
    return lax.dot_general(a, b, (((0,), (0,)), ((), ())), preferred_element_type=F32)


def _split3(x):
    hi = x.astype(BF16)
    r1 = x - hi.astype(F32)
    mid = r1.astype(BF16)
    lo = (r1 - mid.astype(F32)).astype(BF16)
    return hi, mid, lo


D_PACK = D_MODEL // 2
N_SPLIT = 2
D_SLAB = D_PACK // N_SPLIT


def _pack_halves(x):
    half = x.shape[1] // 2
    lo = pltpu.bitcast(x[:, :half].astype(BF16).astype(F32), jnp.uint32)
    hi = pltpu.bitcast(x[:, half:].astype(BF16).astype(F32), jnp.uint32)
    return pltpu.bitcast(jnp.right_shift(lo, jnp.uint32(16)) | hi, jnp.int32)


def _unpack_halves(w):
    u = pltpu.bitcast(w, jnp.uint32)
    lo = pltpu.bitcast(jnp.left_shift(u, jnp.uint32(16)), F32)
    hi = pltpu.bitcast(u & jnp.uint32(0xFFFF0000), F32)
    return lo, hi


def _load_slabs(refs, *lead):
    return jnp.concatenate([r[lead] if lead else r[...] for r in refs], axis=1)


def _store_slabs(refs, packed):
    for si, r in enumerate(refs):
        r[...] = packed[:, si * D_SLAB:(si + 1) * D_SLAB]


def _params(n_axes=1):
    return pltpu.CompilerParams(dimension_semantics=("arbitrary",) * n_axes,
                                vmem_limit_bytes=V7X_VMEM_LIMIT)


def _mod_kernel(cond_ref, w_ref, b_ref, o_ref):
    c = cond_ref[...]
    c = c * jax.nn.sigmoid(c)
    w = w_ref[0]
    c_hi = c.astype(BF16)
    c_lo = (c - c_hi.astype(F32)).astype(BF16)
    w_hi = w.astype(BF16)
    w_lo = (w - w_hi.astype(F32)).astype(BF16)
    o_ref[0] = _dot(c_hi, w_hi) + _dot(c_lo, w_hi) + _dot(c_hi, w_lo) + b_ref[0]


def _modulation(cond, w_mod, b_mod):
    tn = 1536
    return pl.pallas_call(
        _mod_kernel,
        out_shape=jax.ShapeDtypeStruct((DEPTH, MOD_ROWS, 6 * D_MODEL), F32),
        grid=(DEPTH, 6 * D_MODEL // tn),
        in_specs=[pl.BlockSpec((MOD_ROWS, D_MODEL), lambda l, j: (0, 0)),
                  pl.BlockSpec((1, D_MODEL, tn), lambda l, j: (l, 0, j)),
                  pl.BlockSpec((1, 1, tn), lambda l, j: (l, 0, j))],
        out_specs=pl.BlockSpec((1, MOD_ROWS, tn), lambda l, j: (l, 0, j)),
        compiler_params=_params(2),
        name="modulation",
    )(cond, w_mod, b_mod.reshape(DEPTH, 1, 6 * D_MODEL))


def _rms_mod(x, g, shift, scale):
    ms = jnp.mean(x * x, axis=-1, keepdims=True)
    y = x * lax.rsqrt(ms + RMS_EPS) * g
    return y * (1.0 + scale) + shift


def _project_in(hm, w_ref, qkv_ref, h_ref, kc_ref, vc_ref):
    h = _dot(hm.astype(BF16), w_ref[0])
    qkv_ref[...] = h[:, :QKV_COLS].astype(BF16)
    h_ref[...] = h[:, QKV_COLS:]

    @pl.when(pl.program_id(0) < P_TILES)
    def _():
        for sq in range(SEQ_PER_TILE):
            rows = slice(sq * SEQ, (sq + 1) * SEQ)
            kc_ref[sq, 0:NA_WIDTH] = h[rows, NA_WIDTH:2 * NA_WIDTH].T
            vc_ref[sq, 0:NA_WIDTH] = h[rows, 2 * NA_WIDTH:3 * NA_WIDTH].T
            if kc_ref.shape[1] > NA_WIDTH:
                kc_ref[sq, NA_WIDTH:] = jnp.zeros((kc_ref.shape[1] - NA_WIDTH, SEQ), F32)
                vc_ref[sq, NA_WIDTH:] = jnp.zeros((vc_ref.shape[1] - NA_WIDTH, SEQ), F32)


def _pick_group(p_ref, s_ref):
    return jnp.where(pl.program_id(0) < P_TILES, p_ref[...], s_ref[...])


def _p_tile(width):
    return pl.BlockSpec((TM, width), lambda i: (jnp.minimum(i, P_TILES - 1), 0))


def _s_tile(width):
    return pl.BlockSpec((TM, width), lambda i: (jnp.maximum(i - P_TILES, 0), 0))


def _inproj_first_kernel(xp_ref, xs_ref, mod_ref, g_ref, w_ref, qkv_ref, h_ref, kc_ref, vc_ref):
    x = _pick_group(xp_ref, xs_ref)
    hm = _rms_mod(x, g_ref[...], mod_ref[0, 0:1, :], mod_ref[0, 1:2, :])
    _project_in(hm, w_ref, qkv_ref, h_ref, kc_ref, vc_ref)


def _combine_experts(yg_refs, gate_ref):
    gates = gate_ref[...]
    lo_acc = hi_acc = None
    for kk in range(TOP_K):
        lo, hi = _unpack_halves(_load_slabs(yg_refs, kk))
        gk = gates[:, kk:kk + 1]
        lo_acc = gk * lo if lo_acc is None else lo_acc + gk * lo
        hi_acc = gk * hi if hi_acc is None else hi_acc + gk * hi
    return jnp.concatenate([lo_acc, hi_acc], axis=1)


def _inproj_next_kernel(x_ref, yga_ref, ygb_ref, gate_ref, pmod_ref, mod_ref, g_ref, w_ref, kc_in, vc_in,
                        qkv_ref, h_ref, xo_ref, kc_ref, vc_ref):
    del kc_in, vc_in
    x = x_ref[...] + pmod_ref[0, 5:6, :] * _combine_experts((yga_ref, ygb_ref), gate_ref)
    xo_ref[...] = x
    hm = _rms_mod(x, g_ref[...], mod_ref[0, 0:1, :], mod_ref[0, 1:2, :])
    _project_in(hm, w_ref, qkv_ref, h_ref, kc_ref, vc_ref)


_TILE_SPEC = pl.BlockSpec((TM, D_MODEL), lambda i: (i, 0))
_MOD_SPEC = pl.BlockSpec((1, MOD_ROWS, D_MODEL), lambda i: (i, 0, 0))
_ROW_SPEC = pl.BlockSpec((1, D_MODEL), lambda i: (0, 0))
_RT_LANES = 128
_RT_ROWS = 2 * TOP_K
_YG_SPEC = pl.BlockSpec((TOP_K, TM, D_SLAB), lambda i: (0, i, 0))
_GATE_SPEC = pl.BlockSpec((TM, _RT_LANES), lambda i: (i, 0))


def _inproj(layer, x, moe, prev_mod, mod, g, w_bf16, caches):
    w_spec = pl.BlockSpec((1, D_MODEL, IN_COLS), lambda i: (layer, 0, 0))
    h_spec = pl.BlockSpec((TM, REST_COLS), lambda i: (i, 0))
    h_shape = jax.ShapeDtypeStruct((T_ALL, REST_COLS), F32)
    q_spec = pl.BlockSpec((TM, QKV_COLS), lambda i: (i, 0))
    q_shape = jax.ShapeDtypeStruct((T_ALL, QKV_COLS), BF16)
    c_spec = pl.BlockSpec((SEQ_PER_TILE, NA_WIDTH, SEQ), lambda i: (jnp.minimum(i, P_TILES - 1), layer, 0))
    c_shape = jax.ShapeDtypeStruct((BATCH, DEPTH * NA_WIDTH, SEQ), F32)
    x_shape = jax.ShapeDtypeStruct((T_ALL, D_MODEL), F32)
    if moe is None:
        c_all = pl.BlockSpec((SEQ_PER_TILE, DEPTH * NA_WIDTH, SEQ), lambda i: (jnp.minimum(i, P_TILES - 1), 0, 0))
        qkv, h, kc, vc = pl.pallas_call(
            _inproj_first_kernel, out_shape=(q_shape, h_shape, c_shape, c_shape), grid=(N_TILES,),
            in_specs=[_p_tile(D_MODEL), _s_tile(D_MODEL), _MOD_SPEC, _ROW_SPEC, w_spec],
            out_specs=(q_spec, h_spec, c_all, c_all),
            compiler_params=_params(), name="inproj_first",
        )(*x, mod, g, w_bf16)
        return qkv, h, x, (kc, vc)
    qkv, h, x, kc, vc = pl.pallas_call(
        _inproj_next_kernel,
        out_shape=(q_shape, h_shape, x_shape, c_shape, c_shape),
        grid=(N_TILES,),
        in_specs=[_TILE_SPEC, _YG_SPEC, _YG_SPEC, _GATE_SPEC, _MOD_SPEC, _MOD_SPEC, _ROW_SPEC, w_spec,
                  pl.BlockSpec(memory_space=pl.ANY), pl.BlockSpec(memory_space=pl.ANY)],
        out_specs=(q_spec, h_spec, _TILE_SPEC, c_spec, c_spec),
        input_output_aliases={8: 3, 9: 4},
        compiler_params=_params(), name="inproj_next",
    )(x, *moe[0], moe[1], prev_mod, mod, g, w_bf16, *caches)
    return qkv, h, x, (kc, vc)


def _pair_mask(hh):
    lane = lax.broadcasted_iota(jnp.int32, (1, 2 * NA_HEAD_DIM), 1)
    return (lane >= hh * NA_HEAD_DIM) & (lane < (hh + 1) * NA_HEAD_DIM)


_ATT_SEQS = 2


def _stack_pair(qp):
    return jnp.concatenate([jnp.where(_pair_mask(hh), qp, jnp.zeros_like(qp)) for hh in range(2)], axis=0)


def _unstack_pair(o2):
    half = o2.shape[0] // 2
    return jnp.where(_pair_mask(0), o2[:half], o2[half:])


def _attn_prompt_kernel(q_ref, k_ref, v_ref, o_ref):
    scale = NA_HEAD_DIM ** -0.5
    for sq in range(_ATT_SEQS):
        rows = slice(sq * SEQ, (sq + 1) * SEQ)
        for p in range(NA_HEADS // 2):
            cols = slice(p * 128, (p + 1) * 128)
            q2 = _stack_pair(q_ref[rows, cols] * scale)
            s = _dot_nt(q2, k_ref[rows, cols])
            e = jnp.exp(s - jnp.max(s, axis=-1, keepdims=True))
            den = jnp.sum(e, axis=-1, keepdims=True)
            o_ref[rows, cols] = _unstack_pair(_dot(e.astype(BF16), v_ref[rows, cols]) / den).astype(o_ref.dtype)


def _attn_prompt(qkv):
    rows = _ATT_SEQS * SEQ
    return pl.pallas_call(
        _attn_prompt_kernel,
        out_shape=jax.ShapeDtypeStruct((T_PROMPT, NA_WIDTH), BF16),
        grid=(BATCH // _ATT_SEQS,),
        in_specs=[pl.BlockSpec((rows, NA_WIDTH), lambda b: (b, 0)),
                  pl.BlockSpec((rows, NA_WIDTH), lambda b: (b, 1)),
                  pl.BlockSpec((rows, NA_WIDTH), lambda b: (b, 2))],
        out_specs=pl.BlockSpec((rows, NA_WIDTH), lambda b: (b, 0)),
        compiler_params=_params(), name="attn_prompt",
    )(qkv, qkv, qkv)


_NA_ROWS = DEC_SEQ // GRID_W
_NA_LOC = NA_KH * GRID_W
_NA_STEP_ROWS = 2


def _na_window_start(r):
    return jnp.clip(r - NA_KH // 2, 0, _NA_ROWS - NA_KH)


def _attn_sample_kernel(q_ref, k_ref, v_ref, ck_ref, cv_ref, *rest):
    bias_refs, o_ref = rest[:_NA_STEP_ROWS], rest[_NA_STEP_ROWS]
    scale = NA_HEAD_DIM ** -0.5
    for p in range(NA_HEADS // 2):
        cols = slice(p * 128, (p + 1) * 128)
        kc = ck_ref[0, :, cols].astype(BF16)
        vc = cv_ref[0, :, cols].astype(BF16)
        for u in range(_NA_STEP_ROWS):
            rows = slice(u * GRID_W, (u + 1) * GRID_W)
            s0 = pl.multiple_of(_na_window_start(pl.program_id(1) * _NA_STEP_ROWS + u) * GRID_W, GRID_W)
            q2 = _stack_pair(q_ref[rows, cols] * scale)
            bias2 = jnp.concatenate([bias_refs[u][0, 0, 2 * p], bias_refs[u][0, 0, 2 * p + 1]], axis=0)
            sl = _dot_nt(q2, k_ref[pl.ds(s0, _NA_LOC), cols]) + bias2
            sc = _dot_nt(q2, kc)
            mx = jnp.maximum(jnp.max(sl, axis=-1, keepdims=True), jnp.max(sc, axis=-1, keepdims=True))
            el = jnp.exp(sl - mx)
            ec = jnp.exp(sc - mx)
            den = jnp.sum(el, axis=-1, keepdims=True) + jnp.sum(ec, axis=-1, keepdims=True)
            o2 = (_dot(el.astype(BF16), v_ref[pl.ds(s0, _NA_LOC), cols]) + _dot(ec.astype(BF16), vc)) / den
            o_ref[rows, cols] = _unstack_pair(o2).astype(o_ref.dtype)


def _attn_sample(layer, qkv, ck, cv, bias):
    q_rows = _NA_STEP_ROWS * GRID_W
    steps = _NA_ROWS // _NA_STEP_ROWS
    q_blk0 = T_PROMPT // q_rows
    kv_row0 = T_PROMPT // DEC_SEQ

    def bias_spec(u):
        def index(b, r2):
            r = r2 * _NA_STEP_ROWS + u
            return (layer, _na_window_start(r) - r + NA_KH - 1, 0, 0, 0)
        return pl.BlockSpec((1, 1, NA_HEADS, GRID_W, _NA_LOC), index)

    return pl.pallas_call(
        _attn_sample_kernel,
        out_shape=jax.ShapeDtypeStruct((T_SAMPLE, NA_WIDTH), BF16),
        grid=(DEC_BATCH, steps),
        in_specs=[pl.BlockSpec((q_rows, NA_WIDTH), lambda b, r2: (q_blk0 + b * steps + r2, 0)),
                  pl.BlockSpec((DEC_SEQ, NA_WIDTH), lambda b, r2: (kv_row0 + b, 1)),
                  pl.BlockSpec((DEC_SEQ, NA_WIDTH), lambda b, r2: (kv_row0 + b, 2)),
                  pl.BlockSpec((1, PAST_LEN, NA_WIDTH), lambda b, r2: (b, 0, 0)),
                  pl.BlockSpec((1, PAST_LEN, NA_WIDTH), lambda b, r2: (b, 0, 0))]
                 + [bias_spec(u) for u in range(_NA_STEP_ROWS)],
        out_specs=pl.BlockSpec((q_rows, NA_WIDTH), lambda b, r2: (b * steps + r2, 0)),
        compiler_params=_params(2), name="attn_sample",
    )(qkv, qkv, qkv, ck, cv, *([bias] * _NA_STEP_ROWS))


_NA_DR = 2 * NA_KH - 1
_NA_DC = 2 * NA_KW - 1


def _na_bias_kernel(rb_ref, o_ref):
    i = pl.program_id(0)
    qc = lax.broadcasted_iota(jnp.int32, (GRID_W, GRID_W), 0)
    kc = lax.broadcasted_iota(jnp.int32, (GRID_W, GRID_W), 1)
    q_start = jnp.clip(qc - NA_KW // 2, 0, GRID_W - NA_KW)
    in_win = (kc >= q_start) & (kc < q_start + NA_KW)
    dc = jnp.clip(kc - qc + NA_KW - 1, 0, _NA_DC - 1)
    picks = [dc == d for d in range(_NA_DC)]
    tiles = []
    for dr in range(_NA_DR):
        acc = jnp.zeros((GRID_W, GRID_W), F32)
        for d in range(_NA_DC):
            acc = jnp.where(picks[d], rb_ref[i, dr * _NA_DC + d], acc)
        tiles.append(jnp.where(in_win, acc, NEG_INF))
    for base in range(NA_KH):
        o_ref[0, base, 0] = jnp.concatenate(tiles[base:base + NA_KH], axis=1)


def _na_bias_tables(rel_bias):
    rb = rel_bias.astype(F32).reshape(DEPTH * NA_HEADS, _NA_DR * _NA_DC)
    return pl.pallas_call(
        _na_bias_kernel,
        out_shape=jax.ShapeDtypeStruct((DEPTH, NA_KH, NA_HEADS, GRID_W, _NA_LOC), F32),
        grid=(DEPTH * NA_HEADS,),
        in_specs=[pl.BlockSpec(memory_space=pltpu.SMEM)],
        out_specs=pl.BlockSpec((1, NA_KH, 1, GRID_W, _NA_LOC),
                               lambda i: (i // NA_HEADS, 0, i % NA_HEADS, 0, 0)),
        compiler_params=_params(), name="na_bias_tables",
    )(rb)


_HG_GROUP = 8


def _hgrn_kernel(*refs, n_tok, has_state, n_alias):
    refs = refs[:8 + 2 * has_state] + refs[8 + 2 * has_state + n_alias:]
    if has_state:
        (q_ref, zf_ref, zb_ref, v_ref, g_ref, lbf_ref, lbb_ref, og_ref, s0f_ref, s0b_ref,
         rec_ref, sf_ref, sb_ref, kf_s, bf_s, kb_s, bb_s, of_s, ob_s, zf_s, zb_s, qsf_s, qsb_s, stf_s, stb_s) = refs
    else:
        (q_ref, zf_ref, zb_ref, v_ref, g_ref, lbf_ref, lbb_ref, og_ref,
         rec_ref, sf_ref, sb_ref, kf_s, bf_s, kb_s, bb_s, of_s, ob_s, zf_s, zb_s, qsf_s, qsb_s, stf_s, stb_s) = refs
        s0f_ref = s0b_ref = None
    C = HG_CHUNK
    W = HG_WIDTH
    n_chunks = n_tok // C
    rr = lax.broadcasted_iota(jnp.int32, (W, W), 0)
    cc = lax.broadcasted_iota(jnp.int32, (W, W), 1)
    log2_c = C.bit_length() - 1
    same_chunk = jnp.right_shift(rr, log2_c) == jnp.right_shift(cc, log2_c)
    tri_prefix = jnp.where(same_chunk & (cc <= rr), 1.0, 0.0).astype(BF16)
    tri_suffix = jnp.where(same_chunk & (cc >= rr), 1.0, 0.0).astype(BF16)
    same_head = jnp.right_shift(rr, 6) == jnp.right_shift(cc, 6)
    head_ones = jnp.where(same_head, 1.0, 0.0).astype(BF16)

    for ti in range(n_tok // W):
        rows = slice(ti * W, (ti + 1) * W)
        for z_ref, lb_ref, k_s, b_s, tri in ((zf_ref, lbf_ref, kf_s, bf_s, tri_prefix),
                                             (zb_ref, lbb_ref, kb_s, bb_s, tri_suffix)):
            z = z_ref[rows, :]
            lb = lb_ref[...]
            e = jnp.exp(-jnp.abs(z))
            big = 1.0 / (1.0 + e)
            small = e * big
            f = lb + (1.0 - lb) * jnp.where(z >= 0.0, big, small)
            logf = jnp.log(jnp.maximum(f, F_FLOOR))
            k_s[rows, :] = (1.0 - lb) * jnp.where(z >= 0.0, small, big)
            hi, mid, lo = _split3(logf)
            b_s[rows, :] = _dot(tri, hi) + _dot(tri, mid) + _dot(tri, lo)

    G = _HG_GROUP
    n_groups = C // G
    srow = lax.broadcasted_iota(jnp.int32, (G, W), 0)
    zf_s[...] = jnp.zeros_like(zf_s)
    zb_s[...] = jnp.zeros_like(zb_s)

    def scan_chunk(ci, k_s, b_s, z_s, ks_s, st_s, o_dir_s, fwd):
        c = ci if fwd else n_chunks - 1 - ci
        base = pl.multiple_of(c * C, C)
        q = q_ref[pl.ds(base, C), :]
        k = k_s[pl.ds(base, C), :]
        b = b_s[pl.ds(base, C), :]
        v = v_ref[pl.ds(base, C), :]
        q_far = {}
        for gs in range(n_groups):
            others = range(gs + 1, n_groups) if fwd else range(gs)
            if not others:
                continue
            rows_s = slice(gs * G, (gs + 1) * G)
            edge = (gs + 1) * G - 1 if fwd else gs * G
            b_edge = b[edge:edge + 1, :]
            ks_s[rows_s, :] = k[rows_s] * jnp.exp(b_edge - b[rows_s])
            for gt in others:
                rows_t = slice(gt * G, (gt + 1) * G)
                q_far[gs, gt] = q[rows_t] * jnp.exp(b[rows_t] - b_edge)
        for sx in range(C):
            gs = sx // G
            rows_g = slice(gs * G, (gs + 1) * G)
            k_row = k_s[pl.ds(base + sx, 1), :]
            b_row = b_s[pl.ds(base + sx, 1), :]
            keep = (srow + gs * G >= sx) if fwd else (srow + gs * G <= sx)
            z_s[sx * C + gs * G:sx * C + (gs + 1) * G, :] = jnp.where(
                keep, (k_row * q[rows_g]) * jnp.exp(b[rows_g] - b_row), 0.0)
            others = range(gs + 1, n_groups) if fwd else range(gs)
            if others:
                ks_row = ks_s[sx:sx + 1, :]
                for gt in others:
                    z_s[sx * C + gt * G:sx * C + (gt + 1) * G, :] = ks_row * q_far[gs, gt]
        a_rep = _dot(z_s[...].astype(BF16), head_ones)
        o_intra = jnp.sum(a_rep.reshape(C, C, W) * v[:, None, :], axis=0)
        b_end = b_s[pl.ds(base + (C - 1 if fwd else 0), 1), :]
        q_in = q * jnp.exp(b)
        k_st = k * jnp.exp(b_end - b)
        st = st_s[...]
        o_inter = _dot_nt(q_in.astype(BF16), st.astype(BF16))
        upd = _dot_tn(v.astype(BF16), k_st.astype(BF16))
        st_s[...] = st * jnp.exp(b_end) + jnp.where(same_head, upd, 0.0)
        o_dir_s[pl.ds(base, C), :] = o_intra + o_inter

    def load_state(s0_ref, st_s):
        if s0_ref is None:
            st_s[...] = jnp.zeros((W, W), F32)
            return
        for hh in range(HG_HEADS):
            parts = [s0_ref[0, hh] if g == hh else jnp.zeros((HG_DK, HG_DV), F32) for g in range(HG_HEADS)]
            st_s[hh * HG_DK:(hh + 1) * HG_DK, :] = jnp.concatenate(parts, axis=1)
        st_s[...] = st_s[...].T

    def store_state(st_s, out_ref):
        by_head = st_s[...].T
        for hh in range(HG_HEADS):
            out_ref[0, hh] = by_head[hh * HG_DK:(hh + 1) * HG_DK, hh * HG_DV:(hh + 1) * HG_DV]
        if out_ref.shape[1] > HG_HEADS:
            out_ref[0, HG_HEADS:] = jnp.zeros((out_ref.shape[1] - HG_HEADS, HG_DK, HG_DV), F32)

    load_state(s0f_ref, stf_s)
    load_state(s0b_ref, stb_s)

    def scan_both(ci, carry):
        scan_chunk(ci, kf_s, bf_s, zf_s, qsf_s, stf_s, of_s, True)
        scan_chunk(ci, kb_s, bb_s, zb_s, qsb_s, stb_s, ob_s, False)
        return carry
    lax.fori_loop(0, n_chunks, scan_both, 0)
    store_state(stf_s, sf_ref)
    store_state(stb_s, sb_ref)

    for ti in range(n_tok // W):
        rows = slice(ti * W, (ti + 1) * W)
        o = of_s[rows, :] + ob_s[rows, :]
        sq = o * o
        sq_hi = sq.astype(BF16)
        sq_lo = (sq - sq_hi.astype(F32)).astype(BF16)
        ms = (_dot(sq_hi, head_ones) + _dot(sq_lo, head_ones)) * (1.0 / HG_DV)
        g = g_ref[rows, :]
        y = o * lax.rsqrt(ms + RMS_EPS) * og_ref[...] * (g * jax.nn.sigmoid(g))
        rec_ref[rows, :] = y.astype(rec_ref.dtype)


def _hgrn(h, lbf, lbb, og, s0f, s0b, n_tok, n_seq, row0, layer=None, states=None):
    W = HG_WIDTH
    has_state = s0f is not None

    def col(cb):
        return pl.BlockSpec((n_tok, W), lambda i, cb=cb: (row0 + i, cb))

    vec = pl.BlockSpec((1, W), lambda i: (0, 0))
    st_spec = pl.BlockSpec((1, HG_HEADS, HG_DK, HG_DV), lambda i: (i, 0, 0, 0))
    in_specs = [col(_CB_HQ), col(_CB_ZF), col(_CB_ZB), col(_CB_HI), col(_CB_HG), vec, vec, vec]
    args = [h, h, h, h, h, lbf, lbb, og]
    if has_state:
        in_specs += [st_spec, st_spec]
        args += [s0f, s0b]
    seq_f32 = pltpu.VMEM((n_tok, W), F32)
    out_st_spec, st_rows, aliases = st_spec, HG_HEADS, {}
    if layer is not None:
        st_rows = DEPTH * HG_HEADS
        if states is None:
            out_st_spec = pl.BlockSpec((1, st_rows, HG_DK, HG_DV), lambda i: (i, 0, 0, 0))
        else:
            out_st_spec = pl.BlockSpec((1, HG_HEADS, HG_DK, HG_DV), lambda i: (i, layer, 0, 0))
            aliases = {len(args): 1, len(args) + 1: 2}
            in_specs += [pl.BlockSpec(memory_space=pl.ANY)] * 2
            args += list(states)
    return pl.pallas_call(
        functools.partial(_hgrn_kernel, n_tok=n_tok, has_state=has_state, n_alias=len(aliases)),
        out_shape=(jax.ShapeDtypeStruct((n_seq * n_tok, W), BF16),
                   jax.ShapeDtypeStruct((n_seq, st_rows, HG_DK, HG_DV), F32),
                   jax.ShapeDtypeStruct((n_seq, st_rows, HG_DK, HG_DV), F32)),
        grid=(n_seq,),
        in_specs=in_specs,
        input_output_aliases=aliases,
        out_specs=(pl.BlockSpec((n_tok, W), lambda i: (i, 0)), out_st_spec, out_st_spec),
        scratch_shapes=[seq_f32, seq_f32, seq_f32, seq_f32, seq_f32, seq_f32,
                        pltpu.VMEM((HG_CHUNK * HG_CHUNK, W), F32),
                        pltpu.VMEM((HG_CHUNK * HG_CHUNK, W), F32),
                        pltpu.VMEM((HG_CHUNK, W), F32),
                        pltpu.VMEM((HG_CHUNK, W), F32),
                        pltpu.VMEM((W, W), F32),
                        pltpu.VMEM((W, W), F32)],
        compiler_params=_params(), name="hgrn_state" if has_state else "hgrn_zero",
    )(*args)


def _gmlp_kernel(u_ref, v_ref, g_ref, ws_ref, b_ref, o_ref):
    lane = lax.broadcasted_iota(jnp.int32, (1, GM_WIDTH), 1)
    for ci in range(TM // GM_CHUNK):
        rows = slice(ci * GM_CHUNK, (ci + 1) * GM_CHUNK)
        v = v_ref[rows, :]
        ms = jnp.mean(v * v, axis=-1, keepdims=True)
        vn = (v * lax.rsqrt(ms + RMS_EPS) * g_ref[...]).astype(BF16)
        z = b_ref[...]
        for gi in range(GM_GROUPS):
            zg = _dot(ws_ref[gi], vn)
            in_group = (lane >= gi * GM_GDIM) & (lane < (gi + 1) * GM_GDIM)
            z = z + jnp.where(in_group, zg, 0.0)
        o_ref[rows, :] = (u_ref[rows, :] * z).astype(o_ref.dtype)


def _gmlp(h, vnorm_g, ws_bf16, bias_full):
    W = GM_WIDTH
    return pl.pallas_call(
        _gmlp_kernel,
        out_shape=jax.ShapeDtypeStruct((T_ALL, W), BF16),
        grid=(N_TILES,),
        in_specs=[pl.BlockSpec((TM, W), lambda i: (i, _CB_GU)),
                  pl.BlockSpec((TM, W), lambda i: (i, _CB_GV)),
                  pl.BlockSpec((1, W), lambda i: (0, 0)),
                  pl.BlockSpec((GM_GROUPS, GM_CHUNK, GM_CHUNK), lambda i: (0, 0, 0)),
                  pl.BlockSpec((GM_CHUNK, W), lambda i: (0, 0))],
        out_specs=pl.BlockSpec((TM, W), lambda i: (i, 0)),
        compiler_params=_params(), name="gmlp",
    )(h, h, vnorm_g, ws_bf16, bias_full)


def _outproj_kernel(attp_ref, atts_ref, recp_ref, recs_ref, mlp_ref, xp_ref, xs_ref, mod_ref, g_ref, w_ref, wr_ref, br_ref,
                    x1_ref, h2a_ref, h2b_ref, rt_ref, gate_ref, cnt_ref):
    @pl.when(pl.program_id(0) == 0)
    def _():
        cnt_ref[...] = jnp.zeros_like(cnt_ref)

    out = (_dot(_pick_group(attp_ref, atts_ref), w_ref[0, 0:NA_WIDTH, :])
           + _dot(_pick_group(recp_ref, recs_ref), w_ref[0, NA_WIDTH:NA_WIDTH + HG_WIDTH, :])
           + _dot(mlp_ref[...], w_ref[0, NA_WIDTH + HG_WIDTH:, :]))
    x1 = _pick_group(xp_ref, xs_ref) + mod_ref[0, 2:3, :] * out
    x1_ref[...] = x1
    h2 = _rms_mod(x1, g_ref[...], mod_ref[0, 3:4, :], mod_ref[0, 4:5, :])
    _store_slabs((h2a_ref, h2b_ref), _pack_halves(h2))
    h_hi = h2.astype(BF16)
    h_lo = (h2 - h_hi.astype(F32)).astype(BF16)
    wr = wr_ref[...]
    w_hi = wr.astype(BF16)
    w_lo = (wr - w_hi.astype(F32)).astype(BF16)
    logits = _dot(h_hi, w_hi) + _dot(h_lo, w_hi) + _dot(h_hi, w_lo) + br_ref[...]
    lane_e = lax.broadcasted_iota(jnp.int32, (TM, N_EXPERTS), 1).astype(F32)
    lane_o = lax.broadcasted_iota(jnp.int32, (TM, _RT_LANES), 1)
    idx_acc = jnp.zeros((TM, _RT_LANES), F32)
    val_acc = jnp.zeros((TM, _RT_LANES), F32)
    top0 = None
    den = jnp.zeros((TM, 1), F32)
    work = logits
    picks = []
    for kk in range(TOP_K):
        m = jnp.max(work, axis=-1, keepdims=True)
        first = jnp.min(jnp.where(work == m, lane_e, float(N_EXPERTS)), axis=-1, keepdims=True)
        if kk == 0:
            top0 = m
        e = jnp.exp(m - top0)
        den = den + e
        idx_acc = jnp.where(lane_o == kk, first, idx_acc)
        val_acc = jnp.where(lane_o == kk, e, val_acc)
        picks.append(lane_e == first)
        work = jnp.where(picks[-1], -jnp.inf, work)
    gate_ref[...] = val_acc / den
    sel = jnp.zeros((TM, N_EXPERTS), F32)
    for pk in picks:
        sel = sel + jnp.where(pk, 1.0, 0.0)
    rr = lax.broadcasted_iota(jnp.int32, (TM, TM), 0)
    cc = lax.broadcasted_iota(jnp.int32, (TM, TM), 1)
    earlier = jnp.where(cc < rr, 1.0, 0.0).astype(BF16)
    seen = cnt_ref[0:1, 0:N_EXPERTS]
    before = _dot(earlier, sel.astype(BF16)) + seen
    for kk, pk in enumerate(picks):
        rank = jnp.sum(jnp.where(pk, before, 0.0), axis=-1, keepdims=True)
        idx_acc = jnp.where(lane_o == TOP_K + kk, rank, idx_acc)
    rt_ref[...] = idx_acc.T[0:_RT_ROWS, :].astype(jnp.int32)
    cnt_ref[0:1, 0:N_EXPERTS] = seen + jnp.sum(sel, axis=0, keepdims=True)


def _outproj(layer, att_p, att_s, rec_p, rec_s, mlp, x, mod, g, w_bf16, wr, br):
    def tile(width):
        return pl.BlockSpec((TM, width), lambda i: (i, 0))

    if isinstance(x, tuple):
        x_args, x_specs = x, [_p_tile(D_MODEL), _s_tile(D_MODEL)]
    else:
        x_args = (x, x)
        x_specs = [_p_tile(D_MODEL), pl.BlockSpec((TM, D_MODEL), lambda i: (jnp.maximum(i, P_TILES), 0))]
    return pl.pallas_call(
        _outproj_kernel,
        out_shape=(jax.ShapeDtypeStruct((T_ALL, D_MODEL), F32),
                   jax.ShapeDtypeStruct((T_ALL, D_SLAB), jnp.int32),
                   jax.ShapeDtypeStruct((T_ALL, D_SLAB), jnp.int32),
                   jax.ShapeDtypeStruct((_RT_ROWS, T_ALL), jnp.int32),
                   jax.ShapeDtypeStruct((T_ALL, _RT_LANES), F32),
                   jax.ShapeDtypeStruct((8, _RT_LANES), F32)),
        grid=(N_TILES,),
        in_specs=[_p_tile(NA_WIDTH), _s_tile(NA_WIDTH), _p_tile(HG_WIDTH), _s_tile(HG_WIDTH),
                  tile(GM_WIDTH), *x_specs, _MOD_SPEC, _ROW_SPEC,
                  pl.BlockSpec((1, D_MODEL, D_MODEL), lambda i: (layer, 0, 0)),
                  pl.BlockSpec((D_MODEL, N_EXPERTS), lambda i: (0, 0)),
                  pl.BlockSpec((1, N_EXPERTS), lambda i: (0, 0))],
        out_specs=(_TILE_SPEC, tile(D_SLAB), tile(D_SLAB), pl.BlockSpec((_RT_ROWS, TM), lambda i: (0, i)),
                   tile(_RT_LANES),
                   pl.BlockSpec((8, _RT_LANES), lambda i: (0, 0))),
        compiler_params=_params(), name="outproj_router",
    )(att_p, att_s, rec_p, rec_s, mlp, *x_args, mod, g, w_bf16, wr, br)


_W_CHUNKS = 4
_W_CAST_ROWS = 128


def _moe_kernel(blk_e_ref, blk_on_ref, blk_new_ref, blk_next_ref,
                xa_ref, xb_ref, wg_hbm, bg_ref, wu_hbm, bu_ref, wd_hbm, bd_ref,
                ya_ref, yb_ref, w_f32, w_bf16, w_sem, *, layer):
    j = pl.program_id(0)

    def weight_copies(expert):
        rows = D_MODEL // _W_CHUNKS
        return [pltpu.make_async_copy(w_hbm.at[layer, expert, pl.ds(ci * rows, rows)],
                                      w_f32.at[wi, pl.ds(ci * rows, rows)], w_sem.at[wi, ci])
                for wi, w_hbm in enumerate((wg_hbm, wu_hbm, wd_hbm)) for ci in range(_W_CHUNKS)]

    @pl.when(j == 0)
    def _():
        for cp in weight_copies(blk_e_ref[0]):
            cp.start()

    @pl.when(blk_new_ref[j] != 0)
    def _():
        for cp in weight_copies(blk_e_ref[j]):
            cp.wait()

        def cast_rows(ci, carry):
            rows = pl.ds(pl.multiple_of(ci * _W_CAST_ROWS, _W_CAST_ROWS), _W_CAST_ROWS)
            for wi in range(3):
                w_bf16[wi, rows, :] = w_f32[wi, rows, :].astype(BF16)
            return carry
        lax.fori_loop(0, D_MODEL // _W_CAST_ROWS, cast_rows, 0)

        @pl.when(blk_next_ref[j] >= 0)
        def _():
            for cp in weight_copies(blk_next_ref[j]):
                cp.start()

    @pl.when(blk_on_ref[j] != 0)
    def _():
        lo, hi = _unpack_halves(_load_slabs((xa_ref, xb_ref)))
        x = jnp.concatenate([lo.astype(BF16), hi.astype(BF16)], axis=1)
        gate = jnp.minimum(_dot(x, w_bf16[0]) + bg_ref[0, 0], SWIGLU_LIMIT)
        up = jnp.clip(_dot(x, w_bf16[1]) + bu_ref[0, 0], -SWIGLU_LIMIT, SWIGLU_LIMIT)
        glu = gate * jax.nn.sigmoid(SWIGLU_ALPHA * gate)
        act = ((up + 1.0) * glu).astype(BF16)
        _store_slabs((ya_ref, yb_ref), _pack_halves(_dot(act, w_bf16[2]) + bd_ref[0, 0]))

    @pl.when(blk_on_ref[j] == 0)
    def _():
        ya_ref[...] = jnp.zeros_like(ya_ref)
        yb_ref[...] = jnp.zeros_like(yb_ref)


def _moe(layer, plan, x_sorted, wg, bg, wu, bu, wd, bd):
    n_plan = len(plan)
    b_spec = pl.BlockSpec((1, 1, 1, D_MODEL), lambda j, be, *_: (layer, be[j], 0, 0))
    x_spec = pl.BlockSpec((MOE_BM, D_SLAB), lambda j, *_: (j, 0))
    hbm = pl.BlockSpec(memory_space=pl.ANY)
    bias4 = lambda b: b.reshape(DEPTH, N_EXPERTS, 1, D_MODEL)
    return pl.pallas_call(
        functools.partial(_moe_kernel, layer=layer),
        out_shape=(jax.ShapeDtypeStruct((MOE_SLOTS, D_SLAB), jnp.int32),) * N_SPLIT,
        grid_spec=pltpu.PrefetchScalarGridSpec(
            num_scalar_prefetch=n_plan, grid=(MOE_BLOCKS,),
            in_specs=[x_spec, x_spec, hbm, b_spec, hbm, b_spec, hbm, b_spec],
            out_specs=(x_spec, x_spec),
            scratch_shapes=[pltpu.VMEM((3, D_MODEL, D_MODEL), F32), pltpu.VMEM((3, D_MODEL, D_MODEL), BF16),
                            pltpu.SemaphoreType.DMA((3, _W_CHUNKS))]),
        compiler_params=_params(), name="moe_experts",
    )(*plan, *x_sorted, wg, bias4(bg), wu, bias4(bu), wd, bias4(bd))


def _route(rt, counts):
    experts = jnp.arange(N_EXPERTS, dtype=jnp.int32)
    nblk = (counts + MOE_BM - 1) // MOE_BM
    blk_end = jnp.cumsum(nblk)
    row0 = (blk_end - nblk) * MOE_BM
    top_i, rank = rt[:TOP_K], rt[TOP_K:]
    start_of = jnp.sum(jnp.where(top_i[None] == experts[:, None, None], row0[:, None, None], 0), axis=0)
    dest = (start_of + rank).reshape(1, TOP_K * T_ALL)
    live = counts > 0
    last_live = jnp.max(jnp.where(live, experts, 0))
    later_live = live[None, :] & (experts[None, :] > experts[:, None])
    next_live = jnp.min(jnp.where(later_live, experts[None, :], N_EXPERTS), axis=1)
    next_live = jnp.where(next_live == N_EXPERTS, -1, next_live)
    blk = jnp.arange(MOE_BLOCKS, dtype=jnp.int32)
    blk_on = blk < blk_end[-1]
    blk_e = jnp.where(blk_on, jnp.minimum(jnp.sum((blk_end[None, :] <= blk[:, None]).astype(jnp.int32), axis=1),
                                          N_EXPERTS - 1), last_live)
    blk_new = blk_on & jnp.concatenate([jnp.ones((1,), bool), blk_e[1:] != blk_e[:-1]])
    is_e = blk_e[:, None] == experts[None, :]
    lookup = lambda table: jnp.sum(jnp.where(is_e, table[None, :], 0), axis=1)
    plan = (blk_e, blk_on, blk_new, lookup(next_live))
    return dest.astype(jnp.int32), tuple(p.astype(jnp.int32) for p in plan)


_SC_WINDOW = 128


def _sc_mesh():
    return plsc.VectorSubcoreMesh(core_axis_name="core", subcore_axis_name="subcore")


def _sc_scatter_rows(srcs, idx, n_out):
    n_src, width = srcs[0].shape
    n_rep = idx.shape[1] // n_src
    src_windows = n_src // _SC_WINDOW
    assert len(srcs) == 2

    def body(*refs):
        x_hbm = refs[:len(srcs)]
        i_hbm = refs[len(srcs)]
        o_hbm = refs[len(srcs) + 1:]

        def run(xs, os_):
            def step(x_vmem, *i_vmem):
                for iv in i_vmem:
                    pltpu.sync_copy(x_vmem, os_.at[iv.at[0]])

            pltpu.emit_pipeline(
                step, grid=(src_windows,),
                in_specs=[pl.BlockSpec((_SC_WINDOW, width), lambda i: (i, 0))]
                         + [pl.BlockSpec((1, _SC_WINDOW), lambda i, kk=kk: (0, kk * src_windows + i))
                            for kk in range(n_rep)],
                out_specs=[],
                core_axis_name="subcore",
                dimension_semantics=(pltpu.PARALLEL,),
            )(xs, *([i_hbm] * n_rep))

        for ci, (xs, os_) in enumerate(zip(x_hbm, o_hbm)):
            pl.when(lax.axis_index("core") == ci)(functools.partial(run, xs, os_))

    out_type = tuple(jax.ShapeDtypeStruct((n_out, width), s.dtype) for s in srcs)
    return pl.kernel(body, out_type=out_type, mesh=_sc_mesh(), scratch_types=[],
                     name="sc_scatter_rows")(*srcs, idx)


def _sc_gather_rows(tables, idx):
    n_idx = idx.shape[1]
    width = tables[0].shape[1]

    def body(*refs):
        t_hbm = refs[:len(tables)]
        i_hbm = refs[len(tables)]
        o_hbm = refs[len(tables) + 1:]
        for ts, os_ in zip(t_hbm, o_hbm):
            def step(i_vmem, o_vmem, ts=ts):
                pltpu.sync_copy(ts.at[i_vmem.at[0]], o_vmem)

            pltpu.emit_pipeline(
                step, grid=(n_idx // _SC_WINDOW,),
                in_specs=[pl.BlockSpec((1, _SC_WINDOW), lambda i: (0, i))],
                out_specs=[pl.BlockSpec((_SC_WINDOW, width), lambda i: (i, 0))],
                core_axis_name=("core", "subcore"),
                dimension_semantics=(pltpu.PARALLEL,),
            )(i_hbm, os_)

    out_type = tuple(jax.ShapeDtypeStruct((n_idx, width), t.dtype) for t in tables)
    return pl.kernel(body, out_type=out_type, mesh=_sc_mesh(), scratch_types=[],
                     name="sc_gather_rows")(*tables, idx)


def _final_kernel(x_ref, yga_ref, ygb_ref, gate_ref, mod_ref, g_ref, yp_ref, ys_ref):
    x = x_ref[...] + mod_ref[0, 5:6, :] * _combine_experts((yga_ref, ygb_ref), gate_ref)
    ms = jnp.mean(x * x, axis=-1, keepdims=True)
    y = x * lax.rsqrt(ms + RMS_EPS) * g_ref[...]

    @pl.when(pl.program_id(0) < P_TILES)
    def _():
        yp_ref[...] = y

    @pl.when(pl.program_id(0) >= P_TILES)
    def _():
        ys_ref[...] = y


def _final(x, moe, mod, g):
    return pl.pallas_call(
        _final_kernel,
        out_shape=(jax.ShapeDtypeStruct((T_PROMPT, D_MODEL), F32), jax.ShapeDtypeStruct((T_SAMPLE, D_MODEL), F32)),
        grid=(N_TILES,),
        in_specs=[_TILE_SPEC, _YG_SPEC, _YG_SPEC, _GATE_SPEC, _MOD_SPEC, _ROW_SPEC],
        out_specs=(_p_tile(D_MODEL), _s_tile(D_MODEL)),
        compiler_params=_params(), name="final_norm",
    )(x, *moe[0], moe[1], mod, g)


def kernel(x_prompt, x_sample, cache_k, cache_v, state_hgrn_fwd, state_hgrn_bwd, c, c_ctx, w_mod, b_mod, norm1_g, norm2_g, w_in, na_rel_bias, hgrn_lb, hgrn_onorm_g, gmlp_vnorm_g, gmlp_ws, gmlp_b, w_out, router_w, router_b, w_gate, b_gate, w_up, b_up, w_down, b_down, final_g):
    x = (x_prompt.reshape(T_PROMPT, D_MODEL), x_sample.reshape(T_SAMPLE, D_MODEL))

    cond = jnp.zeros((MOD_ROWS, D_MODEL), F32).at[0].set(c_ctx).at[1:1 + DEC_BATCH].set(c)
    mod = _modulation(cond, w_mod, b_mod)
    tile_row = np.concatenate([np.zeros(P_TILES, np.int32),
                               1 + np.arange(N_TILES - P_TILES, dtype=np.int32) // (DEC_SEQ // TM)])
    mod_tiles = mod[:, tile_row].reshape(DEPTH, N_TILES, 6, D_MODEL)
    mod_tiles = jnp.pad(mod_tiles, ((0, 0), (0, 0), (0, MOD_ROWS - 6), (0, 0)))

    lb_soft = jax.nn.softmax(hgrn_lb.astype(F32), axis=1)
    lower = jnp.cumsum(lb_soft, axis=1) - lb_soft[:, :1]

    na_bias = _na_bias_tables(na_rel_bias)
    w_in_bf16 = w_in.astype(BF16)
    w_out_bf16 = w_out.astype(BF16)

    moe_out = caches = states = None
    for l in range(DEPTH):
        qkv, h, x_next, caches = _inproj(l, x, moe_out, mod_tiles[l - 1] if l else None, mod_tiles[l],
                                    norm1_g[l][None, :], w_in_bf16, caches)

        att_p = _attn_prompt(qkv)
        att_s = _attn_sample(l, qkv, cache_k[:, l].reshape(DEC_BATCH, PAST_LEN, NA_WIDTH),
                             cache_v[:, l].reshape(DEC_BATCH, PAST_LEN, NA_WIDTH), na_bias)
        lbf = lower[0, l][None, :]
        lbb = lower[1, l][None, :]
        og = jnp.tile(hgrn_onorm_g[l], HG_HEADS)[None, :]
        rec_p, *states = _hgrn(h, lbf, lbb, og, None, None, SEQ, BATCH, 0, layer=l, states=states)
        rec_s, _, _ = _hgrn(h, lbf, lbb, og, state_hgrn_fwd[:, l].astype(F32), state_hgrn_bwd[:, l].astype(F32),
                            DEC_SEQ, DEC_BATCH, T_PROMPT // DEC_SEQ)
        gm_bias = jnp.repeat(gmlp_b[l].T, GM_GDIM, axis=1)
        mlp = _gmlp(h, gmlp_vnorm_g[l][None, :], gmlp_ws[l].astype(BF16), gm_bias)

        x, h2a, h2b, rt, gate_pad, cnt = _outproj(l, att_p, att_s, rec_p, rec_s, mlp, x_next, mod_tiles[l],
                                                  norm2_g[l][None, :], w_out_bf16,
                                                  router_w[l], router_b[l][None, :])
        dest_flat, plan = _route(rt, cnt[0, :N_EXPERTS].astype(jnp.int32))
        x_sorted = _sc_scatter_rows((h2a, h2b), dest_flat, MOE_SLOTS)
        y_sorted = _moe(l, plan, x_sorted, w_gate, b_gate, w_up, b_up, w_down, b_down)
        y_tok = _sc_gather_rows(y_sorted, dest_flat)
        moe_out = ([yt.reshape(TOP_K, T_ALL, D_SLAB) for yt in y_tok], gate_pad)

    y_prompt, y_sample = _final(x, moe_out, mod_tiles[DEPTH - 1], final_g[None, :])
    y_prompt = y_prompt.reshape(BATCH, SEQ, D_MODEL)
    y_sample = y_sample.reshape(DEC_BATCH, DEC_SEQ, D_MODEL)
    new_k, new_v = (cache.reshape(BATCH, DEPTH, NA_HEADS, NA_HEAD_DIM, SEQ).transpose(0, 1, 4, 2, 3)
                    for cache in caches)
    new_sf, new_sb = (st.reshape(BATCH, DEPTH, HG_HEADS, HG_DK, HG_DV) for st in states)
    return (y_prompt, y_sample, new_k, new_v, new_sf, new_sb)
```

```python
import functools

import numpy as np
import jax
import jax.numpy as jnp
from jax import lax
from jax.experimental import pallas as pl
from jax.experimental.pallas import tpu as pltpu
from jax.experimental.pallas import tpu_sc as plsc

F32 = jnp.float32
BF16 = jnp.bfloat16

D_MODEL = 1024
BATCH = 32
SEQ = 256
DEPTH = 2
DEC_BATCH = 2
DEC_SEQ = 1024
PAST_LEN = 512
GRID_W = 64
NA_HEADS = 8
NA_HEAD_DIM = 64
NA_WIDTH = NA_HEADS * NA_HEAD_DIM
NA_KH = 8
NA_KW = 16
HG_HEADS = 4
HG_DK = 64
HG_DV = 64
HG_WIDTH = HG_HEADS * HG_DV
HG_CHUNK = 16
F_FLOOR = 1e-30
GM_GROUPS = 4
GM_GDIM = 64
GM_WIDTH = GM_GROUPS * GM_GDIM
GM_CHUNK = 128
IN_COLS = 3 * NA_WIDTH + 5 * HG_WIDTH + 2 * GM_WIDTH
N_EXPERTS = 32
TOP_K = 4
SWIGLU_LIMIT = 7.0
SWIGLU_ALPHA = 1.702
RMS_EPS = 1e-6
NEG_INF = -1e30

T_PROMPT = BATCH * SEQ
T_SAMPLE = DEC_BATCH * DEC_SEQ
T_ALL = T_PROMPT + T_SAMPLE
TM = 512
SEQ_PER_TILE = TM // SEQ
N_TILES = T_ALL // TM
P_TILES = T_PROMPT // TM
MOE_BM = 256
MOE_SLOTS = -(-(T_ALL * TOP_K + N_EXPERTS * (MOE_BM - 1)) // MOE_BM) * MOE_BM
MOE_BLOCKS = MOE_SLOTS // MOE_BM
MOD_ROWS = 8
V7X_VMEM_LIMIT = 48 * 1024 * 1024

QKV_COLS = 3 * NA_WIDTH
REST_COLS = IN_COLS - QKV_COLS
_CB_HQ, _CB_ZF, _CB_ZB, _CB_HI, _CB_HG, _CB_GU, _CB_GV = range(7)


def _dot(a, b):
    return jnp.dot(a, b, preferred_element_type=F32)


def _dot_nt(a, b):
    return lax.dot_general(a, b, (((1,), (1,)), ((), ())), preferred_element_type=F32)


def _dot_tn(a, b):
    return lax.dot_general(a, b, (((0,), (0,)), ((), ())), preferred_element_type=F32)


def _split3(x):
    hi = x.astype(BF16)
    r1 = x - hi.astype(F32)
    mid = r1.astype(BF16)
    lo = (r1 - mid.astype(F32)).astype(BF16)
    return hi, mid, lo


D_PACK = D_MODEL // 2
N_SPLIT = 2
D_SLAB = D_PACK // N_SPLIT


def _pack_halves(x):
    half = x.shape[1] // 2
    lo = pltpu.bitcast(x[:, :half].astype(BF16).astype(F32), jnp.uint32)
    hi = pltpu.bitcast(x[:, half:].astype(BF16).astype(F32), jnp.uint32)
    return pltpu.bitcast(jnp.right_shift(lo, jnp.uint32(16)) | hi, jnp.int32)


def _unpack_halves(w):
    u = pltpu.bitcast(w, jnp.uint32)
    lo = pltpu.bitcast(jnp.left_shift(u, jnp.uint32(16)), F32)
    hi = pltpu.bitcast(u & jnp.uint32(0xFFFF0000), F32)
    return lo, hi


def _load_slabs(refs, *lead):
    return jnp.concatenate([r[lead] if lead else r[...] for r in refs], axis=1)


def _store_slabs(refs, packed):
    for si, r in enumerate(refs):
        r[...] = packed[:, si * D_SLAB:(si + 1) * D_SLAB]


def _params(n_axes=1):
    return pltpu.CompilerParams(dimension_semantics=("arbitrary",) * n_axes,
                                vmem_limit_bytes=V7X_VMEM_LIMIT)


def _mod_kernel(cond_ref, w_ref, b_ref, o_ref):
    c = cond_ref[...]
    c = c * jax.nn.sigmoid(c)
    w = w_ref[0]
    c_hi = c.astype(BF16)
    c_lo = (c - c_hi.astype(F32)).astype(BF16)
    w_hi = w.astype(BF16)
    w_lo = (w - w_hi.astype(F32)).astype(BF16)
    o_ref[0] = _dot(c_hi, w_hi) + _dot(c_lo, w_hi) + _dot(c_hi, w_lo) + b_ref[0]


def _modulation(cond, w_mod, b_mod):
    tn = 1536
    return pl.pallas_call(
        _mod_kernel,
        out_shape=jax.ShapeDtypeStruct((DEPTH, MOD_ROWS, 6 * D_MODEL), F32),
        grid=(DEPTH, 6 * D_MODEL // tn),
        in_specs=[pl.BlockSpec((MOD_ROWS, D_MODEL), lambda l, j: (0, 0)),
                  pl.BlockSpec((1, D_MODEL, tn), lambda l, j: (l, 0, j)),
                  pl.BlockSpec((1, 1, tn), lambda l, j: (l, 0, j))],
        out_specs=pl.BlockSpec((1, MOD_ROWS, tn), lambda l, j: (l, 0, j)),
        compiler_params=_params(2),
        name="modulation",
    )(cond, w_mod, b_mod.reshape(DEPTH, 1, 6 * D_MODEL))


def _rms_mod(x, g, shift, scale):
    ms = jnp.mean(x * x, axis=-1, keepdims=True)
    y = x * lax.rsqrt(ms + RMS_EPS) * g
    return y * (1.0 + scale) + shift


def _project_in(hm, w_ref, qkv_ref, h_ref, kc_ref, vc_ref):
    h = _dot(hm.astype(BF16), w_ref[0])
    qkv_ref[...] = h[:, :QKV_COLS].astype(BF16)
    h_ref[...] = h[:, QKV_COLS:]

    @pl.when(pl.program_id(0) < P_TILES)
    def _():
        for sq in range(SEQ_PER_TILE):
            rows = slice(sq * SEQ, (sq + 1) * SEQ)
            kc_ref[sq, 0:NA_WIDTH] = h[rows, NA_WIDTH:2 * NA_WIDTH].T
            vc_ref[sq, 0:NA_WIDTH] = h[rows, 2 * NA_WIDTH:3 * NA_WIDTH].T
            if kc_ref.shape[1] > NA_WIDTH:
                kc_ref[sq, NA_WIDTH:] = jnp.zeros((kc_ref.shape[1] - NA_WIDTH, SEQ), F32)
                vc_ref[sq, NA_WIDTH:] = jnp.zeros((vc_ref.shape[1] - NA_WIDTH, SEQ), F32)


def _pick_group(p_ref, s_ref):
    return jnp.where(pl.program_id(0) < P_TILES, p_ref[...], s_ref[...])


def _p_tile(width):
    return pl.BlockSpec((TM, width), lambda i: (jnp.minimum(i, P_TILES - 1), 0))


def _s_tile(width):
    return pl.BlockSpec((TM, width), lambda i: (jnp.maximum(i - P_TILES, 0), 0))


def _inproj_first_kernel(xp_ref, xs_ref, mod_ref, g_ref, w_ref, qkv_ref, h_ref, kc_ref, vc_ref):
    x = _pick_group(xp_ref, xs_ref)
    hm = _rms_mod(x, g_ref[...], mod_ref[0, 0:1, :], mod_ref[0, 1:2, :])
    _project_in(hm, w_ref, qkv_ref, h_ref, kc_ref, vc_ref)


def _combine_experts(yg_refs, gate_ref):
    gates = gate_ref[...]
    lo_acc = hi_acc = None
    for kk in range(TOP_K):
        lo, hi = _unpack_halves(_load_slabs(yg_refs, kk))
        gk = gates[:, kk:kk + 1]
        lo_acc = gk * lo if lo_acc is None else lo_acc + gk * lo
        hi_acc = gk * hi if hi_acc is None else hi_acc + gk * hi
    return jnp.concatenate([lo_acc, hi_acc], axis=1)


def _inproj_next_kernel(x_ref, yga_ref, ygb_ref, gate_ref, pmod_ref, mod_ref, g_ref, w_ref, kc_in, vc_in,
                        qkv_ref, h_ref, xo_ref, kc_ref, vc_ref):
    del kc_in, vc_in
    x = x_ref[...] + pmod_ref[0, 5:6, :] * _combine_experts((yga_ref, ygb_ref), gate_ref)
    xo_ref[...] = x
    hm = _rms_mod(x, g_ref[...], mod_ref[0, 0:1, :], mod_ref[0, 1:2, :])
    _project_in(hm, w_ref, qkv_ref, h_ref, kc_ref, vc_ref)


_TILE_SPEC = pl.BlockSpec((TM, D_MODEL), lambda i: (i, 0))
_MOD_SPEC = pl.BlockSpec((1, MOD_ROWS, D_MODEL), lambda i: (i, 0, 0))
_ROW_SPEC = pl.BlockSpec((1, D_MODEL), lambda i: (0, 0))
_RT_LANES = 128
_RT_ROWS = 2 * TOP_K
_YG_SPEC = pl.BlockSpec((TOP_K, TM, D_SLAB), lambda i: (0, i, 0))
_GATE_SPEC = pl.BlockSpec((TM, _RT_LANES), lambda i: (i, 0))


def _inproj(layer, x, moe, prev_mod, mod, g, w_bf16, caches):
    w_spec = pl.BlockSpec((1, D_MODEL, IN_COLS), lambda i: (layer, 0, 0))
    h_spec = pl.BlockSpec((TM, REST_COLS), lambda i: (i, 0))
    h_shape = jax.ShapeDtypeStruct((T_ALL, REST_COLS), F32)
    q_spec = pl.BlockSpec((TM, QKV_COLS), lambda i: (i, 0))
    q_shape = jax.ShapeDtypeStruct((T_ALL, QKV_COLS), BF16)
    c_spec = pl.BlockSpec((SEQ_PER_TILE, NA_WIDTH, SEQ), lambda i: (jnp.minimum(i, P_TILES - 1), layer, 0))
    c_shape = jax.ShapeDtypeStruct((BATCH, DEPTH * NA_WIDTH, SEQ), F32)
    x_shape = jax.ShapeDtypeStruct((T_ALL, D_MODEL), F32)
    if moe is None:
        c_all = pl.BlockSpec((SEQ_PER_TILE, DEPTH * NA_WIDTH, SEQ), lambda i: (jnp.minimum(i, P_TILES - 1), 0, 0))
        qkv, h, kc, vc = pl.pallas_call(
            _inproj_first_kernel, out_shape=(q_shape, h_shape, c_shape, c_shape), grid=(N_TILES,),
            in_specs=[_p_tile(D_MODEL), _s_tile(D_MODEL), _MOD_SPEC, _ROW_SPEC, w_spec],
            out_specs=(q_spec, h_spec, c_all, c_all),
            compiler_params=_params(), name="inproj_first",
        )(*x, mod, g, w_bf16)
        return qkv, h, x, (kc, vc)
    qkv, h, x, kc, vc = pl.pallas_call(
        _inproj_next_kernel,
        out_shape=(q_shape, h_shape, x_shape, c_shape, c_shape),
        grid=(N_TILES,),
        in_specs=[_TILE_SPEC, _YG_SPEC, _YG_SPEC, _GATE_SPEC, _MOD_SPEC, _MOD_SPEC, _ROW_SPEC, w_spec,
                  pl.BlockSpec(memory_space=pl.ANY), pl.BlockSpec(memory_space=pl.ANY)],
        out_specs=(q_spec, h_spec, _TILE_SPEC, c_spec, c_spec),
        input_output_aliases={8: 3, 9: 4},
        compiler_params=_params(), name="inproj_next",
    )(x, *moe[0], moe[1], prev_mod, mod, g, w_bf16, *caches)
    return qkv, h, x, (kc, vc)


def _pair_mask(hh):
    lane = lax.broadcasted_iota(jnp.int32, (1, 2 * NA_HEAD_DIM), 1)
    return (lane >= hh * NA_HEAD_DIM) & (lane < (hh + 1) * NA_HEAD_DIM)


_ATT_SEQS = 2


def _stack_pair(qp):
    return jnp.concatenate([jnp.where(_pair_mask(hh), qp, jnp.zeros_like(qp)) for hh in range(2)], axis=0)


def _unstack_pair(o2):
    half = o2.shape[0] // 2
    return jnp.where(_pair_mask(0), o2[:half], o2[half:])


def _attn_prompt_kernel(q_ref, k_ref, v_ref, o_ref):
    scale = NA_HEAD_DIM ** -0.5
    for sq in range(_ATT_SEQS):
        rows = slice(sq * SEQ, (sq + 1) * SEQ)
        for p in range(NA_HEADS // 2):
            cols = slice(p * 128, (p + 1) * 128)
            q2 = _stack_pair(q_ref[rows, cols] * scale)
            s = _dot_nt(q2, k_ref[rows, cols])
            e = jnp.exp(s - jnp.max(s, axis=-1, keepdims=True))
            den = jnp.sum(e, axis=-1, keepdims=True)
            o_ref[rows, cols] = _unstack_pair(_dot(e.astype(BF16), v_ref[rows, cols]) / den).astype(o_ref.dtype)


def _attn_prompt(qkv):
    rows = _ATT_SEQS * SEQ
    return pl.pallas_call(
        _attn_prompt_kernel,
        out_shape=jax.ShapeDtypeStruct((T_PROMPT, NA_WIDTH), BF16),
        grid=(BATCH // _ATT_SEQS,),
        in_specs=[pl.BlockSpec((rows, NA_WIDTH), lambda b: (b, 0)),
                  pl.BlockSpec((rows, NA_WIDTH), lambda b: (b, 1)),
                  pl.BlockSpec((rows, NA_WIDTH), lambda b: (b, 2))],
        out_specs=pl.BlockSpec((rows, NA_WIDTH), lambda b: (b, 0)),
        compiler_params=_params(), name="attn_prompt",
    )(qkv, qkv, qkv)


_NA_ROWS = DEC_SEQ // GRID_W
_NA_LOC = NA_KH * GRID_W
_NA_STEP_ROWS = 2


def _na_window_start(r):
    return jnp.clip(r - NA_KH // 2, 0, _NA_ROWS - NA_KH)


def _attn_sample_kernel(q_ref, k_ref, v_ref, ck_ref, cv_ref, *rest):
    bias_refs, o_ref = rest[:_NA_STEP_ROWS], rest[_NA_STEP_ROWS]
    scale = NA_HEAD_DIM ** -0.5
    for p in range(NA_HEADS // 2):
        cols = slice(p * 128, (p + 1) * 128)
        kc = ck_ref[0, :, cols].astype(BF16)
        vc = cv_ref[0, :, cols].astype(BF16)
        for u in range(_NA_STEP_ROWS):
            rows = slice(u * GRID_W, (u + 1) * GRID_W)
            s0 = pl.multiple_of(_na_window_start(pl.program_id(1) * _NA_STEP_ROWS + u) * GRID_W, GRID_W)
            q2 = _stack_pair(q_ref[rows, cols] * scale)
            bias2 = jnp.concatenate([bias_refs[u][0, 0, 2 * p], bias_refs[u][0, 0, 2 * p + 1]], axis=0)
            sl = _dot_nt(q2, k_ref[pl.ds(s0, _NA_LOC), cols]) + bias2
            sc = _dot_nt(q2, kc)
            mx = jnp.maximum(jnp.max(sl, axis=-1, keepdims=True), jnp.max(sc, axis=-1, keepdims=True))
            el = jnp.exp(sl - mx)
            ec = jnp.exp(sc - mx)
            den = jnp.sum(el, axis=-1, keepdims=True) + jnp.sum(ec, axis=-1, keepdims=True)
            o2 = (_dot(el.astype(BF16), v_ref[pl.ds(s0, _NA_LOC), cols]) + _dot(ec.astype(BF16), vc)) / den
            o_ref[rows, cols] = _unstack_pair(o2).astype(o_ref.dtype)


def _attn_sample(layer, qkv, ck, cv, bias):
    q_rows = _NA_STEP_ROWS * GRID_W
    steps = _NA_ROWS // _NA_STEP_ROWS
    q_blk0 = T_PROMPT // q_rows
    kv_row0 = T_PROMPT // DEC_SEQ

    def bias_spec(u):
        def index(b, r2):
            r = r2 * _NA_STEP_ROWS + u
            return (layer, _na_window_start(r) - r + NA_KH - 1, 0, 0, 0)
        return pl.BlockSpec((1, 1, NA_HEADS, GRID_W, _NA_LOC), index)

    return pl.pallas_call(
        _attn_sample_kernel,
        out_shape=jax.ShapeDtypeStruct((T_SAMPLE, NA_WIDTH), BF16),
        grid=(DEC_BATCH, steps),
        in_specs=[pl.BlockSpec((q_rows, NA_WIDTH), lambda b, r2: (q_blk0 + b * steps + r2, 0)),
                  pl.BlockSpec((DEC_SEQ, NA_WIDTH), lambda b, r2: (kv_row0 + b, 1)),
                  pl.BlockSpec((DEC_SEQ, NA_WIDTH), lambda b, r2: (kv_row0 + b, 2)),
                  pl.BlockSpec((1, PAST_LEN, NA_WIDTH), lambda b, r2: (b, 0, 0)),
                  pl.BlockSpec((1, PAST_LEN, NA_WIDTH), lambda b, r2: (b, 0, 0))]
                 + [bias_spec(u) for u in range(_NA_STEP_ROWS)],
        out_specs=pl.BlockSpec((q_rows, NA_WIDTH), lambda b, r2: (b * steps + r2, 0)),
        compiler_params=_params(2), name="attn_sample",
    )(qkv, qkv, qkv, ck, cv, *([bias] * _NA_STEP_ROWS))


_NA_DR = 2 * NA_KH - 1
_NA_DC = 2 * NA_KW - 1


def _na_bias_kernel(rb_ref, o_ref):
    i = pl.program_id(0)
    qc = lax.broadcasted_iota(jnp.int32, (GRID_W, GRID_W), 0)
    kc = lax.broadcasted_iota(jnp.int32, (GRID_W, GRID_W), 1)
    q_start = jnp.clip(qc - NA_KW // 2, 0, GRID_W - NA_KW)
    in_win = (kc >= q_start) & (kc < q_start + NA_KW)
    dc = jnp.clip(kc - qc + NA_KW - 1, 0, _NA_DC - 1)
    picks = [dc == d for d in range(_NA_DC)]
    tiles = []
    for dr in range(_NA_DR):
        acc = jnp.zeros((GRID_W, GRID_W), F32)
        for d in range(_NA_DC):
            acc = jnp.where(picks[d], rb_ref[i, dr * _NA_DC + d], acc)
        tiles.append(jnp.where(in_win, acc, NEG_INF))
    for base in range(NA_KH):
        o_ref[0, base, 0] = jnp.concatenate(tiles[base:base + NA_KH], axis=1)


def _na_bias_tables(rel_bias):
    rb = rel_bias.astype(F32).reshape(DEPTH * NA_HEADS, _NA_DR * _NA_DC)
    return pl.pallas_call(
        _na_bias_kernel,
        out_shape=jax.ShapeDtypeStruct((DEPTH, NA_KH, NA_HEADS, GRID_W, _NA_LOC), F32),
        grid=(DEPTH * NA_HEADS,),
        in_specs=[pl.BlockSpec(memory_space=pltpu.SMEM)],
        out_specs=pl.BlockSpec((1, NA_KH, 1, GRID_W, _NA_LOC),
                               lambda i: (i // NA_HEADS, 0, i % NA_HEADS, 0, 0)),
        compiler_params=_params(), name="na_bias_tables",
    )(rb)


_HG_GROUP = 8


def _hgrn_kernel(*refs, n_tok, has_state, n_alias):
    refs = refs[:8 + 2 * has_state] + refs[8 + 2 * has_state + n_alias:]
    if has_state:
        (q_ref, zf_ref, zb_ref, v_ref, g_ref, lbf_ref, lbb_ref, og_ref, s0f_ref, s0b_ref,
         rec_ref, sf_ref, sb_ref, kf_s, bf_s, kb_s, bb_s, of_s, ob_s, zf_s, zb_s, qsf_s, qsb_s, stf_s, stb_s) = refs
    else:
        (q_ref, zf_ref, zb_ref, v_ref, g_ref, lbf_ref, lbb_ref, og_ref,
         rec_ref, sf_ref, sb_ref, kf_s, bf_s, kb_s, bb_s, of_s, ob_s, zf_s, zb_s, qsf_s, qsb_s, stf_s, stb_s) = refs
        s0f_ref = s0b_ref = None
    C = HG_CHUNK
    W = HG_WIDTH
    n_chunks = n_tok // C
    rr = lax.broadcasted_iota(jnp.int32, (W, W), 0)
    cc = lax.broadcasted_iota(jnp.int32, (W, W), 1)
    log2_c = C.bit_length() - 1
    same_chunk = jnp.right_shift(rr, log2_c) == jnp.right_shift(cc, log2_c)
    tri_prefix = jnp.where(same_chunk & (cc <= rr), 1.0, 0.0).astype(BF16)
    tri_suffix = jnp.where(same_chunk & (cc >= rr), 1.0, 0.0).astype(BF16)
    same_head = jnp.right_shift(rr, 6) == jnp.right_shift(cc, 6)
    head_ones = jnp.where(same_head, 1.0, 0.0).astype(BF16)

    for ti in range(n_tok // W):
        rows = slice(ti * W, (ti + 1) * W)
        for z_ref, lb_ref, k_s, b_s, tri in ((zf_ref, lbf_ref, kf_s, bf_s, tri_prefix),
                                             (zb_ref, lbb_ref, kb_s, bb_s, tri_suffix)):
            z = z_ref[rows, :]
            lb = lb_ref[...]
            e = jnp.exp(-jnp.abs(z))
            big = 1.0 / (1.0 + e)
            small = e * big
            f = lb + (1.0 - lb) * jnp.where(z >= 0.0, big, small)
            logf = jnp.log(jnp.maximum(f, F_FLOOR))
            k_s[rows, :] = (1.0 - lb) * jnp.where(z >= 0.0, small, big)
            hi, mid, lo = _split3(logf)
            b_s[rows, :] = _dot(tri, hi) + _dot(tri, mid) + _dot(tri, lo)

    G = _HG_GROUP
    n_groups = C // G
    srow = lax.broadcasted_iota(jnp.int32, (G, W), 0)
    zf_s[...] = jnp.zeros_like(zf_s)
    zb_s[...] = jnp.zeros_like(zb_s)

    def scan_chunk(ci, k_s, b_s, z_s, ks_s, st_s, o_dir_s, fwd):
        c = ci if fwd else n_chunks - 1 - ci
        base = pl.multiple_of(c * C, C)
        q = q_ref[pl.ds(base, C), :]
        k = k_s[pl.ds(base, C), :]
        b = b_s[pl.ds(base, C), :]
        v = v_ref[pl.ds(base, C), :]
        q_far = {}
        for gs in range(n_groups):
            others = range(gs + 1, n_groups) if fwd else range(gs)
            if not others:
                continue
            rows_s = slice(gs * G, (gs + 1) * G)
            edge = (gs + 1) * G - 1 if fwd else gs * G
            b_edge = b[edge:edge + 1, :]
            ks_s[rows_s, :] = k[rows_s] * jnp.exp(b_edge - b[rows_s])
            for gt in others:
                rows_t = slice(gt * G, (gt + 1) * G)
                q_far[gs, gt] = q[rows_t] * jnp.exp(b[rows_t] - b_edge)
        for sx in range(C):
            gs = sx // G
            rows_g = slice(gs * G, (gs + 1) * G)
            k_row = k_s[pl.ds(base + sx, 1), :]
            b_row = b_s[pl.ds(base + sx, 1), :]
            keep = (srow + gs * G >= sx) if fwd else (srow + gs * G <= sx)
            z_s[sx * C + gs * G:sx * C + (gs + 1) * G, :] = jnp.where(
                keep, (k_row * q[rows_g]) * jnp.exp(b[rows_g] - b_row), 0.0)
            others = range(gs + 1, n_groups) if fwd else range(gs)
            if others:
                ks_row = ks_s[sx:sx + 1, :]
                for gt in others:
                    z_s[sx * C + gt * G:sx * C + (gt + 1) * G, :] = ks_row * q_far[gs, gt]
        a_rep = _dot(z_s[...].astype(BF16), head_ones)
        o_intra = jnp.sum(a_rep.reshape(C, C, W) * v[:, None, :], axis=0)
        b_end = b_s[pl.ds(base + (C - 1 if fwd else 0), 1), :]
        q_in = q * jnp.exp(b)
        k_st = k * jnp.exp(b_end - b)
        st = st_s[...]
        o_inter = _dot_nt(q_in.astype(BF16), st.astype(BF16))
        upd = _dot_tn(v.astype(BF16), k_st.astype(BF16))
        st_s[...] = st * jnp.exp(b_end) + jnp.where(same_head, upd, 0.0)
        o_dir_s[pl.ds(base, C), :] = o_intra + o_inter

    def load_state(s0_ref, st_s):
        if s0_ref is None:
            st_s[...] = jnp.zeros((W, W), F32)
            return
        for hh in range(HG_HEADS):
            parts = [s0_ref[0, hh] if g == hh else jnp.zeros((HG_DK, HG_DV), F32) for g in range(HG_HEADS)]
            st_s[hh * HG_DK:(hh + 1) * HG_DK, :] = jnp.concatenate(parts, axis=1)
        st_s[...] = st_s[...].T

    def store_state(st_s, out_ref):
        by_head = st_s[...].T
        for hh in range(HG_HEADS):
            out_ref[0, hh] = by_head[hh * HG_DK:(hh + 1) * HG_DK, hh * HG_DV:(hh + 1) * HG_DV]
        if out_ref.shape[1] > HG_HEADS:
            out_ref[0, HG_HEADS:] = jnp.zeros((out_ref.shape[1] - HG_HEADS, HG_DK, HG_DV), F32)

    load_state(s0f_ref, stf_s)
    load_state(s0b_ref, stb_s)

    def scan_both(ci, carry):
        scan_chunk(ci, kf_s, bf_s, zf_s, qsf_s, stf_s, of_s, True)
        scan_chunk(ci, kb_s, bb_s, zb_s, qsb_s, stb_s, ob_s, False)
        return carry
    lax.fori_loop(0, n_chunks, scan_both, 0)
    store_state(stf_s, sf_ref)
    store_state(stb_s, sb_ref)

    for ti in range(n_tok // W):
        rows = slice(ti * W, (ti + 1) * W)
        o = of_s[rows, :] + ob_s[rows, :]
        sq = o * o
        sq_hi = sq.astype(BF16)
        sq_lo = (sq - sq_hi.astype(F32)).astype(BF16)
        ms = (_dot(sq_hi, head_ones) + _dot(sq_lo, head_ones)) * (1.0 / HG_DV)
        g = g_ref[rows, :]
        y = o * lax.rsqrt(ms + RMS_EPS) * og_ref[...] * (g * jax.nn.sigmoid(g))
        rec_ref[rows, :] = y.astype(rec_ref.dtype)


def _hgrn(h, lbf, lbb, og, s0f, s0b, n_tok, n_seq, row0, layer=None, states=None):
    W = HG_WIDTH
    has_state = s0f is not None

    def col(cb):
        return pl.BlockSpec((n_tok, W), lambda i, cb=cb: (row0 + i, cb))

    vec = pl.BlockSpec((1, W), lambda i: (0, 0))
    st_spec = pl.BlockSpec((1, HG_HEADS, HG_DK, HG_DV), lambda i: (i, 0, 0, 0))
    in_specs = [col(_CB_HQ), col(_CB_ZF), col(_CB_ZB), col(_CB_HI), col(_CB_HG), vec, vec, vec]
    args = [h, h, h, h, h, lbf, lbb, og]
    if has_state:
        in_specs += [st_spec, st_spec]
        args += [s0f, s0b]
    seq_f32 = pltpu.VMEM((n_tok, W), F32)
    out_st_spec, st_rows, aliases = st_spec, HG_HEADS, {}
    if layer is not None:
        st_rows = DEPTH * HG_HEADS
        if states is None:
            out_st_spec = pl.BlockSpec((1, st_rows, HG_DK, HG_DV), lambda i: (i, 0, 0, 0))
        else:
            out_st_spec = pl.BlockSpec((1, HG_HEADS, HG_DK, HG_DV), lambda i: (i, layer, 0, 0))
            aliases = {len(args): 1, len(args) + 1: 2}
            in_specs += [pl.BlockSpec(memory_space=pl.ANY)] * 2
            args += list(states)
    return pl.pallas_call(
        functools.partial(_hgrn_kernel, n_tok=n_tok, has_state=has_state, n_alias=len(aliases)),
        out_shape=(jax.ShapeDtypeStruct((n_seq * n_tok, W), BF16),
                   jax.ShapeDtypeStruct((n_seq, st_rows, HG_DK, HG_DV), F32),
                   jax.ShapeDtypeStruct((n_seq, st_rows, HG_DK, HG_DV), F32)),
        grid=(n_seq,),
        in_specs=in_specs,
        input_output_aliases=aliases,
        out_specs=(pl.BlockSpec((n_tok, W), lambda i: (i, 0)), out_st_spec, out_st_spec),
        scratch_shapes=[seq_f32, seq_f32, seq_f32, seq_f32, seq_f32, seq_f32,
                        pltpu.VMEM((HG_CHUNK * HG_CHUNK, W), F32),
                        pltpu.VMEM((HG_CHUNK * HG_CHUNK, W), F32),
                        pltpu.VMEM((HG_CHUNK, W), F32),
                        pltpu.VMEM((HG_CHUNK, W), F32),
                        pltpu.VMEM((W, W), F32),
                        pltpu.VMEM((W, W), F32)],
        compiler_params=_params(), name="hgrn_state" if has_state else "hgrn_zero",
    )(*args)


def _gmlp_tile(u_ref, v_ref, g_ref, ws_ref, b_ref):
    lane = lax.broadcasted_iota(jnp.int32, (1, GM_WIDTH), 1)
    outs = []
    for ci in range(TM // GM_CHUNK):
        rows = slice(ci * GM_CHUNK, (ci + 1) * GM_CHUNK)
        v = v_ref[rows, :]
        ms = jnp.mean(v * v, axis=-1, keepdims=True)
        vn = (v * lax.rsqrt(ms + RMS_EPS) * g_ref[...]).astype(BF16)
        z = b_ref[...]
        for gi in range(GM_GROUPS):
            zg = _dot(ws_ref[gi], vn)
            in_group = (lane >= gi * GM_GDIM) & (lane < (gi + 1) * GM_GDIM)
            z = z + jnp.where(in_group, zg, 0.0)
        outs.append((u_ref[rows, :] * z).astype(BF16))
    return jnp.concatenate(outs, axis=0)


def _outproj_kernel(attp_ref, atts_ref, recp_ref, recs_ref, gu_ref, gv_ref, gg_ref, gws_ref, gb_ref,
                    xp_ref, xs_ref, mod_ref, g_ref, w_ref, wr_ref, br_ref,
                    x1_ref, h2a_ref, h2b_ref, rt_ref, gate_ref, cnt_ref):
    @pl.when(pl.program_id(0) == 0)
    def _():
        cnt_ref[...] = jnp.zeros_like(cnt_ref)

    out = (_dot(_pick_group(attp_ref, atts_ref), w_ref[0, 0:NA_WIDTH, :])
           + _dot(_pick_group(recp_ref, recs_ref), w_ref[0, NA_WIDTH:NA_WIDTH + HG_WIDTH, :])
           + _dot(_gmlp_tile(gu_ref, gv_ref, gg_ref, gws_ref, gb_ref), w_ref[0, NA_WIDTH + HG_WIDTH:, :]))
    x1 = _pick_group(xp_ref, xs_ref) + mod_ref[0, 2:3, :] * out
    x1_ref[...] = x1
    h2 = _rms_mod(x1, g_ref[...], mod_ref[0, 3:4, :], mod_ref[0, 4:5, :])
    _store_slabs((h2a_ref, h2b_ref), _pack_halves(h2))
    h_hi = h2.astype(BF16)
    h_lo = (h2 - h_hi.astype(F32)).astype(BF16)
    wr = wr_ref[...]
    w_hi = wr.astype(BF16)
    w_lo = (wr - w_hi.astype(F32)).astype(BF16)
    logits = _dot(h_hi, w_hi) + _dot(h_lo, w_hi) + _dot(h_hi, w_lo) + br_ref[...]
    lane_e = lax.broadcasted_iota(jnp.int32, (TM, N_EXPERTS), 1).astype(F32)
    lane_o = lax.broadcasted_iota(jnp.int32, (TM, _RT_LANES), 1)
    idx_acc = jnp.zeros((TM, _RT_LANES), F32)
    val_acc = jnp.zeros((TM, _RT_LANES), F32)
    top0 = None
    den = jnp.zeros((TM, 1), F32)
    work = logits
    picks = []
    for kk in range(TOP_K):
        m = jnp.max(work, axis=-1, keepdims=True)
        first = jnp.min(jnp.where(work == m, lane_e, float(N_EXPERTS)), axis=-1, keepdims=True)
        if kk == 0:
            top0 = m
        e = jnp.exp(m - top0)
        den = den + e
        idx_acc = jnp.where(lane_o == kk, first, idx_acc)
        val_acc = jnp.where(lane_o == kk, e, val_acc)
        picks.append(lane_e == first)
        work = jnp.where(picks[-1], -jnp.inf, work)
    gate_ref[...] = val_acc / den
    sel = jnp.zeros((TM, N_EXPERTS), F32)
    for pk in picks:
        sel = sel + jnp.where(pk, 1.0, 0.0)
    rr = lax.broadcasted_iota(jnp.int32, (TM, TM), 0)
    cc = lax.broadcasted_iota(jnp.int32, (TM, TM), 1)
    earlier = jnp.where(cc < rr, 1.0, 0.0).astype(BF16)
    seen = cnt_ref[0:1, 0:N_EXPERTS]
    before = _dot(earlier, sel.astype(BF16)) + seen
    for kk, pk in enumerate(picks):
        rank = jnp.sum(jnp.where(pk, before, 0.0), axis=-1, keepdims=True)
        idx_acc = jnp.where(lane_o == TOP_K + kk, rank, idx_acc)
    rt_ref[...] = idx_acc.T[0:_RT_ROWS, :].astype(jnp.int32)
    cnt_ref[0:1, 0:N_EXPERTS] = seen + jnp.sum(sel, axis=0, keepdims=True)


def _outproj(layer, att_p, att_s, rec_p, rec_s, h, gmlp_params, x, mod, g, w_bf16, wr, br):
    def tile(width):
        return pl.BlockSpec((TM, width), lambda i: (i, 0))

    if isinstance(x, tuple):
        x_args, x_specs = x, [_p_tile(D_MODEL), _s_tile(D_MODEL)]
    else:
        x_args = (x, x)
        x_specs = [_p_tile(D_MODEL), pl.BlockSpec((TM, D_MODEL), lambda i: (jnp.maximum(i, P_TILES), 0))]
    return pl.pallas_call(
        _outproj_kernel,
        out_shape=(jax.ShapeDtypeStruct((T_ALL, D_MODEL), F32),
                   jax.ShapeDtypeStruct((T_ALL, D_SLAB), jnp.int32),
                   jax.ShapeDtypeStruct((T_ALL, D_SLAB), jnp.int32),
                   jax.ShapeDtypeStruct((_RT_ROWS, T_ALL), jnp.int32),
                   jax.ShapeDtypeStruct((T_ALL, _RT_LANES), F32),
                   jax.ShapeDtypeStruct((8, _RT_LANES), F32)),
        grid=(N_TILES,),
        in_specs=[_p_tile(NA_WIDTH), _s_tile(NA_WIDTH), _p_tile(HG_WIDTH), _s_tile(HG_WIDTH),
                  pl.BlockSpec((TM, GM_WIDTH), lambda i: (i, _CB_GU)), pl.BlockSpec((TM, GM_WIDTH), lambda i: (i, _CB_GV)),
                  pl.BlockSpec((1, GM_WIDTH), lambda i: (0, 0)),
                  pl.BlockSpec((GM_GROUPS, GM_CHUNK, GM_CHUNK), lambda i: (0, 0, 0)),
                  pl.BlockSpec((GM_CHUNK, GM_WIDTH), lambda i: (0, 0)),
                  *x_specs, _MOD_SPEC, _ROW_SPEC,
                  pl.BlockSpec((1, D_MODEL, D_MODEL), lambda i: (layer, 0, 0)),
                  pl.BlockSpec((D_MODEL, N_EXPERTS), lambda i: (0, 0)),
                  pl.BlockSpec((1, N_EXPERTS), lambda i: (0, 0))],
        out_specs=(_TILE_SPEC, tile(D_SLAB), tile(D_SLAB), pl.BlockSpec((_RT_ROWS, TM), lambda i: (0, i)),
                   tile(_RT_LANES),
                   pl.BlockSpec((8, _RT_LANES), lambda i: (0, 0))),
        compiler_params=_params(), name="outproj_router",
    )(att_p, att_s, rec_p, rec_s, h, h, *gmlp_params, *x_args, mod, g, w_bf16, wr, br)


_W_CHUNKS = 4
_W_CAST_ROWS = 128


def _moe_kernel(blk_e_ref, blk_on_ref, blk_new_ref, blk_next_ref,
                xa_ref, xb_ref, wg_hbm, bg_ref, wu_hbm, bu_ref, wd_hbm, bd_ref,
                ya_ref, yb_ref, w_f32, w_bf16, w_sem, *, layer):
    j = pl.program_id(0)

    def weight_copies(expert):
        rows = D_MODEL // _W_CHUNKS
        return [pltpu.make_async_copy(w_hbm.at[layer, expert, pl.ds(ci * rows, rows)],
                                      w_f32.at[wi, pl.ds(ci * rows, rows)], w_sem.at[wi, ci])
                for wi, w_hbm in enumerate((wg_hbm, wu_hbm, wd_hbm)) for ci in range(_W_CHUNKS)]

    @pl.when(j == 0)
    def _():
        for cp in weight_copies(blk_e_ref[0]):
            cp.start()

    @pl.when(blk_new_ref[j] != 0)
    def _():
        for cp in weight_copies(blk_e_ref[j]):
            cp.wait()

        def cast_rows(ci, carry):
            rows = pl.ds(pl.multiple_of(ci * _W_CAST_ROWS, _W_CAST_ROWS), _W_CAST_ROWS)
            for wi in range(3):
                w_bf16[wi, rows, :] = w_f32[wi, rows, :].astype(BF16)
            return carry
        lax.fori_loop(0, D_MODEL // _W_CAST_ROWS, cast_rows, 0)

        @pl.when(blk_next_ref[j] >= 0)
        def _():
            for cp in weight_copies(blk_next_ref[j]):
                cp.start()

    @pl.when(blk_on_ref[j] != 0)
    def _():
        lo, hi = _unpack_halves(_load_slabs((xa_ref, xb_ref)))
        x = jnp.concatenate([lo.astype(BF16), hi.astype(BF16)], axis=1)
        gate = jnp.minimum(_dot(x, w_bf16[0]) + bg_ref[0, 0], SWIGLU_LIMIT)
        up = jnp.clip(_dot(x, w_bf16[1]) + bu_ref[0, 0], -SWIGLU_LIMIT, SWIGLU_LIMIT)
        glu = gate * jax.nn.sigmoid(SWIGLU_ALPHA * gate)
        act = ((up + 1.0) * glu).astype(BF16)
        _store_slabs((ya_ref, yb_ref), _pack_halves(_dot(act, w_bf16[2]) + bd_ref[0, 0]))

    @pl.when(blk_on_ref[j] == 0)
    def _():
        ya_ref[...] = jnp.zeros_like(ya_ref)
        yb_ref[...] = jnp.zeros_like(yb_ref)


def _moe(layer, plan, x_sorted, wg, bg, wu, bu, wd, bd):
    n_plan = len(plan)
    b_spec = pl.BlockSpec((1, 1, 1, D_MODEL), lambda j, be, *_: (layer, be[j], 0, 0))
    x_spec = pl.BlockSpec((MOE_BM, D_SLAB), lambda j, *_: (j, 0))
    hbm = pl.BlockSpec(memory_space=pl.ANY)
    bias4 = lambda b: b.reshape(DEPTH, N_EXPERTS, 1, D_MODEL)
    return pl.pallas_call(
        functools.partial(_moe_kernel, layer=layer),
        out_shape=(jax.ShapeDtypeStruct((MOE_SLOTS, D_SLAB), jnp.int32),) * N_SPLIT,
        grid_spec=pltpu.PrefetchScalarGridSpec(
            num_scalar_prefetch=n_plan, grid=(MOE_BLOCKS,),
            in_specs=[x_spec, x_spec, hbm, b_spec, hbm, b_spec, hbm, b_spec],
            out_specs=(x_spec, x_spec),
            scratch_shapes=[pltpu.VMEM((3, D_MODEL, D_MODEL), F32), pltpu.VMEM((3, D_MODEL, D_MODEL), BF16),
                            pltpu.SemaphoreType.DMA((3, _W_CHUNKS))]),
        compiler_params=_params(), name="moe_experts",
    )(*plan, *x_sorted, wg, bias4(bg), wu, bias4(bu), wd, bias4(bd))


def _route(rt, counts):
    experts = jnp.arange(N_EXPERTS, dtype=jnp.int32)
    nblk = (counts + MOE_BM - 1) // MOE_BM
    blk_end = jnp.cumsum(nblk)
    row0 = (blk_end - nblk) * MOE_BM
    top_i, rank = rt[:TOP_K], rt[TOP_K:]
    start_of = jnp.sum(jnp.where(top_i[None] == experts[:, None, None], row0[:, None, None], 0), axis=0)
    dest = (start_of + rank).reshape(1, TOP_K * T_ALL)
    live = counts > 0
    last_live = jnp.max(jnp.where(live, experts, 0))
    later_live = live[None, :] & (experts[None, :] > experts[:, None])
    next_live = jnp.min(jnp.where(later_live, experts[None, :], N_EXPERTS), axis=1)
    next_live = jnp.where(next_live == N_EXPERTS, -1, next_live)
    blk = jnp.arange(MOE_BLOCKS, dtype=jnp.int32)
    blk_on = blk < blk_end[-1]
    blk_e = jnp.where(blk_on, jnp.minimum(jnp.sum((blk_end[None, :] <= blk[:, None]).astype(jnp.int32), axis=1),
                                          N_EXPERTS - 1), last_live)
    blk_new = blk_on & jnp.concatenate([jnp.ones((1,), bool), blk_e[1:] != blk_e[:-1]])
    is_e = blk_e[:, None] == experts[None, :]
    lookup = lambda table: jnp.sum(jnp.where(is_e, table[None, :], 0), axis=1)
    plan = (blk_e, blk_on, blk_new, lookup(next_live))
    return dest.astype(jnp.int32), tuple(p.astype(jnp.int32) for p in plan)


_SC_WINDOW = 128


def _sc_mesh():
    return plsc.VectorSubcoreMesh(core_axis_name="core", subcore_axis_name="subcore")


def _sc_scatter_rows(srcs, idx, n_out):
    n_src, width = srcs[0].shape
    n_rep = idx.shape[1] // n_src
    src_windows = n_src // _SC_WINDOW
    assert len(srcs) == 2

    def body(*refs):
        x_hbm = refs[:len(srcs)]
        i_hbm = refs[len(srcs)]
        o_hbm = refs[len(srcs) + 1:]

        def run(xs, os_):
            def step(x_vmem, *i_vmem):
                for iv in i_vmem:
                    pltpu.sync_copy(x_vmem, os_.at[iv.at[0]])

            pltpu.emit_pipeline(
                step, grid=(src_windows,),
                in_specs=[pl.BlockSpec((_SC_WINDOW, width), lambda i: (i, 0))]
                         + [pl.BlockSpec((1, _SC_WINDOW), lambda i, kk=kk: (0, kk * src_windows + i))
                            for kk in range(n_rep)],
                out_specs=[],
                core_axis_name="subcore",
                dimension_semantics=(pltpu.PARALLEL,),
            )(xs, *([i_hbm] * n_rep))

        for ci, (xs, os_) in enumerate(zip(x_hbm, o_hbm)):
            pl.when(lax.axis_index("core") == ci)(functools.partial(run, xs, os_))

    out_type = tuple(jax.ShapeDtypeStruct((n_out, width), s.dtype) for s in srcs)
    return pl.kernel(body, out_type=out_type, mesh=_sc_mesh(), scratch_types=[],
                     name="sc_scatter_rows")(*srcs, idx)


def _sc_gather_rows(tables, idx):
    n_idx = idx.shape[1]
    width = tables[0].shape[1]

    def body(*refs):
        t_hbm = refs[:len(tables)]
        i_hbm = refs[len(tables)]
        o_hbm = refs[len(tables) + 1:]
        for ts, os_ in zip(t_hbm, o_hbm):
            def step(i_vmem, o_vmem, ts=ts):
                pltpu.sync_copy(ts.at[i_vmem.at[0]], o_vmem)

            pltpu.emit_pipeline(
                step, grid=(n_idx // _SC_WINDOW,),
                in_specs=[pl.BlockSpec((1, _SC_WINDOW), lambda i: (0, i))],
                out_specs=[pl.BlockSpec((_SC_WINDOW, width), lambda i: (i, 0))],
                core_axis_name=("core", "subcore"),
                dimension_semantics=(pltpu.PARALLEL,),
            )(i_hbm, os_)

    out_type = tuple(jax.ShapeDtypeStruct((n_idx, width), t.dtype) for t in tables)
    return pl.kernel(body, out_type=out_type, mesh=_sc_mesh(), scratch_types=[],
                     name="sc_gather_rows")(*tables, idx)


def _final_kernel(x_ref, yga_ref, ygb_ref, gate_ref, mod_ref, g_ref, yp_ref, ys_ref):
    x = x_ref[...] + mod_ref[0, 5:6, :] * _combine_experts((yga_ref, ygb_ref), gate_ref)
    ms = jnp.mean(x * x, axis=-1, keepdims=True)
    y = x * lax.rsqrt(ms + RMS_EPS) * g_ref[...]

    @pl.when(pl.program_id(0) < P_TILES)
    def _():
        yp_ref[...] = y

    @pl.when(pl.program_id(0) >= P_TILES)
    def _():
        ys_ref[...] = y


def _final(x, moe, mod, g):
    return pl.pallas_call(
        _final_kernel,
        out_shape=(jax.ShapeDtypeStruct((T_PROMPT, D_MODEL), F32), jax.ShapeDtypeStruct((T_SAMPLE, D_MODEL), F32)),
        grid=(N_TILES,),
        in_specs=[_TILE_SPEC, _YG_SPEC, _YG_SPEC, _GATE_SPEC, _MOD_SPEC, _ROW_SPEC],
        out_specs=(_p_tile(D_MODEL), _s_tile(D_MODEL)),
        compiler_params=_params(), name="final_norm",
    )(x, *moe[0], moe[1], mod, g)


def kernel(x_prompt, x_sample, cache_k, cache_v, state_hgrn_fwd, state_hgrn_bwd, c, c_ctx, w_mod, b_mod, norm1_g, norm2_g, w_in, na_rel_bias, hgrn_lb, hgrn_onorm_g, gmlp_vnorm_g, gmlp_ws, gmlp_b, w_out, router_w, router_b, w_gate, b_gate, w_up, b_up, w_down, b_down, final_g):
    x = (x_prompt.reshape(T_PROMPT, D_MODEL), x_sample.reshape(T_SAMPLE, D_MODEL))

    cond = jnp.zeros((MOD_ROWS, D_MODEL), F32).at[0].set(c_ctx).at[1:1 + DEC_BATCH].set(c)
    mod = _modulation(cond, w_mod, b_mod)
    tile_row = np.concatenate([np.zeros(P_TILES, np.int32),
                               1 + np.arange(N_TILES - P_TILES, dtype=np.int32) // (DEC_SEQ // TM)])
    mod_tiles = mod[:, tile_row].reshape(DEPTH, N_TILES, 6, D_MODEL)
    mod_tiles = jnp.pad(mod_tiles, ((0, 0), (0, 0), (0, MOD_ROWS - 6), (0, 0)))

    lb_soft = jax.nn.softmax(hgrn_lb.astype(F32), axis=1)
    lower = jnp.cumsum(lb_soft, axis=1) - lb_soft[:, :1]

    na_bias = _na_bias_tables(na_rel_bias)
    w_in_bf16 = w_in.astype(BF16)
    w_out_bf16 = w_out.astype(BF16)

    moe_out = caches = states = None
    for l in range(DEPTH):
        qkv, h, x_next, caches = _inproj(l, x, moe_out, mod_tiles[l - 1] if l else None, mod_tiles[l],
                                    norm1_g[l][None, :], w_in_bf16, caches)

        att_p = _attn_prompt(qkv)
        att_s = _attn_sample(l, qkv, cache_k[:, l].reshape(DEC_BATCH, PAST_LEN, NA_WIDTH),
                             cache_v[:, l].reshape(DEC_BATCH, PAST_LEN, NA_WIDTH), na_bias)
        lbf = lower[0, l][None, :]
        lbb = lower[1, l][None, :]
        og = jnp.tile(hgrn_onorm_g[l], HG_HEADS)[None, :]
        rec_p, *states = _hgrn(h, lbf, lbb, og, None, None, SEQ, BATCH, 0, layer=l, states=states)
        rec_s, _, _ = _hgrn(h, lbf, lbb, og, state_hgrn_fwd[:, l].astype(F32), state_hgrn_bwd[:, l].astype(F32),
                            DEC_SEQ, DEC_BATCH, T_PROMPT // DEC_SEQ)
        gm_bias = jnp.repeat(gmlp_b[l].T, GM_GDIM, axis=1)
        gmlp_params = (gmlp_vnorm_g[l][None, :], gmlp_ws[l].astype(BF16), gm_bias)

        x, h2a, h2b, rt, gate_pad, cnt = _outproj(l, att_p, att_s, rec_p, rec_s, h, gmlp_params, x_next,
                                                  mod_tiles[l],
                                                  norm2_g[l][None, :], w_out_bf16,
                                                  router_w[l], router_b[l][None, :])
        dest_flat, plan = _route(rt, cnt[0, :N_EXPERTS].astype(jnp.int32))
        x_sorted = _sc_scatter_rows((h2a, h2b), dest_flat, MOE_SLOTS)
        y_sorted = _moe(l, plan, x_sorted, w_gate, b_gate, w_up, b_up, w_down, b_down)
        y_tok = _sc_gather_rows(y_sorted, dest_flat)
        moe_out = ([yt.reshape(TOP_K, T_ALL, D_SLAB) for yt in y_tok], gate_pad)

    y_prompt, y_sample = _final(x, moe_out, mod_tiles[DEPTH - 1], final_g[None, :])
    y_prompt = y_prompt.reshape(BATCH, SEQ, D_MODEL)
    y_sample = y_sample.reshape(DEC_BATCH, DEC_SEQ, D_MODEL)
    new_k, new_v = (cache.reshape(BATCH, DEPTH, NA_HEADS, NA_HEAD_DIM, SEQ).transpose(0, 1, 4, 2, 3)
                    for cache in caches)
    new_sf, new_sb = (st.reshape(BATCH, DEPTH, HG_HEADS, HG_DK, HG_DV) for st in states)
    return (y_prompt, y_sample, new_k, new_v, new_sf, new_sb)
```

```python
import functools

import numpy as np
import jax
import jax.numpy as jnp
from jax import lax
from jax.experimental import pallas as pl
from jax.experimental.pallas import tpu as pltpu
from jax.experimental.pallas import tpu_sc as plsc

F32 = jnp.float32
BF16 = jnp.bfloat16

D_MODEL = 1024
BATCH = 32
SEQ = 256
DEPTH = 2
DEC_BATCH = 2
DEC_SEQ = 1024
PAST_LEN = 512
GRID_W = 64
NA_HEADS = 8
NA_HEAD_DIM = 64
NA_WIDTH = NA_HEADS * NA_HEAD_DIM
NA_KH = 8
NA_KW = 16
HG_HEADS = 4
HG_DK = 64
HG_DV = 64
HG_WIDTH = HG_HEADS * HG_DV
HG_CHUNK = 16
F_FLOOR = 1e-30
GM_GROUPS = 4
GM_GDIM = 64
GM_WIDTH = GM_GROUPS * GM_GDIM
GM_CHUNK = 128
IN_COLS = 3 * NA_WIDTH + 5 * HG_WIDTH + 2 * GM_WIDTH
N_EXPERTS = 32
TOP_K = 4
SWIGLU_LIMIT = 7.0
SWIGLU_ALPHA = 1.702
RMS_EPS = 1e-6
NEG_INF = -1e30

T_PROMPT = BATCH * SEQ
T_SAMPLE = DEC_BATCH * DEC_SEQ
T_ALL = T_PROMPT + T_SAMPLE
TM = 512
SEQ_PER_TILE = TM // SEQ
N_TILES = T_ALL // TM
P_TILES = T_PROMPT // TM
MOE_BM = 256
MOE_SLOTS = -(-(T_ALL * TOP_K + N_EXPERTS * (MOE_BM - 1)) // MOE_BM) * MOE_BM
MOE_BLOCKS = MOE_SLOTS // MOE_BM
MOD_ROWS = 8
V7X_VMEM_LIMIT = 48 * 1024 * 1024

QKV_COLS = 3 * NA_WIDTH
REST_COLS = IN_COLS - QKV_COLS
_CB_HQ, _CB_ZF, _CB_ZB, _CB_HI, _CB_HG, _CB_GU, _CB_GV = range(7)


def _dot(a, b):
    return jnp.dot(a, b, preferred_element_type=F32)


def _dot_nt(a, b):
    return lax.dot_general(a, b, (((1,), (1,)), ((), ())), preferred_element_type=F32)


def _dot_tn(a, b):
    return lax.dot_general(a, b, (((0,), (0,)), ((), ())), preferred_element_type=F32)


def _split3(x):
    hi = x.astype(BF16)
    r1 = x - hi.astype(F32)
    mid = r1.astype(BF16)
    lo = (r1 - mid.astype(F32)).astype(BF16)
    return hi, mid, lo


D_PACK = D_MODEL // 2
N_SPLIT = 2
D_SLAB = D_PACK // N_SPLIT


def _pack_halves(x):
    half = x.shape[1] // 2
    lo = pltpu.bitcast(x[:, :half].astype(BF16).astype(F32), jnp.uint32)
    hi = pltpu.bitcast(x[:, half:].astype(BF16).astype(F32), jnp.uint32)
    return pltpu.bitcast(jnp.right_shift(lo, jnp.uint32(16)) | hi, jnp.int32)


def _unpack_halves(w):
    u = pltpu.bitcast(w, jnp.uint32)
    lo = pltpu.bitcast(jnp.left_shift(u, jnp.uint32(16)), F32)
    hi = pltpu.bitcast(u & jnp.uint32(0xFFFF0000), F32)
    return lo, hi


def _load_slabs(refs, *lead):
    return jnp.concatenate([r[lead] if lead else r[...] for r in refs], axis=1)


def _store_slabs(refs, packed):
    for si, r in enumerate(refs):
        r[...] = packed[:, si * D_SLAB:(si + 1) * D_SLAB]


def _params(n_axes=1):
    return pltpu.CompilerParams(dimension_semantics=("arbitrary",) * n_axes,
                                vmem_limit_bytes=V7X_VMEM_LIMIT)


def _mod_kernel(cond_ref, w_ref, b_ref, o_ref):
    c = cond_ref[...]
    c = c * jax.nn.sigmoid(c)
    w = w_ref[0]
    c_hi = c.astype(BF16)
    c_lo = (c - c_hi.astype(F32)).astype(BF16)
    w_hi = w.astype(BF16)
    w_lo = (w - w_hi.astype(F32)).astype(BF16)
    o_ref[0] = _dot(c_hi, w_hi) + _dot(c_lo, w_hi) + _dot(c_hi, w_lo) + b_ref[0]


def _modulation(cond, w_mod, b_mod):
    tn = 1536
    return pl.pallas_call(
        _mod_kernel,
        out_shape=jax.ShapeDtypeStruct((DEPTH, MOD_ROWS, 6 * D_MODEL), F32),
        grid=(DEPTH, 6 * D_MODEL // tn),
        in_specs=[pl.BlockSpec((MOD_ROWS, D_MODEL), lambda l, j: (0, 0)),
                  pl.BlockSpec((1, D_MODEL, tn), lambda l, j: (l, 0, j)),
                  pl.BlockSpec((1, 1, tn), lambda l, j: (l, 0, j))],
        out_specs=pl.BlockSpec((1, MOD_ROWS, tn), lambda l, j: (l, 0, j)),
        compiler_params=_params(2),
        name="modulation",
    )(cond, w_mod, b_mod.reshape(DEPTH, 1, 6 * D_MODEL))


def _rms_mod(x, g, shift, scale):
    ms = jnp.mean(x * x, axis=-1, keepdims=True)
    y = x * lax.rsqrt(ms + RMS_EPS) * g
    return y * (1.0 + scale) + shift


def _project_in(hm, w_ref, qkv_ref, h_ref, kc_ref, vc_ref):
    h = _dot(hm.astype(BF16), w_ref[0])
    qkv_ref[...] = h[:, :QKV_COLS].astype(BF16)
    h_ref[...] = h[:, QKV_COLS:]

    @pl.when(pl.program_id(0) < P_TILES)
    def _():
        for sq in range(SEQ_PER_TILE):
            rows = slice(sq * SEQ, (sq + 1) * SEQ)
            kc_ref[sq, 0:NA_WIDTH] = h[rows, NA_WIDTH:2 * NA_WIDTH].T
            vc_ref[sq, 0:NA_WIDTH] = h[rows, 2 * NA_WIDTH:3 * NA_WIDTH].T
            if kc_ref.shape[1] > NA_WIDTH:
                kc_ref[sq, NA_WIDTH:] = jnp.zeros((kc_ref.shape[1] - NA_WIDTH, SEQ), F32)
                vc_ref[sq, NA_WIDTH:] = jnp.zeros((vc_ref.shape[1] - NA_WIDTH, SEQ), F32)


def _pick_group(p_ref, s_ref):
    return jnp.where(pl.program_id(0) < P_TILES, p_ref[...], s_ref[...])


def _p_tile(width):
    return pl.BlockSpec((TM, width), lambda i: (jnp.minimum(i, P_TILES - 1), 0))


def _s_tile(width):
    return pl.BlockSpec((TM, width), lambda i: (jnp.maximum(i - P_TILES, 0), 0))


def _inproj_first_kernel(xp_ref, xs_ref, mod_ref, g_ref, w_ref, qkv_ref, h_ref, kc_ref, vc_ref):
    x = _pick_group(xp_ref, xs_ref)
    hm = _rms_mod(x, g_ref[...], mod_ref[0, 0:1, :], mod_ref[0, 1:2, :])
    _project_in(hm, w_ref, qkv_ref, h_ref, kc_ref, vc_ref)


def _combine_experts(yg_refs, gate_ref):
    gates = gate_ref[...]
    lo_acc = hi_acc = None
    for kk in range(TOP_K):
        lo, hi = _unpack_halves(_load_slabs(yg_refs, kk))
        gk = gates[:, kk:kk + 1]
        lo_acc = gk * lo if lo_acc is None else lo_acc + gk * lo
        hi_acc = gk * hi if hi_acc is None else hi_acc + gk * hi
    return jnp.concatenate([lo_acc, hi_acc], axis=1)


def _inproj_next_kernel(x_ref, yga_ref, ygb_ref, gate_ref, pmod_ref, mod_ref, g_ref, w_ref, kc_in, vc_in,
                        qkv_ref, h_ref, xo_ref, kc_ref, vc_ref):
    del kc_in, vc_in
    x = x_ref[...] + pmod_ref[0, 5:6, :] * _combine_experts((yga_ref, ygb_ref), gate_ref)
    xo_ref[...] = x
    hm = _rms_mod(x, g_ref[...], mod_ref[0, 0:1, :], mod_ref[0, 1:2, :])
    _project_in(hm, w_ref, qkv_ref, h_ref, kc_ref, vc_ref)


_TILE_SPEC = pl.BlockSpec((TM, D_MODEL), lambda i: (i, 0))
_MOD_SPEC = pl.BlockSpec((1, MOD_ROWS, D_MODEL), lambda i: (i, 0, 0))
_ROW_SPEC = pl.BlockSpec((1, D_MODEL), lambda i: (0, 0))
_RT_LANES = 128
_RT_ROWS = 2 * TOP_K
_YG_SPEC = pl.BlockSpec((TOP_K, TM, D_SLAB), lambda i: (0, i, 0))
_GATE_SPEC = pl.BlockSpec((TM, _RT_LANES), lambda i: (i, 0))


def _inproj(layer, x, moe, prev_mod, mod, g, w_bf16, caches):
    w_spec = pl.BlockSpec((1, D_MODEL, IN_COLS), lambda i: (layer, 0, 0))
    h_spec = pl.BlockSpec((TM, REST_COLS), lambda i: (i, 0))
    h_shape = jax.ShapeDtypeStruct((T_ALL, REST_COLS), F32)
    q_spec = pl.BlockSpec((TM, QKV_COLS), lambda i: (i, 0))
    q_shape = jax.ShapeDtypeStruct((T_ALL, QKV_COLS), BF16)
    c_spec = pl.BlockSpec((SEQ_PER_TILE, NA_WIDTH, SEQ), lambda i: (jnp.minimum(i, P_TILES - 1), layer, 0))
    c_shape = jax.ShapeDtypeStruct((BATCH, DEPTH * NA_WIDTH, SEQ), F32)
    x_shape = jax.ShapeDtypeStruct((T_ALL, D_MODEL), F32)
    if moe is None:
        c_all = pl.BlockSpec((SEQ_PER_TILE, DEPTH * NA_WIDTH, SEQ), lambda i: (jnp.minimum(i, P_TILES - 1), 0, 0))
        qkv, h, kc, vc = pl.pallas_call(
            _inproj_first_kernel, out_shape=(q_shape, h_shape, c_shape, c_shape), grid=(N_TILES,),
            in_specs=[_p_tile(D_MODEL), _s_tile(D_MODEL), _MOD_SPEC, _ROW_SPEC, w_spec],
            out_specs=(q_spec, h_spec, c_all, c_all),
            compiler_params=_params(), name="inproj_first",
        )(*x, mod, g, w_bf16)
        return qkv, h, x, (kc, vc)
    qkv, h, x, kc, vc = pl.pallas_call(
        _inproj_next_kernel,
        out_shape=(q_shape, h_shape, x_shape, c_shape, c_shape),
        grid=(N_TILES,),
        in_specs=[_TILE_SPEC, _YG_SPEC, _YG_SPEC, _GATE_SPEC, _MOD_SPEC, _MOD_SPEC, _ROW_SPEC, w_spec,
                  pl.BlockSpec(memory_space=pl.ANY), pl.BlockSpec(memory_space=pl.ANY)],
        out_specs=(q_spec, h_spec, _TILE_SPEC, c_spec, c_spec),
        input_output_aliases={8: 3, 9: 4},
        compiler_params=_params(), name="inproj_next",
    )(x, *moe[0], moe[1], prev_mod, mod, g, w_bf16, *caches)
    return qkv, h, x, (kc, vc)


def _pair_mask(hh):
    lane = lax.broadcasted_iota(jnp.int32, (1, 2 * NA_HEAD_DIM), 1)
    return (lane >= hh * NA_HEAD_DIM) & (lane < (hh + 1) * NA_HEAD_DIM)


_ATT_SEQS = 2


def _stack_pair(qp):
    return jnp.concatenate([jnp.where(_pair_mask(hh), qp, jnp.zeros_like(qp)) for hh in range(2)], axis=0)


def _unstack_pair(o2):
    half = o2.shape[0] // 2
    return jnp.where(_pair_mask(0), o2[:half], o2[half:])


def _attn_prompt_kernel(q_ref, k_ref, v_ref, o_ref):
    scale = NA_HEAD_DIM ** -0.5
    for sq in range(_ATT_SEQS):
        rows = slice(sq * SEQ, (sq + 1) * SEQ)
        for p in range(NA_HEADS // 2):
            cols = slice(p * 128, (p + 1) * 128)
            q2 = _stack_pair(q_ref[rows, cols] * scale)
            s = _dot_nt(q2, k_ref[rows, cols])
            e = jnp.exp(s - jnp.max(s, axis=-1, keepdims=True))
            den = jnp.sum(e, axis=-1, keepdims=True)
            o_ref[rows, cols] = _unstack_pair(_dot(e.astype(BF16), v_ref[rows, cols]) / den).astype(o_ref.dtype)


def _attn_prompt(qkv):
    rows = _ATT_SEQS * SEQ
    return pl.pallas_call(
        _attn_prompt_kernel,
        out_shape=jax.ShapeDtypeStruct((T_PROMPT, NA_WIDTH), BF16),
        grid=(BATCH // _ATT_SEQS,),
        in_specs=[pl.BlockSpec((rows, NA_WIDTH), lambda b: (b, 0)),
                  pl.BlockSpec((rows, NA_WIDTH), lambda b: (b, 1)),
                  pl.BlockSpec((rows, NA_WIDTH), lambda b: (b, 2))],
        out_specs=pl.BlockSpec((rows, NA_WIDTH), lambda b: (b, 0)),
        compiler_params=_params(), name="attn_prompt",
    )(qkv, qkv, qkv)


_NA_ROWS = DEC_SEQ // GRID_W
_NA_LOC = NA_KH * GRID_W
_NA_STEP_ROWS = 2


def _na_window_start(r):
    return jnp.clip(r - NA_KH // 2, 0, _NA_ROWS - NA_KH)


def _attn_sample_kernel(q_ref, k_ref, v_ref, ck_ref, cv_ref, *rest):
    bias_refs, o_ref = rest[:_NA_STEP_ROWS], rest[_NA_STEP_ROWS]
    scale = NA_HEAD_DIM ** -0.5
    for p in range(NA_HEADS // 2):
        cols = slice(p * 128, (p + 1) * 128)
        kc = ck_ref[0, :, cols].astype(BF16)
        vc = cv_ref[0, :, cols].astype(BF16)
        for u in range(_NA_STEP_ROWS):
            rows = slice(u * GRID_W, (u + 1) * GRID_W)
            s0 = pl.multiple_of(_na_window_start(pl.program_id(1) * _NA_STEP_ROWS + u) * GRID_W, GRID_W)
            q2 = _stack_pair(q_ref[rows, cols] * scale)
            bias2 = jnp.concatenate([bias_refs[u][0, 0, 2 * p], bias_refs[u][0, 0, 2 * p + 1]], axis=0)
            sl = _dot_nt(q2, k_ref[pl.ds(s0, _NA_LOC), cols]) + bias2
            sc = _dot_nt(q2, kc)
            mx = jnp.maximum(jnp.max(sl, axis=-1, keepdims=True), jnp.max(sc, axis=-1, keepdims=True))
            el = jnp.exp(sl - mx)
            ec = jnp.exp(sc - mx)
            den = jnp.sum(el, axis=-1, keepdims=True) + jnp.sum(ec, axis=-1, keepdims=True)
            o2 = (_dot(el.astype(BF16), v_ref[pl.ds(s0, _NA_LOC), cols]) + _dot(ec.astype(BF16), vc)) / den
            o_ref[rows, cols] = _unstack_pair(o2).astype(o_ref.dtype)


def _attn_sample(layer, qkv, ck, cv, bias):
    q_rows = _NA_STEP_ROWS * GRID_W
    steps = _NA_ROWS // _NA_STEP_ROWS
    q_blk0 = T_PROMPT // q_rows
    kv_row0 = T_PROMPT // DEC_SEQ

    def bias_spec(u):
        def index(b, r2):
            r = r2 * _NA_STEP_ROWS + u
            return (layer, _na_window_start(r) - r + NA_KH - 1, 0, 0, 0)
        return pl.BlockSpec((1, 1, NA_HEADS, GRID_W, _NA_LOC), index)

    return pl.pallas_call(
        _attn_sample_kernel,
        out_shape=jax.ShapeDtypeStruct((T_SAMPLE, NA_WIDTH), BF16),
        grid=(DEC_BATCH, steps),
        in_specs=[pl.BlockSpec((q_rows, NA_WIDTH), lambda b, r2: (q_blk0 + b * steps + r2, 0)),
                  pl.BlockSpec((DEC_SEQ, NA_WIDTH), lambda b, r2: (kv_row0 + b, 1)),
                  pl.BlockSpec((DEC_SEQ, NA_WIDTH), lambda b, r2: (kv_row0 + b, 2)),
                  pl.BlockSpec((1, PAST_LEN, NA_WIDTH), lambda b, r2: (b, 0, 0)),
                  pl.BlockSpec((1, PAST_LEN, NA_WIDTH), lambda b, r2: (b, 0, 0))]
                 + [bias_spec(u) for u in range(_NA_STEP_ROWS)],
        out_specs=pl.BlockSpec((q_rows, NA_WIDTH), lambda b, r2: (b * steps + r2, 0)),
        compiler_params=_params(2), name="attn_sample",
    )(qkv, qkv, qkv, ck, cv, *([bias] * _NA_STEP_ROWS))


_NA_DR = 2 * NA_KH - 1
_NA_DC = 2 * NA_KW - 1


def _na_bias_kernel(rb_ref, o_ref):
    i = pl.program_id(0)
    qc = lax.broadcasted_iota(jnp.int32, (GRID_W, GRID_W), 0)
    kc = lax.broadcasted_iota(jnp.int32, (GRID_W, GRID_W), 1)
    q_start = jnp.clip(qc - NA_KW // 2, 0, GRID_W - NA_KW)
    in_win = (kc >= q_start) & (kc < q_start + NA_KW)
    dc = jnp.clip(kc - qc + NA_KW - 1, 0, _NA_DC - 1)
    picks = [dc == d for d in range(_NA_DC)]
    tiles = []
    for dr in range(_NA_DR):
        acc = jnp.zeros((GRID_W, GRID_W), F32)
        for d in range(_NA_DC):
            acc = jnp.where(picks[d], rb_ref[i, dr * _NA_DC + d], acc)
        tiles.append(jnp.where(in_win, acc, NEG_INF))
    for base in range(NA_KH):
        o_ref[0, base, 0] = jnp.concatenate(tiles[base:base + NA_KH], axis=1)


def _na_bias_tables(rel_bias):
    rb = rel_bias.astype(F32).reshape(DEPTH * NA_HEADS, _NA_DR * _NA_DC)
    return pl.pallas_call(
        _na_bias_kernel,
        out_shape=jax.ShapeDtypeStruct((DEPTH, NA_KH, NA_HEADS, GRID_W, _NA_LOC), F32),
        grid=(DEPTH * NA_HEADS,),
        in_specs=[pl.BlockSpec(memory_space=pltpu.SMEM)],
        out_specs=pl.BlockSpec((1, NA_KH, 1, GRID_W, _NA_LOC),
                               lambda i: (i // NA_HEADS, 0, i % NA_HEADS, 0, 0)),
        compiler_params=_params(), name="na_bias_tables",
    )(rb)


_HG_GROUP = 8


def _hgrn_kernel(*refs, n_tok, has_state, n_alias):
    refs = refs[:8 + 2 * has_state] + refs[8 + 2 * has_state + n_alias:]
    if has_state:
        (q_ref, zf_ref, zb_ref, v_ref, g_ref, lbf_ref, lbb_ref, og_ref, s0f_ref, s0b_ref,
         rec_ref, sf_ref, sb_ref, kf_s, bf_s, kb_s, bb_s, of_s, ob_s, zf_s, zb_s, qsf_s, qsb_s, stf_s, stb_s) = refs
    else:
        (q_ref, zf_ref, zb_ref, v_ref, g_ref, lbf_ref, lbb_ref, og_ref,
         rec_ref, sf_ref, sb_ref, kf_s, bf_s, kb_s, bb_s, of_s, ob_s, zf_s, zb_s, qsf_s, qsb_s, stf_s, stb_s) = refs
        s0f_ref = s0b_ref = None
    C = HG_CHUNK
    W = HG_WIDTH
    n_chunks = n_tok // C
    rr = lax.broadcasted_iota(jnp.int32, (W, W), 0)
    cc = lax.broadcasted_iota(jnp.int32, (W, W), 1)
    log2_c = C.bit_length() - 1
    same_chunk = jnp.right_shift(rr, log2_c) == jnp.right_shift(cc, log2_c)
    tri_prefix = jnp.where(same_chunk & (cc <= rr), 1.0, 0.0).astype(BF16)
    tri_suffix = jnp.where(same_chunk & (cc >= rr), 1.0, 0.0).astype(BF16)
    same_head = jnp.right_shift(rr, 6) == jnp.right_shift(cc, 6)
    head_ones = jnp.where(same_head, 1.0, 0.0).astype(BF16)

    for ti in range(n_tok // W):
        rows = slice(ti * W, (ti + 1) * W)
        for z_ref, lb_ref, k_s, b_s, tri in ((zf_ref, lbf_ref, kf_s, bf_s, tri_prefix),
                                             (zb_ref, lbb_ref, kb_s, bb_s, tri_suffix)):
            z = z_ref[rows, :]
            lb = lb_ref[...]
            e = jnp.exp(-jnp.abs(z))
            big = 1.0 / (1.0 + e)
            small = e * big
            f = lb + (1.0 - lb) * jnp.where(z >= 0.0, big, small)
            logf = jnp.log(jnp.maximum(f, F_FLOOR))
            k_s[rows, :] = (1.0 - lb) * jnp.where(z >= 0.0, small, big)
            hi, mid, lo = _split3(logf)
            b_s[rows, :] = _dot(tri, hi) + _dot(tri, mid) + _dot(tri, lo)

    G = _HG_GROUP
    n_groups = C // G
    srow = lax.broadcasted_iota(jnp.int32, (G, W), 0)
    zf_s[...] = jnp.zeros_like(zf_s)
    zb_s[...] = jnp.zeros_like(zb_s)

    def scan_chunk(ci, k_s, b_s, z_s, ks_s, st_s, o_dir_s, fwd):
        c = ci if fwd else n_chunks - 1 - ci
        base = pl.multiple_of(c * C, C)
        q = q_ref[pl.ds(base, C), :]
        k = k_s[pl.ds(base, C), :]
        b = b_s[pl.ds(base, C), :]
        v = v_ref[pl.ds(base, C), :]
        q_far = {}
        for gs in range(n_groups):
            others = range(gs + 1, n_groups) if fwd else range(gs)
            if not others:
                continue
            rows_s = slice(gs * G, (gs + 1) * G)
            edge = (gs + 1) * G - 1 if fwd else gs * G
            b_edge = b[edge:edge + 1, :]
            ks_s[rows_s, :] = k[rows_s] * jnp.exp(b_edge - b[rows_s])
            for gt in others:
                rows_t = slice(gt * G, (gt + 1) * G)
                q_far[gs, gt] = q[rows_t] * jnp.exp(b[rows_t] - b_edge)
        for sx in range(C):
            gs = sx // G
            rows_g = slice(gs * G, (gs + 1) * G)
            k_row = k_s[pl.ds(base + sx, 1), :]
            b_row = b_s[pl.ds(base + sx, 1), :]
            keep = (srow + gs * G >= sx) if fwd else (srow + gs * G <= sx)
            z_s[sx * C + gs * G:sx * C + (gs + 1) * G, :] = jnp.where(
                keep, (k_row * q[rows_g]) * jnp.exp(b[rows_g] - b_row), 0.0)
            others = range(gs + 1, n_groups) if fwd else range(gs)
            if others:
                ks_row = ks_s[sx:sx + 1, :]
                for gt in others:
                    z_s[sx * C + gt * G:sx * C + (gt + 1) * G, :] = ks_row * q_far[gs, gt]
        a_rep = _dot(z_s[...].astype(BF16), head_ones)
        o_intra = jnp.sum(a_rep.reshape(C, C, W) * v[:, None, :], axis=0)
        b_end = b_s[pl.ds(base + (C - 1 if fwd else 0), 1), :]
        q_in = q * jnp.exp(b)
        k_st = k * jnp.exp(b_end - b)
        st = st_s[...]
        o_inter = _dot_nt(q_in.astype(BF16), st.astype(BF16))
        upd = _dot_tn(v.astype(BF16), k_st.astype(BF16))
        st_s[...] = st * jnp.exp(b_end) + jnp.where(same_head, upd, 0.0)
        o_dir_s[pl.ds(base, C), :] = o_intra + o_inter

    def load_state(s0_ref, st_s):
        if s0_ref is None:
            st_s[...] = jnp.zeros((W, W), F32)
            return
        for hh in range(HG_HEADS):
            parts = [s0_ref[0, hh] if g == hh else jnp.zeros((HG_DK, HG_DV), F32) for g in range(HG_HEADS)]
            st_s[hh * HG_DK:(hh + 1) * HG_DK, :] = jnp.concatenate(parts, axis=1)
        st_s[...] = st_s[...].T

    def store_state(st_s, out_ref):
        by_head = st_s[...].T
        for hh in range(HG_HEADS):
            out_ref[0, hh] = by_head[hh * HG_DK:(hh + 1) * HG_DK, hh * HG_DV:(hh + 1) * HG_DV]
        if out_ref.shape[1] > HG_HEADS:
            out_ref[0, HG_HEADS:] = jnp.zeros((out_ref.shape[1] - HG_HEADS, HG_DK, HG_DV), F32)

    load_state(s0f_ref, stf_s)
    load_state(s0b_ref, stb_s)

    def scan_both(ci, carry):
        scan_chunk(ci, kf_s, bf_s, zf_s, qsf_s, stf_s, of_s, True)
        scan_chunk(ci, kb_s, bb_s, zb_s, qsb_s, stb_s, ob_s, False)
        return carry
    lax.fori_loop(0, n_chunks, scan_both, 0)
    store_state(stf_s, sf_ref)
    store_state(stb_s, sb_ref)

    for ti in range(n_tok // W):
        rows = slice(ti * W, (ti + 1) * W)
        o = of_s[rows, :] + ob_s[rows, :]
        sq = o * o
        sq_hi = sq.astype(BF16)
        sq_lo = (sq - sq_hi.astype(F32)).astype(BF16)
        ms = (_dot(sq_hi, head_ones) + _dot(sq_lo, head_ones)) * (1.0 / HG_DV)
        g = g_ref[rows, :]
        y = o * lax.rsqrt(ms + RMS_EPS) * og_ref[...] * (g * jax.nn.sigmoid(g))
        rec_ref[rows, :] = y.astype(rec_ref.dtype)


def _hgrn(h, lbf, lbb, og, s0f, s0b, n_tok, n_seq, row0, layer=None, states=None):
    W = HG_WIDTH
    has_state = s0f is not None

    def col(cb):
        return pl.BlockSpec((n_tok, W), lambda i, cb=cb: (row0 + i, cb))

    vec = pl.BlockSpec((1, W), lambda i: (0, 0))
    st_spec = pl.BlockSpec((1, HG_HEADS, HG_DK, HG_DV), lambda i: (i, 0, 0, 0))
    in_specs = [col(_CB_HQ), col(_CB_ZF), col(_CB_ZB), col(_CB_HI), col(_CB_HG), vec, vec, vec]
    args = [h, h, h, h, h, lbf, lbb, og]
    if has_state:
        in_specs += [st_spec, st_spec]
        args += [s0f, s0b]
    seq_f32 = pltpu.VMEM((n_tok, W), F32)
    out_st_spec, st_rows, aliases = st_spec, HG_HEADS, {}
    if layer is not None:
        st_rows = DEPTH * HG_HEADS
        if states is None:
            out_st_spec = pl.BlockSpec((1, st_rows, HG_DK, HG_DV), lambda i: (i, 0, 0, 0))
        else:
            out_st_spec = pl.BlockSpec((1, HG_HEADS, HG_DK, HG_DV), lambda i: (i, layer, 0, 0))
            aliases = {len(args): 1, len(args) + 1: 2}
            in_specs += [pl.BlockSpec(memory_space=pl.ANY)] * 2
            args += list(states)
    return pl.pallas_call(
        functools.partial(_hgrn_kernel, n_tok=n_tok, has_state=has_state, n_alias=len(aliases)),
        out_shape=(jax.ShapeDtypeStruct((n_seq * n_tok, W), BF16),
                   jax.ShapeDtypeStruct((n_seq, st_rows, HG_DK, HG_DV), F32),
                   jax.ShapeDtypeStruct((n_seq, st_rows, HG_DK, HG_DV), F32)),
        grid=(n_seq,),
        in_specs=in_specs,
        input_output_aliases=aliases,
        out_specs=(pl.BlockSpec((n_tok, W), lambda i: (i, 0)), out_st_spec, out_st_spec),
        scratch_shapes=[seq_f32, seq_f32, seq_f32, seq_f32, seq_f32, seq_f32,
                        pltpu.VMEM((HG_CHUNK * HG_CHUNK, W), F32),
                        pltpu.VMEM((HG_CHUNK * HG_CHUNK, W), F32),
                        pltpu.VMEM((HG_CHUNK, W), F32),
                        pltpu.VMEM((HG_CHUNK, W), F32),
                        pltpu.VMEM((W, W), F32),
                        pltpu.VMEM((W, W), F32)],
        compiler_params=_params(), name="hgrn_state" if has_state else "hgrn_zero",
    )(*args)


def _gmlp_tile(u_ref, v_ref, g_ref, ws_ref, b_ref):
    lane = lax.broadcasted_iota(jnp.int32, (1, GM_WIDTH), 1)
    outs = []
    for ci in range(TM // GM_CHUNK):
        rows = slice(ci * GM_CHUNK, (ci + 1) * GM_CHUNK)
        v = v_ref[rows, :]
        ms = jnp.mean(v * v, axis=-1, keepdims=True)
        vn = (v * lax.rsqrt(ms + RMS_EPS) * g_ref[...]).astype(BF16)
        z = b_ref[...]
        for gi in range(GM_GROUPS):
            zg = _dot(ws_ref[gi], vn)
            in_group = (lane >= gi * GM_GDIM) & (lane < (gi + 1) * GM_GDIM)
            z = z + jnp.where(in_group, zg, 0.0)
        outs.append((u_ref[rows, :] * z).astype(BF16))
    return jnp.concatenate(outs, axis=0)


def _outproj_kernel(attp_ref, atts_ref, recp_ref, recs_ref, gu_ref, gv_ref, gg_ref, gws_ref, gb_ref,
                    xp_ref, xs_ref, mod_ref, g_ref, w_ref, wr_ref, br_ref,
                    x1_ref, h2a_ref, h2b_ref, rt_ref, gate_ref, cnt_ref):
    @pl.when(pl.program_id(0) == 0)
    def _():
        cnt_ref[...] = jnp.zeros_like(cnt_ref)

    out = (_dot(_pick_group(attp_ref, atts_ref), w_ref[0, 0:NA_WIDTH, :])
           + _dot(_pick_group(recp_ref, recs_ref), w_ref[0, NA_WIDTH:NA_WIDTH + HG_WIDTH, :])
           + _dot(_gmlp_tile(gu_ref, gv_ref, gg_ref, gws_ref, gb_ref), w_ref[0, NA_WIDTH + HG_WIDTH:, :]))
    x1 = _pick_group(xp_ref, xs_ref) + mod_ref[0, 2:3, :] * out
    x1_ref[...] = x1
    h2 = _rms_mod(x1, g_ref[...], mod_ref[0, 3:4, :], mod_ref[0, 4:5, :])
    _store_slabs((h2a_ref, h2b_ref), _pack_halves(h2))
    h_hi = h2.astype(BF16)
    h_lo = (h2 - h_hi.astype(F32)).astype(BF16)
    wr = wr_ref[...]
    w_hi = wr.astype(BF16)
    w_lo = (wr - w_hi.astype(F32)).astype(BF16)
    logits = _dot(h_hi, w_hi) + _dot(h_lo, w_hi) + _dot(h_hi, w_lo) + br_ref[...]
    lane_e = lax.broadcasted_iota(jnp.int32, (TM, N_EXPERTS), 1).astype(F32)
    lane_o = lax.broadcasted_iota(jnp.int32, (TM, _RT_LANES), 1)
    idx_acc = jnp.zeros((TM, _RT_LANES), F32)
    val_acc = jnp.zeros((TM, _RT_LANES), F32)
    top0 = None
    den = jnp.zeros((TM, 1), F32)
    work = logits
    picks = []
    for kk in range(TOP_K):
        m = jnp.max(work, axis=-1, keepdims=True)
        first = jnp.min(jnp.where(work == m, lane_e, float(N_EXPERTS)), axis=-1, keepdims=True)
        if kk == 0:
            top0 = m
        e = jnp.exp(m - top0)
        den = den + e
        idx_acc = jnp.where(lane_o == kk, first, idx_acc)
        val_acc = jnp.where(lane_o == kk, e, val_acc)
        picks.append(lane_e == first)
        work = jnp.where(picks[-1], -jnp.inf, work)
    gate_ref[...] = val_acc / den
    sel = jnp.zeros((TM, N_EXPERTS), F32)
    for pk in picks:
        sel = sel + jnp.where(pk, 1.0, 0.0)
    rr = lax.broadcasted_iota(jnp.int32, (TM, TM), 0)
    cc = lax.broadcasted_iota(jnp.int32, (TM, TM), 1)
    earlier = jnp.where(cc < rr, 1.0, 0.0).astype(BF16)
    seen = cnt_ref[0:1, 0:N_EXPERTS]
    before = _dot(earlier, sel.astype(BF16)) + seen
    for kk, pk in enumerate(picks):
        rank = jnp.sum(jnp.where(pk, before, 0.0), axis=-1, keepdims=True)
        idx_acc = jnp.where(lane_o == TOP_K + kk, rank, idx_acc)
    rt_ref[...] = idx_acc.T[0:_RT_ROWS, :].astype(jnp.int32)
    cnt_ref[0:1, 0:N_EXPERTS] = seen + jnp.sum(sel, axis=0, keepdims=True)


def _outproj(layer, att_p, att_s, rec_p, rec_s, h, gmlp_params, x, mod, g, w_bf16, wr, br):
    def tile(width):
        return pl.BlockSpec((TM, width), lambda i: (i, 0))

    if isinstance(x, tuple):
        x_args, x_specs = x, [_p_tile(D_MODEL), _s_tile(D_MODEL)]
    else:
        x_args = (x, x)
        x_specs = [_p_tile(D_MODEL), pl.BlockSpec((TM, D_MODEL), lambda i: (jnp.maximum(i, P_TILES), 0))]
    return pl.pallas_call(
        _outproj_kernel,
        out_shape=(jax.ShapeDtypeStruct((T_ALL, D_MODEL), F32),
                   jax.ShapeDtypeStruct((T_ALL, D_SLAB), jnp.int32),
                   jax.ShapeDtypeStruct((T_ALL, D_SLAB), jnp.int32),
                   jax.ShapeDtypeStruct((_RT_ROWS, T_ALL), jnp.int32),
                   jax.ShapeDtypeStruct((T_ALL, _RT_LANES), F32),
                   jax.ShapeDtypeStruct((8, _RT_LANES), F32)),
        grid=(N_TILES,),
        in_specs=[_p_tile(NA_WIDTH), _s_tile(NA_WIDTH), _p_tile(HG_WIDTH), _s_tile(HG_WIDTH),
                  pl.BlockSpec((TM, GM_WIDTH), lambda i: (i, _CB_GU)), pl.BlockSpec((TM, GM_WIDTH), lambda i: (i, _CB_GV)),
                  pl.BlockSpec((1, GM_WIDTH), lambda i: (0, 0)),
                  pl.BlockSpec((GM_GROUPS, GM_CHUNK, GM_CHUNK), lambda i: (0, 0, 0)),
                  pl.BlockSpec((GM_CHUNK, GM_WIDTH), lambda i: (0, 0)),
                  *x_specs, _MOD_SPEC, _ROW_SPEC,
                  pl.BlockSpec((1, D_MODEL, D_MODEL), lambda i: (layer, 0, 0)),
                  pl.BlockSpec((D_MODEL, N_EXPERTS), lambda i: (0, 0)),
                  pl.BlockSpec((1, N_EXPERTS), lambda i: (0, 0))],
        out_specs=(_TILE_SPEC, tile(D_SLAB), tile(D_SLAB), pl.BlockSpec((_RT_ROWS, TM), lambda i: (0, i)),
                   tile(_RT_LANES),
                   pl.BlockSpec((8, _RT_LANES), lambda i: (0, 0))),
        compiler_params=_params(), name="outproj_router",
    )(att_p, att_s, rec_p, rec_s, h, h, *gmlp_params, *x_args, mod, g, w_bf16, wr, br)


_W_CHUNKS = 4
_W_CAST_ROWS = 128
_W_DMA_PRIORITY = 1


def _moe_kernel(blk_e_ref, blk_on_ref, blk_new_ref, blk_next_ref,
                xa_ref, xb_ref, wg_hbm, bg_ref, wu_hbm, bu_ref, wd_hbm, bd_ref,
                ya_ref, yb_ref, w_f32, w_bf16, w_sem, *, layer):
    j = pl.program_id(0)

    def weight_copies(expert):
        rows = D_MODEL // _W_CHUNKS
        return [pltpu.make_async_copy(w_hbm.at[layer, expert, pl.ds(ci * rows, rows)],
                                      w_f32.at[wi, pl.ds(ci * rows, rows)], w_sem.at[wi, ci])
                for wi, w_hbm in enumerate((wg_hbm, wu_hbm, wd_hbm)) for ci in range(_W_CHUNKS)]

    @pl.when(j == 0)
    def _():
        for cp in weight_copies(blk_e_ref[0]):
            cp.start(priority=_W_DMA_PRIORITY)

    @pl.when(blk_new_ref[j] != 0)
    def _():
        for cp in weight_copies(blk_e_ref[j]):
            cp.wait()

        def cast_rows(ci, carry):
            rows = pl.ds(pl.multiple_of(ci * _W_CAST_ROWS, _W_CAST_ROWS), _W_CAST_ROWS)
            for wi in range(3):
                w_bf16[wi, rows, :] = w_f32[wi, rows, :].astype(BF16)
            return carry
        lax.fori_loop(0, D_MODEL // _W_CAST_ROWS, cast_rows, 0)

        @pl.when(blk_next_ref[j] >= 0)
        def _():
            for cp in weight_copies(blk_next_ref[j]):
                cp.start(priority=_W_DMA_PRIORITY)

    @pl.when(blk_on_ref[j] != 0)
    def _():
        lo, hi = _unpack_halves(_load_slabs((xa_ref, xb_ref)))
        x = jnp.concatenate([lo.astype(BF16), hi.astype(BF16)], axis=1)
        gate = jnp.minimum(_dot(x, w_bf16[0]) + bg_ref[0, 0], SWIGLU_LIMIT)
        up = jnp.clip(_dot(x, w_bf16[1]) + bu_ref[0, 0], -SWIGLU_LIMIT, SWIGLU_LIMIT)
        glu = gate * jax.nn.sigmoid(SWIGLU_ALPHA * gate)
        act = ((up + 1.0) * glu).astype(BF16)
        _store_slabs((ya_ref, yb_ref), _pack_halves(_dot(act, w_bf16[2]) + bd_ref[0, 0]))

    @pl.when(blk_on_ref[j] == 0)
    def _():
        ya_ref[...] = jnp.zeros_like(ya_ref)
        yb_ref[...] = jnp.zeros_like(yb_ref)


def _moe(layer, plan, x_sorted, wg, bg, wu, bu, wd, bd):
    n_plan = len(plan)
    b_spec = pl.BlockSpec((1, 1, 1, D_MODEL), lambda j, be, *_: (layer, be[j], 0, 0))
    x_spec = pl.BlockSpec((MOE_BM, D_SLAB), lambda j, *_: (j, 0))
    hbm = pl.BlockSpec(memory_space=pl.ANY)
    bias4 = lambda b: b.reshape(DEPTH, N_EXPERTS, 1, D_MODEL)
    return pl.pallas_call(
        functools.partial(_moe_kernel, layer=layer),
        out_shape=(jax.ShapeDtypeStruct((MOE_SLOTS, D_SLAB), jnp.int32),) * N_SPLIT,
        grid_spec=pltpu.PrefetchScalarGridSpec(
            num_scalar_prefetch=n_plan, grid=(MOE_BLOCKS,),
            in_specs=[x_spec, x_spec, hbm, b_spec, hbm, b_spec, hbm, b_spec],
            out_specs=(x_spec, x_spec),
            scratch_shapes=[pltpu.VMEM((3, D_MODEL, D_MODEL), F32), pltpu.VMEM((3, D_MODEL, D_MODEL), BF16),
                            pltpu.SemaphoreType.DMA((3, _W_CHUNKS))]),
        compiler_params=_params(), name="moe_experts",
    )(*plan, *x_sorted, wg, bias4(bg), wu, bias4(bu), wd, bias4(bd))


def _route(rt, counts):
    experts = jnp.arange(N_EXPERTS, dtype=jnp.int32)
    nblk = (counts + MOE_BM - 1) // MOE_BM
    blk_end = jnp.cumsum(nblk)
    row0 = (blk_end - nblk) * MOE_BM
    top_i, rank = rt[:TOP_K], rt[TOP_K:]
    start_of = jnp.sum(jnp.where(top_i[None] == experts[:, None, None], row0[:, None, None], 0), axis=0)
    dest = (start_of + rank).reshape(1, TOP_K * T_ALL)
    live = counts > 0
    last_live = jnp.max(jnp.where(live, experts, 0))
    later_live = live[None, :] & (experts[None, :] > experts[:, None])
    next_live = jnp.min(jnp.where(later_live, experts[None, :], N_EXPERTS), axis=1)
    next_live = jnp.where(next_live == N_EXPERTS, -1, next_live)
    blk = jnp.arange(MOE_BLOCKS, dtype=jnp.int32)
    blk_on = blk < blk_end[-1]
    blk_e = jnp.where(blk_on, jnp.minimum(jnp.sum((blk_end[None, :] <= blk[:, None]).astype(jnp.int32), axis=1),
                                          N_EXPERTS - 1), last_live)
    blk_new = blk_on & jnp.concatenate([jnp.ones((1,), bool), blk_e[1:] != blk_e[:-1]])
    is_e = blk_e[:, None] == experts[None, :]
    lookup = lambda table: jnp.sum(jnp.where(is_e, table[None, :], 0), axis=1)
    plan = (blk_e, blk_on, blk_new, lookup(next_live))
    return dest.astype(jnp.int32), tuple(p.astype(jnp.int32) for p in plan)


_SC_WINDOW = 128


def _sc_mesh():
    return plsc.VectorSubcoreMesh(core_axis_name="core", subcore_axis_name="subcore")


def _sc_scatter_rows(srcs, idx, n_out):
    n_src, width = srcs[0].shape
    n_rep = idx.shape[1] // n_src
    src_windows = n_src // _SC_WINDOW
    assert len(srcs) == 2

    def body(*refs):
        x_hbm = refs[:len(srcs)]
        i_hbm = refs[len(srcs)]
        o_hbm = refs[len(srcs) + 1:]

        def run(xs, os_):
            def step(x_vmem, *i_vmem):
                for iv in i_vmem:
                    pltpu.sync_copy(x_vmem, os_.at[iv.at[0]])

            pltpu.emit_pipeline(
                step, grid=(src_windows,),
                in_specs=[pl.BlockSpec((_SC_WINDOW, width), lambda i: (i, 0))]
                         + [pl.BlockSpec((1, _SC_WINDOW), lambda i, kk=kk: (0, kk * src_windows + i))
                            for kk in range(n_rep)],
                out_specs=[],
                core_axis_name="subcore",
                dimension_semantics=(pltpu.PARALLEL,),
            )(xs, *([i_hbm] * n_rep))

        for ci, (xs, os_) in enumerate(zip(x_hbm, o_hbm)):
            pl.when(lax.axis_index("core") == ci)(functools.partial(run, xs, os_))

    out_type = tuple(jax.ShapeDtypeStruct((n_out, width), s.dtype) for s in srcs)
    return pl.kernel(body, out_type=out_type, mesh=_sc_mesh(), scratch_types=[],
                     name="sc_scatter_rows")(*srcs, idx)


def _sc_gather_rows(tables, idx):
    n_idx = idx.shape[1]
    width = tables[0].shape[1]

    def body(*refs):
        t_hbm = refs[:len(tables)]
        i_hbm = refs[len(tables)]
        o_hbm = refs[len(tables) + 1:]
        for ts, os_ in zip(t_hbm, o_hbm):
            def step(i_vmem, o_vmem, ts=ts):
                pltpu.sync_copy(ts.at[i_vmem.at[0]], o_vmem)

            pltpu.emit_pipeline(
                step, grid=(n_idx // _SC_WINDOW,),
                in_specs=[pl.BlockSpec((1, _SC_WINDOW), lambda i: (0, i))],
                out_specs=[pl.BlockSpec((_SC_WINDOW, width), lambda i: (i, 0))],
                core_axis_name=("core", "subcore"),
                dimension_semantics=(pltpu.PARALLEL,),
            )(i_hbm, os_)

    out_type = tuple(jax.ShapeDtypeStruct((n_idx, width), t.dtype) for t in tables)
    return pl.kernel(body, out_type=out_type, mesh=_sc_mesh(), scratch_types=[],
                     name="sc_gather_rows")(*tables, idx)


def _final_kernel(x_ref, yga_ref, ygb_ref, gate_ref, mod_ref, g_ref, yp_ref, ys_ref):
    x = x_ref[...] + mod_ref[0, 5:6, :] * _combine_experts((yga_ref, ygb_ref), gate_ref)
    ms = jnp.mean(x * x, axis=-1, keepdims=True)
    y = x * lax.rsqrt(ms + RMS_EPS) * g_ref[...]

    @pl.when(pl.program_id(0) < P_TILES)
    def _():
        yp_ref[...] = y

    @pl.when(pl.program_id(0) >= P_TILES)
    def _():
        ys_ref[...] = y


def _final(x, moe, mod, g):
    return pl.pallas_call(
        _final_kernel,
        out_shape=(jax.ShapeDtypeStruct((T_PROMPT, D_MODEL), F32), jax.ShapeDtypeStruct((T_SAMPLE, D_MODEL), F32)),
        grid=(N_TILES,),
        in_specs=[_TILE_SPEC, _YG_SPEC, _YG_SPEC, _GATE_SPEC, _MOD_SPEC, _ROW_SPEC],
        out_specs=(_p_tile(D_MODEL), _s_tile(D_MODEL)),
        compiler_params=_params(), name="final_norm",
    )(x, *moe[0], moe[1], mod, g)


def kernel(x_prompt, x_sample, cache_k, cache_v, state_hgrn_fwd, state_hgrn_bwd, c, c_ctx, w_mod, b_mod, norm1_g, norm2_g, w_in, na_rel_bias, hgrn_lb, hgrn_onorm_g, gmlp_vnorm_g, gmlp_ws, gmlp_b, w_out, router_w, router_b, w_gate, b_gate, w_up, b_up, w_down, b_down, final_g):
    x = (x_prompt.reshape(T_PROMPT, D_MODEL), x_sample.reshape(T_SAMPLE, D_MODEL))

    cond = jnp.zeros((MOD_ROWS, D_MODEL), F32).at[0].set(c_ctx).at[1:1 + DEC_BATCH].set(c)
    mod = _modulation(cond, w_mod, b_mod)
    tile_row = np.concatenate([np.zeros(P_TILES, np.int32),
                               1 + np.arange(N_TILES - P_TILES, dtype=np.int32) // (DEC_SEQ // TM)])
    mod_tiles = mod[:, tile_row].reshape(DEPTH, N_TILES, 6, D_MODEL)
    mod_tiles = jnp.pad(mod_tiles, ((0, 0), (0, 0), (0, MOD_ROWS - 6), (0, 0)))

    lb_soft = jax.nn.softmax(hgrn_lb.astype(F32), axis=1)
    lower = jnp.cumsum(lb_soft, axis=1) - lb_soft[:, :1]

    na_bias = _na_bias_tables(na_rel_bias)
    w_in_bf16 = w_in.astype(BF16)
    w_out_bf16 = w_out.astype(BF16)

    moe_out = caches = states = None
    for l in range(DEPTH):
        qkv, h, x_next, caches = _inproj(l, x, moe_out, mod_tiles[l - 1] if l else None, mod_tiles[l],
                                    norm1_g[l][None, :], w_in_bf16, caches)

        att_p = _attn_prompt(qkv)
        att_s = _attn_sample(l, qkv, cache_k[:, l].reshape(DEC_BATCH, PAST_LEN, NA_WIDTH),
                             cache_v[:, l].reshape(DEC_BATCH, PAST_LEN, NA_WIDTH), na_bias)
        lbf = lower[0, l][None, :]
        lbb = lower[1, l][None, :]
        og = jnp.tile(hgrn_onorm_g[l], HG_HEADS)[None, :]
        rec_p, *states = _hgrn(h, lbf, lbb, og, None, None, SEQ, BATCH, 0, layer=l, states=states)
        rec_s, _, _ = _hgrn(h, lbf, lbb, og, state_hgrn_fwd[:, l].astype(F32), state_hgrn_bwd[:, l].astype(F32),
                            DEC_SEQ, DEC_BATCH, T_PROMPT // DEC_SEQ)
        gm_bias = jnp.repeat(gmlp_b[l].T, GM_GDIM, axis=1)
        gmlp_params = (gmlp_vnorm_g[l][None, :], gmlp_ws[l].astype(BF16), gm_bias)

        x, h2a, h2b, rt, gate_pad, cnt = _outproj(l, att_p, att_s, rec_p, rec_s, h, gmlp_params, x_next,
                                                  mod_tiles[l],
                                                  norm2_g[l][None, :], w_out_bf16,
                                                  router_w[l], router_b[l][None, :])
        dest_flat, plan = _route(rt, cnt[0, :N_EXPERTS].astype(jnp.int32))
        x_sorted = _sc_scatter_rows((h2a, h2b), dest_flat, MOE_SLOTS)
        y_sorted = _moe(l, plan, x_sorted, w_gate, b_gate, w_up, b_up, w_down, b_down)
        y_tok = _sc_gather_rows(y_sorted, dest_flat)
        moe_out = ([yt.reshape(TOP_K, T_ALL, D_SLAB) for yt in y_tok], gate_pad)

    y_prompt, y_sample = _final(x, moe_out, mod_tiles[DEPTH - 1], final_g[None, :])
    y_prompt = y_prompt.reshape(BATCH, SEQ, D_MODEL)
    y_sample = y_sample.reshape(DEC_BATCH, DEC_SEQ, D_MODEL)
    new_k, new_v = (cache.reshape(BATCH, DEPTH, NA_HEADS, NA_HEAD_DIM, SEQ).transpose(0, 1, 4, 2, 3)
                    for cache in caches)
    new_sf, new_sb = (st.reshape(BATCH, DEPTH, HG_HEADS, HG_DK, HG_DV) for st in states)
    return (y_prompt, y_sample, new_k, new_v, new_sf, new_sb)
```

```python
import functools

import numpy as np
import jax
import jax.numpy as jnp
from jax import lax
from jax.experimental import pallas as pl
from jax.experimental.pallas import tpu as pltpu
from jax.experimental.pallas import tpu_sc as plsc

F32 = jnp.float32
BF16 = jnp.bfloat16

D_MODEL = 1024
BATCH = 32
SEQ = 256
DEPTH = 2
DEC_BATCH = 2
DEC_SEQ = 1024
PAST_LEN = 512
GRID_W = 64
NA_HEADS = 8
NA_HEAD_DIM = 64
NA_WIDTH = NA_HEADS * NA_HEAD_DIM
NA_KH = 8
NA_KW = 16
HG_HEADS = 4
HG_DK = 64
HG_DV = 64
HG_WIDTH = HG_HEADS * HG_DV
HG_CHUNK = 16
F_FLOOR = 1e-30
GM_GROUPS = 4
GM_GDIM = 64
GM_WIDTH = GM_GROUPS * GM_GDIM
GM_CHUNK = 128
IN_COLS = 3 * NA_WIDTH + 5 * HG_WIDTH + 2 * GM_WIDTH
N_EXPERTS = 32
TOP_K = 4
SWIGLU_LIMIT = 7.0
SWIGLU_ALPHA = 1.702
RMS_EPS = 1e-6
NEG_INF = -1e30

T_PROMPT = BATCH * SEQ
T_SAMPLE = DEC_BATCH * DEC_SEQ
T_ALL = T_PROMPT + T_SAMPLE
TM = 512
SEQ_PER_TILE = TM // SEQ
N_TILES = T_ALL // TM
P_TILES = T_PROMPT // TM
MOE_BM = 512
MOE_SLOTS = -(-(T_ALL * TOP_K + N_EXPERTS * (MOE_BM - 1)) // MOE_BM) * MOE_BM
MOE_BLOCKS = MOE_SLOTS // MOE_BM
MOD_ROWS = 8
V7X_VMEM_LIMIT = 48 * 1024 * 1024

QKV_COLS = 3 * NA_WIDTH
REST_COLS = IN_COLS - QKV_COLS
_CB_HQ, _CB_ZF, _CB_ZB, _CB_HI, _CB_HG, _CB_GU, _CB_GV = range(7)


def _dot(a, b):
    return jnp.dot(a, b, preferred_element_type=F32)


def _dot_nt(a, b):
    return lax.dot_general(a, b, (((1,), (1,)), ((), ())), preferred_element_type=F32)


def _dot_tn(a, b):
    return lax.dot_general(a, b, (((0,), (0,)), ((), ())), preferred_element_type=F32)


def _split3(x):
    hi = x.astype(BF16)
    r1 = x - hi.astype(F32)
    mid = r1.astype(BF16)
    lo = (r1 - mid.astype(F32)).astype(BF16)
    return hi, mid, lo


D_PACK = D_MODEL // 2
N_SPLIT = 2
D_SLAB = D_PACK // N_SPLIT


def _pack_halves(x):
    half = x.shape[1] // 2
    lo = pltpu.bitcast(x[:, :half].astype(BF16).astype(F32), jnp.uint32)
    hi = pltpu.bitcast(x[:, half:].astype(BF16).astype(F32), jnp.uint32)
    return pltpu.bitcast(jnp.right_shift(lo, jnp.uint32(16)) | hi, jnp.int32)


def _unpack_halves(w):
    u = pltpu.bitcast(w, jnp.uint32)
    lo = pltpu.bitcast(jnp.left_shift(u, jnp.uint32(16)), F32)
    hi = pltpu.bitcast(u & jnp.uint32(0xFFFF0000), F32)
    return lo, hi


def _load_slabs(refs, *lead):
    return jnp.concatenate([r[lead] if lead else r[...] for r in refs], axis=1)


def _store_slabs(refs, packed):
    for si, r in enumerate(refs):
        r[...] = packed[:, si * D_SLAB:(si + 1) * D_SLAB]


def _params(n_axes=1):
    return pltpu.CompilerParams(dimension_semantics=("arbitrary",) * n_axes,
                                vmem_limit_bytes=V7X_VMEM_LIMIT)


def _mod_kernel(cond_ref, w_ref, b_ref, o_ref):
    c = cond_ref[...]
    c = c * jax.nn.sigmoid(c)
    w = w_ref[0]
    c_hi = c.astype(BF16)
    c_lo = (c - c_hi.astype(F32)).astype(BF16)
    w_hi = w.astype(BF16)
    w_lo = (w - w_hi.astype(F32)).astype(BF16)
    o_ref[0] = _dot(c_hi, w_hi) + _dot(c_lo, w_hi) + _dot(c_hi, w_lo) + b_ref[0]


def _modulation(cond, w_mod, b_mod):
    tn = 1536
    return pl.pallas_call(
        _mod_kernel,
        out_shape=jax.ShapeDtypeStruct((DEPTH, MOD_ROWS, 6 * D_MODEL), F32),
        grid=(DEPTH, 6 * D_MODEL // tn),
        in_specs=[pl.BlockSpec((MOD_ROWS, D_MODEL), lambda l, j: (0, 0)),
                  pl.BlockSpec((1, D_MODEL, tn), lambda l, j: (l, 0, j)),
                  pl.BlockSpec((1, 1, tn), lambda l, j: (l, 0, j))],
        out_specs=pl.BlockSpec((1, MOD_ROWS, tn), lambda l, j: (l, 0, j)),
        compiler_params=_params(2),
        name="modulation",
    )(cond, w_mod, b_mod.reshape(DEPTH, 1, 6 * D_MODEL))


def _rms_mod(x, g, shift, scale):
    ms = jnp.mean(x * x, axis=-1, keepdims=True)
    y = x * lax.rsqrt(ms + RMS_EPS) * g
    return y * (1.0 + scale) + shift


def _project_in(hm, w_ref, qkv_ref, h_ref, kc_ref, vc_ref):
    h = _dot(hm.astype(BF16), w_ref[0])
    qkv_ref[...] = h[:, :QKV_COLS].astype(BF16)
    h_ref[...] = h[:, QKV_COLS:]

    @pl.when(pl.program_id(0) < P_TILES)
    def _():
        for sq in range(SEQ_PER_TILE):
            rows = slice(sq * SEQ, (sq + 1) * SEQ)
            kc_ref[sq, 0:NA_WIDTH] = h[rows, NA_WIDTH:2 * NA_WIDTH].T
            vc_ref[sq, 0:NA_WIDTH] = h[rows, 2 * NA_WIDTH:3 * NA_WIDTH].T
            if kc_ref.shape[1] > NA_WIDTH:
                kc_ref[sq, NA_WIDTH:] = jnp.zeros((kc_ref.shape[1] - NA_WIDTH, SEQ), F32)
                vc_ref[sq, NA_WIDTH:] = jnp.zeros((vc_ref.shape[1] - NA_WIDTH, SEQ), F32)


def _pick_group(p_ref, s_ref):
    return jnp.where(pl.program_id(0) < P_TILES, p_ref[...], s_ref[...])


def _p_tile(width):
    return pl.BlockSpec((TM, width), lambda i: (jnp.minimum(i, P_TILES - 1), 0))


def _s_tile(width):
    return pl.BlockSpec((TM, width), lambda i: (jnp.maximum(i - P_TILES, 0), 0))


def _inproj_first_kernel(xp_ref, xs_ref, mod_ref, g_ref, w_ref, qkv_ref, h_ref, kc_ref, vc_ref):
    x = _pick_group(xp_ref, xs_ref)
    hm = _rms_mod(x, g_ref[...], mod_ref[0, 0:1, :], mod_ref[0, 1:2, :])
    _project_in(hm, w_ref, qkv_ref, h_ref, kc_ref, vc_ref)


def _combine_experts(yg_refs, gate_ref):
    gates = gate_ref[...]
    lo_acc = hi_acc = None
    for kk in range(TOP_K):
        lo, hi = _unpack_halves(_load_slabs(yg_refs, kk))
        gk = gates[:, kk:kk + 1]
        lo_acc = gk * lo if lo_acc is None else lo_acc + gk * lo
        hi_acc = gk * hi if hi_acc is None else hi_acc + gk * hi
    return jnp.concatenate([lo_acc, hi_acc], axis=1)


def _inproj_next_kernel(x_ref, yga_ref, ygb_ref, gate_ref, pmod_ref, mod_ref, g_ref, w_ref, kc_in, vc_in,
                        qkv_ref, h_ref, xo_ref, kc_ref, vc_ref):
    del kc_in, vc_in
    x = x_ref[...] + pmod_ref[0, 5:6, :] * _combine_experts((yga_ref, ygb_ref), gate_ref)
    xo_ref[...] = x
    hm = _rms_mod(x, g_ref[...], mod_ref[0, 0:1, :], mod_ref[0, 1:2, :])
    _project_in(hm, w_ref, qkv_ref, h_ref, kc_ref, vc_ref)


_TILE_SPEC = pl.BlockSpec((TM, D_MODEL), lambda i: (i, 0))
_MOD_SPEC = pl.BlockSpec((1, MOD_ROWS, D_MODEL), lambda i: (i, 0, 0))
_ROW_SPEC = pl.BlockSpec((1, D_MODEL), lambda i: (0, 0))
_RT_LANES = 128
_RT_ROWS = 2 * TOP_K
_YG_SPEC = pl.BlockSpec((TOP_K, TM, D_SLAB), lambda i: (0, i, 0))
_GATE_SPEC = pl.BlockSpec((TM, _RT_LANES), lambda i: (i, 0))


def _inproj(layer, x, moe, prev_mod, mod, g, w_bf16, caches):
    w_spec = pl.BlockSpec((1, D_MODEL, IN_COLS), lambda i: (layer, 0, 0))
    h_spec = pl.BlockSpec((TM, REST_COLS), lambda i: (i, 0))
    h_shape = jax.ShapeDtypeStruct((T_ALL, REST_COLS), F32)
    q_spec = pl.BlockSpec((TM, QKV_COLS), lambda i: (i, 0))
    q_shape = jax.ShapeDtypeStruct((T_ALL, QKV_COLS), BF16)
    c_spec = pl.BlockSpec((SEQ_PER_TILE, NA_WIDTH, SEQ), lambda i: (jnp.minimum(i, P_TILES - 1), layer, 0))
    c_shape = jax.ShapeDtypeStruct((BATCH, DEPTH * NA_WIDTH, SEQ), F32)
    x_shape = jax.ShapeDtypeStruct((T_ALL, D_MODEL), F32)
    if moe is None:
        c_all = pl.BlockSpec((SEQ_PER_TILE, DEPTH * NA_WIDTH, SEQ), lambda i: (jnp.minimum(i, P_TILES - 1), 0, 0))
        qkv, h, kc, vc = pl.pallas_call(
            _inproj_first_kernel, out_shape=(q_shape, h_shape, c_shape, c_shape), grid=(N_TILES,),
            in_specs=[_p_tile(D_MODEL), _s_tile(D_MODEL), _MOD_SPEC, _ROW_SPEC, w_spec],
            out_specs=(q_spec, h_spec, c_all, c_all),
            compiler_params=_params(), name="inproj_first",
        )(*x, mod, g, w_bf16)
        return qkv, h, x, (kc, vc)
    qkv, h, x, kc, vc = pl.pallas_call(
        _inproj_next_kernel,
        out_shape=(q_shape, h_shape, x_shape, c_shape, c_shape),
        grid=(N_TILES,),
        in_specs=[_TILE_SPEC, _YG_SPEC, _YG_SPEC, _GATE_SPEC, _MOD_SPEC, _MOD_SPEC, _ROW_SPEC, w_spec,
                  pl.BlockSpec(memory_space=pl.ANY), pl.BlockSpec(memory_space=pl.ANY)],
        out_specs=(q_spec, h_spec, _TILE_SPEC, c_spec, c_spec),
        input_output_aliases={8: 3, 9: 4},
        compiler_params=_params(), name="inproj_next",
    )(x, *moe[0], moe[1], prev_mod, mod, g, w_bf16, *caches)
    return qkv, h, x, (kc, vc)


def _pair_mask(hh):
    lane = lax.broadcasted_iota(jnp.int32, (1, 2 * NA_HEAD_DIM), 1)
    return (lane >= hh * NA_HEAD_DIM) & (lane < (hh + 1) * NA_HEAD_DIM)


_ATT_SEQS = 2


def _stack_pair(qp):
    return jnp.concatenate([jnp.where(_pair_mask(hh), qp, jnp.zeros_like(qp)) for hh in range(2)], axis=0)


def _unstack_pair(o2):
    half = o2.shape[0] // 2
    return jnp.where(_pair_mask(0), o2[:half], o2[half:])


def _attn_prompt_kernel(q_ref, k_ref, v_ref, o_ref):
    scale = NA_HEAD_DIM ** -0.5
    for sq in range(_ATT_SEQS):
        rows = slice(sq * SEQ, (sq + 1) * SEQ)
        for p in range(NA_HEADS // 2):
            cols = slice(p * 128, (p + 1) * 128)
            q2 = _stack_pair(q_ref[rows, cols] * scale)
            s = _dot_nt(q2, k_ref[rows, cols])
            e = jnp.exp(s - jnp.max(s, axis=-1, keepdims=True))
            den = jnp.sum(e, axis=-1, keepdims=True)
            o_ref[rows, cols] = _unstack_pair(_dot(e.astype(BF16), v_ref[rows, cols]) / den).astype(o_ref.dtype)


def _attn_prompt(qkv):
    rows = _ATT_SEQS * SEQ
    return pl.pallas_call(
        _attn_prompt_kernel,
        out_shape=jax.ShapeDtypeStruct((T_PROMPT, NA_WIDTH), BF16),
        grid=(BATCH // _ATT_SEQS,),
        in_specs=[pl.BlockSpec((rows, NA_WIDTH), lambda b: (b, 0)),
                  pl.BlockSpec((rows, NA_WIDTH), lambda b: (b, 1)),
                  pl.BlockSpec((rows, NA_WIDTH), lambda b: (b, 2))],
        out_specs=pl.BlockSpec((rows, NA_WIDTH), lambda b: (b, 0)),
        compiler_params=_params(), name="attn_prompt",
    )(qkv, qkv, qkv)


_NA_ROWS = DEC_SEQ // GRID_W
_NA_LOC = NA_KH * GRID_W
_NA_STEP_ROWS = 2


def _na_window_start(r):
    return jnp.clip(r - NA_KH // 2, 0, _NA_ROWS - NA_KH)


def _attn_sample_kernel(q_ref, k_ref, v_ref, ck_ref, cv_ref, *rest):
    bias_refs, o_ref = rest[:_NA_STEP_ROWS], rest[_NA_STEP_ROWS]
    scale = NA_HEAD_DIM ** -0.5
    for p in range(NA_HEADS // 2):
        cols = slice(p * 128, (p + 1) * 128)
        kc = ck_ref[0, :, cols].astype(BF16)
        vc = cv_ref[0, :, cols].astype(BF16)
        for u in range(_NA_STEP_ROWS):
            rows = slice(u * GRID_W, (u + 1) * GRID_W)
            s0 = pl.multiple_of(_na_window_start(pl.program_id(1) * _NA_STEP_ROWS + u) * GRID_W, GRID_W)
            q2 = _stack_pair(q_ref[rows, cols] * scale)
            bias2 = jnp.concatenate([bias_refs[u][0, 0, 2 * p], bias_refs[u][0, 0, 2 * p + 1]], axis=0)
            sl = _dot_nt(q2, k_ref[pl.ds(s0, _NA_LOC), cols]) + bias2
            sc = _dot_nt(q2, kc)
            mx = jnp.maximum(jnp.max(sl, axis=-1, keepdims=True), jnp.max(sc, axis=-1, keepdims=True))
            el = jnp.exp(sl - mx)
            ec = jnp.exp(sc - mx)
            den = jnp.sum(el, axis=-1, keepdims=True) + jnp.sum(ec, axis=-1, keepdims=True)
            o2 = (_dot(el.astype(BF16), v_ref[pl.ds(s0, _NA_LOC), cols]) + _dot(ec.astype(BF16), vc)) / den
            o_ref[rows, cols] = _unstack_pair(o2).astype(o_ref.dtype)


def _attn_sample(layer, qkv, ck, cv, bias):
    q_rows = _NA_STEP_ROWS * GRID_W
    steps = _NA_ROWS // _NA_STEP_ROWS
    q_blk0 = T_PROMPT // q_rows
    kv_row0 = T_PROMPT // DEC_SEQ

    def bias_spec(u):
        def index(b, r2):
            r = r2 * _NA_STEP_ROWS + u
            return (layer, _na_window_start(r) - r + NA_KH - 1, 0, 0, 0)
        return pl.BlockSpec((1, 1, NA_HEADS, GRID_W, _NA_LOC), index)

    return pl.pallas_call(
        _attn_sample_kernel,
        out_shape=jax.ShapeDtypeStruct((T_SAMPLE, NA_WIDTH), BF16),
        grid=(DEC_BATCH, steps),
        in_specs=[pl.BlockSpec((q_rows, NA_WIDTH), lambda b, r2: (q_blk0 + b * steps + r2, 0)),
                  pl.BlockSpec((DEC_SEQ, NA_WIDTH), lambda b, r2: (kv_row0 + b, 1)),
                  pl.BlockSpec((DEC_SEQ, NA_WIDTH), lambda b, r2: (kv_row0 + b, 2)),
                  pl.BlockSpec((1, PAST_LEN, NA_WIDTH), lambda b, r2: (b, 0, 0)),
                  pl.BlockSpec((1, PAST_LEN, NA_WIDTH), lambda b, r2: (b, 0, 0))]
                 + [bias_spec(u) for u in range(_NA_STEP_ROWS)],
        out_specs=pl.BlockSpec((q_rows, NA_WIDTH), lambda b, r2: (b * steps + r2, 0)),
        compiler_params=_params(2), name="attn_sample",
    )(qkv, qkv, qkv, ck, cv, *([bias] * _NA_STEP_ROWS))


_NA_DR = 2 * NA_KH - 1
_NA_DC = 2 * NA_KW - 1


def _na_bias_kernel(rb_ref, o_ref):
    i = pl.program_id(0)
    qc = lax.broadcasted_iota(jnp.int32, (GRID_W, GRID_W), 0)
    kc = lax.broadcasted_iota(jnp.int32, (GRID_W, GRID_W), 1)
    q_start = jnp.clip(qc - NA_KW // 2, 0, GRID_W - NA_KW)
    in_win = (kc >= q_start) & (kc < q_start + NA_KW)
    dc = jnp.clip(kc - qc + NA_KW - 1, 0, _NA_DC - 1)
    picks = [dc == d for d in range(_NA_DC)]
    tiles = []
    for dr in range(_NA_DR):
        acc = jnp.zeros((GRID_W, GRID_W), F32)
        for d in range(_NA_DC):
            acc = jnp.where(picks[d], rb_ref[i, dr * _NA_DC + d], acc)
        tiles.append(jnp.where(in_win, acc, NEG_INF))
    for base in range(NA_KH):
        o_ref[0, base, 0] = jnp.concatenate(tiles[base:base + NA_KH], axis=1)


def _na_bias_tables(rel_bias):
    rb = rel_bias.astype(F32).reshape(DEPTH * NA_HEADS, _NA_DR * _NA_DC)
    return pl.pallas_call(
        _na_bias_kernel,
        out_shape=jax.ShapeDtypeStruct((DEPTH, NA_KH, NA_HEADS, GRID_W, _NA_LOC), F32),
        grid=(DEPTH * NA_HEADS,),
        in_specs=[pl.BlockSpec(memory_space=pltpu.SMEM)],
        out_specs=pl.BlockSpec((1, NA_KH, 1, GRID_W, _NA_LOC),
                               lambda i: (i // NA_HEADS, 0, i % NA_HEADS, 0, 0)),
        compiler_params=_params(), name="na_bias_tables",
    )(rb)


_HG_GROUP = 8


def _hgrn_kernel(*refs, n_tok, has_state, n_alias):
    refs = refs[:8 + 2 * has_state] + refs[8 + 2 * has_state + n_alias:]
    if has_state:
        (q_ref, zf_ref, zb_ref, v_ref, g_ref, lbf_ref, lbb_ref, og_ref, s0f_ref, s0b_ref,
         rec_ref, sf_ref, sb_ref, kf_s, bf_s, kb_s, bb_s, of_s, ob_s, zf_s, zb_s, qsf_s, qsb_s, stf_s, stb_s) = refs
    else:
        (q_ref, zf_ref, zb_ref, v_ref, g_ref, lbf_ref, lbb_ref, og_ref,
         rec_ref, sf_ref, sb_ref, kf_s, bf_s, kb_s, bb_s, of_s, ob_s, zf_s, zb_s, qsf_s, qsb_s, stf_s, stb_s) = refs
        s0f_ref = s0b_ref = None
    C = HG_CHUNK
    W = HG_WIDTH
    n_chunks = n_tok // C
    rr = lax.broadcasted_iota(jnp.int32, (W, W), 0)
    cc = lax.broadcasted_iota(jnp.int32, (W, W), 1)
    log2_c = C.bit_length() - 1
    same_chunk = jnp.right_shift(rr, log2_c) == jnp.right_shift(cc, log2_c)
    tri_prefix = jnp.where(same_chunk & (cc <= rr), 1.0, 0.0).astype(BF16)
    tri_suffix = jnp.where(same_chunk & (cc >= rr), 1.0, 0.0).astype(BF16)
    same_head = jnp.right_shift(rr, 6) == jnp.right_shift(cc, 6)
    head_ones = jnp.where(same_head, 1.0, 0.0).astype(BF16)

    for ti in range(n_tok // W):
        rows = slice(ti * W, (ti + 1) * W)
        for z_ref, lb_ref, k_s, b_s, tri in ((zf_ref, lbf_ref, kf_s, bf_s, tri_prefix),
                                             (zb_ref, lbb_ref, kb_s, bb_s, tri_suffix)):
            z = z_ref[rows, :]
            lb = lb_ref[...]
            e = jnp.exp(-jnp.abs(z))
            big = 1.0 / (1.0 + e)
            small = e * big
            f = lb + (1.0 - lb) * jnp.where(z >= 0.0, big, small)
            logf = jnp.log(jnp.maximum(f, F_FLOOR))
            k_s[rows, :] = (1.0 - lb) * jnp.where(z >= 0.0, small, big)
            hi, mid, lo = _split3(logf)
            b_s[rows, :] = _dot(tri, hi) + _dot(tri, mid) + _dot(tri, lo)

    G = _HG_GROUP
    n_groups = C // G
    srow = lax.broadcasted_iota(jnp.int32, (G, W), 0)
    zf_s[...] = jnp.zeros_like(zf_s)
    zb_s[...] = jnp.zeros_like(zb_s)

    def scan_chunk(ci, k_s, b_s, z_s, ks_s, st_s, o_dir_s, fwd):
        c = ci if fwd else n_chunks - 1 - ci
        base = pl.multiple_of(c * C, C)
        q = q_ref[pl.ds(base, C), :]
        k = k_s[pl.ds(base, C), :]
        b = b_s[pl.ds(base, C), :]
        v = v_ref[pl.ds(base, C), :]
        q_far = {}
        for gs in range(n_groups):
            others = range(gs + 1, n_groups) if fwd else range(gs)
            if not others:
                continue
            rows_s = slice(gs * G, (gs + 1) * G)
            edge = (gs + 1) * G - 1 if fwd else gs * G
            b_edge = b[edge:edge + 1, :]
            ks_s[rows_s, :] = k[rows_s] * jnp.exp(b_edge - b[rows_s])
            for gt in others:
                rows_t = slice(gt * G, (gt + 1) * G)
                q_far[gs, gt] = q[rows_t] * jnp.exp(b[rows_t] - b_edge)
        for sx in range(C):
            gs = sx // G
            rows_g = slice(gs * G, (gs + 1) * G)
            k_row = k_s[pl.ds(base + sx, 1), :]
            b_row = b_s[pl.ds(base + sx, 1), :]
            keep = (srow + gs * G >= sx) if fwd else (srow + gs * G <= sx)
            z_s[sx * C + gs * G:sx * C + (gs + 1) * G, :] = jnp.where(
                keep, (k_row * q[rows_g]) * jnp.exp(b[rows_g] - b_row), 0.0)
            others = range(gs + 1, n_groups) if fwd else range(gs)
            if others:
                ks_row = ks_s[sx:sx + 1, :]
                for gt in others:
                    z_s[sx * C + gt * G:sx * C + (gt + 1) * G, :] = ks_row * q_far[gs, gt]
        a_rep = _dot(z_s[...].astype(BF16), head_ones)
        o_intra = jnp.sum(a_rep.reshape(C, C, W) * v[:, None, :], axis=0)
        b_end = b_s[pl.ds(base + (C - 1 if fwd else 0), 1), :]
        q_in = q * jnp.exp(b)
        k_st = k * jnp.exp(b_end - b)
        st = st_s[...]
        o_inter = _dot_nt(q_in.astype(BF16), st.astype(BF16))
        upd = _dot_tn(v.astype(BF16), k_st.astype(BF16))
        st_s[...] = st * jnp.exp(b_end) + jnp.where(same_head, upd, 0.0)
        o_dir_s[pl.ds(base, C), :] = o_intra + o_inter

    def load_state(s0_ref, st_s):
        if s0_ref is None:
            st_s[...] = jnp.zeros((W, W), F32)
            return
        for hh in range(HG_HEADS):
            parts = [s0_ref[0, hh] if g == hh else jnp.zeros((HG_DK, HG_DV), F32) for g in range(HG_HEADS)]
            st_s[hh * HG_DK:(hh + 1) * HG_DK, :] = jnp.concatenate(parts, axis=1)
        st_s[...] = st_s[...].T

    def store_state(st_s, out_ref):
        by_head = st_s[...].T
        for hh in range(HG_HEADS):
            out_ref[0, hh] = by_head[hh * HG_DK:(hh + 1) * HG_DK, hh * HG_DV:(hh + 1) * HG_DV]
        if out_ref.shape[1] > HG_HEADS:
            out_ref[0, HG_HEADS:] = jnp.zeros((out_ref.shape[1] - HG_HEADS, HG_DK, HG_DV), F32)

    load_state(s0f_ref, stf_s)
    load_state(s0b_ref, stb_s)

    def scan_both(ci, carry):
        scan_chunk(ci, kf_s, bf_s, zf_s, qsf_s, stf_s, of_s, True)
        scan_chunk(ci, kb_s, bb_s, zb_s, qsb_s, stb_s, ob_s, False)
        return carry
    lax.fori_loop(0, n_chunks, scan_both, 0)
    store_state(stf_s, sf_ref)
    store_state(stb_s, sb_ref)

    for ti in range(n_tok // W):
        rows = slice(ti * W, (ti + 1) * W)
        o = of_s[rows, :] + ob_s[rows, :]
        sq = o * o
        sq_hi = sq.astype(BF16)
        sq_lo = (sq - sq_hi.astype(F32)).astype(BF16)
        ms = (_dot(sq_hi, head_ones) + _dot(sq_lo, head_ones)) * (1.0 / HG_DV)
        g = g_ref[rows, :]
        y = o * lax.rsqrt(ms + RMS_EPS) * og_ref[...] * (g * jax.nn.sigmoid(g))
        rec_ref[rows, :] = y.astype(rec_ref.dtype)


def _hgrn(h, lbf, lbb, og, s0f, s0b, n_tok, n_seq, row0, layer=None, states=None):
    W = HG_WIDTH
    has_state = s0f is not None

    def col(cb):
        return pl.BlockSpec((n_tok, W), lambda i, cb=cb: (row0 + i, cb))

    vec = pl.BlockSpec((1, W), lambda i: (0, 0))
    st_spec = pl.BlockSpec((1, HG_HEADS, HG_DK, HG_DV), lambda i: (i, 0, 0, 0))
    in_specs = [col(_CB_HQ), col(_CB_ZF), col(_CB_ZB), col(_CB_HI), col(_CB_HG), vec, vec, vec]
    args = [h, h, h, h, h, lbf, lbb, og]
    if has_state:
        in_specs += [st_spec, st_spec]
        args += [s0f, s0b]
    seq_f32 = pltpu.VMEM((n_tok, W), F32)
    out_st_spec, st_rows, aliases = st_spec, HG_HEADS, {}
    if layer is not None:
        st_rows = DEPTH * HG_HEADS
        if states is None:
            out_st_spec = pl.BlockSpec((1, st_rows, HG_DK, HG_DV), lambda i: (i, 0, 0, 0))
        else:
            out_st_spec = pl.BlockSpec((1, HG_HEADS, HG_DK, HG_DV), lambda i: (i, layer, 0, 0))
            aliases = {len(args): 1, len(args) + 1: 2}
            in_specs += [pl.BlockSpec(memory_space=pl.ANY)] * 2
            args += list(states)
    return pl.pallas_call(
        functools.partial(_hgrn_kernel, n_tok=n_tok, has_state=has_state, n_alias=len(aliases)),
        out_shape=(jax.ShapeDtypeStruct((n_seq * n_tok, W), BF16),
                   jax.ShapeDtypeStruct((n_seq, st_rows, HG_DK, HG_DV), F32),
                   jax.ShapeDtypeStruct((n_seq, st_rows, HG_DK, HG_DV), F32)),
        grid=(n_seq,),
        in_specs=in_specs,
        input_output_aliases=aliases,
        out_specs=(pl.BlockSpec((n_tok, W), lambda i: (i, 0)), out_st_spec, out_st_spec),
        scratch_shapes=[seq_f32, seq_f32, seq_f32, seq_f32, seq_f32, seq_f32,
                        pltpu.VMEM((HG_CHUNK * HG_CHUNK, W), F32),
                        pltpu.VMEM((HG_CHUNK * HG_CHUNK, W), F32),
                        pltpu.VMEM((HG_CHUNK, W), F32),
                        pltpu.VMEM((HG_CHUNK, W), F32),
                        pltpu.VMEM((W, W), F32),
                        pltpu.VMEM((W, W), F32)],
        compiler_params=_params(), name="hgrn_state" if has_state else "hgrn_zero",
    )(*args)


def _gmlp_tile(u_ref, v_ref, g_ref, ws_ref, b_ref):
    lane = lax.broadcasted_iota(jnp.int32, (1, GM_WIDTH), 1)
    outs = []
    for ci in range(TM // GM_CHUNK):
        rows = slice(ci * GM_CHUNK, (ci + 1) * GM_CHUNK)
        v = v_ref[rows, :]
        ms = jnp.mean(v * v, axis=-1, keepdims=True)
        vn = (v * lax.rsqrt(ms + RMS_EPS) * g_ref[...]).astype(BF16)
        z = b_ref[...]
        for gi in range(GM_GROUPS):
            zg = _dot(ws_ref[gi], vn)
            in_group = (lane >= gi * GM_GDIM) & (lane < (gi + 1) * GM_GDIM)
            z = z + jnp.where(in_group, zg, 0.0)
        outs.append((u_ref[rows, :] * z).astype(BF16))
    return jnp.concatenate(outs, axis=0)


def _outproj_kernel(attp_ref, atts_ref, recp_ref, recs_ref, gu_ref, gv_ref, gg_ref, gws_ref, gb_ref,
                    xp_ref, xs_ref, mod_ref, g_ref, w_ref, wr_ref, br_ref,
                    x1_ref, h2a_ref, h2b_ref, rt_ref, gate_ref, cnt_ref):
    @pl.when(pl.program_id(0) == 0)
    def _():
        cnt_ref[...] = jnp.zeros_like(cnt_ref)

    out = (_dot(_pick_group(attp_ref, atts_ref), w_ref[0, 0:NA_WIDTH, :])
           + _dot(_pick_group(recp_ref, recs_ref), w_ref[0, NA_WIDTH:NA_WIDTH + HG_WIDTH, :])
           + _dot(_gmlp_tile(gu_ref, gv_ref, gg_ref, gws_ref, gb_ref), w_ref[0, NA_WIDTH + HG_WIDTH:, :]))
    x1 = _pick_group(xp_ref, xs_ref) + mod_ref[0, 2:3, :] * out
    x1_ref[...] = x1
    h2 = _rms_mod(x1, g_ref[...], mod_ref[0, 3:4, :], mod_ref[0, 4:5, :])
    _store_slabs((h2a_ref, h2b_ref), _pack_halves(h2))
    h_hi = h2.astype(BF16)
    h_lo = (h2 - h_hi.astype(F32)).astype(BF16)
    wr = wr_ref[...]
    w_hi = wr.astype(BF16)
    w_lo = (wr - w_hi.astype(F32)).astype(BF16)
    logits = _dot(h_hi, w_hi) + _dot(h_lo, w_hi) + _dot(h_hi, w_lo) + br_ref[...]
    lane_e = lax.broadcasted_iota(jnp.int32, (TM, N_EXPERTS), 1).astype(F32)
    lane_o = lax.broadcasted_iota(jnp.int32, (TM, _RT_LANES), 1)
    idx_acc = jnp.zeros((TM, _RT_LANES), F32)
    val_acc = jnp.zeros((TM, _RT_LANES), F32)
    top0 = None
    den = jnp.zeros((TM, 1), F32)
    work = logits
    picks = []
    for kk in range(TOP_K):
        m = jnp.max(work, axis=-1, keepdims=True)
        first = jnp.min(jnp.where(work == m, lane_e, float(N_EXPERTS)), axis=-1, keepdims=True)
        if kk == 0:
            top0 = m
        e = jnp.exp(m - top0)
        den = den + e
        idx_acc = jnp.where(lane_o == kk, first, idx_acc)
        val_acc = jnp.where(lane_o == kk, e, val_acc)
        picks.append(lane_e == first)
        work = jnp.where(picks[-1], -jnp.inf, work)
    gate_ref[...] = val_acc / den
    sel = jnp.zeros((TM, N_EXPERTS), F32)
    for pk in picks:
        sel = sel + jnp.where(pk, 1.0, 0.0)
    rr = lax.broadcasted_iota(jnp.int32, (TM, TM), 0)
    cc = lax.broadcasted_iota(jnp.int32, (TM, TM), 1)
    earlier = jnp.where(cc < rr, 1.0, 0.0).astype(BF16)
    seen = cnt_ref[0:1, 0:N_EXPERTS]
    before = _dot(earlier, sel.astype(BF16)) + seen
    for kk, pk in enumerate(picks):
        rank = jnp.sum(jnp.where(pk, before, 0.0), axis=-1, keepdims=True)
        idx_acc = jnp.where(lane_o == TOP_K + kk, rank, idx_acc)
    rt_ref[...] = idx_acc.T[0:_RT_ROWS, :].astype(jnp.int32)
    cnt_ref[0:1, 0:N_EXPERTS] = seen + jnp.sum(sel, axis=0, keepdims=True)


def _outproj(layer, att_p, att_s, rec_p, rec_s, h, gmlp_params, x, mod, g, w_bf16, wr, br):
    def tile(width):
        return pl.BlockSpec((TM, width), lambda i: (i, 0))

    if isinstance(x, tuple):
        x_args, x_specs = x, [_p_tile(D_MODEL), _s_tile(D_MODEL)]
    else:
        x_args = (x, x)
        x_specs = [_p_tile(D_MODEL), pl.BlockSpec((TM, D_MODEL), lambda i: (jnp.maximum(i, P_TILES), 0))]
    return pl.pallas_call(
        _outproj_kernel,
        out_shape=(jax.ShapeDtypeStruct((T_ALL, D_MODEL), F32),
                   jax.ShapeDtypeStruct((T_ALL, D_SLAB), jnp.int32),
                   jax.ShapeDtypeStruct((T_ALL, D_SLAB), jnp.int32),
                   jax.ShapeDtypeStruct((_RT_ROWS, T_ALL), jnp.int32),
                   jax.ShapeDtypeStruct((T_ALL, _RT_LANES), F32),
                   jax.ShapeDtypeStruct((8, _RT_LANES), F32)),
        grid=(N_TILES,),
        in_specs=[_p_tile(NA_WIDTH), _s_tile(NA_WIDTH), _p_tile(HG_WIDTH), _s_tile(HG_WIDTH),
                  pl.BlockSpec((TM, GM_WIDTH), lambda i: (i, _CB_GU)), pl.BlockSpec((TM, GM_WIDTH), lambda i: (i, _CB_GV)),
                  pl.BlockSpec((1, GM_WIDTH), lambda i: (0, 0)),
                  pl.BlockSpec((GM_GROUPS, GM_CHUNK, GM_CHUNK), lambda i: (0, 0, 0)),
                  pl.BlockSpec((GM_CHUNK, GM_WIDTH), lambda i: (0, 0)),
                  *x_specs, _MOD_SPEC, _ROW_SPEC,
                  pl.BlockSpec((1, D_MODEL, D_MODEL), lambda i: (layer, 0, 0)),
                  pl.BlockSpec((D_MODEL, N_EXPERTS), lambda i: (0, 0)),
                  pl.BlockSpec((1, N_EXPERTS), lambda i: (0, 0))],
        out_specs=(_TILE_SPEC, tile(D_SLAB), tile(D_SLAB), pl.BlockSpec((_RT_ROWS, TM), lambda i: (0, i)),
                   tile(_RT_LANES),
                   pl.BlockSpec((8, _RT_LANES), lambda i: (0, 0))),
        compiler_params=_params(), name="outproj_router",
    )(att_p, att_s, rec_p, rec_s, h, h, *gmlp_params, *x_args, mod, g, w_bf16, wr, br)


_W_CHUNKS = 4
_W_CAST_ROWS = 128
_W_DMA_PRIORITY = 1


def _moe_kernel(blk_e_ref, blk_on_ref, blk_new_ref, blk_next_ref,
                xa_ref, xb_ref, wg_hbm, bg_ref, wu_hbm, bu_ref, wd_hbm, bd_ref,
                ya_ref, yb_ref, w_f32, w_bf16, w_sem, *, layer):
    j = pl.program_id(0)

    def weight_copies(expert):
        rows = D_MODEL // _W_CHUNKS
        return [pltpu.make_async_copy(w_hbm.at[layer, expert, pl.ds(ci * rows, rows)],
                                      w_f32.at[wi, pl.ds(ci * rows, rows)], w_sem.at[wi, ci])
                for wi, w_hbm in enumerate((wg_hbm, wu_hbm, wd_hbm)) for ci in range(_W_CHUNKS)]

    @pl.when(j == 0)
    def _():
        for cp in weight_copies(blk_e_ref[0]):
            cp.start(priority=_W_DMA_PRIORITY)

    @pl.when(blk_new_ref[j] != 0)
    def _():
        for cp in weight_copies(blk_e_ref[j]):
            cp.wait()

        def cast_rows(ci, carry):
            rows = pl.ds(pl.multiple_of(ci * _W_CAST_ROWS, _W_CAST_ROWS), _W_CAST_ROWS)
            for wi in range(3):
                w_bf16[wi, rows, :] = w_f32[wi, rows, :].astype(BF16)
            return carry
        lax.fori_loop(0, D_MODEL // _W_CAST_ROWS, cast_rows, 0)

        @pl.when(blk_next_ref[j] >= 0)
        def _():
            for cp in weight_copies(blk_next_ref[j]):
                cp.start(priority=_W_DMA_PRIORITY)

    @pl.when(blk_on_ref[j] != 0)
    def _():
        lo, hi = _unpack_halves(_load_slabs((xa_ref, xb_ref)))
        x = jnp.concatenate([lo.astype(BF16), hi.astype(BF16)], axis=1)
        gate = jnp.minimum(_dot(x, w_bf16[0]) + bg_ref[0, 0], SWIGLU_LIMIT)
        up = jnp.clip(_dot(x, w_bf16[1]) + bu_ref[0, 0], -SWIGLU_LIMIT, SWIGLU_LIMIT)
        glu = gate * jax.nn.sigmoid(SWIGLU_ALPHA * gate)
        act = ((up + 1.0) * glu).astype(BF16)
        _store_slabs((ya_ref, yb_ref), _pack_halves(_dot(act, w_bf16[2]) + bd_ref[0, 0]))

    @pl.when(blk_on_ref[j] == 0)
    def _():
        ya_ref[...] = jnp.zeros_like(ya_ref)
        yb_ref[...] = jnp.zeros_like(yb_ref)


def _moe(layer, plan, x_sorted, wg, bg, wu, bu, wd, bd):
    n_plan = len(plan)
    b_spec = pl.BlockSpec((1, 1, 1, D_MODEL), lambda j, be, *_: (layer, be[j], 0, 0))
    x_spec = pl.BlockSpec((MOE_BM, D_SLAB), lambda j, *_: (j, 0))
    hbm = pl.BlockSpec(memory_space=pl.ANY)
    bias4 = lambda b: b.reshape(DEPTH, N_EXPERTS, 1, D_MODEL)
    return pl.pallas_call(
        functools.partial(_moe_kernel, layer=layer),
        out_shape=(jax.ShapeDtypeStruct((MOE_SLOTS, D_SLAB), jnp.int32),) * N_SPLIT,
        grid_spec=pltpu.PrefetchScalarGridSpec(
            num_scalar_prefetch=n_plan, grid=(MOE_BLOCKS,),
            in_specs=[x_spec, x_spec, hbm, b_spec, hbm, b_spec, hbm, b_spec],
            out_specs=(x_spec, x_spec),
            scratch_shapes=[pltpu.VMEM((3, D_MODEL, D_MODEL), F32), pltpu.VMEM((3, D_MODEL, D_MODEL), BF16),
                            pltpu.SemaphoreType.DMA((3, _W_CHUNKS))]),
        compiler_params=_params(), name="moe_experts",
    )(*plan, *x_sorted, wg, bias4(bg), wu, bias4(bu), wd, bias4(bd))


def _route(rt, counts):
    experts = jnp.arange(N_EXPERTS, dtype=jnp.int32)
    nblk = (counts + MOE_BM - 1) // MOE_BM
    blk_end = jnp.cumsum(nblk)
    row0 = (blk_end - nblk) * MOE_BM
    top_i, rank = rt[:TOP_K], rt[TOP_K:]
    start_of = jnp.sum(jnp.where(top_i[None] == experts[:, None, None], row0[:, None, None], 0), axis=0)
    dest = (start_of + rank).reshape(1, TOP_K * T_ALL)
    live = counts > 0
    last_live = jnp.max(jnp.where(live, experts, 0))
    later_live = live[None, :] & (experts[None, :] > experts[:, None])
    next_live = jnp.min(jnp.where(later_live, experts[None, :], N_EXPERTS), axis=1)
    next_live = jnp.where(next_live == N_EXPERTS, -1, next_live)
    blk = jnp.arange(MOE_BLOCKS, dtype=jnp.int32)
    blk_on = blk < blk_end[-1]
    blk_e = jnp.where(blk_on, jnp.minimum(jnp.sum((blk_end[None, :] <= blk[:, None]).astype(jnp.int32), axis=1),
                                          N_EXPERTS - 1), last_live)
    blk_new = blk_on & jnp.concatenate([jnp.ones((1,), bool), blk_e[1:] != blk_e[:-1]])
    is_e = blk_e[:, None] == experts[None, :]
    lookup = lambda table: jnp.sum(jnp.where(is_e, table[None, :], 0), axis=1)
    plan = (blk_e, blk_on, blk_new, lookup(next_live))
    return dest.astype(jnp.int32), tuple(p.astype(jnp.int32) for p in plan)


_SC_WINDOW = 128


def _sc_mesh():
    return plsc.VectorSubcoreMesh(core_axis_name="core", subcore_axis_name="subcore")


def _sc_scatter_rows(srcs, idx, n_out):
    n_src, width = srcs[0].shape
    n_rep = idx.shape[1] // n_src
    src_windows = n_src // _SC_WINDOW
    assert len(srcs) == 2

    def body(*refs):
        x_hbm = refs[:len(srcs)]
        i_hbm = refs[len(srcs)]
        o_hbm = refs[len(srcs) + 1:]

        def run(xs, os_):
            def step(x_vmem, *i_vmem):
                for iv in i_vmem:
                    pltpu.sync_copy(x_vmem, os_.at[iv.at[0]])

            pltpu.emit_pipeline(
                step, grid=(src_windows,),
                in_specs=[pl.BlockSpec((_SC_WINDOW, width), lambda i: (i, 0))]
                         + [pl.BlockSpec((1, _SC_WINDOW), lambda i, kk=kk: (0, kk * src_windows + i))
                            for kk in range(n_rep)],
                out_specs=[],
                core_axis_name="subcore",
                dimension_semantics=(pltpu.PARALLEL,),
            )(xs, *([i_hbm] * n_rep))

        for ci, (xs, os_) in enumerate(zip(x_hbm, o_hbm)):
            pl.when(lax.axis_index("core") == ci)(functools.partial(run, xs, os_))

    out_type = tuple(jax.ShapeDtypeStruct((n_out, width), s.dtype) for s in srcs)
    return pl.kernel(body, out_type=out_type, mesh=_sc_mesh(), scratch_types=[],
                     name="sc_scatter_rows")(*srcs, idx)


def _sc_gather_rows(tables, idx):
    n_idx = idx.shape[1]
    width = tables[0].shape[1]

    def body(*refs):
        t_hbm = refs[:len(tables)]
        i_hbm = refs[len(tables)]
        o_hbm = refs[len(tables) + 1:]
        for ts, os_ in zip(t_hbm, o_hbm):
            def step(i_vmem, o_vmem, ts=ts):
                pltpu.sync_copy(ts.at[i_vmem.at[0]], o_vmem)

            pltpu.emit_pipeline(
                step, grid=(n_idx // _SC_WINDOW,),
                in_specs=[pl.BlockSpec((1, _SC_WINDOW), lambda i: (0, i))],
                out_specs=[pl.BlockSpec((_SC_WINDOW, width), lambda i: (i, 0))],
                core_axis_name=("core", "subcore"),
                dimension_semantics=(pltpu.PARALLEL,),
            )(i_hbm, os_)

    out_type = tuple(jax.ShapeDtypeStruct((n_idx, width), t.dtype) for t in tables)
    return pl.kernel(body, out_type=out_type, mesh=_sc_mesh(), scratch_types=[],
                     name="sc_gather_rows")(*tables, idx)


def _final_kernel(x_ref, yga_ref, ygb_ref, gate_ref, mod_ref, g_ref, yp_ref, ys_ref):
    x = x_ref[...] + mod_ref[0, 5:6, :] * _combine_experts((yga_ref, ygb_ref), gate_ref)
    ms = jnp.mean(x * x, axis=-1, keepdims=True)
    y = x * lax.rsqrt(ms + RMS_EPS) * g_ref[...]

    @pl.when(pl.program_id(0) < P_TILES)
    def _():
        yp_ref[...] = y

    @pl.when(pl.program_id(0) >= P_TILES)
    def _():
        ys_ref[...] = y


def _final(x, moe, mod, g):
    return pl.pallas_call(
        _final_kernel,
        out_shape=(jax.ShapeDtypeStruct((T_PROMPT, D_MODEL), F32), jax.ShapeDtypeStruct((T_SAMPLE, D_MODEL), F32)),
        grid=(N_TILES,),
        in_specs=[_TILE_SPEC, _YG_SPEC, _YG_SPEC, _GATE_SPEC, _MOD_SPEC, _ROW_SPEC],
        out_specs=(_p_tile(D_MODEL), _s_tile(D_MODEL)),
        compiler_params=_params(), name="final_norm",
    )(x, *moe[0], moe[1], mod, g)


def kernel(x_prompt, x_sample, cache_k, cache_v, state_hgrn_fwd, state_hgrn_bwd, c, c_ctx, w_mod, b_mod, norm1_g, norm2_g, w_in, na_rel_bias, hgrn_lb, hgrn_onorm_g, gmlp_vnorm_g, gmlp_ws, gmlp_b, w_out, router_w, router_b, w_gate, b_gate, w_up, b_up, w_down, b_down, final_g):
    x = (x_prompt.reshape(T_PROMPT, D_MODEL), x_sample.reshape(T_SAMPLE, D_MODEL))

    cond = jnp.zeros((MOD_ROWS, D_MODEL), F32).at[0].set(c_ctx).at[1:1 + DEC_BATCH].set(c)
    mod = _modulation(cond, w_mod, b_mod)
    tile_row = np.concatenate([np.zeros(P_TILES, np.int32),
                               1 + np.arange(N_TILES - P_TILES, dtype=np.int32) // (DEC_SEQ // TM)])
    mod_tiles = mod[:, tile_row].reshape(DEPTH, N_TILES, 6, D_MODEL)
    mod_tiles = jnp.pad(mod_tiles, ((0, 0), (0, 0), (0, MOD_ROWS - 6), (0, 0)))

    lb_soft = jax.nn.softmax(hgrn_lb.astype(F32), axis=1)
    lower = jnp.cumsum(lb_soft, axis=1) - lb_soft[:, :1]

    na_bias = _na_bias_tables(na_rel_bias)
    w_in_bf16 = w_in.astype(BF16)
    w_out_bf16 = w_out.astype(BF16)

    moe_out = caches = states = None
    for l in range(DEPTH):
        qkv, h, x_next, caches = _inproj(l, x, moe_out, mod_tiles[l - 1] if l else None, mod_tiles[l],
                                    norm1_g[l][None, :], w_in_bf16, caches)

        att_p = _attn_prompt(qkv)
        att_s = _attn_sample(l, qkv, cache_k[:, l].reshape(DEC_BATCH, PAST_LEN, NA_WIDTH),
                             cache_v[:, l].reshape(DEC_BATCH, PAST_LEN, NA_WIDTH), na_bias)
        lbf = lower[0, l][None, :]
        lbb = lower[1, l][None, :]
        og = jnp.tile(hgrn_onorm_g[l], HG_HEADS)[None, :]
        rec_p, *states = _hgrn(h, lbf, lbb, og, None, None, SEQ, BATCH, 0, layer=l, states=states)
        rec_s, _, _ = _hgrn(h, lbf, lbb, og, state_hgrn_fwd[:, l].astype(F32), state_hgrn_bwd[:, l].astype(F32),
                            DEC_SEQ, DEC_BATCH, T_PROMPT // DEC_SEQ)
        gm_bias = jnp.repeat(gmlp_b[l].T, GM_GDIM, axis=1)
        gmlp_params = (gmlp_vnorm_g[l][None, :], gmlp_ws[l].astype(BF16), gm_bias)

        x, h2a, h2b, rt, gate_pad, cnt = _outproj(l, att_p, att_s, rec_p, rec_s, h, gmlp_params, x_next,
                                                  mod_tiles[l],
                                                  norm2_g[l][None, :], w_out_bf16,
                                                  router_w[l], router_b[l][None, :])
        dest_flat, plan = _route(rt, cnt[0, :N_EXPERTS].astype(jnp.int32))
        x_sorted = _sc_scatter_rows((h2a, h2b), dest_flat, MOE_SLOTS)
        y_sorted = _moe(l, plan, x_sorted, w_gate, b_gate, w_up, b_up, w_down, b_down)
        y_tok = _sc_gather_rows(y_sorted, dest_flat)
        moe_out = ([yt.reshape(TOP_K, T_ALL, D_SLAB) for yt in y_tok], gate_pad)

    y_prompt, y_sample = _final(x, moe_out, mod_tiles[DEPTH - 1], final_g[None, :])
    y_prompt = y_prompt.reshape(BATCH, SEQ, D_MODEL)
    y_sample = y_sample.reshape(DEC_BATCH, DEC_SEQ, D_MODEL)
    new_k, new_v = (cache.reshape(BATCH, DEPTH, NA_HEADS, NA_HEAD_DIM, SEQ).transpose(0, 1, 4, 2, 3)
                    for cache in caches)
    new_sf, new_sb = (st.reshape(BATCH, DEPTH, HG_HEADS, HG_DK, HG_DV) for st in states)
    return (y_prompt, y_sample, new_k, new_v, new_sf, new_sb)
```

```python
import functools

import numpy as np
import jax
import jax.numpy as jnp
from jax import lax
from jax.experimental import pallas as pl
from jax.experimental.pallas import tpu as pltpu
from jax.experimental.pallas import tpu_sc as plsc

F32 = jnp.float32
BF16 = jnp.bfloat16

D_MODEL = 1024
BATCH = 32
SEQ = 256
DEPTH = 2
DEC_BATCH = 2
DEC_SEQ = 1024
PAST_LEN = 512
GRID_W = 64
NA_HEADS = 8
NA_HEAD_DIM = 64
NA_WIDTH = NA_HEADS * NA_HEAD_DIM
NA_KH = 8
NA_KW = 16
HG_HEADS = 4
HG_DK = 64
HG_DV = 64
HG_WIDTH = HG_HEADS * HG_DV
HG_CHUNK = 16
F_FLOOR = 1e-30
GM_GROUPS = 4
GM_GDIM = 64
GM_WIDTH = GM_GROUPS * GM_GDIM
GM_CHUNK = 128
IN_COLS = 3 * NA_WIDTH + 5 * HG_WIDTH + 2 * GM_WIDTH
N_EXPERTS = 32
TOP_K = 4
SWIGLU_LIMIT = 7.0
SWIGLU_ALPHA = 1.702
RMS_EPS = 1e-6
NEG_INF = -1e30

T_PROMPT = BATCH * SEQ
T_SAMPLE = DEC_BATCH * DEC_SEQ
T_ALL = T_PROMPT + T_SAMPLE
TM = 512
SEQ_PER_TILE = TM // SEQ
N_TILES = T_ALL // TM
P_TILES = T_PROMPT // TM
MOE_BM = 512
MOE_SLOTS = -(-(T_ALL * TOP_K + N_EXPERTS * (MOE_BM - 1)) // MOE_BM) * MOE_BM
MOE_BLOCKS = MOE_SLOTS // MOE_BM
MOD_ROWS = 8
V7X_VMEM_LIMIT = 48 * 1024 * 1024

QKV_COLS = 3 * NA_WIDTH
REST_COLS = IN_COLS - QKV_COLS
_CB_HQ, _CB_ZF, _CB_ZB, _CB_HI, _CB_HG, _CB_GU, _CB_GV = range(7)


def _dot(a, b):
    return jnp.dot(a, b, preferred_element_type=F32)


def _dot_nt(a, b):
    return lax.dot_general(a, b, (((1,), (1,)), ((), ())), preferred_element_type=F32)


def _dot_tn(a, b):
    return lax.dot_general(a, b, (((0,), (0,)), ((), ())), preferred_element_type=F32)


def _split3(x):
    hi = x.astype(BF16)
    r1 = x - hi.astype(F32)
    mid = r1.astype(BF16)
    lo = (r1 - mid.astype(F32)).astype(BF16)
    return hi, mid, lo


D_PACK = D_MODEL // 2
N_SPLIT = 2
D_SLAB = D_PACK // N_SPLIT


def _pack_halves(x):
    half = x.shape[1] // 2
    lo = pltpu.bitcast(x[:, :half].astype(BF16).astype(F32), jnp.uint32)
    hi = pltpu.bitcast(x[:, half:].astype(BF16).astype(F32), jnp.uint32)
    return pltpu.bitcast(jnp.right_shift(lo, jnp.uint32(16)) | hi, jnp.int32)


def _unpack_halves(w):
    u = pltpu.bitcast(w, jnp.uint32)
    lo = pltpu.bitcast(jnp.left_shift(u, jnp.uint32(16)), F32)
    hi = pltpu.bitcast(u & jnp.uint32(0xFFFF0000), F32)
    return lo, hi


def _load_slabs(refs, *lead):
    return jnp.concatenate([r[lead] if lead else r[...] for r in refs], axis=1)


def _store_slabs(refs, packed):
    for si, r in enumerate(refs):
        r[...] = packed[:, si * D_SLAB:(si + 1) * D_SLAB]


def _params(n_axes=1):
    return pltpu.CompilerParams(dimension_semantics=("arbitrary",) * n_axes,
                                vmem_limit_bytes=V7X_VMEM_LIMIT)


def _mod_kernel(cond_ref, w_ref, b_ref, o_ref):
    c = cond_ref[...]
    c = c * jax.nn.sigmoid(c)
    w = w_ref[0]
    c_hi = c.astype(BF16)
    c_lo = (c - c_hi.astype(F32)).astype(BF16)
    w_hi = w.astype(BF16)
    w_lo = (w - w_hi.astype(F32)).astype(BF16)
    o_ref[0] = _dot(c_hi, w_hi) + _dot(c_lo, w_hi) + _dot(c_hi, w_lo) + b_ref[0]


def _modulation(cond, w_mod, b_mod):
    tn = 1536
    return pl.pallas_call(
        _mod_kernel,
        out_shape=jax.ShapeDtypeStruct((DEPTH, MOD_ROWS, 6 * D_MODEL), F32),
        grid=(DEPTH, 6 * D_MODEL // tn),
        in_specs=[pl.BlockSpec((MOD_ROWS, D_MODEL), lambda l, j: (0, 0)),
                  pl.BlockSpec((1, D_MODEL, tn), lambda l, j: (l, 0, j)),
                  pl.BlockSpec((1, 1, tn), lambda l, j: (l, 0, j))],
        out_specs=pl.BlockSpec((1, MOD_ROWS, tn), lambda l, j: (l, 0, j)),
        compiler_params=_params(2),
        name="modulation",
    )(cond, w_mod, b_mod.reshape(DEPTH, 1, 6 * D_MODEL))


def _rms_mod(x, g, shift, scale):
    ms = jnp.mean(x * x, axis=-1, keepdims=True)
    y = x * lax.rsqrt(ms + RMS_EPS) * g
    return y * (1.0 + scale) + shift


def _project_in(hm, w_ref, qkv_ref, h_ref, kc_ref, vc_ref):
    h = _dot(hm.astype(BF16), w_ref[0])
    qkv_ref[...] = h[:, :QKV_COLS].astype(BF16)
    h_ref[...] = h[:, QKV_COLS:]

    @pl.when(pl.program_id(0) < P_TILES)
    def _():
        for sq in range(SEQ_PER_TILE):
            rows = slice(sq * SEQ, (sq + 1) * SEQ)
            kc_ref[sq, 0:NA_WIDTH] = h[rows, NA_WIDTH:2 * NA_WIDTH].T
            vc_ref[sq, 0:NA_WIDTH] = h[rows, 2 * NA_WIDTH:3 * NA_WIDTH].T
            if kc_ref.shape[1] > NA_WIDTH:
                kc_ref[sq, NA_WIDTH:] = jnp.zeros((kc_ref.shape[1] - NA_WIDTH, SEQ), F32)
                vc_ref[sq, NA_WIDTH:] = jnp.zeros((vc_ref.shape[1] - NA_WIDTH, SEQ), F32)


def _pick_group(p_ref, s_ref):
    return jnp.where(pl.program_id(0) < P_TILES, p_ref[...], s_ref[...])


def _p_tile(width):
    return pl.BlockSpec((TM, width), lambda i: (jnp.minimum(i, P_TILES - 1), 0))


def _s_tile(width):
    return pl.BlockSpec((TM, width), lambda i: (jnp.maximum(i - P_TILES, 0), 0))


def _inproj_first_kernel(xp_ref, xs_ref, mod_ref, g_ref, w_ref, qkv_ref, h_ref, kc_ref, vc_ref):
    x = _pick_group(xp_ref, xs_ref)
    hm = _rms_mod(x, g_ref[...], mod_ref[0, 0:1, :], mod_ref[0, 1:2, :])
    _project_in(hm, w_ref, qkv_ref, h_ref, kc_ref, vc_ref)


def _combine_experts(yg_refs, gate_ref):
    gates = gate_ref[...]
    lo_acc = hi_acc = None
    for kk in range(TOP_K):
        lo, hi = _unpack_halves(_load_slabs(yg_refs, kk))
        gk = gates[:, kk:kk + 1]
        lo_acc = gk * lo if lo_acc is None else lo_acc + gk * lo
        hi_acc = gk * hi if hi_acc is None else hi_acc + gk * hi
    return jnp.concatenate([lo_acc, hi_acc], axis=1)


def _inproj_next_kernel(x_ref, yga_ref, ygb_ref, gate_ref, pmod_ref, mod_ref, g_ref, w_ref, kc_in, vc_in,
                        qkv_ref, h_ref, xo_ref, kc_ref, vc_ref):
    del kc_in, vc_in
    x = x_ref[...] + pmod_ref[0, 5:6, :] * _combine_experts((yga_ref, ygb_ref), gate_ref)
    xo_ref[...] = x
    hm = _rms_mod(x, g_ref[...], mod_ref[0, 0:1, :], mod_ref[0, 1:2, :])
    _project_in(hm, w_ref, qkv_ref, h_ref, kc_ref, vc_ref)


_TILE_SPEC = pl.BlockSpec((TM, D_MODEL), lambda i: (i, 0))
_MOD_SPEC = pl.BlockSpec((1, MOD_ROWS, D_MODEL), lambda i: (i, 0, 0))
_ROW_SPEC = pl.BlockSpec((1, D_MODEL), lambda i: (0, 0))
_RT_LANES = 128
_RT_ROWS = 2 * TOP_K
_YG_SPEC = pl.BlockSpec((TOP_K, TM, D_SLAB), lambda i: (0, i, 0))
_GATE_SPEC = pl.BlockSpec((TM, _RT_LANES), lambda i: (i, 0))


def _inproj(layer, x, moe, prev_mod, mod, g, w_bf16, caches):
    w_spec = pl.BlockSpec((1, D_MODEL, IN_COLS), lambda i: (layer, 0, 0))
    h_spec = pl.BlockSpec((TM, REST_COLS), lambda i: (i, 0))
    h_shape = jax.ShapeDtypeStruct((T_ALL, REST_COLS), F32)
    q_spec = pl.BlockSpec((TM, QKV_COLS), lambda i: (i, 0))
    q_shape = jax.ShapeDtypeStruct((T_ALL, QKV_COLS), BF16)
    c_spec = pl.BlockSpec((SEQ_PER_TILE, NA_WIDTH, SEQ), lambda i: (jnp.minimum(i, P_TILES - 1), layer, 0))
    c_shape = jax.ShapeDtypeStruct((BATCH, DEPTH * NA_WIDTH, SEQ), F32)
    x_shape = jax.ShapeDtypeStruct((T_ALL, D_MODEL), F32)
    if moe is None:
        c_all = pl.BlockSpec((SEQ_PER_TILE, DEPTH * NA_WIDTH, SEQ), lambda i: (jnp.minimum(i, P_TILES - 1), 0, 0))
        qkv, h, kc, vc = pl.pallas_call(
            _inproj_first_kernel, out_shape=(q_shape, h_shape, c_shape, c_shape), grid=(N_TILES,),
            in_specs=[_p_tile(D_MODEL), _s_tile(D_MODEL), _MOD_SPEC, _ROW_SPEC, w_spec],
            out_specs=(q_spec, h_spec, c_all, c_all),
            compiler_params=_params(), name="inproj_first",
        )(*x, mod, g, w_bf16)
        return qkv, h, x, (kc, vc)
    qkv, h, x, kc, vc = pl.pallas_call(
        _inproj_next_kernel,
        out_shape=(q_shape, h_shape, x_shape, c_shape, c_shape),
        grid=(N_TILES,),
        in_specs=[_TILE_SPEC, _YG_SPEC, _YG_SPEC, _GATE_SPEC, _MOD_SPEC, _MOD_SPEC, _ROW_SPEC, w_spec,
                  pl.BlockSpec(memory_space=pl.ANY), pl.BlockSpec(memory_space=pl.ANY)],
        out_specs=(q_spec, h_spec, _TILE_SPEC, c_spec, c_spec),
        input_output_aliases={8: 3, 9: 4},
        compiler_params=_params(), name="inproj_next",
    )(x, *moe[0], moe[1], prev_mod, mod, g, w_bf16, *caches)
    return qkv, h, x, (kc, vc)


def _pair_mask(hh):
    lane = lax.broadcasted_iota(jnp.int32, (1, 2 * NA_HEAD_DIM), 1)
    return (lane >= hh * NA_HEAD_DIM) & (lane < (hh + 1) * NA_HEAD_DIM)


_ATT_SEQS = 4


def _stack_pair(qp):
    return jnp.concatenate([jnp.where(_pair_mask(hh), qp, jnp.zeros_like(qp)) for hh in range(2)], axis=0)


def _unstack_pair(o2):
    half = o2.shape[0] // 2
    return jnp.where(_pair_mask(0), o2[:half], o2[half:])


def _attn_prompt_kernel(q_ref, k_ref, v_ref, o_ref):
    scale = NA_HEAD_DIM ** -0.5
    for sq in range(_ATT_SEQS):
        rows = slice(sq * SEQ, (sq + 1) * SEQ)
        for p in range(NA_HEADS // 2):
            cols = slice(p * 128, (p + 1) * 128)
            q2 = _stack_pair(q_ref[rows, cols] * scale)
            s = _dot_nt(q2, k_ref[rows, cols])
            e = jnp.exp(s - jnp.max(s, axis=-1, keepdims=True))
            den = jnp.sum(e, axis=-1, keepdims=True)
            o_ref[rows, cols] = _unstack_pair(_dot(e.astype(BF16), v_ref[rows, cols]) / den).astype(o_ref.dtype)


def _attn_prompt(qkv):
    rows = _ATT_SEQS * SEQ
    return pl.pallas_call(
        _attn_prompt_kernel,
        out_shape=jax.ShapeDtypeStruct((T_PROMPT, NA_WIDTH), BF16),
        grid=(BATCH // _ATT_SEQS,),
        in_specs=[pl.BlockSpec((rows, NA_WIDTH), lambda b: (b, 0)),
                  pl.BlockSpec((rows, NA_WIDTH), lambda b: (b, 1)),
                  pl.BlockSpec((rows, NA_WIDTH), lambda b: (b, 2))],
        out_specs=pl.BlockSpec((rows, NA_WIDTH), lambda b: (b, 0)),
        compiler_params=_params(), name="attn_prompt",
    )(qkv, qkv, qkv)


_NA_ROWS = DEC_SEQ // GRID_W
_NA_LOC = NA_KH * GRID_W
_NA_STEP_ROWS = 4


def _na_window_start(r):
    return jnp.clip(r - NA_KH // 2, 0, _NA_ROWS - NA_KH)


def _attn_sample_kernel(q_ref, k_ref, v_ref, ck_ref, cv_ref, *rest):
    bias_refs, o_ref = rest[:_NA_STEP_ROWS], rest[_NA_STEP_ROWS]
    scale = NA_HEAD_DIM ** -0.5
    for p in range(NA_HEADS // 2):
        cols = slice(p * 128, (p + 1) * 128)
        kc = ck_ref[0, :, cols].astype(BF16)
        vc = cv_ref[0, :, cols].astype(BF16)
        for u in range(_NA_STEP_ROWS):
            rows = slice(u * GRID_W, (u + 1) * GRID_W)
            s0 = pl.multiple_of(_na_window_start(pl.program_id(1) * _NA_STEP_ROWS + u) * GRID_W, GRID_W)
            q2 = _stack_pair(q_ref[rows, cols] * scale)
            bias2 = jnp.concatenate([bias_refs[u][0, 0, 2 * p], bias_refs[u][0, 0, 2 * p + 1]], axis=0)
            sl = _dot_nt(q2, k_ref[pl.ds(s0, _NA_LOC), cols]) + bias2
            sc = _dot_nt(q2, kc)
            mx = jnp.maximum(jnp.max(sl, axis=-1, keepdims=True), jnp.max(sc, axis=-1, keepdims=True))
            el = jnp.exp(sl - mx)
            ec = jnp.exp(sc - mx)
            den = jnp.sum(el, axis=-1, keepdims=True) + jnp.sum(ec, axis=-1, keepdims=True)
            o2 = (_dot(el.astype(BF16), v_ref[pl.ds(s0, _NA_LOC), cols]) + _dot(ec.astype(BF16), vc)) / den
            o_ref[rows, cols] = _unstack_pair(o2).astype(o_ref.dtype)


def _attn_sample(layer, qkv, ck, cv, bias):
    q_rows = _NA_STEP_ROWS * GRID_W
    steps = _NA_ROWS // _NA_STEP_ROWS
    q_blk0 = T_PROMPT // q_rows
    kv_row0 = T_PROMPT // DEC_SEQ

    def bias_spec(u):
        def index(b, r2):
            r = r2 * _NA_STEP_ROWS + u
            return (layer, _na_window_start(r) - r + NA_KH - 1, 0, 0, 0)
        return pl.BlockSpec((1, 1, NA_HEADS, GRID_W, _NA_LOC), index)

    return pl.pallas_call(
        _attn_sample_kernel,
        out_shape=jax.ShapeDtypeStruct((T_SAMPLE, NA_WIDTH), BF16),
        grid=(DEC_BATCH, steps),
        in_specs=[pl.BlockSpec((q_rows, NA_WIDTH), lambda b, r2: (q_blk0 + b * steps + r2, 0)),
                  pl.BlockSpec((DEC_SEQ, NA_WIDTH), lambda b, r2: (kv_row0 + b, 1)),
                  pl.BlockSpec((DEC_SEQ, NA_WIDTH), lambda b, r2: (kv_row0 + b, 2)),
                  pl.BlockSpec((1, PAST_LEN, NA_WIDTH), lambda b, r2: (b, 0, 0)),
                  pl.BlockSpec((1, PAST_LEN, NA_WIDTH), lambda b, r2: (b, 0, 0))]
                 + [bias_spec(u) for u in range(_NA_STEP_ROWS)],
        out_specs=pl.BlockSpec((q_rows, NA_WIDTH), lambda b, r2: (b * steps + r2, 0)),
        compiler_params=_params(2), name="attn_sample",
    )(qkv, qkv, qkv, ck, cv, *([bias] * _NA_STEP_ROWS))


_NA_DR = 2 * NA_KH - 1
_NA_DC = 2 * NA_KW - 1


def _na_bias_kernel(rb_ref, o_ref):
    qc = lax.broadcasted_iota(jnp.int32, (GRID_W, GRID_W), 0)
    kc = lax.broadcasted_iota(jnp.int32, (GRID_W, GRID_W), 1)
    q_start = jnp.clip(qc - NA_KW // 2, 0, GRID_W - NA_KW)
    in_win = (kc >= q_start) & (kc < q_start + NA_KW)
    dc = jnp.clip(kc - qc + NA_KW - 1, 0, _NA_DC - 1)
    picks = [dc == d for d in range(_NA_DC)]

    def one_head(hh, carry):
        i = pl.program_id(0) * NA_HEADS + hh
        tiles = []
        for dr in range(_NA_DR):
            acc = jnp.zeros((GRID_W, GRID_W), F32)
            for d in range(_NA_DC):
                acc = jnp.where(picks[d], rb_ref[i, dr * _NA_DC + d], acc)
            tiles.append(jnp.where(in_win, acc, NEG_INF))
        for base in range(NA_KH):
            o_ref[0, base, hh] = jnp.concatenate(tiles[base:base + NA_KH], axis=1)
        return carry
    lax.fori_loop(0, NA_HEADS, one_head, 0)


def _na_bias_tables(rel_bias):
    rb = rel_bias.astype(F32).reshape(DEPTH * NA_HEADS, _NA_DR * _NA_DC)
    return pl.pallas_call(
        _na_bias_kernel,
        out_shape=jax.ShapeDtypeStruct((DEPTH, NA_KH, NA_HEADS, GRID_W, _NA_LOC), F32),
        grid=(DEPTH,),
        in_specs=[pl.BlockSpec(memory_space=pltpu.SMEM)],
        out_specs=pl.BlockSpec((1, NA_KH, NA_HEADS, GRID_W, _NA_LOC), lambda i: (i, 0, 0, 0, 0)),
        compiler_params=_params(), name="na_bias_tables",
    )(rb)


_HG_GROUP = 8


def _hgrn_kernel(*refs, n_tok, has_state, n_alias):
    refs = refs[:8 + 2 * has_state] + refs[8 + 2 * has_state + n_alias:]
    if has_state:
        (q_ref, zf_ref, zb_ref, v_ref, g_ref, lbf_ref, lbb_ref, og_ref, s0f_ref, s0b_ref,
         rec_ref, sf_ref, sb_ref, kf_s, bf_s, kb_s, bb_s, of_s, ob_s, zf_s, zb_s, qsf_s, qsb_s, stf_s, stb_s) = refs
    else:
        (q_ref, zf_ref, zb_ref, v_ref, g_ref, lbf_ref, lbb_ref, og_ref,
         rec_ref, sf_ref, sb_ref, kf_s, bf_s, kb_s, bb_s, of_s, ob_s, zf_s, zb_s, qsf_s, qsb_s, stf_s, stb_s) = refs
        s0f_ref = s0b_ref = None
    C = HG_CHUNK
    W = HG_WIDTH
    n_chunks = n_tok // C
    rr = lax.broadcasted_iota(jnp.int32, (W, W), 0)
    cc = lax.broadcasted_iota(jnp.int32, (W, W), 1)
    log2_c = C.bit_length() - 1
    same_chunk = jnp.right_shift(rr, log2_c) == jnp.right_shift(cc, log2_c)
    tri_prefix = jnp.where(same_chunk & (cc <= rr), 1.0, 0.0).astype(BF16)
    tri_suffix = jnp.where(same_chunk & (cc >= rr), 1.0, 0.0).astype(BF16)
    same_head = jnp.right_shift(rr, 6) == jnp.right_shift(cc, 6)
    head_ones = jnp.where(same_head, 1.0, 0.0).astype(BF16)

    for ti in range(n_tok // W):
        rows = slice(ti * W, (ti + 1) * W)
        for z_ref, lb_ref, k_s, b_s, tri in ((zf_ref, lbf_ref, kf_s, bf_s, tri_prefix),
                                             (zb_ref, lbb_ref, kb_s, bb_s, tri_suffix)):
            z = z_ref[rows, :]
            lb = lb_ref[...]
            e = jnp.exp(-jnp.abs(z))
            big = 1.0 / (1.0 + e)
            small = e * big
            f = lb + (1.0 - lb) * jnp.where(z >= 0.0, big, small)
            logf = jnp.log(jnp.maximum(f, F_FLOOR))
            k_s[rows, :] = (1.0 - lb) * jnp.where(z >= 0.0, small, big)
            hi, mid, lo = _split3(logf)
            b_s[rows, :] = _dot(tri, hi) + _dot(tri, mid) + _dot(tri, lo)

    G = _HG_GROUP
    n_groups = C // G
    srow = lax.broadcasted_iota(jnp.int32, (G, W), 0)
    zf_s[...] = jnp.zeros_like(zf_s)
    zb_s[...] = jnp.zeros_like(zb_s)

    def scan_chunk(ci, k_s, b_s, z_s, ks_s, st_s, o_dir_s, fwd):
        c = ci if fwd else n_chunks - 1 - ci
        base = pl.multiple_of(c * C, C)
        q = q_ref[pl.ds(base, C), :]
        k = k_s[pl.ds(base, C), :]
        b = b_s[pl.ds(base, C), :]
        v = v_ref[pl.ds(base, C), :]
        q_far = {}
        for gs in range(n_groups):
            others = range(gs + 1, n_groups) if fwd else range(gs)
            if not others:
                continue
            rows_s = slice(gs * G, (gs + 1) * G)
            edge = (gs + 1) * G - 1 if fwd else gs * G
            b_edge = b[edge:edge + 1, :]
            ks_s[rows_s, :] = k[rows_s] * jnp.exp(b_edge - b[rows_s])
            for gt in others:
                rows_t = slice(gt * G, (gt + 1) * G)
                q_far[gs, gt] = q[rows_t] * jnp.exp(b[rows_t] - b_edge)
        for sx in range(C):
            gs = sx // G
            rows_g = slice(gs * G, (gs + 1) * G)
            k_row = k_s[pl.ds(base + sx, 1), :]
            b_row = b_s[pl.ds(base + sx, 1), :]
            keep = (srow + gs * G >= sx) if fwd else (srow + gs * G <= sx)
            z_s[sx * C + gs * G:sx * C + (gs + 1) * G, :] = jnp.where(
                keep, (k_row * q[rows_g]) * jnp.exp(b[rows_g] - b_row), 0.0)
            others = range(gs + 1, n_groups) if fwd else range(gs)
            if others:
                ks_row = ks_s[sx:sx + 1, :]
                for gt in others:
                    z_s[sx * C + gt * G:sx * C + (gt + 1) * G, :] = ks_row * q_far[gs, gt]
        a_rep = _dot(z_s[...].astype(BF16), head_ones)
        o_intra = jnp.sum(a_rep.reshape(C, C, W) * v[:, None, :], axis=0)
        b_end = b_s[pl.ds(base + (C - 1 if fwd else 0), 1), :]
        q_in = q * jnp.exp(b)
        k_st = k * jnp.exp(b_end - b)
        st = st_s[...]
        o_inter = _dot_nt(q_in.astype(BF16), st.astype(BF16))
        upd = _dot_tn(v.astype(BF16), k_st.astype(BF16))
        st_s[...] = st * jnp.exp(b_end) + jnp.where(same_head, upd, 0.0)
        o_dir_s[pl.ds(base, C), :] = o_intra + o_inter

    def load_state(s0_ref, st_s):
        if s0_ref is None:
            st_s[...] = jnp.zeros((W, W), F32)
            return
        for hh in range(HG_HEADS):
            parts = [s0_ref[0, hh] if g == hh else jnp.zeros((HG_DK, HG_DV), F32) for g in range(HG_HEADS)]
            st_s[hh * HG_DK:(hh + 1) * HG_DK, :] = jnp.concatenate(parts, axis=1)
        st_s[...] = st_s[...].T

    def store_state(st_s, out_ref):
        by_head = st_s[...].T
        for hh in range(HG_HEADS):
            out_ref[0, hh] = by_head[hh * HG_DK:(hh + 1) * HG_DK, hh * HG_DV:(hh + 1) * HG_DV]
        if out_ref.shape[1] > HG_HEADS:
            out_ref[0, HG_HEADS:] = jnp.zeros((out_ref.shape[1] - HG_HEADS, HG_DK, HG_DV), F32)

    load_state(s0f_ref, stf_s)
    load_state(s0b_ref, stb_s)

    def scan_both(ci, carry):
        scan_chunk(ci, kf_s, bf_s, zf_s, qsf_s, stf_s, of_s, True)
        scan_chunk(ci, kb_s, bb_s, zb_s, qsb_s, stb_s, ob_s, False)
        return carry
    lax.fori_loop(0, n_chunks, scan_both, 0)
    store_state(stf_s, sf_ref)
    store_state(stb_s, sb_ref)

    for ti in range(n_tok // W):
        rows = slice(ti * W, (ti + 1) * W)
        o = of_s[rows, :] + ob_s[rows, :]
        sq = o * o
        sq_hi = sq.astype(BF16)
        sq_lo = (sq - sq_hi.astype(F32)).astype(BF16)
        ms = (_dot(sq_hi, head_ones) + _dot(sq_lo, head_ones)) * (1.0 / HG_DV)
        g = g_ref[rows, :]
        y = o * lax.rsqrt(ms + RMS_EPS) * og_ref[...] * (g * jax.nn.sigmoid(g))
        rec_ref[rows, :] = y.astype(rec_ref.dtype)


def _hgrn(h, lbf, lbb, og, s0f, s0b, n_tok, n_seq, row0, layer=None, states=None):
    W = HG_WIDTH
    has_state = s0f is not None

    def col(cb):
        return pl.BlockSpec((n_tok, W), lambda i, cb=cb: (row0 + i, cb))

    vec = pl.BlockSpec((1, W), lambda i: (0, 0))
    st_spec = pl.BlockSpec((1, HG_HEADS, HG_DK, HG_DV), lambda i: (i, 0, 0, 0))
    in_specs = [col(_CB_HQ), col(_CB_ZF), col(_CB_ZB), col(_CB_HI), col(_CB_HG), vec, vec, vec]
    args = [h, h, h, h, h, lbf, lbb, og]
    if has_state:
        in_specs += [st_spec, st_spec]
        args += [s0f, s0b]
    seq_f32 = pltpu.VMEM((n_tok, W), F32)
    out_st_spec, st_rows, aliases = st_spec, HG_HEADS, {}
    if layer is not None:
        st_rows = DEPTH * HG_HEADS
        if states is None:
            out_st_spec = pl.BlockSpec((1, st_rows, HG_DK, HG_DV), lambda i: (i, 0, 0, 0))
        else:
            out_st_spec = pl.BlockSpec((1, HG_HEADS, HG_DK, HG_DV), lambda i: (i, layer, 0, 0))
            aliases = {len(args): 1, len(args) + 1: 2}
            in_specs += [pl.BlockSpec(memory_space=pl.ANY)] * 2
            args += list(states)
    return pl.pallas_call(
        functools.partial(_hgrn_kernel, n_tok=n_tok, has_state=has_state, n_alias=len(aliases)),
        out_shape=(jax.ShapeDtypeStruct((n_seq * n_tok, W), BF16),
                   jax.ShapeDtypeStruct((n_seq, st_rows, HG_DK, HG_DV), F32),
                   jax.ShapeDtypeStruct((n_seq, st_rows, HG_DK, HG_DV), F32)),
        grid=(n_seq,),
        in_specs=in_specs,
        input_output_aliases=aliases,
        out_specs=(pl.BlockSpec((n_tok, W), lambda i: (i, 0)), out_st_spec, out_st_spec),
        scratch_shapes=[seq_f32, seq_f32, seq_f32, seq_f32, seq_f32, seq_f32,
                        pltpu.VMEM((HG_CHUNK * HG_CHUNK, W), F32),
                        pltpu.VMEM((HG_CHUNK * HG_CHUNK, W), F32),
                        pltpu.VMEM((HG_CHUNK, W), F32),
                        pltpu.VMEM((HG_CHUNK, W), F32),
                        pltpu.VMEM((W, W), F32),
                        pltpu.VMEM((W, W), F32)],
        compiler_params=_params(), name="hgrn_state" if has_state else "hgrn_zero",
    )(*args)


def _gmlp_tile(u_ref, v_ref, g_ref, ws_ref, b_ref):
    lane = lax.broadcasted_iota(jnp.int32, (1, GM_WIDTH), 1)
    outs = []
    for ci in range(TM // GM_CHUNK):
        rows = slice(ci * GM_CHUNK, (ci + 1) * GM_CHUNK)
        v = v_ref[rows, :]
        ms = jnp.mean(v * v, axis=-1, keepdims=True)
        vn = (v * lax.rsqrt(ms + RMS_EPS) * g_ref[...]).astype(BF16)
        z = b_ref[...]
        for gi in range(GM_GROUPS):
            zg = _dot(ws_ref[gi], vn)
            in_group = (lane >= gi * GM_GDIM) & (lane < (gi + 1) * GM_GDIM)
            z = z + jnp.where(in_group, zg, 0.0)
        outs.append((u_ref[rows, :] * z).astype(BF16))
    return jnp.concatenate(outs, axis=0)


def _outproj_kernel(attp_ref, atts_ref, recp_ref, recs_ref, gu_ref, gv_ref, gg_ref, gws_ref, gb_ref,
                    xp_ref, xs_ref, mod_ref, g_ref, w_ref, wr_ref, br_ref,
                    x1_ref, h2a_ref, h2b_ref, rt_ref, gate_ref, cnt_ref):
    @pl.when(pl.program_id(0) == 0)
    def _():
        cnt_ref[...] = jnp.zeros_like(cnt_ref)

    out = (_dot(_pick_group(attp_ref, atts_ref), w_ref[0, 0:NA_WIDTH, :])
           + _dot(_pick_group(recp_ref, recs_ref), w_ref[0, NA_WIDTH:NA_WIDTH + HG_WIDTH, :])
           + _dot(_gmlp_tile(gu_ref, gv_ref, gg_ref, gws_ref, gb_ref), w_ref[0, NA_WIDTH + HG_WIDTH:, :]))
    x1 = _pick_group(xp_ref, xs_ref) + mod_ref[0, 2:3, :] * out
    x1_ref[...] = x1
    h2 = _rms_mod(x1, g_ref[...], mod_ref[0, 3:4, :], mod_ref[0, 4:5, :])
    _store_slabs((h2a_ref, h2b_ref), _pack_halves(h2))
    h_hi = h2.astype(BF16)
    h_lo = (h2 - h_hi.astype(F32)).astype(BF16)
    wr = wr_ref[...]
    w_hi = wr.astype(BF16)
    w_lo = (wr - w_hi.astype(F32)).astype(BF16)
    logits = _dot(h_hi, w_hi) + _dot(h_lo, w_hi) + _dot(h_hi, w_lo) + br_ref[...]
    lane_e = lax.broadcasted_iota(jnp.int32, (TM, N_EXPERTS), 1).astype(F32)
    lane_o = lax.broadcasted_iota(jnp.int32, (TM, _RT_LANES), 1)
    idx_acc = jnp.zeros((TM, _RT_LANES), F32)
    val_acc = jnp.zeros((TM, _RT_LANES), F32)
    top0 = None
    den = jnp.zeros((TM, 1), F32)
    work = logits
    picks = []
    for kk in range(TOP_K):
        m = jnp.max(work, axis=-1, keepdims=True)
        first = jnp.min(jnp.where(work == m, lane_e, float(N_EXPERTS)), axis=-1, keepdims=True)
        if kk == 0:
            top0 = m
        e = jnp.exp(m - top0)
        den = den + e
        idx_acc = jnp.where(lane_o == kk, first, idx_acc)
        val_acc = jnp.where(lane_o == kk, e, val_acc)
        picks.append(lane_e == first)
        work = jnp.where(picks[-1], -jnp.inf, work)
    gate_ref[...] = val_acc / den
    sel = jnp.zeros((TM, N_EXPERTS), F32)
    for pk in picks:
        sel = sel + jnp.where(pk, 1.0, 0.0)
    rr = lax.broadcasted_iota(jnp.int32, (TM, TM), 0)
    cc = lax.broadcasted_iota(jnp.int32, (TM, TM), 1)
    earlier = jnp.where(cc < rr, 1.0, 0.0).astype(BF16)
    seen = cnt_ref[0:1, 0:N_EXPERTS]
    before = _dot(earlier, sel.astype(BF16)) + seen
    for kk, pk in enumerate(picks):
        rank = jnp.sum(jnp.where(pk, before, 0.0), axis=-1, keepdims=True)
        idx_acc = jnp.where(lane_o == TOP_K + kk, rank, idx_acc)
    rt_ref[...] = idx_acc.T[0:_RT_ROWS, :].astype(jnp.int32)
    cnt_ref[0:1, 0:N_EXPERTS] = seen + jnp.sum(sel, axis=0, keepdims=True)


def _outproj(layer, att_p, att_s, rec_p, rec_s, h, gmlp_params, x, mod, g, w_bf16, wr, br):
    def tile(width):
        return pl.BlockSpec((TM, width), lambda i: (i, 0))

    if isinstance(x, tuple):
        x_args, x_specs = x, [_p_tile(D_MODEL), _s_tile(D_MODEL)]
    else:
        x_args = (x, x)
        x_specs = [_p_tile(D_MODEL), pl.BlockSpec((TM, D_MODEL), lambda i: (jnp.maximum(i, P_TILES), 0))]
    return pl.pallas_call(
        _outproj_kernel,
        out_shape=(jax.ShapeDtypeStruct((T_ALL, D_MODEL), F32),
                   jax.ShapeDtypeStruct((T_ALL, D_SLAB), jnp.int32),
                   jax.ShapeDtypeStruct((T_ALL, D_SLAB), jnp.int32),
                   jax.ShapeDtypeStruct((_RT_ROWS, T_ALL), jnp.int32),
                   jax.ShapeDtypeStruct((T_ALL, _RT_LANES), F32),
                   jax.ShapeDtypeStruct((8, _RT_LANES), F32)),
        grid=(N_TILES,),
        in_specs=[_p_tile(NA_WIDTH), _s_tile(NA_WIDTH), _p_tile(HG_WIDTH), _s_tile(HG_WIDTH),
                  pl.BlockSpec((TM, GM_WIDTH), lambda i: (i, _CB_GU)), pl.BlockSpec((TM, GM_WIDTH), lambda i: (i, _CB_GV)),
                  pl.BlockSpec((1, GM_WIDTH), lambda i: (0, 0)),
                  pl.BlockSpec((GM_GROUPS, GM_CHUNK, GM_CHUNK), lambda i: (0, 0, 0)),
                  pl.BlockSpec((GM_CHUNK, GM_WIDTH), lambda i: (0, 0)),
                  *x_specs, _MOD_SPEC, _ROW_SPEC,
                  pl.BlockSpec((1, D_MODEL, D_MODEL), lambda i: (layer, 0, 0)),
                  pl.BlockSpec((D_MODEL, N_EXPERTS), lambda i: (0, 0)),
                  pl.BlockSpec((1, N_EXPERTS), lambda i: (0, 0))],
        out_specs=(_TILE_SPEC, tile(D_SLAB), tile(D_SLAB), pl.BlockSpec((_RT_ROWS, TM), lambda i: (0, i)),
                   tile(_RT_LANES),
                   pl.BlockSpec((8, _RT_LANES), lambda i: (0, 0))),
        compiler_params=_params(), name="outproj_router",
    )(att_p, att_s, rec_p, rec_s, h, h, *gmlp_params, *x_args, mod, g, w_bf16, wr, br)


_W_CHUNKS = 4
_W_CAST_ROWS = 128
_W_DMA_PRIORITY = 1


def _moe_kernel(blk_e_ref, blk_on_ref, blk_new_ref, blk_next_ref,
                xa_ref, xb_ref, wg_hbm, bg_ref, wu_hbm, bu_ref, wd_hbm, bd_ref,
                ya_ref, yb_ref, w_f32, w_bf16, w_sem, *, layer):
    j = pl.program_id(0)

    def weight_copies(expert):
        rows = D_MODEL // _W_CHUNKS
        return [pltpu.make_async_copy(w_hbm.at[layer, expert, pl.ds(ci * rows, rows)],
                                      w_f32.at[wi, pl.ds(ci * rows, rows)], w_sem.at[wi, ci])
                for wi, w_hbm in enumerate((wg_hbm, wu_hbm, wd_hbm)) for ci in range(_W_CHUNKS)]

    @pl.when(j == 0)
    def _():
        for cp in weight_copies(blk_e_ref[0]):
            cp.start(priority=_W_DMA_PRIORITY)

    @pl.when(blk_new_ref[j] != 0)
    def _():
        for cp in weight_copies(blk_e_ref[j]):
            cp.wait()

        def cast_rows(ci, carry):
            rows = pl.ds(pl.multiple_of(ci * _W_CAST_ROWS, _W_CAST_ROWS), _W_CAST_ROWS)
            for wi in range(3):
                w_bf16[wi, rows, :] = w_f32[wi, rows, :].astype(BF16)
            return carry
        lax.fori_loop(0, D_MODEL // _W_CAST_ROWS, cast_rows, 0)

        @pl.when(blk_next_ref[j] >= 0)
        def _():
            for cp in weight_copies(blk_next_ref[j]):
                cp.start(priority=_W_DMA_PRIORITY)

    @pl.when(blk_on_ref[j] != 0)
    def _():
        lo, hi = _unpack_halves(_load_slabs((xa_ref, xb_ref)))
        x = jnp.concatenate([lo.astype(BF16), hi.astype(BF16)], axis=1)
        gate = jnp.minimum(_dot(x, w_bf16[0]) + bg_ref[0, 0], SWIGLU_LIMIT)
        up = jnp.clip(_dot(x, w_bf16[1]) + bu_ref[0, 0], -SWIGLU_LIMIT, SWIGLU_LIMIT)
        glu = gate * jax.nn.sigmoid(SWIGLU_ALPHA * gate)
        act = ((up + 1.0) * glu).astype(BF16)
        _store_slabs((ya_ref, yb_ref), _pack_halves(_dot(act, w_bf16[2]) + bd_ref[0, 0]))

    @pl.when(blk_on_ref[j] == 0)
    def _():
        ya_ref[...] = jnp.zeros_like(ya_ref)
        yb_ref[...] = jnp.zeros_like(yb_ref)


def _moe(layer, plan, x_sorted, wg, bg, wu, bu, wd, bd):
    n_plan = len(plan)
    b_spec = pl.BlockSpec((1, 1, 1, D_MODEL), lambda j, be, *_: (layer, be[j], 0, 0))
    x_spec = pl.BlockSpec((MOE_BM, D_SLAB), lambda j, *_: (j, 0))
    hbm = pl.BlockSpec(memory_space=pl.ANY)
    bias4 = lambda b: b.reshape(DEPTH, N_EXPERTS, 1, D_MODEL)
    return pl.pallas_call(
        functools.partial(_moe_kernel, layer=layer),
        out_shape=(jax.ShapeDtypeStruct((MOE_SLOTS, D_SLAB), jnp.int32),) * N_SPLIT,
        grid_spec=pltpu.PrefetchScalarGridSpec(
            num_scalar_prefetch=n_plan, grid=(MOE_BLOCKS,),
            in_specs=[x_spec, x_spec, hbm, b_spec, hbm, b_spec, hbm, b_spec],
            out_specs=(x_spec, x_spec),
            scratch_shapes=[pltpu.VMEM((3, D_MODEL, D_MODEL), F32), pltpu.VMEM((3, D_MODEL, D_MODEL), BF16),
                            pltpu.SemaphoreType.DMA((3, _W_CHUNKS))]),
        compiler_params=_params(), name="moe_experts",
    )(*plan, *x_sorted, wg, bias4(bg), wu, bias4(bu), wd, bias4(bd))


def _route(rt, counts):
    experts = jnp.arange(N_EXPERTS, dtype=jnp.int32)
    nblk = (counts + MOE_BM - 1) // MOE_BM
    blk_end = jnp.cumsum(nblk)
    row0 = (blk_end - nblk) * MOE_BM
    top_i, rank = rt[:TOP_K], rt[TOP_K:]
    start_of = jnp.sum(jnp.where(top_i[None] == experts[:, None, None], row0[:, None, None], 0), axis=0)
    dest = (start_of + rank).reshape(1, TOP_K * T_ALL)
    live = counts > 0
    last_live = jnp.max(jnp.where(live, experts, 0))
    later_live = live[None, :] & (experts[None, :] > experts[:, None])
    next_live = jnp.min(jnp.where(later_live, experts[None, :], N_EXPERTS), axis=1)
    next_live = jnp.where(next_live == N_EXPERTS, -1, next_live)
    blk = jnp.arange(MOE_BLOCKS, dtype=jnp.int32)
    blk_on = blk < blk_end[-1]
    blk_e = jnp.where(blk_on, jnp.minimum(jnp.sum((blk_end[None, :] <= blk[:, None]).astype(jnp.int32), axis=1),
                                          N_EXPERTS - 1), last_live)
    blk_new = blk_on & jnp.concatenate([jnp.ones((1,), bool), blk_e[1:] != blk_e[:-1]])
    is_e = blk_e[:, None] == experts[None, :]
    lookup = lambda table: jnp.sum(jnp.where(is_e, table[None, :], 0), axis=1)
    plan = (blk_e, blk_on, blk_new, lookup(next_live))
    return dest.astype(jnp.int32), tuple(p.astype(jnp.int32) for p in plan)


_SC_WINDOW = 128


def _sc_mesh():
    return plsc.VectorSubcoreMesh(core_axis_name="core", subcore_axis_name="subcore")


def _sc_scatter_rows(srcs, idx, n_out):
    n_src, width = srcs[0].shape
    n_rep = idx.shape[1] // n_src
    src_windows = n_src // _SC_WINDOW
    assert len(srcs) == 2

    def body(*refs):
        x_hbm = refs[:len(srcs)]
        i_hbm = refs[len(srcs)]
        o_hbm = refs[len(srcs) + 1:]

        def run(xs, os_):
            def step(x_vmem, *i_vmem):
                for iv in i_vmem:
                    pltpu.sync_copy(x_vmem, os_.at[iv.at[0]])

            pltpu.emit_pipeline(
                step, grid=(src_windows,),
                in_specs=[pl.BlockSpec((_SC_WINDOW, width), lambda i: (i, 0))]
                         + [pl.BlockSpec((1, _SC_WINDOW), lambda i, kk=kk: (0, kk * src_windows + i))
                            for kk in range(n_rep)],
                out_specs=[],
                core_axis_name="subcore",
                dimension_semantics=(pltpu.PARALLEL,),
            )(xs, *([i_hbm] * n_rep))

        for ci, (xs, os_) in enumerate(zip(x_hbm, o_hbm)):
            pl.when(lax.axis_index("core") == ci)(functools.partial(run, xs, os_))

    out_type = tuple(jax.ShapeDtypeStruct((n_out, width), s.dtype) for s in srcs)
    return pl.kernel(body, out_type=out_type, mesh=_sc_mesh(), scratch_types=[],
                     name="sc_scatter_rows")(*srcs, idx)


def _sc_gather_rows(tables, idx):
    n_idx = idx.shape[1]
    width = tables[0].shape[1]

    def body(*refs):
        t_hbm = refs[:len(tables)]
        i_hbm = refs[len(tables)]
        o_hbm = refs[len(tables) + 1:]
        for ts, os_ in zip(t_hbm, o_hbm):
            def step(i_vmem, o_vmem, ts=ts):
                pltpu.sync_copy(ts.at[i_vmem.at[0]], o_vmem)

            pltpu.emit_pipeline(
                step, grid=(n_idx // _SC_WINDOW,),
                in_specs=[pl.BlockSpec((1, _SC_WINDOW), lambda i: (0, i))],
                out_specs=[pl.BlockSpec((_SC_WINDOW, width), lambda i: (i, 0))],
                core_axis_name=("core", "subcore"),
                dimension_semantics=(pltpu.PARALLEL,),
            )(i_hbm, os_)

    out_type = tuple(jax.ShapeDtypeStruct((n_idx, width), t.dtype) for t in tables)
    return pl.kernel(body, out_type=out_type, mesh=_sc_mesh(), scratch_types=[],
                     name="sc_gather_rows")(*tables, idx)


def _final_kernel(x_ref, yga_ref, ygb_ref, gate_ref, mod_ref, g_ref, yp_ref, ys_ref):
    x = x_ref[...] + mod_ref[0, 5:6, :] * _combine_experts((yga_ref, ygb_ref), gate_ref)
    ms = jnp.mean(x * x, axis=-1, keepdims=True)
    y = x * lax.rsqrt(ms + RMS_EPS) * g_ref[...]

    @pl.when(pl.program_id(0) < P_TILES)
    def _():
        yp_ref[...] = y

    @pl.when(pl.program_id(0) >= P_TILES)
    def _():
        ys_ref[...] = y


def _final(x, moe, mod, g):
    return pl.pallas_call(
        _final_kernel,
        out_shape=(jax.ShapeDtypeStruct((T_PROMPT, D_MODEL), F32), jax.ShapeDtypeStruct((T_SAMPLE, D_MODEL), F32)),
        grid=(N_TILES,),
        in_specs=[_TILE_SPEC, _YG_SPEC, _YG_SPEC, _GATE_SPEC, _MOD_SPEC, _ROW_SPEC],
        out_specs=(_p_tile(D_MODEL), _s_tile(D_MODEL)),
        compiler_params=_params(), name="final_norm",
    )(x, *moe[0], moe[1], mod, g)


def kernel(x_prompt, x_sample, cache_k, cache_v, state_hgrn_fwd, state_hgrn_bwd, c, c_ctx, w_mod, b_mod, norm1_g, norm2_g, w_in, na_rel_bias, hgrn_lb, hgrn_onorm_g, gmlp_vnorm_g, gmlp_ws, gmlp_b, w_out, router_w, router_b, w_gate, b_gate, w_up, b_up, w_down, b_down, final_g):
    x = (x_prompt.reshape(T_PROMPT, D_MODEL), x_sample.reshape(T_SAMPLE, D_MODEL))

    cond = jnp.zeros((MOD_ROWS, D_MODEL), F32).at[0].set(c_ctx).at[1:1 + DEC_BATCH].set(c)
    mod = _modulation(cond, w_mod, b_mod)
    tile_row = np.concatenate([np.zeros(P_TILES, np.int32),
                               1 + np.arange(N_TILES - P_TILES, dtype=np.int32) // (DEC_SEQ // TM)])
    mod_tiles = mod[:, tile_row].reshape(DEPTH, N_TILES, 6, D_MODEL)
    mod_tiles = jnp.pad(mod_tiles, ((0, 0), (0, 0), (0, MOD_ROWS - 6), (0, 0)))

    lb_soft = jax.nn.softmax(hgrn_lb.astype(F32), axis=1)
    lower = jnp.cumsum(lb_soft, axis=1) - lb_soft[:, :1]

    na_bias = _na_bias_tables(na_rel_bias)
    w_in_bf16 = w_in.astype(BF16)
    w_out_bf16 = w_out.astype(BF16)

    moe_out = caches = states = None
    for l in range(DEPTH):
        qkv, h, x_next, caches = _inproj(l, x, moe_out, mod_tiles[l - 1] if l else None, mod_tiles[l],
                                    norm1_g[l][None, :], w_in_bf16, caches)

        att_p = _attn_prompt(qkv)
        att_s = _attn_sample(l, qkv, cache_k[:, l].reshape(DEC_BATCH, PAST_LEN, NA_WIDTH),
                             cache_v[:, l].reshape(DEC_BATCH, PAST_LEN, NA_WIDTH), na_bias)
        lbf = lower[0, l][None, :]
        lbb = lower[1, l][None, :]
        og = jnp.tile(hgrn_onorm_g[l], HG_HEADS)[None, :]
        rec_p, *states = _hgrn(h, lbf, lbb, og, None, None, SEQ, BATCH, 0, layer=l, states=states)
        rec_s, _, _ = _hgrn(h, lbf, lbb, og, state_hgrn_fwd[:, l].astype(F32), state_hgrn_bwd[:, l].astype(F32),
                            DEC_SEQ, DEC_BATCH, T_PROMPT // DEC_SEQ)
        gm_bias = jnp.repeat(gmlp_b[l].T, GM_GDIM, axis=1)
        gmlp_params = (gmlp_vnorm_g[l][None, :], gmlp_ws[l].astype(BF16), gm_bias)

        x, h2a, h2b, rt, gate_pad, cnt = _outproj(l, att_p, att_s, rec_p, rec_s, h, gmlp_params, x_next,
                                                  mod_tiles[l],
                                                  norm2_g[l][None, :], w_out_bf16,
                                                  router_w[l], router_b[l][None, :])
        dest_flat, plan = _route(rt, cnt[0, :N_EXPERTS].astype(jnp.int32))
        x_sorted = _sc_scatter_rows((h2a, h2b), dest_flat, MOE_SLOTS)
        y_sorted = _moe(l, plan, x_sorted, w_gate, b_gate, w_up, b_up, w_down, b_down)
        y_tok = _sc_gather_rows(y_sorted, dest_flat)
        moe_out = ([yt.reshape(TOP_K, T_ALL, D_SLAB) for yt in y_tok], gate_pad)

    y_prompt, y_sample = _final(x, moe_out, mod_tiles[DEPTH - 1], final_g[None, :])
    y_prompt = y_prompt.reshape(BATCH, SEQ, D_MODEL)
    y_sample = y_sample.reshape(DEC_BATCH, DEC_SEQ, D_MODEL)
    new_k, new_v = (cache.reshape(BATCH, DEPTH, NA_HEADS, NA_HEAD_DIM, SEQ).transpose(0, 1, 4, 2, 3)
                    for cache in caches)
    new_sf, new_sb = (st.reshape(BATCH, DEPTH, HG_HEADS, HG_DK, HG_DV) for st in states)
    return (y_prompt, y_sample, new_k, new_v, new_sf, new_sb)
```

```python
import functools

import numpy as np
import jax
import jax.numpy as jnp
from jax import lax
from jax.experimental import pallas as pl
from jax.experimental.pallas import tpu as pltpu
from jax.experimental.pallas import tpu_sc as plsc

F32 = jnp.float32
BF16 = jnp.bfloat16

D_MODEL = 1024
BATCH = 32
SEQ = 256
DEPTH = 2
DEC_BATCH = 2
DEC_SEQ = 1024
PAST_LEN = 512
GRID_W = 64
NA_HEADS = 8
NA_HEAD_DIM = 64
NA_WIDTH = NA_HEADS * NA_HEAD_DIM
NA_KH = 8
NA_KW = 16
HG_HEADS = 4
HG_DK = 64
HG_DV = 64
HG_WIDTH = HG_HEADS * HG_DV
HG_CHUNK = 16
F_FLOOR = 1e-30
GM_GROUPS = 4
GM_GDIM = 64
GM_WIDTH = GM_GROUPS * GM_GDIM
GM_CHUNK = 128
IN_COLS = 3 * NA_WIDTH + 5 * HG_WIDTH + 2 * GM_WIDTH
N_EXPERTS = 32
TOP_K = 4
SWIGLU_LIMIT = 7.0
SWIGLU_ALPHA = 1.702
RMS_EPS = 1e-6
NEG_INF = -1e30

T_PROMPT = BATCH * SEQ
T_SAMPLE = DEC_BATCH * DEC_SEQ
T_ALL = T_PROMPT + T_SAMPLE
TM = 512
SEQ_PER_TILE = TM // SEQ
N_TILES = T_ALL // TM
P_TILES = T_PROMPT // TM
MOE_BM = 512
MOE_SLOTS = -(-(T_ALL * TOP_K + N_EXPERTS * (MOE_BM - 1)) // MOE_BM) * MOE_BM
MOE_BLOCKS = MOE_SLOTS // MOE_BM
MOD_ROWS = 8
V7X_VMEM_LIMIT = 48 * 1024 * 1024

QKV_COLS = 3 * NA_WIDTH
REST_COLS = IN_COLS - QKV_COLS
_CB_HQ, _CB_ZF, _CB_ZB, _CB_HI, _CB_HG, _CB_GU, _CB_GV = range(7)


def _dot(a, b):
    return jnp.dot(a, b, preferred_element_type=F32)


def _dot_nt(a, b):
    return lax.dot_general(a, b, (((1,), (1,)), ((), ())), preferred_element_type=F32)


def _dot_tn(a, b):
    return lax.dot_general(a, b, (((0,), (0,)), ((), ())), preferred_element_type=F32)


def _split3(x):
    hi = x.astype(BF16)
    r1 = x - hi.astype(F32)
    mid = r1.astype(BF16)
    lo = (r1 - mid.astype(F32)).astype(BF16)
    return hi, mid, lo


D_PACK = D_MODEL // 2
N_SPLIT = 2
D_SLAB = D_PACK // N_SPLIT


def _pack_halves(x):
    half = x.shape[1] // 2
    lo = pltpu.bitcast(x[:, :half].astype(BF16).astype(F32), jnp.uint32)
    hi = pltpu.bitcast(x[:, half:].astype(BF16).astype(F32), jnp.uint32)
    return pltpu.bitcast(jnp.right_shift(lo, jnp.uint32(16)) | hi, jnp.int32)


def _unpack_halves(w):
    u = pltpu.bitcast(w, jnp.uint32)
    lo = pltpu.bitcast(jnp.left_shift(u, jnp.uint32(16)), F32)
    hi = pltpu.bitcast(u & jnp.uint32(0xFFFF0000), F32)
    return lo, hi


def _load_slabs(refs, *lead):
    return jnp.concatenate([r[lead] if lead else r[...] for r in refs], axis=1)


def _store_slabs(refs, packed):
    for si, r in enumerate(refs):
        r[...] = packed[:, si * D_SLAB:(si + 1) * D_SLAB]


def _params(n_axes=1):
    return pltpu.CompilerParams(dimension_semantics=("arbitrary",) * n_axes,
                                vmem_limit_bytes=V7X_VMEM_LIMIT)


def _mod_kernel(cond_ref, w_ref, b_ref, o_ref):
    c = cond_ref[...]
    c = c * jax.nn.sigmoid(c)
    w = w_ref[0]
    c_hi = c.astype(BF16)
    c_lo = (c - c_hi.astype(F32)).astype(BF16)
    w_hi = w.astype(BF16)
    w_lo = (w - w_hi.astype(F32)).astype(BF16)
    o_ref[0] = _dot(c_hi, w_hi) + _dot(c_lo, w_hi) + _dot(c_hi, w_lo) + b_ref[0]


def _modulation(cond, w_mod, b_mod):
    tn = 1536
    return pl.pallas_call(
        _mod_kernel,
        out_shape=jax.ShapeDtypeStruct((DEPTH, MOD_ROWS, 6 * D_MODEL), F32),
        grid=(DEPTH, 6 * D_MODEL // tn),
        in_specs=[pl.BlockSpec((MOD_ROWS, D_MODEL), lambda l, j: (0, 0)),
                  pl.BlockSpec((1, D_MODEL, tn), lambda l, j: (l, 0, j)),
                  pl.BlockSpec((1, 1, tn), lambda l, j: (l, 0, j))],
        out_specs=pl.BlockSpec((1, MOD_ROWS, tn), lambda l, j: (l, 0, j)),
        compiler_params=_params(2),
        name="modulation",
    )(cond, w_mod, b_mod.reshape(DEPTH, 1, 6 * D_MODEL))


def _rms_mod(x, g, shift, scale):
    ms = jnp.mean(x * x, axis=-1, keepdims=True)
    y = x * lax.rsqrt(ms + RMS_EPS) * g
    return y * (1.0 + scale) + shift


def _project_in(hm, w_ref, qkv_ref, h_ref, kc_ref, vc_ref):
    h = _dot(hm.astype(BF16), w_ref[0])
    qkv_ref[...] = h[:, :QKV_COLS].astype(BF16)
    h_ref[...] = h[:, QKV_COLS:]

    @pl.when(pl.program_id(0) < P_TILES)
    def _():
        for sq in range(SEQ_PER_TILE):
            rows = slice(sq * SEQ, (sq + 1) * SEQ)
            kc_ref[sq, 0:NA_WIDTH] = h[rows, NA_WIDTH:2 * NA_WIDTH].T
            vc_ref[sq, 0:NA_WIDTH] = h[rows, 2 * NA_WIDTH:3 * NA_WIDTH].T
            if kc_ref.shape[1] > NA_WIDTH:
                kc_ref[sq, NA_WIDTH:] = jnp.zeros((kc_ref.shape[1] - NA_WIDTH, SEQ), F32)
                vc_ref[sq, NA_WIDTH:] = jnp.zeros((vc_ref.shape[1] - NA_WIDTH, SEQ), F32)


def _pick_group(p_ref, s_ref):
    return jnp.where(pl.program_id(0) < P_TILES, p_ref[...], s_ref[...])


def _p_tile(width):
    return pl.BlockSpec((TM, width), lambda i: (jnp.minimum(i, P_TILES - 1), 0))


def _s_tile(width):
    return pl.BlockSpec((TM, width), lambda i: (jnp.maximum(i - P_TILES, 0), 0))


def _inproj_first_kernel(xp_ref, xs_ref, mod_ref, g_ref, w_ref, qkv_ref, h_ref, kc_ref, vc_ref):
    x = _pick_group(xp_ref, xs_ref)
    hm = _rms_mod(x, g_ref[...], mod_ref[0, 0:1, :], mod_ref[0, 1:2, :])
    _project_in(hm, w_ref, qkv_ref, h_ref, kc_ref, vc_ref)


def _combine_experts(yg_refs, gate_ref):
    gates = gate_ref[...]
    lo_acc = hi_acc = None
    for kk in range(TOP_K):
        lo, hi = _unpack_halves(_load_slabs(yg_refs, kk))
        gk = gates[:, kk:kk + 1]
        lo_acc = gk * lo if lo_acc is None else lo_acc + gk * lo
        hi_acc = gk * hi if hi_acc is None else hi_acc + gk * hi
    return jnp.concatenate([lo_acc, hi_acc], axis=1)


def _inproj_next_kernel(x_ref, yga_ref, ygb_ref, gate_ref, pmod_ref, mod_ref, g_ref, w_ref, kc_in, vc_in,
                        qkv_ref, h_ref, xo_ref, kc_ref, vc_ref):
    del kc_in, vc_in
    x = x_ref[...] + pmod_ref[0, 5:6, :] * _combine_experts((yga_ref, ygb_ref), gate_ref)
    xo_ref[...] = x
    hm = _rms_mod(x, g_ref[...], mod_ref[0, 0:1, :], mod_ref[0, 1:2, :])
    _project_in(hm, w_ref, qkv_ref, h_ref, kc_ref, vc_ref)


_TILE_SPEC = pl.BlockSpec((TM, D_MODEL), lambda i: (i, 0))
_MOD_SPEC = pl.BlockSpec((1, MOD_ROWS, D_MODEL), lambda i: (i, 0, 0))
_ROW_SPEC = pl.BlockSpec((1, D_MODEL), lambda i: (0, 0))
_RT_LANES = 128
_RT_ROWS = 2 * TOP_K
_YG_SPEC = pl.BlockSpec((TOP_K, TM, D_SLAB), lambda i: (0, i, 0))
_GATE_SPEC = pl.BlockSpec((TM, _RT_LANES), lambda i: (i, 0))


def _inproj(layer, x, moe, prev_mod, mod, g, w_bf16, caches):
    w_spec = pl.BlockSpec((1, D_MODEL, IN_COLS), lambda i: (layer, 0, 0))
    h_spec = pl.BlockSpec((TM, REST_COLS), lambda i: (i, 0))
    h_shape = jax.ShapeDtypeStruct((T_ALL, REST_COLS), F32)
    q_spec = pl.BlockSpec((TM, QKV_COLS), lambda i: (i, 0))
    q_shape = jax.ShapeDtypeStruct((T_ALL, QKV_COLS), BF16)
    c_spec = pl.BlockSpec((SEQ_PER_TILE, NA_WIDTH, SEQ), lambda i: (jnp.minimum(i, P_TILES - 1), layer, 0))
    c_shape = jax.ShapeDtypeStruct((BATCH, DEPTH * NA_WIDTH, SEQ), F32)
    x_shape = jax.ShapeDtypeStruct((T_ALL, D_MODEL), F32)
    if moe is None:
        c_all = pl.BlockSpec((SEQ_PER_TILE, DEPTH * NA_WIDTH, SEQ), lambda i: (jnp.minimum(i, P_TILES - 1), 0, 0))
        qkv, h, kc, vc = pl.pallas_call(
            _inproj_first_kernel, out_shape=(q_shape, h_shape, c_shape, c_shape), grid=(N_TILES,),
            in_specs=[_p_tile(D_MODEL), _s_tile(D_MODEL), _MOD_SPEC, _ROW_SPEC, w_spec],
            out_specs=(q_spec, h_spec, c_all, c_all),
            compiler_params=_params(), name="inproj_first",
        )(*x, mod, g, w_bf16)
        return qkv, h, x, (kc, vc)
    qkv, h, x, kc, vc = pl.pallas_call(
        _inproj_next_kernel,
        out_shape=(q_shape, h_shape, x_shape, c_shape, c_shape),
        grid=(N_TILES,),
        in_specs=[_TILE_SPEC, _YG_SPEC, _YG_SPEC, _GATE_SPEC, _MOD_SPEC, _MOD_SPEC, _ROW_SPEC, w_spec,
                  pl.BlockSpec(memory_space=pl.ANY), pl.BlockSpec(memory_space=pl.ANY)],
        out_specs=(q_spec, h_spec, _TILE_SPEC, c_spec, c_spec),
        input_output_aliases={8: 3, 9: 4},
        compiler_params=_params(), name="inproj_next",
    )(x, *moe[0], moe[1], prev_mod, mod, g, w_bf16, *caches)
    return qkv, h, x, (kc, vc)


def _pair_mask(hh):
    lane = lax.broadcasted_iota(jnp.int32, (1, 2 * NA_HEAD_DIM), 1)
    return (lane >= hh * NA_HEAD_DIM) & (lane < (hh + 1) * NA_HEAD_DIM)


_ATT_SEQS = 4


def _stack_pair(qp):
    return jnp.concatenate([jnp.where(_pair_mask(hh), qp, jnp.zeros_like(qp)) for hh in range(2)], axis=0)


def _unstack_pair(o2):
    half = o2.shape[0] // 2
    return jnp.where(_pair_mask(0), o2[:half], o2[half:])


def _attn_prompt_kernel(q_ref, k_ref, v_ref, o_ref):
    scale = NA_HEAD_DIM ** -0.5
    for sq in range(_ATT_SEQS):
        rows = slice(sq * SEQ, (sq + 1) * SEQ)
        for p in range(NA_HEADS // 2):
            cols = slice(p * 128, (p + 1) * 128)
            q2 = _stack_pair(q_ref[rows, cols] * scale)
            s = _dot_nt(q2, k_ref[rows, cols])
            e = jnp.exp(s - jnp.max(s, axis=-1, keepdims=True))
            den = jnp.sum(e, axis=-1, keepdims=True)
            o_ref[rows, cols] = _unstack_pair(_dot(e.astype(BF16), v_ref[rows, cols]) / den).astype(o_ref.dtype)


def _attn_prompt(qkv):
    rows = _ATT_SEQS * SEQ
    return pl.pallas_call(
        _attn_prompt_kernel,
        out_shape=jax.ShapeDtypeStruct((T_PROMPT, NA_WIDTH), BF16),
        grid=(BATCH // _ATT_SEQS,),
        in_specs=[pl.BlockSpec((rows, NA_WIDTH), lambda b: (b, 0)),
                  pl.BlockSpec((rows, NA_WIDTH), lambda b: (b, 1)),
                  pl.BlockSpec((rows, NA_WIDTH), lambda b: (b, 2))],
        out_specs=pl.BlockSpec((rows, NA_WIDTH), lambda b: (b, 0)),
        compiler_params=_params(), name="attn_prompt",
    )(qkv, qkv, qkv)


_NA_ROWS = DEC_SEQ // GRID_W
_NA_LOC = NA_KH * GRID_W
_NA_STEP_ROWS = 4


def _na_window_start(r):
    return jnp.clip(r - NA_KH // 2, 0, _NA_ROWS - NA_KH)


def _attn_sample_kernel(q_ref, k_ref, v_ref, ck_ref, cv_ref, *rest):
    bias_refs, o_ref = rest[:_NA_STEP_ROWS], rest[_NA_STEP_ROWS]
    scale = NA_HEAD_DIM ** -0.5
    for p in range(NA_HEADS // 2):
        cols = slice(p * 128, (p + 1) * 128)
        kc = ck_ref[0, :, cols].astype(BF16)
        vc = cv_ref[0, :, cols].astype(BF16)
        for u in range(_NA_STEP_ROWS):
            rows = slice(u * GRID_W, (u + 1) * GRID_W)
            s0 = pl.multiple_of(_na_window_start(pl.program_id(1) * _NA_STEP_ROWS + u) * GRID_W, GRID_W)
            q2 = _stack_pair(q_ref[rows, cols] * scale)
            bias2 = jnp.concatenate([bias_refs[u][0, 0, 2 * p], bias_refs[u][0, 0, 2 * p + 1]], axis=0)
            sl = _dot_nt(q2, k_ref[pl.ds(s0, _NA_LOC), cols]) + bias2
            sc = _dot_nt(q2, kc)
            mx = jnp.maximum(jnp.max(sl, axis=-1, keepdims=True), jnp.max(sc, axis=-1, keepdims=True))
            el = jnp.exp(sl - mx)
            ec = jnp.exp(sc - mx)
            den = jnp.sum(el, axis=-1, keepdims=True) + jnp.sum(ec, axis=-1, keepdims=True)
            o2 = (_dot(el.astype(BF16), v_ref[pl.ds(s0, _NA_LOC), cols]) + _dot(ec.astype(BF16), vc)) / den
            o_ref[rows, cols] = _unstack_pair(o2).astype(o_ref.dtype)


def _attn_sample(layer, qkv, ck, cv, bias):
    q_rows = _NA_STEP_ROWS * GRID_W
    steps = _NA_ROWS // _NA_STEP_ROWS
    q_blk0 = T_PROMPT // q_rows
    kv_row0 = T_PROMPT // DEC_SEQ

    def bias_spec(u):
        def index(b, r2):
            r = r2 * _NA_STEP_ROWS + u
            return (layer, _na_window_start(r) - r + NA_KH - 1, 0, 0, 0)
        return pl.BlockSpec((1, 1, NA_HEADS, GRID_W, _NA_LOC), index)

    return pl.pallas_call(
        _attn_sample_kernel,
        out_shape=jax.ShapeDtypeStruct((T_SAMPLE, NA_WIDTH), BF16),
        grid=(DEC_BATCH, steps),
        in_specs=[pl.BlockSpec((q_rows, NA_WIDTH), lambda b, r2: (q_blk0 + b * steps + r2, 0)),
                  pl.BlockSpec((DEC_SEQ, NA_WIDTH), lambda b, r2: (kv_row0 + b, 1)),
                  pl.BlockSpec((DEC_SEQ, NA_WIDTH), lambda b, r2: (kv_row0 + b, 2)),
                  pl.BlockSpec((1, PAST_LEN, NA_WIDTH), lambda b, r2: (b, 0, 0)),
                  pl.BlockSpec((1, PAST_LEN, NA_WIDTH), lambda b, r2: (b, 0, 0))]
                 + [bias_spec(u) for u in range(_NA_STEP_ROWS)],
        out_specs=pl.BlockSpec((q_rows, NA_WIDTH), lambda b, r2: (b * steps + r2, 0)),
        compiler_params=_params(2), name="attn_sample",
    )(qkv, qkv, qkv, ck, cv, *([bias] * _NA_STEP_ROWS))


_NA_DR = 2 * NA_KH - 1
_NA_DC = 2 * NA_KW - 1


def _na_bias_kernel(rb_ref, o_ref):
    qc = lax.broadcasted_iota(jnp.int32, (GRID_W, GRID_W), 0)
    kc = lax.broadcasted_iota(jnp.int32, (GRID_W, GRID_W), 1)
    q_start = jnp.clip(qc - NA_KW // 2, 0, GRID_W - NA_KW)
    in_win = (kc >= q_start) & (kc < q_start + NA_KW)
    dc = jnp.clip(kc - qc + NA_KW - 1, 0, _NA_DC - 1)
    picks = [dc == d for d in range(_NA_DC)]

    def one_head(hh, carry):
        i = pl.program_id(0) * NA_HEADS + hh
        tiles = []
        for dr in range(_NA_DR):
            acc = jnp.zeros((GRID_W, GRID_W), F32)
            for d in range(_NA_DC):
                acc = jnp.where(picks[d], rb_ref[i, dr * _NA_DC + d], acc)
            tiles.append(jnp.where(in_win, acc, NEG_INF))
        for base in range(NA_KH):
            o_ref[0, base, hh] = jnp.concatenate(tiles[base:base + NA_KH], axis=1)
        return carry
    lax.fori_loop(0, NA_HEADS, one_head, 0)


def _na_bias_tables(rel_bias):
    rb = rel_bias.astype(F32).reshape(DEPTH * NA_HEADS, _NA_DR * _NA_DC)
    return pl.pallas_call(
        _na_bias_kernel,
        out_shape=jax.ShapeDtypeStruct((DEPTH, NA_KH, NA_HEADS, GRID_W, _NA_LOC), F32),
        grid=(DEPTH,),
        in_specs=[pl.BlockSpec(memory_space=pltpu.SMEM)],
        out_specs=pl.BlockSpec((1, NA_KH, NA_HEADS, GRID_W, _NA_LOC), lambda i: (i, 0, 0, 0, 0)),
        compiler_params=_params(), name="na_bias_tables",
    )(rb)


_HG_GROUP = 8
_HG_UNROLL = 2


def _hgrn_kernel(*refs, n_tok, has_state, n_alias):
    refs = refs[:8 + 2 * has_state] + refs[8 + 2 * has_state + n_alias:]
    if has_state:
        (q_ref, zf_ref, zb_ref, v_ref, g_ref, lbf_ref, lbb_ref, og_ref, s0f_ref, s0b_ref,
         rec_ref, sf_ref, sb_ref, kf_s, bf_s, kb_s, bb_s, of_s, ob_s, zf_s, zb_s, qsf_s, qsb_s, stf_s, stb_s) = refs
    else:
        (q_ref, zf_ref, zb_ref, v_ref, g_ref, lbf_ref, lbb_ref, og_ref,
         rec_ref, sf_ref, sb_ref, kf_s, bf_s, kb_s, bb_s, of_s, ob_s, zf_s, zb_s, qsf_s, qsb_s, stf_s, stb_s) = refs
        s0f_ref = s0b_ref = None
    C = HG_CHUNK
    W = HG_WIDTH
    n_chunks = n_tok // C
    rr = lax.broadcasted_iota(jnp.int32, (W, W), 0)
    cc = lax.broadcasted_iota(jnp.int32, (W, W), 1)
    log2_c = C.bit_length() - 1
    same_chunk = jnp.right_shift(rr, log2_c) == jnp.right_shift(cc, log2_c)
    tri_prefix = jnp.where(same_chunk & (cc <= rr), 1.0, 0.0).astype(BF16)
    tri_suffix = jnp.where(same_chunk & (cc >= rr), 1.0, 0.0).astype(BF16)
    same_head = jnp.right_shift(rr, 6) == jnp.right_shift(cc, 6)
    head_ones = jnp.where(same_head, 1.0, 0.0).astype(BF16)

    for ti in range(n_tok // W):
        rows = slice(ti * W, (ti + 1) * W)
        for z_ref, lb_ref, k_s, b_s, tri in ((zf_ref, lbf_ref, kf_s, bf_s, tri_prefix),
                                             (zb_ref, lbb_ref, kb_s, bb_s, tri_suffix)):
            z = z_ref[rows, :]
            lb = lb_ref[...]
            e = jnp.exp(-jnp.abs(z))
            big = 1.0 / (1.0 + e)
            small = e * big
            f = lb + (1.0 - lb) * jnp.where(z >= 0.0, big, small)
            logf = jnp.log(jnp.maximum(f, F_FLOOR))
            k_s[rows, :] = (1.0 - lb) * jnp.where(z >= 0.0, small, big)
            hi, mid, lo = _split3(logf)
            b_s[rows, :] = _dot(tri, hi) + _dot(tri, mid) + _dot(tri, lo)

    G = _HG_GROUP
    n_groups = C // G
    srow = lax.broadcasted_iota(jnp.int32, (G, W), 0)
    zf_s[...] = jnp.zeros_like(zf_s)
    zb_s[...] = jnp.zeros_like(zb_s)

    def scan_chunk(ci, k_s, b_s, z_s, ks_s, st_s, o_dir_s, fwd):
        c = ci if fwd else n_chunks - 1 - ci
        base = pl.multiple_of(c * C, C)
        q = q_ref[pl.ds(base, C), :]
        k = k_s[pl.ds(base, C), :]
        b = b_s[pl.ds(base, C), :]
        v = v_ref[pl.ds(base, C), :]
        q_far = {}
        for gs in range(n_groups):
            others = range(gs + 1, n_groups) if fwd else range(gs)
            if not others:
                continue
            rows_s = slice(gs * G, (gs + 1) * G)
            edge = (gs + 1) * G - 1 if fwd else gs * G
            b_edge = b[edge:edge + 1, :]
            ks_s[rows_s, :] = k[rows_s] * jnp.exp(b_edge - b[rows_s])
            for gt in others:
                rows_t = slice(gt * G, (gt + 1) * G)
                q_far[gs, gt] = q[rows_t] * jnp.exp(b[rows_t] - b_edge)
        for sx in range(C):
            gs = sx // G
            rows_g = slice(gs * G, (gs + 1) * G)
            k_row = k_s[pl.ds(base + sx, 1), :]
            b_row = b_s[pl.ds(base + sx, 1), :]
            keep = (srow + gs * G >= sx) if fwd else (srow + gs * G <= sx)
            z_s[sx * C + gs * G:sx * C + (gs + 1) * G, :] = jnp.where(
                keep, (k_row * q[rows_g]) * jnp.exp(b[rows_g] - b_row), 0.0)
            others = range(gs + 1, n_groups) if fwd else range(gs)
            if others:
                ks_row = ks_s[sx:sx + 1, :]
                for gt in others:
                    z_s[sx * C + gt * G:sx * C + (gt + 1) * G, :] = ks_row * q_far[gs, gt]
        a_rep = _dot(z_s[...].astype(BF16), head_ones)
        o_intra = jnp.sum(a_rep.reshape(C, C, W) * v[:, None, :], axis=0)
        b_end = b_s[pl.ds(base + (C - 1 if fwd else 0), 1), :]
        q_in = q * jnp.exp(b)
        k_st = k * jnp.exp(b_end - b)
        st = st_s[...]
        o_inter = _dot_nt(q_in.astype(BF16), st.astype(BF16))
        upd = _dot_tn(v.astype(BF16), k_st.astype(BF16))
        st_s[...] = st * jnp.exp(b_end) + jnp.where(same_head, upd, 0.0)
        o_dir_s[pl.ds(base, C), :] = o_intra + o_inter

    def load_state(s0_ref, st_s):
        if s0_ref is None:
            st_s[...] = jnp.zeros((W, W), F32)
            return
        for hh in range(HG_HEADS):
            parts = [s0_ref[0, hh] if g == hh else jnp.zeros((HG_DK, HG_DV), F32) for g in range(HG_HEADS)]
            st_s[hh * HG_DK:(hh + 1) * HG_DK, :] = jnp.concatenate(parts, axis=1)
        st_s[...] = st_s[...].T

    def store_state(st_s, out_ref):
        by_head = st_s[...].T
        for hh in range(HG_HEADS):
            out_ref[0, hh] = by_head[hh * HG_DK:(hh + 1) * HG_DK, hh * HG_DV:(hh + 1) * HG_DV]
        if out_ref.shape[1] > HG_HEADS:
            out_ref[0, HG_HEADS:] = jnp.zeros((out_ref.shape[1] - HG_HEADS, HG_DK, HG_DV), F32)

    load_state(s0f_ref, stf_s)
    load_state(s0b_ref, stb_s)

    def scan_both(ci, carry):
        for u in range(_HG_UNROLL):
            scan_chunk(ci * _HG_UNROLL + u, kf_s, bf_s, zf_s.at[u], qsf_s.at[u], stf_s, of_s, True)
            scan_chunk(ci * _HG_UNROLL + u, kb_s, bb_s, zb_s.at[u], qsb_s.at[u], stb_s, ob_s, False)
        return carry
    lax.fori_loop(0, n_chunks // _HG_UNROLL, scan_both, 0)
    store_state(stf_s, sf_ref)
    store_state(stb_s, sb_ref)

    for ti in range(n_tok // W):
        rows = slice(ti * W, (ti + 1) * W)
        o = of_s[rows, :] + ob_s[rows, :]
        sq = o * o
        sq_hi = sq.astype(BF16)
        sq_lo = (sq - sq_hi.astype(F32)).astype(BF16)
        ms = (_dot(sq_hi, head_ones) + _dot(sq_lo, head_ones)) * (1.0 / HG_DV)
        g = g_ref[rows, :]
        y = o * lax.rsqrt(ms + RMS_EPS) * og_ref[...] * (g * jax.nn.sigmoid(g))
        rec_ref[rows, :] = y.astype(rec_ref.dtype)


def _hgrn(h, lbf, lbb, og, s0f, s0b, n_tok, n_seq, row0, layer=None, states=None):
    W = HG_WIDTH
    has_state = s0f is not None

    def col(cb):
        return pl.BlockSpec((n_tok, W), lambda i, cb=cb: (row0 + i, cb))

    vec = pl.BlockSpec((1, W), lambda i: (0, 0))
    st_spec = pl.BlockSpec((1, HG_HEADS, HG_DK, HG_DV), lambda i: (i, 0, 0, 0))
    in_specs = [col(_CB_HQ), col(_CB_ZF), col(_CB_ZB), col(_CB_HI), col(_CB_HG), vec, vec, vec]
    args = [h, h, h, h, h, lbf, lbb, og]
    if has_state:
        in_specs += [st_spec, st_spec]
        args += [s0f, s0b]
    seq_f32 = pltpu.VMEM((n_tok, W), F32)
    out_st_spec, st_rows, aliases = st_spec, HG_HEADS, {}
    if layer is not None:
        st_rows = DEPTH * HG_HEADS
        if states is None:
            out_st_spec = pl.BlockSpec((1, st_rows, HG_DK, HG_DV), lambda i: (i, 0, 0, 0))
        else:
            out_st_spec = pl.BlockSpec((1, HG_HEADS, HG_DK, HG_DV), lambda i: (i, layer, 0, 0))
            aliases = {len(args): 1, len(args) + 1: 2}
            in_specs += [pl.BlockSpec(memory_space=pl.ANY)] * 2
            args += list(states)
    return pl.pallas_call(
        functools.partial(_hgrn_kernel, n_tok=n_tok, has_state=has_state, n_alias=len(aliases)),
        out_shape=(jax.ShapeDtypeStruct((n_seq * n_tok, W), BF16),
                   jax.ShapeDtypeStruct((n_seq, st_rows, HG_DK, HG_DV), F32),
                   jax.ShapeDtypeStruct((n_seq, st_rows, HG_DK, HG_DV), F32)),
        grid=(n_seq,),
        in_specs=in_specs,
        input_output_aliases=aliases,
        out_specs=(pl.BlockSpec((n_tok, W), lambda i: (i, 0)), out_st_spec, out_st_spec),
        scratch_shapes=[seq_f32, seq_f32, seq_f32, seq_f32, seq_f32, seq_f32,
                        pltpu.VMEM((_HG_UNROLL, HG_CHUNK * HG_CHUNK, W), F32),
                        pltpu.VMEM((_HG_UNROLL, HG_CHUNK * HG_CHUNK, W), F32),
                        pltpu.VMEM((_HG_UNROLL, HG_CHUNK, W), F32),
                        pltpu.VMEM((_HG_UNROLL, HG_CHUNK, W), F32),
                        pltpu.VMEM((W, W), F32),
                        pltpu.VMEM((W, W), F32)],
        compiler_params=_params(), name="hgrn_state" if has_state else "hgrn_zero",
    )(*args)


def _gmlp_tile(u_ref, v_ref, g_ref, ws_ref, b_ref):
    lane = lax.broadcasted_iota(jnp.int32, (1, GM_WIDTH), 1)
    outs = []
    for ci in range(TM // GM_CHUNK):
        rows = slice(ci * GM_CHUNK, (ci + 1) * GM_CHUNK)
        v = v_ref[rows, :]
        ms = jnp.mean(v * v, axis=-1, keepdims=True)
        vn = (v * lax.rsqrt(ms + RMS_EPS) * g_ref[...]).astype(BF16)
        z = b_ref[...]
        for gi in range(GM_GROUPS):
            zg = _dot(ws_ref[gi], vn)
            in_group = (lane >= gi * GM_GDIM) & (lane < (gi + 1) * GM_GDIM)
            z = z + jnp.where(in_group, zg, 0.0)
        outs.append((u_ref[rows, :] * z).astype(BF16))
    return jnp.concatenate(outs, axis=0)


def _outproj_kernel(attp_ref, atts_ref, recp_ref, recs_ref, gu_ref, gv_ref, gg_ref, gws_ref, gb_ref,
                    xp_ref, xs_ref, mod_ref, g_ref, w_ref, wr_ref, br_ref,
                    x1_ref, h2a_ref, h2b_ref, rt_ref, gate_ref, cnt_ref):
    @pl.when(pl.program_id(0) == 0)
    def _():
        cnt_ref[...] = jnp.zeros_like(cnt_ref)

    out = (_dot(_pick_group(attp_ref, atts_ref), w_ref[0, 0:NA_WIDTH, :])
           + _dot(_pick_group(recp_ref, recs_ref), w_ref[0, NA_WIDTH:NA_WIDTH + HG_WIDTH, :])
           + _dot(_gmlp_tile(gu_ref, gv_ref, gg_ref, gws_ref, gb_ref), w_ref[0, NA_WIDTH + HG_WIDTH:, :]))
    x1 = _pick_group(xp_ref, xs_ref) + mod_ref[0, 2:3, :] * out
    x1_ref[...] = x1
    h2 = _rms_mod(x1, g_ref[...], mod_ref[0, 3:4, :], mod_ref[0, 4:5, :])
    _store_slabs((h2a_ref, h2b_ref), _pack_halves(h2))
    h_hi = h2.astype(BF16)
    h_lo = (h2 - h_hi.astype(F32)).astype(BF16)
    wr = wr_ref[...]
    w_hi = wr.astype(BF16)
    w_lo = (wr - w_hi.astype(F32)).astype(BF16)
    logits = _dot(h_hi, w_hi) + _dot(h_lo, w_hi) + _dot(h_hi, w_lo) + br_ref[...]
    lane_e = lax.broadcasted_iota(jnp.int32, (TM, N_EXPERTS), 1).astype(F32)
    lane_o = lax.broadcasted_iota(jnp.int32, (TM, _RT_LANES), 1)
    idx_acc = jnp.zeros((TM, _RT_LANES), F32)
    val_acc = jnp.zeros((TM, _RT_LANES), F32)
    top0 = None
    den = jnp.zeros((TM, 1), F32)
    work = logits
    picks = []
    for kk in range(TOP_K):
        m = jnp.max(work, axis=-1, keepdims=True)
        first = jnp.min(jnp.where(work == m, lane_e, float(N_EXPERTS)), axis=-1, keepdims=True)
        if kk == 0:
            top0 = m
        e = jnp.exp(m - top0)
        den = den + e
        idx_acc = jnp.where(lane_o == kk, first, idx_acc)
        val_acc = jnp.where(lane_o == kk, e, val_acc)
        picks.append(lane_e == first)
        work = jnp.where(picks[-1], -jnp.inf, work)
    gate_ref[...] = val_acc / den
    sel = jnp.zeros((TM, N_EXPERTS), F32)
    for pk in picks:
        sel = sel + jnp.where(pk, 1.0, 0.0)
    rr = lax.broadcasted_iota(jnp.int32, (TM, TM), 0)
    cc = lax.broadcasted_iota(jnp.int32, (TM, TM), 1)
    earlier = jnp.where(cc < rr, 1.0, 0.0).astype(BF16)
    seen = cnt_ref[0:1, 0:N_EXPERTS]
    before = _dot(earlier, sel.astype(BF16)) + seen
    for kk, pk in enumerate(picks):
        rank = jnp.sum(jnp.where(pk, before, 0.0), axis=-1, keepdims=True)
        idx_acc = jnp.where(lane_o == TOP_K + kk, rank, idx_acc)
    rt_ref[...] = idx_acc.T[0:_RT_ROWS, :].astype(jnp.int32)
    cnt_ref[0:1, 0:N_EXPERTS] = seen + jnp.sum(sel, axis=0, keepdims=True)


def _outproj(layer, att_p, att_s, rec_p, rec_s, h, gmlp_params, x, mod, g, w_bf16, wr, br):
    def tile(width):
        return pl.BlockSpec((TM, width), lambda i: (i, 0))

    if isinstance(x, tuple):
        x_args, x_specs = x, [_p_tile(D_MODEL), _s_tile(D_MODEL)]
    else:
        x_args = (x, x)
        x_specs = [_p_tile(D_MODEL), pl.BlockSpec((TM, D_MODEL), lambda i: (jnp.maximum(i, P_TILES), 0))]
    return pl.pallas_call(
        _outproj_kernel,
        out_shape=(jax.ShapeDtypeStruct((T_ALL, D_MODEL), F32),
                   jax.ShapeDtypeStruct((T_ALL, D_SLAB), jnp.int32),
                   jax.ShapeDtypeStruct((T_ALL, D_SLAB), jnp.int32),
                   jax.ShapeDtypeStruct((_RT_ROWS, T_ALL), jnp.int32),
                   jax.ShapeDtypeStruct((T_ALL, _RT_LANES), F32),
                   jax.ShapeDtypeStruct((8, _RT_LANES), F32)),
        grid=(N_TILES,),
        in_specs=[_p_tile(NA_WIDTH), _s_tile(NA_WIDTH), _p_tile(HG_WIDTH), _s_tile(HG_WIDTH),
                  pl.BlockSpec((TM, GM_WIDTH), lambda i: (i, _CB_GU)), pl.BlockSpec((TM, GM_WIDTH), lambda i: (i, _CB_GV)),
                  pl.BlockSpec((1, GM_WIDTH), lambda i: (0, 0)),
                  pl.BlockSpec((GM_GROUPS, GM_CHUNK, GM_CHUNK), lambda i: (0, 0, 0)),
                  pl.BlockSpec((GM_CHUNK, GM_WIDTH), lambda i: (0, 0)),
                  *x_specs, _MOD_SPEC, _ROW_SPEC,
                  pl.BlockSpec((1, D_MODEL, D_MODEL), lambda i: (layer, 0, 0)),
                  pl.BlockSpec((D_MODEL, N_EXPERTS), lambda i: (0, 0)),
                  pl.BlockSpec((1, N_EXPERTS), lambda i: (0, 0))],
        out_specs=(_TILE_SPEC, tile(D_SLAB), tile(D_SLAB), pl.BlockSpec((_RT_ROWS, TM), lambda i: (0, i)),
                   tile(_RT_LANES),
                   pl.BlockSpec((8, _RT_LANES), lambda i: (0, 0))),
        compiler_params=_params(), name="outproj_router",
    )(att_p, att_s, rec_p, rec_s, h, h, *gmlp_params, *x_args, mod, g, w_bf16, wr, br)


_W_CHUNKS = 4
_W_CAST_ROWS = 128
_W_DMA_PRIORITY = 1


def _moe_kernel(blk_e_ref, blk_on_ref, blk_new_ref, blk_next_ref,
                xa_ref, xb_ref, wg_hbm, bg_ref, wu_hbm, bu_ref, wd_hbm, bd_ref,
                ya_ref, yb_ref, w_f32, w_bf16, w_sem, *, layer):
    j = pl.program_id(0)

    def weight_copies(expert):
        rows = D_MODEL // _W_CHUNKS
        return [pltpu.make_async_copy(w_hbm.at[layer, expert, pl.ds(ci * rows, rows)],
                                      w_f32.at[wi, pl.ds(ci * rows, rows)], w_sem.at[wi, ci])
                for wi, w_hbm in enumerate((wg_hbm, wu_hbm, wd_hbm)) for ci in range(_W_CHUNKS)]

    @pl.when(j == 0)
    def _():
        for cp in weight_copies(blk_e_ref[0]):
            cp.start(priority=_W_DMA_PRIORITY)

    @pl.when(blk_new_ref[j] != 0)
    def _():
        for cp in weight_copies(blk_e_ref[j]):
            cp.wait()

        def cast_rows(ci, carry):
            rows = pl.ds(pl.multiple_of(ci * _W_CAST_ROWS, _W_CAST_ROWS), _W_CAST_ROWS)
            for wi in range(3):
                w_bf16[wi, rows, :] = w_f32[wi, rows, :].astype(BF16)
            return carry
        lax.fori_loop(0, D_MODEL // _W_CAST_ROWS, cast_rows, 0)

        @pl.when(blk_next_ref[j] >= 0)
        def _():
            for cp in weight_copies(blk_next_ref[j]):
                cp.start(priority=_W_DMA_PRIORITY)

    @pl.when(blk_on_ref[j] != 0)
    def _():
        lo, hi = _unpack_halves(_load_slabs((xa_ref, xb_ref)))
        x = jnp.concatenate([lo.astype(BF16), hi.astype(BF16)], axis=1)
        gate = jnp.minimum(_dot(x, w_bf16[0]) + bg_ref[0, 0], SWIGLU_LIMIT)
        up = jnp.clip(_dot(x, w_bf16[1]) + bu_ref[0, 0], -SWIGLU_LIMIT, SWIGLU_LIMIT)
        glu = gate * jax.nn.sigmoid(SWIGLU_ALPHA * gate)
        act = ((up + 1.0) * glu).astype(BF16)
        _store_slabs((ya_ref, yb_ref), _pack_halves(_dot(act, w_bf16[2]) + bd_ref[0, 0]))

    @pl.when(blk_on_ref[j] == 0)
    def _():
        ya_ref[...] = jnp.zeros_like(ya_ref)
        yb_ref[...] = jnp.zeros_like(yb_ref)


def _moe(layer, plan, x_sorted, wg, bg, wu, bu, wd, bd):
    n_plan = len(plan)
    b_spec = pl.BlockSpec((1, 1, 1, D_MODEL), lambda j, be, *_: (layer, be[j], 0, 0))
    x_spec = pl.BlockSpec((MOE_BM, D_SLAB), lambda j, *_: (j, 0))
    hbm = pl.BlockSpec(memory_space=pl.ANY)
    bias4 = lambda b: b.reshape(DEPTH, N_EXPERTS, 1, D_MODEL)
    return pl.pallas_call(
        functools.partial(_moe_kernel, layer=layer),
        out_shape=(jax.ShapeDtypeStruct((MOE_SLOTS, D_SLAB), jnp.int32),) * N_SPLIT,
        grid_spec=pltpu.PrefetchScalarGridSpec(
            num_scalar_prefetch=n_plan, grid=(MOE_BLOCKS,),
            in_specs=[x_spec, x_spec, hbm, b_spec, hbm, b_spec, hbm, b_spec],
            out_specs=(x_spec, x_spec),
            scratch_shapes=[pltpu.VMEM((3, D_MODEL, D_MODEL), F32), pltpu.VMEM((3, D_MODEL, D_MODEL), BF16),
                            pltpu.SemaphoreType.DMA((3, _W_CHUNKS))]),
        compiler_params=_params(), name="moe_experts",
    )(*plan, *x_sorted, wg, bias4(bg), wu, bias4(bu), wd, bias4(bd))


def _route(rt, counts):
    experts = jnp.arange(N_EXPERTS, dtype=jnp.int32)
    nblk = (counts + MOE_BM - 1) // MOE_BM
    blk_end = jnp.cumsum(nblk)
    row0 = (blk_end - nblk) * MOE_BM
    top_i, rank = rt[:TOP_K], rt[TOP_K:]
    start_of = jnp.sum(jnp.where(top_i[None] == experts[:, None, None], row0[:, None, None], 0), axis=0)
    dest = (start_of + rank).reshape(1, TOP_K * T_ALL)
    live = counts > 0
    last_live = jnp.max(jnp.where(live, experts, 0))
    later_live = live[None, :] & (experts[None, :] > experts[:, None])
    next_live = jnp.min(jnp.where(later_live, experts[None, :], N_EXPERTS), axis=1)
    next_live = jnp.where(next_live == N_EXPERTS, -1, next_live)
    blk = jnp.arange(MOE_BLOCKS, dtype=jnp.int32)
    blk_on = blk < blk_end[-1]
    blk_e = jnp.where(blk_on, jnp.minimum(jnp.sum((blk_end[None, :] <= blk[:, None]).astype(jnp.int32), axis=1),
                                          N_EXPERTS - 1), last_live)
    blk_new = blk_on & jnp.concatenate([jnp.ones((1,), bool), blk_e[1:] != blk_e[:-1]])
    is_e = blk_e[:, None] == experts[None, :]
    lookup = lambda table: jnp.sum(jnp.where(is_e, table[None, :], 0), axis=1)
    plan = (blk_e, blk_on, blk_new, lookup(next_live))
    return dest.astype(jnp.int32), tuple(p.astype(jnp.int32) for p in plan)


_SC_WINDOW = 128


def _sc_mesh():
    return plsc.VectorSubcoreMesh(core_axis_name="core", subcore_axis_name="subcore")


def _sc_scatter_rows(srcs, idx, n_out):
    n_src, width = srcs[0].shape
    n_rep = idx.shape[1] // n_src
    src_windows = n_src // _SC_WINDOW
    assert len(srcs) == 2

    def body(*refs):
        x_hbm = refs[:len(srcs)]
        i_hbm = refs[len(srcs)]
        o_hbm = refs[len(srcs) + 1:]

        def run(xs, os_):
            def step(x_vmem, *i_vmem):
                for iv in i_vmem:
                    pltpu.sync_copy(x_vmem, os_.at[iv.at[0]])

            pltpu.emit_pipeline(
                step, grid=(src_windows,),
                in_specs=[pl.BlockSpec((_SC_WINDOW, width), lambda i: (i, 0))]
                         + [pl.BlockSpec((1, _SC_WINDOW), lambda i, kk=kk: (0, kk * src_windows + i))
                            for kk in range(n_rep)],
                out_specs=[],
                core_axis_name="subcore",
                dimension_semantics=(pltpu.PARALLEL,),
            )(xs, *([i_hbm] * n_rep))

        for ci, (xs, os_) in enumerate(zip(x_hbm, o_hbm)):
            pl.when(lax.axis_index("core") == ci)(functools.partial(run, xs, os_))

    out_type = tuple(jax.ShapeDtypeStruct((n_out, width), s.dtype) for s in srcs)
    return pl.kernel(body, out_type=out_type, mesh=_sc_mesh(), scratch_types=[],
                     name="sc_scatter_rows")(*srcs, idx)


def _sc_gather_rows(tables, idx):
    n_idx = idx.shape[1]
    width = tables[0].shape[1]

    def body(*refs):
        t_hbm = refs[:len(tables)]
        i_hbm = refs[len(tables)]
        o_hbm = refs[len(tables) + 1:]
        for ts, os_ in zip(t_hbm, o_hbm):
            def step(i_vmem, o_vmem, ts=ts):
                pltpu.sync_copy(ts.at[i_vmem.at[0]], o_vmem)

            pltpu.emit_pipeline(
                step, grid=(n_idx // _SC_WINDOW,),
                in_specs=[pl.BlockSpec((1, _SC_WINDOW), lambda i: (0, i))],
                out_specs=[pl.BlockSpec((_SC_WINDOW, width), lambda i: (i, 0))],
                core_axis_name=("core", "subcore"),
                dimension_semantics=(pltpu.PARALLEL,),
            )(i_hbm, os_)

    out_type = tuple(jax.ShapeDtypeStruct((n_idx, width), t.dtype) for t in tables)
    return pl.kernel(body, out_type=out_type, mesh=_sc_mesh(), scratch_types=[],
                     name="sc_gather_rows")(*tables, idx)


def _final_kernel(x_ref, yga_ref, ygb_ref, gate_ref, mod_ref, g_ref, yp_ref, ys_ref):
    x = x_ref[...] + mod_ref[0, 5:6, :] * _combine_experts((yga_ref, ygb_ref), gate_ref)
    ms = jnp.mean(x * x, axis=-1, keepdims=True)
    y = x * lax.rsqrt(ms + RMS_EPS) * g_ref[...]

    @pl.when(pl.program_id(0) < P_TILES)
    def _():
        yp_ref[...] = y

    @pl.when(pl.program_id(0) >= P_TILES)
    def _():
        ys_ref[...] = y


def _final(x, moe, mod, g):
    return pl.pallas_call(
        _final_kernel,
        out_shape=(jax.ShapeDtypeStruct((T_PROMPT, D_MODEL), F32), jax.ShapeDtypeStruct((T_SAMPLE, D_MODEL), F32)),
        grid=(N_TILES,),
        in_specs=[_TILE_SPEC, _YG_SPEC, _YG_SPEC, _GATE_SPEC, _MOD_SPEC, _ROW_SPEC],
        out_specs=(_p_tile(D_MODEL), _s_tile(D_MODEL)),
        compiler_params=_params(), name="final_norm",
    )(x, *moe[0], moe[1], mod, g)


def kernel(x_prompt, x_sample, cache_k, cache_v, state_hgrn_fwd, state_hgrn_bwd, c, c_ctx, w_mod, b_mod, norm1_g, norm2_g, w_in, na_rel_bias, hgrn_lb, hgrn_onorm_g, gmlp_vnorm_g, gmlp_ws, gmlp_b, w_out, router_w, router_b, w_gate, b_gate, w_up, b_up, w_down, b_down, final_g):
    x = (x_prompt.reshape(T_PROMPT, D_MODEL), x_sample.reshape(T_SAMPLE, D_MODEL))

    cond = jnp.zeros((MOD_ROWS, D_MODEL), F32).at[0].set(c_ctx).at[1:1 + DEC_BATCH].set(c)
    mod = _modulation(cond, w_mod, b_mod)
    tile_row = np.concatenate([np.zeros(P_TILES, np.int32),
                               1 + np.arange(N_TILES - P_TILES, dtype=np.int32) // (DEC_SEQ // TM)])
    mod_tiles = mod[:, tile_row].reshape(DEPTH, N_TILES, 6, D_MODEL)
    mod_tiles = jnp.pad(mod_tiles, ((0, 0), (0, 0), (0, MOD_ROWS - 6), (0, 0)))

    lb_soft = jax.nn.softmax(hgrn_lb.astype(F32), axis=1)
    lower = jnp.cumsum(lb_soft, axis=1) - lb_soft[:, :1]

    na_bias = _na_bias_tables(na_rel_bias)
    w_in_bf16 = w_in.astype(BF16)
    w_out_bf16 = w_out.astype(BF16)

    moe_out = caches = states = None
    for l in range(DEPTH):
        qkv, h, x_next, caches = _inproj(l, x, moe_out, mod_tiles[l - 1] if l else None, mod_tiles[l],
                                    norm1_g[l][None, :], w_in_bf16, caches)

        att_p = _attn_prompt(qkv)
        att_s = _attn_sample(l, qkv, cache_k[:, l].reshape(DEC_BATCH, PAST_LEN, NA_WIDTH),
                             cache_v[:, l].reshape(DEC_BATCH, PAST_LEN, NA_WIDTH), na_bias)
        lbf = lower[0, l][None, :]
        lbb = lower[1, l][None, :]
        og = jnp.tile(hgrn_onorm_g[l], HG_HEADS)[None, :]
        rec_p, *states = _hgrn(h, lbf, lbb, og, None, None, SEQ, BATCH, 0, layer=l, states=states)
        rec_s, _, _ = _hgrn(h, lbf, lbb, og, state_hgrn_fwd[:, l].astype(F32), state_hgrn_bwd[:, l].astype(F32),
                            DEC_SEQ, DEC_BATCH, T_PROMPT // DEC_SEQ)
        gm_bias = jnp.repeat(gmlp_b[l].T, GM_GDIM, axis=1)
        gmlp_params = (gmlp_vnorm_g[l][None, :], gmlp_ws[l].astype(BF16), gm_bias)

        x, h2a, h2b, rt, gate_pad, cnt = _outproj(l, att_p, att_s, rec_p, rec_s, h, gmlp_params, x_next,
                                                  mod_tiles[l],
                                                  norm2_g[l][None, :], w_out_bf16,
                                                  router_w[l], router_b[l][None, :])
        dest_flat, plan = _route(rt, cnt[0, :N_EXPERTS].astype(jnp.int32))
        x_sorted = _sc_scatter_rows((h2a, h2b), dest_flat, MOE_SLOTS)
        y_sorted = _moe(l, plan, x_sorted, w_gate, b_gate, w_up, b_up, w_down, b_down)
        y_tok = _sc_gather_rows(y_sorted, dest_flat)
        moe_out = ([yt.reshape(TOP_K, T_ALL, D_SLAB) for yt in y_tok], gate_pad)

    y_prompt, y_sample = _final(x, moe_out, mod_tiles[DEPTH - 1], final_g[None, :])
    y_prompt = y_prompt.reshape(BATCH, SEQ, D_MODEL)
    y_sample = y_sample.reshape(DEC_BATCH, DEC_SEQ, D_MODEL)
    new_k, new_v = (cache.reshape(BATCH, DEPTH, NA_HEADS, NA_HEAD_DIM, SEQ).transpose(0, 1, 4, 2, 3)
                    for cache in caches)
    new_sf, new_sb = (st.reshape(BATCH, DEPTH, HG_HEADS, HG_DK, HG_DV) for st in states)
    return (y_prompt, y_sample, new_k, new_v, new_sf, new_sb)
```

```python
import functools

import numpy as np
import jax
import jax.numpy as jnp
from jax import lax
from jax.experimental import pallas as pl
from jax.experimental.pallas import tpu as pltpu
from jax.experimental.pallas import tpu_sc as plsc

F32 = jnp.float32
BF16 = jnp.bfloat16

D_MODEL = 1024
BATCH = 32
SEQ = 256
DEPTH = 2
DEC_BATCH = 2
DEC_SEQ = 1024
PAST_LEN = 512
GRID_W = 64
NA_HEADS = 8
NA_HEAD_DIM = 64
NA_WIDTH = NA_HEADS * NA_HEAD_DIM
NA_KH = 8
NA_KW = 16
HG_HEADS = 4
HG_DK = 64
HG_DV = 64
HG_WIDTH = HG_HEADS * HG_DV
HG_CHUNK = 16
F_FLOOR = 1e-30
GM_GROUPS = 4
GM_GDIM = 64
GM_WIDTH = GM_GROUPS * GM_GDIM
GM_CHUNK = 128
IN_COLS = 3 * NA_WIDTH + 5 * HG_WIDTH + 2 * GM_WIDTH
N_EXPERTS = 32
TOP_K = 4
SWIGLU_LIMIT = 7.0
SWIGLU_ALPHA = 1.702
RMS_EPS = 1e-6
NEG_INF = -1e30

T_PROMPT = BATCH * SEQ
T_SAMPLE = DEC_BATCH * DEC_SEQ
T_ALL = T_PROMPT + T_SAMPLE
TM = 512
SEQ_PER_TILE = TM // SEQ
N_TILES = T_ALL // TM
P_TILES = T_PROMPT // TM
MOE_BM = 512
MOE_SLOTS = -(-(T_ALL * TOP_K + N_EXPERTS * (MOE_BM - 1)) // MOE_BM) * MOE_BM
MOE_BLOCKS = MOE_SLOTS // MOE_BM
MOD_ROWS = 8
V7X_VMEM_LIMIT = 48 * 1024 * 1024

QKV_COLS = 3 * NA_WIDTH
REST_COLS = IN_COLS - QKV_COLS
_CB_HQ, _CB_ZF, _CB_ZB, _CB_HI, _CB_HG, _CB_GU, _CB_GV = range(7)


def _dot(a, b):
    return jnp.dot(a, b, preferred_element_type=F32)


def _dot_nt(a, b):
    return lax.dot_general(a, b, (((1,), (1,)), ((), ())), preferred_element_type=F32)


def _dot_tn(a, b):
    return lax.dot_general(a, b, (((0,), (0,)), ((), ())), preferred_element_type=F32)


def _split3(x):
    hi = x.astype(BF16)
    r1 = x - hi.astype(F32)
    mid = r1.astype(BF16)
    lo = (r1 - mid.astype(F32)).astype(BF16)
    return hi, mid, lo


D_PACK = D_MODEL // 2
N_SPLIT = 2
D_SLAB = D_PACK // N_SPLIT


def _pack_halves(x):
    half = x.shape[1] // 2
    lo = pltpu.bitcast(x[:, :half].astype(BF16).astype(F32), jnp.uint32)
    hi = pltpu.bitcast(x[:, half:].astype(BF16).astype(F32), jnp.uint32)
    return pltpu.bitcast(jnp.right_shift(lo, jnp.uint32(16)) | hi, jnp.int32)


def _unpack_halves(w):
    u = pltpu.bitcast(w, jnp.uint32)
    lo = pltpu.bitcast(jnp.left_shift(u, jnp.uint32(16)), F32)
    hi = pltpu.bitcast(u & jnp.uint32(0xFFFF0000), F32)
    return lo, hi


def _load_slabs(refs, *lead):
    return jnp.concatenate([r[lead] if lead else r[...] for r in refs], axis=1)


def _store_slabs(refs, packed):
    for si, r in enumerate(refs):
        r[...] = packed[:, si * D_SLAB:(si + 1) * D_SLAB]


def _params(n_axes=1):
    return pltpu.CompilerParams(dimension_semantics=("arbitrary",) * n_axes,
                                vmem_limit_bytes=V7X_VMEM_LIMIT)


def _mod_kernel(cond_ref, w_ref, b_ref, o_ref):
    c = cond_ref[...]
    c = c * jax.nn.sigmoid(c)
    w = w_ref[0]
    c_hi = c.astype(BF16)
    c_lo = (c - c_hi.astype(F32)).astype(BF16)
    w_hi = w.astype(BF16)
    w_lo = (w - w_hi.astype(F32)).astype(BF16)
    o_ref[0] = _dot(c_hi, w_hi) + _dot(c_lo, w_hi) + _dot(c_hi, w_lo) + b_ref[0]


def _modulation(cond, w_mod, b_mod):
    tn = 1536
    return pl.pallas_call(
        _mod_kernel,
        out_shape=jax.ShapeDtypeStruct((DEPTH, MOD_ROWS, 6 * D_MODEL), F32),
        grid=(DEPTH, 6 * D_MODEL // tn),
        in_specs=[pl.BlockSpec((MOD_ROWS, D_MODEL), lambda l, j: (0, 0)),
                  pl.BlockSpec((1, D_MODEL, tn), lambda l, j: (l, 0, j)),
                  pl.BlockSpec((1, 1, tn), lambda l, j: (l, 0, j))],
        out_specs=pl.BlockSpec((1, MOD_ROWS, tn), lambda l, j: (l, 0, j)),
        compiler_params=_params(2),
        name="modulation",
    )(cond, w_mod, b_mod.reshape(DEPTH, 1, 6 * D_MODEL))


def _rms_mod(x, g, shift, scale):
    ms = jnp.mean(x * x, axis=-1, keepdims=True)
    y = x * lax.rsqrt(ms + RMS_EPS) * g
    return y * (1.0 + scale) + shift


def _project_in(hm, w_ref, qkv_ref, h_ref, kc_ref, vc_ref):
    h = _dot(hm.astype(BF16), w_ref[0])
    qkv_ref[...] = h[:, :QKV_COLS].astype(BF16)
    h_ref[...] = h[:, QKV_COLS:]

    @pl.when(pl.program_id(0) < P_TILES)
    def _():
        for sq in range(SEQ_PER_TILE):
            rows = slice(sq * SEQ, (sq + 1) * SEQ)
            kc_ref[sq, 0:NA_WIDTH] = h[rows, NA_WIDTH:2 * NA_WIDTH].T
            vc_ref[sq, 0:NA_WIDTH] = h[rows, 2 * NA_WIDTH:3 * NA_WIDTH].T
            if kc_ref.shape[1] > NA_WIDTH:
                kc_ref[sq, NA_WIDTH:] = jnp.zeros((kc_ref.shape[1] - NA_WIDTH, SEQ), F32)
                vc_ref[sq, NA_WIDTH:] = jnp.zeros((vc_ref.shape[1] - NA_WIDTH, SEQ), F32)


def _pick_group(p_ref, s_ref):
    return jnp.where(pl.program_id(0) < P_TILES, p_ref[...], s_ref[...])


def _p_tile(width):
    return pl.BlockSpec((TM, width), lambda i: (jnp.minimum(i, P_TILES - 1), 0))


def _s_tile(width):
    return pl.BlockSpec((TM, width), lambda i: (jnp.maximum(i - P_TILES, 0), 0))


def _inproj_first_kernel(xp_ref, xs_ref, mod_ref, g_ref, w_ref, qkv_ref, h_ref, kc_ref, vc_ref):
    x = _pick_group(xp_ref, xs_ref)
    hm = _rms_mod(x, g_ref[...], mod_ref[0, 0:1, :], mod_ref[0, 1:2, :])
    _project_in(hm, w_ref, qkv_ref, h_ref, kc_ref, vc_ref)


def _combine_experts(yg_refs, gate_ref):
    gates = gate_ref[...]
    lo_acc = hi_acc = None
    for kk in range(TOP_K):
        lo, hi = _unpack_halves(_load_slabs(yg_refs, kk))
        gk = gates[:, kk:kk + 1]
        lo_acc = gk * lo if lo_acc is None else lo_acc + gk * lo
        hi_acc = gk * hi if hi_acc is None else hi_acc + gk * hi
    return jnp.concatenate([lo_acc, hi_acc], axis=1)


def _inproj_next_kernel(x_ref, yga_ref, ygb_ref, gate_ref, pmod_ref, mod_ref, g_ref, w_ref, kc_in, vc_in,
                        qkv_ref, h_ref, xo_ref, kc_ref, vc_ref):
    del kc_in, vc_in
    x = x_ref[...] + pmod_ref[0, 5:6, :] * _combine_experts((yga_ref, ygb_ref), gate_ref)
    xo_ref[...] = x
    hm = _rms_mod(x, g_ref[...], mod_ref[0, 0:1, :], mod_ref[0, 1:2, :])
    _project_in(hm, w_ref, qkv_ref, h_ref, kc_ref, vc_ref)


_TILE_SPEC = pl.BlockSpec((TM, D_MODEL), lambda i: (i, 0))
_MOD_SPEC = pl.BlockSpec((1, MOD_ROWS, D_MODEL), lambda i: (i, 0, 0))
_ROW_SPEC = pl.BlockSpec((1, D_MODEL), lambda i: (0, 0))
_RT_LANES = 128
_RT_ROWS = 2 * TOP_K
_YG_SPEC = pl.BlockSpec((TOP_K, TM, D_SLAB), lambda i: (0, i, 0))
_GATE_SPEC = pl.BlockSpec((TM, _RT_LANES), lambda i: (i, 0))


def _inproj(layer, x, moe, prev_mod, mod, g, w_bf16, caches):
    w_spec = pl.BlockSpec((1, D_MODEL, IN_COLS), lambda i: (layer, 0, 0))
    h_spec = pl.BlockSpec((TM, REST_COLS), lambda i: (i, 0))
    h_shape = jax.ShapeDtypeStruct((T_ALL, REST_COLS), F32)
    q_spec = pl.BlockSpec((TM, QKV_COLS), lambda i: (i, 0))
    q_shape = jax.ShapeDtypeStruct((T_ALL, QKV_COLS), BF16)
    c_spec = pl.BlockSpec((SEQ_PER_TILE, NA_WIDTH, SEQ), lambda i: (jnp.minimum(i, P_TILES - 1), layer, 0))
    c_shape = jax.ShapeDtypeStruct((BATCH, DEPTH * NA_WIDTH, SEQ), F32)
    x_shape = jax.ShapeDtypeStruct((T_ALL, D_MODEL), F32)
    if moe is None:
        c_all = pl.BlockSpec((SEQ_PER_TILE, DEPTH * NA_WIDTH, SEQ), lambda i: (jnp.minimum(i, P_TILES - 1), 0, 0))
        qkv, h, kc, vc = pl.pallas_call(
            _inproj_first_kernel, out_shape=(q_shape, h_shape, c_shape, c_shape), grid=(N_TILES,),
            in_specs=[_p_tile(D_MODEL), _s_tile(D_MODEL), _MOD_SPEC, _ROW_SPEC, w_spec],
            out_specs=(q_spec, h_spec, c_all, c_all),
            compiler_params=_params(), name="inproj_first",
        )(*x, mod, g, w_bf16)
        return qkv, h, x, (kc, vc)
    qkv, h, x, kc, vc = pl.pallas_call(
        _inproj_next_kernel,
        out_shape=(q_shape, h_shape, x_shape, c_shape, c_shape),
        grid=(N_TILES,),
        in_specs=[_TILE_SPEC, _YG_SPEC, _YG_SPEC, _GATE_SPEC, _MOD_SPEC, _MOD_SPEC, _ROW_SPEC, w_spec,
                  pl.BlockSpec(memory_space=pl.ANY), pl.BlockSpec(memory_space=pl.ANY)],
        out_specs=(q_spec, h_spec, _TILE_SPEC, c_spec, c_spec),
        input_output_aliases={8: 3, 9: 4},
        compiler_params=_params(), name="inproj_next",
    )(x, *moe[0], moe[1], prev_mod, mod, g, w_bf16, *caches)
    return qkv, h, x, (kc, vc)


def _pair_mask(hh):
    lane = lax.broadcasted_iota(jnp.int32, (1, 2 * NA_HEAD_DIM), 1)
    return (lane >= hh * NA_HEAD_DIM) & (lane < (hh + 1) * NA_HEAD_DIM)


_ATT_SEQS = 4


def _stack_pair(qp):
    return jnp.concatenate([jnp.where(_pair_mask(hh), qp, jnp.zeros_like(qp)) for hh in range(2)], axis=0)


def _unstack_pair(o2):
    half = o2.shape[0] // 2
    return jnp.where(_pair_mask(0), o2[:half], o2[half:])


def _attn_prompt_kernel(q_ref, k_ref, v_ref, o_ref):
    scale = NA_HEAD_DIM ** -0.5
    for sq in range(_ATT_SEQS):
        rows = slice(sq * SEQ, (sq + 1) * SEQ)
        for p in range(NA_HEADS // 2):
            cols = slice(p * 128, (p + 1) * 128)
            q2 = _stack_pair(q_ref[rows, cols] * scale)
            s = _dot_nt(q2, k_ref[rows, cols])
            e = jnp.exp(s - jnp.max(s, axis=-1, keepdims=True))
            den = jnp.sum(e, axis=-1, keepdims=True)
            o_ref[rows, cols] = _unstack_pair(_dot(e.astype(BF16), v_ref[rows, cols]) / den).astype(o_ref.dtype)


def _attn_prompt(qkv):
    rows = _ATT_SEQS * SEQ
    return pl.pallas_call(
        _attn_prompt_kernel,
        out_shape=jax.ShapeDtypeStruct((T_PROMPT, NA_WIDTH), BF16),
        grid=(BATCH // _ATT_SEQS,),
        in_specs=[pl.BlockSpec((rows, NA_WIDTH), lambda b: (b, 0)),
                  pl.BlockSpec((rows, NA_WIDTH), lambda b: (b, 1)),
                  pl.BlockSpec((rows, NA_WIDTH), lambda b: (b, 2))],
        out_specs=pl.BlockSpec((rows, NA_WIDTH), lambda b: (b, 0)),
        compiler_params=_params(), name="attn_prompt",
    )(qkv, qkv, qkv)


_NA_ROWS = DEC_SEQ // GRID_W
_NA_LOC = NA_KH * GRID_W
_NA_STEP_ROWS = 4


def _na_window_start(r):
    return jnp.clip(r - NA_KH // 2, 0, _NA_ROWS - NA_KH)


def _attn_sample_kernel(q_ref, k_ref, v_ref, ck_ref, cv_ref, *rest):
    bias_refs, o_ref = rest[:_NA_STEP_ROWS], rest[_NA_STEP_ROWS]
    scale = NA_HEAD_DIM ** -0.5
    for p in range(NA_HEADS // 2):
        cols = slice(p * 128, (p + 1) * 128)
        kc = ck_ref[0, :, cols].astype(BF16)
        vc = cv_ref[0, :, cols].astype(BF16)
        for u in range(_NA_STEP_ROWS):
            rows = slice(u * GRID_W, (u + 1) * GRID_W)
            s0 = pl.multiple_of(_na_window_start(pl.program_id(1) * _NA_STEP_ROWS + u) * GRID_W, GRID_W)
            q2 = _stack_pair(q_ref[rows, cols] * scale)
            bias2 = jnp.concatenate([bias_refs[u][0, 0, 2 * p], bias_refs[u][0, 0, 2 * p + 1]], axis=0)
            sl = _dot_nt(q2, k_ref[pl.ds(s0, _NA_LOC), cols]) + bias2
            sc = _dot_nt(q2, kc)
            mx = jnp.maximum(jnp.max(sl, axis=-1, keepdims=True), jnp.max(sc, axis=-1, keepdims=True))
            el = jnp.exp(sl - mx)
            ec = jnp.exp(sc - mx)
            den = jnp.sum(el, axis=-1, keepdims=True) + jnp.sum(ec, axis=-1, keepdims=True)
            o2 = (_dot(el.astype(BF16), v_ref[pl.ds(s0, _NA_LOC), cols]) + _dot(ec.astype(BF16), vc)) / den
            o_ref[rows, cols] = _unstack_pair(o2).astype(o_ref.dtype)


def _attn_sample(layer, qkv, ck, cv, bias):
    q_rows = _NA_STEP_ROWS * GRID_W
    steps = _NA_ROWS // _NA_STEP_ROWS
    q_blk0 = T_PROMPT // q_rows
    kv_row0 = T_PROMPT // DEC_SEQ

    def bias_spec(u):
        def index(b, r2):
            r = r2 * _NA_STEP_ROWS + u
            return (layer, _na_window_start(r) - r + NA_KH - 1, 0, 0, 0)
        return pl.BlockSpec((1, 1, NA_HEADS, GRID_W, _NA_LOC), index)

    return pl.pallas_call(
        _attn_sample_kernel,
        out_shape=jax.ShapeDtypeStruct((T_SAMPLE, NA_WIDTH), BF16),
        grid=(DEC_BATCH, steps),
        in_specs=[pl.BlockSpec((q_rows, NA_WIDTH), lambda b, r2: (q_blk0 + b * steps + r2, 0)),
                  pl.BlockSpec((DEC_SEQ, NA_WIDTH), lambda b, r2: (kv_row0 + b, 1)),
                  pl.BlockSpec((DEC_SEQ, NA_WIDTH), lambda b, r2: (kv_row0 + b, 2)),
                  pl.BlockSpec((1, PAST_LEN, NA_WIDTH), lambda b, r2: (b, 0, 0)),
                  pl.BlockSpec((1, PAST_LEN, NA_WIDTH), lambda b, r2: (b, 0, 0))]
                 + [bias_spec(u) for u in range(_NA_STEP_ROWS)],
        out_specs=pl.BlockSpec((q_rows, NA_WIDTH), lambda b, r2: (b * steps + r2, 0)),
        compiler_params=_params(2), name="attn_sample",
    )(qkv, qkv, qkv, ck, cv, *([bias] * _NA_STEP_ROWS))


_NA_DR = 2 * NA_KH - 1
_NA_DC = 2 * NA_KW - 1


def _na_bias_kernel(rb_ref, o_ref):
    qc = lax.broadcasted_iota(jnp.int32, (GRID_W, GRID_W), 0)
    kc = lax.broadcasted_iota(jnp.int32, (GRID_W, GRID_W), 1)
    q_start = jnp.clip(qc - NA_KW // 2, 0, GRID_W - NA_KW)
    in_win = (kc >= q_start) & (kc < q_start + NA_KW)
    dc = jnp.clip(kc - qc + NA_KW - 1, 0, _NA_DC - 1)
    picks = [dc == d for d in range(_NA_DC)]

    def one_head(hh, carry):
        i = pl.program_id(0) * NA_HEADS + hh
        tiles = []
        for dr in range(_NA_DR):
            acc = jnp.zeros((GRID_W, GRID_W), F32)
            for d in range(_NA_DC):
                acc = jnp.where(picks[d], rb_ref[i, dr * _NA_DC + d], acc)
            tiles.append(jnp.where(in_win, acc, NEG_INF))
        for base in range(NA_KH):
            o_ref[0, base, hh] = jnp.concatenate(tiles[base:base + NA_KH], axis=1)
        return carry
    lax.fori_loop(0, NA_HEADS, one_head, 0)


def _na_bias_tables(rel_bias):
    rb = rel_bias.astype(F32).reshape(DEPTH * NA_HEADS, _NA_DR * _NA_DC)
    return pl.pallas_call(
        _na_bias_kernel,
        out_shape=jax.ShapeDtypeStruct((DEPTH, NA_KH, NA_HEADS, GRID_W, _NA_LOC), F32),
        grid=(DEPTH,),
        in_specs=[pl.BlockSpec(memory_space=pltpu.SMEM)],
        out_specs=pl.BlockSpec((1, NA_KH, NA_HEADS, GRID_W, _NA_LOC), lambda i: (i, 0, 0, 0, 0)),
        compiler_params=_params(), name="na_bias_tables",
    )(rb)


_HG_GROUP = 8
_HG_UNROLL = 4


def _hgrn_kernel(*refs, n_tok, has_state, n_alias):
    refs = refs[:8 + 2 * has_state] + refs[8 + 2 * has_state + n_alias:]
    if has_state:
        (q_ref, zf_ref, zb_ref, v_ref, g_ref, lbf_ref, lbb_ref, og_ref, s0f_ref, s0b_ref,
         rec_ref, sf_ref, sb_ref, kf_s, bf_s, kb_s, bb_s, of_s, ob_s, zf_s, zb_s, qsf_s, qsb_s, stf_s, stb_s) = refs
    else:
        (q_ref, zf_ref, zb_ref, v_ref, g_ref, lbf_ref, lbb_ref, og_ref,
         rec_ref, sf_ref, sb_ref, kf_s, bf_s, kb_s, bb_s, of_s, ob_s, zf_s, zb_s, qsf_s, qsb_s, stf_s, stb_s) = refs
        s0f_ref = s0b_ref = None
    C = HG_CHUNK
    W = HG_WIDTH
    n_chunks = n_tok // C
    rr = lax.broadcasted_iota(jnp.int32, (W, W), 0)
    cc = lax.broadcasted_iota(jnp.int32, (W, W), 1)
    log2_c = C.bit_length() - 1
    same_chunk = jnp.right_shift(rr, log2_c) == jnp.right_shift(cc, log2_c)
    tri_prefix = jnp.where(same_chunk & (cc <= rr), 1.0, 0.0).astype(BF16)
    tri_suffix = jnp.where(same_chunk & (cc >= rr), 1.0, 0.0).astype(BF16)
    same_head = jnp.right_shift(rr, 6) == jnp.right_shift(cc, 6)
    head_ones = jnp.where(same_head, 1.0, 0.0).astype(BF16)

    for ti in range(n_tok // W):
        rows = slice(ti * W, (ti + 1) * W)
        for z_ref, lb_ref, k_s, b_s, tri in ((zf_ref, lbf_ref, kf_s, bf_s, tri_prefix),
                                             (zb_ref, lbb_ref, kb_s, bb_s, tri_suffix)):
            z = z_ref[rows, :]
            lb = lb_ref[...]
            e = jnp.exp(-jnp.abs(z))
            big = 1.0 / (1.0 + e)
            small = e * big
            f = lb + (1.0 - lb) * jnp.where(z >= 0.0, big, small)
            logf = jnp.log(jnp.maximum(f, F_FLOOR))
            k_s[rows, :] = (1.0 - lb) * jnp.where(z >= 0.0, small, big)
            hi, mid, lo = _split3(logf)
            b_s[rows, :] = _dot(tri, hi) + _dot(tri, mid) + _dot(tri, lo)

    G = _HG_GROUP
    n_groups = C // G
    srow = lax.broadcasted_iota(jnp.int32, (G, W), 0)
    zf_s[...] = jnp.zeros_like(zf_s)
    zb_s[...] = jnp.zeros_like(zb_s)

    def scan_chunk(ci, k_s, b_s, z_s, ks_s, st_s, o_dir_s, fwd):
        c = ci if fwd else n_chunks - 1 - ci
        base = pl.multiple_of(c * C, C)
        q = q_ref[pl.ds(base, C), :]
        k = k_s[pl.ds(base, C), :]
        b = b_s[pl.ds(base, C), :]
        v = v_ref[pl.ds(base, C), :]
        q_far = {}
        for gs in range(n_groups):
            others = range(gs + 1, n_groups) if fwd else range(gs)
            if not others:
                continue
            rows_s = slice(gs * G, (gs + 1) * G)
            edge = (gs + 1) * G - 1 if fwd else gs * G
            b_edge = b[edge:edge + 1, :]
            ks_s[rows_s, :] = k[rows_s] * jnp.exp(b_edge - b[rows_s])
            for gt in others:
                rows_t = slice(gt * G, (gt + 1) * G)
                q_far[gs, gt] = q[rows_t] * jnp.exp(b[rows_t] - b_edge)
        for sx in range(C):
            gs = sx // G
            rows_g = slice(gs * G, (gs + 1) * G)
            k_row = k_s[pl.ds(base + sx, 1), :]
            b_row = b_s[pl.ds(base + sx, 1), :]
            keep = (srow + gs * G >= sx) if fwd else (srow + gs * G <= sx)
            z_s[sx * C + gs * G:sx * C + (gs + 1) * G, :] = jnp.where(
                keep, (k_row * q[rows_g]) * jnp.exp(b[rows_g] - b_row), 0.0)
            others = range(gs + 1, n_groups) if fwd else range(gs)
            if others:
                ks_row = ks_s[sx:sx + 1, :]
                for gt in others:
                    z_s[sx * C + gt * G:sx * C + (gt + 1) * G, :] = ks_row * q_far[gs, gt]
        a_rep = _dot(z_s[...].astype(BF16), head_ones)
        o_intra = jnp.sum(a_rep.reshape(C, C, W) * v[:, None, :], axis=0)
        b_end = b_s[pl.ds(base + (C - 1 if fwd else 0), 1), :]
        q_in = q * jnp.exp(b)
        k_st = k * jnp.exp(b_end - b)
        st = st_s[...]
        o_inter = _dot_nt(q_in.astype(BF16), st.astype(BF16))
        upd = _dot_tn(v.astype(BF16), k_st.astype(BF16))
        st_s[...] = st * jnp.exp(b_end) + jnp.where(same_head, upd, 0.0)
        o_dir_s[pl.ds(base, C), :] = o_intra + o_inter

    def load_state(s0_ref, st_s):
        if s0_ref is None:
            st_s[...] = jnp.zeros((W, W), F32)
            return
        for hh in range(HG_HEADS):
            parts = [s0_ref[0, hh] if g == hh else jnp.zeros((HG_DK, HG_DV), F32) for g in range(HG_HEADS)]
            st_s[hh * HG_DK:(hh + 1) * HG_DK, :] = jnp.concatenate(parts, axis=1)
        st_s[...] = st_s[...].T

    def store_state(st_s, out_ref):
        by_head = st_s[...].T
        for hh in range(HG_HEADS):
            out_ref[0, hh] = by_head[hh * HG_DK:(hh + 1) * HG_DK, hh * HG_DV:(hh + 1) * HG_DV]
        if out_ref.shape[1] > HG_HEADS:
            out_ref[0, HG_HEADS:] = jnp.zeros((out_ref.shape[1] - HG_HEADS, HG_DK, HG_DV), F32)

    load_state(s0f_ref, stf_s)
    load_state(s0b_ref, stb_s)

    def scan_both(ci, carry):
        for u in range(_HG_UNROLL):
            scan_chunk(ci * _HG_UNROLL + u, kf_s, bf_s, zf_s.at[u], qsf_s.at[u], stf_s, of_s, True)
            scan_chunk(ci * _HG_UNROLL + u, kb_s, bb_s, zb_s.at[u], qsb_s.at[u], stb_s, ob_s, False)
        return carry
    lax.fori_loop(0, n_chunks // _HG_UNROLL, scan_both, 0)
    store_state(stf_s, sf_ref)
    store_state(stb_s, sb_ref)

    for ti in range(n_tok // W):
        rows = slice(ti * W, (ti + 1) * W)
        o = of_s[rows, :] + ob_s[rows, :]
        sq = o * o
        sq_hi = sq.astype(BF16)
        sq_lo = (sq - sq_hi.astype(F32)).astype(BF16)
        ms = (_dot(sq_hi, head_ones) + _dot(sq_lo, head_ones)) * (1.0 / HG_DV)
        g = g_ref[rows, :]
        y = o * lax.rsqrt(ms + RMS_EPS) * og_ref[...] * (g * jax.nn.sigmoid(g))
        rec_ref[rows, :] = y.astype(rec_ref.dtype)


def _hgrn(h, lbf, lbb, og, s0f, s0b, n_tok, n_seq, row0, layer=None, states=None):
    W = HG_WIDTH
    has_state = s0f is not None

    def col(cb):
        return pl.BlockSpec((n_tok, W), lambda i, cb=cb: (row0 + i, cb))

    vec = pl.BlockSpec((1, W), lambda i: (0, 0))
    st_spec = pl.BlockSpec((1, HG_HEADS, HG_DK, HG_DV), lambda i: (i, 0, 0, 0))
    in_specs = [col(_CB_HQ), col(_CB_ZF), col(_CB_ZB), col(_CB_HI), col(_CB_HG), vec, vec, vec]
    args = [h, h, h, h, h, lbf, lbb, og]
    if has_state:
        in_specs += [st_spec, st_spec]
        args += [s0f, s0b]
    seq_f32 = pltpu.VMEM((n_tok, W), F32)
    out_st_spec, st_rows, aliases = st_spec, HG_HEADS, {}
    if layer is not None:
        st_rows = DEPTH * HG_HEADS
        if states is None:
            out_st_spec = pl.BlockSpec((1, st_rows, HG_DK, HG_DV), lambda i: (i, 0, 0, 0))
        else:
            out_st_spec = pl.BlockSpec((1, HG_HEADS, HG_DK, HG_DV), lambda i: (i, layer, 0, 0))
            aliases = {len(args): 1, len(args) + 1: 2}
            in_specs += [pl.BlockSpec(memory_space=pl.ANY)] * 2
            args += list(states)
    return pl.pallas_call(
        functools.partial(_hgrn_kernel, n_tok=n_tok, has_state=has_state, n_alias=len(aliases)),
        out_shape=(jax.ShapeDtypeStruct((n_seq * n_tok, W), BF16),
                   jax.ShapeDtypeStruct((n_seq, st_rows, HG_DK, HG_DV), F32),
                   jax.ShapeDtypeStruct((n_seq, st_rows, HG_DK, HG_DV), F32)),
        grid=(n_seq,),
        in_specs=in_specs,
        input_output_aliases=aliases,
        out_specs=(pl.BlockSpec((n_tok, W), lambda i: (i, 0)), out_st_spec, out_st_spec),
        scratch_shapes=[seq_f32, seq_f32, seq_f32, seq_f32, seq_f32, seq_f32,
                        pltpu.VMEM((_HG_UNROLL, HG_CHUNK * HG_CHUNK, W), F32),
                        pltpu.VMEM((_HG_UNROLL, HG_CHUNK * HG_CHUNK, W), F32),
                        pltpu.VMEM((_HG_UNROLL, HG_CHUNK, W), F32),
                        pltpu.VMEM((_HG_UNROLL, HG_CHUNK, W), F32),
                        pltpu.VMEM((W, W), F32),
                        pltpu.VMEM((W, W), F32)],
        compiler_params=_params(), name="hgrn_state" if has_state else "hgrn_zero",
    )(*args)


def _gmlp_tile(u_ref, v_ref, g_ref, ws_ref, b_ref):
    lane = lax.broadcasted_iota(jnp.int32, (1, GM_WIDTH), 1)
    outs = []
    for ci in range(TM // GM_CHUNK):
        rows = slice(ci * GM_CHUNK, (ci + 1) * GM_CHUNK)
        v = v_ref[rows, :]
        ms = jnp.mean(v * v, axis=-1, keepdims=True)
        vn = (v * lax.rsqrt(ms + RMS_EPS) * g_ref[...]).astype(BF16)
        z = b_ref[...]
        for gi in range(GM_GROUPS):
            zg = _dot(ws_ref[gi], vn)
            in_group = (lane >= gi * GM_GDIM) & (lane < (gi + 1) * GM_GDIM)
            z = z + jnp.where(in_group, zg, 0.0)
        outs.append((u_ref[rows, :] * z).astype(BF16))
    return jnp.concatenate(outs, axis=0)


def _outproj_kernel(attp_ref, atts_ref, recp_ref, recs_ref, gu_ref, gv_ref, gg_ref, gws_ref, gb_ref,
                    xp_ref, xs_ref, mod_ref, g_ref, w_ref, wr_ref, br_ref,
                    x1_ref, h2a_ref, h2b_ref, rt_ref, gate_ref, cnt_ref):
    @pl.when(pl.program_id(0) == 0)
    def _():
        cnt_ref[...] = jnp.zeros_like(cnt_ref)

    out = (_dot(_pick_group(attp_ref, atts_ref), w_ref[0, 0:NA_WIDTH, :])
           + _dot(_pick_group(recp_ref, recs_ref), w_ref[0, NA_WIDTH:NA_WIDTH + HG_WIDTH, :])
           + _dot(_gmlp_tile(gu_ref, gv_ref, gg_ref, gws_ref, gb_ref), w_ref[0, NA_WIDTH + HG_WIDTH:, :]))
    x1 = _pick_group(xp_ref, xs_ref) + mod_ref[0, 2:3, :] * out
    x1_ref[...] = x1
    h2 = _rms_mod(x1, g_ref[...], mod_ref[0, 3:4, :], mod_ref[0, 4:5, :])
    _store_slabs((h2a_ref, h2b_ref), _pack_halves(h2))
    h_hi = h2.astype(BF16)
    h_lo = (h2 - h_hi.astype(F32)).astype(BF16)
    wr = wr_ref[...]
    w_hi = wr.astype(BF16)
    w_lo = (wr - w_hi.astype(F32)).astype(BF16)
    logits = _dot(h_hi, w_hi) + _dot(h_lo, w_hi) + _dot(h_hi, w_lo) + br_ref[...]
    lane_e = lax.broadcasted_iota(jnp.int32, (TM, N_EXPERTS), 1).astype(F32)
    lane_o = lax.broadcasted_iota(jnp.int32, (TM, _RT_LANES), 1)
    idx_acc = jnp.zeros((TM, _RT_LANES), F32)
    val_acc = jnp.zeros((TM, _RT_LANES), F32)
    top0 = None
    den = jnp.zeros((TM, 1), F32)
    work = logits
    picks = []
    for kk in range(TOP_K):
        m = jnp.max(work, axis=-1, keepdims=True)
        first = jnp.min(jnp.where(work == m, lane_e, float(N_EXPERTS)), axis=-1, keepdims=True)
        if kk == 0:
            top0 = m
        e = jnp.exp(m - top0)
        den = den + e
        idx_acc = jnp.where(lane_o == kk, first, idx_acc)
        val_acc = jnp.where(lane_o == kk, e, val_acc)
        picks.append(lane_e == first)
        work = jnp.where(picks[-1], -jnp.inf, work)
    gate_ref[...] = val_acc / den
    sel = jnp.zeros((TM, N_EXPERTS), F32)
    for pk in picks:
        sel = sel + jnp.where(pk, 1.0, 0.0)
    rr = lax.broadcasted_iota(jnp.int32, (TM, TM), 0)
    cc = lax.broadcasted_iota(jnp.int32, (TM, TM), 1)
    earlier = jnp.where(cc < rr, 1.0, 0.0).astype(BF16)
    seen = cnt_ref[0:1, 0:N_EXPERTS]
    before = _dot(earlier, sel.astype(BF16)) + seen
    for kk, pk in enumerate(picks):
        rank = jnp.sum(jnp.where(pk, before, 0.0), axis=-1, keepdims=True)
        idx_acc = jnp.where(lane_o == TOP_K + kk, rank, idx_acc)
    rt_ref[...] = idx_acc.T[0:_RT_ROWS, :].astype(jnp.int32)
    cnt_ref[0:1, 0:N_EXPERTS] = seen + jnp.sum(sel, axis=0, keepdims=True)


def _outproj(layer, att_p, att_s, rec_p, rec_s, h, gmlp_params, x, mod, g, w_bf16, wr, br):
    def tile(width):
        return pl.BlockSpec((TM, width), lambda i: (i, 0))

    if isinstance(x, tuple):
        x_args, x_specs = x, [_p_tile(D_MODEL), _s_tile(D_MODEL)]
    else:
        x_args = (x, x)
        x_specs = [_p_tile(D_MODEL), pl.BlockSpec((TM, D_MODEL), lambda i: (jnp.maximum(i, P_TILES), 0))]
    return pl.pallas_call(
        _outproj_kernel,
        out_shape=(jax.ShapeDtypeStruct((T_ALL, D_MODEL), F32),
                   jax.ShapeDtypeStruct((T_ALL, D_SLAB), jnp.int32),
                   jax.ShapeDtypeStruct((T_ALL, D_SLAB), jnp.int32),
                   jax.ShapeDtypeStruct((_RT_ROWS, T_ALL), jnp.int32),
                   jax.ShapeDtypeStruct((T_ALL, _RT_LANES), F32),
                   jax.ShapeDtypeStruct((8, _RT_LANES), F32)),
        grid=(N_TILES,),
        in_specs=[_p_tile(NA_WIDTH), _s_tile(NA_WIDTH), _p_tile(HG_WIDTH), _s_tile(HG_WIDTH),
                  pl.BlockSpec((TM, GM_WIDTH), lambda i: (i, _CB_GU)), pl.BlockSpec((TM, GM_WIDTH), lambda i: (i, _CB_GV)),
                  pl.BlockSpec((1, GM_WIDTH), lambda i: (0, 0)),
                  pl.BlockSpec((GM_GROUPS, GM_CHUNK, GM_CHUNK), lambda i: (0, 0, 0)),
                  pl.BlockSpec((GM_CHUNK, GM_WIDTH), lambda i: (0, 0)),
                  *x_specs, _MOD_SPEC, _ROW_SPEC,
                  pl.BlockSpec((1, D_MODEL, D_MODEL), lambda i: (layer, 0, 0)),
                  pl.BlockSpec((D_MODEL, N_EXPERTS), lambda i: (0, 0)),
                  pl.BlockSpec((1, N_EXPERTS), lambda i: (0, 0))],
        out_specs=(_TILE_SPEC, tile(D_SLAB), tile(D_SLAB), pl.BlockSpec((_RT_ROWS, TM), lambda i: (0, i)),
                   tile(_RT_LANES),
                   pl.BlockSpec((8, _RT_LANES), lambda i: (0, 0))),
        compiler_params=_params(), name="outproj_router",
    )(att_p, att_s, rec_p, rec_s, h, h, *gmlp_params, *x_args, mod, g, w_bf16, wr, br)


_W_CHUNKS = 4
_W_CAST_ROWS = 128
_W_DMA_PRIORITY = 1


def _moe_kernel(blk_e_ref, blk_on_ref, blk_new_ref, blk_next_ref,
                xa_ref, xb_ref, wg_hbm, bg_ref, wu_hbm, bu_ref, wd_hbm, bd_ref,
                ya_ref, yb_ref, w_f32, w_bf16, w_sem, *, layer):
    j = pl.program_id(0)

    def weight_copies(expert):
        rows = D_MODEL // _W_CHUNKS
        return [pltpu.make_async_copy(w_hbm.at[layer, expert, pl.ds(ci * rows, rows)],
                                      w_f32.at[wi, pl.ds(ci * rows, rows)], w_sem.at[wi, ci])
                for wi, w_hbm in enumerate((wg_hbm, wu_hbm, wd_hbm)) for ci in range(_W_CHUNKS)]

    @pl.when(j == 0)
    def _():
        for cp in weight_copies(blk_e_ref[0]):
            cp.start(priority=_W_DMA_PRIORITY)

    @pl.when(blk_new_ref[j] != 0)
    def _():
        for cp in weight_copies(blk_e_ref[j]):
            cp.wait()

        def cast_rows(ci, carry):
            rows = pl.ds(pl.multiple_of(ci * _W_CAST_ROWS, _W_CAST_ROWS), _W_CAST_ROWS)
            for wi in range(3):
                w_bf16[wi, rows, :] = w_f32[wi, rows, :].astype(BF16)
            return carry
        lax.fori_loop(0, D_MODEL // _W_CAST_ROWS, cast_rows, 0)

        @pl.when(blk_next_ref[j] >= 0)
        def _():
            for cp in weight_copies(blk_next_ref[j]):
                cp.start(priority=_W_DMA_PRIORITY)

    @pl.when(blk_on_ref[j] != 0)
    def _():
        lo, hi = _unpack_halves(_load_slabs((xa_ref, xb_ref)))
        x = jnp.concatenate([lo.astype(BF16), hi.astype(BF16)], axis=1)
        gate = jnp.minimum(_dot(x, w_bf16[0]) + bg_ref[0, 0], SWIGLU_LIMIT)
        up = jnp.clip(_dot(x, w_bf16[1]) + bu_ref[0, 0], -SWIGLU_LIMIT, SWIGLU_LIMIT)
        glu = gate * jax.nn.sigmoid(SWIGLU_ALPHA * gate)
        act = ((up + 1.0) * glu).astype(BF16)
        _store_slabs((ya_ref, yb_ref), _pack_halves(_dot(act, w_bf16[2]) + bd_ref[0, 0]))

    @pl.when(blk_on_ref[j] == 0)
    def _():
        ya_ref[...] = jnp.zeros_like(ya_ref)
        yb_ref[...] = jnp.zeros_like(yb_ref)


def _moe(layer, plan, x_sorted, wg, bg, wu, bu, wd, bd):
    n_plan = len(plan)
    b_spec = pl.BlockSpec((1, 1, 1, D_MODEL), lambda j, be, *_: (layer, be[j], 0, 0))
    x_spec = pl.BlockSpec((MOE_BM, D_SLAB), lambda j, *_: (j, 0))
    hbm = pl.BlockSpec(memory_space=pl.ANY)
    bias4 = lambda b: b.reshape(DEPTH, N_EXPERTS, 1, D_MODEL)
    return pl.pallas_call(
        functools.partial(_moe_kernel, layer=layer),
        out_shape=(jax.ShapeDtypeStruct((MOE_SLOTS, D_SLAB), jnp.int32),) * N_SPLIT,
        grid_spec=pltpu.PrefetchScalarGridSpec(
            num_scalar_prefetch=n_plan, grid=(MOE_BLOCKS,),
            in_specs=[x_spec, x_spec, hbm, b_spec, hbm, b_spec, hbm, b_spec],
            out_specs=(x_spec, x_spec),
            scratch_shapes=[pltpu.VMEM((3, D_MODEL, D_MODEL), F32), pltpu.VMEM((3, D_MODEL, D_MODEL), BF16),
                            pltpu.SemaphoreType.DMA((3, _W_CHUNKS))]),
        compiler_params=_params(), name="moe_experts",
    )(*plan, *x_sorted, wg, bias4(bg), wu, bias4(bu), wd, bias4(bd))


def _route(rt, counts):
    experts = jnp.arange(N_EXPERTS, dtype=jnp.int32)
    nblk = (counts + MOE_BM - 1) // MOE_BM
    blk_end = jnp.cumsum(nblk)
    row0 = (blk_end - nblk) * MOE_BM
    top_i, rank = rt[:TOP_K], rt[TOP_K:]
    start_of = jnp.sum(jnp.where(top_i[None] == experts[:, None, None], row0[:, None, None], 0), axis=0)
    dest = (start_of + rank).reshape(1, TOP_K * T_ALL)
    live = counts > 0
    last_live = jnp.max(jnp.where(live, experts, 0))
    later_live = live[None, :] & (experts[None, :] > experts[:, None])
    next_live = jnp.min(jnp.where(later_live, experts[None, :], N_EXPERTS), axis=1)
    next_live = jnp.where(next_live == N_EXPERTS, -1, next_live)
    blk = jnp.arange(MOE_BLOCKS, dtype=jnp.int32)
    blk_on = blk < blk_end[-1]
    blk_e = jnp.where(blk_on, jnp.minimum(jnp.sum((blk_end[None, :] <= blk[:, None]).astype(jnp.int32), axis=1),
                                          N_EXPERTS - 1), last_live)
    blk_new = blk_on & jnp.concatenate([jnp.ones((1,), bool), blk_e[1:] != blk_e[:-1]])
    is_e = blk_e[:, None] == experts[None, :]
    lookup = lambda table: jnp.sum(jnp.where(is_e, table[None, :], 0), axis=1)
    plan = (blk_e, blk_on, blk_new, lookup(next_live))
    return dest.astype(jnp.int32), tuple(p.astype(jnp.int32) for p in plan)


_SC_WINDOW = 128


def _sc_mesh():
    return plsc.VectorSubcoreMesh(core_axis_name="core", subcore_axis_name="subcore")


def _sc_scatter_rows(srcs, idx, n_out):
    n_src, width = srcs[0].shape
    n_rep = idx.shape[1] // n_src
    src_windows = n_src // _SC_WINDOW
    assert len(srcs) == 2

    def body(*refs):
        x_hbm = refs[:len(srcs)]
        i_hbm = refs[len(srcs)]
        o_hbm = refs[len(srcs) + 1:]

        def run(xs, os_):
            def step(x_vmem, *i_vmem):
                for iv in i_vmem:
                    pltpu.sync_copy(x_vmem, os_.at[iv.at[0]])

            pltpu.emit_pipeline(
                step, grid=(src_windows,),
                in_specs=[pl.BlockSpec((_SC_WINDOW, width), lambda i: (i, 0))]
                         + [pl.BlockSpec((1, _SC_WINDOW), lambda i, kk=kk: (0, kk * src_windows + i))
                            for kk in range(n_rep)],
                out_specs=[],
                core_axis_name="subcore",
                dimension_semantics=(pltpu.PARALLEL,),
            )(xs, *([i_hbm] * n_rep))

        for ci, (xs, os_) in enumerate(zip(x_hbm, o_hbm)):
            pl.when(lax.axis_index("core") == ci)(functools.partial(run, xs, os_))

    out_type = tuple(jax.ShapeDtypeStruct((n_out, width), s.dtype) for s in srcs)
    return pl.kernel(body, out_type=out_type, mesh=_sc_mesh(), scratch_types=[],
                     name="sc_scatter_rows")(*srcs, idx)


def _sc_gather_rows(tables, idx):
    n_idx = idx.shape[1]
    width = tables[0].shape[1]

    def body(*refs):
        t_hbm = refs[:len(tables)]
        i_hbm = refs[len(tables)]
        o_hbm = refs[len(tables) + 1:]
        for ts, os_ in zip(t_hbm, o_hbm):
            def step(i_vmem, o_vmem, ts=ts):
                pltpu.sync_copy(ts.at[i_vmem.at[0]], o_vmem)

            pltpu.emit_pipeline(
                step, grid=(n_idx // _SC_WINDOW,),
                in_specs=[pl.BlockSpec((1, _SC_WINDOW), lambda i: (0, i))],
                out_specs=[pl.BlockSpec((_SC_WINDOW, width), lambda i: (i, 0))],
                core_axis_name=("core", "subcore"),
                dimension_semantics=(pltpu.PARALLEL,),
            )(i_hbm, os_)

    out_type = tuple(jax.ShapeDtypeStruct((n_idx, width), t.dtype) for t in tables)
    return pl.kernel(body, out_type=out_type, mesh=_sc_mesh(), scratch_types=[],
                     name="sc_gather_rows")(*tables, idx)


def _final_kernel(x_ref, yga_ref, ygb_ref, gate_ref, mod_ref, g_ref, yp_ref, ys_ref):
    x = x_ref[...] + mod_ref[0, 5:6, :] * _combine_experts((yga_ref, ygb_ref), gate_ref)
    ms = jnp.mean(x * x, axis=-1, keepdims=True)
    y = x * lax.rsqrt(ms + RMS_EPS) * g_ref[...]

    @pl.when(pl.program_id(0) < P_TILES)
    def _():
        yp_ref[...] = y

    @pl.when(pl.program_id(0) >= P_TILES)
    def _():
        ys_ref[...] = y


def _final(x, moe, mod, g):
    return pl.pallas_call(
        _final_kernel,
        out_shape=(jax.ShapeDtypeStruct((T_PROMPT, D_MODEL), F32), jax.ShapeDtypeStruct((T_SAMPLE, D_MODEL), F32)),
        grid=(N_TILES,),
        in_specs=[_TILE_SPEC, _YG_SPEC, _YG_SPEC, _GATE_SPEC, _MOD_SPEC, _ROW_SPEC],
        out_specs=(_p_tile(D_MODEL), _s_tile(D_MODEL)),
        compiler_params=_params(), name="final_norm",
    )(x, *moe[0], moe[1], mod, g)


def kernel(x_prompt, x_sample, cache_k, cache_v, state_hgrn_fwd, state_hgrn_bwd, c, c_ctx, w_mod, b_mod, norm1_g, norm2_g, w_in, na_rel_bias, hgrn_lb, hgrn_onorm_g, gmlp_vnorm_g, gmlp_ws, gmlp_b, w_out, router_w, router_b, w_gate, b_gate, w_up, b_up, w_down, b_down, final_g):
    x = (x_prompt.reshape(T_PROMPT, D_MODEL), x_sample.reshape(T_SAMPLE, D_MODEL))

    cond = jnp.zeros((MOD_ROWS, D_MODEL), F32).at[0].set(c_ctx).at[1:1 + DEC_BATCH].set(c)
    mod = _modulation(cond, w_mod, b_mod)
    tile_row = np.concatenate([np.zeros(P_TILES, np.int32),
                               1 + np.arange(N_TILES - P_TILES, dtype=np.int32) // (DEC_SEQ // TM)])
    mod_tiles = mod[:, tile_row].reshape(DEPTH, N_TILES, 6, D_MODEL)
    mod_tiles = jnp.pad(mod_tiles, ((0, 0), (0, 0), (0, MOD_ROWS - 6), (0, 0)))

    lb_soft = jax.nn.softmax(hgrn_lb.astype(F32), axis=1)
    lower = jnp.cumsum(lb_soft, axis=1) - lb_soft[:, :1]

    na_bias = _na_bias_tables(na_rel_bias)
    w_in_bf16 = w_in.astype(BF16)
    w_out_bf16 = w_out.astype(BF16)

    moe_out = caches = states = None
    for l in range(DEPTH):
        qkv, h, x_next, caches = _inproj(l, x, moe_out, mod_tiles[l - 1] if l else None, mod_tiles[l],
                                    norm1_g[l][None, :], w_in_bf16, caches)

        att_p = _attn_prompt(qkv)
        att_s = _attn_sample(l, qkv, cache_k[:, l].reshape(DEC_BATCH, PAST_LEN, NA_WIDTH),
                             cache_v[:, l].reshape(DEC_BATCH, PAST_LEN, NA_WIDTH), na_bias)
        lbf = lower[0, l][None, :]
        lbb = lower[1, l][None, :]
        og = jnp.tile(hgrn_onorm_g[l], HG_HEADS)[None, :]
        rec_p, *states = _hgrn(h, lbf, lbb, og, None, None, SEQ, BATCH, 0, layer=l, states=states)
        rec_s, _, _ = _hgrn(h, lbf, lbb, og, state_hgrn_fwd[:, l].astype(F32), state_hgrn_bwd[:, l].astype(F32),
                            DEC_SEQ, DEC_BATCH, T_PROMPT // DEC_SEQ)
        gm_bias = jnp.repeat(gmlp_b[l].T, GM_GDIM, axis=1)
        gmlp_params = (gmlp_vnorm_g[l][None, :], gmlp_ws[l].astype(BF16), gm_bias)

        x, h2a, h2b, rt, gate_pad, cnt = _outproj(l, att_p, att_s, rec_p, rec_s, h, gmlp_params, x_next,
                                                  mod_tiles[l],
                                                  norm2_g[l][None, :], w_out_bf16,
                                                  router_w[l], router_b[l][None, :])
        dest_flat, plan = _route(rt, cnt[0, :N_EXPERTS].astype(jnp.int32))
        x_sorted = _sc_scatter_rows((h2a, h2b), dest_flat, MOE_SLOTS)
        y_sorted = _moe(l, plan, x_sorted, w_gate, b_gate, w_up, b_up, w_down, b_down)
        y_tok = _sc_gather_rows(y_sorted, dest_flat)
        moe_out = ([yt.reshape(TOP_K, T_ALL, D_SLAB) for yt in y_tok], gate_pad)

    y_prompt, y_sample = _final(x, moe_out, mod_tiles[DEPTH - 1], final_g[None, :])
    y_prompt = y_prompt.reshape(BATCH, SEQ, D_MODEL)
    y_sample = y_sample.reshape(DEC_BATCH, DEC_SEQ, D_MODEL)
    new_k, new_v = (cache.reshape(BATCH, DEPTH, NA_HEADS, NA_HEAD_DIM, SEQ).transpose(0, 1, 4, 2, 3)
                    for cache in caches)
    new_sf, new_sb = (st.reshape(BATCH, DEPTH, HG_HEADS, HG_DK, HG_DV) for st in states)
    return (y_prompt, y_sample, new_k, new_v, new_sf, new_sb)
```

```python
import functools

import numpy as np
import jax
import jax.numpy as jnp
from jax import lax
from jax.experimental import pallas as pl
from jax.experimental.pallas import tpu as pltpu
from jax.experimental.pallas import tpu_sc as plsc

F32 = jnp.float32
BF16 = jnp.bfloat16

D_MODEL = 1024
BATCH = 32
SEQ = 256
DEPTH = 2
DEC_BATCH = 2
DEC_SEQ = 1024
PAST_LEN = 512
GRID_W = 64
NA_HEADS = 8
NA_HEAD_DIM = 64
NA_WIDTH = NA_HEADS * NA_HEAD_DIM
NA_KH = 8
NA_KW = 16
HG_HEADS = 4
HG_DK = 64
HG_DV = 64
HG_WIDTH = HG_HEADS * HG_DV
HG_CHUNK = 16
F_FLOOR = 1e-30
GM_GROUPS = 4
GM_GDIM = 64
GM_WIDTH = GM_GROUPS * GM_GDIM
GM_CHUNK = 128
IN_COLS = 3 * NA_WIDTH + 5 * HG_WIDTH + 2 * GM_WIDTH
N_EXPERTS = 32
TOP_K = 4
SWIGLU_LIMIT = 7.0
SWIGLU_ALPHA = 1.702
RMS_EPS = 1e-6
NEG_INF = -1e30

T_PROMPT = BATCH * SEQ
T_SAMPLE = DEC_BATCH * DEC_SEQ
T_ALL = T_PROMPT + T_SAMPLE
TM = 512
SEQ_PER_TILE = TM // SEQ
N_TILES = T_ALL // TM
P_TILES = T_PROMPT // TM
MOE_BM = 512
MOE_SLOTS = -(-(T_ALL * TOP_K + N_EXPERTS * (MOE_BM - 1)) // MOE_BM) * MOE_BM
MOE_BLOCKS = MOE_SLOTS // MOE_BM
MOD_ROWS = 8
V7X_VMEM_LIMIT = 48 * 1024 * 1024

QKV_COLS = 3 * NA_WIDTH
REST_COLS = IN_COLS - QKV_COLS
_CB_HQ, _CB_ZF, _CB_ZB, _CB_HI, _CB_HG, _CB_GU, _CB_GV = range(7)


def _dot(a, b):
    return jnp.dot(a, b, preferred_element_type=F32)


def _dot_nt(a, b):
    return lax.dot_general(a, b, (((1,), (1,)), ((), ())), preferred_element_type=F32)


def _dot_tn(a, b):
    return lax.dot_general(a, b, (((0,), (0,)), ((), ())), preferred_element_type=F32)


def _split3(x):
    hi = x.astype(BF16)
    r1 = x - hi.astype(F32)
    mid = r1.astype(BF16)
    lo = (r1 - mid.astype(F32)).astype(BF16)
    return hi, mid, lo


D_PACK = D_MODEL // 2
N_SPLIT = 2
D_SLAB = D_PACK // N_SPLIT


def _pack_halves(x):
    half = x.shape[1] // 2
    lo = pltpu.bitcast(x[:, :half].astype(BF16).astype(F32), jnp.uint32)
    hi = pltpu.bitcast(x[:, half:].astype(BF16).astype(F32), jnp.uint32)
    return pltpu.bitcast(jnp.right_shift(lo, jnp.uint32(16)) | hi, jnp.int32)


def _unpack_halves(w):
    u = pltpu.bitcast(w, jnp.uint32)
    lo = pltpu.bitcast(jnp.left_shift(u, jnp.uint32(16)), F32)
    hi = pltpu.bitcast(u & jnp.uint32(0xFFFF0000), F32)
    return lo, hi


def _load_slabs(refs, *lead):
    return jnp.concatenate([r[lead] if lead else r[...] for r in refs], axis=1)


def _store_slabs(refs, packed):
    for si, r in enumerate(refs):
        r[...] = packed[:, si * D_SLAB:(si + 1) * D_SLAB]


def _params(n_axes=1):
    return pltpu.CompilerParams(dimension_semantics=("arbitrary",) * n_axes,
                                vmem_limit_bytes=V7X_VMEM_LIMIT)


def _mod_kernel(cond_ref, w_ref, b_ref, o_ref):
    c = cond_ref[...]
    c = c * jax.nn.sigmoid(c)
    w = w_ref[0]
    c_hi = c.astype(BF16)
    c_lo = (c - c_hi.astype(F32)).astype(BF16)
    w_hi = w.astype(BF16)
    w_lo = (w - w_hi.astype(F32)).astype(BF16)
    o_ref[0] = _dot(c_hi, w_hi) + _dot(c_lo, w_hi) + _dot(c_hi, w_lo) + b_ref[0]


def _modulation(cond, w_mod, b_mod):
    tn = 1536
    return pl.pallas_call(
        _mod_kernel,
        out_shape=jax.ShapeDtypeStruct((DEPTH, MOD_ROWS, 6 * D_MODEL), F32),
        grid=(DEPTH, 6 * D_MODEL // tn),
        in_specs=[pl.BlockSpec((MOD_ROWS, D_MODEL), lambda l, j: (0, 0)),
                  pl.BlockSpec((1, D_MODEL, tn), lambda l, j: (l, 0, j)),
                  pl.BlockSpec((1, 1, tn), lambda l, j: (l, 0, j))],
        out_specs=pl.BlockSpec((1, MOD_ROWS, tn), lambda l, j: (l, 0, j)),
        compiler_params=_params(2),
        name="modulation",
    )(cond, w_mod, b_mod.reshape(DEPTH, 1, 6 * D_MODEL))


def _rms_mod(x, g, shift, scale):
    ms = jnp.mean(x * x, axis=-1, keepdims=True)
    y = x * lax.rsqrt(ms + RMS_EPS) * g
    return y * (1.0 + scale) + shift


def _project_in(hm, w_ref, qkv_ref, h_ref, kc_ref, vc_ref):
    h = _dot(hm.astype(BF16), w_ref[0])
    qkv_ref[...] = h[:, :QKV_COLS].astype(BF16)
    h_ref[...] = h[:, QKV_COLS:]

    @pl.when(pl.program_id(0) < P_TILES)
    def _():
        for sq in range(SEQ_PER_TILE):
            rows = slice(sq * SEQ, (sq + 1) * SEQ)
            kc_ref[sq, 0:NA_WIDTH] = h[rows, NA_WIDTH:2 * NA_WIDTH].T
            vc_ref[sq, 0:NA_WIDTH] = h[rows, 2 * NA_WIDTH:3 * NA_WIDTH].T
            if kc_ref.shape[1] > NA_WIDTH:
                kc_ref[sq, NA_WIDTH:] = jnp.zeros((kc_ref.shape[1] - NA_WIDTH, SEQ), F32)
                vc_ref[sq, NA_WIDTH:] = jnp.zeros((vc_ref.shape[1] - NA_WIDTH, SEQ), F32)


def _pick_group(p_ref, s_ref):
    return jnp.where(pl.program_id(0) < P_TILES, p_ref[...], s_ref[...])


def _p_tile(width):
    return pl.BlockSpec((TM, width), lambda i: (jnp.minimum(i, P_TILES - 1), 0))


def _s_tile(width):
    return pl.BlockSpec((TM, width), lambda i: (jnp.maximum(i - P_TILES, 0), 0))


def _inproj_first_kernel(xp_ref, xs_ref, mod_ref, g_ref, w_ref, qkv_ref, h_ref, kc_ref, vc_ref):
    x = _pick_group(xp_ref, xs_ref)
    hm = _rms_mod(x, g_ref[...], mod_ref[0, 0:1, :], mod_ref[0, 1:2, :])
    _project_in(hm, w_ref, qkv_ref, h_ref, kc_ref, vc_ref)


def _combine_experts(yg_refs, gate_ref):
    gates = gate_ref[...]
    lo_acc = hi_acc = None
    for kk in range(TOP_K):
        lo, hi = _unpack_halves(_load_slabs(yg_refs, kk))
        gk = gates[:, kk:kk + 1]
        lo_acc = gk * lo if lo_acc is None else lo_acc + gk * lo
        hi_acc = gk * hi if hi_acc is None else hi_acc + gk * hi
    return jnp.concatenate([lo_acc, hi_acc], axis=1)


def _inproj_next_kernel(x_ref, yga_ref, ygb_ref, gate_ref, pmod_ref, mod_ref, g_ref, w_ref, kc_in, vc_in,
                        qkv_ref, h_ref, xo_ref, kc_ref, vc_ref):
    del kc_in, vc_in
    x = x_ref[...] + pmod_ref[0, 5:6, :] * _combine_experts((yga_ref, ygb_ref), gate_ref)
    xo_ref[...] = x
    hm = _rms_mod(x, g_ref[...], mod_ref[0, 0:1, :], mod_ref[0, 1:2, :])
    _project_in(hm, w_ref, qkv_ref, h_ref, kc_ref, vc_ref)


_TILE_SPEC = pl.BlockSpec((TM, D_MODEL), lambda i: (i, 0))
_MOD_SPEC = pl.BlockSpec((1, MOD_ROWS, D_MODEL), lambda i: (i, 0, 0))
_ROW_SPEC = pl.BlockSpec((1, D_MODEL), lambda i: (0, 0))
_RT_LANES = 128
_RT_ROWS = 2 * TOP_K
_YG_SPEC = pl.BlockSpec((TOP_K, TM, D_SLAB), lambda i: (0, i, 0))
_GATE_SPEC = pl.BlockSpec((TM, _RT_LANES), lambda i: (i, 0))


def _inproj(layer, x, moe, prev_mod, mod, g, w_bf16, caches):
    w_spec = pl.BlockSpec((1, D_MODEL, IN_COLS), lambda i: (layer, 0, 0))
    h_spec = pl.BlockSpec((TM, REST_COLS), lambda i: (i, 0))
    h_shape = jax.ShapeDtypeStruct((T_ALL, REST_COLS), F32)
    q_spec = pl.BlockSpec((TM, QKV_COLS), lambda i: (i, 0))
    q_shape = jax.ShapeDtypeStruct((T_ALL, QKV_COLS), BF16)
    c_spec = pl.BlockSpec((SEQ_PER_TILE, NA_WIDTH, SEQ), lambda i: (jnp.minimum(i, P_TILES - 1), layer, 0))
    c_shape = jax.ShapeDtypeStruct((BATCH, DEPTH * NA_WIDTH, SEQ), F32)
    x_shape = jax.ShapeDtypeStruct((T_ALL, D_MODEL), F32)
    if moe is None:
        c_all = pl.BlockSpec((SEQ_PER_TILE, DEPTH * NA_WIDTH, SEQ), lambda i: (jnp.minimum(i, P_TILES - 1), 0, 0))
        qkv, h, kc, vc = pl.pallas_call(
            _inproj_first_kernel, out_shape=(q_shape, h_shape, c_shape, c_shape), grid=(N_TILES,),
            in_specs=[_p_tile(D_MODEL), _s_tile(D_MODEL), _MOD_SPEC, _ROW_SPEC, w_spec],
            out_specs=(q_spec, h_spec, c_all, c_all),
            compiler_params=_params(), name="inproj_first",
        )(*x, mod, g, w_bf16)
        return qkv, h, x, (kc, vc)
    qkv, h, x, kc, vc = pl.pallas_call(
        _inproj_next_kernel,
        out_shape=(q_shape, h_shape, x_shape, c_shape, c_shape),
        grid=(N_TILES,),
        in_specs=[_TILE_SPEC, _YG_SPEC, _YG_SPEC, _GATE_SPEC, _MOD_SPEC, _MOD_SPEC, _ROW_SPEC, w_spec,
                  pl.BlockSpec(memory_space=pl.ANY), pl.BlockSpec(memory_space=pl.ANY)],
        out_specs=(q_spec, h_spec, _TILE_SPEC, c_spec, c_spec),
        input_output_aliases={8: 3, 9: 4},
        compiler_params=_params(), name="inproj_next",
    )(x, *moe[0], moe[1], prev_mod, mod, g, w_bf16, *caches)
    return qkv, h, x, (kc, vc)


def _pair_mask(hh):
    lane = lax.broadcasted_iota(jnp.int32, (1, 2 * NA_HEAD_DIM), 1)
    return (lane >= hh * NA_HEAD_DIM) & (lane < (hh + 1) * NA_HEAD_DIM)


_ATT_SEQS = 4


def _stack_pair(qp):
    return jnp.concatenate([jnp.where(_pair_mask(hh), qp, jnp.zeros_like(qp)) for hh in range(2)], axis=0)


def _unstack_pair(o2):
    half = o2.shape[0] // 2
    return jnp.where(_pair_mask(0), o2[:half], o2[half:])


def _attn_prompt_kernel(q_ref, k_ref, v_ref, o_ref):
    scale = NA_HEAD_DIM ** -0.5
    for sq in range(_ATT_SEQS):
        rows = slice(sq * SEQ, (sq + 1) * SEQ)
        for p in range(NA_HEADS // 2):
            cols = slice(p * 128, (p + 1) * 128)
            q2 = _stack_pair(q_ref[rows, cols] * scale)
            s = _dot_nt(q2, k_ref[rows, cols])
            e = jnp.exp(s - jnp.max(s, axis=-1, keepdims=True))
            den = jnp.sum(e, axis=-1, keepdims=True)
            o_ref[rows, cols] = _unstack_pair(_dot(e.astype(BF16), v_ref[rows, cols]) / den).astype(o_ref.dtype)


def _attn_prompt(qkv):
    rows = _ATT_SEQS * SEQ
    return pl.pallas_call(
        _attn_prompt_kernel,
        out_shape=jax.ShapeDtypeStruct((T_PROMPT, NA_WIDTH), BF16),
        grid=(BATCH // _ATT_SEQS,),
        in_specs=[pl.BlockSpec((rows, NA_WIDTH), lambda b: (b, 0)),
                  pl.BlockSpec((rows, NA_WIDTH), lambda b: (b, 1)),
                  pl.BlockSpec((rows, NA_WIDTH), lambda b: (b, 2))],
        out_specs=pl.BlockSpec((rows, NA_WIDTH), lambda b: (b, 0)),
        compiler_params=_params(), name="attn_prompt",
    )(qkv, qkv, qkv)


_NA_ROWS = DEC_SEQ // GRID_W
_NA_LOC = NA_KH * GRID_W
_NA_STEP_ROWS = 4


def _na_window_start(r):
    return jnp.clip(r - NA_KH // 2, 0, _NA_ROWS - NA_KH)


def _attn_sample_kernel(q_ref, k_ref, v_ref, ck_ref, cv_ref, *rest):
    bias_refs, o_ref = rest[:_NA_STEP_ROWS], rest[_NA_STEP_ROWS]
    scale = NA_HEAD_DIM ** -0.5
    for p in range(NA_HEADS // 2):
        cols = slice(p * 128, (p + 1) * 128)
        kc = ck_ref[0, :, cols].astype(BF16)
        vc = cv_ref[0, :, cols].astype(BF16)
        for u in range(_NA_STEP_ROWS):
            rows = slice(u * GRID_W, (u + 1) * GRID_W)
            s0 = pl.multiple_of(_na_window_start(pl.program_id(1) * _NA_STEP_ROWS + u) * GRID_W, GRID_W)
            q2 = _stack_pair(q_ref[rows, cols] * scale)
            bias2 = jnp.concatenate([bias_refs[u][0, 0, 2 * p], bias_refs[u][0, 0, 2 * p + 1]], axis=0)
            sl = _dot_nt(q2, k_ref[pl.ds(s0, _NA_LOC), cols]) + bias2
            sc = _dot_nt(q2, kc)
            mx = jnp.maximum(jnp.max(sl, axis=-1, keepdims=True), jnp.max(sc, axis=-1, keepdims=True))
            el = jnp.exp(sl - mx)
            ec = jnp.exp(sc - mx)
            den = jnp.sum(el, axis=-1, keepdims=True) + jnp.sum(ec, axis=-1, keepdims=True)
            o2 = (_dot(el.astype(BF16), v_ref[pl.ds(s0, _NA_LOC), cols]) + _dot(ec.astype(BF16), vc)) / den
            o_ref[rows, cols] = _unstack_pair(o2).astype(o_ref.dtype)


def _attn_sample(layer, qkv, ck, cv, bias):
    q_rows = _NA_STEP_ROWS * GRID_W
    steps = _NA_ROWS // _NA_STEP_ROWS
    q_blk0 = T_PROMPT // q_rows
    kv_row0 = T_PROMPT // DEC_SEQ

    def bias_spec(u):
        def index(b, r2):
            r = r2 * _NA_STEP_ROWS + u
            return (layer, _na_window_start(r) - r + NA_KH - 1, 0, 0, 0)
        return pl.BlockSpec((1, 1, NA_HEADS, GRID_W, _NA_LOC), index)

    return pl.pallas_call(
        _attn_sample_kernel,
        out_shape=jax.ShapeDtypeStruct((T_SAMPLE, NA_WIDTH), BF16),
        grid=(DEC_BATCH, steps),
        in_specs=[pl.BlockSpec((q_rows, NA_WIDTH), lambda b, r2: (q_blk0 + b * steps + r2, 0)),
                  pl.BlockSpec((DEC_SEQ, NA_WIDTH), lambda b, r2: (kv_row0 + b, 1)),
                  pl.BlockSpec((DEC_SEQ, NA_WIDTH), lambda b, r2: (kv_row0 + b, 2)),
                  pl.BlockSpec((1, PAST_LEN, NA_WIDTH), lambda b, r2: (b, 0, 0)),
                  pl.BlockSpec((1, PAST_LEN, NA_WIDTH), lambda b, r2: (b, 0, 0))]
                 + [bias_spec(u) for u in range(_NA_STEP_ROWS)],
        out_specs=pl.BlockSpec((q_rows, NA_WIDTH), lambda b, r2: (b * steps + r2, 0)),
        compiler_params=_params(2), name="attn_sample",
    )(qkv, qkv, qkv, ck, cv, *([bias] * _NA_STEP_ROWS))


_NA_DR = 2 * NA_KH - 1
_NA_DC = 2 * NA_KW - 1


def _na_bias_kernel(rb_ref, o_ref):
    qc = lax.broadcasted_iota(jnp.int32, (GRID_W, GRID_W), 0)
    kc = lax.broadcasted_iota(jnp.int32, (GRID_W, GRID_W), 1)
    q_start = jnp.clip(qc - NA_KW // 2, 0, GRID_W - NA_KW)
    in_win = (kc >= q_start) & (kc < q_start + NA_KW)
    dc = jnp.clip(kc - qc + NA_KW - 1, 0, _NA_DC - 1)
    picks = [dc == d for d in range(_NA_DC)]

    def one_head(hh, carry):
        i = pl.program_id(0) * NA_HEADS + hh
        tiles = []
        for dr in range(_NA_DR):
            acc = jnp.zeros((GRID_W, GRID_W), F32)
            for d in range(_NA_DC):
                acc = jnp.where(picks[d], rb_ref[i, dr * _NA_DC + d], acc)
            tiles.append(jnp.where(in_win, acc, NEG_INF))
        for base in range(NA_KH):
            o_ref[0, base, hh] = jnp.concatenate(tiles[base:base + NA_KH], axis=1)
        return carry
    lax.fori_loop(0, NA_HEADS, one_head, 0)


def _na_bias_tables(rel_bias):
    rb = rel_bias.astype(F32).reshape(DEPTH * NA_HEADS, _NA_DR * _NA_DC)
    return pl.pallas_call(
        _na_bias_kernel,
        out_shape=jax.ShapeDtypeStruct((DEPTH, NA_KH, NA_HEADS, GRID_W, _NA_LOC), F32),
        grid=(DEPTH,),
        in_specs=[pl.BlockSpec(memory_space=pltpu.SMEM)],
        out_specs=pl.BlockSpec((1, NA_KH, NA_HEADS, GRID_W, _NA_LOC), lambda i: (i, 0, 0, 0, 0)),
        compiler_params=_params(), name="na_bias_tables",
    )(rb)


_HG_GROUP = 8
_HG_UNROLL = 4


def _hgrn_kernel(*refs, n_tok, has_state, n_alias):
    refs = refs[:8 + 2 * has_state] + refs[8 + 2 * has_state + n_alias:]
    if has_state:
        (q_ref, zf_ref, zb_ref, v_ref, g_ref, lbf_ref, lbb_ref, og_ref, s0f_ref, s0b_ref,
         rec_ref, sf_ref, sb_ref, kf_s, bf_s, kb_s, bb_s, of_s, ob_s, zf_s, zb_s, qsf_s, qsb_s, stf_s, stb_s) = refs
    else:
        (q_ref, zf_ref, zb_ref, v_ref, g_ref, lbf_ref, lbb_ref, og_ref,
         rec_ref, sf_ref, sb_ref, kf_s, bf_s, kb_s, bb_s, of_s, ob_s, zf_s, zb_s, qsf_s, qsb_s, stf_s, stb_s) = refs
        s0f_ref = s0b_ref = None
    C = HG_CHUNK
    W = HG_WIDTH
    n_chunks = n_tok // C
    rr = lax.broadcasted_iota(jnp.int32, (W, W), 0)
    cc = lax.broadcasted_iota(jnp.int32, (W, W), 1)
    log2_c = C.bit_length() - 1
    same_chunk = jnp.right_shift(rr, log2_c) == jnp.right_shift(cc, log2_c)
    tri_prefix = jnp.where(same_chunk & (cc <= rr), 1.0, 0.0).astype(BF16)
    tri_suffix = jnp.where(same_chunk & (cc >= rr), 1.0, 0.0).astype(BF16)
    same_head = jnp.right_shift(rr, 6) == jnp.right_shift(cc, 6)
    head_ones = jnp.where(same_head, 1.0, 0.0).astype(BF16)

    for ti in range(n_tok // W):
        rows = slice(ti * W, (ti + 1) * W)
        for z_ref, lb_ref, k_s, b_s, tri in ((zf_ref, lbf_ref, kf_s, bf_s, tri_prefix),
                                             (zb_ref, lbb_ref, kb_s, bb_s, tri_suffix)):
            z = z_ref[rows, :]
            lb = lb_ref[...]
            e = jnp.exp(-jnp.abs(z))
            big = 1.0 / (1.0 + e)
            small = e * big
            f = lb + (1.0 - lb) * jnp.where(z >= 0.0, big, small)
            logf = jnp.log(jnp.maximum(f, F_FLOOR))
            k_s[rows, :] = (1.0 - lb) * jnp.where(z >= 0.0, small, big)
            hi, mid, lo = _split3(logf)
            b_s[rows, :] = _dot(tri, hi) + _dot(tri, mid) + _dot(tri, lo)

    G = _HG_GROUP
    n_groups = C // G
    srow = lax.broadcasted_iota(jnp.int32, (G, W), 0)
    zf_s[...] = jnp.zeros_like(zf_s)
    zb_s[...] = jnp.zeros_like(zb_s)

    def scan_chunk(ci, k_s, b_s, z_s, ks_s, st_s, o_dir_s, fwd):
        c = ci if fwd else n_chunks - 1 - ci
        base = pl.multiple_of(c * C, C)
        q = q_ref[pl.ds(base, C), :]
        k = k_s[pl.ds(base, C), :]
        b = b_s[pl.ds(base, C), :]
        v = v_ref[pl.ds(base, C), :]
        q_far = {}
        for gs in range(n_groups):
            others = range(gs + 1, n_groups) if fwd else range(gs)
            if not others:
                continue
            rows_s = slice(gs * G, (gs + 1) * G)
            edge = (gs + 1) * G - 1 if fwd else gs * G
            b_edge = b[edge:edge + 1, :]
            ks_s[rows_s, :] = k[rows_s] * jnp.exp(b_edge - b[rows_s])
            for gt in others:
                rows_t = slice(gt * G, (gt + 1) * G)
                q_far[gs, gt] = q[rows_t] * jnp.exp(b[rows_t] - b_edge)
        for sx in range(C):
            gs = sx // G
            rows_g = slice(gs * G, (gs + 1) * G)
            k_row = k_s[pl.ds(base + sx, 1), :]
            b_row = b_s[pl.ds(base + sx, 1), :]
            keep = (srow + gs * G >= sx) if fwd else (srow + gs * G <= sx)
            z_s[sx * C + gs * G:sx * C + (gs + 1) * G, :] = jnp.where(
                keep, (k_row * q[rows_g]) * jnp.exp(b[rows_g] - b_row), 0.0)
            others = range(gs + 1, n_groups) if fwd else range(gs)
            if others:
                ks_row = ks_s[sx:sx + 1, :]
                for gt in others:
                    z_s[sx * C + gt * G:sx * C + (gt + 1) * G, :] = ks_row * q_far[gs, gt]
        a_rep = _dot(z_s[...].astype(BF16), head_ones)
        o_intra = jnp.sum(a_rep.reshape(C, C, W) * v[:, None, :], axis=0)
        b_end = b_s[pl.ds(base + (C - 1 if fwd else 0), 1), :]
        q_in = q * jnp.exp(b)
        k_st = k * jnp.exp(b_end - b)
        st = st_s[...]
        o_inter = _dot_nt(q_in.astype(BF16), st.astype(BF16))
        upd = _dot_tn(v.astype(BF16), k_st.astype(BF16))
        st_s[...] = st * jnp.exp(b_end) + jnp.where(same_head, upd, 0.0)
        o_dir_s[pl.ds(base, C), :] = o_intra + o_inter

    def load_state(s0_ref, st_s):
        if s0_ref is None:
            st_s[...] = jnp.zeros((W, W), F32)
            return
        for hh in range(HG_HEADS):
            parts = [s0_ref[0, hh] if g == hh else jnp.zeros((HG_DK, HG_DV), F32) for g in range(HG_HEADS)]
            st_s[hh * HG_DK:(hh + 1) * HG_DK, :] = jnp.concatenate(parts, axis=1)
        st_s[...] = st_s[...].T

    def store_state(st_s, out_ref):
        by_head = st_s[...].T
        for hh in range(HG_HEADS):
            out_ref[0, hh] = by_head[hh * HG_DK:(hh + 1) * HG_DK, hh * HG_DV:(hh + 1) * HG_DV]
        if out_ref.shape[1] > HG_HEADS:
            out_ref[0, HG_HEADS:] = jnp.zeros((out_ref.shape[1] - HG_HEADS, HG_DK, HG_DV), F32)

    load_state(s0f_ref, stf_s)
    load_state(s0b_ref, stb_s)

    def scan_both(ci, carry):
        for u in range(_HG_UNROLL):
            scan_chunk(ci * _HG_UNROLL + u, kf_s, bf_s, zf_s.at[u], qsf_s.at[u], stf_s, of_s, True)
            scan_chunk(ci * _HG_UNROLL + u, kb_s, bb_s, zb_s.at[u], qsb_s.at[u], stb_s, ob_s, False)
        return carry
    lax.fori_loop(0, n_chunks // _HG_UNROLL, scan_both, 0)
    store_state(stf_s, sf_ref)
    store_state(stb_s, sb_ref)

    for ti in range(n_tok // W):
        rows = slice(ti * W, (ti + 1) * W)
        o = of_s[rows, :] + ob_s[rows, :]
        sq = o * o
        sq_hi = sq.astype(BF16)
        sq_lo = (sq - sq_hi.astype(F32)).astype(BF16)
        ms = (_dot(sq_hi, head_ones) + _dot(sq_lo, head_ones)) * (1.0 / HG_DV)
        g = g_ref[rows, :]
        y = o * lax.rsqrt(ms + RMS_EPS) * og_ref[...] * (g * jax.nn.sigmoid(g))
        rec_ref[rows, :] = y.astype(rec_ref.dtype)


def _hgrn(h, lbf, lbb, og, s0f, s0b, n_tok, n_seq, row0, layer=None, states=None):
    W = HG_WIDTH
    has_state = s0f is not None

    def col(cb):
        return pl.BlockSpec((n_tok, W), lambda i, cb=cb: (row0 + i, cb))

    vec = pl.BlockSpec((1, W), lambda i: (0, 0))
    st_spec = pl.BlockSpec((1, HG_HEADS, HG_DK, HG_DV), lambda i: (i, 0, 0, 0))
    in_specs = [col(_CB_HQ), col(_CB_ZF), col(_CB_ZB), col(_CB_HI), col(_CB_HG), vec, vec, vec]
    args = [h, h, h, h, h, lbf, lbb, og]
    if has_state:
        in_specs += [st_spec, st_spec]
        args += [s0f, s0b]
    seq_f32 = pltpu.VMEM((n_tok, W), F32)
    out_st_spec, st_rows, aliases = st_spec, HG_HEADS, {}
    if layer is not None:
        st_rows = DEPTH * HG_HEADS
        if states is None:
            out_st_spec = pl.BlockSpec((1, st_rows, HG_DK, HG_DV), lambda i: (i, 0, 0, 0))
        else:
            out_st_spec = pl.BlockSpec((1, HG_HEADS, HG_DK, HG_DV), lambda i: (i, layer, 0, 0))
            aliases = {len(args): 1, len(args) + 1: 2}
            in_specs += [pl.BlockSpec(memory_space=pl.ANY)] * 2
            args += list(states)
    return pl.pallas_call(
        functools.partial(_hgrn_kernel, n_tok=n_tok, has_state=has_state, n_alias=len(aliases)),
        out_shape=(jax.ShapeDtypeStruct((n_seq * n_tok, W), BF16),
                   jax.ShapeDtypeStruct((n_seq, st_rows, HG_DK, HG_DV), F32),
                   jax.ShapeDtypeStruct((n_seq, st_rows, HG_DK, HG_DV), F32)),
        grid=(n_seq,),
        in_specs=in_specs,
        input_output_aliases=aliases,
        out_specs=(pl.BlockSpec((n_tok, W), lambda i: (i, 0)), out_st_spec, out_st_spec),
        scratch_shapes=[seq_f32, seq_f32, seq_f32, seq_f32, seq_f32, seq_f32,
                        pltpu.VMEM((_HG_UNROLL, HG_CHUNK * HG_CHUNK, W), F32),
                        pltpu.VMEM((_HG_UNROLL, HG_CHUNK * HG_CHUNK, W), F32),
                        pltpu.VMEM((_HG_UNROLL, HG_CHUNK, W), F32),
                        pltpu.VMEM((_HG_UNROLL, HG_CHUNK, W), F32),
                        pltpu.VMEM((W, W), F32),
                        pltpu.VMEM((W, W), F32)],
        compiler_params=_params(), name="hgrn_state" if has_state else "hgrn_zero",
    )(*args)


def _gmlp_tile(u_ref, v_ref, g_ref, ws_ref, b_ref):
    lane = lax.broadcasted_iota(jnp.int32, (1, GM_WIDTH), 1)
    outs = []
    for ci in range(TM // GM_CHUNK):
        rows = slice(ci * GM_CHUNK, (ci + 1) * GM_CHUNK)
        v = v_ref[rows, :]
        ms = jnp.mean(v * v, axis=-1, keepdims=True)
        vn = (v * lax.rsqrt(ms + RMS_EPS) * g_ref[...]).astype(BF16)
        z = b_ref[...]
        for gi in range(GM_GROUPS):
            zg = _dot(ws_ref[gi], vn)
            in_group = (lane >= gi * GM_GDIM) & (lane < (gi + 1) * GM_GDIM)
            z = z + jnp.where(in_group, zg, 0.0)
        outs.append((u_ref[rows, :] * z).astype(BF16))
    return jnp.concatenate(outs, axis=0)


def _outproj_kernel(attp_ref, atts_ref, recp_ref, recs_ref, gu_ref, gv_ref, gg_ref, gws_ref, gb_ref,
                    xp_ref, xs_ref, mod_ref, g_ref, w_ref, wr_ref, br_ref,
                    x1_ref, h2a_ref, h2b_ref, rt_ref, gate_ref, cnt_ref):
    @pl.when(pl.program_id(0) == 0)
    def _():
        cnt_ref[...] = jnp.zeros_like(cnt_ref)

    out = (_dot(_pick_group(attp_ref, atts_ref), w_ref[0, 0:NA_WIDTH, :])
           + _dot(_pick_group(recp_ref, recs_ref), w_ref[0, NA_WIDTH:NA_WIDTH + HG_WIDTH, :])
           + _dot(_gmlp_tile(gu_ref, gv_ref, gg_ref, gws_ref, gb_ref), w_ref[0, NA_WIDTH + HG_WIDTH:, :]))
    x1 = _pick_group(xp_ref, xs_ref) + mod_ref[0, 2:3, :] * out
    x1_ref[...] = x1
    h2 = _rms_mod(x1, g_ref[...], mod_ref[0, 3:4, :], mod_ref[0, 4:5, :])
    _store_slabs((h2a_ref, h2b_ref), _pack_halves(h2))
    h_hi = h2.astype(BF16)
    h_lo = (h2 - h_hi.astype(F32)).astype(BF16)
    wr = wr_ref[...]
    w_hi = wr.astype(BF16)
    w_lo = (wr - w_hi.astype(F32)).astype(BF16)
    logits = _dot(h_hi, w_hi) + _dot(h_lo, w_hi) + _dot(h_hi, w_lo) + br_ref[...]
    lane_e = lax.broadcasted_iota(jnp.int32, (TM, N_EXPERTS), 1).astype(F32)
    lane_o = lax.broadcasted_iota(jnp.int32, (TM, _RT_LANES), 1)
    idx_acc = jnp.zeros((TM, _RT_LANES), F32)
    val_acc = jnp.zeros((TM, _RT_LANES), F32)
    top0 = None
    den = jnp.zeros((TM, 1), F32)
    work = logits
    picks = []
    for kk in range(TOP_K):
        m = jnp.max(work, axis=-1, keepdims=True)
        first = jnp.min(jnp.where(work == m, lane_e, float(N_EXPERTS)), axis=-1, keepdims=True)
        if kk == 0:
            top0 = m
        e = jnp.exp(m - top0)
        den = den + e
        idx_acc = jnp.where(lane_o == kk, first, idx_acc)
        val_acc = jnp.where(lane_o == kk, e, val_acc)
        picks.append(lane_e == first)
        work = jnp.where(picks[-1], -jnp.inf, work)
    gate_ref[...] = val_acc / den
    sel = jnp.zeros((TM, N_EXPERTS), F32)
    for pk in picks:
        sel = sel + jnp.where(pk, 1.0, 0.0)
    rr = lax.broadcasted_iota(jnp.int32, (TM, TM), 0)
    cc = lax.broadcasted_iota(jnp.int32, (TM, TM), 1)
    earlier = jnp.where(cc < rr, 1.0, 0.0).astype(BF16)
    seen = cnt_ref[0:1, 0:N_EXPERTS]
    before = _dot(earlier, sel.astype(BF16)) + seen
    for kk, pk in enumerate(picks):
        rank = jnp.sum(jnp.where(pk, before, 0.0), axis=-1, keepdims=True)
        idx_acc = jnp.where(lane_o == TOP_K + kk, rank, idx_acc)
    rt_ref[...] = idx_acc.T[0:_RT_ROWS, :].astype(jnp.int32)
    cnt_ref[0:1, 0:N_EXPERTS] = seen + jnp.sum(sel, axis=0, keepdims=True)


def _outproj(layer, att_p, att_s, rec_p, rec_s, h, gmlp_params, x, mod, g, w_bf16, wr, br):
    def tile(width):
        return pl.BlockSpec((TM, width), lambda i: (i, 0))

    if isinstance(x, tuple):
        x_args, x_specs = x, [_p_tile(D_MODEL), _s_tile(D_MODEL)]
    else:
        x_args = (x, x)
        x_specs = [_p_tile(D_MODEL), pl.BlockSpec((TM, D_MODEL), lambda i: (jnp.maximum(i, P_TILES), 0))]
    return pl.pallas_call(
        _outproj_kernel,
        out_shape=(jax.ShapeDtypeStruct((T_ALL, D_MODEL), F32),
                   jax.ShapeDtypeStruct((T_ALL, D_SLAB), jnp.int32),
                   jax.ShapeDtypeStruct((T_ALL, D_SLAB), jnp.int32),
                   jax.ShapeDtypeStruct((_RT_ROWS, T_ALL), jnp.int32),
                   jax.ShapeDtypeStruct((T_ALL, _RT_LANES), F32),
                   jax.ShapeDtypeStruct((8, _RT_LANES), F32)),
        grid=(N_TILES,),
        in_specs=[_p_tile(NA_WIDTH), _s_tile(NA_WIDTH), _p_tile(HG_WIDTH), _s_tile(HG_WIDTH),
                  pl.BlockSpec((TM, GM_WIDTH), lambda i: (i, _CB_GU)), pl.BlockSpec((TM, GM_WIDTH), lambda i: (i, _CB_GV)),
                  pl.BlockSpec((1, GM_WIDTH), lambda i: (0, 0)),
                  pl.BlockSpec((GM_GROUPS, GM_CHUNK, GM_CHUNK), lambda i: (0, 0, 0)),
                  pl.BlockSpec((GM_CHUNK, GM_WIDTH), lambda i: (0, 0)),
                  *x_specs, _MOD_SPEC, _ROW_SPEC,
                  pl.BlockSpec((1, D_MODEL, D_MODEL), lambda i: (layer, 0, 0)),
                  pl.BlockSpec((D_MODEL, N_EXPERTS), lambda i: (0, 0)),
                  pl.BlockSpec((1, N_EXPERTS), lambda i: (0, 0))],
        out_specs=(_TILE_SPEC, tile(D_SLAB), tile(D_SLAB), pl.BlockSpec((_RT_ROWS, TM), lambda i: (0, i)),
                   tile(_RT_LANES),
                   pl.BlockSpec((8, _RT_LANES), lambda i: (0, 0))),
        compiler_params=_params(), name="outproj_router",
    )(att_p, att_s, rec_p, rec_s, h, h, *gmlp_params, *x_args, mod, g, w_bf16, wr, br)


_W_CHUNKS = 4
_W_CAST_ROWS = 128
_W_DMA_PRIORITY = 1


def _moe_kernel(blk_e_ref, blk_on_ref, blk_new_ref, blk_next_ref,
                xa_ref, xb_ref, wg_hbm, bg_ref, wu_hbm, bu_ref, wd_hbm, bd_ref,
                ya_ref, yb_ref, w_f32, w_bf16, w_sem, *, layer):
    j = pl.program_id(0)

    def weight_copies(expert):
        rows = D_MODEL // _W_CHUNKS
        return [pltpu.make_async_copy(w_hbm.at[layer, expert, pl.ds(ci * rows, rows)],
                                      w_f32.at[wi, pl.ds(ci * rows, rows)], w_sem.at[wi, ci])
                for wi, w_hbm in enumerate((wg_hbm, wu_hbm, wd_hbm)) for ci in range(_W_CHUNKS)]

    @pl.when(j == 0)
    def _():
        for cp in weight_copies(blk_e_ref[0]):
            cp.start(priority=_W_DMA_PRIORITY)

    @pl.when(blk_new_ref[j] != 0)
    def _():
        for cp in weight_copies(blk_e_ref[j]):
            cp.wait()

        def cast_rows(ci, carry):
            rows = pl.ds(pl.multiple_of(ci * _W_CAST_ROWS, _W_CAST_ROWS), _W_CAST_ROWS)
            for wi in range(3):
                w_bf16[wi, rows, :] = w_f32[wi, rows, :].astype(BF16)
            return carry
        lax.fori_loop(0, D_MODEL // _W_CAST_ROWS, cast_rows, 0)

        @pl.when(blk_next_ref[j] >= 0)
        def _():
            for cp in weight_copies(blk_next_ref[j]):
                cp.start(priority=_W_DMA_PRIORITY)

    @pl.when(blk_on_ref[j] != 0)
    def _():
        lo, hi = _unpack_halves(_load_slabs((xa_ref, xb_ref)))
        x = jnp.concatenate([lo.astype(BF16), hi.astype(BF16)], axis=1)
        gate = jnp.minimum(_dot(x, w_bf16[0]) + bg_ref[0, 0], SWIGLU_LIMIT)
        up = jnp.clip(_dot(x, w_bf16[1]) + bu_ref[0, 0], -SWIGLU_LIMIT, SWIGLU_LIMIT)
        glu = gate * jax.nn.sigmoid(SWIGLU_ALPHA * gate)
        act = ((up + 1.0) * glu).astype(BF16)
        _store_slabs((ya_ref, yb_ref), _pack_halves(_dot(act, w_bf16[2]) + bd_ref[0, 0]))

    @pl.when(blk_on_ref[j] == 0)
    def _():
        ya_ref[...] = jnp.zeros_like(ya_ref)
        yb_ref[...] = jnp.zeros_like(yb_ref)


def _moe(layer, plan, x_sorted, wg, bg, wu, bu, wd, bd):
    n_plan = len(plan)
    b_spec = pl.BlockSpec((1, 1, 1, D_MODEL), lambda j, be, *_: (layer, be[j], 0, 0))
    x_spec = pl.BlockSpec((MOE_BM, D_SLAB), lambda j, *_: (j, 0))
    hbm = pl.BlockSpec(memory_space=pl.ANY)
    bias4 = lambda b: b.reshape(DEPTH, N_EXPERTS, 1, D_MODEL)
    return pl.pallas_call(
        functools.partial(_moe_kernel, layer=layer),
        out_shape=(jax.ShapeDtypeStruct((MOE_SLOTS, D_SLAB), jnp.int32),) * N_SPLIT,
        grid_spec=pltpu.PrefetchScalarGridSpec(
            num_scalar_prefetch=n_plan, grid=(MOE_BLOCKS,),
            in_specs=[x_spec, x_spec, hbm, b_spec, hbm, b_spec, hbm, b_spec],
            out_specs=(x_spec, x_spec),
            scratch_shapes=[pltpu.VMEM((3, D_MODEL, D_MODEL), F32), pltpu.VMEM((3, D_MODEL, D_MODEL), BF16),
                            pltpu.SemaphoreType.DMA((3, _W_CHUNKS))]),
        compiler_params=_params(), name="moe_experts",
    )(*plan, *x_sorted, wg, bias4(bg), wu, bias4(bu), wd, bias4(bd))


def _route(rt, counts):
    experts = jnp.arange(N_EXPERTS, dtype=jnp.int32)
    nblk = (counts + MOE_BM - 1) // MOE_BM
    blk_end = jnp.cumsum(nblk)
    row0 = (blk_end - nblk) * MOE_BM
    top_i, rank = rt[:TOP_K], rt[TOP_K:]
    start_of = jnp.sum(jnp.where(top_i[None] == experts[:, None, None], row0[:, None, None], 0), axis=0)
    dest = (start_of + rank).reshape(1, TOP_K * T_ALL)
    live = counts > 0
    last_live = jnp.max(jnp.where(live, experts, 0))
    later_live = live[None, :] & (experts[None, :] > experts[:, None])
    next_live = jnp.min(jnp.where(later_live, experts[None, :], N_EXPERTS), axis=1)
    next_live = jnp.where(next_live == N_EXPERTS, -1, next_live)
    blk = jnp.arange(MOE_BLOCKS, dtype=jnp.int32)
    blk_on = blk < blk_end[-1]
    blk_e = jnp.where(blk_on, jnp.minimum(jnp.sum((blk_end[None, :] <= blk[:, None]).astype(jnp.int32), axis=1),
                                          N_EXPERTS - 1), last_live)
    blk_new = blk_on & jnp.concatenate([jnp.ones((1,), bool), blk_e[1:] != blk_e[:-1]])
    is_e = blk_e[:, None] == experts[None, :]
    lookup = lambda table: jnp.sum(jnp.where(is_e, table[None, :], 0), axis=1)
    plan = (blk_e, blk_on, blk_new, lookup(next_live))
    return dest.astype(jnp.int32), tuple(p.astype(jnp.int32) for p in plan)


_SC_WINDOW = 128


def _sc_mesh():
    return plsc.VectorSubcoreMesh(core_axis_name="core", subcore_axis_name="subcore")


def _sc_scatter_rows(srcs, idx, n_out):
    n_src, width = srcs[0].shape
    n_rep = idx.shape[1] // n_src
    src_windows = n_src // _SC_WINDOW
    assert len(srcs) == 2

    def body(*refs):
        x_hbm = refs[:len(srcs)]
        i_hbm = refs[len(srcs)]
        o_hbm = refs[len(srcs) + 1:]

        def run(xs, os_):
            def step(x_vmem, *i_vmem):
                for iv in i_vmem:
                    pltpu.sync_copy(x_vmem, os_.at[iv.at[0]])

            pltpu.emit_pipeline(
                step, grid=(src_windows,),
                in_specs=[pl.BlockSpec((_SC_WINDOW, width), lambda i: (i, 0))]
                         + [pl.BlockSpec((1, _SC_WINDOW), lambda i, kk=kk: (0, kk * src_windows + i))
                            for kk in range(n_rep)],
                out_specs=[],
                core_axis_name="subcore",
                dimension_semantics=(pltpu.PARALLEL,),
            )(xs, *([i_hbm] * n_rep))

        for ci, (xs, os_) in enumerate(zip(x_hbm, o_hbm)):
            pl.when(lax.axis_index("core") == ci)(functools.partial(run, xs, os_))

    out_type = tuple(jax.ShapeDtypeStruct((n_out, width), s.dtype) for s in srcs)
    return pl.kernel(body, out_type=out_type, mesh=_sc_mesh(), scratch_types=[],
                     name="sc_scatter_rows")(*srcs, idx)


def _sc_gather_rows(tables, idx):
    n_idx = idx.shape[1]
    width = tables[0].shape[1]

    def body(*refs):
        t_hbm = refs[:len(tables)]
        i_hbm = refs[len(tables)]
        o_hbm = refs[len(tables) + 1:]
        for ts, os_ in zip(t_hbm, o_hbm):
            def step(i_vmem, o_vmem, ts=ts):
                pltpu.sync_copy(ts.at[i_vmem.at[0]], o_vmem)

            pltpu.emit_pipeline(
                step, grid=(n_idx // _SC_WINDOW,),
                in_specs=[pl.BlockSpec((1, _SC_WINDOW), lambda i: (0, i))],
                out_specs=[pl.BlockSpec((_SC_WINDOW, width), lambda i: (i, 0))],
                core_axis_name=("core", "subcore"),
                dimension_semantics=(pltpu.PARALLEL,),
            )(i_hbm, os_)

    out_type = tuple(jax.ShapeDtypeStruct((n_idx, width), t.dtype) for t in tables)
    return pl.kernel(body, out_type=out_type, mesh=_sc_mesh(), scratch_types=[],
                     name="sc_gather_rows")(*tables, idx)


def _final_kernel(x_hbm, yga_hbm, ygb_hbm, gate_hbm, mod_hbm, g_ref, yp_hbm, ys_hbm):
    def tile_body(x_ref, yga_ref, ygb_ref, gate_ref, mod_ref, o_ref):
        x = x_ref[...] + mod_ref[0, 5:6, :] * _combine_experts((yga_ref, ygb_ref), gate_ref)
        ms = jnp.mean(x * x, axis=-1, keepdims=True)
        o_ref[...] = x * lax.rsqrt(ms + RMS_EPS) * g_ref[...]

    for tile0, n_tiles, out_hbm in ((0, P_TILES, yp_hbm), (P_TILES, N_TILES - P_TILES, ys_hbm)):
        pltpu.emit_pipeline(
            tile_body, grid=(n_tiles,),
            in_specs=[pl.BlockSpec((TM, D_MODEL), lambda i, t0=tile0: (t0 + i, 0)),
                      pl.BlockSpec((TOP_K, TM, D_SLAB), lambda i, t0=tile0: (0, t0 + i, 0)),
                      pl.BlockSpec((TOP_K, TM, D_SLAB), lambda i, t0=tile0: (0, t0 + i, 0)),
                      pl.BlockSpec((TM, _RT_LANES), lambda i, t0=tile0: (t0 + i, 0)),
                      pl.BlockSpec((1, MOD_ROWS, D_MODEL), lambda i, t0=tile0: (t0 + i, 0, 0))],
            out_specs=[pl.BlockSpec((TM, D_MODEL), lambda i: (i, 0))],
        )(x_hbm, yga_hbm, ygb_hbm, gate_hbm, mod_hbm, out_hbm)


def _final(x, moe, mod, g):
    hbm = pl.BlockSpec(memory_space=pl.ANY)
    return pl.pallas_call(
        _final_kernel,
        out_shape=(jax.ShapeDtypeStruct((T_PROMPT, D_MODEL), F32), jax.ShapeDtypeStruct((T_SAMPLE, D_MODEL), F32)),
        in_specs=[hbm, hbm, hbm, hbm, hbm, pl.BlockSpec(memory_space=pltpu.VMEM)],
        out_specs=(hbm, hbm),
        compiler_params=pltpu.CompilerParams(vmem_limit_bytes=V7X_VMEM_LIMIT), name="final_norm",
    )(x, *moe[0], moe[1], mod, g)


def kernel(x_prompt, x_sample, cache_k, cache_v, state_hgrn_fwd, state_hgrn_bwd, c, c_ctx, w_mod, b_mod, norm1_g, norm2_g, w_in, na_rel_bias, hgrn_lb, hgrn_onorm_g, gmlp_vnorm_g, gmlp_ws, gmlp_b, w_out, router_w, router_b, w_gate, b_gate, w_up, b_up, w_down, b_down, final_g):
    x = (x_prompt.reshape(T_PROMPT, D_MODEL), x_sample.reshape(T_SAMPLE, D_MODEL))

    cond = jnp.zeros((MOD_ROWS, D_MODEL), F32).at[0].set(c_ctx).at[1:1 + DEC_BATCH].set(c)
    mod = _modulation(cond, w_mod, b_mod)
    tile_row = np.concatenate([np.zeros(P_TILES, np.int32),
                               1 + np.arange(N_TILES - P_TILES, dtype=np.int32) // (DEC_SEQ // TM)])
    mod_tiles = mod[:, tile_row].reshape(DEPTH, N_TILES, 6, D_MODEL)
    mod_tiles = jnp.pad(mod_tiles, ((0, 0), (0, 0), (0, MOD_ROWS - 6), (0, 0)))

    lb_soft = jax.nn.softmax(hgrn_lb.astype(F32), axis=1)
    lower = jnp.cumsum(lb_soft, axis=1) - lb_soft[:, :1]

    na_bias = _na_bias_tables(na_rel_bias)
    w_in_bf16 = w_in.astype(BF16)
    w_out_bf16 = w_out.astype(BF16)

    moe_out = caches = states = None
    for l in range(DEPTH):
        qkv, h, x_next, caches = _inproj(l, x, moe_out, mod_tiles[l - 1] if l else None, mod_tiles[l],
                                    norm1_g[l][None, :], w_in_bf16, caches)

        att_p = _attn_prompt(qkv)
        att_s = _attn_sample(l, qkv, cache_k[:, l].reshape(DEC_BATCH, PAST_LEN, NA_WIDTH),
                             cache_v[:, l].reshape(DEC_BATCH, PAST_LEN, NA_WIDTH), na_bias)
        lbf = lower[0, l][None, :]
        lbb = lower[1, l][None, :]
        og = jnp.tile(hgrn_onorm_g[l], HG_HEADS)[None, :]
        rec_p, *states = _hgrn(h, lbf, lbb, og, None, None, SEQ, BATCH, 0, layer=l, states=states)
        rec_s, _, _ = _hgrn(h, lbf, lbb, og, state_hgrn_fwd[:, l].astype(F32), state_hgrn_bwd[:, l].astype(F32),
                            DEC_SEQ, DEC_BATCH, T_PROMPT // DEC_SEQ)
        gm_bias = jnp.repeat(gmlp_b[l].T, GM_GDIM, axis=1)
        gmlp_params = (gmlp_vnorm_g[l][None, :], gmlp_ws[l].astype(BF16), gm_bias)

        x, h2a, h2b, rt, gate_pad, cnt = _outproj(l, att_p, att_s, rec_p, rec_s, h, gmlp_params, x_next,
                                                  mod_tiles[l],
                                                  norm2_g[l][None, :], w_out_bf16,
                                                  router_w[l], router_b[l][None, :])
        dest_flat, plan = _route(rt, cnt[0, :N_EXPERTS].astype(jnp.int32))
        x_sorted = _sc_scatter_rows((h2a, h2b), dest_flat, MOE_SLOTS)
        y_sorted = _moe(l, plan, x_sorted, w_gate, b_gate, w_up, b_up, w_down, b_down)
        y_tok = _sc_gather_rows(y_sorted, dest_flat)
        moe_out = ([yt.reshape(TOP_K, T_ALL, D_SLAB) for yt in y_tok], gate_pad)

    y_prompt, y_sample = _final(x, moe_out, mod_tiles[DEPTH - 1], final_g[None, :])
    y_prompt = y_prompt.reshape(BATCH, SEQ, D_MODEL)
    y_sample = y_sample.reshape(DEC_BATCH, DEC_SEQ, D_MODEL)
    new_k, new_v = (cache.reshape(BATCH, DEPTH, NA_HEADS, NA_HEAD_DIM, SEQ).transpose(0, 1, 4, 2, 3)
                    for cache in caches)
    new_sf, new_sb = (st.reshape(BATCH, DEPTH, HG_HEADS, HG_DK, HG_DV) for st in states)
    return (y_prompt, y_sample, new_k, new_v, new_sf, new_sb)
```
